```python
import math
import jax
import jax.numpy as jnp
from jax import lax
import numpy as np

D_MODEL = 1024
BATCH = 16
SEQ = 4096
DEPTH = 2

N_MIXERS = 2
N_CONV_LAYERS = (DEPTH + 1) // 2
N_GDN_LAYERS = DEPTH // 2
D_FF = 4 * D_MODEL
CONV_WIDTH = 31
GDN_HEADS = 8
GDN_HEAD_K = D_MODEL // GDN_HEADS
GDN_HEAD_V = D_MODEL // GDN_HEADS
GDN_KEY_DIM = GDN_HEADS * GDN_HEAD_K
GDN_VAL_DIM = GDN_HEADS * GDN_HEAD_V
GDN_QKV_DIM = 2 * GDN_KEY_DIM + GDN_VAL_DIM
GDN_IN_DIM = GDN_QKV_DIM + GDN_VAL_DIM + 2 * GDN_HEADS
SHORT_CONV_WIDTH = 4
CHUNK = 64
NORM_EPS = 1e-6

kernel_name = 'hybrid_conformer_conv_gated_deltanet_trunk'


def rms_norm(x, g, eps=NORM_EPS):
    xf = x.astype(jnp.float32)
    y = xf * lax.rsqrt(jnp.mean(xf * xf, axis=-1, keepdims=True) + eps)
    return (y * g.astype(jnp.float32)).astype(x.dtype)


def layer_norm(x, g, b, eps=NORM_EPS):
    xf = x.astype(jnp.float32)
    mu = jnp.mean(xf, axis=-1, keepdims=True)
    xc = xf - mu
    y = xc * lax.rsqrt(jnp.mean(xc * xc, axis=-1, keepdims=True) + eps)
    return (y * g.astype(jnp.float32) + b.astype(jnp.float32)).astype(x.dtype)


def l2norm(x, eps=1e-6):
    xf = x.astype(jnp.float32)
    return xf * lax.rsqrt(jnp.sum(xf * xf, axis=-1, keepdims=True) + eps)


def causal_depthwise_conv(x, w):
    K, C = w.shape
    return lax.conv_general_dilated(
        x, w[:, None, :].astype(x.dtype), window_strides=(1,),
        padding=[(K - 1, 0)], dimension_numbers=('NWC', 'WIO', 'NWC'),
        feature_group_count=C)


def conformer_conv(h, w_pw1, b_pw1, w_dw, b_dw, ln_g, ln_b, w_pw2, b_pw2):
    u = h @ w_pw1 + b_pw1
    u = jax.nn.glu(u, axis=-1)
    u = causal_depthwise_conv(u, w_dw) + b_dw
    u = jax.nn.silu(layer_norm(u, ln_g, ln_b))
    return u @ w_pw2 + b_pw2


def chunk_gated_delta_rule(q, k, v, g, beta):
    B, S, H, dk = q.shape
    dv = v.shape[-1]
    N = S // CHUNK
    q, k, v = [jnp.swapaxes(t, 1, 2).reshape(B, H, N, CHUNK, -1) for t in (q, k, v)]
    g, beta = [jnp.swapaxes(t, 1, 2).reshape(B, H, N, CHUNK) for t in (g, beta)]
    g = jnp.cumsum(g, axis=-1)
    idx = jnp.arange(CHUNK)
    causal = idx[:, None] >= idx[None, :]
    strict = idx[:, None] > idx[None, :]
    decay = jnp.exp(jnp.where(causal, g[..., :, None] - g[..., None, :], -jnp.inf))
    k_beta = k * beta[..., None]
    kk = jnp.einsum('bhnid,bhnjd->bhnij', k_beta, k) * decay
    m = jnp.where(strict, kk, 0.0) + jnp.eye(CHUNK, dtype=jnp.float32)
    rhs = jnp.concatenate([v * beta[..., None], k_beta * jnp.exp(g)[..., None]], axis=-1)
    sol = lax.linalg.triangular_solve(m, rhs, left_side=True, lower=True, unit_diagonal=True)
    u = sol[..., :dv]
    w = sol[..., dv:]
    qk = jnp.einsum('bhnid,bhnjd->bhnij', q, k) * decay

    def step(state, xs):
        q_c, k_c, u_c, w_c, qk_c, g_c = xs
        v_new = u_c - jnp.einsum('bhcd,bhde->bhce', w_c, state)
        o = (jnp.einsum('bhcd,bhde->bhce', q_c * jnp.exp(g_c)[..., None], state)
             + jnp.einsum('bhij,bhje->bhie', qk_c, v_new))
        g_last = g_c[..., -1]
        state = (state * jnp.exp(g_last)[..., None, None]
                 + jnp.einsum('bhcd,bhce->bhde',
                              k_c * jnp.exp(g_last[..., None] - g_c)[..., None], v_new))
        return state, o

    xs = tuple(jnp.moveaxis(t, 2, 0) for t in (q, k, u, w, qk, g))
    state0 = jnp.zeros((B, H, dk, dv), jnp.float32)
    _, o = lax.scan(step, state0, xs)
    o = jnp.moveaxis(o, 0, 2).reshape(B, H, S, dv)
    return jnp.swapaxes(o, 1, 2)


def gated_deltanet(h, w_in, conv_w, a_log, dt_bias, norm_g, w_out):
    B, S, _ = h.shape
    proj = h @ w_in
    qkv = proj[..., :GDN_QKV_DIM]
    z = proj[..., GDN_QKV_DIM:GDN_QKV_DIM + GDN_VAL_DIM]
    a_raw = proj[..., GDN_QKV_DIM + GDN_VAL_DIM:GDN_QKV_DIM + GDN_VAL_DIM + GDN_HEADS]
    b_raw = proj[..., GDN_QKV_DIM + GDN_VAL_DIM + GDN_HEADS:]
    qkv = jax.nn.silu(causal_depthwise_conv(qkv, conv_w))
    q = qkv[..., :GDN_KEY_DIM].reshape(B, S, GDN_HEADS, GDN_HEAD_K)
    k = qkv[..., GDN_KEY_DIM:2 * GDN_KEY_DIM].reshape(B, S, GDN_HEADS, GDN_HEAD_K)
    v = qkv[..., 2 * GDN_KEY_DIM:].reshape(B, S, GDN_HEADS, GDN_HEAD_V).astype(jnp.float32)
    q = l2norm(q) * (GDN_HEAD_K ** -0.5)
    k = l2norm(k)
    beta = jax.nn.sigmoid(b_raw.astype(jnp.float32))
    g = -jnp.exp(a_log.astype(jnp.float32)) * jax.nn.softplus(
        a_raw.astype(jnp.float32) + dt_bias.astype(jnp.float32))
    o = chunk_gated_delta_rule(q, k, v, g, beta)
    zf = z.reshape(B, S, GDN_HEADS, GDN_HEAD_V).astype(jnp.float32)
    o = rms_norm(o, norm_g) * jax.nn.silu(zf)
    return o.reshape(B, S, GDN_VAL_DIM).astype(h.dtype) @ w_out


def sqrelu_mlp(h, w1, w2):
    return jnp.square(jax.nn.relu(h @ w1)) @ w2


def _normal(key, shape, fan_in):
    return jax.random.normal(key, shape, jnp.float32) * (fan_in ** -0.5)


def _fwd_setup_inputs(seed: int = 0) -> dict:
    key = jax.random.key(seed)
    ks = jax.random.split(key, 24)
    D = D_MODEL
    Nc, Ng = N_CONV_LAYERS, N_GDN_LAYERS
    x = jax.random.normal(ks[0], (BATCH, SEQ, D), jnp.float32)
    norm_mix_g = 1.0 + 0.02 * jax.random.normal(ks[1], (DEPTH, D), jnp.float32)
    norm_ffn_g = 1.0 + 0.02 * jax.random.normal(ks[2], (DEPTH, D), jnp.float32)
    final_norm_g = 1.0 + 0.02 * jax.random.normal(ks[3], (D,), jnp.float32)
    cv_w_pw1 = _normal(ks[4], (Nc, D, 2 * D), D)
    cv_b_pw1 = 0.01 * jax.random.normal(ks[5], (Nc, 2 * D), jnp.float32)
    cv_w_dw = _normal(ks[6], (Nc, CONV_WIDTH, D), CONV_WIDTH)
    cv_b_dw = 0.01 * jax.random.normal(ks[7], (Nc, D), jnp.float32)
    cv_ln_g = 1.0 + 0.02 * jax.random.normal(ks[8], (Nc, D), jnp.float32)
    cv_ln_b = 0.01 * jax.random.normal(ks[9], (Nc, D), jnp.float32)
    cv_w_pw2 = _normal(ks[10], (Nc, D, D), D)
    cv_b_pw2 = 0.01 * jax.random.normal(ks[11], (Nc, D), jnp.float32)
    gdn_w_in = _normal(ks[12], (Ng, D, GDN_IN_DIM), D)
    gdn_conv_w = _normal(ks[13], (Ng, SHORT_CONV_WIDTH, GDN_QKV_DIM), SHORT_CONV_WIDTH)
    gdn_a_log = jnp.log(jax.random.uniform(ks[14], (Ng, GDN_HEADS), jnp.float32, 1.0, 16.0))
    dt = jnp.exp(jax.random.uniform(ks[15], (Ng, GDN_HEADS), jnp.float32,
                                    math.log(1e-3), math.log(1e-1)))
    gdn_dt_bias = dt + jnp.log(-jnp.expm1(-dt))
    gdn_norm_g = 1.0 + 0.02 * jax.random.normal(ks[16], (Ng, GDN_HEAD_V), jnp.float32)
    gdn_w_out = _normal(ks[17], (Ng, GDN_VAL_DIM, D), GDN_VAL_DIM)
    mlp_w1 = _normal(ks[18], (DEPTH, D, D_FF), D)
    mlp_w2 = _normal(ks[19], (DEPTH, D_FF, D), D_FF)
    return {'x': x, 'norm_mix_g': norm_mix_g, 'norm_ffn_g': norm_ffn_g,
            'final_norm_g': final_norm_g,
            'cv_w_pw1': cv_w_pw1, 'cv_b_pw1': cv_b_pw1, 'cv_w_dw': cv_w_dw,
            'cv_b_dw': cv_b_dw, 'cv_ln_g': cv_ln_g, 'cv_ln_b': cv_ln_b,
            'cv_w_pw2': cv_w_pw2, 'cv_b_pw2': cv_b_pw2,
            'gdn_w_in': gdn_w_in, 'gdn_conv_w': gdn_conv_w, 'gdn_a_log': gdn_a_log,
            'gdn_dt_bias': gdn_dt_bias, 'gdn_norm_g': gdn_norm_g, 'gdn_w_out': gdn_w_out,
            'mlp_w1': mlp_w1, 'mlp_w2': mlp_w2}


def _fwd_reference(x, norm_mix_g, norm_ffn_g, final_norm_g,
              cv_w_pw1, cv_b_pw1, cv_w_dw, cv_b_dw, cv_ln_g, cv_ln_b, cv_w_pw2, cv_b_pw2,
              gdn_w_in, gdn_conv_w, gdn_a_log, gdn_dt_bias, gdn_norm_g, gdn_w_out,
              mlp_w1, mlp_w2):
    h = x
    for i in range(DEPTH):
        hn = rms_norm(h, norm_mix_g[i])
        j = i // N_MIXERS
        if i % N_MIXERS == 0:
            mix = conformer_conv(hn, cv_w_pw1[j], cv_b_pw1[j], cv_w_dw[j], cv_b_dw[j],
                                 cv_ln_g[j], cv_ln_b[j], cv_w_pw2[j], cv_b_pw2[j])
        else:
            mix = gated_deltanet(hn, gdn_w_in[j], gdn_conv_w[j], gdn_a_log[j],
                                 gdn_dt_bias[j], gdn_norm_g[j], gdn_w_out[j])
        h = h + mix
        h = h + sqrelu_mlp(rms_norm(h, norm_ffn_g[i]), mlp_w1[i], mlp_w2[i])
    return rms_norm(h, final_norm_g)


import jax as _jax
import jax.numpy as _jnp

TWIN_FORMAT = 'train_step'
FWD_PARAMS = ['x', 'norm_mix_g', 'norm_ffn_g', 'final_norm_g', 'cv_w_pw1', 'cv_b_pw1', 'cv_w_dw', 'cv_b_dw', 'cv_ln_g', 'cv_ln_b', 'cv_w_pw2', 'cv_b_pw2', 'gdn_w_in', 'gdn_conv_w', 'gdn_a_log', 'gdn_dt_bias', 'gdn_norm_g', 'gdn_w_out', 'mlp_w1', 'mlp_w2']
TWIN_WEIGHTS = ['norm_mix_g', 'norm_ffn_g', 'final_norm_g', 'cv_w_pw1', 'cv_b_pw1', 'cv_w_dw', 'cv_b_dw', 'cv_ln_g', 'cv_ln_b', 'cv_w_pw2', 'cv_b_pw2', 'gdn_w_in', 'gdn_conv_w', 'gdn_a_log', 'gdn_dt_bias', 'gdn_norm_g', 'gdn_w_out', 'mlp_w1', 'mlp_w2']
TWIN_DIFF_INPUT = 'x'
TWIN_INPUTS = ['x', 'norm_mix_g', 'norm_ffn_g', 'final_norm_g', 'cv_w_pw1', 'cv_b_pw1', 'cv_w_dw', 'cv_b_dw', 'cv_ln_g', 'cv_ln_b', 'cv_w_pw2', 'cv_b_pw2', 'gdn_w_in', 'gdn_conv_w', 'gdn_a_log', 'gdn_dt_bias', 'gdn_norm_g', 'gdn_w_out', 'mlp_w1', 'mlp_w2', 'loss_target', 'm_norm_mix_g', 'm_norm_ffn_g', 'm_final_norm_g', 'm_cv_w_pw1', 'm_cv_b_pw1', 'm_cv_w_dw', 'm_cv_b_dw', 'm_cv_ln_g', 'm_cv_ln_b', 'm_cv_w_pw2', 'm_cv_b_pw2', 'm_gdn_w_in', 'm_gdn_conv_w', 'm_gdn_a_log', 'm_gdn_dt_bias', 'm_gdn_norm_g', 'm_gdn_w_out', 'm_mlp_w1', 'm_mlp_w2', 'v_norm_mix_g', 'v_norm_ffn_g', 'v_final_norm_g', 'v_cv_w_pw1', 'v_cv_b_pw1', 'v_cv_w_dw', 'v_cv_b_dw', 'v_cv_ln_g', 'v_cv_ln_b', 'v_cv_w_pw2', 'v_cv_b_pw2', 'v_gdn_w_in', 'v_gdn_conv_w', 'v_gdn_a_log', 'v_gdn_dt_bias', 'v_gdn_norm_g', 'v_gdn_w_out', 'v_mlp_w1', 'v_mlp_w2']
TWIN_OUTPUTS = ['loss', 'grad_x', 'grad_norm_mix_g', 'grad_norm_ffn_g', 'grad_final_norm_g', 'grad_cv_w_pw1', 'grad_cv_b_pw1', 'grad_cv_w_dw', 'grad_cv_b_dw', 'grad_cv_ln_g', 'grad_cv_ln_b', 'grad_cv_w_pw2', 'grad_cv_b_pw2', 'grad_gdn_w_in', 'grad_gdn_conv_w', 'grad_gdn_a_log', 'grad_gdn_dt_bias', 'grad_gdn_norm_g', 'grad_gdn_w_out', 'grad_mlp_w1', 'grad_mlp_w2', 'delta_norm_mix_g', 'delta_norm_ffn_g', 'delta_final_norm_g', 'delta_cv_w_pw1', 'delta_cv_b_pw1', 'delta_cv_w_dw', 'delta_cv_b_dw', 'delta_cv_ln_g', 'delta_cv_ln_b', 'delta_cv_w_pw2', 'delta_cv_b_pw2', 'delta_gdn_w_in', 'delta_gdn_conv_w', 'delta_gdn_a_log', 'delta_gdn_dt_bias', 'delta_gdn_norm_g', 'delta_gdn_w_out', 'delta_mlp_w1', 'delta_mlp_w2', 'new_m_norm_mix_g', 'new_m_norm_ffn_g', 'new_m_final_norm_g', 'new_m_cv_w_pw1', 'new_m_cv_b_pw1', 'new_m_cv_w_dw', 'new_m_cv_b_dw', 'new_m_cv_ln_g', 'new_m_cv_ln_b', 'new_m_cv_w_pw2', 'new_m_cv_b_pw2', 'new_m_gdn_w_in', 'new_m_gdn_conv_w', 'new_m_gdn_a_log', 'new_m_gdn_dt_bias', 'new_m_gdn_norm_g', 'new_m_gdn_w_out', 'new_m_mlp_w1', 'new_m_mlp_w2', 'new_v_norm_mix_g', 'new_v_norm_ffn_g', 'new_v_final_norm_g', 'new_v_cv_w_pw1', 'new_v_cv_b_pw1', 'new_v_cv_w_dw', 'new_v_cv_b_dw', 'new_v_cv_ln_g', 'new_v_cv_ln_b', 'new_v_cv_w_pw2', 'new_v_cv_b_pw2', 'new_v_gdn_w_in', 'new_v_gdn_conv_w', 'new_v_gdn_a_log', 'new_v_gdn_dt_bias', 'new_v_gdn_norm_g', 'new_v_gdn_w_out', 'new_v_mlp_w1', 'new_v_mlp_w2']
TWIN_LEAF_KINDS = {'loss': 'loss', 'grad_x': 'grad_x', 'grad_norm_mix_g': 'grad_w', 'grad_norm_ffn_g': 'grad_w', 'grad_final_norm_g': 'grad_w', 'grad_cv_w_pw1': 'grad_w', 'grad_cv_b_pw1': 'grad_w', 'grad_cv_w_dw': 'grad_w', 'grad_cv_b_dw': 'grad_w', 'grad_cv_ln_g': 'grad_w', 'grad_cv_ln_b': 'grad_w', 'grad_cv_w_pw2': 'grad_w', 'grad_cv_b_pw2': 'grad_w', 'grad_gdn_w_in': 'grad_w', 'grad_gdn_conv_w': 'grad_w', 'grad_gdn_a_log': 'grad_w', 'grad_gdn_dt_bias': 'grad_w', 'grad_gdn_norm_g': 'grad_w', 'grad_gdn_w_out': 'grad_w', 'grad_mlp_w1': 'grad_w', 'grad_mlp_w2': 'grad_w', 'delta_norm_mix_g': 'delta_w', 'delta_norm_ffn_g': 'delta_w', 'delta_final_norm_g': 'delta_w', 'delta_cv_w_pw1': 'delta_w', 'delta_cv_b_pw1': 'delta_w', 'delta_cv_w_dw': 'delta_w', 'delta_cv_b_dw': 'delta_w', 'delta_cv_ln_g': 'delta_w', 'delta_cv_ln_b': 'delta_w', 'delta_cv_w_pw2': 'delta_w', 'delta_cv_b_pw2': 'delta_w', 'delta_gdn_w_in': 'delta_w', 'delta_gdn_conv_w': 'delta_w', 'delta_gdn_a_log': 'delta_w', 'delta_gdn_dt_bias': 'delta_w', 'delta_gdn_norm_g': 'delta_w', 'delta_gdn_w_out': 'delta_w', 'delta_mlp_w1': 'delta_w', 'delta_mlp_w2': 'delta_w', 'new_m_norm_mix_g': 'new_m', 'new_m_norm_ffn_g': 'new_m', 'new_m_final_norm_g': 'new_m', 'new_m_cv_w_pw1': 'new_m', 'new_m_cv_b_pw1': 'new_m', 'new_m_cv_w_dw': 'new_m', 'new_m_cv_b_dw': 'new_m', 'new_m_cv_ln_g': 'new_m', 'new_m_cv_ln_b': 'new_m', 'new_m_cv_w_pw2': 'new_m', 'new_m_cv_b_pw2': 'new_m', 'new_m_gdn_w_in': 'new_m', 'new_m_gdn_conv_w': 'new_m', 'new_m_gdn_a_log': 'new_m', 'new_m_gdn_dt_bias': 'new_m', 'new_m_gdn_norm_g': 'new_m', 'new_m_gdn_w_out': 'new_m', 'new_m_mlp_w1': 'new_m', 'new_m_mlp_w2': 'new_m', 'new_v_norm_mix_g': 'new_v', 'new_v_norm_ffn_g': 'new_v', 'new_v_final_norm_g': 'new_v', 'new_v_cv_w_pw1': 'new_v', 'new_v_cv_b_pw1': 'new_v', 'new_v_cv_w_dw': 'new_v', 'new_v_cv_b_dw': 'new_v', 'new_v_cv_ln_g': 'new_v', 'new_v_cv_ln_b': 'new_v', 'new_v_cv_w_pw2': 'new_v', 'new_v_cv_b_pw2': 'new_v', 'new_v_gdn_w_in': 'new_v', 'new_v_gdn_conv_w': 'new_v', 'new_v_gdn_a_log': 'new_v', 'new_v_gdn_dt_bias': 'new_v', 'new_v_gdn_norm_g': 'new_v', 'new_v_gdn_w_out': 'new_v', 'new_v_mlp_w1': 'new_v', 'new_v_mlp_w2': 'new_v'}


def _forward(args):
    return _fwd_reference(*[args[k] for k in FWD_PARAMS])


def _output_shape():
    out = _jax.eval_shape(lambda: _forward(_fwd_setup_inputs(0)))
    return out.shape, out.dtype

N_MICROBATCH = 1
ADAM_LR = 0.001
ADAM_B1 = 0.9
ADAM_B2 = 0.999
ADAM_EPS = 1e-08
ADAM_WD = 0.01
ADAM_STEP = 10
PER_EXAMPLE_BATCH_AXIS = {'x': 0, 'loss_target': 0}
SHARED_INPUTS = []
_WEIGHT_DTYPES = {'norm_mix_g': _jnp.float32, 'norm_ffn_g': _jnp.float32, 'final_norm_g': _jnp.float32, 'cv_w_pw1': _jnp.float32, 'cv_b_pw1': _jnp.float32, 'cv_w_dw': _jnp.float32, 'cv_b_dw': _jnp.float32, 'cv_ln_g': _jnp.float32, 'cv_ln_b': _jnp.float32, 'cv_w_pw2': _jnp.float32, 'cv_b_pw2': _jnp.float32, 'gdn_w_in': _jnp.float32, 'gdn_conv_w': _jnp.float32, 'gdn_a_log': _jnp.float32, 'gdn_dt_bias': _jnp.float32, 'gdn_norm_g': _jnp.float32, 'gdn_w_out': _jnp.float32, 'mlp_w1': _jnp.float32, 'mlp_w2': _jnp.float32}
MOMENT_SCALE = {'norm_mix_g': 1.605556e-01, 'norm_ffn_g': 2.089497e-01, 'final_norm_g': 6.493633e+01, 'cv_w_pw1': 1.260743e-01, 'cv_b_pw1': 1.708718e-01, 'cv_w_dw': 1.689018e-01, 'cv_b_dw': 3.771399e-01, 'cv_ln_g': 2.349739e-01, 'cv_ln_b': 2.160930e-01, 'cv_w_pw2': 1.689210e-01, 'cv_b_pw2': 3.927759e-01, 'gdn_w_in': 7.304870e-02, 'gdn_conv_w': 6.471791e-02, 'gdn_a_log': 2.945672e-01, 'gdn_dt_bias': 2.820885e-01, 'gdn_norm_g': 2.710729e-01, 'gdn_w_out': 9.261342e-02, 'mlp_w1': 1.047138e-01, 'mlp_w2': 2.027757e-01}


def _to_microbatches(a, axis):
    t = _jnp.moveaxis(a, axis, 0)
    t = t.reshape((N_MICROBATCH, t.shape[0] // N_MICROBATCH) + t.shape[1:])
    return _jnp.moveaxis(t, 1, axis + 1)


def setup_inputs(seed: int = 0) -> dict:
    inp = _fwd_setup_inputs(seed)
    key = _jax.random.fold_in(_jax.random.key(seed), 7919)
    shape, _ = _output_shape()
    out = dict(inp)
    out["loss_target"] = _jax.random.normal(_jax.random.fold_in(key, 0), shape, _jnp.float32)
    for i, name in enumerate(TWIN_WEIGHTS):
        w = inp[name].astype(_jnp.float32)
        if MOMENT_SCALE is None:
            s = _jnp.sqrt(_jnp.mean(_jnp.square(w)) + 1e-30)
        else:
            s = MOMENT_SCALE[name]
        km, kv = _jax.random.split(_jax.random.fold_in(key, i + 1))
        out[name] = w
        out["m_" + name] = s * _jax.random.normal(km, w.shape, _jnp.float32)
        out["v_" + name] = (s * s) * _jax.random.uniform(kv, w.shape, _jnp.float32, 0.5, 1.5)
    if N_MICROBATCH > 1:
        for name, axis in PER_EXAMPLE_BATCH_AXIS.items():
            out[name] = _to_microbatches(out[name], axis)
    return {'x': out['x'], 'norm_mix_g': out['norm_mix_g'], 'norm_ffn_g': out['norm_ffn_g'], 'final_norm_g': out['final_norm_g'], 'cv_w_pw1': out['cv_w_pw1'], 'cv_b_pw1': out['cv_b_pw1'], 'cv_w_dw': out['cv_w_dw'], 'cv_b_dw': out['cv_b_dw'], 'cv_ln_g': out['cv_ln_g'], 'cv_ln_b': out['cv_ln_b'], 'cv_w_pw2': out['cv_w_pw2'], 'cv_b_pw2': out['cv_b_pw2'], 'gdn_w_in': out['gdn_w_in'], 'gdn_conv_w': out['gdn_conv_w'], 'gdn_a_log': out['gdn_a_log'], 'gdn_dt_bias': out['gdn_dt_bias'], 'gdn_norm_g': out['gdn_norm_g'], 'gdn_w_out': out['gdn_w_out'], 'mlp_w1': out['mlp_w1'], 'mlp_w2': out['mlp_w2'], 'loss_target': out['loss_target'], 'm_norm_mix_g': out['m_norm_mix_g'], 'm_norm_ffn_g': out['m_norm_ffn_g'], 'm_final_norm_g': out['m_final_norm_g'], 'm_cv_w_pw1': out['m_cv_w_pw1'], 'm_cv_b_pw1': out['m_cv_b_pw1'], 'm_cv_w_dw': out['m_cv_w_dw'], 'm_cv_b_dw': out['m_cv_b_dw'], 'm_cv_ln_g': out['m_cv_ln_g'], 'm_cv_ln_b': out['m_cv_ln_b'], 'm_cv_w_pw2': out['m_cv_w_pw2'], 'm_cv_b_pw2': out['m_cv_b_pw2'], 'm_gdn_w_in': out['m_gdn_w_in'], 'm_gdn_conv_w': out['m_gdn_conv_w'], 'm_gdn_a_log': out['m_gdn_a_log'], 'm_gdn_dt_bias': out['m_gdn_dt_bias'], 'm_gdn_norm_g': out['m_gdn_norm_g'], 'm_gdn_w_out': out['m_gdn_w_out'], 'm_mlp_w1': out['m_mlp_w1'], 'm_mlp_w2': out['m_mlp_w2'], 'v_norm_mix_g': out['v_norm_mix_g'], 'v_norm_ffn_g': out['v_norm_ffn_g'], 'v_final_norm_g': out['v_final_norm_g'], 'v_cv_w_pw1': out['v_cv_w_pw1'], 'v_cv_b_pw1': out['v_cv_b_pw1'], 'v_cv_w_dw': out['v_cv_w_dw'], 'v_cv_b_dw': out['v_cv_b_dw'], 'v_cv_ln_g': out['v_cv_ln_g'], 'v_cv_ln_b': out['v_cv_ln_b'], 'v_cv_w_pw2': out['v_cv_w_pw2'], 'v_cv_b_pw2': out['v_cv_b_pw2'], 'v_gdn_w_in': out['v_gdn_w_in'], 'v_gdn_conv_w': out['v_gdn_conv_w'], 'v_gdn_a_log': out['v_gdn_a_log'], 'v_gdn_dt_bias': out['v_gdn_dt_bias'], 'v_gdn_norm_g': out['v_gdn_norm_g'], 'v_gdn_w_out': out['v_gdn_w_out'], 'v_mlp_w1': out['v_mlp_w1'], 'v_mlp_w2': out['v_mlp_w2']}


def _loss(weights, diff, rest, loss_target):
    with _jax.named_scope("forward"):
        args = {**rest, TWIN_DIFF_INPUT: diff, **{k: w.astype(_WEIGHT_DTYPES[k]) for k, w in weights.items()}}
        y = _forward(args)
    with _jax.named_scope("loss_head"):
        err = _jnp.square(y.astype(_jnp.float32) - loss_target)
        return 0.5 * _jnp.sum(_jnp.mean(err, axis=-1)) if err.ndim else 0.5 * err


def _adamw(w, g, m, v):
    m = ADAM_B1 * m + (1.0 - ADAM_B1) * g
    v = ADAM_B2 * v + (1.0 - ADAM_B2) * _jnp.square(g)
    m_hat = m / (1.0 - ADAM_B1 ** ADAM_STEP)
    v_hat = v / (1.0 - ADAM_B2 ** ADAM_STEP)
    delta = -ADAM_LR * (m_hat / (_jnp.sqrt(v_hat) + ADAM_EPS) + ADAM_WD * w)
    return delta, m, v


def reference(x, norm_mix_g, norm_ffn_g, final_norm_g, cv_w_pw1, cv_b_pw1, cv_w_dw, cv_b_dw, cv_ln_g, cv_ln_b, cv_w_pw2, cv_b_pw2, gdn_w_in, gdn_conv_w, gdn_a_log, gdn_dt_bias, gdn_norm_g, gdn_w_out, mlp_w1, mlp_w2, loss_target, m_norm_mix_g, m_norm_ffn_g, m_final_norm_g, m_cv_w_pw1, m_cv_b_pw1, m_cv_w_dw, m_cv_b_dw, m_cv_ln_g, m_cv_ln_b, m_cv_w_pw2, m_cv_b_pw2, m_gdn_w_in, m_gdn_conv_w, m_gdn_a_log, m_gdn_dt_bias, m_gdn_norm_g, m_gdn_w_out, m_mlp_w1, m_mlp_w2, v_norm_mix_g, v_norm_ffn_g, v_final_norm_g, v_cv_w_pw1, v_cv_b_pw1, v_cv_w_dw, v_cv_b_dw, v_cv_ln_g, v_cv_ln_b, v_cv_w_pw2, v_cv_b_pw2, v_gdn_w_in, v_gdn_conv_w, v_gdn_a_log, v_gdn_dt_bias, v_gdn_norm_g, v_gdn_w_out, v_mlp_w1, v_mlp_w2):
    given = dict(x=x, norm_mix_g=norm_mix_g, norm_ffn_g=norm_ffn_g, final_norm_g=final_norm_g, cv_w_pw1=cv_w_pw1, cv_b_pw1=cv_b_pw1, cv_w_dw=cv_w_dw, cv_b_dw=cv_b_dw, cv_ln_g=cv_ln_g, cv_ln_b=cv_ln_b, cv_w_pw2=cv_w_pw2, cv_b_pw2=cv_b_pw2, gdn_w_in=gdn_w_in, gdn_conv_w=gdn_conv_w, gdn_a_log=gdn_a_log, gdn_dt_bias=gdn_dt_bias, gdn_norm_g=gdn_norm_g, gdn_w_out=gdn_w_out, mlp_w1=mlp_w1, mlp_w2=mlp_w2, loss_target=loss_target, m_norm_mix_g=m_norm_mix_g, m_norm_ffn_g=m_norm_ffn_g, m_final_norm_g=m_final_norm_g, m_cv_w_pw1=m_cv_w_pw1, m_cv_b_pw1=m_cv_b_pw1, m_cv_w_dw=m_cv_w_dw, m_cv_b_dw=m_cv_b_dw, m_cv_ln_g=m_cv_ln_g, m_cv_ln_b=m_cv_ln_b, m_cv_w_pw2=m_cv_w_pw2, m_cv_b_pw2=m_cv_b_pw2, m_gdn_w_in=m_gdn_w_in, m_gdn_conv_w=m_gdn_conv_w, m_gdn_a_log=m_gdn_a_log, m_gdn_dt_bias=m_gdn_dt_bias, m_gdn_norm_g=m_gdn_norm_g, m_gdn_w_out=m_gdn_w_out, m_mlp_w1=m_mlp_w1, m_mlp_w2=m_mlp_w2, v_norm_mix_g=v_norm_mix_g, v_norm_ffn_g=v_norm_ffn_g, v_final_norm_g=v_final_norm_g, v_cv_w_pw1=v_cv_w_pw1, v_cv_b_pw1=v_cv_b_pw1, v_cv_w_dw=v_cv_w_dw, v_cv_b_dw=v_cv_b_dw, v_cv_ln_g=v_cv_ln_g, v_cv_ln_b=v_cv_ln_b, v_cv_w_pw2=v_cv_w_pw2, v_cv_b_pw2=v_cv_b_pw2, v_gdn_w_in=v_gdn_w_in, v_gdn_conv_w=v_gdn_conv_w, v_gdn_a_log=v_gdn_a_log, v_gdn_dt_bias=v_gdn_dt_bias, v_gdn_norm_g=v_gdn_norm_g, v_gdn_w_out=v_gdn_w_out, v_mlp_w1=v_mlp_w1, v_mlp_w2=v_mlp_w2)
    weights = {n: given[n] for n in TWIN_WEIGHTS}
    shared = {n: given[n] for n in SHARED_INPUTS}
    per_example = {n: given[n] for n in ['x']}
    grad_fn = _jax.value_and_grad(_loss, argnums=(0, 1))

    def one_microbatch(ex, loss_target):
        ex = dict(ex)
        diff = ex.pop(TWIN_DIFF_INPUT)
        return grad_fn(weights, diff, {**shared, **ex}, loss_target)

    if N_MICROBATCH == 1:
        loss, (grad_w, grad_x) = one_microbatch(per_example, given["loss_target"])
    else:
        def body(carry, xs):
            loss_sum, grad_sum = carry
            l_k, (gw_k, gx_k) = one_microbatch(xs[0], xs[1])
            with _jax.named_scope("update"):
                return (loss_sum + l_k, _jax.tree.map(_jnp.add, grad_sum, gw_k)), gx_k

        init = (_jnp.zeros((), _jnp.float32), _jax.tree.map(_jnp.zeros_like, weights))
        (loss, grad_w), grad_x = _jax.lax.scan(body, init, (per_example, given["loss_target"]))
    with _jax.named_scope("update"):
        delta_w, new_m, new_v = {}, {}, {}
        for n in TWIN_WEIGHTS:
            delta_w[n], new_m[n], new_v[n] = _adamw(weights[n], grad_w[n], given["m_" + n], given["v_" + n])
    return (loss, grad_x, *[grad_w[n] for n in TWIN_WEIGHTS], *[delta_w[n] for n in TWIN_WEIGHTS],
            *[new_m[n] for n in TWIN_WEIGHTS], *[new_v[n] for n in TWIN_WEIGHTS])
```

```python
import functools

import jax
import jax.numpy as jnp
from jax import lax
from jax.experimental import pallas as pl
from jax.experimental.pallas import tpu as pltpu

f32, bf16 = jnp.float32, jnp.bfloat16

NORM_EPS = 1e-6
L2_EPS = 1e-6
CHUNK = 64
LANES = 128
SUBLANES = 8
N_DEV = 8
VMEM_LIMIT = 56 * 1024 * 1024
CONV_PAD = 32
CHUNKS_PER_STEP = 4
NEG = -1e30

ADAM_LR, ADAM_B1, ADAM_B2, ADAM_EPS, ADAM_WD, ADAM_STEP = 0.001, 0.9, 0.999, 1e-08, 0.01, 10

NT = (((1,), (1,)), ((), ()))
TN = (((0,), (0,)), ((), ()))
HI = lax.Precision.HIGHEST


def _pc(body, *, name, grid, in_specs, out_specs, out_shape, scratch=(), sem=None):
    return pl.pallas_call(
        body, name=name, grid=grid, in_specs=in_specs, out_specs=out_specs, out_shape=out_shape,
        scratch_shapes=list(scratch),
        compiler_params=pltpu.CompilerParams(dimension_semantics=sem, vmem_limit_bytes=VMEM_LIMIT))


def _rows(tm, n):
    return pl.BlockSpec((tm, n), lambda i: (i, 0))


def _const(shape):
    return pl.BlockSpec(shape, lambda *_: (0,) * len(shape))


def _resident(shape):
    return pl.BlockSpec(shape, lambda *_: (0,) * len(shape), pipeline_mode=pl.Buffered(1))


def _tile(t, pref):
    return pref if t % pref == 0 else t


def _dot(a, b):
    return jnp.dot(a.astype(bf16), b.astype(bf16), preferred_element_type=f32)


def _dot_nt(a, b):
    return lax.dot_general(a.astype(bf16), b.astype(bf16), NT, preferred_element_type=f32)


def _dot_tn(a, b):
    return lax.dot_general(a.astype(bf16), b.astype(bf16), TN, preferred_element_type=f32)


def _sigmoid(x):
    return 1.0 / (1.0 + jnp.exp(-x))


def _silu_grad(x):
    s = _sigmoid(x)
    return s * (1.0 + x * (1.0 - s))


def _rms(x, g):
    rstd = lax.rsqrt(jnp.mean(x * x, axis=-1, keepdims=True) + NORM_EPS)
    xh = x * rstd
    return xh * g, xh, rstd


def _rms_bwd(dn, xh, rstd, g):
    dxh = dn * g
    return rstd * (dxh - xh * jnp.mean(dxh * xh, axis=-1, keepdims=True))


def _acc_init(step, *refs):
    @pl.when(step == 0)
    def _():
        for r in refs:
            r[...] = jnp.zeros(r.shape, r.dtype)


def _acc_rows(ref, val):
    ref[0:1, :] += jnp.sum(val, axis=0, keepdims=True)


def _pw1_glu(x, g, w, b):
    t, d = x.shape
    tm = _tile(t, 256)

    def body(x_ref, g_ref, w_ref, b_ref, n_ref, u_ref, gl_ref):
        n, _, _ = _rms(x_ref[...], g_ref[...])
        nb = n.astype(bf16)
        n_ref[...] = nb
        u = jnp.dot(nb, w_ref[...], preferred_element_type=f32) + b_ref[...]
        u_ref[...] = u.astype(bf16)
        gl_ref[...] = u[:, :d] * _sigmoid(u[:, d:])

    return _pc(body, name="pw1_glu", grid=(t // tm,),
               in_specs=[_rows(tm, d), _const((1, d)), _resident((d, 2 * d)), _const((1, 2 * d))],
               out_specs=[_rows(tm, d), _rows(tm, 2 * d), _rows(tm, d)],
               out_shape=[jax.ShapeDtypeStruct((t, d), bf16), jax.ShapeDtypeStruct((t, 2 * d), bf16),
                          jax.ShapeDtypeStruct((t, d), f32)],
               sem=("parallel",))(x, g, w, b)


def _ln_silu_mm_res(dc, ln_g, ln_b, w, b, res):
    t, d = dc.shape
    tm = _tile(t, 256)

    def body(x_ref, g_ref, bb_ref, w_ref, b_ref, r_ref, s_ref, o_ref):
        x = x_ref[...]
        xc = x - jnp.mean(x, axis=-1, keepdims=True)
        rstd = lax.rsqrt(jnp.mean(xc * xc, axis=-1, keepdims=True) + NORM_EPS)
        ln = xc * rstd * g_ref[...] + bb_ref[...]
        sb = (ln * _sigmoid(ln)).astype(bf16)
        s_ref[...] = sb
        o_ref[...] = r_ref[...] + jnp.dot(sb, w_ref[...], preferred_element_type=f32) + b_ref[...]

    return _pc(body, name="ln_silu_pw2", grid=(t // tm,),
               in_specs=[_rows(tm, d), _const((1, d)), _const((1, d)), _resident((d, d)), _const((1, d)), _rows(tm, d)],
               out_specs=[_rows(tm, d), _rows(tm, d)],
               out_shape=[jax.ShapeDtypeStruct((t, d), bf16), jax.ShapeDtypeStruct((t, d), f32)],
               sem=("parallel",))(dc, ln_g, ln_b, w, b, res)


def _mlp_fwd(h, g, w1, w2, name):
    t, d = h.shape
    ff = w1.shape[1]
    tm = _tile(t, 256)

    def body(h_ref, g_ref, w1_ref, w2_ref, n_ref, f_ref, r_ref, o_ref):
        hv = h_ref[...]
        n, _, _ = _rms(hv, g_ref[...])
        nb = n.astype(bf16)
        n_ref[...] = nb
        f = jnp.dot(nb, w1_ref[...], preferred_element_type=f32)
        f_ref[...] = f.astype(bf16)
        rb = jnp.square(jnp.maximum(f, 0.0)).astype(bf16)
        r_ref[...] = rb
        o_ref[...] = hv + jnp.dot(rb, w2_ref[...], preferred_element_type=f32)

    return _pc(body, name=name, grid=(t // tm,),
               in_specs=[_rows(tm, d), _const((1, d)), _resident((d, ff)), _resident((ff, d))],
               out_specs=[_rows(tm, d), _rows(tm, ff), _rows(tm, ff), _rows(tm, d)],
               out_shape=[jax.ShapeDtypeStruct((t, d), bf16), jax.ShapeDtypeStruct((t, ff), bf16),
                          jax.ShapeDtypeStruct((t, ff), bf16), jax.ShapeDtypeStruct((t, d), f32)],
               sem=("parallel",))(h, g, w1, w2)


def _softplus(x):
    return jnp.maximum(x, 0.0) + jnp.log(1.0 + jnp.exp(-jnp.abs(x)))


def _gdn_in(h, g, w_main, w_ab, a_log_pad, dt_pad, n_heads):
    t, d = h.shape
    tm = _tile(t, 256)

    def body(h_ref, g_ref, wm_ref, wab_ref, al_ref, dt_ref, n_ref, qkv_ref, z_ref, ab_ref, gb_ref):
        n, _, _ = _rms(h_ref[...], g_ref[...])
        nb = n.astype(bf16)
        n_ref[...] = nb
        p = jnp.dot(nb, wm_ref[...], preferred_element_type=f32)
        qkv_ref[...] = p[:, :3 * d]
        z_ref[...] = p[:, 3 * d:]
        ab = jnp.dot(nb, wab_ref[...], preferred_element_type=f32)
        ab_ref[...] = ab
        lane = lax.broadcasted_iota(jnp.int32, ab.shape, 1)
        decay = -jnp.exp(al_ref[...]) * _softplus(ab + dt_ref[...])
        gb_ref[...] = jnp.where(lane < n_heads, decay, jnp.where(lane < 2 * n_heads, _sigmoid(ab), 0.0))

    return _pc(body, name="gdn_in", grid=(t // tm,),
               in_specs=[_rows(tm, d), _const((1, d)), _resident((d, 4 * d)), _resident((d, LANES)),
                         _const((1, LANES)), _const((1, LANES))],
               out_specs=[_rows(tm, d), _rows(tm, 3 * d), _rows(tm, d), _rows(tm, LANES), _rows(tm, LANES)],
               out_shape=[jax.ShapeDtypeStruct((t, d), bf16), jax.ShapeDtypeStruct((t, 3 * d), f32),
                          jax.ShapeDtypeStruct((t, d), f32), jax.ShapeDtypeStruct((t, LANES), f32),
                          jax.ShapeDtypeStruct((t, LANES), f32)],
               sem=("parallel",))(h, g, w_main, w_ab, a_log_pad, dt_pad)


def _gated_norm_mm_res(o, z, ng, w, res, n_heads):
    t, d = o.shape
    tm = _tile(t, 256)

    def body(o_ref, z_ref, ng_ref, w_ref, r_ref, on_ref, out_ref):
        for hd in range(n_heads):
            sl = slice(hd * LANES, (hd + 1) * LANES)
            rn, _, _ = _rms(o_ref[:, sl], ng_ref[...])
            zz = z_ref[:, sl]
            on_ref[:, sl] = (rn * (zz * _sigmoid(zz))).astype(bf16)
        out_ref[...] = r_ref[...] + jnp.dot(on_ref[...], w_ref[...], preferred_element_type=f32)

    return _pc(body, name="gated_norm_wout", grid=(t // tm,),
               in_specs=[_rows(tm, d), _rows(tm, d), _const((1, LANES)), _resident((d, d)), _rows(tm, d)],
               out_specs=[_rows(tm, d), _rows(tm, d)],
               out_shape=[jax.ShapeDtypeStruct((t, d), bf16), jax.ShapeDtypeStruct((t, d), f32)],
               sem=("parallel",))(o, z, ng, w, res)


def _loss_head(h, g, tgt):
    t, d = h.shape
    tm = _tile(t, 256)

    def body(h_ref, g_ref, t_ref, loss_ref, dh_ref, dg_ref):
        _acc_init(pl.program_id(0), loss_ref, dg_ref)
        gv = g_ref[...]
        y, xh, rstd = _rms(h_ref[...], gv)
        e = y - t_ref[...]
        loss_ref[...] += 0.5 * jnp.sum(jnp.mean(e * e, axis=-1, keepdims=True))
        dy = e * (1.0 / d)
        _acc_rows(dg_ref, dy * xh)
        dh_ref[...] = _rms_bwd(dy, xh, rstd, gv)

    return _pc(body, name="loss_head", grid=(t // tm,),
               in_specs=[_rows(tm, d), _const((1, d)), _rows(tm, d)],
               out_specs=[_const((SUBLANES, LANES)), _rows(tm, d), _const((SUBLANES, d))],
               out_shape=[jax.ShapeDtypeStruct((SUBLANES, LANES), f32), jax.ShapeDtypeStruct((t, d), f32),
                          jax.ShapeDtypeStruct((SUBLANES, d), f32)],
               sem=("arbitrary",))(h, g, tgt)


def _rms_bwd_res(dn, h, g, dres, name):
    t, d = h.shape
    tm = _tile(t, 256)

    def body(dn_ref, h_ref, g_ref, dr_ref, dh_ref, dg_ref):
        _acc_init(pl.program_id(0), dg_ref)
        gv = g_ref[...]
        _, xh, rstd = _rms(h_ref[...], gv)
        dn = dn_ref[...]
        _acc_rows(dg_ref, dn * xh)
        dh_ref[...] = dr_ref[...] + _rms_bwd(dn, xh, rstd, gv)

    return _pc(body, name=name, grid=(t // tm,),
               in_specs=[_rows(tm, d), _rows(tm, d), _const((1, d)), _rows(tm, d)],
               out_specs=[_rows(tm, d), _const((SUBLANES, d))],
               out_shape=[jax.ShapeDtypeStruct((t, d), f32), jax.ShapeDtypeStruct((SUBLANES, d), f32)],
               sem=("arbitrary",))(dn, h, g, dres)


def _mlp_bwd(dho, h, g, fb, w1, w2, name):
    t, d = h.shape
    ff = w1.shape[1]
    tm = _tile(t, 256)

    def body(do_ref, h_ref, g_ref, f_ref, w1_ref, w2_ref, df_ref, dh_ref, dg_ref, cs_ref):
        _acc_init(pl.program_id(0), dg_ref, cs_ref)
        do = do_ref[...]
        dr = lax.dot_general(do.astype(bf16), w2_ref[...], NT, preferred_element_type=f32)
        dfb = (dr * (2.0 * jnp.maximum(f_ref[...].astype(f32), 0.0))).astype(bf16)
        df_ref[...] = dfb
        dn = lax.dot_general(dfb, w1_ref[...], NT, preferred_element_type=f32)
        gv = g_ref[...]
        _, xh, rstd = _rms(h_ref[...], gv)
        _acc_rows(dg_ref, dn * xh)
        dh = do + _rms_bwd(dn, xh, rstd, gv)
        dh_ref[...] = dh
        _acc_rows(cs_ref, dh)

    return _pc(body, name=name, grid=(t // tm,),
               in_specs=[_rows(tm, d), _rows(tm, d), _const((1, d)), _rows(tm, ff), _resident((d, ff)), _resident((ff, d))],
               out_specs=[_rows(tm, ff), _rows(tm, d), _const((SUBLANES, d)), _const((SUBLANES, d))],
               out_shape=[jax.ShapeDtypeStruct((t, ff), bf16), jax.ShapeDtypeStruct((t, d), f32),
                          jax.ShapeDtypeStruct((SUBLANES, d), f32), jax.ShapeDtypeStruct((SUBLANES, d), f32)],
               sem=("arbitrary",))(dho, h, g, fb, w1, w2)


def _mm_nt(pairs, name):
    t = pairs[0][0].shape[0]
    k = pairs[0][1].shape[0]
    tm = _tile(t, 256)
    npair = len(pairs)

    def body(*refs):
        o_ref = refs[2 * npair]
        acc = None
        for p in range(npair):
            part = lax.dot_general(refs[2 * p][...].astype(bf16), refs[2 * p + 1][...], NT, preferred_element_type=f32)
            acc = part if acc is None else acc + part
        o_ref[...] = acc

    in_specs, args = [], []
    for dy, w in pairs:
        in_specs += [_rows(tm, dy.shape[1]), _resident(w.shape)]
        args += [dy, w]
    return _pc(body, name=name, grid=(t // tm,), in_specs=in_specs, out_specs=_rows(tm, k),
               out_shape=jax.ShapeDtypeStruct((t, k), f32), sem=("parallel",))(*args)


def _mm_tn(x, dy, name):
    t, k = x.shape
    n = dy.shape[1]
    tm = _tile(t, 512)
    cap = max(LANES, (2 * 1024 * 1024) // k)
    tn = n
    if n > cap:
        tn = max(c for c in range(LANES, cap + 1, LANES) if n % c == 0)

    def body(x_ref, dy_ref, o_ref):
        _acc_init(pl.program_id(1), o_ref)
        o_ref[...] += lax.dot_general(x_ref[...].astype(bf16), dy_ref[...].astype(bf16), TN, preferred_element_type=f32)

    return _pc(body, name=name, grid=(n // tn, t // tm),
               in_specs=[pl.BlockSpec((tm, k), lambda j, i: (i, 0)), pl.BlockSpec((tm, tn), lambda j, i: (i, j))],
               out_specs=pl.BlockSpec((k, tn), lambda j, i: (0, j)),
               out_shape=jax.ShapeDtypeStruct((k, n), f32), sem=("parallel", "arbitrary"))(x, dy)


def _gated_norm_bwd(don, o, z, ng, n_heads):
    t, d = o.shape
    tm = _tile(t, 256)

    def body(don_ref, o_ref, z_ref, ng_ref, do_ref, dz_ref, dng_ref):
        _acc_init(pl.program_id(0), dng_ref)
        gv = ng_ref[...]
        for hd in range(n_heads):
            sl = slice(hd * LANES, (hd + 1) * LANES)
            rn, xh, rstd = _rms(o_ref[:, sl], gv)
            zz = z_ref[:, sl]
            don = don_ref[:, sl]
            dz_ref[:, sl] = don * rn * _silu_grad(zz)
            drn = don * (zz * _sigmoid(zz))
            _acc_rows(dng_ref, drn * xh)
            do_ref[:, sl] = _rms_bwd(drn, xh, rstd, gv)

    return _pc(body, name="gated_norm_bwd", grid=(t // tm,),
               in_specs=[_rows(tm, d), _rows(tm, d), _rows(tm, d), _const((1, LANES))],
               out_specs=[_rows(tm, d), _rows(tm, d), _const((SUBLANES, LANES))],
               out_shape=[jax.ShapeDtypeStruct((t, d), f32), jax.ShapeDtypeStruct((t, d), f32),
                          jax.ShapeDtypeStruct((SUBLANES, LANES), f32)],
               sem=("arbitrary",))(don, o, z, ng)


def _gates_bwd(dgb, ab, a_log_pad, dt_pad, n_heads):
    t = ab.shape[0]
    tm = _tile(t, 256)

    def body(dgb_ref, ab_ref, al_ref, dt_ref, dab_ref, dal_ref, ddt_ref):
        _acc_init(pl.program_id(0), dal_ref, ddt_ref)
        ab = ab_ref[...]
        dgb = dgb_ref[...]
        lane = lax.broadcasted_iota(jnp.int32, ab.shape, 1)
        is_a = lane < n_heads
        is_b = jnp.logical_and(lane >= n_heads, lane < 2 * n_heads)
        xa = ab + dt_ref[...]
        neg_a = -jnp.exp(al_ref[...])
        dg_da = neg_a * _sigmoid(xa)
        beta = _sigmoid(ab)
        da = jnp.where(is_a, dgb * dg_da, 0.0)
        dab_ref[...] = da + jnp.where(is_b, dgb * beta * (1.0 - beta), 0.0)
        _acc_rows(dal_ref, jnp.where(is_a, dgb * neg_a * _softplus(xa), 0.0))
        _acc_rows(ddt_ref, da)

    return _pc(body, name="gates_bwd", grid=(t // tm,),
               in_specs=[_rows(tm, LANES), _rows(tm, LANES), _const((1, LANES)), _const((1, LANES))],
               out_specs=[_rows(tm, LANES), _const((SUBLANES, LANES)), _const((SUBLANES, LANES))],
               out_shape=[jax.ShapeDtypeStruct((t, LANES), f32), jax.ShapeDtypeStruct((SUBLANES, LANES), f32),
                          jax.ShapeDtypeStruct((SUBLANES, LANES), f32)],
               sem=("arbitrary",))(dgb, ab, a_log_pad, dt_pad)


def _ln_silu_bwd(ds, dc, ln_g, ln_b):
    t, d = dc.shape
    tm = _tile(t, 256)

    def body(ds_ref, x_ref, g_ref, b_ref, dx_ref, dg_ref, db_ref, cs_ref):
        _acc_init(pl.program_id(0), dg_ref, db_ref, cs_ref)
        x = x_ref[...]
        gv = g_ref[...]
        xc = x - jnp.mean(x, axis=-1, keepdims=True)
        rstd = lax.rsqrt(jnp.mean(xc * xc, axis=-1, keepdims=True) + NORM_EPS)
        xh = xc * rstd
        ln = xh * gv + b_ref[...]
        dln = ds_ref[...] * _silu_grad(ln)
        _acc_rows(dg_ref, dln * xh)
        _acc_rows(db_ref, dln)
        dxh = dln * gv
        dx = rstd * (dxh - jnp.mean(dxh, axis=-1, keepdims=True) - xh * jnp.mean(dxh * xh, axis=-1, keepdims=True))
        dx_ref[...] = dx
        _acc_rows(cs_ref, dx)

    return _pc(body, name="ln_silu_bwd", grid=(t // tm,),
               in_specs=[_rows(tm, d), _rows(tm, d), _const((1, d)), _const((1, d))],
               out_specs=[_rows(tm, d), _const((SUBLANES, d)), _const((SUBLANES, d)), _const((SUBLANES, d))],
               out_shape=[jax.ShapeDtypeStruct((t, d), f32)] + [jax.ShapeDtypeStruct((SUBLANES, d), f32)] * 3,
               sem=("arbitrary",))(ds, dc, ln_g, ln_b)


def _glu_bwd(dgl, ub):
    t, d = dgl.shape
    tm = _tile(t, 256)

    def body(dgl_ref, u_ref, du_ref, cs_ref):
        _acc_init(pl.program_id(0), cs_ref)
        dgl = dgl_ref[...]
        a = u_ref[:, :d].astype(f32)
        sb = _sigmoid(u_ref[:, d:].astype(f32))
        da = dgl * sb
        db = dgl * a * sb * (1.0 - sb)
        du_ref[:, :d] = da.astype(bf16)
        du_ref[:, d:] = db.astype(bf16)
        cs_ref[0:1, :d] += jnp.sum(da, axis=0, keepdims=True)
        cs_ref[0:1, d:] += jnp.sum(db, axis=0, keepdims=True)

    return _pc(body, name="glu_bwd", grid=(t // tm,),
               in_specs=[_rows(tm, d), _rows(tm, 2 * d)],
               out_specs=[_rows(tm, 2 * d), _const((SUBLANES, 2 * d))],
               out_shape=[jax.ShapeDtypeStruct((t, 2 * d), bf16), jax.ShapeDtypeStruct((SUBLANES, 2 * d), f32)],
               sem=("arbitrary",))(dgl, ub)


def _conv_rows(s):
    return 256 if s % 256 == 0 else s


def _conv_tap_sum(pad_ref, w_ref, base, rows, width):
    acc = jnp.zeros((rows, LANES), f32)
    for j in range(width):
        acc = acc + w_ref[j:j + 1, :] * pad_ref[pl.ds(base + CONV_PAD - (width - 1) + j, rows), :]
    return acc


def _l2_silu_post(c, j, n_heads, scale):
    a = c * _sigmoid(c)
    r = lax.rsqrt(jnp.sum(a * a, axis=-1, keepdims=True) + L2_EPS)
    mult = jnp.where(j < n_heads, r * scale, jnp.where(j < 2 * n_heads, r, 1.0))
    return a, r, a * mult


def _dwconv_fwd(x, w, b, name, qk_heads=None):
    bl, s, cn = x.shape
    width = w.shape[0]
    rows = _conv_rows(s)
    scale = float(LANES) ** -0.5

    def body(x_ref, w_ref, b_ref, o_ref, pad_ref):
        j = pl.program_id(1)
        pad_ref[0:CONV_PAD, :] = jnp.zeros((CONV_PAD, LANES), f32)
        pad_ref[CONV_PAD:, :] = x_ref[0]

        def step(i, carry):
            base = pl.multiple_of(i * rows, rows)
            acc = _conv_tap_sum(pad_ref, w_ref, base, rows, width)
            if qk_heads is None:
                acc = acc + b_ref[...]
            else:
                _, _, acc = _l2_silu_post(acc, j, qk_heads, scale)
            o_ref[0, pl.ds(base, rows), :] = acc
            return carry

        lax.fori_loop(0, s // rows, step, 0)

    return _pc(body, name=name, grid=(bl, cn // LANES),
               in_specs=[pl.BlockSpec((1, s, LANES), lambda bi, j: (bi, 0, j)),
                         pl.BlockSpec((width, LANES), lambda bi, j: (0, j)),
                         pl.BlockSpec((1, LANES), lambda bi, j: (0, j))],
               out_specs=pl.BlockSpec((1, s, LANES), lambda bi, j: (bi, 0, j)),
               out_shape=jax.ShapeDtypeStruct((bl, s, cn), f32),
               scratch=[pltpu.VMEM((s + CONV_PAD, LANES), f32)],
               sem=("parallel", "parallel"))(x, w, b)


def _dwconv_bwd(x, dys, w, name, qk_heads=None):
    bl, s, cn = x.shape
    width = w.shape[0]
    wp = -(-width // SUBLANES) * SUBLANES
    rows = _conv_rows(s)
    scale = float(LANES) ** -0.5
    nblk = s // rows
    ndy = len(dys)

    def body(*refs):
        x_ref, w_ref = refs[0], refs[1]
        dy_refs = refs[2:2 + ndy]
        dx_ref, dw_ref, db_ref, xpad, dypad, acc = refs[2 + ndy:]
        j = pl.program_id(0)
        bi = pl.program_id(1)
        _acc_init(bi, acc, db_ref)
        xpad[0:CONV_PAD, :] = jnp.zeros((CONV_PAD, LANES), f32)
        xpad[CONV_PAD:, :] = x_ref[0]
        dypad[s:, :] = jnp.zeros((CONV_PAD, LANES), f32)
        if qk_heads is None:
            dypad[0:s, :] = dy_refs[0][0]
        else:
            def pre(i, carry):
                base = pl.multiple_of(i * rows, rows)
                c = _conv_tap_sum(xpad, w_ref, base, rows, width)
                a, r, _ = _l2_silu_post(c, j, qk_heads, scale)
                dq = dy_refs[0][0, pl.ds(base, rows), :]
                dk = dy_refs[1][0, pl.ds(base, rows), :]
                dv = dy_refs[2][0, pl.ds(base, rows), :]
                dy = jnp.where(j < qk_heads, dq * scale, jnp.where(j < 2 * qk_heads, dk, dv))
                da_l2 = r * (dy - a * (r * r) * jnp.sum(a * dy, axis=-1, keepdims=True))
                da = jnp.where(j < 2 * qk_heads, da_l2, dy)
                dypad[pl.ds(base, rows), :] = da * _silu_grad(c)
                return carry

            lax.fori_loop(0, nblk, pre, 0)

        def step(i, carry):
            base = pl.multiple_of(i * rows, rows)
            dxa = jnp.zeros((rows, LANES), f32)
            for jj in range(width):
                dxa = dxa + w_ref[jj:jj + 1, :] * dypad[pl.ds(base + (width - 1) - jj, rows), :]
            dx_ref[0, pl.ds(base, rows), :] = dxa
            dyc = dypad[pl.ds(base, rows), :]
            db_ref[...] += dyc.reshape(rows // SUBLANES, SUBLANES, LANES).sum(axis=0)
            for jj in range(width):
                prod = dyc * xpad[pl.ds(base + CONV_PAD - (width - 1) + jj, rows), :]
                acc[jj * SUBLANES:(jj + 1) * SUBLANES, :] += prod.reshape(rows // SUBLANES, SUBLANES, LANES).sum(axis=0)
            return carry

        lax.fori_loop(0, nblk, step, 0)

        @pl.when(bi == bl - 1)
        def _():
            dw_ref[...] = jnp.zeros((wp, LANES), f32)
            for jj in range(width):
                dw_ref[jj:jj + 1, :] = jnp.sum(acc[jj * SUBLANES:(jj + 1) * SUBLANES, :], axis=0, keepdims=True)

    if qk_heads is None:
        dy_specs = [pl.BlockSpec((1, s, LANES), lambda j, bi: (bi, 0, j))]
    else:
        hh = qk_heads
        dy_specs = [pl.BlockSpec((1, s, LANES), lambda j, bi: (bi * hh + jnp.minimum(j, hh - 1), 0, 0)),
                    pl.BlockSpec((1, s, LANES), lambda j, bi: (bi * hh + jnp.clip(j - hh, 0, hh - 1), 0, 0)),
                    pl.BlockSpec((1, s, LANES), lambda j, bi: (bi * hh + jnp.clip(j - 2 * hh, 0, hh - 1), 0, 0))]
    return _pc(body, name=name, grid=(cn // LANES, bl),
               in_specs=[pl.BlockSpec((1, s, LANES), lambda j, bi: (bi, 0, j)),
                         pl.BlockSpec((width, LANES), lambda j, bi: (0, j))] + dy_specs,
               out_specs=[pl.BlockSpec((1, s, LANES), lambda j, bi: (bi, 0, j)),
                          pl.BlockSpec((wp, LANES), lambda j, bi: (0, j)),
                          pl.BlockSpec((SUBLANES, LANES), lambda j, bi: (0, j))],
               out_shape=[jax.ShapeDtypeStruct((bl, s, cn), f32), jax.ShapeDtypeStruct((wp, cn), f32),
                          jax.ShapeDtypeStruct((SUBLANES, cn), f32)],
               scratch=[pltpu.VMEM((s + CONV_PAD, LANES), f32), pltpu.VMEM((s + CONV_PAD, LANES), f32),
                        pltpu.VMEM((width * SUBLANES, LANES), f32)],
               sem=("parallel", "arbitrary"))(x, w, *dys)


def _iota2():
    r = lax.broadcasted_iota(jnp.int32, (CHUNK, CHUNK), 0)
    c = lax.broadcasted_iota(jnp.int32, (CHUNK, CHUNK), 1)
    return r, c


def _bcast(col):
    return jnp.broadcast_to(col, (CHUNK, LANES))


def _cumsum_col(g_raw, r, c):
    tri = (r >= c).astype(f32)
    return jnp.dot(tri, _bcast(g_raw), precision=HI, preferred_element_type=f32)[:, :1]


def _rev_cumsum_col(x, r, c):
    tri = (c >= r).astype(f32)
    return jnp.dot(tri, _bcast(x), precision=HI, preferred_element_type=f32)[:, :1]


def _colsum_as_col(e):
    return lax.dot_general(e, jnp.ones((CHUNK, LANES), f32), TN, precision=HI, preferred_element_type=f32)[:, :1]


def _decay(gc, r, c):
    gcb = _bcast(gc)
    grow = lax.dot_general(jnp.ones((CHUNK, LANES), f32), gcb, NT, precision=HI, preferred_element_type=f32) * (1.0 / LANES)
    return jnp.exp(jnp.where(r >= c, gcb[:, :CHUNK] - grow, NEG))


def _mm3(a, b):
    ah = a.astype(bf16)
    al = (a - ah.astype(f32)).astype(bf16)
    bh = b.astype(bf16)
    bl = (b - bh.astype(f32)).astype(bf16)
    d = functools.partial(jnp.dot, preferred_element_type=f32)
    return d(ah, bh) + d(ah, bl) + d(al, bh)


def _inv_unit_lower(a, r, c):
    eye = (r == c).astype(f32)
    same = (r >> 4) == (c >> 4)
    ad = jnp.where(same, a, 0.0)
    ao = a - ad
    x = ad
    td = eye - x
    for _ in range(3):
        x = _mm3(x, x)
        td = td + _mm3(td, x)
    b = _mm3(td, ao)
    b2 = _mm3(b, b)
    return _mm3(eye - b + b2 - _mm3(b, b2), td)


def _gdn_specs(n_heads, nblk, rows):
    def qkv(off):
        return pl.BlockSpec((rows, LANES), lambda bh, n: ((bh // n_heads) * nblk + n, off * n_heads + bh % n_heads))

    def per_head(last):
        return pl.BlockSpec((1, rows, last), lambda bh, n: (bh, n, 0))

    return qkv, per_head


def _gdn_prep(qkv, gb2, bl, s, n_heads):
    g_chunks = CHUNKS_PER_STEP if (s // CHUNK) % CHUNKS_PER_STEP == 0 else 1
    rows = g_chunks * CHUNK
    nblk = s // rows
    bh_n = bl * n_heads
    qkv_spec, ph = _gdn_specs(n_heads, nblk, rows)

    def body(k_ref, v_ref, gb_ref, u_ref, w_ref, t_ref, gc_ref):
        r, c = _iota2()
        for ci in range(g_chunks):
            sl = slice(ci * CHUNK, (ci + 1) * CHUNK)
            k = k_ref[sl, :]
            v = v_ref[sl, :]
            gb = gb_ref[0, sl, :]
            beta = gb[:, 1:2]
            gc = _cumsum_col(gb[:, 0:1], r, c)
            dmat = _decay(gc, r, c)
            kb = k * beta
            a = jnp.where(r > c, _dot_nt(kb, k) * dmat, 0.0)
            tm = _inv_unit_lower(a, r, c)
            u_ref[0, sl, :] = _dot(tm, v * beta)
            w_ref[0, sl, :] = _dot(tm, kb * jnp.exp(gc))
            t_ref[0, sl, :] = tm
            gc_ref[0, sl, :] = gc

    return _pc(body, name="gdn_prep", grid=(bh_n, nblk),
               in_specs=[qkv_spec(1), qkv_spec(2), ph(2)],
               out_specs=[ph(LANES), ph(LANES), ph(CHUNK), ph(1)],
               out_shape=[jax.ShapeDtypeStruct((bh_n, s, LANES), f32), jax.ShapeDtypeStruct((bh_n, s, LANES), f32),
                          jax.ShapeDtypeStruct((bh_n, s, CHUNK), f32), jax.ShapeDtypeStruct((bh_n, s, 1), f32)],
               sem=("parallel", "parallel"))(qkv, qkv, gb2)


def _gdn_scan(qkv, u, w, gc, bl, s, n_heads):
    g_chunks = CHUNKS_PER_STEP if (s // CHUNK) % CHUNKS_PER_STEP == 0 else 1
    rows = g_chunks * CHUNK
    nblk = s // rows
    bh_n = bl * n_heads
    d = n_heads * LANES
    qkv_spec, ph = _gdn_specs(n_heads, nblk, rows)

    def body(q_ref, k_ref, u_ref, w_ref, gc_ref, o_ref, vn_ref, ss_ref, s_scr):
        _acc_init(pl.program_id(1), s_scr)
        r, c = _iota2()
        for ci in range(g_chunks):
            sl = slice(ci * CHUNK, (ci + 1) * CHUNK)
            q = q_ref[sl, :]
            k = k_ref[sl, :]
            gc = gc_ref[0, sl, :]
            st = s_scr[...]
            ss_ref[0, ci * LANES:(ci + 1) * LANES, :] = st
            dmat = _decay(gc, r, c)
            vn = u_ref[0, sl, :] - _dot(w_ref[0, sl, :], st)
            vn_ref[0, sl, :] = vn
            p = _dot_nt(q, k) * dmat
            o_ref[sl, :] = _dot(q * jnp.exp(gc), st) + _dot(p, vn)
            g_last = gc[CHUNK - 1:CHUNK, :]
            s_scr[...] = jnp.exp(g_last) * st + _dot_tn(k * jnp.exp(g_last - gc), vn)

    return _pc(body, name="gdn_scan", grid=(bh_n, nblk),
               in_specs=[qkv_spec(0), qkv_spec(1), ph(LANES), ph(LANES), ph(1)],
               out_specs=[pl.BlockSpec((rows, LANES), lambda bh, n: ((bh // n_heads) * nblk + n, bh % n_heads)),
                          ph(LANES),
                          pl.BlockSpec((1, g_chunks * LANES, LANES), lambda bh, n: (bh, n, 0))],
               out_shape=[jax.ShapeDtypeStruct((bl * s, d), f32), jax.ShapeDtypeStruct((bh_n, s, LANES), f32),
                          jax.ShapeDtypeStruct((bh_n, (s // CHUNK) * LANES, LANES), f32)],
               scratch=[pltpu.VMEM((LANES, LANES), f32)],
               sem=("parallel", "arbitrary"))(qkv, qkv, u, w, gc)


def _gdn_scan_bwd(do, qkv, w, vn, gc, ss, bl, s, n_heads):
    g_chunks = CHUNKS_PER_STEP if (s // CHUNK) % CHUNKS_PER_STEP == 0 else 1
    rows = g_chunks * CHUNK
    nblk = s // rows
    bh_n = bl * n_heads

    def rev(n):
        return nblk - 1 - n

    def qkv_spec(off):
        return pl.BlockSpec((rows, LANES), lambda bh, n: ((bh // n_heads) * nblk + rev(n), off * n_heads + bh % n_heads))

    def ph(last):
        return pl.BlockSpec((1, rows, last), lambda bh, n: (bh, rev(n), 0))

    def body(do_ref, q_ref, k_ref, w_ref, vn_ref, gc_ref, ss_ref, du_ref, dw_ref, dq_ref, dk_ref, dgc_ref, ds_scr):
        _acc_init(pl.program_id(1), ds_scr)
        r, c = _iota2()
        row = lax.broadcasted_iota(jnp.int32, (CHUNK, 1), 0)
        for ci in reversed(range(g_chunks)):
            sl = slice(ci * CHUNK, (ci + 1) * CHUNK)
            do = do_ref[sl, :]
            q = q_ref[sl, :]
            k = k_ref[sl, :]
            w_c = w_ref[0, sl, :]
            vn = vn_ref[0, sl, :]
            gc = gc_ref[0, sl, :]
            st = ss_ref[0, ci * LANES:(ci + 1) * LANES, :]
            ds = ds_scr[...]
            dmat = _decay(gc, r, c)
            gam = jnp.exp(gc)
            g_last = gc[CHUNK - 1:CHUNK, :]
            gam_last = jnp.exp(g_last)
            kd_scale = jnp.exp(g_last - gc)
            kdec = k * kd_scale
            qg = q * gam
            qk = _dot_nt(q, k)
            p = qk * dmat
            dvn = _dot_tn(p, do) + _dot(kdec, ds)
            ds_scr[...] = gam_last * ds + _dot_tn(qg, do) - _dot_tn(w_c, dvn)
            du_ref[0, sl, :] = dvn
            dw_ref[0, sl, :] = -_dot_nt(dvn, st)
            dqg = _dot_nt(do, st)
            dpd = jnp.where(r >= c, _dot_nt(do, vn), 0.0) * dmat
            dq_ref[0, sl, :] = dqg * gam + _dot(dpd, k)
            dkdec = _dot_nt(vn, ds)
            dk_ref[0, sl, :] = _dot_tn(dpd, q) + dkdec * kd_scale
            ep = dpd * qk
            kd_rows = jnp.sum(dkdec * kdec, axis=-1, keepdims=True)
            dgc = (jnp.sum(dqg * qg, axis=-1, keepdims=True) + jnp.sum(ep, axis=-1, keepdims=True)
                   - _colsum_as_col(ep) - kd_rows)
            extra = jnp.sum(kd_rows) + gam_last * jnp.sum(st * ds)
            dgc_ref[0, sl, :] = dgc + jnp.where(row == CHUNK - 1, extra, 0.0)

    return _pc(body, name="gdn_scan_bwd", grid=(bh_n, nblk),
               in_specs=[pl.BlockSpec((rows, LANES), lambda bh, n: ((bh // n_heads) * nblk + rev(n), bh % n_heads)),
                         qkv_spec(0), qkv_spec(1), ph(LANES), ph(LANES), ph(1),
                         pl.BlockSpec((1, g_chunks * LANES, LANES), lambda bh, n: (bh, rev(n), 0))],
               out_specs=[ph(LANES), ph(LANES), ph(LANES), ph(LANES), ph(1)],
               out_shape=[jax.ShapeDtypeStruct((bh_n, s, LANES), f32)] * 4 + [jax.ShapeDtypeStruct((bh_n, s, 1), f32)],
               scratch=[pltpu.VMEM((LANES, LANES), f32)],
               sem=("parallel", "arbitrary"))(do, qkv, qkv, w, vn, gc, ss)


def _gdn_prep_bwd(qkv, gb2, tmat, gc, du, dw, dk_scan, dgc_scan, bl, s, n_heads):
    g_chunks = CHUNKS_PER_STEP if (s // CHUNK) % CHUNKS_PER_STEP == 0 else 1
    rows = g_chunks * CHUNK
    nblk = s // rows
    bh_n = bl * n_heads
    qkv_spec, ph = _gdn_specs(n_heads, nblk, rows)

    def body(k_ref, v_ref, gb_ref, t_ref, gc_ref, du_ref, dw_ref, dks_ref, dgs_ref, dk_ref, dv_ref, dgb_ref):
        r, c = _iota2()
        for ci in range(g_chunks):
            sl = slice(ci * CHUNK, (ci + 1) * CHUNK)
            k = k_ref[sl, :]
            v = v_ref[sl, :]
            beta = gb_ref[0, sl, 1:2]
            tm = t_ref[0, sl, :]
            gc = gc_ref[0, sl, :]
            du = du_ref[0, sl, :]
            dw_c = dw_ref[0, sl, :]
            dmat = _decay(gc, r, c)
            gam = jnp.exp(gc)
            kb = k * beta
            kbg = kb * gam
            dt = _dot_nt(du, v * beta) + _dot_nt(dw_c, kbg)
            dvb = _dot_tn(tm, du)
            dkbg = _dot_tn(tm, dw_c)
            da = jnp.where(r > c, -_dot_tn(tm, _dot_nt(dt, tm)), 0.0)
            dad = da * dmat
            kk = _dot_nt(kb, k)
            dkb = dkbg * gam + _dot(dad, k)
            dk_ref[0, sl, :] = dks_ref[0, sl, :] + _dot_tn(dad, kb) + dkb * beta
            dv_ref[0, sl, :] = dvb * beta
            dbeta = jnp.sum(dvb * v, axis=-1, keepdims=True) + jnp.sum(dkb * k, axis=-1, keepdims=True)
            ea = dad * kk
            dgc = (dgs_ref[0, sl, :] + jnp.sum(dkbg * kbg, axis=-1, keepdims=True)
                   + jnp.sum(ea, axis=-1, keepdims=True) - _colsum_as_col(ea))
            dgb_ref[0, sl, 0:1] = _rev_cumsum_col(dgc, r, c)
            dgb_ref[0, sl, 1:2] = dbeta

    return _pc(body, name="gdn_prep_bwd", grid=(bh_n, nblk),
               in_specs=[qkv_spec(1), qkv_spec(2), ph(2), ph(CHUNK), ph(1), ph(LANES), ph(LANES), ph(LANES), ph(1)],
               out_specs=[ph(LANES), ph(LANES), ph(2)],
               out_shape=[jax.ShapeDtypeStruct((bh_n, s, LANES), f32), jax.ShapeDtypeStruct((bh_n, s, LANES), f32),
                          jax.ShapeDtypeStruct((bh_n, s, 2), f32)],
               sem=("parallel", "parallel"))(qkv, qkv, gb2, tmat, gc, du, dw, dk_scan, dgc_scan)


def _row(v):
    return v.reshape(1, -1).astype(f32)


def _pad_lanes(v):
    v = v.reshape(1, -1).astype(f32)
    return jnp.pad(v, ((0, 0), (0, LANES - v.shape[1])))


def _local_step(x, tgt, p):
    bl, s, d = x.shape
    t = bl * s
    n_heads = p["gdn_a_log"].shape[-1]
    assert d == n_heads * LANES and s % CHUNK == 0
    x2 = x.reshape(t, d)
    tgt2 = tgt.reshape(t, d)
    gr = {}

    n0, ub, gl = _pw1_glu(x2, _row(p["norm_mix_g"][0]), p["cv_w_pw1"], _row(p["cv_b_pw1"]))
    dc = _dwconv_fwd(gl.reshape(bl, s, d), p["cv_w_dw"], _row(p["cv_b_dw"]), "dwconv_fwd").reshape(t, d)
    sb, h1 = _ln_silu_mm_res(dc, _row(p["cv_ln_g"]), _row(p["cv_ln_b"]), p["cv_w_pw2"], _row(p["cv_b_pw2"]), x2)
    n1, f0, r0, h2 = _mlp_fwd(h1, _row(p["norm_ffn_g"][0]), p["mlp_w1"][0], p["mlp_w2"][0], "mlp_fwd0")

    w_in = p["gdn_w_in"]
    w_main = w_in[:, :4 * d]
    w_ab = jnp.pad(w_in[:, 4 * d:], ((0, 0), (0, LANES - 2 * n_heads)))
    a_log_pad = _pad_lanes(p["gdn_a_log"])
    dt_pad = _pad_lanes(p["gdn_dt_bias"])
    n2, qkv_pre, z, ab, gbeta = _gdn_in(h2, _row(p["norm_mix_g"][1]), w_main, w_ab, a_log_pad, dt_pad, n_heads)
    zero_bias = jnp.zeros((1, 3 * d), f32)
    qkv = _dwconv_fwd(qkv_pre.reshape(bl, s, 3 * d), p["gdn_conv_w"], zero_bias, "sconv_fwd", qk_heads=n_heads).reshape(t, 3 * d)
    gb2 = gbeta[:, :2 * n_heads].reshape(bl, s, 2, n_heads).transpose(0, 3, 1, 2).reshape(bl * n_heads, s, 2)
    u, w, tmat, gc = _gdn_prep(qkv, gb2, bl, s, n_heads)
    o, vn, ss = _gdn_scan(qkv, u, w, gc, bl, s, n_heads)
    onb, h3 = _gated_norm_mm_res(o, z, _row(p["gdn_norm_g"]), p["gdn_w_out"], h2, n_heads)
    n3, f1, r1, h4 = _mlp_fwd(h3, _row(p["norm_ffn_g"][1]), p["mlp_w1"][1], p["mlp_w2"][1], "mlp_fwd1")
    loss_acc, dh4, dgf = _loss_head(h4, _row(p["final_norm_g"]), tgt2)
    loss = loss_acc[0, 0]
    gr["final_norm_g"] = dgf[0]

    df1, dh3, dg_ffn1, _ = _mlp_bwd(dh4, h3, _row(p["norm_ffn_g"][1]), f1, p["mlp_w1"][1], p["mlp_w2"][1], "mlp_bwd1")
    dw2_1 = _mm_tn(r1, dh4, "dw_mlp2_1")
    dw1_1 = _mm_tn(n3, df1, "dw_mlp1_1")

    gr["gdn_w_out"] = _mm_tn(onb, dh3, "dw_gdn_out")
    don = _mm_nt([(dh3, p["gdn_w_out"])], "dx_gdn_out")
    do, dz, dng = _gated_norm_bwd(don, o, z, _row(p["gdn_norm_g"]), n_heads)
    gr["gdn_norm_g"] = dng[0]
    du, dw_, dq, dk_scan, dgc_scan = _gdn_scan_bwd(do, qkv, w, vn, gc, ss, bl, s, n_heads)
    dk, dv, dgb2 = _gdn_prep_bwd(qkv, gb2, tmat, gc, du, dw_, dk_scan, dgc_scan, bl, s, n_heads)
    dqkv_pre, dconv_w, _ = _dwconv_bwd(qkv_pre.reshape(bl, s, 3 * d), [dq, dk, dv], p["gdn_conv_w"], "sconv_bwd", qk_heads=n_heads)
    gr["gdn_conv_w"] = dconv_w[:p["gdn_conv_w"].shape[0]]
    dgb = dgb2.reshape(bl, n_heads, s, 2).transpose(0, 2, 3, 1).reshape(t, 2 * n_heads)
    dgb = jnp.pad(dgb, ((0, 0), (0, LANES - 2 * n_heads)))
    dab, dal, ddt = _gates_bwd(dgb, ab, a_log_pad, dt_pad, n_heads)
    gr["gdn_a_log"] = dal[0, :n_heads]
    gr["gdn_dt_bias"] = ddt[0, :n_heads]
    dqkv2 = dqkv_pre.reshape(t, 3 * d)
    gr["gdn_w_in"] = jnp.concatenate(
        [_mm_tn(n2, dqkv2, "dw_gdn_in_qkv"), _mm_tn(n2, dz, "dw_gdn_in_z"), _mm_tn(n2, dab, "dw_gdn_in_ab")[:, :2 * n_heads]], axis=1)
    dn2 = _mm_nt([(dqkv2, w_main[:, :3 * d]), (dz, w_main[:, 3 * d:]), (dab, w_ab)], "dx_gdn_in")
    dh2, dg_mix1 = _rms_bwd_res(dn2, h2, _row(p["norm_mix_g"][1]), dh3, "rms_bwd_gdn")

    df0, dh1, dg_ffn0, cs_h1 = _mlp_bwd(dh2, h1, _row(p["norm_ffn_g"][0]), f0, p["mlp_w1"][0], p["mlp_w2"][0], "mlp_bwd0")
    dw2_0 = _mm_tn(r0, dh2, "dw_mlp2_0")
    dw1_0 = _mm_tn(n1, df0, "dw_mlp1_0")
    gr["mlp_w1"] = jnp.stack([dw1_0, dw1_1])
    gr["mlp_w2"] = jnp.stack([dw2_0, dw2_1])
    gr["norm_ffn_g"] = jnp.stack([dg_ffn0[0], dg_ffn1[0]])

    gr["cv_b_pw2"] = cs_h1[0]
    gr["cv_w_pw2"] = _mm_tn(sb, dh1, "dw_pw2")
    ds = _mm_nt([(dh1, p["cv_w_pw2"])], "dx_pw2")
    ddc, dlng, dlnb, cs_dc = _ln_silu_bwd(ds, dc, _row(p["cv_ln_g"]), _row(p["cv_ln_b"]))
    gr["cv_ln_g"] = dlng[0]
    gr["cv_ln_b"] = dlnb[0]
    gr["cv_b_dw"] = cs_dc[0]
    dgl, dw_dw, _ = _dwconv_bwd(gl.reshape(bl, s, d), [ddc.reshape(bl, s, d)], p["cv_w_dw"], "dwconv_bwd")
    gr["cv_w_dw"] = dw_dw[:p["cv_w_dw"].shape[0]]
    dub, cs_u = _glu_bwd(dgl.reshape(t, d), ub)
    gr["cv_b_pw1"] = cs_u[0]
    gr["cv_w_pw1"] = _mm_tn(n0, dub, "dw_pw1")
    dn0 = _mm_nt([(dub, p["cv_w_pw1"])], "dx_pw1")
    dx, dg_mix0 = _rms_bwd_res(dn0, x2, _row(p["norm_mix_g"][0]), dh1, "rms_bwd_conv")
    gr["norm_mix_g"] = jnp.stack([dg_mix0[0], dg_mix1[0]])
    return loss, dx.reshape(bl, s, d), gr


ANY = pl.BlockSpec(memory_space=pl.ANY)
MESH = pl.DeviceIdType.MESH


def _flip(v, bit):
    return 1 - v if bit else v


def _all_gather(shard):
    rr = shard.shape[0]

    def body(x_ref, out_ref, send_sems, recv_sems, local_sem):
        x, y, c = lax.axis_index("x"), lax.axis_index("y"), lax.axis_index("c")
        me, sibling = (x, y, c), (x, y, 1 - c)
        chips = [(1 - x, y), (x, 1 - y), (1 - x, 1 - y)]

        def slot(px, py, pc):
            return out_ref.at[4 * px + 2 * py + pc]

        def copy(k, block, to, src=None):
            return pltpu.make_async_remote_copy(
                src_ref=slot(*block) if src is None else src, dst_ref=slot(*block),
                send_sem=send_sems.at[k], recv_sem=recv_sems.at[k], device_id=to, device_id_type=MESH)

        mine = pltpu.make_async_copy(x_ref, slot(*me), local_sem)
        mine.start()
        first = [copy(0, me, sibling, src=x_ref)]
        first += [copy(1 + j, me, (*chip, c), src=x_ref) for j, chip in enumerate(chips)]
        for cp in first:
            cp.start()
        passed = [copy(4 + j, (*chip, c), sibling) for j, chip in enumerate(chips)]
        for j, chip in enumerate(chips):
            copy(1 + j, (*chip, c), me).wait_recv()
            passed[j].start()
        copy(0, sibling, me).wait_recv()
        for j, chip in enumerate(chips):
            copy(4 + j, (*chip, 1 - c), me).wait_recv()
        for cp in first + passed:
            cp.wait_send()
        mine.wait()

    return pl.pallas_call(
        body, name="weights_all_gather",
        out_shape=jax.ShapeDtypeStruct((N_DEV, rr, LANES), shard.dtype),
        in_specs=[ANY], out_specs=ANY,
        scratch_shapes=[pltpu.SemaphoreType.DMA((7,)), pltpu.SemaphoreType.DMA((7,)), pltpu.SemaphoreType.DMA(())],
        compiler_params=pltpu.CompilerParams(has_side_effects=True),
    )(shard)


def _all_to_all(send):
    def body(s_ref, r_ref, send_sems, recv_sems, local_sem):
        x, y, c = lax.axis_index("x"), lax.axis_index("y"), lax.axis_index("c")
        me = 4 * x + 2 * y + c
        mine = pltpu.make_async_copy(s_ref.at[me], r_ref.at[me], local_sem)
        mine.start()
        copies = []
        for k in range(1, N_DEV):
            px, py, pc = _flip(x, k & 4), _flip(y, k & 2), _flip(c, k & 1)
            peer = 4 * px + 2 * py + pc
            copies.append(pltpu.make_async_remote_copy(
                src_ref=s_ref.at[peer], dst_ref=r_ref.at[me],
                send_sem=send_sems.at[k - 1], recv_sem=recv_sems.at[k - 1],
                device_id=(px, py, pc), device_id_type=MESH))
        for cp in copies:
            cp.start()
        for cp in copies:
            cp.wait_recv()
        for cp in copies:
            cp.wait_send()
        mine.wait()

    return pl.pallas_call(
        body, name="grads_all_to_all",
        out_shape=jax.ShapeDtypeStruct(send.shape, send.dtype),
        in_specs=[ANY], out_specs=ANY,
        scratch_shapes=[pltpu.SemaphoreType.DMA((7,)), pltpu.SemaphoreType.DMA((7,)), pltpu.SemaphoreType.DMA(())],
        compiler_params=pltpu.CompilerParams(has_side_effects=True),
    )(send)


def _sum_devices(recv):
    _, rr, _ = recv.shape
    tr = _tile(rr, 512)

    def body(r_ref, o_ref):
        acc = r_ref[0]
        for i in range(1, N_DEV):
            acc = acc + r_ref[i]
        o_ref[...] = acc

    return _pc(body, name="grads_sum", grid=(rr // tr,),
               in_specs=[pl.BlockSpec((N_DEV, tr, LANES), lambda i: (0, i, 0))],
               out_specs=_rows(tr, LANES), out_shape=jax.ShapeDtypeStruct((rr, LANES), f32), sem=("parallel",))(recv)


def _adamw(w, g, m, v):
    rr = w.shape[0]
    tr = _tile(rr, 512)
    bc1 = 1.0 - ADAM_B1 ** ADAM_STEP
    bc2 = 1.0 - ADAM_B2 ** ADAM_STEP

    def body(w_ref, g_ref, m_ref, v_ref, d_ref, nm_ref, nv_ref):
        gv = g_ref[...]
        nm = ADAM_B1 * m_ref[...] + (1.0 - ADAM_B1) * gv
        nv = ADAM_B2 * v_ref[...] + (1.0 - ADAM_B2) * (gv * gv)
        nm_ref[...] = nm
        nv_ref[...] = nv
        d_ref[...] = -ADAM_LR * ((nm / bc1) / (jnp.sqrt(nv / bc2) + ADAM_EPS) + ADAM_WD * w_ref[...])

    spec = _rows(tr, LANES)
    return _pc(body, name="adamw", grid=(rr // tr,), in_specs=[spec] * 4, out_specs=[spec] * 3,
               out_shape=[jax.ShapeDtypeStruct((rr, LANES), f32)] * 3, sem=("parallel",))(w, g, m, v)


PACK_ROWS = 512


def _pack(arrs, lead=()):
    nl = len(lead)
    parts, sizes = [], []
    for a in arrs:
        flat = a.reshape(lead + (-1,))
        n = flat.shape[-1]
        rows = -(-n // LANES)
        flat = jnp.pad(flat, [(0, 0)] * nl + [(0, rows * LANES - n)])
        parts.append(flat.reshape(lead + (rows, LANES)))
        sizes.append((rows, n))
    total = sum(r for r, _ in sizes)
    padded = -(-total // PACK_ROWS) * PACK_ROWS
    if padded > total:
        parts.append(jnp.zeros(lead + (padded - total, LANES), parts[0].dtype))
    return jnp.concatenate(parts, axis=nl), sizes


def _unpack(packed, sizes, shapes, lead=()):
    nl = len(lead)
    out, off = [], 0
    for (rows, n), shp in zip(sizes, shapes):
        piece = lax.slice_in_dim(packed, off, off + rows, axis=nl).reshape(lead + (rows * LANES,))
        out.append(lax.slice_in_dim(piece, 0, n, axis=nl).reshape(lead + tuple(shp)))
        off += rows
    return out


def _cols_to_blocks(a):
    n = a.shape[-1] // N_DEV
    a = a.reshape(a.shape[:-1] + (N_DEV, n))
    return jnp.moveaxis(a, -2, 0)


def _blocks_to_cols(a):
    a = jnp.moveaxis(a, 0, -2)
    return a.reshape(a.shape[:-2] + (a.shape[-2] * a.shape[-1],))


def _rows_to_blocks(a):
    k = a.shape[-2] // N_DEV
    a = a.reshape(a.shape[:-2] + (N_DEV, k, a.shape[-1]))
    return jnp.moveaxis(a, -3, 0)


def _blocks_to_rows(a):
    a = jnp.moveaxis(a, 0, -3)
    return a.reshape(a.shape[:-3] + (a.shape[-3] * a.shape[-2], a.shape[-1]))


COL_SHARDED = ("cv_w_pw1", "gdn_w_in", "mlp_w1")
ROW_SHARDED = ("cv_w_pw2", "gdn_w_out", "mlp_w2")
CONV_SHARDED = ("cv_w_dw", "gdn_conv_w")
REPLICATED = ("norm_mix_g", "norm_ffn_g", "final_norm_g", "cv_b_pw1", "cv_b_dw", "cv_ln_g", "cv_ln_b", "cv_b_pw2",
              "gdn_a_log", "gdn_dt_bias", "gdn_norm_g")
WEIGHTS = ("norm_mix_g", "norm_ffn_g", "final_norm_g", "cv_w_pw1", "cv_b_pw1", "cv_w_dw", "cv_b_dw", "cv_ln_g",
           "cv_ln_b", "cv_w_pw2", "cv_b_pw2", "gdn_w_in", "gdn_conv_w", "gdn_a_log", "gdn_dt_bias", "gdn_norm_g",
           "gdn_w_out", "mlp_w1", "mlp_w2")
MATMUL_SHARDED = COL_SHARDED + ROW_SHARDED


def _squeeze_layer(name, a):
    if name in ("norm_mix_g", "norm_ffn_g", "final_norm_g", "mlp_w1", "mlp_w2"):
        return a
    return a[0]


def _gather_weights(shards):
    big = [shards[n].astype(bf16) for n in MATMUL_SHARDED]
    conv = [lax.bitcast_convert_type(shards[n], bf16) for n in CONV_SHARDED]
    packed, sizes = _pack(big + conv)
    gathered = _all_gather(packed)
    pieces = _unpack(gathered, sizes, [a.shape for a in big + conv], lead=(N_DEV,))
    full = {}
    for n, pc in zip(MATMUL_SHARDED + CONV_SHARDED, pieces):
        if n in COL_SHARDED:
            full[n] = _blocks_to_cols(pc)
        elif n in ROW_SHARDED:
            full[n] = _blocks_to_rows(pc)
        else:
            full[n] = _blocks_to_cols(lax.bitcast_convert_type(pc, f32))
    return full


def kernel(x, norm_mix_g, norm_ffn_g, final_norm_g, cv_w_pw1, cv_b_pw1, cv_w_dw, cv_b_dw, cv_ln_g, cv_ln_b, cv_w_pw2, cv_b_pw2, gdn_w_in, gdn_conv_w, gdn_a_log, gdn_dt_bias, gdn_norm_g, gdn_w_out, mlp_w1, mlp_w2, loss_target, m_norm_mix_g, m_norm_ffn_g, m_final_norm_g, m_cv_w_pw1, m_cv_b_pw1, m_cv_w_dw, m_cv_b_dw, m_cv_ln_g, m_cv_ln_b, m_cv_w_pw2, m_cv_b_pw2, m_gdn_w_in, m_gdn_conv_w, m_gdn_a_log, m_gdn_dt_bias, m_gdn_norm_g, m_gdn_w_out, m_mlp_w1, m_mlp_w2, v_norm_mix_g, v_norm_ffn_g, v_final_norm_g, v_cv_w_pw1, v_cv_b_pw1, v_cv_w_dw, v_cv_b_dw, v_cv_ln_g, v_cv_ln_b, v_cv_w_pw2, v_cv_b_pw2, v_gdn_w_in, v_gdn_conv_w, v_gdn_a_log, v_gdn_dt_bias, v_gdn_norm_g, v_gdn_w_out, v_mlp_w1, v_mlp_w2):
    w_in = dict(zip(WEIGHTS, (norm_mix_g, norm_ffn_g, final_norm_g, cv_w_pw1, cv_b_pw1, cv_w_dw, cv_b_dw, cv_ln_g, cv_ln_b, cv_w_pw2, cv_b_pw2, gdn_w_in, gdn_conv_w, gdn_a_log, gdn_dt_bias, gdn_norm_g, gdn_w_out, mlp_w1, mlp_w2)))
    m_in = dict(zip(WEIGHTS, (m_norm_mix_g, m_norm_ffn_g, m_final_norm_g, m_cv_w_pw1, m_cv_b_pw1, m_cv_w_dw, m_cv_b_dw, m_cv_ln_g, m_cv_ln_b, m_cv_w_pw2, m_cv_b_pw2, m_gdn_w_in, m_gdn_conv_w, m_gdn_a_log, m_gdn_dt_bias, m_gdn_norm_g, m_gdn_w_out, m_mlp_w1, m_mlp_w2)))
    v_in = dict(zip(WEIGHTS, (v_norm_mix_g, v_norm_ffn_g, v_final_norm_g, v_cv_w_pw1, v_cv_b_pw1, v_cv_w_dw, v_cv_b_dw, v_cv_ln_g, v_cv_ln_b, v_cv_w_pw2, v_cv_b_pw2, v_gdn_w_in, v_gdn_conv_w, v_gdn_a_log, v_gdn_dt_bias, v_gdn_norm_g, v_gdn_w_out, v_mlp_w1, v_mlp_w2)))
    me = 4 * lax.axis_index("x") + 2 * lax.axis_index("y") + lax.axis_index("c")

    shards = {n: _squeeze_layer(n, w_in[n]) for n in WEIGHTS}
    params = {n: shards[n] for n in REPLICATED}
    params.update(_gather_weights(shards))

    loss_part, grad_x, gr = _local_step(x, loss_target, params)
    loss = lax.psum(loss_part, ("x", "y", "c"))

    small = REPLICATED + CONV_SHARDED
    parts = []
    for n in MATMUL_SHARDED:
        parts.append(_cols_to_blocks(gr[n]) if n in COL_SHARDED else _rows_to_blocks(gr[n]))
    for n in small:
        parts.append(jnp.broadcast_to(gr[n][None], (N_DEV,) + gr[n].shape))
    send, sizes = _pack(parts, lead=(N_DEV,))
    summed = _sum_devices(_all_to_all(send))
    shapes = [pt.shape[1:] for pt in parts]
    g_list = _unpack(summed, sizes, shapes)
    grads = dict(zip(MATMUL_SHARDED + small, g_list))
    for n in CONV_SHARDED:
        cn = shards[n].shape[-1]
        grads[n] = lax.dynamic_slice_in_dim(grads[n], me * cn, cn, axis=1)

    order = list(WEIGHTS)
    gs = [grads[n].reshape(shards[n].shape) for n in order]
    ws = [shards[n] for n in order]
    ms = [_squeeze_layer(n, m_in[n]) for n in order]
    vs = [_squeeze_layer(n, v_in[n]) for n in order]
    wp, psz = _pack(ws)
    gp, _ = _pack(gs)
    mp, _ = _pack(ms)
    vp, _ = _pack(vs)
    dp, nmp, nvp = _adamw(wp, gp, mp, vp)
    shp = [a.shape for a in ws]
    deltas = _unpack(dp, psz, shp)
    new_m = _unpack(nmp, psz, shp)
    new_v = _unpack(nvp, psz, shp)

    def out(n, a):
        return a.reshape(w_in[n].shape)

    outs = [loss, grad_x]
    for group in (gs, deltas, new_m, new_v):
        outs += [out(n, a) for n, a in zip(order, group)]
    return tuple(outs)
```

```python
import functools

import jax
import jax.numpy as jnp
from jax import lax
from jax.experimental import pallas as pl
from jax.experimental.pallas import tpu as pltpu

f32, bf16 = jnp.float32, jnp.bfloat16

NORM_EPS = 1e-6
L2_EPS = 1e-6
CHUNK = 64
LANES = 128
SUBLANES = 8
N_DEV = 8
VMEM_LIMIT = 56 * 1024 * 1024
CONV_PAD = 32
HEADS_PER_STEP = 2
NEG = -1e30

ADAM_LR, ADAM_B1, ADAM_B2, ADAM_EPS, ADAM_WD, ADAM_STEP = 0.001, 0.9, 0.999, 1e-08, 0.01, 10

NT = (((1,), (1,)), ((), ()))
TN = (((0,), (0,)), ((), ()))
HI = lax.Precision.HIGHEST


def _pc(body, *, name, grid, in_specs, out_specs, out_shape, scratch=(), sem=None):
    return pl.pallas_call(
        body, name=name, grid=grid, in_specs=in_specs, out_specs=out_specs, out_shape=out_shape,
        scratch_shapes=list(scratch),
        compiler_params=pltpu.CompilerParams(dimension_semantics=sem, vmem_limit_bytes=VMEM_LIMIT))


def _rows(tm, n):
    return pl.BlockSpec((tm, n), lambda i: (i, 0))


def _const(shape):
    return pl.BlockSpec(shape, lambda *_: (0,) * len(shape))


def _resident(shape):
    return pl.BlockSpec(shape, lambda *_: (0,) * len(shape), pipeline_mode=pl.Buffered(1))


def _tile(t, pref):
    return pref if t % pref == 0 else t


def _dot(a, b):
    return jnp.dot(a.astype(bf16), b.astype(bf16), preferred_element_type=f32)


def _dot_nt(a, b):
    return lax.dot_general(a.astype(bf16), b.astype(bf16), NT, preferred_element_type=f32)


def _dot_tn(a, b):
    return lax.dot_general(a.astype(bf16), b.astype(bf16), TN, preferred_element_type=f32)


def _sigmoid(x):
    return 1.0 / (1.0 + jnp.exp(-x))


def _silu_grad(x):
    s = _sigmoid(x)
    return s * (1.0 + x * (1.0 - s))


def _rms(x, g):
    rstd = lax.rsqrt(jnp.mean(x * x, axis=-1, keepdims=True) + NORM_EPS)
    xh = x * rstd
    return xh * g, xh, rstd


def _rms_bwd(dn, xh, rstd, g):
    dxh = dn * g
    return rstd * (dxh - xh * jnp.mean(dxh * xh, axis=-1, keepdims=True))


def _acc_init(step, *refs):
    @pl.when(step == 0)
    def _():
        for r in refs:
            r[...] = jnp.zeros(r.shape, r.dtype)


def _acc_rows(ref, val):
    ref[0:1, :] += jnp.sum(val, axis=0, keepdims=True)


def _pw1_glu(x, g, w, b):
    t, d = x.shape
    tm = _tile(t, 256)

    def body(x_ref, g_ref, w_ref, b_ref, n_ref, u_ref, gl_ref):
        n, _, _ = _rms(x_ref[...], g_ref[...])
        nb = n.astype(bf16)
        n_ref[...] = nb
        u = jnp.dot(nb, w_ref[...], preferred_element_type=f32) + b_ref[...]
        u_ref[...] = u.astype(bf16)
        gl_ref[...] = u[:, :d] * _sigmoid(u[:, d:])

    return _pc(body, name="pw1_glu", grid=(t // tm,),
               in_specs=[_rows(tm, d), _const((1, d)), _resident((d, 2 * d)), _const((1, 2 * d))],
               out_specs=[_rows(tm, d), _rows(tm, 2 * d), _rows(tm, d)],
               out_shape=[jax.ShapeDtypeStruct((t, d), bf16), jax.ShapeDtypeStruct((t, 2 * d), bf16),
                          jax.ShapeDtypeStruct((t, d), f32)],
               sem=("parallel",))(x, g, w, b)


def _ln_silu_mm_res(dc, ln_g, ln_b, w, b, res):
    t, d = dc.shape
    tm = _tile(t, 256)

    def body(x_ref, g_ref, bb_ref, w_ref, b_ref, r_ref, s_ref, o_ref):
        x = x_ref[...]
        xc = x - jnp.mean(x, axis=-1, keepdims=True)
        rstd = lax.rsqrt(jnp.mean(xc * xc, axis=-1, keepdims=True) + NORM_EPS)
        ln = xc * rstd * g_ref[...] + bb_ref[...]
        sb = (ln * _sigmoid(ln)).astype(bf16)
        s_ref[...] = sb
        o_ref[...] = r_ref[...] + jnp.dot(sb, w_ref[...], preferred_element_type=f32) + b_ref[...]

    return _pc(body, name="ln_silu_pw2", grid=(t // tm,),
               in_specs=[_rows(tm, d), _const((1, d)), _const((1, d)), _resident((d, d)), _const((1, d)), _rows(tm, d)],
               out_specs=[_rows(tm, d), _rows(tm, d)],
               out_shape=[jax.ShapeDtypeStruct((t, d), bf16), jax.ShapeDtypeStruct((t, d), f32)],
               sem=("parallel",))(dc, ln_g, ln_b, w, b, res)


def _mlp_fwd(h, g, w1, w2, name):
    t, d = h.shape
    ff = w1.shape[1]
    tm = _tile(t, 256)

    def body(h_ref, g_ref, w1_ref, w2_ref, n_ref, f_ref, r_ref, o_ref):
        hv = h_ref[...]
        n, _, _ = _rms(hv, g_ref[...])
        nb = n.astype(bf16)
        n_ref[...] = nb
        f = jnp.dot(nb, w1_ref[...], preferred_element_type=f32)
        f_ref[...] = f.astype(bf16)
        rb = jnp.square(jnp.maximum(f, 0.0)).astype(bf16)
        r_ref[...] = rb
        o_ref[...] = hv + jnp.dot(rb, w2_ref[...], preferred_element_type=f32)

    return _pc(body, name=name, grid=(t // tm,),
               in_specs=[_rows(tm, d), _const((1, d)), _resident((d, ff)), _resident((ff, d))],
               out_specs=[_rows(tm, d), _rows(tm, ff), _rows(tm, ff), _rows(tm, d)],
               out_shape=[jax.ShapeDtypeStruct((t, d), bf16), jax.ShapeDtypeStruct((t, ff), bf16),
                          jax.ShapeDtypeStruct((t, ff), bf16), jax.ShapeDtypeStruct((t, d), f32)],
               sem=("parallel",))(h, g, w1, w2)


def _softplus(x):
    return jnp.maximum(x, 0.0) + jnp.log(1.0 + jnp.exp(-jnp.abs(x)))


def _gdn_in(h, g, w_main, w_ab, a_log_pad, dt_pad, n_heads):
    t, d = h.shape
    tm = _tile(t, 256)

    def body(h_ref, g_ref, wm_ref, wab_ref, al_ref, dt_ref, n_ref, qkv_ref, z_ref, ab_ref, gb_ref):
        n, _, _ = _rms(h_ref[...], g_ref[...])
        nb = n.astype(bf16)
        n_ref[...] = nb
        p = jnp.dot(nb, wm_ref[...], preferred_element_type=f32)
        qkv_ref[...] = p[:, :3 * d]
        z_ref[...] = p[:, 3 * d:]
        ab = jnp.dot(nb, wab_ref[...], preferred_element_type=f32)
        ab_ref[...] = ab
        lane = lax.broadcasted_iota(jnp.int32, ab.shape, 1)
        decay = -jnp.exp(al_ref[...]) * _softplus(ab + dt_ref[...])
        gb_ref[...] = jnp.where(lane < n_heads, decay, jnp.where(lane < 2 * n_heads, _sigmoid(ab), 0.0))

    return _pc(body, name="gdn_in", grid=(t // tm,),
               in_specs=[_rows(tm, d), _const((1, d)), _resident((d, 4 * d)), _resident((d, LANES)),
                         _const((1, LANES)), _const((1, LANES))],
               out_specs=[_rows(tm, d), _rows(tm, 3 * d), _rows(tm, d), _rows(tm, LANES), _rows(tm, LANES)],
               out_shape=[jax.ShapeDtypeStruct((t, d), bf16), jax.ShapeDtypeStruct((t, 3 * d), f32),
                          jax.ShapeDtypeStruct((t, d), f32), jax.ShapeDtypeStruct((t, LANES), f32),
                          jax.ShapeDtypeStruct((t, LANES), f32)],
               sem=("parallel",))(h, g, w_main, w_ab, a_log_pad, dt_pad)


def _gated_norm_mm_res(o, z, ng, w, res, n_heads):
    t, d = o.shape
    tm = _tile(t, 256)

    def body(o_ref, z_ref, ng_ref, w_ref, r_ref, on_ref, out_ref):
        for hd in range(n_heads):
            sl = slice(hd * LANES, (hd + 1) * LANES)
            rn, _, _ = _rms(o_ref[:, sl], ng_ref[...])
            zz = z_ref[:, sl]
            on_ref[:, sl] = (rn * (zz * _sigmoid(zz))).astype(bf16)
        out_ref[...] = r_ref[...] + jnp.dot(on_ref[...], w_ref[...], preferred_element_type=f32)

    return _pc(body, name="gated_norm_wout", grid=(t // tm,),
               in_specs=[_rows(tm, d), _rows(tm, d), _const((1, LANES)), _resident((d, d)), _rows(tm, d)],
               out_specs=[_rows(tm, d), _rows(tm, d)],
               out_shape=[jax.ShapeDtypeStruct((t, d), bf16), jax.ShapeDtypeStruct((t, d), f32)],
               sem=("parallel",))(o, z, ng, w, res)


def _loss_head(h, g, tgt):
    t, d = h.shape
    tm = _tile(t, 256)

    def body(h_ref, g_ref, t_ref, loss_ref, dh_ref, dg_ref):
        _acc_init(pl.program_id(0), loss_ref, dg_ref)
        gv = g_ref[...]
        y, xh, rstd = _rms(h_ref[...], gv)
        e = y - t_ref[...]
        loss_ref[...] += 0.5 * jnp.sum(jnp.mean(e * e, axis=-1, keepdims=True))
        dy = e * (1.0 / d)
        _acc_rows(dg_ref, dy * xh)
        dh_ref[...] = _rms_bwd(dy, xh, rstd, gv)

    return _pc(body, name="loss_head", grid=(t // tm,),
               in_specs=[_rows(tm, d), _const((1, d)), _rows(tm, d)],
               out_specs=[_const((SUBLANES, LANES)), _rows(tm, d), _const((SUBLANES, d))],
               out_shape=[jax.ShapeDtypeStruct((SUBLANES, LANES), f32), jax.ShapeDtypeStruct((t, d), f32),
                          jax.ShapeDtypeStruct((SUBLANES, d), f32)],
               sem=("arbitrary",))(h, g, tgt)


def _rms_bwd_res(dn, h, g, dres, name):
    t, d = h.shape
    tm = _tile(t, 256)

    def body(dn_ref, h_ref, g_ref, dr_ref, dh_ref, dg_ref):
        _acc_init(pl.program_id(0), dg_ref)
        gv = g_ref[...]
        _, xh, rstd = _rms(h_ref[...], gv)
        dn = dn_ref[...]
        _acc_rows(dg_ref, dn * xh)
        dh_ref[...] = dr_ref[...] + _rms_bwd(dn, xh, rstd, gv)

    return _pc(body, name=name, grid=(t // tm,),
               in_specs=[_rows(tm, d), _rows(tm, d), _const((1, d)), _rows(tm, d)],
               out_specs=[_rows(tm, d), _const((SUBLANES, d))],
               out_shape=[jax.ShapeDtypeStruct((t, d), f32), jax.ShapeDtypeStruct((SUBLANES, d), f32)],
               sem=("arbitrary",))(dn, h, g, dres)


def _mlp_bwd(dho, h, g, fb, w1, w2, name):
    t, d = h.shape
    ff = w1.shape[1]
    tm = _tile(t, 256)

    def body(do_ref, h_ref, g_ref, f_ref, w1_ref, w2_ref, df_ref, dh_ref, dg_ref, cs_ref):
        _acc_init(pl.program_id(0), dg_ref, cs_ref)
        do = do_ref[...]
        dr = lax.dot_general(do.astype(bf16), w2_ref[...], NT, preferred_element_type=f32)
        dfb = (dr * (2.0 * jnp.maximum(f_ref[...].astype(f32), 0.0))).astype(bf16)
        df_ref[...] = dfb
        dn = lax.dot_general(dfb, w1_ref[...], NT, preferred_element_type=f32)
        gv = g_ref[...]
        _, xh, rstd = _rms(h_ref[...], gv)
        _acc_rows(dg_ref, dn * xh)
        dh = do + _rms_bwd(dn, xh, rstd, gv)
        dh_ref[...] = dh
        _acc_rows(cs_ref, dh)

    return _pc(body, name=name, grid=(t // tm,),
               in_specs=[_rows(tm, d), _rows(tm, d), _const((1, d)), _rows(tm, ff), _resident((d, ff)), _resident((ff, d))],
               out_specs=[_rows(tm, ff), _rows(tm, d), _const((SUBLANES, d)), _const((SUBLANES, d))],
               out_shape=[jax.ShapeDtypeStruct((t, ff), bf16), jax.ShapeDtypeStruct((t, d), f32),
                          jax.ShapeDtypeStruct((SUBLANES, d), f32), jax.ShapeDtypeStruct((SUBLANES, d), f32)],
               sem=("arbitrary",))(dho, h, g, fb, w1, w2)


def _mm_nt(pairs, name):
    t = pairs[0][0].shape[0]
    k = pairs[0][1].shape[0]
    tm = _tile(t, 256)
    npair = len(pairs)

    def body(*refs):
        o_ref = refs[2 * npair]
        acc = None
        for p in range(npair):
            part = lax.dot_general(refs[2 * p][...].astype(bf16), refs[2 * p + 1][...], NT, preferred_element_type=f32)
            acc = part if acc is None else acc + part
        o_ref[...] = acc

    in_specs, args = [], []
    for dy, w in pairs:
        in_specs += [_rows(tm, dy.shape[1]), _resident(w.shape)]
        args += [dy, w]
    return _pc(body, name=name, grid=(t // tm,), in_specs=in_specs, out_specs=_rows(tm, k),
               out_shape=jax.ShapeDtypeStruct((t, k), f32), sem=("parallel",))(*args)


def _mm_tn(x, dy, name):
    t, k = x.shape
    n = dy.shape[1]
    tm = _tile(t, 512)
    cap = max(LANES, (2 * 1024 * 1024) // k)
    tn = n
    if n > cap:
        tn = max(c for c in range(LANES, cap + 1, LANES) if n % c == 0)

    def body(x_ref, dy_ref, o_ref):
        _acc_init(pl.program_id(1), o_ref)
        o_ref[...] += lax.dot_general(x_ref[...].astype(bf16), dy_ref[...].astype(bf16), TN, preferred_element_type=f32)

    return _pc(body, name=name, grid=(n // tn, t // tm),
               in_specs=[pl.BlockSpec((tm, k), lambda j, i: (i, 0)), pl.BlockSpec((tm, tn), lambda j, i: (i, j))],
               out_specs=pl.BlockSpec((k, tn), lambda j, i: (0, j)),
               out_shape=jax.ShapeDtypeStruct((k, n), f32), sem=("parallel", "arbitrary"))(x, dy)


def _gated_norm_bwd(don, o, z, ng, n_heads):
    t, d = o.shape
    tm = _tile(t, 256)

    def body(don_ref, o_ref, z_ref, ng_ref, do_ref, dz_ref, dng_ref):
        _acc_init(pl.program_id(0), dng_ref)
        gv = ng_ref[...]
        for hd in range(n_heads):
            sl = slice(hd * LANES, (hd + 1) * LANES)
            rn, xh, rstd = _rms(o_ref[:, sl], gv)
            zz = z_ref[:, sl]
            don = don_ref[:, sl]
            dz_ref[:, sl] = don * rn * _silu_grad(zz)
            drn = don * (zz * _sigmoid(zz))
            _acc_rows(dng_ref, drn * xh)
            do_ref[:, sl] = _rms_bwd(drn, xh, rstd, gv)

    return _pc(body, name="gated_norm_bwd", grid=(t // tm,),
               in_specs=[_rows(tm, d), _rows(tm, d), _rows(tm, d), _const((1, LANES))],
               out_specs=[_rows(tm, d), _rows(tm, d), _const((SUBLANES, LANES))],
               out_shape=[jax.ShapeDtypeStruct((t, d), f32), jax.ShapeDtypeStruct((t, d), f32),
                          jax.ShapeDtypeStruct((SUBLANES, LANES), f32)],
               sem=("arbitrary",))(don, o, z, ng)


def _gates_bwd(dgb, ab, a_log_pad, dt_pad, n_heads):
    t = ab.shape[0]
    tm = _tile(t, 256)

    def body(dgb_ref, ab_ref, al_ref, dt_ref, dab_ref, dal_ref, ddt_ref):
        _acc_init(pl.program_id(0), dal_ref, ddt_ref)
        ab = ab_ref[...]
        dgb = dgb_ref[...]
        lane = lax.broadcasted_iota(jnp.int32, ab.shape, 1)
        is_a = lane < n_heads
        is_b = jnp.logical_and(lane >= n_heads, lane < 2 * n_heads)
        xa = ab + dt_ref[...]
        neg_a = -jnp.exp(al_ref[...])
        dg_da = neg_a * _sigmoid(xa)
        beta = _sigmoid(ab)
        da = jnp.where(is_a, dgb * dg_da, 0.0)
        dab_ref[...] = da + jnp.where(is_b, dgb * beta * (1.0 - beta), 0.0)
        _acc_rows(dal_ref, jnp.where(is_a, dgb * neg_a * _softplus(xa), 0.0))
        _acc_rows(ddt_ref, da)

    return _pc(body, name="gates_bwd", grid=(t // tm,),
               in_specs=[_rows(tm, LANES), _rows(tm, LANES), _const((1, LANES)), _const((1, LANES))],
               out_specs=[_rows(tm, LANES), _const((SUBLANES, LANES)), _const((SUBLANES, LANES))],
               out_shape=[jax.ShapeDtypeStruct((t, LANES), f32), jax.ShapeDtypeStruct((SUBLANES, LANES), f32),
                          jax.ShapeDtypeStruct((SUBLANES, LANES), f32)],
               sem=("arbitrary",))(dgb, ab, a_log_pad, dt_pad)


def _ln_silu_bwd(ds, dc, ln_g, ln_b):
    t, d = dc.shape
    tm = _tile(t, 256)

    def body(ds_ref, x_ref, g_ref, b_ref, dx_ref, dg_ref, db_ref, cs_ref):
        _acc_init(pl.program_id(0), dg_ref, db_ref, cs_ref)
        x = x_ref[...]
        gv = g_ref[...]
        xc = x - jnp.mean(x, axis=-1, keepdims=True)
        rstd = lax.rsqrt(jnp.mean(xc * xc, axis=-1, keepdims=True) + NORM_EPS)
        xh = xc * rstd
        ln = xh * gv + b_ref[...]
        dln = ds_ref[...] * _silu_grad(ln)
        _acc_rows(dg_ref, dln * xh)
        _acc_rows(db_ref, dln)
        dxh = dln * gv
        dx = rstd * (dxh - jnp.mean(dxh, axis=-1, keepdims=True) - xh * jnp.mean(dxh * xh, axis=-1, keepdims=True))
        dx_ref[...] = dx
        _acc_rows(cs_ref, dx)

    return _pc(body, name="ln_silu_bwd", grid=(t // tm,),
               in_specs=[_rows(tm, d), _rows(tm, d), _const((1, d)), _const((1, d))],
               out_specs=[_rows(tm, d), _const((SUBLANES, d)), _const((SUBLANES, d)), _const((SUBLANES, d))],
               out_shape=[jax.ShapeDtypeStruct((t, d), f32)] + [jax.ShapeDtypeStruct((SUBLANES, d), f32)] * 3,
               sem=("arbitrary",))(ds, dc, ln_g, ln_b)


def _glu_bwd(dgl, ub):
    t, d = dgl.shape
    tm = _tile(t, 256)

    def body(dgl_ref, u_ref, du_ref, cs_ref):
        _acc_init(pl.program_id(0), cs_ref)
        dgl = dgl_ref[...]
        a = u_ref[:, :d].astype(f32)
        sb = _sigmoid(u_ref[:, d:].astype(f32))
        da = dgl * sb
        db = dgl * a * sb * (1.0 - sb)
        du_ref[:, :d] = da.astype(bf16)
        du_ref[:, d:] = db.astype(bf16)
        cs_ref[0:1, :d] += jnp.sum(da, axis=0, keepdims=True)
        cs_ref[0:1, d:] += jnp.sum(db, axis=0, keepdims=True)

    return _pc(body, name="glu_bwd", grid=(t // tm,),
               in_specs=[_rows(tm, d), _rows(tm, 2 * d)],
               out_specs=[_rows(tm, 2 * d), _const((SUBLANES, 2 * d))],
               out_shape=[jax.ShapeDtypeStruct((t, 2 * d), bf16), jax.ShapeDtypeStruct((SUBLANES, 2 * d), f32)],
               sem=("arbitrary",))(dgl, ub)


def _conv_rows(s):
    return 256 if s % 256 == 0 else s


def _conv_tap_sum(pad_ref, w_ref, base, rows, width):
    acc = jnp.zeros((rows, LANES), f32)
    for j in range(width):
        acc = acc + w_ref[j:j + 1, :] * pad_ref[pl.ds(base + CONV_PAD - (width - 1) + j, rows), :]
    return acc


def _l2_silu_post(c, j, n_heads, scale):
    a = c * _sigmoid(c)
    r = lax.rsqrt(jnp.sum(a * a, axis=-1, keepdims=True) + L2_EPS)
    mult = jnp.where(j < n_heads, r * scale, jnp.where(j < 2 * n_heads, r, 1.0))
    return a, r, a * mult


def _dwconv_fwd(x, w, b, name, qk_heads=None):
    bl, s, cn = x.shape
    width = w.shape[0]
    rows = _conv_rows(s)
    scale = float(LANES) ** -0.5

    def body(x_ref, w_ref, b_ref, o_ref, pad_ref):
        j = pl.program_id(1)
        pad_ref[0:CONV_PAD, :] = jnp.zeros((CONV_PAD, LANES), f32)
        pad_ref[CONV_PAD:, :] = x_ref[0]

        def step(i, carry):
            base = pl.multiple_of(i * rows, rows)
            acc = _conv_tap_sum(pad_ref, w_ref, base, rows, width)
            if qk_heads is None:
                acc = acc + b_ref[...]
            else:
                _, _, acc = _l2_silu_post(acc, j, qk_heads, scale)
            o_ref[0, pl.ds(base, rows), :] = acc
            return carry

        lax.fori_loop(0, s // rows, step, 0)

    return _pc(body, name=name, grid=(bl, cn // LANES),
               in_specs=[pl.BlockSpec((1, s, LANES), lambda bi, j: (bi, 0, j)),
                         pl.BlockSpec((width, LANES), lambda bi, j: (0, j)),
                         pl.BlockSpec((1, LANES), lambda bi, j: (0, j))],
               out_specs=pl.BlockSpec((1, s, LANES), lambda bi, j: (bi, 0, j)),
               out_shape=jax.ShapeDtypeStruct((bl, s, cn), f32),
               scratch=[pltpu.VMEM((s + CONV_PAD, LANES), f32)],
               sem=("parallel", "parallel"))(x, w, b)


def _dwconv_bwd(x, dys, w, name, qk_heads=None):
    bl, s, cn = x.shape
    width = w.shape[0]
    wp = -(-width // SUBLANES) * SUBLANES
    rows = _conv_rows(s)
    scale = float(LANES) ** -0.5
    nblk = s // rows
    ndy = len(dys)

    def body(*refs):
        x_ref, w_ref = refs[0], refs[1]
        dy_refs = refs[2:2 + ndy]
        dx_ref, dw_ref, db_ref, xpad, dypad, acc = refs[2 + ndy:]
        j = pl.program_id(0)
        bi = pl.program_id(1)
        _acc_init(bi, acc, db_ref)
        xpad[0:CONV_PAD, :] = jnp.zeros((CONV_PAD, LANES), f32)
        xpad[CONV_PAD:, :] = x_ref[0]
        dypad[s:, :] = jnp.zeros((CONV_PAD, LANES), f32)
        if qk_heads is None:
            dypad[0:s, :] = dy_refs[0][0]
        else:
            def pre(i, carry):
                base = pl.multiple_of(i * rows, rows)
                c = _conv_tap_sum(xpad, w_ref, base, rows, width)
                a, r, _ = _l2_silu_post(c, j, qk_heads, scale)
                dq = dy_refs[0][0, pl.ds(base, rows), :]
                dk = dy_refs[1][0, pl.ds(base, rows), :]
                dv = dy_refs[2][0, pl.ds(base, rows), :]
                dy = jnp.where(j < qk_heads, dq * scale, jnp.where(j < 2 * qk_heads, dk, dv))
                da_l2 = r * (dy - a * (r * r) * jnp.sum(a * dy, axis=-1, keepdims=True))
                da = jnp.where(j < 2 * qk_heads, da_l2, dy)
                dypad[pl.ds(base, rows), :] = da * _silu_grad(c)
                return carry

            lax.fori_loop(0, nblk, pre, 0)

        def step(i, carry):
            base = pl.multiple_of(i * rows, rows)
            dxa = jnp.zeros((rows, LANES), f32)
            for jj in range(width):
                dxa = dxa + w_ref[jj:jj + 1, :] * dypad[pl.ds(base + (width - 1) - jj, rows), :]
            dx_ref[0, pl.ds(base, rows), :] = dxa
            dyc = dypad[pl.ds(base, rows), :]
            db_ref[...] += dyc.reshape(rows // SUBLANES, SUBLANES, LANES).sum(axis=0)
            for jj in range(width):
                prod = dyc * xpad[pl.ds(base + CONV_PAD - (width - 1) + jj, rows), :]
                acc[jj * SUBLANES:(jj + 1) * SUBLANES, :] += prod.reshape(rows // SUBLANES, SUBLANES, LANES).sum(axis=0)
            return carry

        lax.fori_loop(0, nblk, step, 0)

        @pl.when(bi == bl - 1)
        def _():
            dw_ref[...] = jnp.zeros((wp, LANES), f32)
            for jj in range(width):
                dw_ref[jj:jj + 1, :] = jnp.sum(acc[jj * SUBLANES:(jj + 1) * SUBLANES, :], axis=0, keepdims=True)

    if qk_heads is None:
        dy_specs = [pl.BlockSpec((1, s, LANES), lambda j, bi: (bi, 0, j))]
    else:
        hh = qk_heads
        dy_specs = [pl.BlockSpec((1, s, LANES), lambda j, bi: (bi * hh + jnp.minimum(j, hh - 1), 0, 0)),
                    pl.BlockSpec((1, s, LANES), lambda j, bi: (bi * hh + jnp.clip(j - hh, 0, hh - 1), 0, 0)),
                    pl.BlockSpec((1, s, LANES), lambda j, bi: (bi * hh + jnp.clip(j - 2 * hh, 0, hh - 1), 0, 0))]
    return _pc(body, name=name, grid=(cn // LANES, bl),
               in_specs=[pl.BlockSpec((1, s, LANES), lambda j, bi: (bi, 0, j)),
                         pl.BlockSpec((width, LANES), lambda j, bi: (0, j))] + dy_specs,
               out_specs=[pl.BlockSpec((1, s, LANES), lambda j, bi: (bi, 0, j)),
                          pl.BlockSpec((wp, LANES), lambda j, bi: (0, j)),
                          pl.BlockSpec((SUBLANES, LANES), lambda j, bi: (0, j))],
               out_shape=[jax.ShapeDtypeStruct((bl, s, cn), f32), jax.ShapeDtypeStruct((wp, cn), f32),
                          jax.ShapeDtypeStruct((SUBLANES, cn), f32)],
               scratch=[pltpu.VMEM((s + CONV_PAD, LANES), f32), pltpu.VMEM((s + CONV_PAD, LANES), f32),
                        pltpu.VMEM((width * SUBLANES, LANES), f32)],
               sem=("parallel", "arbitrary"))(x, w, *dys)


def _group_rows(s):
    for rows in (256, 128):
        if s % rows == 0:
            return rows
    return CHUNK


def _group_masks(rows):
    r = lax.broadcasted_iota(jnp.int32, (rows, rows), 0)
    c = lax.broadcasted_iota(jnp.int32, (rows, rows), 1)
    same_chunk = (r >> 6) == (c >> 6)
    return r, c, same_chunk


def _decay(gc_col, gc_row, causal):
    return jnp.exp(jnp.where(causal, gc_col - gc_row, NEG))


def _inv_unit_lower(a, r, c):
    eye = (r == c).astype(f32)
    same16 = (r >> 4) == (c >> 4)
    ad = jnp.where(same16, a, 0.0)
    ao = a - ad
    x = ad
    td = eye - x
    for _ in range(3):
        x = _dot(x, x)
        td = td + _dot(td, x)
    b = _dot(td, ao)
    b2 = _dot(b, b)
    return _dot(eye - b + b2 - _dot(b, b2), td)


def _lane_cumsum(x, y, reverse, name):
    rr = x.shape[0]

    def body(x_ref, y_ref, o_ref):
        i = lax.broadcasted_iota(jnp.int32, (CHUNK, CHUNK), 0)
        j = lax.broadcasted_iota(jnp.int32, (CHUNK, CHUNK), 1)
        tri = ((i >= j) if reverse else (i <= j)).astype(f32)
        o_ref[...] = jnp.dot(x_ref[...] + y_ref[...], tri, precision=HI, preferred_element_type=f32)

    spec = pl.BlockSpec((rr, CHUNK), lambda: (0, 0))
    return pl.pallas_call(body, name=name, in_specs=[spec, spec], out_specs=spec,
                          out_shape=jax.ShapeDtypeStruct((rr, CHUNK), f32))(x, y)


def _gdn_specs(n_heads, nblk, rows, hp=1, rev=False):
    def blk(n):
        return nblk - 1 - n if rev else n

    def qkv(off):
        return pl.BlockSpec((rows, hp * LANES), lambda g, n: (
            lax.div(g * hp, n_heads) * nblk + blk(n), lax.div(off * n_heads + lax.rem(g * hp, n_heads), hp)))

    def per_head(last, mult=1):
        return pl.BlockSpec((hp, rows * mult, last), lambda g, n: (g, blk(n), 0))

    def row_vec():
        return pl.BlockSpec((hp, 1, 1, rows), lambda g, n: (g, blk(n), 0, 0))

    return qkv, per_head, row_vec


def _gdn_prep(qkv, cols, grow, bl, s, n_heads):
    rows = _group_rows(s)
    nblk = s // rows
    bh_n = bl * n_heads
    qkv_spec, ph, rv = _gdn_specs(n_heads, nblk, rows)

    def body(k_ref, v_ref, cols_ref, grow_ref, u_ref, w_ref, t_ref):
        r, c, same = _group_masks(rows)
        k = k_ref[...]
        gc = cols_ref[0, :, 0:1]
        beta = cols_ref[0, :, 1:2]
        dmat = _decay(gc, grow_ref[0, 0], jnp.logical_and(same, r >= c))
        kb = k * beta
        a = jnp.where(r > c, _dot_nt(kb, k) * dmat, 0.0)
        tm = _inv_unit_lower(a, r, c)
        u_ref[0] = _dot(tm, v_ref[...] * beta)
        w_ref[0] = _dot(tm, kb * jnp.exp(gc))
        t_ref[0] = tm.astype(bf16)

    return _pc(body, name="gdn_prep", grid=(bh_n, nblk),
               in_specs=[qkv_spec(1), qkv_spec(2), ph(2), rv()],
               out_specs=[ph(LANES), ph(LANES), ph(rows)],
               out_shape=[jax.ShapeDtypeStruct((bh_n, s, LANES), f32), jax.ShapeDtypeStruct((bh_n, s, LANES), f32),
                          jax.ShapeDtypeStruct((bh_n, s, rows), bf16)],
               sem=("parallel", "parallel"))(qkv, qkv, cols, grow)


def _gdn_scan(qkv, u, w, cols, grow, bl, s, n_heads):
    rows = _group_rows(s)
    g_chunks = rows // CHUNK
    nblk = s // rows
    bh_n = bl * n_heads
    d = n_heads * LANES
    hp = HEADS_PER_STEP if n_heads % HEADS_PER_STEP == 0 else 1
    qkv_spec, ph, rv = _gdn_specs(n_heads, nblk, rows, hp)

    def body(q_ref, k_ref, u_ref, w_ref, cols_ref, grow_ref, o_ref, vn_ref, ss_ref, s_scr):
        _acc_init(pl.program_id(1), s_scr)
        r, c, same = _group_masks(rows)
        causal = jnp.logical_and(same, r >= c)
        qs, ks, gcs, qgs, ps = [], [], [], [], []
        for h in range(hp):
            hs = slice(h * LANES, (h + 1) * LANES)
            q, k, gc = q_ref[:, hs], k_ref[:, hs], cols_ref[h, :, 0:1]
            qs.append(q)
            ks.append(k)
            gcs.append(gc)
            qgs.append(q * jnp.exp(gc))
            ps.append(_dot_nt(q, k) * _decay(gc, grow_ref[h, 0], causal))
        st = [s_scr[h] for h in range(hp)]
        o_state = [[] for _ in range(hp)]
        vns = [[] for _ in range(hp)]
        for ci in range(g_chunks):
            sl = slice(ci * CHUNK, (ci + 1) * CHUNK)
            for h in range(hp):
                ss_ref[h, ci * LANES:(ci + 1) * LANES, :] = st[h]
                vn = u_ref[h, sl, :] - _dot(w_ref[h, sl, :], st[h])
                vns[h].append(vn)
                o_state[h].append(_dot(qgs[h][sl], st[h]))
                gc = gcs[h][sl]
                g_last = gc[CHUNK - 1:CHUNK, :]
                st[h] = jnp.exp(g_last) * st[h] + _dot_tn(ks[h][sl] * jnp.exp(g_last - gc), vn)
        for h in range(hp):
            s_scr[h] = st[h]
            vn_all = jnp.concatenate(vns[h], axis=0)
            vn_ref[h] = vn_all
            o_ref[:, h * LANES:(h + 1) * LANES] = jnp.concatenate(o_state[h], axis=0) + _dot(ps[h], vn_all)

    return _pc(body, name="gdn_scan", grid=(bh_n // hp, nblk),
               in_specs=[qkv_spec(0), qkv_spec(1), ph(LANES), ph(LANES), ph(2), rv()],
               out_specs=[qkv_spec(0), ph(LANES), ph(LANES, mult=LANES // CHUNK)],
               out_shape=[jax.ShapeDtypeStruct((bl * s, d), f32), jax.ShapeDtypeStruct((bh_n, s, LANES), f32),
                          jax.ShapeDtypeStruct((bh_n, (s // CHUNK) * LANES, LANES), f32)],
               scratch=[pltpu.VMEM((hp, LANES, LANES), f32)],
               sem=("parallel", "arbitrary"))(qkv, qkv, u, w, cols, grow)


def _gdn_scan_bwd(do, qkv, w, vn, cols, grow, ss, bl, s, n_heads):
    rows = _group_rows(s)
    g_chunks = rows // CHUNK
    nblk = s // rows
    bh_n = bl * n_heads
    hp = HEADS_PER_STEP if n_heads % HEADS_PER_STEP == 0 else 1
    qkv_spec, ph, rv = _gdn_specs(n_heads, nblk, rows, hp, rev=True)

    def body(do_ref, q_ref, k_ref, w_ref, vn_ref, cols_ref, grow_ref, ss_ref,
             du_ref, dw_ref, dq_ref, dk_ref, dcol_ref, drow_ref, ds_scr):
        _acc_init(pl.program_id(1), ds_scr)
        r, c, same = _group_masks(rows)
        causal = jnp.logical_and(same, r >= c)
        last_row = lax.broadcasted_iota(jnp.int32, (CHUNK, 1), 0) == CHUNK - 1
        pre = []
        for h in range(hp):
            hs = slice(h * LANES, (h + 1) * LANES)
            do, q, k = do_ref[:, hs], q_ref[:, hs], k_ref[:, hs]
            vn = vn_ref[h]
            gc = cols_ref[h, :, 0:1]
            dmat = _decay(gc, grow_ref[h, 0], causal)
            gam = jnp.exp(gc)
            qk = _dot_nt(q, k)
            dpd = _dot_nt(do, vn) * dmat
            ep = dpd * qk
            drow_ref[h, 0] = -jnp.sum(ep, axis=0, keepdims=True)
            pre.append(dict(do=do, k=k, vn=vn, gc=gc, gam=gam, qg=q * gam,
                            dvn_intra=_dot_tn(qk * dmat, do), dq_intra=_dot(dpd, k), dk_intra=_dot_tn(dpd, q),
                            ep_rows=jnp.sum(ep, axis=-1, keepdims=True)))
        ds = [ds_scr[h] for h in range(hp)]
        for ci in reversed(range(g_chunks)):
            sl = slice(ci * CHUNK, (ci + 1) * CHUNK)
            for h in range(hp):
                pr = pre[h]
                st = ss_ref[h, ci * LANES:(ci + 1) * LANES, :]
                gc = pr["gc"][sl]
                g_last = gc[CHUNK - 1:CHUNK, :]
                gam_last = jnp.exp(g_last)
                kd_scale = jnp.exp(g_last - gc)
                kdec = pr["k"][sl] * kd_scale
                do_c = pr["do"][sl]
                qg_c = pr["qg"][sl]
                dvn = pr["dvn_intra"][sl] + _dot(kdec, ds[h])
                dkdec = _dot_nt(pr["vn"][sl], ds[h])
                du_ref[h, sl, :] = dvn
                dw_ref[h, sl, :] = -_dot_nt(dvn, st)
                dqg = _dot_nt(do_c, st)
                dq_ref[h, sl, :] = dqg * pr["gam"][sl] + pr["dq_intra"][sl]
                dk_ref[h, sl, :] = pr["dk_intra"][sl] + dkdec * kd_scale
                kd_rows = jnp.sum(dkdec * kdec, axis=-1, keepdims=True)
                extra = jnp.sum(kd_rows) + gam_last * jnp.sum(st * ds[h])
                dcol_ref[h, sl, :] = (jnp.sum(dqg * qg_c, axis=-1, keepdims=True) + pr["ep_rows"][sl] - kd_rows
                                      + jnp.where(last_row, extra, 0.0))
                ds[h] = gam_last * ds[h] + _dot_tn(qg_c, do_c) - _dot_tn(w_ref[h, sl, :], dvn)
        for h in range(hp):
            ds_scr[h] = ds[h]

    return _pc(body, name="gdn_scan_bwd", grid=(bh_n // hp, nblk),
               in_specs=[qkv_spec(0), qkv_spec(0), qkv_spec(1), ph(LANES), ph(LANES), ph(2), rv(),
                         ph(LANES, mult=LANES // CHUNK)],
               out_specs=[ph(LANES), ph(LANES), ph(LANES), ph(LANES), ph(1), rv()],
               out_shape=[jax.ShapeDtypeStruct((bh_n, s, LANES), f32)] * 4
               + [jax.ShapeDtypeStruct((bh_n, s, 1), f32), jax.ShapeDtypeStruct((bh_n, nblk, 1, rows), f32)],
               scratch=[pltpu.VMEM((hp, LANES, LANES), f32)],
               sem=("parallel", "arbitrary"))(do, qkv, qkv, w, vn, cols, grow, ss)


def _gdn_prep_bwd(qkv, cols, grow, tmat, du, dw, dk_scan, dcol_scan, drow_scan, bl, s, n_heads):
    rows = _group_rows(s)
    nblk = s // rows
    bh_n = bl * n_heads
    qkv_spec, ph, rv = _gdn_specs(n_heads, nblk, rows)

    def body(k_ref, v_ref, cols_ref, grow_ref, t_ref, du_ref, dw_ref, dks_ref, dcs_ref, drs_ref,
             dk_ref, dv_ref, dcols_ref, drow_ref):
        r, c, same = _group_masks(rows)
        k = k_ref[...]
        v = v_ref[...]
        gc = cols_ref[0, :, 0:1]
        beta = cols_ref[0, :, 1:2]
        tm = t_ref[0]
        du = du_ref[0]
        dw_c = dw_ref[0]
        dmat = _decay(gc, grow_ref[0, 0], jnp.logical_and(same, r >= c))
        gam = jnp.exp(gc)
        kb = k * beta
        kbg = kb * gam
        dt = jnp.where(same, _dot_nt(du, v * beta) + _dot_nt(dw_c, kbg), 0.0)
        dvb = _dot_tn(tm, du)
        dkbg = _dot_tn(tm, dw_c)
        dad = jnp.where(r > c, -_dot_tn(tm, _dot_nt(dt, tm)), 0.0) * dmat
        kk = _dot_nt(kb, k)
        dkb = dkbg * gam + _dot(dad, k)
        dk_ref[0] = dks_ref[0] + _dot_tn(dad, kb) + dkb * beta
        dv_ref[0] = dvb * beta
        ea = dad * kk
        dcols_ref[0, :, 0:1] = (dcs_ref[0] + jnp.sum(dkbg * kbg, axis=-1, keepdims=True)
                                + jnp.sum(ea, axis=-1, keepdims=True))
        dcols_ref[0, :, 1:2] = jnp.sum(dvb * v, axis=-1, keepdims=True) + jnp.sum(dkb * k, axis=-1, keepdims=True)
        drow_ref[0, 0] = drs_ref[0, 0] - jnp.sum(ea, axis=0, keepdims=True)

    return _pc(body, name="gdn_prep_bwd", grid=(bh_n, nblk),
               in_specs=[qkv_spec(1), qkv_spec(2), ph(2), rv(), ph(rows), ph(LANES), ph(LANES), ph(LANES), ph(1), rv()],
               out_specs=[ph(LANES), ph(LANES), ph(2), rv()],
               out_shape=[jax.ShapeDtypeStruct((bh_n, s, LANES), f32), jax.ShapeDtypeStruct((bh_n, s, LANES), f32),
                          jax.ShapeDtypeStruct((bh_n, s, 2), f32), jax.ShapeDtypeStruct((bh_n, nblk, 1, rows), f32)],
               sem=("parallel", "parallel"))(qkv, qkv, cols, grow, tmat, du, dw, dk_scan, dcol_scan, drow_scan)


def _row(v):
    return v.reshape(1, -1).astype(f32)


def _pad_lanes(v):
    v = v.reshape(1, -1).astype(f32)
    return jnp.pad(v, ((0, 0), (0, LANES - v.shape[1])))


def _local_step(x, tgt, p):
    bl, s, d = x.shape
    t = bl * s
    n_heads = p["gdn_a_log"].shape[-1]
    assert d == n_heads * LANES and s % CHUNK == 0
    x2 = x.reshape(t, d)
    tgt2 = tgt.reshape(t, d)
    gr = {}

    n0, ub, gl = _pw1_glu(x2, _row(p["norm_mix_g"][0]), p["cv_w_pw1"], _row(p["cv_b_pw1"]))
    dc = _dwconv_fwd(gl.reshape(bl, s, d), p["cv_w_dw"], _row(p["cv_b_dw"]), "dwconv_fwd").reshape(t, d)
    sb, h1 = _ln_silu_mm_res(dc, _row(p["cv_ln_g"]), _row(p["cv_ln_b"]), p["cv_w_pw2"], _row(p["cv_b_pw2"]), x2)
    n1, f0, r0, h2 = _mlp_fwd(h1, _row(p["norm_ffn_g"][0]), p["mlp_w1"][0], p["mlp_w2"][0], "mlp_fwd0")

    w_in = p["gdn_w_in"]
    w_main = w_in[:, :4 * d]
    w_ab = jnp.pad(w_in[:, 4 * d:], ((0, 0), (0, LANES - 2 * n_heads)))
    a_log_pad = _pad_lanes(p["gdn_a_log"])
    dt_pad = _pad_lanes(p["gdn_dt_bias"])
    n2, qkv_pre, z, ab, gbeta = _gdn_in(h2, _row(p["norm_mix_g"][1]), w_main, w_ab, a_log_pad, dt_pad, n_heads)
    zero_bias = jnp.zeros((1, 3 * d), f32)
    qkv = _dwconv_fwd(qkv_pre.reshape(bl, s, 3 * d), p["gdn_conv_w"], zero_bias, "sconv_fwd", qk_heads=n_heads).reshape(t, 3 * d)
    bh_n, rows = bl * n_heads, _group_rows(s)
    gates = gbeta[:, :2 * n_heads].reshape(bl, s, 2, n_heads).transpose(2, 0, 3, 1).reshape(2, bh_n, s)
    g_lanes = gates[0].reshape(bh_n * (s // CHUNK), CHUNK)
    gc_lanes = _lane_cumsum(g_lanes, jnp.zeros_like(g_lanes), False, "gdn_gate_cumsum")
    grow = gc_lanes.reshape(bh_n, s // rows, 1, rows)
    cols = jnp.stack([gc_lanes.reshape(bh_n, s), gates[1]], axis=-1)
    u, w, tmat = _gdn_prep(qkv, cols, grow, bl, s, n_heads)
    o, vn, ss = _gdn_scan(qkv, u, w, cols, grow, bl, s, n_heads)
    onb, h3 = _gated_norm_mm_res(o, z, _row(p["gdn_norm_g"]), p["gdn_w_out"], h2, n_heads)
    n3, f1, r1, h4 = _mlp_fwd(h3, _row(p["norm_ffn_g"][1]), p["mlp_w1"][1], p["mlp_w2"][1], "mlp_fwd1")
    loss_acc, dh4, dgf = _loss_head(h4, _row(p["final_norm_g"]), tgt2)
    loss = loss_acc[0, 0]
    gr["final_norm_g"] = dgf[0]

    df1, dh3, dg_ffn1, _ = _mlp_bwd(dh4, h3, _row(p["norm_ffn_g"][1]), f1, p["mlp_w1"][1], p["mlp_w2"][1], "mlp_bwd1")
    dw2_1 = _mm_tn(r1, dh4, "dw_mlp2_1")
    dw1_1 = _mm_tn(n3, df1, "dw_mlp1_1")

    gr["gdn_w_out"] = _mm_tn(onb, dh3, "dw_gdn_out")
    don = _mm_nt([(dh3, p["gdn_w_out"])], "dx_gdn_out")
    do, dz, dng = _gated_norm_bwd(don, o, z, _row(p["gdn_norm_g"]), n_heads)
    gr["gdn_norm_g"] = dng[0]
    du, dw_, dq, dk_scan, dcol_scan, drow_scan = _gdn_scan_bwd(do, qkv, w, vn, cols, grow, ss, bl, s, n_heads)
    dk, dv, dcols, drow = _gdn_prep_bwd(qkv, cols, grow, tmat, du, dw_, dk_scan, dcol_scan, drow_scan, bl, s, n_heads)
    dg_lanes = _lane_cumsum(dcols[..., 0].reshape(g_lanes.shape), drow.reshape(g_lanes.shape), True, "gdn_gate_cumsum_bwd")
    dgb2 = jnp.stack([dg_lanes.reshape(bh_n, s), dcols[..., 1]], axis=-1)
    dqkv_pre, dconv_w, _ = _dwconv_bwd(qkv_pre.reshape(bl, s, 3 * d), [dq, dk, dv], p["gdn_conv_w"], "sconv_bwd", qk_heads=n_heads)
    gr["gdn_conv_w"] = dconv_w[:p["gdn_conv_w"].shape[0]]
    dgb = dgb2.reshape(bl, n_heads, s, 2).transpose(0, 2, 3, 1).reshape(t, 2 * n_heads)
    dgb = jnp.pad(dgb, ((0, 0), (0, LANES - 2 * n_heads)))
    dab, dal, ddt = _gates_bwd(dgb, ab, a_log_pad, dt_pad, n_heads)
    gr["gdn_a_log"] = dal[0, :n_heads]
    gr["gdn_dt_bias"] = ddt[0, :n_heads]
    dqkv2 = dqkv_pre.reshape(t, 3 * d)
    gr["gdn_w_in"] = jnp.concatenate(
        [_mm_tn(n2, dqkv2, "dw_gdn_in_qkv"), _mm_tn(n2, dz, "dw_gdn_in_z"), _mm_tn(n2, dab, "dw_gdn_in_ab")[:, :2 * n_heads]], axis=1)
    dn2 = _mm_nt([(dqkv2, w_main[:, :3 * d]), (dz, w_main[:, 3 * d:]), (dab, w_ab)], "dx_gdn_in")
    dh2, dg_mix1 = _rms_bwd_res(dn2, h2, _row(p["norm_mix_g"][1]), dh3, "rms_bwd_gdn")

    df0, dh1, dg_ffn0, cs_h1 = _mlp_bwd(dh2, h1, _row(p["norm_ffn_g"][0]), f0, p["mlp_w1"][0], p["mlp_w2"][0], "mlp_bwd0")
    dw2_0 = _mm_tn(r0, dh2, "dw_mlp2_0")
    dw1_0 = _mm_tn(n1, df0, "dw_mlp1_0")
    gr["mlp_w1"] = jnp.stack([dw1_0, dw1_1])
    gr["mlp_w2"] = jnp.stack([dw2_0, dw2_1])
    gr["norm_ffn_g"] = jnp.stack([dg_ffn0[0], dg_ffn1[0]])

    gr["cv_b_pw2"] = cs_h1[0]
    gr["cv_w_pw2"] = _mm_tn(sb, dh1, "dw_pw2")
    ds = _mm_nt([(dh1, p["cv_w_pw2"])], "dx_pw2")
    ddc, dlng, dlnb, cs_dc = _ln_silu_bwd(ds, dc, _row(p["cv_ln_g"]), _row(p["cv_ln_b"]))
    gr["cv_ln_g"] = dlng[0]
    gr["cv_ln_b"] = dlnb[0]
    gr["cv_b_dw"] = cs_dc[0]
    dgl, dw_dw, _ = _dwconv_bwd(gl.reshape(bl, s, d), [ddc.reshape(bl, s, d)], p["cv_w_dw"], "dwconv_bwd")
    gr["cv_w_dw"] = dw_dw[:p["cv_w_dw"].shape[0]]
    dub, cs_u = _glu_bwd(dgl.reshape(t, d), ub)
    gr["cv_b_pw1"] = cs_u[0]
    gr["cv_w_pw1"] = _mm_tn(n0, dub, "dw_pw1")
    dn0 = _mm_nt([(dub, p["cv_w_pw1"])], "dx_pw1")
    dx, dg_mix0 = _rms_bwd_res(dn0, x2, _row(p["norm_mix_g"][0]), dh1, "rms_bwd_conv")
    gr["norm_mix_g"] = jnp.stack([dg_mix0[0], dg_mix1[0]])
    return loss, dx.reshape(bl, s, d), gr


ANY = pl.BlockSpec(memory_space=pl.ANY)
MESH = pl.DeviceIdType.MESH


def _flip(v, bit):
    return 1 - v if bit else v


def _all_gather(shard):
    rr = shard.shape[0]

    def body(x_ref, out_ref, send_sems, recv_sems, local_sem):
        x, y, c = lax.axis_index("x"), lax.axis_index("y"), lax.axis_index("c")
        me, sibling = (x, y, c), (x, y, 1 - c)
        chips = [(1 - x, y), (x, 1 - y), (1 - x, 1 - y)]

        def slot(px, py, pc):
            return out_ref.at[4 * px + 2 * py + pc]

        def copy(k, block, to, src=None):
            return pltpu.make_async_remote_copy(
                src_ref=slot(*block) if src is None else src, dst_ref=slot(*block),
                send_sem=send_sems.at[k], recv_sem=recv_sems.at[k], device_id=to, device_id_type=MESH)

        mine = pltpu.make_async_copy(x_ref, slot(*me), local_sem)
        mine.start()
        first = [copy(0, me, sibling, src=x_ref)]
        first += [copy(1 + j, me, (*chip, c), src=x_ref) for j, chip in enumerate(chips)]
        for cp in first:
            cp.start()
        passed = [copy(4 + j, (*chip, c), sibling) for j, chip in enumerate(chips)]
        for j, chip in enumerate(chips):
            copy(1 + j, (*chip, c), me).wait_recv()
            passed[j].start()
        copy(0, sibling, me).wait_recv()
        for j, chip in enumerate(chips):
            copy(4 + j, (*chip, 1 - c), me).wait_recv()
        for cp in first + passed:
            cp.wait_send()
        mine.wait()

    return pl.pallas_call(
        body, name="weights_all_gather",
        out_shape=jax.ShapeDtypeStruct((N_DEV, rr, LANES), shard.dtype),
        in_specs=[ANY], out_specs=ANY,
        scratch_shapes=[pltpu.SemaphoreType.DMA((7,)), pltpu.SemaphoreType.DMA((7,)), pltpu.SemaphoreType.DMA(())],
        compiler_params=pltpu.CompilerParams(has_side_effects=True),
    )(shard)


def _all_to_all(send):
    def body(s_ref, r_ref, send_sems, recv_sems, local_sem):
        x, y, c = lax.axis_index("x"), lax.axis_index("y"), lax.axis_index("c")
        me = 4 * x + 2 * y + c
        mine = pltpu.make_async_copy(s_ref.at[me], r_ref.at[me], local_sem)
        mine.start()
        copies = []
        for k in range(1, N_DEV):
            px, py, pc = _flip(x, k & 4), _flip(y, k & 2), _flip(c, k & 1)
            peer = 4 * px + 2 * py + pc
            copies.append(pltpu.make_async_remote_copy(
                src_ref=s_ref.at[peer], dst_ref=r_ref.at[me],
                send_sem=send_sems.at[k - 1], recv_sem=recv_sems.at[k - 1],
                device_id=(px, py, pc), device_id_type=MESH))
        for cp in copies:
            cp.start()
        for cp in copies:
            cp.wait_recv()
        for cp in copies:
            cp.wait_send()
        mine.wait()

    return pl.pallas_call(
        body, name="grads_all_to_all",
        out_shape=jax.ShapeDtypeStruct(send.shape, send.dtype),
        in_specs=[ANY], out_specs=ANY,
        scratch_shapes=[pltpu.SemaphoreType.DMA((7,)), pltpu.SemaphoreType.DMA((7,)), pltpu.SemaphoreType.DMA(())],
        compiler_params=pltpu.CompilerParams(has_side_effects=True),
    )(send)


def _sum_devices(recv):
    _, rr, _ = recv.shape
    tr = _tile(rr, 512)

    def body(r_ref, o_ref):
        acc = r_ref[0]
        for i in range(1, N_DEV):
            acc = acc + r_ref[i]
        o_ref[...] = acc

    return _pc(body, name="grads_sum", grid=(rr // tr,),
               in_specs=[pl.BlockSpec((N_DEV, tr, LANES), lambda i: (0, i, 0))],
               out_specs=_rows(tr, LANES), out_shape=jax.ShapeDtypeStruct((rr, LANES), f32), sem=("parallel",))(recv)


def _adamw(w, g, m, v):
    rr = w.shape[0]
    tr = _tile(rr, 512)
    bc1 = 1.0 - ADAM_B1 ** ADAM_STEP
    bc2 = 1.0 - ADAM_B2 ** ADAM_STEP

    def body(w_ref, g_ref, m_ref, v_ref, d_ref, nm_ref, nv_ref):
        gv = g_ref[...]
        nm = ADAM_B1 * m_ref[...] + (1.0 - ADAM_B1) * gv
        nv = ADAM_B2 * v_ref[...] + (1.0 - ADAM_B2) * (gv * gv)
        nm_ref[...] = nm
        nv_ref[...] = nv
        d_ref[...] = -ADAM_LR * ((nm / bc1) / (jnp.sqrt(nv / bc2) + ADAM_EPS) + ADAM_WD * w_ref[...])

    spec = _rows(tr, LANES)
    return _pc(body, name="adamw", grid=(rr // tr,), in_specs=[spec] * 4, out_specs=[spec] * 3,
               out_shape=[jax.ShapeDtypeStruct((rr, LANES), f32)] * 3, sem=("parallel",))(w, g, m, v)


PACK_ROWS = 512
PART_ROWS = 16


def _pack(arrs, lead=()):
    nl = len(lead)
    parts, sizes = [], []
    for a in arrs:
        flat = a.reshape(lead + (-1,))
        n = flat.shape[-1]
        rows = -(-n // (LANES * PART_ROWS)) * PART_ROWS
        flat = jnp.pad(flat, [(0, 0)] * nl + [(0, rows * LANES - n)])
        parts.append(flat.reshape(lead + (rows, LANES)))
        sizes.append((rows, n))
    total = sum(r for r, _ in sizes)
    padded = -(-total // PACK_ROWS) * PACK_ROWS
    if padded > total:
        parts.append(jnp.zeros(lead + (padded - total, LANES), parts[0].dtype))
    return jnp.concatenate(parts, axis=nl), sizes


def _unpack(packed, sizes, shapes, lead=()):
    nl = len(lead)
    out, off = [], 0
    for (rows, n), shp in zip(sizes, shapes):
        piece = lax.slice_in_dim(packed, off, off + rows, axis=nl).reshape(lead + (rows * LANES,))
        out.append(lax.slice_in_dim(piece, 0, n, axis=nl).reshape(lead + tuple(shp)))
        off += rows
    return out


def _cols_to_blocks(a):
    n = a.shape[-1] // N_DEV
    a = a.reshape(a.shape[:-1] + (N_DEV, n))
    return jnp.moveaxis(a, -2, 0)


def _blocks_to_cols(a):
    a = jnp.moveaxis(a, 0, -2)
    return a.reshape(a.shape[:-2] + (a.shape[-2] * a.shape[-1],))


def _rows_to_blocks(a):
    k = a.shape[-2] // N_DEV
    a = a.reshape(a.shape[:-2] + (N_DEV, k, a.shape[-1]))
    return jnp.moveaxis(a, -3, 0)


def _blocks_to_rows(a):
    a = jnp.moveaxis(a, 0, -3)
    return a.reshape(a.shape[:-3] + (a.shape[-3] * a.shape[-2], a.shape[-1]))


COL_SHARDED = ("cv_w_pw1", "gdn_w_in", "mlp_w1")
ROW_SHARDED = ("cv_w_pw2", "gdn_w_out", "mlp_w2")
CONV_SHARDED = ("cv_w_dw", "gdn_conv_w")
REPLICATED = ("norm_mix_g", "norm_ffn_g", "final_norm_g", "cv_b_pw1", "cv_b_dw", "cv_ln_g", "cv_ln_b", "cv_b_pw2",
              "gdn_a_log", "gdn_dt_bias", "gdn_norm_g")
WEIGHTS = ("norm_mix_g", "norm_ffn_g", "final_norm_g", "cv_w_pw1", "cv_b_pw1", "cv_w_dw", "cv_b_dw", "cv_ln_g",
           "cv_ln_b", "cv_w_pw2", "cv_b_pw2", "gdn_w_in", "gdn_conv_w", "gdn_a_log", "gdn_dt_bias", "gdn_norm_g",
           "gdn_w_out", "mlp_w1", "mlp_w2")
MATMUL_SHARDED = COL_SHARDED + ROW_SHARDED


def _squeeze_layer(name, a):
    if name in ("norm_mix_g", "norm_ffn_g", "final_norm_g", "mlp_w1", "mlp_w2"):
        return a
    return a[0]


def _gather_weights(shards):
    big = [shards[n].astype(bf16) for n in MATMUL_SHARDED]
    conv = [lax.bitcast_convert_type(shards[n], bf16) for n in CONV_SHARDED]
    packed, sizes = _pack(big + conv)
    gathered = _all_gather(packed)
    pieces = _unpack(gathered, sizes, [a.shape for a in big + conv], lead=(N_DEV,))
    full = {}
    for n, pc in zip(MATMUL_SHARDED + CONV_SHARDED, pieces):
        if n in COL_SHARDED:
            full[n] = _blocks_to_cols(pc)
        elif n in ROW_SHARDED:
            full[n] = _blocks_to_rows(pc)
        else:
            full[n] = _blocks_to_cols(lax.bitcast_convert_type(pc, f32))
    return full


def kernel(x, norm_mix_g, norm_ffn_g, final_norm_g, cv_w_pw1, cv_b_pw1, cv_w_dw, cv_b_dw, cv_ln_g, cv_ln_b, cv_w_pw2, cv_b_pw2, gdn_w_in, gdn_conv_w, gdn_a_log, gdn_dt_bias, gdn_norm_g, gdn_w_out, mlp_w1, mlp_w2, loss_target, m_norm_mix_g, m_norm_ffn_g, m_final_norm_g, m_cv_w_pw1, m_cv_b_pw1, m_cv_w_dw, m_cv_b_dw, m_cv_ln_g, m_cv_ln_b, m_cv_w_pw2, m_cv_b_pw2, m_gdn_w_in, m_gdn_conv_w, m_gdn_a_log, m_gdn_dt_bias, m_gdn_norm_g, m_gdn_w_out, m_mlp_w1, m_mlp_w2, v_norm_mix_g, v_norm_ffn_g, v_final_norm_g, v_cv_w_pw1, v_cv_b_pw1, v_cv_w_dw, v_cv_b_dw, v_cv_ln_g, v_cv_ln_b, v_cv_w_pw2, v_cv_b_pw2, v_gdn_w_in, v_gdn_conv_w, v_gdn_a_log, v_gdn_dt_bias, v_gdn_norm_g, v_gdn_w_out, v_mlp_w1, v_mlp_w2):
    w_in = dict(zip(WEIGHTS, (norm_mix_g, norm_ffn_g, final_norm_g, cv_w_pw1, cv_b_pw1, cv_w_dw, cv_b_dw, cv_ln_g, cv_ln_b, cv_w_pw2, cv_b_pw2, gdn_w_in, gdn_conv_w, gdn_a_log, gdn_dt_bias, gdn_norm_g, gdn_w_out, mlp_w1, mlp_w2)))
    m_in = dict(zip(WEIGHTS, (m_norm_mix_g, m_norm_ffn_g, m_final_norm_g, m_cv_w_pw1, m_cv_b_pw1, m_cv_w_dw, m_cv_b_dw, m_cv_ln_g, m_cv_ln_b, m_cv_w_pw2, m_cv_b_pw2, m_gdn_w_in, m_gdn_conv_w, m_gdn_a_log, m_gdn_dt_bias, m_gdn_norm_g, m_gdn_w_out, m_mlp_w1, m_mlp_w2)))
    v_in = dict(zip(WEIGHTS, (v_norm_mix_g, v_norm_ffn_g, v_final_norm_g, v_cv_w_pw1, v_cv_b_pw1, v_cv_w_dw, v_cv_b_dw, v_cv_ln_g, v_cv_ln_b, v_cv_w_pw2, v_cv_b_pw2, v_gdn_w_in, v_gdn_conv_w, v_gdn_a_log, v_gdn_dt_bias, v_gdn_norm_g, v_gdn_w_out, v_mlp_w1, v_mlp_w2)))
    me = 4 * lax.axis_index("x") + 2 * lax.axis_index("y") + lax.axis_index("c")

    shards = {n: _squeeze_layer(n, w_in[n]) for n in WEIGHTS}
    params = {n: shards[n] for n in REPLICATED}
    params.update(_gather_weights(shards))

    loss_part, grad_x, gr = _local_step(x, loss_target, params)
    loss = lax.psum(loss_part, ("x", "y", "c"))

    small = REPLICATED + CONV_SHARDED
    parts = []
    for n in MATMUL_SHARDED:
        parts.append(_cols_to_blocks(gr[n]) if n in COL_SHARDED else _rows_to_blocks(gr[n]))
    for n in small:
        parts.append(jnp.broadcast_to(gr[n][None], (N_DEV,) + gr[n].shape))
    send, sizes = _pack(parts, lead=(N_DEV,))
    summed = _sum_devices(_all_to_all(send))
    shapes = [pt.shape[1:] for pt in parts]
    g_list = _unpack(summed, sizes, shapes)
    grads = dict(zip(MATMUL_SHARDED + small, g_list))
    for n in CONV_SHARDED:
        cn = shards[n].shape[-1]
        grads[n] = lax.dynamic_slice_in_dim(grads[n], me * cn, cn, axis=1)

    order = list(WEIGHTS)
    gs = [grads[n].reshape(shards[n].shape) for n in order]
    ws = [shards[n] for n in order]
    ms = [_squeeze_layer(n, m_in[n]) for n in order]
    vs = [_squeeze_layer(n, v_in[n]) for n in order]
    wp, psz = _pack(ws)
    gp, _ = _pack(gs)
    mp, _ = _pack(ms)
    vp, _ = _pack(vs)
    dp, nmp, nvp = _adamw(wp, gp, mp, vp)
    shp = [a.shape for a in ws]
    deltas = _unpack(dp, psz, shp)
    new_m = _unpack(nmp, psz, shp)
    new_v = _unpack(nvp, psz, shp)

    def out(n, a):
        return a.reshape(w_in[n].shape)

    outs = [loss, grad_x]
    for group in (gs, deltas, new_m, new_v):
        outs += [out(n, a) for n, a in zip(order, group)]
    return tuple(outs)
```

```python
import functools

import jax
import jax.numpy as jnp
from jax import lax
from jax.experimental import pallas as pl
from jax.experimental.pallas import tpu as pltpu

f32, bf16 = jnp.float32, jnp.bfloat16

NORM_EPS = 1e-6
L2_EPS = 1e-6
CHUNK = 64
LANES = 128
SUBLANES = 8
N_DEV = 8
VMEM_LIMIT = 56 * 1024 * 1024
CONV_PAD = 32
HEADS_PER_STEP = 2
NEG = -1e30

ADAM_LR, ADAM_B1, ADAM_B2, ADAM_EPS, ADAM_WD, ADAM_STEP = 0.001, 0.9, 0.999, 1e-08, 0.01, 10

NT = (((1,), (1,)), ((), ()))
TN = (((0,), (0,)), ((), ()))
HI = lax.Precision.HIGHEST


def _pc(body, *, name, grid, in_specs, out_specs, out_shape, scratch=(), sem=None):
    return pl.pallas_call(
        body, name=name, grid=grid, in_specs=in_specs, out_specs=out_specs, out_shape=out_shape,
        scratch_shapes=list(scratch),
        compiler_params=pltpu.CompilerParams(dimension_semantics=sem, vmem_limit_bytes=VMEM_LIMIT))


def _rows(tm, n):
    return pl.BlockSpec((tm, n), lambda i: (i, 0))


def _const(shape):
    return pl.BlockSpec(shape, lambda *_: (0,) * len(shape))


def _resident(shape):
    return pl.BlockSpec(shape, lambda *_: (0,) * len(shape), pipeline_mode=pl.Buffered(1))


def _tile(t, pref):
    return pref if t % pref == 0 else t


def _dot(a, b):
    return jnp.dot(a.astype(bf16), b.astype(bf16), preferred_element_type=f32)


def _dot_nt(a, b):
    return lax.dot_general(a.astype(bf16), b.astype(bf16), NT, preferred_element_type=f32)


def _dot_tn(a, b):
    return lax.dot_general(a.astype(bf16), b.astype(bf16), TN, preferred_element_type=f32)


def _sigmoid(x):
    return 1.0 / (1.0 + jnp.exp(-x))


def _silu_grad(x):
    s = _sigmoid(x)
    return s * (1.0 + x * (1.0 - s))


def _rms(x, g):
    rstd = lax.rsqrt(jnp.mean(x * x, axis=-1, keepdims=True) + NORM_EPS)
    xh = x * rstd
    return xh * g, xh, rstd


def _rms_bwd(dn, xh, rstd, g):
    dxh = dn * g
    return rstd * (dxh - xh * jnp.mean(dxh * xh, axis=-1, keepdims=True))


def _acc_init(step, *refs):
    @pl.when(step == 0)
    def _():
        for r in refs:
            r[...] = jnp.zeros(r.shape, r.dtype)


def _acc_rows(ref, val):
    ref[0:1, :] += jnp.sum(val, axis=0, keepdims=True)


def _pw1_glu(x, g, w, b):
    t, d = x.shape
    tm = _tile(t, 256)

    nb_w, _, bn = w.shape
    half = nb_w // 2

    def body(x_ref, g_ref, w_ref, b_ref, n_ref, u_ref, gl_ref):
        n, _, _ = _rms(x_ref[...], g_ref[...])
        nb = n.astype(bf16)
        n_ref[...] = nb
        us = []
        for j in range(nb_w):
            cs = slice(j * bn, (j + 1) * bn)
            uj = jnp.dot(nb, w_ref[j], preferred_element_type=f32) + b_ref[:, cs]
            u_ref[:, cs] = uj.astype(bf16)
            us.append(uj)
        for j in range(half):
            gl_ref[:, j * bn:(j + 1) * bn] = us[j] * _sigmoid(us[j + half])

    return _pc(body, name="pw1_glu", grid=(t // tm,),
               in_specs=[_rows(tm, d), _const((1, d)), _resident((nb_w, d, bn)), _const((1, 2 * d))],
               out_specs=[_rows(tm, d), _rows(tm, 2 * d), _rows(tm, d)],
               out_shape=[jax.ShapeDtypeStruct((t, d), bf16), jax.ShapeDtypeStruct((t, 2 * d), bf16),
                          jax.ShapeDtypeStruct((t, d), f32)],
               sem=("parallel",))(x, g, w, b)


def _ln_silu_mm_res(dc, ln_g, ln_b, w, b, res):
    t, d = dc.shape
    tm = _tile(t, 256)

    def body(x_ref, g_ref, bb_ref, w_ref, b_ref, r_ref, s_ref, o_ref):
        x = x_ref[...]
        xc = x - jnp.mean(x, axis=-1, keepdims=True)
        rstd = lax.rsqrt(jnp.mean(xc * xc, axis=-1, keepdims=True) + NORM_EPS)
        ln = xc * rstd * g_ref[...] + bb_ref[...]
        sb = (ln * _sigmoid(ln)).astype(bf16)
        s_ref[...] = sb
        o_ref[...] = r_ref[...] + jnp.dot(sb, w_ref[...], preferred_element_type=f32) + b_ref[...]

    return _pc(body, name="ln_silu_pw2", grid=(t // tm,),
               in_specs=[_rows(tm, d), _const((1, d)), _const((1, d)), _resident((d, d)), _const((1, d)), _rows(tm, d)],
               out_specs=[_rows(tm, d), _rows(tm, d)],
               out_shape=[jax.ShapeDtypeStruct((t, d), bf16), jax.ShapeDtypeStruct((t, d), f32)],
               sem=("parallel",))(dc, ln_g, ln_b, w, b, res)


def _layer_blocks(wg, layer):
    nb_w, _, rr, cc = wg.shape
    return pl.BlockSpec((nb_w, None, rr, cc), lambda *_: (0, layer, 0, 0), pipeline_mode=pl.Buffered(1))


def _mlp_fwd(h, g, w1g, w2g, layer, name):
    t, d = h.shape
    nb_w, _, _, bn = w1g.shape
    ff = nb_w * bn
    tm = _tile(t, 256)

    def body(h_ref, g_ref, w1_ref, w2_ref, n_ref, f_ref, r_ref, o_ref):
        hv = h_ref[...]
        n, _, _ = _rms(hv, g_ref[...])
        nb = n.astype(bf16)
        n_ref[...] = nb
        acc = hv
        for j in range(nb_w):
            cs = slice(j * bn, (j + 1) * bn)
            f = jnp.dot(nb, w1_ref[j], preferred_element_type=f32)
            f_ref[:, cs] = f.astype(bf16)
            rb = jnp.square(jnp.maximum(f, 0.0)).astype(bf16)
            r_ref[:, cs] = rb
            acc = acc + jnp.dot(rb, w2_ref[j], preferred_element_type=f32)
        o_ref[...] = acc

    return _pc(body, name=name, grid=(t // tm,),
               in_specs=[_rows(tm, d), _const((1, d)), _layer_blocks(w1g, layer), _layer_blocks(w2g, layer)],
               out_specs=[_rows(tm, d), _rows(tm, ff), _rows(tm, ff), _rows(tm, d)],
               out_shape=[jax.ShapeDtypeStruct((t, d), bf16), jax.ShapeDtypeStruct((t, ff), bf16),
                          jax.ShapeDtypeStruct((t, ff), bf16), jax.ShapeDtypeStruct((t, d), f32)],
               sem=("parallel",))(h, g, w1g, w2g)


def _softplus(x):
    return jnp.maximum(x, 0.0) + jnp.log(1.0 + jnp.exp(-jnp.abs(x)))


def _gdn_in(h, g, w_main, w_ab, a_log_pad, dt_pad, n_heads):
    t, d = h.shape
    tm = _tile(t, 256)

    def body(h_ref, g_ref, wm_ref, wab_ref, al_ref, dt_ref, n_ref, qkv_ref, z_ref, ab_ref, gb_ref):
        n, _, _ = _rms(h_ref[...], g_ref[...])
        nb = n.astype(bf16)
        n_ref[...] = nb
        p = jnp.dot(nb, wm_ref[...], preferred_element_type=f32)
        qkv_ref[...] = p[:, :3 * d]
        z_ref[...] = p[:, 3 * d:]
        ab = jnp.dot(nb, wab_ref[...], preferred_element_type=f32)
        ab_ref[...] = ab
        lane = lax.broadcasted_iota(jnp.int32, ab.shape, 1)
        decay = -jnp.exp(al_ref[...]) * _softplus(ab + dt_ref[...])
        gb_ref[...] = jnp.where(lane < n_heads, decay, jnp.where(lane < 2 * n_heads, _sigmoid(ab), 0.0))

    return _pc(body, name="gdn_in", grid=(t // tm,),
               in_specs=[_rows(tm, d), _const((1, d)), _resident((d, 4 * d)), _resident((d, LANES)),
                         _const((1, LANES)), _const((1, LANES))],
               out_specs=[_rows(tm, d), _rows(tm, 3 * d), _rows(tm, d), _rows(tm, LANES), _rows(tm, LANES)],
               out_shape=[jax.ShapeDtypeStruct((t, d), bf16), jax.ShapeDtypeStruct((t, 3 * d), f32),
                          jax.ShapeDtypeStruct((t, d), f32), jax.ShapeDtypeStruct((t, LANES), f32),
                          jax.ShapeDtypeStruct((t, LANES), f32)],
               sem=("parallel",))(h, g, w_main, w_ab, a_log_pad, dt_pad)


def _gated_norm_mm_res(o, z, ng, w, res, n_heads):
    t, d = o.shape
    tm = _tile(t, 256)

    def body(o_ref, z_ref, ng_ref, w_ref, r_ref, on_ref, out_ref):
        for hd in range(n_heads):
            sl = slice(hd * LANES, (hd + 1) * LANES)
            rn, _, _ = _rms(o_ref[:, sl], ng_ref[...])
            zz = z_ref[:, sl]
            on_ref[:, sl] = (rn * (zz * _sigmoid(zz))).astype(bf16)
        out_ref[...] = r_ref[...] + jnp.dot(on_ref[...], w_ref[...], preferred_element_type=f32)

    return _pc(body, name="gated_norm_wout", grid=(t // tm,),
               in_specs=[_rows(tm, d), _rows(tm, d), _const((1, LANES)), _resident((d, d)), _rows(tm, d)],
               out_specs=[_rows(tm, d), _rows(tm, d)],
               out_shape=[jax.ShapeDtypeStruct((t, d), bf16), jax.ShapeDtypeStruct((t, d), f32)],
               sem=("parallel",))(o, z, ng, w, res)


def _loss_head(h, g, tgt):
    t, d = h.shape
    tm = _tile(t, 256)

    def body(h_ref, g_ref, t_ref, loss_ref, dh_ref, dg_ref):
        _acc_init(pl.program_id(0), loss_ref, dg_ref)
        gv = g_ref[...]
        y, xh, rstd = _rms(h_ref[...], gv)
        e = y - t_ref[...]
        loss_ref[...] += 0.5 * jnp.sum(jnp.mean(e * e, axis=-1, keepdims=True))
        dy = e * (1.0 / d)
        _acc_rows(dg_ref, dy * xh)
        dh_ref[...] = _rms_bwd(dy, xh, rstd, gv)

    return _pc(body, name="loss_head", grid=(t // tm,),
               in_specs=[_rows(tm, d), _const((1, d)), _rows(tm, d)],
               out_specs=[_const((SUBLANES, LANES)), _rows(tm, d), _const((SUBLANES, d))],
               out_shape=[jax.ShapeDtypeStruct((SUBLANES, LANES), f32), jax.ShapeDtypeStruct((t, d), f32),
                          jax.ShapeDtypeStruct((SUBLANES, d), f32)],
               sem=("arbitrary",))(h, g, tgt)


def _rms_bwd_res(dn, h, g, dres, name):
    t, d = h.shape
    tm = _tile(t, 256)

    def body(dn_ref, h_ref, g_ref, dr_ref, dh_ref, dg_ref):
        _acc_init(pl.program_id(0), dg_ref)
        gv = g_ref[...]
        _, xh, rstd = _rms(h_ref[...], gv)
        dn = dn_ref[...]
        _acc_rows(dg_ref, dn * xh)
        dh_ref[...] = dr_ref[...] + _rms_bwd(dn, xh, rstd, gv)

    return _pc(body, name=name, grid=(t // tm,),
               in_specs=[_rows(tm, d), _rows(tm, d), _const((1, d)), _rows(tm, d)],
               out_specs=[_rows(tm, d), _const((SUBLANES, d))],
               out_shape=[jax.ShapeDtypeStruct((t, d), f32), jax.ShapeDtypeStruct((SUBLANES, d), f32)],
               sem=("arbitrary",))(dn, h, g, dres)


def _mlp_bwd(dho, h, g, fb, w1g, w2g, layer, name):
    t, d = h.shape
    nb_w, _, _, bn = w1g.shape
    ff = nb_w * bn
    tm = _tile(t, 256)

    def body(do_ref, h_ref, g_ref, f_ref, w1_ref, w2_ref, df_ref, dh_ref, dg_ref, cs_ref):
        _acc_init(pl.program_id(0), dg_ref, cs_ref)
        do = do_ref[...]
        dob = do.astype(bf16)
        dn = None
        for j in range(nb_w):
            cs = slice(j * bn, (j + 1) * bn)
            dr = lax.dot_general(dob, w2_ref[j], NT, preferred_element_type=f32)
            dfb = (dr * (2.0 * jnp.maximum(f_ref[:, cs].astype(f32), 0.0))).astype(bf16)
            df_ref[:, cs] = dfb
            part = lax.dot_general(dfb, w1_ref[j], NT, preferred_element_type=f32)
            dn = part if dn is None else dn + part
        gv = g_ref[...]
        _, xh, rstd = _rms(h_ref[...], gv)
        _acc_rows(dg_ref, dn * xh)
        dh = do + _rms_bwd(dn, xh, rstd, gv)
        dh_ref[...] = dh
        _acc_rows(cs_ref, dh)

    return _pc(body, name=name, grid=(t // tm,),
               in_specs=[_rows(tm, d), _rows(tm, d), _const((1, d)), _rows(tm, ff),
                         _layer_blocks(w1g, layer), _layer_blocks(w2g, layer)],
               out_specs=[_rows(tm, ff), _rows(tm, d), _const((SUBLANES, d)), _const((SUBLANES, d))],
               out_shape=[jax.ShapeDtypeStruct((t, ff), bf16), jax.ShapeDtypeStruct((t, d), f32),
                          jax.ShapeDtypeStruct((SUBLANES, d), f32), jax.ShapeDtypeStruct((SUBLANES, d), f32)],
               sem=("arbitrary",))(dho, h, g, fb, w1g, w2g)


def _mm_nt(pairs, name):
    t = pairs[0][0].shape[0]
    w0 = pairs[0][1]
    k = (w0[0] if isinstance(w0, tuple) else w0).shape[0]
    tm = _tile(t, 256)
    npair = len(pairs)

    def body(*refs):
        o_ref = refs[2 * npair]
        acc = None
        for p in range(npair):
            part = lax.dot_general(refs[2 * p][...].astype(bf16), refs[2 * p + 1][...], NT, preferred_element_type=f32)
            acc = part if acc is None else acc + part
        o_ref[...] = acc

    in_specs, args = [], []
    for dy, w in pairs:
        nn = dy.shape[1]
        if isinstance(w, tuple):
            w, idx = w
            wspec = pl.BlockSpec((k, nn), lambda *_, idx=idx: (0, idx), pipeline_mode=pl.Buffered(1))
        else:
            wspec = _resident(w.shape)
        in_specs += [_rows(tm, nn), wspec]
        args += [dy, w]
    return _pc(body, name=name, grid=(t // tm,), in_specs=in_specs, out_specs=_rows(tm, k),
               out_shape=jax.ShapeDtypeStruct((t, k), f32), sem=("parallel",))(*args)


def _mm_nt_blocked(dy, wb, name):
    t = dy.shape[0]
    nb_w, k, bn = wb.shape
    tm = _tile(t, 256)

    def body(dy_ref, w_ref, o_ref):
        acc = None
        for j in range(nb_w):
            part = lax.dot_general(dy_ref[:, j * bn:(j + 1) * bn].astype(bf16), w_ref[j], NT, preferred_element_type=f32)
            acc = part if acc is None else acc + part
        o_ref[...] = acc

    return _pc(body, name=name, grid=(t // tm,), in_specs=[_rows(tm, nb_w * bn), _resident(wb.shape)],
               out_specs=_rows(tm, k), out_shape=jax.ShapeDtypeStruct((t, k), f32), sem=("parallel",))(dy, wb)


def _mm_tn_blocked(x, dy, name):
    t, k = x.shape
    bn = dy.shape[1] // N_DEV
    tm = _tile(t, 512)

    def body(x_ref, dy_ref, o_ref):
        _acc_init(pl.program_id(1), o_ref)
        o_ref[...] += lax.dot_general(x_ref[...].astype(bf16), dy_ref[...].astype(bf16), TN, preferred_element_type=f32)

    return _pc(body, name=name, grid=(N_DEV, t // tm),
               in_specs=[pl.BlockSpec((tm, k), lambda j, i: (i, 0)), pl.BlockSpec((tm, bn), lambda j, i: (i, j))],
               out_specs=pl.BlockSpec((None, k, bn), lambda j, i: (j, 0, 0)),
               out_shape=jax.ShapeDtypeStruct((N_DEV, k, bn), f32), sem=("parallel", "arbitrary"))(x, dy)


def _mm_tn(x, dy, name):
    t, k = x.shape
    n = dy.shape[1]
    tm = _tile(t, 512)
    cap = max(LANES, (2 * 1024 * 1024) // k)
    tn = n
    if n > cap:
        tn = max(c for c in range(LANES, cap + 1, LANES) if n % c == 0)

    def body(x_ref, dy_ref, o_ref):
        _acc_init(pl.program_id(1), o_ref)
        o_ref[...] += lax.dot_general(x_ref[...].astype(bf16), dy_ref[...].astype(bf16), TN, preferred_element_type=f32)

    return _pc(body, name=name, grid=(n // tn, t // tm),
               in_specs=[pl.BlockSpec((tm, k), lambda j, i: (i, 0)), pl.BlockSpec((tm, tn), lambda j, i: (i, j))],
               out_specs=pl.BlockSpec((k, tn), lambda j, i: (0, j)),
               out_shape=jax.ShapeDtypeStruct((k, n), f32), sem=("parallel", "arbitrary"))(x, dy)


def _gated_norm_bwd(don, o, z, ng, n_heads):
    t, d = o.shape
    tm = _tile(t, 256)

    def body(don_ref, o_ref, z_ref, ng_ref, do_ref, dz_ref, dng_ref):
        _acc_init(pl.program_id(0), dng_ref)
        gv = ng_ref[...]
        for hd in range(n_heads):
            sl = slice(hd * LANES, (hd + 1) * LANES)
            rn, xh, rstd = _rms(o_ref[:, sl], gv)
            zz = z_ref[:, sl]
            don = don_ref[:, sl]
            dz_ref[:, sl] = don * rn * _silu_grad(zz)
            drn = don * (zz * _sigmoid(zz))
            _acc_rows(dng_ref, drn * xh)
            do_ref[:, sl] = _rms_bwd(drn, xh, rstd, gv)

    return _pc(body, name="gated_norm_bwd", grid=(t // tm,),
               in_specs=[_rows(tm, d), _rows(tm, d), _rows(tm, d), _const((1, LANES))],
               out_specs=[_rows(tm, d), _rows(tm, d), _const((SUBLANES, LANES))],
               out_shape=[jax.ShapeDtypeStruct((t, d), f32), jax.ShapeDtypeStruct((t, d), f32),
                          jax.ShapeDtypeStruct((SUBLANES, LANES), f32)],
               sem=("arbitrary",))(don, o, z, ng)


def _gates_bwd(dgb, ab, a_log_pad, dt_pad, n_heads):
    t = ab.shape[0]
    tm = _tile(t, 256)

    def body(dgb_ref, ab_ref, al_ref, dt_ref, dab_ref, dal_ref, ddt_ref):
        _acc_init(pl.program_id(0), dal_ref, ddt_ref)
        ab = ab_ref[...]
        dgb = dgb_ref[...]
        lane = lax.broadcasted_iota(jnp.int32, ab.shape, 1)
        is_a = lane < n_heads
        is_b = jnp.logical_and(lane >= n_heads, lane < 2 * n_heads)
        xa = ab + dt_ref[...]
        neg_a = -jnp.exp(al_ref[...])
        dg_da = neg_a * _sigmoid(xa)
        beta = _sigmoid(ab)
        da = jnp.where(is_a, dgb * dg_da, 0.0)
        dab_ref[...] = da + jnp.where(is_b, dgb * beta * (1.0 - beta), 0.0)
        _acc_rows(dal_ref, jnp.where(is_a, dgb * neg_a * _softplus(xa), 0.0))
        _acc_rows(ddt_ref, da)

    return _pc(body, name="gates_bwd", grid=(t // tm,),
               in_specs=[_rows(tm, LANES), _rows(tm, LANES), _const((1, LANES)), _const((1, LANES))],
               out_specs=[_rows(tm, LANES), _const((SUBLANES, LANES)), _const((SUBLANES, LANES))],
               out_shape=[jax.ShapeDtypeStruct((t, LANES), f32), jax.ShapeDtypeStruct((SUBLANES, LANES), f32),
                          jax.ShapeDtypeStruct((SUBLANES, LANES), f32)],
               sem=("arbitrary",))(dgb, ab, a_log_pad, dt_pad)


def _ln_silu_bwd(ds, dc, ln_g, ln_b):
    t, d = dc.shape
    tm = _tile(t, 256)

    def body(ds_ref, x_ref, g_ref, b_ref, dx_ref, dg_ref, db_ref, cs_ref):
        _acc_init(pl.program_id(0), dg_ref, db_ref, cs_ref)
        x = x_ref[...]
        gv = g_ref[...]
        xc = x - jnp.mean(x, axis=-1, keepdims=True)
        rstd = lax.rsqrt(jnp.mean(xc * xc, axis=-1, keepdims=True) + NORM_EPS)
        xh = xc * rstd
        ln = xh * gv + b_ref[...]
        dln = ds_ref[...] * _silu_grad(ln)
        _acc_rows(dg_ref, dln * xh)
        _acc_rows(db_ref, dln)
        dxh = dln * gv
        dx = rstd * (dxh - jnp.mean(dxh, axis=-1, keepdims=True) - xh * jnp.mean(dxh * xh, axis=-1, keepdims=True))
        dx_ref[...] = dx
        _acc_rows(cs_ref, dx)

    return _pc(body, name="ln_silu_bwd", grid=(t // tm,),
               in_specs=[_rows(tm, d), _rows(tm, d), _const((1, d)), _const((1, d))],
               out_specs=[_rows(tm, d), _const((SUBLANES, d)), _const((SUBLANES, d)), _const((SUBLANES, d))],
               out_shape=[jax.ShapeDtypeStruct((t, d), f32)] + [jax.ShapeDtypeStruct((SUBLANES, d), f32)] * 3,
               sem=("arbitrary",))(ds, dc, ln_g, ln_b)


def _glu_bwd(dgl, ub):
    t, d = dgl.shape
    tm = _tile(t, 256)

    def body(dgl_ref, u_ref, du_ref, cs_ref):
        _acc_init(pl.program_id(0), cs_ref)
        dgl = dgl_ref[...]
        a = u_ref[:, :d].astype(f32)
        sb = _sigmoid(u_ref[:, d:].astype(f32))
        da = dgl * sb
        db = dgl * a * sb * (1.0 - sb)
        du_ref[:, :d] = da.astype(bf16)
        du_ref[:, d:] = db.astype(bf16)
        cs_ref[0:1, :d] += jnp.sum(da, axis=0, keepdims=True)
        cs_ref[0:1, d:] += jnp.sum(db, axis=0, keepdims=True)

    return _pc(body, name="glu_bwd", grid=(t // tm,),
               in_specs=[_rows(tm, d), _rows(tm, 2 * d)],
               out_specs=[_rows(tm, 2 * d), _const((SUBLANES, 2 * d))],
               out_shape=[jax.ShapeDtypeStruct((t, 2 * d), bf16), jax.ShapeDtypeStruct((SUBLANES, 2 * d), f32)],
               sem=("arbitrary",))(dgl, ub)


def _conv_rows(s):
    return 256 if s % 256 == 0 else s


def _conv_tap_sum(pad_ref, w_ref, base, rows, width):
    acc = jnp.zeros((rows, LANES), f32)
    for j in range(width):
        acc = acc + w_ref[j:j + 1, :] * pad_ref[pl.ds(base + CONV_PAD - (width - 1) + j, rows), :]
    return acc


def _l2_silu_post(c, j, n_heads, scale):
    a = c * _sigmoid(c)
    r = lax.rsqrt(jnp.sum(a * a, axis=-1, keepdims=True) + L2_EPS)
    mult = jnp.where(j < n_heads, r * scale, jnp.where(j < 2 * n_heads, r, 1.0))
    return a, r, a * mult


def _dwconv_fwd(x, w, b, name, qk_heads=None):
    bl, s, cn = x.shape
    width = w.shape[0]
    rows = _conv_rows(s)
    scale = float(LANES) ** -0.5

    def body(x_ref, w_ref, b_ref, o_ref, pad_ref):
        j = pl.program_id(1)
        pad_ref[0:CONV_PAD, :] = jnp.zeros((CONV_PAD, LANES), f32)
        pad_ref[CONV_PAD:, :] = x_ref[0]

        def step(i, carry):
            base = pl.multiple_of(i * rows, rows)
            acc = _conv_tap_sum(pad_ref, w_ref, base, rows, width)
            if qk_heads is None:
                acc = acc + b_ref[...]
            else:
                _, _, acc = _l2_silu_post(acc, j, qk_heads, scale)
            o_ref[0, pl.ds(base, rows), :] = acc
            return carry

        lax.fori_loop(0, s // rows, step, 0)

    return _pc(body, name=name, grid=(bl, cn // LANES),
               in_specs=[pl.BlockSpec((1, s, LANES), lambda bi, j: (bi, 0, j)),
                         pl.BlockSpec((width, LANES), lambda bi, j: (0, j)),
                         pl.BlockSpec((1, LANES), lambda bi, j: (0, j))],
               out_specs=pl.BlockSpec((1, s, LANES), lambda bi, j: (bi, 0, j)),
               out_shape=jax.ShapeDtypeStruct((bl, s, cn), f32),
               scratch=[pltpu.VMEM((s + CONV_PAD, LANES), f32)],
               sem=("parallel", "parallel"))(x, w, b)


def _dwconv_bwd(x, dys, w, name, qk_heads=None):
    bl, s, cn = x.shape
    width = w.shape[0]
    wp = -(-width // SUBLANES) * SUBLANES
    rows = _conv_rows(s)
    scale = float(LANES) ** -0.5
    nblk = s // rows
    ndy = len(dys)

    def body(*refs):
        x_ref, w_ref = refs[0], refs[1]
        dy_refs = refs[2:2 + ndy]
        dx_ref, dw_ref, db_ref, xpad, dypad, acc = refs[2 + ndy:]
        j = pl.program_id(0)
        bi = pl.program_id(1)
        _acc_init(bi, acc, db_ref)
        xpad[0:CONV_PAD, :] = jnp.zeros((CONV_PAD, LANES), f32)
        xpad[CONV_PAD:, :] = x_ref[0]
        dypad[s:, :] = jnp.zeros((CONV_PAD, LANES), f32)
        if qk_heads is None:
            dypad[0:s, :] = dy_refs[0][0]
        else:
            def pre(i, carry):
                base = pl.multiple_of(i * rows, rows)
                c = _conv_tap_sum(xpad, w_ref, base, rows, width)
                a, r, _ = _l2_silu_post(c, j, qk_heads, scale)
                dq = dy_refs[0][0, pl.ds(base, rows), :]
                dk = dy_refs[1][0, pl.ds(base, rows), :]
                dv = dy_refs[2][0, pl.ds(base, rows), :]
                dy = jnp.where(j < qk_heads, dq * scale, jnp.where(j < 2 * qk_heads, dk, dv))
                da_l2 = r * (dy - a * (r * r) * jnp.sum(a * dy, axis=-1, keepdims=True))
                da = jnp.where(j < 2 * qk_heads, da_l2, dy)
                dypad[pl.ds(base, rows), :] = da * _silu_grad(c)
                return carry

            lax.fori_loop(0, nblk, pre, 0)

        def step(i, carry):
            base = pl.multiple_of(i * rows, rows)
            dxa = jnp.zeros((rows, LANES), f32)
            for jj in range(width):
                dxa = dxa + w_ref[jj:jj + 1, :] * dypad[pl.ds(base + (width - 1) - jj, rows), :]
            dx_ref[0, pl.ds(base, rows), :] = dxa
            dyc = dypad[pl.ds(base, rows), :]
            db_ref[...] += dyc.reshape(rows // SUBLANES, SUBLANES, LANES).sum(axis=0)
            for jj in range(width):
                prod = dyc * xpad[pl.ds(base + CONV_PAD - (width - 1) + jj, rows), :]
                acc[jj * SUBLANES:(jj + 1) * SUBLANES, :] += prod.reshape(rows // SUBLANES, SUBLANES, LANES).sum(axis=0)
            return carry

        lax.fori_loop(0, nblk, step, 0)

        @pl.when(bi == bl - 1)
        def _():
            dw_ref[...] = jnp.zeros((wp, LANES), f32)
            for jj in range(width):
                dw_ref[jj:jj + 1, :] = jnp.sum(acc[jj * SUBLANES:(jj + 1) * SUBLANES, :], axis=0, keepdims=True)

    if qk_heads is None:
        dy_specs = [pl.BlockSpec((1, s, LANES), lambda j, bi: (bi, 0, j))]
    else:
        hh = qk_heads
        dy_specs = [pl.BlockSpec((1, s, LANES), lambda j, bi: (bi * hh + jnp.minimum(j, hh - 1), 0, 0)),
                    pl.BlockSpec((1, s, LANES), lambda j, bi: (bi * hh + jnp.clip(j - hh, 0, hh - 1), 0, 0)),
                    pl.BlockSpec((1, s, LANES), lambda j, bi: (bi * hh + jnp.clip(j - 2 * hh, 0, hh - 1), 0, 0))]
    return _pc(body, name=name, grid=(cn // LANES, bl),
               in_specs=[pl.BlockSpec((1, s, LANES), lambda j, bi: (bi, 0, j)),
                         pl.BlockSpec((width, LANES), lambda j, bi: (0, j))] + dy_specs,
               out_specs=[pl.BlockSpec((1, s, LANES), lambda j, bi: (bi, 0, j)),
                          pl.BlockSpec((wp, LANES), lambda j, bi: (0, j)),
                          pl.BlockSpec((SUBLANES, LANES), lambda j, bi: (0, j))],
               out_shape=[jax.ShapeDtypeStruct((bl, s, cn), f32), jax.ShapeDtypeStruct((wp, cn), f32),
                          jax.ShapeDtypeStruct((SUBLANES, cn), f32)],
               scratch=[pltpu.VMEM((s + CONV_PAD, LANES), f32), pltpu.VMEM((s + CONV_PAD, LANES), f32),
                        pltpu.VMEM((width * SUBLANES, LANES), f32)],
               sem=("parallel", "arbitrary"))(x, w, *dys)


def _group_rows(s):
    for rows in (256, 128):
        if s % rows == 0:
            return rows
    return CHUNK


def _group_masks(rows):
    r = lax.broadcasted_iota(jnp.int32, (rows, rows), 0)
    c = lax.broadcasted_iota(jnp.int32, (rows, rows), 1)
    same_chunk = (r >> 6) == (c >> 6)
    return r, c, same_chunk


def _decay(gc_col, gc_row, causal):
    return jnp.exp(jnp.where(causal, gc_col - gc_row, NEG))


def _inv_unit_lower(a, r, c):
    eye = (r == c).astype(f32)
    same16 = (r >> 4) == (c >> 4)
    ad = jnp.where(same16, a, 0.0)
    ao = a - ad
    x = ad
    td = eye - x
    for _ in range(3):
        x = _dot(x, x)
        td = td + _dot(td, x)
    b = _dot(td, ao)
    b2 = _dot(b, b)
    return _dot(eye - b + b2 - _dot(b, b2), td)


def _lane_cumsum(x, y, reverse, name):
    rr = x.shape[0]

    def body(x_ref, y_ref, o_ref):
        i = lax.broadcasted_iota(jnp.int32, (CHUNK, CHUNK), 0)
        j = lax.broadcasted_iota(jnp.int32, (CHUNK, CHUNK), 1)
        tri = ((i >= j) if reverse else (i <= j)).astype(f32)
        o_ref[...] = jnp.dot(x_ref[...] + y_ref[...], tri, precision=HI, preferred_element_type=f32)

    spec = pl.BlockSpec((rr, CHUNK), lambda: (0, 0))
    return pl.pallas_call(body, name=name, in_specs=[spec, spec], out_specs=spec,
                          out_shape=jax.ShapeDtypeStruct((rr, CHUNK), f32))(x, y)


def _gdn_specs(n_heads, nblk, rows, hp=1, rev=False):
    def blk(n):
        return nblk - 1 - n if rev else n

    def qkv(off):
        return pl.BlockSpec((rows, hp * LANES), lambda g, n: (
            lax.div(g * hp, n_heads) * nblk + blk(n), lax.div(off * n_heads + lax.rem(g * hp, n_heads), hp)))

    def per_head(last, mult=1):
        return pl.BlockSpec((hp, rows * mult, last), lambda g, n: (g, blk(n), 0))

    def row_vec():
        return pl.BlockSpec((hp, 1, 1, rows), lambda g, n: (g, blk(n), 0, 0))

    return qkv, per_head, row_vec


def _gdn_prep(qkv, cols, grow, bl, s, n_heads):
    rows = _group_rows(s)
    nblk = s // rows
    bh_n = bl * n_heads
    qkv_spec, ph, rv = _gdn_specs(n_heads, nblk, rows)

    def body(k_ref, v_ref, cols_ref, grow_ref, u_ref, w_ref, t_ref):
        r, c, same = _group_masks(rows)
        k = k_ref[...]
        gc = cols_ref[0, :, 0:1]
        beta = cols_ref[0, :, 1:2]
        dmat = _decay(gc, grow_ref[0, 0], jnp.logical_and(same, r >= c))
        kb = k * beta
        a = jnp.where(r > c, _dot_nt(kb, k) * dmat, 0.0)
        tm = _inv_unit_lower(a, r, c)
        u_ref[0] = _dot(tm, v_ref[...] * beta)
        w_ref[0] = _dot(tm, kb * jnp.exp(gc))
        t_ref[0] = tm.astype(bf16)

    return _pc(body, name="gdn_prep", grid=(bh_n, nblk),
               in_specs=[qkv_spec(1), qkv_spec(2), ph(2), rv()],
               out_specs=[ph(LANES), ph(LANES), ph(rows)],
               out_shape=[jax.ShapeDtypeStruct((bh_n, s, LANES), f32), jax.ShapeDtypeStruct((bh_n, s, LANES), f32),
                          jax.ShapeDtypeStruct((bh_n, s, rows), bf16)],
               sem=("parallel", "parallel"))(qkv, qkv, cols, grow)


def _gdn_scan(qkv, u, w, cols, grow, bl, s, n_heads):
    rows = _group_rows(s)
    g_chunks = rows // CHUNK
    nblk = s // rows
    bh_n = bl * n_heads
    d = n_heads * LANES
    hp = HEADS_PER_STEP if n_heads % HEADS_PER_STEP == 0 else 1
    qkv_spec, ph, rv = _gdn_specs(n_heads, nblk, rows, hp)

    def body(q_ref, k_ref, u_ref, w_ref, cols_ref, grow_ref, o_ref, vn_ref, ss_ref, s_scr):
        _acc_init(pl.program_id(1), s_scr)
        r, c, same = _group_masks(rows)
        causal = jnp.logical_and(same, r >= c)
        qs, ks, gcs, qgs, ps = [], [], [], [], []
        for h in range(hp):
            hs = slice(h * LANES, (h + 1) * LANES)
            q, k, gc = q_ref[:, hs], k_ref[:, hs], cols_ref[h, :, 0:1]
            qs.append(q)
            ks.append(k)
            gcs.append(gc)
            qgs.append(q * jnp.exp(gc))
            ps.append(_dot_nt(q, k) * _decay(gc, grow_ref[h, 0], causal))
        st = [s_scr[h] for h in range(hp)]
        o_state = [[] for _ in range(hp)]
        vns = [[] for _ in range(hp)]
        for ci in range(g_chunks):
            sl = slice(ci * CHUNK, (ci + 1) * CHUNK)
            for h in range(hp):
                ss_ref[h, ci * LANES:(ci + 1) * LANES, :] = st[h]
                vn = u_ref[h, sl, :] - _dot(w_ref[h, sl, :], st[h])
                vns[h].append(vn)
                o_state[h].append(_dot(qgs[h][sl], st[h]))
                gc = gcs[h][sl]
                g_last = gc[CHUNK - 1:CHUNK, :]
                st[h] = jnp.exp(g_last) * st[h] + _dot_tn(ks[h][sl] * jnp.exp(g_last - gc), vn)
        for h in range(hp):
            s_scr[h] = st[h]
            vn_all = jnp.concatenate(vns[h], axis=0)
            vn_ref[h] = vn_all
            o_ref[:, h * LANES:(h + 1) * LANES] = jnp.concatenate(o_state[h], axis=0) + _dot(ps[h], vn_all)

    return _pc(body, name="gdn_scan", grid=(bh_n // hp, nblk),
               in_specs=[qkv_spec(0), qkv_spec(1), ph(LANES), ph(LANES), ph(2), rv()],
               out_specs=[qkv_spec(0), ph(LANES), ph(LANES, mult=LANES // CHUNK)],
               out_shape=[jax.ShapeDtypeStruct((bl * s, d), f32), jax.ShapeDtypeStruct((bh_n, s, LANES), f32),
                          jax.ShapeDtypeStruct((bh_n, (s // CHUNK) * LANES, LANES), f32)],
               scratch=[pltpu.VMEM((hp, LANES, LANES), f32)],
               sem=("parallel", "arbitrary"))(qkv, qkv, u, w, cols, grow)


def _gdn_scan_bwd(do, qkv, w, vn, cols, grow, ss, bl, s, n_heads):
    rows = _group_rows(s)
    g_chunks = rows // CHUNK
    nblk = s // rows
    bh_n = bl * n_heads
    hp = HEADS_PER_STEP if n_heads % HEADS_PER_STEP == 0 else 1
    qkv_spec, ph, rv = _gdn_specs(n_heads, nblk, rows, hp, rev=True)

    def body(do_ref, q_ref, k_ref, w_ref, vn_ref, cols_ref, grow_ref, ss_ref,
             du_ref, dw_ref, dq_ref, dk_ref, dcol_ref, drow_ref, ds_scr):
        _acc_init(pl.program_id(1), ds_scr)
        r, c, same = _group_masks(rows)
        causal = jnp.logical_and(same, r >= c)
        last_row = lax.broadcasted_iota(jnp.int32, (CHUNK, 1), 0) == CHUNK - 1
        pre = []
        for h in range(hp):
            hs = slice(h * LANES, (h + 1) * LANES)
            do, q, k = do_ref[:, hs], q_ref[:, hs], k_ref[:, hs]
            vn = vn_ref[h]
            gc = cols_ref[h, :, 0:1]
            dmat = _decay(gc, grow_ref[h, 0], causal)
            gam = jnp.exp(gc)
            qk = _dot_nt(q, k)
            dpd = _dot_nt(do, vn) * dmat
            ep = dpd * qk
            drow_ref[h, 0] = -jnp.sum(ep, axis=0, keepdims=True)
            pre.append(dict(do=do, k=k, vn=vn, gc=gc, gam=gam, qg=q * gam,
                            dvn_intra=_dot_tn(qk * dmat, do), dq_intra=_dot(dpd, k), dk_intra=_dot_tn(dpd, q),
                            ep_rows=jnp.sum(ep, axis=-1, keepdims=True)))
        ds = [ds_scr[h] for h in range(hp)]
        for ci in reversed(range(g_chunks)):
            sl = slice(ci * CHUNK, (ci + 1) * CHUNK)
            for h in range(hp):
                pr = pre[h]
                st = ss_ref[h, ci * LANES:(ci + 1) * LANES, :]
                gc = pr["gc"][sl]
                g_last = gc[CHUNK - 1:CHUNK, :]
                gam_last = jnp.exp(g_last)
                kd_scale = jnp.exp(g_last - gc)
                kdec = pr["k"][sl] * kd_scale
                do_c = pr["do"][sl]
                qg_c = pr["qg"][sl]
                dvn = pr["dvn_intra"][sl] + _dot(kdec, ds[h])
                dkdec = _dot_nt(pr["vn"][sl], ds[h])
                du_ref[h, sl, :] = dvn
                dw_ref[h, sl, :] = -_dot_nt(dvn, st)
                dqg = _dot_nt(do_c, st)
                dq_ref[h, sl, :] = dqg * pr["gam"][sl] + pr["dq_intra"][sl]
                dk_ref[h, sl, :] = pr["dk_intra"][sl] + dkdec * kd_scale
                kd_rows = jnp.sum(dkdec * kdec, axis=-1, keepdims=True)
                extra = jnp.sum(kd_rows) + gam_last * jnp.sum(st * ds[h])
                dcol_ref[h, sl, :] = (jnp.sum(dqg * qg_c, axis=-1, keepdims=True) + pr["ep_rows"][sl] - kd_rows
                                      + jnp.where(last_row, extra, 0.0))
                ds[h] = gam_last * ds[h] + _dot_tn(qg_c, do_c) - _dot_tn(w_ref[h, sl, :], dvn)
        for h in range(hp):
            ds_scr[h] = ds[h]

    return _pc(body, name="gdn_scan_bwd", grid=(bh_n // hp, nblk),
               in_specs=[qkv_spec(0), qkv_spec(0), qkv_spec(1), ph(LANES), ph(LANES), ph(2), rv(),
                         ph(LANES, mult=LANES // CHUNK)],
               out_specs=[ph(LANES), ph(LANES), ph(LANES), ph(LANES), ph(1), rv()],
               out_shape=[jax.ShapeDtypeStruct((bh_n, s, LANES), f32)] * 4
               + [jax.ShapeDtypeStruct((bh_n, s, 1), f32), jax.ShapeDtypeStruct((bh_n, nblk, 1, rows), f32)],
               scratch=[pltpu.VMEM((hp, LANES, LANES), f32)],
               sem=("parallel", "arbitrary"))(do, qkv, qkv, w, vn, cols, grow, ss)


def _gdn_prep_bwd(qkv, cols, grow, tmat, du, dw, dk_scan, dcol_scan, drow_scan, bl, s, n_heads):
    rows = _group_rows(s)
    nblk = s // rows
    bh_n = bl * n_heads
    qkv_spec, ph, rv = _gdn_specs(n_heads, nblk, rows)

    def body(k_ref, v_ref, cols_ref, grow_ref, t_ref, du_ref, dw_ref, dks_ref, dcs_ref, drs_ref,
             dk_ref, dv_ref, dcols_ref, drow_ref):
        r, c, same = _group_masks(rows)
        k = k_ref[...]
        v = v_ref[...]
        gc = cols_ref[0, :, 0:1]
        beta = cols_ref[0, :, 1:2]
        tm = t_ref[0]
        du = du_ref[0]
        dw_c = dw_ref[0]
        dmat = _decay(gc, grow_ref[0, 0], jnp.logical_and(same, r >= c))
        gam = jnp.exp(gc)
        kb = k * beta
        kbg = kb * gam
        dt = jnp.where(same, _dot_nt(du, v * beta) + _dot_nt(dw_c, kbg), 0.0)
        dvb = _dot_tn(tm, du)
        dkbg = _dot_tn(tm, dw_c)
        dad = jnp.where(r > c, -_dot_tn(tm, _dot_nt(dt, tm)), 0.0) * dmat
        kk = _dot_nt(kb, k)
        dkb = dkbg * gam + _dot(dad, k)
        dk_ref[0] = dks_ref[0] + _dot_tn(dad, kb) + dkb * beta
        dv_ref[0] = dvb * beta
        ea = dad * kk
        dcols_ref[0, :, 0:1] = (dcs_ref[0] + jnp.sum(dkbg * kbg, axis=-1, keepdims=True)
                                + jnp.sum(ea, axis=-1, keepdims=True))
        dcols_ref[0, :, 1:2] = jnp.sum(dvb * v, axis=-1, keepdims=True) + jnp.sum(dkb * k, axis=-1, keepdims=True)
        drow_ref[0, 0] = drs_ref[0, 0] - jnp.sum(ea, axis=0, keepdims=True)

    return _pc(body, name="gdn_prep_bwd", grid=(bh_n, nblk),
               in_specs=[qkv_spec(1), qkv_spec(2), ph(2), rv(), ph(rows), ph(LANES), ph(LANES), ph(LANES), ph(1), rv()],
               out_specs=[ph(LANES), ph(LANES), ph(2), rv()],
               out_shape=[jax.ShapeDtypeStruct((bh_n, s, LANES), f32), jax.ShapeDtypeStruct((bh_n, s, LANES), f32),
                          jax.ShapeDtypeStruct((bh_n, s, 2), f32), jax.ShapeDtypeStruct((bh_n, nblk, 1, rows), f32)],
               sem=("parallel", "parallel"))(qkv, qkv, cols, grow, tmat, du, dw, dk_scan, dcol_scan, drow_scan)


def _row(v):
    return v.reshape(1, -1).astype(f32)


def _pad_lanes(v):
    v = v.reshape(1, -1).astype(f32)
    return jnp.pad(v, ((0, 0), (0, LANES - v.shape[1])))


def _local_step(x, tgt, p):
    bl, s, d = x.shape
    t = bl * s
    n_heads = p["gdn_a_log"].shape[-1]
    assert d == n_heads * LANES and s % CHUNK == 0
    x2 = x.reshape(t, d)
    tgt2 = tgt.reshape(t, d)
    gr = {}

    n0, ub, gl = _pw1_glu(x2, _row(p["norm_mix_g"][0]), p["cv_w_pw1"], _row(p["cv_b_pw1"]))
    dc = _dwconv_fwd(gl.reshape(bl, s, d), p["cv_w_dw"], _row(p["cv_b_dw"]), "dwconv_fwd").reshape(t, d)
    sb, h1 = _ln_silu_mm_res(dc, _row(p["cv_ln_g"]), _row(p["cv_ln_b"]), p["cv_w_pw2"], _row(p["cv_b_pw2"]), x2)
    n1, f0, r0, h2 = _mlp_fwd(h1, _row(p["norm_ffn_g"][0]), p["mlp_w1"], p["mlp_w2"], 0, "mlp_fwd0")

    w_in = p["gdn_w_in"]
    w_ab = jnp.pad(w_in[:, 4 * d:], ((0, 0), (0, LANES - 2 * n_heads)))
    a_log_pad = _pad_lanes(p["gdn_a_log"])
    dt_pad = _pad_lanes(p["gdn_dt_bias"])
    n2, qkv_pre, z, ab, gbeta = _gdn_in(h2, _row(p["norm_mix_g"][1]), w_in, w_ab, a_log_pad, dt_pad, n_heads)
    zero_bias = jnp.zeros((1, 3 * d), f32)
    qkv = _dwconv_fwd(qkv_pre.reshape(bl, s, 3 * d), p["gdn_conv_w"], zero_bias, "sconv_fwd", qk_heads=n_heads).reshape(t, 3 * d)
    bh_n, rows = bl * n_heads, _group_rows(s)
    gates = gbeta[:, :2 * n_heads].reshape(bl, s, 2, n_heads).transpose(2, 0, 3, 1).reshape(2, bh_n, s)
    g_lanes = gates[0].reshape(bh_n * (s // CHUNK), CHUNK)
    gc_lanes = _lane_cumsum(g_lanes, jnp.zeros_like(g_lanes), False, "gdn_gate_cumsum")
    grow = gc_lanes.reshape(bh_n, s // rows, 1, rows)
    cols = jnp.stack([gc_lanes.reshape(bh_n, s), gates[1]], axis=-1)
    u, w, tmat = _gdn_prep(qkv, cols, grow, bl, s, n_heads)
    o, vn, ss = _gdn_scan(qkv, u, w, cols, grow, bl, s, n_heads)
    onb, h3 = _gated_norm_mm_res(o, z, _row(p["gdn_norm_g"]), p["gdn_w_out"], h2, n_heads)
    n3, f1, r1, h4 = _mlp_fwd(h3, _row(p["norm_ffn_g"][1]), p["mlp_w1"], p["mlp_w2"], 1, "mlp_fwd1")
    loss_acc, dh4, dgf = _loss_head(h4, _row(p["final_norm_g"]), tgt2)
    loss = loss_acc[0, 0]
    gr["final_norm_g"] = dgf[0]

    df1, dh3, dg_ffn1, _ = _mlp_bwd(dh4, h3, _row(p["norm_ffn_g"][1]), f1, p["mlp_w1"], p["mlp_w2"], 1, "mlp_bwd1")
    dw2_1 = _mm_tn(r1, dh4, "dw_mlp2_1")
    dw1_1 = _mm_tn_blocked(n3, df1, "dw_mlp1_1")

    gr["gdn_w_out"] = _mm_tn(onb, dh3, "dw_gdn_out")
    don = _mm_nt([(dh3, p["gdn_w_out"])], "dx_gdn_out")
    do, dz, dng = _gated_norm_bwd(don, o, z, _row(p["gdn_norm_g"]), n_heads)
    gr["gdn_norm_g"] = dng[0]
    du, dw_, dq, dk_scan, dcol_scan, drow_scan = _gdn_scan_bwd(do, qkv, w, vn, cols, grow, ss, bl, s, n_heads)
    dk, dv, dcols, drow = _gdn_prep_bwd(qkv, cols, grow, tmat, du, dw_, dk_scan, dcol_scan, drow_scan, bl, s, n_heads)
    dg_lanes = _lane_cumsum(dcols[..., 0].reshape(g_lanes.shape), drow.reshape(g_lanes.shape), True, "gdn_gate_cumsum_bwd")
    dgb2 = jnp.stack([dg_lanes.reshape(bh_n, s), dcols[..., 1]], axis=-1)
    dqkv_pre, dconv_w, _ = _dwconv_bwd(qkv_pre.reshape(bl, s, 3 * d), [dq, dk, dv], p["gdn_conv_w"], "sconv_bwd", qk_heads=n_heads)
    gr["gdn_conv_w"] = dconv_w[:p["gdn_conv_w"].shape[0]]
    dgb = dgb2.reshape(bl, n_heads, s, 2).transpose(0, 2, 3, 1).reshape(t, 2 * n_heads)
    dgb = jnp.pad(dgb, ((0, 0), (0, LANES - 2 * n_heads)))
    dab, dal, ddt = _gates_bwd(dgb, ab, a_log_pad, dt_pad, n_heads)
    gr["gdn_a_log"] = dal[0, :n_heads]
    gr["gdn_dt_bias"] = ddt[0, :n_heads]
    dqkv2 = dqkv_pre.reshape(t, 3 * d)
    gr["gdn_w_in"] = jnp.concatenate(
        [_mm_tn(n2, dqkv2, "dw_gdn_in_qkv"), _mm_tn(n2, dz, "dw_gdn_in_z"), _mm_tn(n2, dab, "dw_gdn_in_ab")[:, :2 * n_heads]], axis=1)
    dn2 = _mm_nt([(dqkv2, (w_in, 0)), (dz, (w_in, 3)), (dab, w_ab)], "dx_gdn_in")
    dh2, dg_mix1 = _rms_bwd_res(dn2, h2, _row(p["norm_mix_g"][1]), dh3, "rms_bwd_gdn")

    df0, dh1, dg_ffn0, cs_h1 = _mlp_bwd(dh2, h1, _row(p["norm_ffn_g"][0]), f0, p["mlp_w1"], p["mlp_w2"], 0, "mlp_bwd0")
    dw2_0 = _mm_tn(r0, dh2, "dw_mlp2_0")
    dw1_0 = _mm_tn_blocked(n1, df0, "dw_mlp1_0")
    gr["mlp_w1"] = [dw1_0, dw1_1]
    gr["mlp_w2"] = [dw2_0, dw2_1]
    gr["norm_ffn_g"] = jnp.stack([dg_ffn0[0], dg_ffn1[0]])

    gr["cv_b_pw2"] = cs_h1[0]
    gr["cv_w_pw2"] = _mm_tn(sb, dh1, "dw_pw2")
    ds = _mm_nt([(dh1, p["cv_w_pw2"])], "dx_pw2")
    ddc, dlng, dlnb, cs_dc = _ln_silu_bwd(ds, dc, _row(p["cv_ln_g"]), _row(p["cv_ln_b"]))
    gr["cv_ln_g"] = dlng[0]
    gr["cv_ln_b"] = dlnb[0]
    gr["cv_b_dw"] = cs_dc[0]
    dgl, dw_dw, _ = _dwconv_bwd(gl.reshape(bl, s, d), [ddc.reshape(bl, s, d)], p["cv_w_dw"], "dwconv_bwd")
    gr["cv_w_dw"] = dw_dw[:p["cv_w_dw"].shape[0]]
    dub, cs_u = _glu_bwd(dgl.reshape(t, d), ub)
    gr["cv_b_pw1"] = cs_u[0]
    gr["cv_w_pw1"] = _mm_tn_blocked(n0, dub, "dw_pw1")
    dn0 = _mm_nt_blocked(dub, p["cv_w_pw1"], "dx_pw1")
    dx, dg_mix0 = _rms_bwd_res(dn0, x2, _row(p["norm_mix_g"][0]), dh1, "rms_bwd_conv")
    gr["norm_mix_g"] = jnp.stack([dg_mix0[0], dg_mix1[0]])
    return loss, dx.reshape(bl, s, d), gr


ANY = pl.BlockSpec(memory_space=pl.ANY)
MESH = pl.DeviceIdType.MESH


def _flip(v, bit):
    return 1 - v if bit else v


def _all_gather_many(shards):
    na = len(shards)

    def body(*refs):
        x_refs, o_refs = refs[:na], refs[na:2 * na]
        send_sems, recv_sems, local_sems = refs[2 * na:]
        x, y, c = lax.axis_index("x"), lax.axis_index("y"), lax.axis_index("c")
        me, sibling = (x, y, c), (x, y, 1 - c)
        chips = [(1 - x, y), (x, 1 - y), (1 - x, 1 - y)]

        def copy(a, k, block, to, src=None):
            px, py, pc = block
            dst = o_refs[a].at[4 * px + 2 * py + pc]
            return pltpu.make_async_remote_copy(
                src_ref=dst if src is None else src, dst_ref=dst,
                send_sem=send_sems.at[a, k], recv_sem=recv_sems.at[a, k], device_id=to, device_id_type=MESH)

        mine = [pltpu.make_async_copy(x_refs[a], o_refs[a].at[4 * x + 2 * y + c], local_sems.at[a]) for a in range(na)]
        first = []
        for a in range(na):
            first.append(copy(a, 0, me, sibling, src=x_refs[a]))
            first += [copy(a, 1 + j, me, (*chip, c), src=x_refs[a]) for j, chip in enumerate(chips)]
        for cp in mine + first:
            cp.start()
        passed = []
        for j, chip in enumerate(chips):
            for a in range(na):
                copy(a, 1 + j, (*chip, c), me).wait_recv()
                fwd = copy(a, 4 + j, (*chip, c), sibling)
                fwd.start()
                passed.append(fwd)
        for a in range(na):
            copy(a, 0, sibling, me).wait_recv()
        for j, chip in enumerate(chips):
            for a in range(na):
                copy(a, 4 + j, (*chip, 1 - c), me).wait_recv()
        for cp in first + passed:
            cp.wait_send()
        for cp in mine:
            cp.wait()

    return pl.pallas_call(
        body, name="weights_all_gather",
        out_shape=[jax.ShapeDtypeStruct((N_DEV,) + a.shape, a.dtype) for a in shards],
        in_specs=[ANY] * na, out_specs=[ANY] * na,
        scratch_shapes=[pltpu.SemaphoreType.DMA((na, 7)), pltpu.SemaphoreType.DMA((na, 7)), pltpu.SemaphoreType.DMA((na,))],
        compiler_params=pltpu.CompilerParams(has_side_effects=True),
    )(*shards)


def _pair_exchange(gs):
    na = len(gs)

    def body(*refs):
        g_refs, r_refs = refs[:na], refs[na:2 * na]
        send_sems, recv_sems = refs[2 * na:]
        x, y, c = lax.axis_index("x"), lax.axis_index("y"), lax.axis_index("c")
        copies = []
        for a in range(na):
            for q in range(4):
                copies.append(pltpu.make_async_remote_copy(
                    src_ref=g_refs[a].at[2 * q + (1 - c)], dst_ref=r_refs[a].at[q],
                    send_sem=send_sems.at[a, q], recv_sem=recv_sems.at[a, q],
                    device_id=(x, y, 1 - c), device_id_type=MESH))
        for cp in copies:
            cp.start()
        for cp in copies:
            cp.wait_recv()
        for cp in copies:
            cp.wait_send()

    return pl.pallas_call(
        body, name="grads_pair_exchange",
        out_shape=[jax.ShapeDtypeStruct((4,) + g.shape[1:], g.dtype) for g in gs],
        in_specs=[ANY] * na, out_specs=[ANY] * na,
        scratch_shapes=[pltpu.SemaphoreType.DMA((na, 4)), pltpu.SemaphoreType.DMA((na, 4))],
        compiler_params=pltpu.CompilerParams(has_side_effects=True),
    )(*gs)


def _pair_sum(g, r1, core, name):
    _, rr, cc = g.shape
    tr = _tile(rr, 256)

    def body(c_ref, g_ref, r_ref, o_ref):
        o_ref[...] = (g_ref[...] + r_ref[...]).astype(bf16)

    grid_spec = pltpu.PrefetchScalarGridSpec(
        num_scalar_prefetch=1, grid=(4, rr // tr),
        in_specs=[pl.BlockSpec((None, tr, cc), lambda q, i, c_ref: (2 * q + c_ref[0], i, 0)),
                  pl.BlockSpec((None, tr, cc), lambda q, i, c_ref: (q, i, 0))],
        out_specs=pl.BlockSpec((None, tr, cc), lambda q, i, c_ref: (q, i, 0)))
    return pl.pallas_call(
        body, name=name, grid_spec=grid_spec, out_shape=jax.ShapeDtypeStruct((4, rr, cc), bf16),
        compiler_params=pltpu.CompilerParams(dimension_semantics=("parallel", "parallel"), vmem_limit_bytes=VMEM_LIMIT),
    )(core, g, r1)


def _chip_exchange(hs, small):
    na = len(hs)

    def body(*refs):
        h_refs, s_ref = refs[:na], refs[na]
        r_refs, sr_ref = refs[na + 1:2 * na + 1], refs[2 * na + 1]
        send_sems, recv_sems, local_sems, s_send, s_recv, s_local = refs[2 * na + 2:]
        x, y, c = lax.axis_index("x"), lax.axis_index("y"), lax.axis_index("c")
        q_me = 2 * x + y
        me = 4 * x + 2 * y + c
        local = [pltpu.make_async_copy(h_refs[a].at[q_me], r_refs[a].at[q_me], local_sems.at[a]) for a in range(na)]
        local.append(pltpu.make_async_copy(s_ref, sr_ref.at[me], s_local))
        copies = []
        for rel in range(1, 4):
            px, py = _flip(x, rel & 2), _flip(y, rel & 1)
            for a in range(na):
                copies.append(pltpu.make_async_remote_copy(
                    src_ref=h_refs[a].at[2 * px + py], dst_ref=r_refs[a].at[q_me],
                    send_sem=send_sems.at[a, rel - 1], recv_sem=recv_sems.at[a, rel - 1],
                    device_id=(px, py, c), device_id_type=MESH))
        for k in range(1, N_DEV):
            px, py, pc = _flip(x, k & 4), _flip(y, k & 2), _flip(c, k & 1)
            copies.append(pltpu.make_async_remote_copy(
                src_ref=s_ref, dst_ref=sr_ref.at[me], send_sem=s_send.at[k - 1], recv_sem=s_recv.at[k - 1],
                device_id=(px, py, pc), device_id_type=MESH))
        for cp in local + copies:
            cp.start()
        for cp in copies:
            cp.wait_recv()
        for cp in copies:
            cp.wait_send()
        for cp in local:
            cp.wait()

    return pl.pallas_call(
        body, name="grads_chip_exchange",
        out_shape=[jax.ShapeDtypeStruct(h.shape, h.dtype) for h in hs]
        + [jax.ShapeDtypeStruct((N_DEV,) + small.shape, small.dtype)],
        in_specs=[ANY] * (na + 1), out_specs=[ANY] * (na + 1),
        scratch_shapes=[pltpu.SemaphoreType.DMA((na, 3)), pltpu.SemaphoreType.DMA((na, 3)), pltpu.SemaphoreType.DMA((na,)),
                        pltpu.SemaphoreType.DMA((7,)), pltpu.SemaphoreType.DMA((7,)), pltpu.SemaphoreType.DMA(())],
        compiler_params=pltpu.CompilerParams(has_side_effects=True),
    )(*hs, small)


def _sum4_adamw(r2, w, m, v, layer, name):
    _, rr, cc = r2.shape
    tr = _tile(rr, 256)
    bc1 = 1.0 - ADAM_B1 ** ADAM_STEP
    bc2 = 1.0 - ADAM_B2 ** ADAM_STEP

    def body(r_ref, w_ref, m_ref, v_ref, g_ref, d_ref, nm_ref, nv_ref):
        gv = r_ref[0].astype(f32)
        for q in range(1, 4):
            gv = gv + r_ref[q].astype(f32)
        g_ref[...] = gv
        nm = ADAM_B1 * m_ref[...] + (1.0 - ADAM_B1) * gv
        nv = ADAM_B2 * v_ref[...] + (1.0 - ADAM_B2) * (gv * gv)
        nm_ref[...] = nm
        nv_ref[...] = nv
        d_ref[...] = -ADAM_LR * ((nm / bc1) / (jnp.sqrt(nv / bc2) + ADAM_EPS) + ADAM_WD * w_ref[...])

    lspec = pl.BlockSpec((None, tr, cc), lambda i: (layer, i, 0))
    return _pc(body, name=name, grid=(rr // tr,),
               in_specs=[pl.BlockSpec((4, tr, cc), lambda i: (0, i, 0)), lspec, lspec, lspec],
               out_specs=[_rows(tr, cc)] * 4, out_shape=[jax.ShapeDtypeStruct((rr, cc), f32)] * 4,
               sem=("parallel",))(r2, w, m, v)


def _sum_devices(recv):
    _, rr, _ = recv.shape
    tr = _tile(rr, 512)

    def body(r_ref, o_ref):
        acc = r_ref[0]
        for i in range(1, N_DEV):
            acc = acc + r_ref[i]
        o_ref[...] = acc

    return _pc(body, name="grads_sum", grid=(rr // tr,),
               in_specs=[pl.BlockSpec((N_DEV, tr, LANES), lambda i: (0, i, 0))],
               out_specs=_rows(tr, LANES), out_shape=jax.ShapeDtypeStruct((rr, LANES), f32), sem=("parallel",))(recv)


def _adamw(w, g, m, v):
    rr = w.shape[0]
    tr = _tile(rr, 512)
    bc1 = 1.0 - ADAM_B1 ** ADAM_STEP
    bc2 = 1.0 - ADAM_B2 ** ADAM_STEP

    def body(w_ref, g_ref, m_ref, v_ref, d_ref, nm_ref, nv_ref):
        gv = g_ref[...]
        nm = ADAM_B1 * m_ref[...] + (1.0 - ADAM_B1) * gv
        nv = ADAM_B2 * v_ref[...] + (1.0 - ADAM_B2) * (gv * gv)
        nm_ref[...] = nm
        nv_ref[...] = nv
        d_ref[...] = -ADAM_LR * ((nm / bc1) / (jnp.sqrt(nv / bc2) + ADAM_EPS) + ADAM_WD * w_ref[...])

    spec = _rows(tr, LANES)
    return _pc(body, name="adamw", grid=(rr // tr,), in_specs=[spec] * 4, out_specs=[spec] * 3,
               out_shape=[jax.ShapeDtypeStruct((rr, LANES), f32)] * 3, sem=("parallel",))(w, g, m, v)


PACK_ROWS = 512
PART_ROWS = 16


def _pack(arrs, lead=()):
    nl = len(lead)
    parts, sizes = [], []
    for a in arrs:
        flat = a.reshape(lead + (-1,))
        n = flat.shape[-1]
        rows = -(-n // (LANES * PART_ROWS)) * PART_ROWS
        flat = jnp.pad(flat, [(0, 0)] * nl + [(0, rows * LANES - n)])
        parts.append(flat.reshape(lead + (rows, LANES)))
        sizes.append((rows, n))
    total = sum(r for r, _ in sizes)
    padded = -(-total // PACK_ROWS) * PACK_ROWS
    if padded > total:
        parts.append(jnp.zeros(lead + (padded - total, LANES), parts[0].dtype))
    return jnp.concatenate(parts, axis=nl), sizes


def _unpack(packed, sizes, shapes, lead=()):
    nl = len(lead)
    out, off = [], 0
    for (rows, n), shp in zip(sizes, shapes):
        piece = lax.slice_in_dim(packed, off, off + rows, axis=nl).reshape(lead + (rows * LANES,))
        out.append(lax.slice_in_dim(piece, 0, n, axis=nl).reshape(lead + tuple(shp)))
        off += rows
    return out


def _cols_to_blocks(a):
    n = a.shape[-1] // N_DEV
    a = a.reshape(a.shape[:-1] + (N_DEV, n))
    return jnp.moveaxis(a, -2, 0)


def _blocks_to_cols(a):
    a = jnp.moveaxis(a, 0, -2)
    return a.reshape(a.shape[:-2] + (a.shape[-2] * a.shape[-1],))


def _rows_to_blocks(a):
    k = a.shape[-2] // N_DEV
    a = a.reshape(a.shape[:-2] + (N_DEV, k, a.shape[-1]))
    return jnp.moveaxis(a, -3, 0)


def _blocks_to_rows(a):
    a = jnp.moveaxis(a, 0, -3)
    return a.reshape(a.shape[:-3] + (a.shape[-3] * a.shape[-2], a.shape[-1]))


COL_SHARDED = ("cv_w_pw1", "gdn_w_in", "mlp_w1")
ROW_SHARDED = ("cv_w_pw2", "gdn_w_out", "mlp_w2")
CONV_SHARDED = ("cv_w_dw", "gdn_conv_w")
REPLICATED = ("norm_mix_g", "norm_ffn_g", "final_norm_g", "cv_b_pw1", "cv_b_dw", "cv_ln_g", "cv_ln_b", "cv_b_pw2",
              "gdn_a_log", "gdn_dt_bias", "gdn_norm_g")
WEIGHTS = ("norm_mix_g", "norm_ffn_g", "final_norm_g", "cv_w_pw1", "cv_b_pw1", "cv_w_dw", "cv_b_dw", "cv_ln_g",
           "cv_ln_b", "cv_w_pw2", "cv_b_pw2", "gdn_w_in", "gdn_conv_w", "gdn_a_log", "gdn_dt_bias", "gdn_norm_g",
           "gdn_w_out", "mlp_w1", "mlp_w2")
MATMUL_SHARDED = COL_SHARDED + ROW_SHARDED


def _squeeze_layer(name, a):
    if name in ("norm_mix_g", "norm_ffn_g", "final_norm_g", "mlp_w1", "mlp_w2"):
        return a
    return a[0]


def _gather_weights(shards):
    names = MATMUL_SHARDED + CONV_SHARDED
    send = []
    for n in names:
        a = shards[n] if n in CONV_SHARDED else shards[n].astype(bf16)
        if n == "gdn_w_in":
            a = a.reshape(-1, LANES)
        send.append(a)
    got = dict(zip(names, _all_gather_many(send)))
    full = {
        "cv_w_pw1": got["cv_w_pw1"],
        "mlp_w1": got["mlp_w1"],
        "mlp_w2": got["mlp_w2"],
        "cv_w_pw2": _blocks_to_rows(got["cv_w_pw2"]),
        "gdn_w_out": _blocks_to_rows(got["gdn_w_out"]),
        "gdn_w_in": _blocks_to_cols(got["gdn_w_in"].reshape((N_DEV,) + shards["gdn_w_in"].shape)),
    }
    for n in CONV_SHARDED:
        full[n] = _blocks_to_cols(got[n])
    return full


def kernel(x, norm_mix_g, norm_ffn_g, final_norm_g, cv_w_pw1, cv_b_pw1, cv_w_dw, cv_b_dw, cv_ln_g, cv_ln_b, cv_w_pw2, cv_b_pw2, gdn_w_in, gdn_conv_w, gdn_a_log, gdn_dt_bias, gdn_norm_g, gdn_w_out, mlp_w1, mlp_w2, loss_target, m_norm_mix_g, m_norm_ffn_g, m_final_norm_g, m_cv_w_pw1, m_cv_b_pw1, m_cv_w_dw, m_cv_b_dw, m_cv_ln_g, m_cv_ln_b, m_cv_w_pw2, m_cv_b_pw2, m_gdn_w_in, m_gdn_conv_w, m_gdn_a_log, m_gdn_dt_bias, m_gdn_norm_g, m_gdn_w_out, m_mlp_w1, m_mlp_w2, v_norm_mix_g, v_norm_ffn_g, v_final_norm_g, v_cv_w_pw1, v_cv_b_pw1, v_cv_w_dw, v_cv_b_dw, v_cv_ln_g, v_cv_ln_b, v_cv_w_pw2, v_cv_b_pw2, v_gdn_w_in, v_gdn_conv_w, v_gdn_a_log, v_gdn_dt_bias, v_gdn_norm_g, v_gdn_w_out, v_mlp_w1, v_mlp_w2):
    w_in = dict(zip(WEIGHTS, (norm_mix_g, norm_ffn_g, final_norm_g, cv_w_pw1, cv_b_pw1, cv_w_dw, cv_b_dw, cv_ln_g, cv_ln_b, cv_w_pw2, cv_b_pw2, gdn_w_in, gdn_conv_w, gdn_a_log, gdn_dt_bias, gdn_norm_g, gdn_w_out, mlp_w1, mlp_w2)))
    m_in = dict(zip(WEIGHTS, (m_norm_mix_g, m_norm_ffn_g, m_final_norm_g, m_cv_w_pw1, m_cv_b_pw1, m_cv_w_dw, m_cv_b_dw, m_cv_ln_g, m_cv_ln_b, m_cv_w_pw2, m_cv_b_pw2, m_gdn_w_in, m_gdn_conv_w, m_gdn_a_log, m_gdn_dt_bias, m_gdn_norm_g, m_gdn_w_out, m_mlp_w1, m_mlp_w2)))
    v_in = dict(zip(WEIGHTS, (v_norm_mix_g, v_norm_ffn_g, v_final_norm_g, v_cv_w_pw1, v_cv_b_pw1, v_cv_w_dw, v_cv_b_dw, v_cv_ln_g, v_cv_ln_b, v_cv_w_pw2, v_cv_b_pw2, v_gdn_w_in, v_gdn_conv_w, v_gdn_a_log, v_gdn_dt_bias, v_gdn_norm_g, v_gdn_w_out, v_mlp_w1, v_mlp_w2)))
    me = 4 * lax.axis_index("x") + 2 * lax.axis_index("y") + lax.axis_index("c")

    core = lax.axis_index("c").astype(jnp.int32).reshape(1)
    d_model = x.shape[-1]

    shards = {n: _squeeze_layer(n, w_in[n]) for n in WEIGHTS}
    params = {n: shards[n] for n in REPLICATED}
    params.update(_gather_weights(shards))

    loss_part, grad_x, gr = _local_step(x, loss_target, params)
    loss = lax.psum(loss_part, ("x", "y", "c"))

    def row_blocks(a):
        return a.reshape(N_DEV, a.shape[0] // N_DEV, a.shape[1])

    def flat_blocks(a):
        k, n8 = a.shape
        return _cols_to_blocks(a).reshape(N_DEV, k * (n8 // N_DEV) // LANES, LANES)

    big = [("cv_w_pw1", 0, gr["cv_w_pw1"]), ("cv_w_pw2", 0, row_blocks(gr["cv_w_pw2"])),
           ("gdn_w_in", 0, flat_blocks(gr["gdn_w_in"])), ("gdn_w_out", 0, row_blocks(gr["gdn_w_out"])),
           ("mlp_w1", 0, gr["mlp_w1"][0]), ("mlp_w1", 1, gr["mlp_w1"][1]),
           ("mlp_w2", 0, row_blocks(gr["mlp_w2"][0])), ("mlp_w2", 1, row_blocks(gr["mlp_w2"][1]))]
    g_blocks = [g for _, _, g in big]
    from_sibling = _pair_exchange(g_blocks)
    chip_sums = [_pair_sum(g, r1, core, f"grads_pair_sum_{i}") for i, (g, r1) in enumerate(zip(g_blocks, from_sibling))]

    small = REPLICATED + CONV_SHARDED
    small_send, ssz = _pack([gr[n] for n in small])
    *from_chips, small_recv = _chip_exchange(chip_sums, small_send)
    small_sum = _sum_devices(small_recv)
    grads = dict(zip(small, _unpack(small_sum, ssz, [gr[n].shape for n in small])))
    for n in CONV_SHARDED:
        cn = shards[n].shape[-1]
        grads[n] = lax.dynamic_slice_in_dim(grads[n], me * cn, cn, axis=1)

    def as3d(n, a):
        if n == "gdn_w_in":
            return a.reshape(a.shape[0], -1, LANES)
        return a

    res = {n: {} for n in MATMUL_SHARDED}
    for (n, layer, _), r2 in zip(big, from_chips):
        res[n][layer] = _sum4_adamw(r2, as3d(n, w_in[n]), as3d(n, m_in[n]), as3d(n, v_in[n]), layer, f"adamw_{n}_{layer}")
    out_groups = {n: [] for n in WEIGHTS}
    for n in MATMUL_SHARDED:
        layers = sorted(res[n])
        for k in range(4):
            pieces = [res[n][layer][k] for layer in layers]
            out_groups[n].append(jnp.stack(pieces).reshape(w_in[n].shape))

    sm_w = [shards[n] for n in small]
    sm_g = [grads[n].reshape(shards[n].shape) for n in small]
    sm_m = [_squeeze_layer(n, m_in[n]) for n in small]
    sm_v = [_squeeze_layer(n, v_in[n]) for n in small]
    wp, psz = _pack(sm_w)
    gp, _ = _pack(sm_g)
    mp, _ = _pack(sm_m)
    vp, _ = _pack(sm_v)
    dp, nmp, nvp = _adamw(wp, gp, mp, vp)
    shp = [a.shape for a in sm_w]
    for n, g, dl, nm, nv in zip(small, sm_g, _unpack(dp, psz, shp), _unpack(nmp, psz, shp), _unpack(nvp, psz, shp)):
        out_groups[n] = [a.reshape(w_in[n].shape) for a in (g, dl, nm, nv)]

    outs = [loss, grad_x]
    for k in range(4):
        outs += [out_groups[n][k] for n in WEIGHTS]
    return tuple(outs)
```

```python
import functools

import jax
import jax.numpy as jnp
from jax import lax
from jax.experimental import pallas as pl
from jax.experimental.pallas import tpu as pltpu

f32, bf16 = jnp.float32, jnp.bfloat16

NORM_EPS = 1e-6
L2_EPS = 1e-6
CHUNK = 64
LANES = 128
SUBLANES = 8
N_DEV = 8
VMEM_LIMIT = 56 * 1024 * 1024
CONV_PAD = 32
HEADS_PER_STEP = 4
PREP_HEADS_PER_STEP = 2
NEG = -1e30

ADAM_LR, ADAM_B1, ADAM_B2, ADAM_EPS, ADAM_WD, ADAM_STEP = 0.001, 0.9, 0.999, 1e-08, 0.01, 10

NT = (((1,), (1,)), ((), ()))
TN = (((0,), (0,)), ((), ()))
HI = lax.Precision.HIGHEST


def _pc(body, *, name, grid, in_specs, out_specs, out_shape, scratch=(), sem=None):
    return pl.pallas_call(
        body, name=name, grid=grid, in_specs=in_specs, out_specs=out_specs, out_shape=out_shape,
        scratch_shapes=list(scratch),
        compiler_params=pltpu.CompilerParams(dimension_semantics=sem, vmem_limit_bytes=VMEM_LIMIT))


def _rows(tm, n):
    return pl.BlockSpec((tm, n), lambda i: (i, 0))


def _const(shape):
    return pl.BlockSpec(shape, lambda *_: (0,) * len(shape))


def _resident(shape):
    return pl.BlockSpec(shape, lambda *_: (0,) * len(shape), pipeline_mode=pl.Buffered(1))


def _tile(t, pref):
    return pref if t % pref == 0 else t


def _dot(a, b):
    return jnp.dot(a.astype(bf16), b.astype(bf16), preferred_element_type=f32)


def _dot_nt(a, b):
    return lax.dot_general(a.astype(bf16), b.astype(bf16), NT, preferred_element_type=f32)


def _dot_tn(a, b):
    return lax.dot_general(a.astype(bf16), b.astype(bf16), TN, preferred_element_type=f32)


def _sigmoid(x):
    return 1.0 / (1.0 + jnp.exp(-x))


def _silu_grad(x):
    s = _sigmoid(x)
    return s * (1.0 + x * (1.0 - s))


def _rms(x, g):
    rstd = lax.rsqrt(jnp.mean(x * x, axis=-1, keepdims=True) + NORM_EPS)
    xh = x * rstd
    return xh * g, xh, rstd


def _rms_bwd(dn, xh, rstd, g):
    dxh = dn * g
    return rstd * (dxh - xh * jnp.mean(dxh * xh, axis=-1, keepdims=True))


def _acc_init(step, *refs):
    @pl.when(step == 0)
    def _():
        for r in refs:
            r[...] = jnp.zeros(r.shape, r.dtype)


def _acc_rows(ref, val):
    ref[0:1, :] += jnp.sum(val, axis=0, keepdims=True)


def _pw1_glu(x, g, w, b):
    t, d = x.shape
    tm = _tile(t, 256)

    nb_w = w.shape[0]

    def body(x_ref, g_ref, w_hbm, b_ref, n_ref, u_ref, gl_ref, w_ref, sems):
        _fetch_blocks(pl.program_id(0), w_hbm, w_ref, sems, True)
        n, _, _ = _rms(x_ref[...], g_ref[...])
        nb = n.astype(bf16)
        n_ref[...] = nb
        u = jnp.dot(nb, w_ref[...], preferred_element_type=f32) + b_ref[...]
        u_ref[...] = u.astype(bf16)
        gl_ref[...] = u[:, :d] * _sigmoid(u[:, d:])

    return _pc(body, name="pw1_glu", grid=(t // tm,),
               in_specs=[_rows(tm, d), _const((1, d)), ANY, _const((1, 2 * d))],
               out_specs=[_rows(tm, d), _rows(tm, 2 * d), _rows(tm, d)],
               out_shape=[jax.ShapeDtypeStruct((t, d), bf16), jax.ShapeDtypeStruct((t, 2 * d), bf16),
                          jax.ShapeDtypeStruct((t, d), f32)],
               scratch=[pltpu.VMEM((d, 2 * d), bf16), pltpu.SemaphoreType.DMA((nb_w,))],
               sem=("arbitrary",))(x, g, w, b)


def _ln_silu_mm_res(dc, ln_g, ln_b, w, b, res):
    t, d = dc.shape
    tm = _tile(t, 256)

    def body(x_ref, g_ref, bb_ref, w_ref, b_ref, r_ref, s_ref, o_ref):
        x = x_ref[...]
        xc = x - jnp.mean(x, axis=-1, keepdims=True)
        rstd = lax.rsqrt(jnp.mean(xc * xc, axis=-1, keepdims=True) + NORM_EPS)
        ln = xc * rstd * g_ref[...] + bb_ref[...]
        sb = (ln * _sigmoid(ln)).astype(bf16)
        s_ref[...] = sb
        o_ref[...] = r_ref[...] + jnp.dot(sb, w_ref[...], preferred_element_type=f32) + b_ref[...]

    return _pc(body, name="ln_silu_pw2", grid=(t // tm,),
               in_specs=[_rows(tm, d), _const((1, d)), _const((1, d)), _resident((d, d)), _const((1, d)), _rows(tm, d)],
               out_specs=[_rows(tm, d), _rows(tm, d)],
               out_shape=[jax.ShapeDtypeStruct((t, d), bf16), jax.ShapeDtypeStruct((t, d), f32)],
               sem=("parallel",))(dc, ln_g, ln_b, w, b, res)


def _fetch_blocks(step, w_hbm, dst, sems, by_cols, layer=None):
    nb_w = w_hbm.shape[0]
    step_rows, step_cols = w_hbm.shape[-2], w_hbm.shape[-1]

    @pl.when(step == 0)
    def _():
        copies = []
        for j in range(nb_w):
            src = w_hbm.at[j] if layer is None else w_hbm.at[j, layer]
            if by_cols:
                part = dst.at[:, pl.ds(j * step_cols, step_cols)]
            else:
                part = dst.at[pl.ds(j * step_rows, step_rows), :]
            copies.append(pltpu.make_async_copy(src, part, sems.at[j]))
        for cp in copies:
            cp.start()
        for cp in copies:
            cp.wait()


def _mlp_fwd(h, g, w1g, w2g, layer, name):
    t, d = h.shape
    nb_w, _, _, bn = w1g.shape
    ff = nb_w * bn
    tm = _tile(t, 256)

    def body(h_ref, g_ref, w1_hbm, w2_hbm, n_ref, f_ref, r_ref, o_ref, w1_ref, w2_ref, sem1, sem2):
        _fetch_blocks(pl.program_id(0), w1_hbm, w1_ref, sem1, True, layer)
        _fetch_blocks(pl.program_id(0), w2_hbm, w2_ref, sem2, False, layer)
        hv = h_ref[...]
        n, _, _ = _rms(hv, g_ref[...])
        nb = n.astype(bf16)
        n_ref[...] = nb
        f = jnp.dot(nb, w1_ref[...], preferred_element_type=f32)
        f_ref[...] = f.astype(bf16)
        rb = jnp.square(jnp.maximum(f, 0.0)).astype(bf16)
        r_ref[...] = rb
        o_ref[...] = hv + jnp.dot(rb, w2_ref[...], preferred_element_type=f32)

    return _pc(body, name=name, grid=(t // tm,),
               in_specs=[_rows(tm, d), _const((1, d)), ANY, ANY],
               out_specs=[_rows(tm, d), _rows(tm, ff), _rows(tm, ff), _rows(tm, d)],
               out_shape=[jax.ShapeDtypeStruct((t, d), bf16), jax.ShapeDtypeStruct((t, ff), bf16),
                          jax.ShapeDtypeStruct((t, ff), bf16), jax.ShapeDtypeStruct((t, d), f32)],
               scratch=[pltpu.VMEM((d, ff), bf16), pltpu.VMEM((ff, d), bf16),
                        pltpu.SemaphoreType.DMA((nb_w,)), pltpu.SemaphoreType.DMA((nb_w,))],
               sem=("arbitrary",))(h, g, w1g, w2g)


def _softplus(x):
    return jnp.maximum(x, 0.0) + jnp.log(1.0 + jnp.exp(-jnp.abs(x)))


def _gdn_in(h, g, w_main, w_ab, a_log_pad, dt_pad, n_heads):
    t, d = h.shape
    tm = _tile(t, 256)

    def body(h_ref, g_ref, wm_ref, wab_ref, al_ref, dt_ref, n_ref, qkv_ref, z_ref, ab_ref, gb_ref):
        n, _, _ = _rms(h_ref[...], g_ref[...])
        nb = n.astype(bf16)
        n_ref[...] = nb
        p = jnp.dot(nb, wm_ref[...], preferred_element_type=f32)
        qkv_ref[...] = p[:, :3 * d]
        z_ref[...] = p[:, 3 * d:]
        ab = jnp.dot(nb, wab_ref[...], preferred_element_type=f32)
        ab_ref[...] = ab
        lane = lax.broadcasted_iota(jnp.int32, ab.shape, 1)
        decay = -jnp.exp(al_ref[...]) * _softplus(ab + dt_ref[...])
        gb_ref[...] = jnp.where(lane < n_heads, decay, jnp.where(lane < 2 * n_heads, _sigmoid(ab), 0.0))

    return _pc(body, name="gdn_in", grid=(t // tm,),
               in_specs=[_rows(tm, d), _const((1, d)), _resident((d, 4 * d)), _resident((d, LANES)),
                         _const((1, LANES)), _const((1, LANES))],
               out_specs=[_rows(tm, d), _rows(tm, 3 * d), _rows(tm, d), _rows(tm, LANES), _rows(tm, LANES)],
               out_shape=[jax.ShapeDtypeStruct((t, d), bf16), jax.ShapeDtypeStruct((t, 3 * d), f32),
                          jax.ShapeDtypeStruct((t, d), f32), jax.ShapeDtypeStruct((t, LANES), f32),
                          jax.ShapeDtypeStruct((t, LANES), f32)],
               sem=("parallel",))(h, g, w_main, w_ab, a_log_pad, dt_pad)


def _gated_norm_mm_res(o, z, ng, w, res, n_heads):
    t, d = o.shape
    tm = _tile(t, 256)

    def body(o_ref, z_ref, ng_ref, w_ref, r_ref, on_ref, out_ref):
        for hd in range(n_heads):
            sl = slice(hd * LANES, (hd + 1) * LANES)
            rn, _, _ = _rms(o_ref[:, sl], ng_ref[...])
            zz = z_ref[:, sl]
            on_ref[:, sl] = (rn * (zz * _sigmoid(zz))).astype(bf16)
        out_ref[...] = r_ref[...] + jnp.dot(on_ref[...], w_ref[...], preferred_element_type=f32)

    return _pc(body, name="gated_norm_wout", grid=(t // tm,),
               in_specs=[_rows(tm, d), _rows(tm, d), _const((1, LANES)), _resident((d, d)), _rows(tm, d)],
               out_specs=[_rows(tm, d), _rows(tm, d)],
               out_shape=[jax.ShapeDtypeStruct((t, d), bf16), jax.ShapeDtypeStruct((t, d), f32)],
               sem=("parallel",))(o, z, ng, w, res)


def _loss_head(h, g, tgt):
    t, d = h.shape
    tm = _tile(t, 256)

    def body(h_ref, g_ref, t_ref, loss_ref, dh_ref, dg_ref):
        _acc_init(pl.program_id(0), loss_ref, dg_ref)
        gv = g_ref[...]
        y, xh, rstd = _rms(h_ref[...], gv)
        e = y - t_ref[...]
        loss_ref[...] += 0.5 * jnp.sum(jnp.mean(e * e, axis=-1, keepdims=True))
        dy = e * (1.0 / d)
        _acc_rows(dg_ref, dy * xh)
        dh_ref[...] = _rms_bwd(dy, xh, rstd, gv)

    return _pc(body, name="loss_head", grid=(t // tm,),
               in_specs=[_rows(tm, d), _const((1, d)), _rows(tm, d)],
               out_specs=[_const((SUBLANES, LANES)), _rows(tm, d), _const((SUBLANES, d))],
               out_shape=[jax.ShapeDtypeStruct((SUBLANES, LANES), f32), jax.ShapeDtypeStruct((t, d), f32),
                          jax.ShapeDtypeStruct((SUBLANES, d), f32)],
               sem=("arbitrary",))(h, g, tgt)


def _rms_bwd_res(dn, h, g, dres, name):
    t, d = h.shape
    tm = _tile(t, 256)

    def body(dn_ref, h_ref, g_ref, dr_ref, dh_ref, dg_ref):
        _acc_init(pl.program_id(0), dg_ref)
        gv = g_ref[...]
        _, xh, rstd = _rms(h_ref[...], gv)
        dn = dn_ref[...]
        _acc_rows(dg_ref, dn * xh)
        dh_ref[...] = dr_ref[...] + _rms_bwd(dn, xh, rstd, gv)

    return _pc(body, name=name, grid=(t // tm,),
               in_specs=[_rows(tm, d), _rows(tm, d), _const((1, d)), _rows(tm, d)],
               out_specs=[_rows(tm, d), _const((SUBLANES, d))],
               out_shape=[jax.ShapeDtypeStruct((t, d), f32), jax.ShapeDtypeStruct((SUBLANES, d), f32)],
               sem=("arbitrary",))(dn, h, g, dres)


def _mlp_bwd(dho, h, g, fb, w1g, w2g, layer, name):
    t, d = h.shape
    nb_w, _, _, bn = w1g.shape
    ff = nb_w * bn
    tm = _tile(t, 256)

    def body(do_ref, h_ref, g_ref, f_ref, w1_hbm, w2_hbm, df_ref, dh_ref, dg_ref, cs_ref, w1_ref, w2_ref, sem1, sem2):
        _fetch_blocks(pl.program_id(0), w1_hbm, w1_ref, sem1, True, layer)
        _fetch_blocks(pl.program_id(0), w2_hbm, w2_ref, sem2, False, layer)
        _acc_init(pl.program_id(0), dg_ref, cs_ref)
        do = do_ref[...]
        dr = lax.dot_general(do.astype(bf16), w2_ref[...], NT, preferred_element_type=f32)
        dfb = (dr * (2.0 * jnp.maximum(f_ref[...].astype(f32), 0.0))).astype(bf16)
        df_ref[...] = dfb
        dn = lax.dot_general(dfb, w1_ref[...], NT, preferred_element_type=f32)
        gv = g_ref[...]
        _, xh, rstd = _rms(h_ref[...], gv)
        _acc_rows(dg_ref, dn * xh)
        dh = do + _rms_bwd(dn, xh, rstd, gv)
        dh_ref[...] = dh
        _acc_rows(cs_ref, dh)

    return _pc(body, name=name, grid=(t // tm,),
               in_specs=[_rows(tm, d), _rows(tm, d), _const((1, d)), _rows(tm, ff), ANY, ANY],
               out_specs=[_rows(tm, ff), _rows(tm, d), _const((SUBLANES, d)), _const((SUBLANES, d))],
               out_shape=[jax.ShapeDtypeStruct((t, ff), bf16), jax.ShapeDtypeStruct((t, d), f32),
                          jax.ShapeDtypeStruct((SUBLANES, d), f32), jax.ShapeDtypeStruct((SUBLANES, d), f32)],
               scratch=[pltpu.VMEM((d, ff), bf16), pltpu.VMEM((ff, d), bf16),
                        pltpu.SemaphoreType.DMA((nb_w,)), pltpu.SemaphoreType.DMA((nb_w,))],
               sem=("arbitrary",))(dho, h, g, fb, w1g, w2g)


def _mm_nt(pairs, name):
    t = pairs[0][0].shape[0]
    w0 = pairs[0][1]
    k = (w0[0] if isinstance(w0, tuple) else w0).shape[0]
    tm = _tile(t, 256)
    npair = len(pairs)

    def body(*refs):
        o_ref = refs[2 * npair]
        acc = None
        for p in range(npair):
            part = lax.dot_general(refs[2 * p][...].astype(bf16), refs[2 * p + 1][...], NT, preferred_element_type=f32)
            acc = part if acc is None else acc + part
        o_ref[...] = acc

    in_specs, args = [], []
    for dy, w in pairs:
        nn = dy.shape[1]
        if isinstance(w, tuple):
            w, idx = w
            wspec = pl.BlockSpec((k, nn), lambda *_, idx=idx: (0, idx), pipeline_mode=pl.Buffered(1))
        else:
            wspec = _resident(w.shape)
        in_specs += [_rows(tm, nn), wspec]
        args += [dy, w]
    return _pc(body, name=name, grid=(t // tm,), in_specs=in_specs, out_specs=_rows(tm, k),
               out_shape=jax.ShapeDtypeStruct((t, k), f32), sem=("parallel",))(*args)


def _mm_nt_blocked(dy, wb, name):
    t = dy.shape[0]
    nb_w, k, bn = wb.shape
    tm = _tile(t, 256)

    def body(dy_ref, w_hbm, o_ref, w_ref, sems):
        _fetch_blocks(pl.program_id(0), w_hbm, w_ref, sems, True)
        o_ref[...] = lax.dot_general(dy_ref[...].astype(bf16), w_ref[...], NT, preferred_element_type=f32)

    return _pc(body, name=name, grid=(t // tm,), in_specs=[_rows(tm, nb_w * bn), ANY],
               out_specs=_rows(tm, k), out_shape=jax.ShapeDtypeStruct((t, k), f32),
               scratch=[pltpu.VMEM((k, nb_w * bn), bf16), pltpu.SemaphoreType.DMA((nb_w,))],
               sem=("arbitrary",))(dy, wb)


def _mm_tn_blocked(x, dy, name):
    t, k = x.shape
    bn = dy.shape[1] // N_DEV
    tm = _tile(t, 512)
    jb = N_DEV
    while jb > 1 and k * jb * bn * 4 > 8 * 1024 * 1024:
        jb //= 2

    def body(x_ref, dy_ref, o_ref):
        _acc_init(pl.program_id(1), o_ref)
        xt = x_ref[...].astype(bf16).T
        for jj in range(jb):
            o_ref[jj] += jnp.dot(xt, dy_ref[:, jj * bn:(jj + 1) * bn].astype(bf16), preferred_element_type=f32)

    return _pc(body, name=name, grid=(N_DEV // jb, t // tm),
               in_specs=[pl.BlockSpec((tm, k), lambda j, i: (i, 0)), pl.BlockSpec((tm, jb * bn), lambda j, i: (i, j))],
               out_specs=pl.BlockSpec((jb, k, bn), lambda j, i: (j, 0, 0)),
               out_shape=jax.ShapeDtypeStruct((N_DEV, k, bn), f32), sem=("parallel", "arbitrary"))(x, dy)


def _mm_tn(x, dy, name):
    t, k = x.shape
    n = dy.shape[1]
    tm = _tile(t, 512)
    cap = max(LANES, (2 * 1024 * 1024) // k)
    tn = n
    if n > cap:
        tn = max(c for c in range(LANES, cap + 1, LANES) if n % c == 0)

    def body(x_ref, dy_ref, o_ref):
        _acc_init(pl.program_id(1), o_ref)
        o_ref[...] += lax.dot_general(x_ref[...].astype(bf16), dy_ref[...].astype(bf16), TN, preferred_element_type=f32)

    return _pc(body, name=name, grid=(n // tn, t // tm),
               in_specs=[pl.BlockSpec((tm, k), lambda j, i: (i, 0)), pl.BlockSpec((tm, tn), lambda j, i: (i, j))],
               out_specs=pl.BlockSpec((k, tn), lambda j, i: (0, j)),
               out_shape=jax.ShapeDtypeStruct((k, n), f32), sem=("parallel", "arbitrary"))(x, dy)


def _gated_norm_bwd(don, o, z, ng, n_heads):
    t, d = o.shape
    tm = _tile(t, 256)

    def body(don_ref, o_ref, z_ref, ng_ref, do_ref, dz_ref, dng_ref):
        _acc_init(pl.program_id(0), dng_ref)
        gv = ng_ref[...]
        for hd in range(n_heads):
            sl = slice(hd * LANES, (hd + 1) * LANES)
            rn, xh, rstd = _rms(o_ref[:, sl], gv)
            zz = z_ref[:, sl]
            don = don_ref[:, sl]
            dz_ref[:, sl] = don * rn * _silu_grad(zz)
            drn = don * (zz * _sigmoid(zz))
            _acc_rows(dng_ref, drn * xh)
            do_ref[:, sl] = _rms_bwd(drn, xh, rstd, gv)

    return _pc(body, name="gated_norm_bwd", grid=(t // tm,),
               in_specs=[_rows(tm, d), _rows(tm, d), _rows(tm, d), _const((1, LANES))],
               out_specs=[_rows(tm, d), _rows(tm, d), _const((SUBLANES, LANES))],
               out_shape=[jax.ShapeDtypeStruct((t, d), f32), jax.ShapeDtypeStruct((t, d), f32),
                          jax.ShapeDtypeStruct((SUBLANES, LANES), f32)],
               sem=("arbitrary",))(don, o, z, ng)


def _gates_bwd(dgb, ab, a_log_pad, dt_pad, n_heads):
    t = ab.shape[0]
    tm = _tile(t, 256)

    def body(dgb_ref, ab_ref, al_ref, dt_ref, dab_ref, dal_ref, ddt_ref):
        _acc_init(pl.program_id(0), dal_ref, ddt_ref)
        ab = ab_ref[...]
        dgb = dgb_ref[...]
        lane = lax.broadcasted_iota(jnp.int32, ab.shape, 1)
        is_a = lane < n_heads
        is_b = jnp.logical_and(lane >= n_heads, lane < 2 * n_heads)
        xa = ab + dt_ref[...]
        neg_a = -jnp.exp(al_ref[...])
        dg_da = neg_a * _sigmoid(xa)
        beta = _sigmoid(ab)
        da = jnp.where(is_a, dgb * dg_da, 0.0)
        dab_ref[...] = da + jnp.where(is_b, dgb * beta * (1.0 - beta), 0.0)
        _acc_rows(dal_ref, jnp.where(is_a, dgb * neg_a * _softplus(xa), 0.0))
        _acc_rows(ddt_ref, da)

    return _pc(body, name="gates_bwd", grid=(t // tm,),
               in_specs=[_rows(tm, LANES), _rows(tm, LANES), _const((1, LANES)), _const((1, LANES))],
               out_specs=[_rows(tm, LANES), _const((SUBLANES, LANES)), _const((SUBLANES, LANES))],
               out_shape=[jax.ShapeDtypeStruct((t, LANES), f32), jax.ShapeDtypeStruct((SUBLANES, LANES), f32),
                          jax.ShapeDtypeStruct((SUBLANES, LANES), f32)],
               sem=("arbitrary",))(dgb, ab, a_log_pad, dt_pad)


def _ln_silu_bwd(ds, dc, ln_g, ln_b):
    t, d = dc.shape
    tm = _tile(t, 256)

    def body(ds_ref, x_ref, g_ref, b_ref, dx_ref, dg_ref, db_ref, cs_ref):
        _acc_init(pl.program_id(0), dg_ref, db_ref, cs_ref)
        x = x_ref[...]
        gv = g_ref[...]
        xc = x - jnp.mean(x, axis=-1, keepdims=True)
        rstd = lax.rsqrt(jnp.mean(xc * xc, axis=-1, keepdims=True) + NORM_EPS)
        xh = xc * rstd
        ln = xh * gv + b_ref[...]
        dln = ds_ref[...] * _silu_grad(ln)
        _acc_rows(dg_ref, dln * xh)
        _acc_rows(db_ref, dln)
        dxh = dln * gv
        dx = rstd * (dxh - jnp.mean(dxh, axis=-1, keepdims=True) - xh * jnp.mean(dxh * xh, axis=-1, keepdims=True))
        dx_ref[...] = dx
        _acc_rows(cs_ref, dx)

    return _pc(body, name="ln_silu_bwd", grid=(t // tm,),
               in_specs=[_rows(tm, d), _rows(tm, d), _const((1, d)), _const((1, d))],
               out_specs=[_rows(tm, d), _const((SUBLANES, d)), _const((SUBLANES, d)), _const((SUBLANES, d))],
               out_shape=[jax.ShapeDtypeStruct((t, d), f32)] + [jax.ShapeDtypeStruct((SUBLANES, d), f32)] * 3,
               sem=("arbitrary",))(ds, dc, ln_g, ln_b)


def _glu_bwd(dgl, ub):
    t, d = dgl.shape
    tm = _tile(t, 256)

    def body(dgl_ref, u_ref, du_ref, cs_ref):
        _acc_init(pl.program_id(0), cs_ref)
        dgl = dgl_ref[...]
        a = u_ref[:, :d].astype(f32)
        sb = _sigmoid(u_ref[:, d:].astype(f32))
        da = dgl * sb
        db = dgl * a * sb * (1.0 - sb)
        du_ref[:, :d] = da.astype(bf16)
        du_ref[:, d:] = db.astype(bf16)
        cs_ref[0:1, :d] += jnp.sum(da, axis=0, keepdims=True)
        cs_ref[0:1, d:] += jnp.sum(db, axis=0, keepdims=True)

    return _pc(body, name="glu_bwd", grid=(t // tm,),
               in_specs=[_rows(tm, d), _rows(tm, 2 * d)],
               out_specs=[_rows(tm, 2 * d), _const((SUBLANES, 2 * d))],
               out_shape=[jax.ShapeDtypeStruct((t, 2 * d), bf16), jax.ShapeDtypeStruct((SUBLANES, 2 * d), f32)],
               sem=("arbitrary",))(dgl, ub)


def _conv_rows(s):
    return 256 if s % 256 == 0 else s


def _conv_tap_sum(pad_ref, w_ref, base, rows, width):
    acc = jnp.zeros((rows, LANES), f32)
    for j in range(width):
        acc = acc + w_ref[j:j + 1, :] * pad_ref[pl.ds(base + CONV_PAD - (width - 1) + j, rows), :]
    return acc


def _l2_silu_post(c, j, n_heads, scale):
    a = c * _sigmoid(c)
    r = lax.rsqrt(jnp.sum(a * a, axis=-1, keepdims=True) + L2_EPS)
    mult = jnp.where(j < n_heads, r * scale, jnp.where(j < 2 * n_heads, r, 1.0))
    return a, r, a * mult


def _dwconv_fwd(x, w, b, name, qk_heads=None):
    bl, s, cn = x.shape
    width = w.shape[0]
    rows = _conv_rows(s)
    scale = float(LANES) ** -0.5

    def body(x_ref, w_ref, b_ref, o_ref, pad_ref):
        j = pl.program_id(1)
        pad_ref[0:CONV_PAD, :] = jnp.zeros((CONV_PAD, LANES), f32)
        pad_ref[CONV_PAD:, :] = x_ref[0]

        def step(i, carry):
            base = pl.multiple_of(i * rows, rows)
            acc = _conv_tap_sum(pad_ref, w_ref, base, rows, width)
            if qk_heads is None:
                acc = acc + b_ref[...]
            else:
                _, _, acc = _l2_silu_post(acc, j, qk_heads, scale)
            o_ref[0, pl.ds(base, rows), :] = acc
            return carry

        lax.fori_loop(0, s // rows, step, 0)

    return _pc(body, name=name, grid=(bl, cn // LANES),
               in_specs=[pl.BlockSpec((1, s, LANES), lambda bi, j: (bi, 0, j)),
                         pl.BlockSpec((width, LANES), lambda bi, j: (0, j)),
                         pl.BlockSpec((1, LANES), lambda bi, j: (0, j))],
               out_specs=pl.BlockSpec((1, s, LANES), lambda bi, j: (bi, 0, j)),
               out_shape=jax.ShapeDtypeStruct((bl, s, cn), f32),
               scratch=[pltpu.VMEM((s + CONV_PAD, LANES), f32)],
               sem=("parallel", "parallel"))(x, w, b)


def _dwconv_bwd(x, dys, w, name, qk_heads=None):
    bl, s, cn = x.shape
    width = w.shape[0]
    wp = -(-width // SUBLANES) * SUBLANES
    rows = _conv_rows(s)
    scale = float(LANES) ** -0.5
    nblk = s // rows
    ndy = len(dys)

    def body(*refs):
        x_ref, w_ref = refs[0], refs[1]
        dy_refs = refs[2:2 + ndy]
        dx_ref, dw_ref, db_ref, xpad, dypad, acc = refs[2 + ndy:]
        j = pl.program_id(0)
        bi = pl.program_id(1)
        _acc_init(bi, acc, db_ref)
        xpad[0:CONV_PAD, :] = jnp.zeros((CONV_PAD, LANES), f32)
        xpad[CONV_PAD:, :] = x_ref[0]
        dypad[s:, :] = jnp.zeros((CONV_PAD, LANES), f32)
        if qk_heads is None:
            dypad[0:s, :] = dy_refs[0][0]
        else:
            def pre(i, carry):
                base = pl.multiple_of(i * rows, rows)
                c = _conv_tap_sum(xpad, w_ref, base, rows, width)
                a, r, _ = _l2_silu_post(c, j, qk_heads, scale)
                dq = dy_refs[0][0, pl.ds(base, rows), :]
                dk = dy_refs[1][0, pl.ds(base, rows), :]
                dv = dy_refs[2][0, pl.ds(base, rows), :]
                dy = jnp.where(j < qk_heads, dq * scale, jnp.where(j < 2 * qk_heads, dk, dv))
                da_l2 = r * (dy - a * (r * r) * jnp.sum(a * dy, axis=-1, keepdims=True))
                da = jnp.where(j < 2 * qk_heads, da_l2, dy)
                dypad[pl.ds(base, rows), :] = da * _silu_grad(c)
                return carry

            lax.fori_loop(0, nblk, pre, 0)

        def step(i, carry):
            base = pl.multiple_of(i * rows, rows)
            dxa = jnp.zeros((rows, LANES), f32)
            for jj in range(width):
                dxa = dxa + w_ref[jj:jj + 1, :] * dypad[pl.ds(base + (width - 1) - jj, rows), :]
            dx_ref[0, pl.ds(base, rows), :] = dxa
            dyc = dypad[pl.ds(base, rows), :]
            db_ref[...] += dyc.reshape(rows // SUBLANES, SUBLANES, LANES).sum(axis=0)
            for jj in range(width):
                prod = dyc * xpad[pl.ds(base + CONV_PAD - (width - 1) + jj, rows), :]
                acc[jj * SUBLANES:(jj + 1) * SUBLANES, :] += prod.reshape(rows // SUBLANES, SUBLANES, LANES).sum(axis=0)
            return carry

        lax.fori_loop(0, nblk, step, 0)

        @pl.when(bi == bl - 1)
        def _():
            dw_ref[...] = jnp.zeros((wp, LANES), f32)
            for jj in range(width):
                dw_ref[jj:jj + 1, :] = jnp.sum(acc[jj * SUBLANES:(jj + 1) * SUBLANES, :], axis=0, keepdims=True)

    if qk_heads is None:
        dy_specs = [pl.BlockSpec((1, s, LANES), lambda j, bi: (bi, 0, j))]
    else:
        hh = qk_heads
        dy_specs = [pl.BlockSpec((1, s, LANES), lambda j, bi: (bi * hh + jnp.minimum(j, hh - 1), 0, 0)),
                    pl.BlockSpec((1, s, LANES), lambda j, bi: (bi * hh + jnp.clip(j - hh, 0, hh - 1), 0, 0)),
                    pl.BlockSpec((1, s, LANES), lambda j, bi: (bi * hh + jnp.clip(j - 2 * hh, 0, hh - 1), 0, 0))]
    return _pc(body, name=name, grid=(cn // LANES, bl),
               in_specs=[pl.BlockSpec((1, s, LANES), lambda j, bi: (bi, 0, j)),
                         pl.BlockSpec((width, LANES), lambda j, bi: (0, j))] + dy_specs,
               out_specs=[pl.BlockSpec((1, s, LANES), lambda j, bi: (bi, 0, j)),
                          pl.BlockSpec((wp, LANES), lambda j, bi: (0, j)),
                          pl.BlockSpec((SUBLANES, LANES), lambda j, bi: (0, j))],
               out_shape=[jax.ShapeDtypeStruct((bl, s, cn), f32), jax.ShapeDtypeStruct((wp, cn), f32),
                          jax.ShapeDtypeStruct((SUBLANES, cn), f32)],
               scratch=[pltpu.VMEM((s + CONV_PAD, LANES), f32), pltpu.VMEM((s + CONV_PAD, LANES), f32),
                        pltpu.VMEM((width * SUBLANES, LANES), f32)],
               sem=("parallel", "arbitrary"))(x, w, *dys)


def _group_rows(s):
    for rows in (256, 128):
        if s % rows == 0:
            return rows
    return CHUNK


def _group_masks(rows):
    r = lax.broadcasted_iota(jnp.int32, (rows, rows), 0)
    c = lax.broadcasted_iota(jnp.int32, (rows, rows), 1)
    same_chunk = (r >> 6) == (c >> 6)
    return r, c, same_chunk


def _decay(gc_col, gc_row, causal):
    return jnp.exp(jnp.where(causal, gc_col - gc_row, NEG))


def _inv_unit_lower(a, r, c):
    return _inv_unit_lower_many([a], r, c)[0]


def _inv_unit_lower_many(mats, r, c):
    eye = (r == c).astype(f32)
    same16 = (r >> 4) == (c >> 4)
    ads = [jnp.where(same16, a, 0.0) for a in mats]
    aos = [a - ad for a, ad in zip(mats, ads)]
    xs = ads
    tds = [eye - x for x in xs]
    for _ in range(3):
        xs = [_dot(x, x) for x in xs]
        tds = [td + _dot(td, x) for td, x in zip(tds, xs)]
    bs = [_dot(td, ao) for td, ao in zip(tds, aos)]
    b2s = [_dot(b, b) for b in bs]
    b3s = [_dot(b, b2) for b, b2 in zip(bs, b2s)]
    return [_dot(eye - b + b2 - b3, td) for b, b2, b3, td in zip(bs, b2s, b3s, tds)]


def _lane_cumsum(x, y, reverse, name):
    rr = x.shape[0]

    def body(x_ref, y_ref, o_ref):
        i = lax.broadcasted_iota(jnp.int32, (CHUNK, CHUNK), 0)
        j = lax.broadcasted_iota(jnp.int32, (CHUNK, CHUNK), 1)
        tri = ((i >= j) if reverse else (i <= j)).astype(f32)
        o_ref[...] = jnp.dot(x_ref[...] + y_ref[...], tri, precision=HI, preferred_element_type=f32)

    spec = pl.BlockSpec((rr, CHUNK), lambda: (0, 0))
    return pl.pallas_call(body, name=name, in_specs=[spec, spec], out_specs=spec,
                          out_shape=jax.ShapeDtypeStruct((rr, CHUNK), f32))(x, y)


def _gdn_specs(n_heads, nblk, rows, hp=1, rev=False):
    def blk(n):
        return nblk - 1 - n if rev else n

    def qkv(off):
        return pl.BlockSpec((rows, hp * LANES), lambda g, n: (
            lax.div(g * hp, n_heads) * nblk + blk(n), lax.div(off * n_heads + lax.rem(g * hp, n_heads), hp)))

    def per_head(last, mult=1):
        return pl.BlockSpec((hp, rows * mult, last), lambda g, n: (g, blk(n), 0))

    def row_vec():
        return pl.BlockSpec((hp, 1, 1, rows), lambda g, n: (g, blk(n), 0, 0))

    return qkv, per_head, row_vec


def _gdn_prep(qkv, cols, grow, bl, s, n_heads):
    rows = _group_rows(s)
    nblk = s // rows
    bh_n = bl * n_heads
    hp = PREP_HEADS_PER_STEP if n_heads % PREP_HEADS_PER_STEP == 0 else 1
    qkv_spec, ph, rv = _gdn_specs(n_heads, nblk, rows, hp)

    def body(k_ref, v_ref, cols_ref, grow_ref, u_ref, w_ref, t_ref):
        r, c, same = _group_masks(rows)
        causal = jnp.logical_and(same, r >= c)
        mats, rhs = [], []
        for h in range(hp):
            hs = slice(h * LANES, (h + 1) * LANES)
            k = k_ref[:, hs]
            gc = cols_ref[h, :, 0:1]
            beta = cols_ref[h, :, 1:2]
            kb = k * beta
            mats.append(jnp.where(r > c, _dot_nt(kb, k) * _decay(gc, grow_ref[h, 0], causal), 0.0))
            rhs.append((v_ref[:, hs] * beta, kb * jnp.exp(gc)))
        for h, tm in enumerate(_inv_unit_lower_many(mats, r, c)):
            tb = tm.astype(bf16)
            u_ref[h] = jnp.dot(tb, rhs[h][0].astype(bf16), preferred_element_type=f32)
            w_ref[h] = jnp.dot(tb, rhs[h][1].astype(bf16), preferred_element_type=f32)
            t_ref[h] = tb

    return _pc(body, name="gdn_prep", grid=(bh_n // hp, nblk),
               in_specs=[qkv_spec(1), qkv_spec(2), ph(2), rv()],
               out_specs=[ph(LANES), ph(LANES), ph(rows)],
               out_shape=[jax.ShapeDtypeStruct((bh_n, s, LANES), f32), jax.ShapeDtypeStruct((bh_n, s, LANES), f32),
                          jax.ShapeDtypeStruct((bh_n, s, rows), bf16)],
               sem=("parallel", "parallel"))(qkv, qkv, cols, grow)


def _gdn_scan(qkv, u, w, cols, grow, bl, s, n_heads):
    rows = _group_rows(s)
    g_chunks = rows // CHUNK
    nblk = s // rows
    bh_n = bl * n_heads
    d = n_heads * LANES
    hp = HEADS_PER_STEP if n_heads % HEADS_PER_STEP == 0 else 1
    qkv_spec, ph, rv = _gdn_specs(n_heads, nblk, rows, hp)

    def body(q_ref, k_ref, u_ref, w_ref, cols_ref, grow_ref, o_ref, vn_ref, ss_ref, s_scr):
        _acc_init(pl.program_id(1), s_scr)
        r, c, same = _group_masks(rows)
        causal = jnp.logical_and(same, r >= c)
        qs, ks, gcs, qgs, ps = [], [], [], [], []
        for h in range(hp):
            hs = slice(h * LANES, (h + 1) * LANES)
            q, k, gc = q_ref[:, hs], k_ref[:, hs], cols_ref[h, :, 0:1]
            qs.append(q)
            ks.append(k)
            gcs.append(gc)
            qgs.append(q * jnp.exp(gc))
            ps.append(_dot_nt(q, k) * _decay(gc, grow_ref[h, 0], causal))
        st = [s_scr[h] for h in range(hp)]
        o_state = [[] for _ in range(hp)]
        vns = [[] for _ in range(hp)]
        for ci in range(g_chunks):
            sl = slice(ci * CHUNK, (ci + 1) * CHUNK)
            for h in range(hp):
                ss_ref[h, ci * LANES:(ci + 1) * LANES, :] = st[h]
                vn = u_ref[h, sl, :] - _dot(w_ref[h, sl, :], st[h])
                vns[h].append(vn)
                o_state[h].append(_dot(qgs[h][sl], st[h]))
                gc = gcs[h][sl]
                g_last = gc[CHUNK - 1:CHUNK, :]
                st[h] = jnp.exp(g_last) * st[h] + _dot_tn(ks[h][sl] * jnp.exp(g_last - gc), vn)
        for h in range(hp):
            s_scr[h] = st[h]
            vn_all = jnp.concatenate(vns[h], axis=0)
            vn_ref[h] = vn_all
            o_ref[:, h * LANES:(h + 1) * LANES] = jnp.concatenate(o_state[h], axis=0) + _dot(ps[h], vn_all)

    return _pc(body, name="gdn_scan", grid=(bh_n // hp, nblk),
               in_specs=[qkv_spec(0), qkv_spec(1), ph(LANES), ph(LANES), ph(2), rv()],
               out_specs=[qkv_spec(0), ph(LANES), ph(LANES, mult=LANES // CHUNK)],
               out_shape=[jax.ShapeDtypeStruct((bl * s, d), f32), jax.ShapeDtypeStruct((bh_n, s, LANES), f32),
                          jax.ShapeDtypeStruct((bh_n, (s // CHUNK) * LANES, LANES), f32)],
               scratch=[pltpu.VMEM((hp, LANES, LANES), f32)],
               sem=("parallel", "arbitrary"))(qkv, qkv, u, w, cols, grow)


def _gdn_scan_bwd(do, qkv, w, vn, cols, grow, ss, bl, s, n_heads):
    rows = _group_rows(s)
    g_chunks = rows // CHUNK
    nblk = s // rows
    bh_n = bl * n_heads
    hp = HEADS_PER_STEP if n_heads % HEADS_PER_STEP == 0 else 1
    qkv_spec, ph, rv = _gdn_specs(n_heads, nblk, rows, hp, rev=True)

    def body(do_ref, q_ref, k_ref, w_ref, vn_ref, cols_ref, grow_ref, ss_ref,
             du_ref, dw_ref, dq_ref, dk_ref, dcol_ref, drow_ref, ds_scr):
        _acc_init(pl.program_id(1), ds_scr)
        r, c, same = _group_masks(rows)
        causal = jnp.logical_and(same, r >= c)
        last_row = lax.broadcasted_iota(jnp.int32, (CHUNK, 1), 0) == CHUNK - 1
        pre = []
        for h in range(hp):
            hs = slice(h * LANES, (h + 1) * LANES)
            do, q, k = do_ref[:, hs], q_ref[:, hs], k_ref[:, hs]
            vn = vn_ref[h]
            gc = cols_ref[h, :, 0:1]
            dmat = _decay(gc, grow_ref[h, 0], causal)
            gam = jnp.exp(gc)
            qk = _dot_nt(q, k)
            dpd = _dot_nt(do, vn) * dmat
            ep = dpd * qk
            drow_ref[h, 0] = -jnp.sum(ep, axis=0, keepdims=True)
            pre.append(dict(do=do, k=k, vn=vn, gc=gc, gam=gam, qg=q * gam,
                            dvn_intra=_dot_tn(qk * dmat, do), dq_intra=_dot(dpd, k), dk_intra=_dot_tn(dpd, q),
                            ep_rows=jnp.sum(ep, axis=-1, keepdims=True)))
        ds = [ds_scr[h] for h in range(hp)]
        for ci in reversed(range(g_chunks)):
            sl = slice(ci * CHUNK, (ci + 1) * CHUNK)
            for h in range(hp):
                pr = pre[h]
                st = ss_ref[h, ci * LANES:(ci + 1) * LANES, :]
                gc = pr["gc"][sl]
                g_last = gc[CHUNK - 1:CHUNK, :]
                gam_last = jnp.exp(g_last)
                kd_scale = jnp.exp(g_last - gc)
                kdec = pr["k"][sl] * kd_scale
                do_c = pr["do"][sl]
                qg_c = pr["qg"][sl]
                dvn = pr["dvn_intra"][sl] + _dot(kdec, ds[h])
                dkdec = _dot_nt(pr["vn"][sl], ds[h])
                du_ref[h, sl, :] = dvn
                dw_ref[h, sl, :] = -_dot_nt(dvn, st)
                dqg = _dot_nt(do_c, st)
                dq_ref[h, sl, :] = dqg * pr["gam"][sl] + pr["dq_intra"][sl]
                dk_ref[h, sl, :] = pr["dk_intra"][sl] + dkdec * kd_scale
                kd_rows = jnp.sum(dkdec * kdec, axis=-1, keepdims=True)
                extra = jnp.sum(kd_rows) + gam_last * jnp.sum(st * ds[h])
                dcol_ref[h, sl, :] = (jnp.sum(dqg * qg_c, axis=-1, keepdims=True) + pr["ep_rows"][sl] - kd_rows
                                      + jnp.where(last_row, extra, 0.0))
                ds[h] = gam_last * ds[h] + _dot_tn(qg_c, do_c) - _dot_tn(w_ref[h, sl, :], dvn)
        for h in range(hp):
            ds_scr[h] = ds[h]

    return _pc(body, name="gdn_scan_bwd", grid=(bh_n // hp, nblk),
               in_specs=[qkv_spec(0), qkv_spec(0), qkv_spec(1), ph(LANES), ph(LANES), ph(2), rv(),
                         ph(LANES, mult=LANES // CHUNK)],
               out_specs=[ph(LANES), ph(LANES), ph(LANES), ph(LANES), ph(1), rv()],
               out_shape=[jax.ShapeDtypeStruct((bh_n, s, LANES), f32)] * 4
               + [jax.ShapeDtypeStruct((bh_n, s, 1), f32), jax.ShapeDtypeStruct((bh_n, nblk, 1, rows), f32)],
               scratch=[pltpu.VMEM((hp, LANES, LANES), f32)],
               sem=("parallel", "arbitrary"))(do, qkv, qkv, w, vn, cols, grow, ss)


def _gdn_prep_bwd(qkv, cols, grow, tmat, du, dw, dk_scan, dcol_scan, drow_scan, bl, s, n_heads):
    rows = _group_rows(s)
    nblk = s // rows
    bh_n = bl * n_heads
    hp = PREP_HEADS_PER_STEP if n_heads % PREP_HEADS_PER_STEP == 0 else 1
    qkv_spec, ph, rv = _gdn_specs(n_heads, nblk, rows, hp)

    def body(k_ref, v_ref, cols_ref, grow_ref, t_ref, du_ref, dw_ref, dks_ref, dcs_ref, drs_ref,
             dk_ref, dv_ref, dcols_ref, drow_ref):
        r, c, same = _group_masks(rows)
        causal = jnp.logical_and(same, r >= c)
        hh = range(hp)
        ks = [k_ref[:, h * LANES:(h + 1) * LANES] for h in hh]
        vs = [v_ref[:, h * LANES:(h + 1) * LANES] for h in hh]
        gcs = [cols_ref[h, :, 0:1] for h in hh]
        betas = [cols_ref[h, :, 1:2] for h in hh]
        tms = [t_ref[h] for h in hh]
        dus = [du_ref[h] for h in hh]
        dws = [dw_ref[h] for h in hh]
        gams = [jnp.exp(gc) for gc in gcs]
        kbs = [k * b for k, b in zip(ks, betas)]
        kbgs = [kb * g for kb, g in zip(kbs, gams)]
        dts = [jnp.where(same, _dot_nt(dus[h], vs[h] * betas[h]) + _dot_nt(dws[h], kbgs[h]), 0.0) for h in hh]
        dvbs = [_dot_tn(tms[h], dus[h]) for h in hh]
        dkbgs = [_dot_tn(tms[h], dws[h]) for h in hh]
        kks = [_dot_nt(kbs[h], ks[h]) for h in hh]
        inner = [_dot_nt(dts[h], tms[h]) for h in hh]
        dads = [jnp.where(r > c, -_dot_tn(tms[h], inner[h]), 0.0) * _decay(gcs[h], grow_ref[h, 0], causal) for h in hh]
        dkbs = [dkbgs[h] * gams[h] + _dot(dads[h], ks[h]) for h in hh]
        dk2 = [_dot_tn(dads[h], kbs[h]) for h in hh]
        for h in hh:
            dk_ref[h] = dks_ref[h] + dk2[h] + dkbs[h] * betas[h]
            dv_ref[h] = dvbs[h] * betas[h]
            ea = dads[h] * kks[h]
            dcols_ref[h, :, 0:1] = (dcs_ref[h] + jnp.sum(dkbgs[h] * kbgs[h], axis=-1, keepdims=True)
                                    + jnp.sum(ea, axis=-1, keepdims=True))
            dcols_ref[h, :, 1:2] = (jnp.sum(dvbs[h] * vs[h], axis=-1, keepdims=True)
                                    + jnp.sum(dkbs[h] * ks[h], axis=-1, keepdims=True))
            drow_ref[h, 0] = drs_ref[h, 0] - jnp.sum(ea, axis=0, keepdims=True)

    return _pc(body, name="gdn_prep_bwd", grid=(bh_n // hp, nblk),
               in_specs=[qkv_spec(1), qkv_spec(2), ph(2), rv(), ph(rows), ph(LANES), ph(LANES), ph(LANES), ph(1), rv()],
               out_specs=[ph(LANES), ph(LANES), ph(2), rv()],
               out_shape=[jax.ShapeDtypeStruct((bh_n, s, LANES), f32), jax.ShapeDtypeStruct((bh_n, s, LANES), f32),
                          jax.ShapeDtypeStruct((bh_n, s, 2), f32), jax.ShapeDtypeStruct((bh_n, nblk, 1, rows), f32)],
               sem=("parallel", "parallel"))(qkv, qkv, cols, grow, tmat, du, dw, dk_scan, dcol_scan, drow_scan)


def _row(v):
    return v.reshape(1, -1).astype(f32)


def _pad_lanes(v):
    v = v.reshape(1, -1).astype(f32)
    return jnp.pad(v, ((0, 0), (0, LANES - v.shape[1])))


def _local_step(x, tgt, p):
    bl, s, d = x.shape
    t = bl * s
    n_heads = p["gdn_a_log"].shape[-1]
    assert d == n_heads * LANES and s % CHUNK == 0
    x2 = x.reshape(t, d)
    tgt2 = tgt.reshape(t, d)
    gr = {}

    n0, ub, gl = _pw1_glu(x2, _row(p["norm_mix_g"][0]), p["cv_w_pw1"], _row(p["cv_b_pw1"]))
    dc = _dwconv_fwd(gl.reshape(bl, s, d), p["cv_w_dw"], _row(p["cv_b_dw"]), "dwconv_fwd").reshape(t, d)
    sb, h1 = _ln_silu_mm_res(dc, _row(p["cv_ln_g"]), _row(p["cv_ln_b"]), p["cv_w_pw2"], _row(p["cv_b_pw2"]), x2)
    n1, f0, r0, h2 = _mlp_fwd(h1, _row(p["norm_ffn_g"][0]), p["mlp_w1"], p["mlp_w2"], 0, "mlp_fwd0")

    w_in = p["gdn_w_in"]
    w_ab = jnp.pad(w_in[:, 4 * d:], ((0, 0), (0, LANES - 2 * n_heads)))
    a_log_pad = _pad_lanes(p["gdn_a_log"])
    dt_pad = _pad_lanes(p["gdn_dt_bias"])
    n2, qkv_pre, z, ab, gbeta = _gdn_in(h2, _row(p["norm_mix_g"][1]), w_in, w_ab, a_log_pad, dt_pad, n_heads)
    zero_bias = jnp.zeros((1, 3 * d), f32)
    qkv = _dwconv_fwd(qkv_pre.reshape(bl, s, 3 * d), p["gdn_conv_w"], zero_bias, "sconv_fwd", qk_heads=n_heads).reshape(t, 3 * d)
    bh_n, rows = bl * n_heads, _group_rows(s)
    gates = gbeta[:, :2 * n_heads].reshape(bl, s, 2, n_heads).transpose(2, 0, 3, 1).reshape(2, bh_n, s)
    g_lanes = gates[0].reshape(bh_n * (s // CHUNK), CHUNK)
    gc_lanes = _lane_cumsum(g_lanes, jnp.zeros_like(g_lanes), False, "gdn_gate_cumsum")
    grow = gc_lanes.reshape(bh_n, s // rows, 1, rows)
    cols = jnp.stack([gc_lanes.reshape(bh_n, s), gates[1]], axis=-1)
    u, w, tmat = _gdn_prep(qkv, cols, grow, bl, s, n_heads)
    o, vn, ss = _gdn_scan(qkv, u, w, cols, grow, bl, s, n_heads)
    onb, h3 = _gated_norm_mm_res(o, z, _row(p["gdn_norm_g"]), p["gdn_w_out"], h2, n_heads)
    n3, f1, r1, h4 = _mlp_fwd(h3, _row(p["norm_ffn_g"][1]), p["mlp_w1"], p["mlp_w2"], 1, "mlp_fwd1")
    loss_acc, dh4, dgf = _loss_head(h4, _row(p["final_norm_g"]), tgt2)
    loss = loss_acc[0, 0]
    gr["final_norm_g"] = dgf[0]

    df1, dh3, dg_ffn1, _ = _mlp_bwd(dh4, h3, _row(p["norm_ffn_g"][1]), f1, p["mlp_w1"], p["mlp_w2"], 1, "mlp_bwd1")
    dw2_1 = _mm_tn(r1, dh4, "dw_mlp2_1")
    dw1_1 = _mm_tn_blocked(n3, df1, "dw_mlp1_1")

    gr["gdn_w_out"] = _mm_tn(onb, dh3, "dw_gdn_out")
    don = _mm_nt([(dh3, p["gdn_w_out"])], "dx_gdn_out")
    do, dz, dng = _gated_norm_bwd(don, o, z, _row(p["gdn_norm_g"]), n_heads)
    gr["gdn_norm_g"] = dng[0]
    du, dw_, dq, dk_scan, dcol_scan, drow_scan = _gdn_scan_bwd(do, qkv, w, vn, cols, grow, ss, bl, s, n_heads)
    dk, dv, dcols, drow = _gdn_prep_bwd(qkv, cols, grow, tmat, du, dw_, dk_scan, dcol_scan, drow_scan, bl, s, n_heads)
    dg_lanes = _lane_cumsum(dcols[..., 0].reshape(g_lanes.shape), drow.reshape(g_lanes.shape), True, "gdn_gate_cumsum_bwd")
    dgb2 = jnp.stack([dg_lanes.reshape(bh_n, s), dcols[..., 1]], axis=-1)
    dqkv_pre, dconv_w, _ = _dwconv_bwd(qkv_pre.reshape(bl, s, 3 * d), [dq, dk, dv], p["gdn_conv_w"], "sconv_bwd", qk_heads=n_heads)
    gr["gdn_conv_w"] = dconv_w[:p["gdn_conv_w"].shape[0]]
    dgb = dgb2.reshape(bl, n_heads, s, 2).transpose(0, 2, 3, 1).reshape(t, 2 * n_heads)
    dgb = jnp.pad(dgb, ((0, 0), (0, LANES - 2 * n_heads)))
    dab, dal, ddt = _gates_bwd(dgb, ab, a_log_pad, dt_pad, n_heads)
    gr["gdn_a_log"] = dal[0, :n_heads]
    gr["gdn_dt_bias"] = ddt[0, :n_heads]
    dqkv2 = dqkv_pre.reshape(t, 3 * d)
    gr["gdn_w_in"] = jnp.concatenate(
        [_mm_tn(n2, dqkv2, "dw_gdn_in_qkv"), _mm_tn(n2, dz, "dw_gdn_in_z"), _mm_tn(n2, dab, "dw_gdn_in_ab")[:, :2 * n_heads]], axis=1)
    dn2 = _mm_nt([(dqkv2, (w_in, 0)), (dz, (w_in, 3)), (dab, w_ab)], "dx_gdn_in")
    dh2, dg_mix1 = _rms_bwd_res(dn2, h2, _row(p["norm_mix_g"][1]), dh3, "rms_bwd_gdn")

    df0, dh1, dg_ffn0, cs_h1 = _mlp_bwd(dh2, h1, _row(p["norm_ffn_g"][0]), f0, p["mlp_w1"], p["mlp_w2"], 0, "mlp_bwd0")
    dw2_0 = _mm_tn(r0, dh2, "dw_mlp2_0")
    dw1_0 = _mm_tn_blocked(n1, df0, "dw_mlp1_0")
    gr["mlp_w1"] = [dw1_0, dw1_1]
    gr["mlp_w2"] = [dw2_0, dw2_1]
    gr["norm_ffn_g"] = jnp.stack([dg_ffn0[0], dg_ffn1[0]])

    gr["cv_b_pw2"] = cs_h1[0]
    gr["cv_w_pw2"] = _mm_tn(sb, dh1, "dw_pw2")
    ds = _mm_nt([(dh1, p["cv_w_pw2"])], "dx_pw2")
    ddc, dlng, dlnb, cs_dc = _ln_silu_bwd(ds, dc, _row(p["cv_ln_g"]), _row(p["cv_ln_b"]))
    gr["cv_ln_g"] = dlng[0]
    gr["cv_ln_b"] = dlnb[0]
    gr["cv_b_dw"] = cs_dc[0]
    dgl, dw_dw, _ = _dwconv_bwd(gl.reshape(bl, s, d), [ddc.reshape(bl, s, d)], p["cv_w_dw"], "dwconv_bwd")
    gr["cv_w_dw"] = dw_dw[:p["cv_w_dw"].shape[0]]
    dub, cs_u = _glu_bwd(dgl.reshape(t, d), ub)
    gr["cv_b_pw1"] = cs_u[0]
    gr["cv_w_pw1"] = _mm_tn_blocked(n0, dub, "dw_pw1")
    dn0 = _mm_nt_blocked(dub, p["cv_w_pw1"], "dx_pw1")
    dx, dg_mix0 = _rms_bwd_res(dn0, x2, _row(p["norm_mix_g"][0]), dh1, "rms_bwd_conv")
    gr["norm_mix_g"] = jnp.stack([dg_mix0[0], dg_mix1[0]])
    return loss, dx.reshape(bl, s, d), gr


ANY = pl.BlockSpec(memory_space=pl.ANY)
MESH = pl.DeviceIdType.MESH


def _flip(v, bit):
    return 1 - v if bit else v


def _all_gather_many(shards):
    na = len(shards)

    def body(*refs):
        x_refs, o_refs = refs[:na], refs[na:2 * na]
        send_sems, recv_sems, local_sems = refs[2 * na:]
        x, y, c = lax.axis_index("x"), lax.axis_index("y"), lax.axis_index("c")
        me, sibling = (x, y, c), (x, y, 1 - c)
        chips = [(1 - x, y), (x, 1 - y), (1 - x, 1 - y)]

        def copy(a, k, block, to, src=None):
            px, py, pc = block
            dst = o_refs[a].at[4 * px + 2 * py + pc]
            return pltpu.make_async_remote_copy(
                src_ref=dst if src is None else src, dst_ref=dst,
                send_sem=send_sems.at[a, k], recv_sem=recv_sems.at[a, k], device_id=to, device_id_type=MESH)

        mine = [pltpu.make_async_copy(x_refs[a], o_refs[a].at[4 * x + 2 * y + c], local_sems.at[a]) for a in range(na)]
        first = []
        for a in range(na):
            first.append(copy(a, 0, me, sibling, src=x_refs[a]))
            first += [copy(a, 1 + j, me, (*chip, c), src=x_refs[a]) for j, chip in enumerate(chips)]
        for cp in mine + first:
            cp.start()
        passed = []
        for j, chip in enumerate(chips):
            for a in range(na):
                copy(a, 1 + j, (*chip, c), me).wait_recv()
                fwd = copy(a, 4 + j, (*chip, c), sibling)
                fwd.start()
                passed.append(fwd)
        for a in range(na):
            copy(a, 0, sibling, me).wait_recv()
        for j, chip in enumerate(chips):
            for a in range(na):
                copy(a, 4 + j, (*chip, 1 - c), me).wait_recv()
        for cp in first + passed:
            cp.wait_send()
        for cp in mine:
            cp.wait()

    return pl.pallas_call(
        body, name="weights_all_gather",
        out_shape=[jax.ShapeDtypeStruct((N_DEV,) + a.shape, a.dtype) for a in shards],
        in_specs=[ANY] * na, out_specs=[ANY] * na,
        scratch_shapes=[pltpu.SemaphoreType.DMA((na, 7)), pltpu.SemaphoreType.DMA((na, 7)), pltpu.SemaphoreType.DMA((na,))],
        compiler_params=pltpu.CompilerParams(has_side_effects=True),
    )(*shards)


def _pair_exchange(gs):
    na = len(gs)

    def body(*refs):
        g_refs, r_refs = refs[:na], refs[na:2 * na]
        send_sems, recv_sems = refs[2 * na:]
        x, y, c = lax.axis_index("x"), lax.axis_index("y"), lax.axis_index("c")
        copies = []
        for a in range(na):
            for q in range(4):
                copies.append(pltpu.make_async_remote_copy(
                    src_ref=g_refs[a].at[2 * q + (1 - c)], dst_ref=r_refs[a].at[q],
                    send_sem=send_sems.at[a, q], recv_sem=recv_sems.at[a, q],
                    device_id=(x, y, 1 - c), device_id_type=MESH))
        for cp in copies:
            cp.start()
        for cp in copies:
            cp.wait_recv()
        for cp in copies:
            cp.wait_send()

    return pl.pallas_call(
        body, name="grads_pair_exchange",
        out_shape=[jax.ShapeDtypeStruct((4,) + g.shape[1:], g.dtype) for g in gs],
        in_specs=[ANY] * na, out_specs=[ANY] * na,
        scratch_shapes=[pltpu.SemaphoreType.DMA((na, 4)), pltpu.SemaphoreType.DMA((na, 4))],
        compiler_params=pltpu.CompilerParams(has_side_effects=True),
    )(*gs)


def _pair_sum(g, r1, core, name):
    _, rr, cc = g.shape
    tr = _tile(rr, 256)

    def body(c_ref, g_ref, r_ref, o_ref):
        o_ref[...] = (g_ref[...] + r_ref[...]).astype(bf16)

    grid_spec = pltpu.PrefetchScalarGridSpec(
        num_scalar_prefetch=1, grid=(4, rr // tr),
        in_specs=[pl.BlockSpec((None, tr, cc), lambda q, i, c_ref: (2 * q + c_ref[0], i, 0)),
                  pl.BlockSpec((None, tr, cc), lambda q, i, c_ref: (q, i, 0))],
        out_specs=pl.BlockSpec((None, tr, cc), lambda q, i, c_ref: (q, i, 0)))
    return pl.pallas_call(
        body, name=name, grid_spec=grid_spec, out_shape=jax.ShapeDtypeStruct((4, rr, cc), bf16),
        compiler_params=pltpu.CompilerParams(dimension_semantics=("parallel", "parallel"), vmem_limit_bytes=VMEM_LIMIT),
    )(core, g, r1)


def _chip_exchange(hs, small):
    na = len(hs)

    def body(*refs):
        h_refs, s_ref = refs[:na], refs[na]
        r_refs, sr_ref = refs[na + 1:2 * na + 1], refs[2 * na + 1]
        send_sems, recv_sems, local_sems, s_send, s_recv, s_local = refs[2 * na + 2:]
        x, y, c = lax.axis_index("x"), lax.axis_index("y"), lax.axis_index("c")
        q_me = 2 * x + y
        me = 4 * x + 2 * y + c
        local = [pltpu.make_async_copy(h_refs[a].at[q_me], r_refs[a].at[q_me], local_sems.at[a]) for a in range(na)]
        local.append(pltpu.make_async_copy(s_ref, sr_ref.at[me], s_local))
        copies = []
        for rel in range(1, 4):
            px, py = _flip(x, rel & 2), _flip(y, rel & 1)
            for a in range(na):
                copies.append(pltpu.make_async_remote_copy(
                    src_ref=h_refs[a].at[2 * px + py], dst_ref=r_refs[a].at[q_me],
                    send_sem=send_sems.at[a, rel - 1], recv_sem=recv_sems.at[a, rel - 1],
                    device_id=(px, py, c), device_id_type=MESH))
        for k in range(1, N_DEV):
            px, py, pc = _flip(x, k & 4), _flip(y, k & 2), _flip(c, k & 1)
            copies.append(pltpu.make_async_remote_copy(
                src_ref=s_ref, dst_ref=sr_ref.at[me], send_sem=s_send.at[k - 1], recv_sem=s_recv.at[k - 1],
                device_id=(px, py, pc), device_id_type=MESH))
        for cp in local + copies:
            cp.start()
        for cp in copies:
            cp.wait_recv()
        for cp in copies:
            cp.wait_send()
        for cp in local:
            cp.wait()

    return pl.pallas_call(
        body, name="grads_chip_exchange",
        out_shape=[jax.ShapeDtypeStruct(h.shape, h.dtype) for h in hs]
        + [jax.ShapeDtypeStruct((N_DEV,) + small.shape, small.dtype)],
        in_specs=[ANY] * (na + 1), out_specs=[ANY] * (na + 1),
        scratch_shapes=[pltpu.SemaphoreType.DMA((na, 3)), pltpu.SemaphoreType.DMA((na, 3)), pltpu.SemaphoreType.DMA((na,)),
                        pltpu.SemaphoreType.DMA((7,)), pltpu.SemaphoreType.DMA((7,)), pltpu.SemaphoreType.DMA(())],
        compiler_params=pltpu.CompilerParams(has_side_effects=True),
    )(*hs, small)


def _sum4_adamw(r2, w, m, v, layer, name):
    _, rr, cc = r2.shape
    tr = _tile(rr, 256)
    bc1 = 1.0 - ADAM_B1 ** ADAM_STEP
    bc2 = 1.0 - ADAM_B2 ** ADAM_STEP

    def body(r_ref, w_ref, m_ref, v_ref, g_ref, d_ref, nm_ref, nv_ref):
        gv = r_ref[0].astype(f32)
        for q in range(1, 4):
            gv = gv + r_ref[q].astype(f32)
        g_ref[...] = gv
        nm = ADAM_B1 * m_ref[...] + (1.0 - ADAM_B1) * gv
        nv = ADAM_B2 * v_ref[...] + (1.0 - ADAM_B2) * (gv * gv)
        nm_ref[...] = nm
        nv_ref[...] = nv
        d_ref[...] = -ADAM_LR * ((nm / bc1) / (jnp.sqrt(nv / bc2) + ADAM_EPS) + ADAM_WD * w_ref[...])

    lspec = pl.BlockSpec((None, tr, cc), lambda i: (layer, i, 0))
    return _pc(body, name=name, grid=(rr // tr,),
               in_specs=[pl.BlockSpec((4, tr, cc), lambda i: (0, i, 0)), lspec, lspec, lspec],
               out_specs=[_rows(tr, cc)] * 4, out_shape=[jax.ShapeDtypeStruct((rr, cc), f32)] * 4,
               sem=("parallel",))(r2, w, m, v)


def _sum_devices(recv):
    _, rr, _ = recv.shape
    tr = _tile(rr, 512)

    def body(r_ref, o_ref):
        acc = r_ref[0]
        for i in range(1, N_DEV):
            acc = acc + r_ref[i]
        o_ref[...] = acc

    return _pc(body, name="grads_sum", grid=(rr // tr,),
               in_specs=[pl.BlockSpec((N_DEV, tr, LANES), lambda i: (0, i, 0))],
               out_specs=_rows(tr, LANES), out_shape=jax.ShapeDtypeStruct((rr, LANES), f32), sem=("parallel",))(recv)


def _adamw(w, g, m, v):
    rr = w.shape[0]
    tr = _tile(rr, 512)
    bc1 = 1.0 - ADAM_B1 ** ADAM_STEP
    bc2 = 1.0 - ADAM_B2 ** ADAM_STEP

    def body(w_ref, g_ref, m_ref, v_ref, d_ref, nm_ref, nv_ref):
        gv = g_ref[...]
        nm = ADAM_B1 * m_ref[...] + (1.0 - ADAM_B1) * gv
        nv = ADAM_B2 * v_ref[...] + (1.0 - ADAM_B2) * (gv * gv)
        nm_ref[...] = nm
        nv_ref[...] = nv
        d_ref[...] = -ADAM_LR * ((nm / bc1) / (jnp.sqrt(nv / bc2) + ADAM_EPS) + ADAM_WD * w_ref[...])

    spec = _rows(tr, LANES)
    return _pc(body, name="adamw", grid=(rr // tr,), in_specs=[spec] * 4, out_specs=[spec] * 3,
               out_shape=[jax.ShapeDtypeStruct((rr, LANES), f32)] * 3, sem=("parallel",))(w, g, m, v)


PACK_ROWS = 512
PART_ROWS = 16


def _pack(arrs, lead=()):
    nl = len(lead)
    parts, sizes = [], []
    for a in arrs:
        flat = a.reshape(lead + (-1,))
        n = flat.shape[-1]
        rows = -(-n // (LANES * PART_ROWS)) * PART_ROWS
        flat = jnp.pad(flat, [(0, 0)] * nl + [(0, rows * LANES - n)])
        parts.append(flat.reshape(lead + (rows, LANES)))
        sizes.append((rows, n))
    total = sum(r for r, _ in sizes)
    padded = -(-total // PACK_ROWS) * PACK_ROWS
    if padded > total:
        parts.append(jnp.zeros(lead + (padded - total, LANES), parts[0].dtype))
    return jnp.concatenate(parts, axis=nl), sizes


def _unpack(packed, sizes, shapes, lead=()):
    nl = len(lead)
    out, off = [], 0
    for (rows, n), shp in zip(sizes, shapes):
        piece = lax.slice_in_dim(packed, off, off + rows, axis=nl).reshape(lead + (rows * LANES,))
        out.append(lax.slice_in_dim(piece, 0, n, axis=nl).reshape(lead + tuple(shp)))
        off += rows
    return out


def _cols_to_blocks(a):
    n = a.shape[-1] // N_DEV
    a = a.reshape(a.shape[:-1] + (N_DEV, n))
    return jnp.moveaxis(a, -2, 0)


def _blocks_to_cols(a):
    a = jnp.moveaxis(a, 0, -2)
    return a.reshape(a.shape[:-2] + (a.shape[-2] * a.shape[-1],))


def _rows_to_blocks(a):
    k = a.shape[-2] // N_DEV
    a = a.reshape(a.shape[:-2] + (N_DEV, k, a.shape[-1]))
    return jnp.moveaxis(a, -3, 0)


def _blocks_to_rows(a):
    a = jnp.moveaxis(a, 0, -3)
    return a.reshape(a.shape[:-3] + (a.shape[-3] * a.shape[-2], a.shape[-1]))


COL_SHARDED = ("cv_w_pw1", "gdn_w_in", "mlp_w1")
ROW_SHARDED = ("cv_w_pw2", "gdn_w_out", "mlp_w2")
CONV_SHARDED = ("cv_w_dw", "gdn_conv_w")
REPLICATED = ("norm_mix_g", "norm_ffn_g", "final_norm_g", "cv_b_pw1", "cv_b_dw", "cv_ln_g", "cv_ln_b", "cv_b_pw2",
              "gdn_a_log", "gdn_dt_bias", "gdn_norm_g")
WEIGHTS = ("norm_mix_g", "norm_ffn_g", "final_norm_g", "cv_w_pw1", "cv_b_pw1", "cv_w_dw", "cv_b_dw", "cv_ln_g",
           "cv_ln_b", "cv_w_pw2", "cv_b_pw2", "gdn_w_in", "gdn_conv_w", "gdn_a_log", "gdn_dt_bias", "gdn_norm_g",
           "gdn_w_out", "mlp_w1", "mlp_w2")
MATMUL_SHARDED = COL_SHARDED + ROW_SHARDED


def _squeeze_layer(name, a):
    if name in ("norm_mix_g", "norm_ffn_g", "final_norm_g", "mlp_w1", "mlp_w2"):
        return a
    return a[0]


def _gather_weights(shards):
    names = MATMUL_SHARDED + CONV_SHARDED
    send = []
    for n in names:
        a = shards[n] if n in CONV_SHARDED else shards[n].astype(bf16)
        if n == "gdn_w_in":
            a = a.reshape(-1, LANES)
        send.append(a)
    got = dict(zip(names, _all_gather_many(send)))
    full = {
        "cv_w_pw1": got["cv_w_pw1"],
        "mlp_w1": got["mlp_w1"],
        "mlp_w2": got["mlp_w2"],
        "cv_w_pw2": _blocks_to_rows(got["cv_w_pw2"]),
        "gdn_w_out": _blocks_to_rows(got["gdn_w_out"]),
        "gdn_w_in": _blocks_to_cols(got["gdn_w_in"].reshape((N_DEV,) + shards["gdn_w_in"].shape)),
    }
    for n in CONV_SHARDED:
        full[n] = _blocks_to_cols(got[n])
    return full


def kernel(x, norm_mix_g, norm_ffn_g, final_norm_g, cv_w_pw1, cv_b_pw1, cv_w_dw, cv_b_dw, cv_ln_g, cv_ln_b, cv_w_pw2, cv_b_pw2, gdn_w_in, gdn_conv_w, gdn_a_log, gdn_dt_bias, gdn_norm_g, gdn_w_out, mlp_w1, mlp_w2, loss_target, m_norm_mix_g, m_norm_ffn_g, m_final_norm_g, m_cv_w_pw1, m_cv_b_pw1, m_cv_w_dw, m_cv_b_dw, m_cv_ln_g, m_cv_ln_b, m_cv_w_pw2, m_cv_b_pw2, m_gdn_w_in, m_gdn_conv_w, m_gdn_a_log, m_gdn_dt_bias, m_gdn_norm_g, m_gdn_w_out, m_mlp_w1, m_mlp_w2, v_norm_mix_g, v_norm_ffn_g, v_final_norm_g, v_cv_w_pw1, v_cv_b_pw1, v_cv_w_dw, v_cv_b_dw, v_cv_ln_g, v_cv_ln_b, v_cv_w_pw2, v_cv_b_pw2, v_gdn_w_in, v_gdn_conv_w, v_gdn_a_log, v_gdn_dt_bias, v_gdn_norm_g, v_gdn_w_out, v_mlp_w1, v_mlp_w2):
    w_in = dict(zip(WEIGHTS, (norm_mix_g, norm_ffn_g, final_norm_g, cv_w_pw1, cv_b_pw1, cv_w_dw, cv_b_dw, cv_ln_g, cv_ln_b, cv_w_pw2, cv_b_pw2, gdn_w_in, gdn_conv_w, gdn_a_log, gdn_dt_bias, gdn_norm_g, gdn_w_out, mlp_w1, mlp_w2)))
    m_in = dict(zip(WEIGHTS, (m_norm_mix_g, m_norm_ffn_g, m_final_norm_g, m_cv_w_pw1, m_cv_b_pw1, m_cv_w_dw, m_cv_b_dw, m_cv_ln_g, m_cv_ln_b, m_cv_w_pw2, m_cv_b_pw2, m_gdn_w_in, m_gdn_conv_w, m_gdn_a_log, m_gdn_dt_bias, m_gdn_norm_g, m_gdn_w_out, m_mlp_w1, m_mlp_w2)))
    v_in = dict(zip(WEIGHTS, (v_norm_mix_g, v_norm_ffn_g, v_final_norm_g, v_cv_w_pw1, v_cv_b_pw1, v_cv_w_dw, v_cv_b_dw, v_cv_ln_g, v_cv_ln_b, v_cv_w_pw2, v_cv_b_pw2, v_gdn_w_in, v_gdn_conv_w, v_gdn_a_log, v_gdn_dt_bias, v_gdn_norm_g, v_gdn_w_out, v_mlp_w1, v_mlp_w2)))
    me = 4 * lax.axis_index("x") + 2 * lax.axis_index("y") + lax.axis_index("c")

    core = lax.axis_index("c").astype(jnp.int32).reshape(1)
    d_model = x.shape[-1]

    shards = {n: _squeeze_layer(n, w_in[n]) for n in WEIGHTS}
    params = {n: shards[n] for n in REPLICATED}
    params.update(_gather_weights(shards))

    loss_part, grad_x, gr = _local_step(x, loss_target, params)
    loss = lax.psum(loss_part, ("x", "y", "c"))

    def row_blocks(a):
        return a.reshape(N_DEV, a.shape[0] // N_DEV, a.shape[1])

    def flat_blocks(a):
        k, n8 = a.shape
        return _cols_to_blocks(a).reshape(N_DEV, k * (n8 // N_DEV) // LANES, LANES)

    big = [("cv_w_pw1", 0, gr["cv_w_pw1"]), ("cv_w_pw2", 0, row_blocks(gr["cv_w_pw2"])),
           ("gdn_w_in", 0, flat_blocks(gr["gdn_w_in"])), ("gdn_w_out", 0, row_blocks(gr["gdn_w_out"])),
           ("mlp_w1", 0, gr["mlp_w1"][0]), ("mlp_w1", 1, gr["mlp_w1"][1]),
           ("mlp_w2", 0, row_blocks(gr["mlp_w2"][0])), ("mlp_w2", 1, row_blocks(gr["mlp_w2"][1]))]
    g_blocks = [g for _, _, g in big]
    from_sibling = _pair_exchange(g_blocks)
    chip_sums = [_pair_sum(g, r1, core, f"grads_pair_sum_{i}") for i, (g, r1) in enumerate(zip(g_blocks, from_sibling))]

    small = REPLICATED + CONV_SHARDED
    small_send, ssz = _pack([gr[n] for n in small])
    *from_chips, small_recv = _chip_exchange(chip_sums, small_send)
    small_sum = _sum_devices(small_recv)
    grads = dict(zip(small, _unpack(small_sum, ssz, [gr[n].shape for n in small])))
    for n in CONV_SHARDED:
        cn = shards[n].shape[-1]
        grads[n] = lax.dynamic_slice_in_dim(grads[n], me * cn, cn, axis=1)

    def as3d(n, a):
        if n == "gdn_w_in":
            return a.reshape(a.shape[0], -1, LANES)
        return a

    res = {n: {} for n in MATMUL_SHARDED}
    for (n, layer, _), r2 in zip(big, from_chips):
        res[n][layer] = _sum4_adamw(r2, as3d(n, w_in[n]), as3d(n, m_in[n]), as3d(n, v_in[n]), layer, f"adamw_{n}_{layer}")
    out_groups = {n: [] for n in WEIGHTS}
    for n in MATMUL_SHARDED:
        layers = sorted(res[n])
        for k in range(4):
            pieces = [res[n][layer][k] for layer in layers]
            out_groups[n].append(jnp.stack(pieces).reshape(w_in[n].shape))

    sm_w = [shards[n] for n in small]
    sm_g = [grads[n].reshape(shards[n].shape) for n in small]
    sm_m = [_squeeze_layer(n, m_in[n]) for n in small]
    sm_v = [_squeeze_layer(n, v_in[n]) for n in small]
    wp, psz = _pack(sm_w)
    gp, _ = _pack(sm_g)
    mp, _ = _pack(sm_m)
    vp, _ = _pack(sm_v)
    dp, nmp, nvp = _adamw(wp, gp, mp, vp)
    shp = [a.shape for a in sm_w]
    for n, g, dl, nm, nv in zip(small, sm_g, _unpack(dp, psz, shp), _unpack(nmp, psz, shp), _unpack(nvp, psz, shp)):
        out_groups[n] = [a.reshape(w_in[n].shape) for a in (g, dl, nm, nv)]

    outs = [loss, grad_x]
    for k in range(4):
        outs += [out_groups[n][k] for n in WEIGHTS]
    return tuple(outs)
```

```python
import functools

import jax
import jax.numpy as jnp
from jax import lax
from jax.experimental import pallas as pl
from jax.experimental.pallas import tpu as pltpu

f32, bf16 = jnp.float32, jnp.bfloat16

NORM_EPS = 1e-6
L2_EPS = 1e-6
CHUNK = 64
LANES = 128
SUBLANES = 8
N_DEV = 8
VMEM_LIMIT = 56 * 1024 * 1024
CONV_PAD = 32
HEADS_PER_STEP = 4
PREP_HEADS_PER_STEP = 2
NEG = -1e30

ADAM_LR, ADAM_B1, ADAM_B2, ADAM_EPS, ADAM_WD, ADAM_STEP = 0.001, 0.9, 0.999, 1e-08, 0.01, 10

NT = (((1,), (1,)), ((), ()))
TN = (((0,), (0,)), ((), ()))
HI = lax.Precision.HIGHEST


def _pc(body, *, name, grid, in_specs, out_specs, out_shape, scratch=(), sem=None):
    return pl.pallas_call(
        body, name=name, grid=grid, in_specs=in_specs, out_specs=out_specs, out_shape=out_shape,
        scratch_shapes=list(scratch),
        compiler_params=pltpu.CompilerParams(dimension_semantics=sem, vmem_limit_bytes=VMEM_LIMIT))


def _rows(tm, n):
    return pl.BlockSpec((tm, n), lambda i: (i, 0))


def _const(shape):
    return pl.BlockSpec(shape, lambda *_: (0,) * len(shape))


def _resident(shape):
    return pl.BlockSpec(shape, lambda *_: (0,) * len(shape), pipeline_mode=pl.Buffered(1))


def _tile(t, pref):
    return pref if t % pref == 0 else t


def _dot(a, b):
    return jnp.dot(a.astype(bf16), b.astype(bf16), preferred_element_type=f32)


def _dot_nt(a, b):
    return lax.dot_general(a.astype(bf16), b.astype(bf16), NT, preferred_element_type=f32)


def _dot_tn(a, b):
    return lax.dot_general(a.astype(bf16), b.astype(bf16), TN, preferred_element_type=f32)


def _sigmoid(x):
    return 1.0 / (1.0 + jnp.exp(-x))


def _silu_grad(x):
    s = _sigmoid(x)
    return s * (1.0 + x * (1.0 - s))


def _rms(x, g):
    rstd = lax.rsqrt(jnp.mean(x * x, axis=-1, keepdims=True) + NORM_EPS)
    xh = x * rstd
    return xh * g, xh, rstd


def _rms_bwd(dn, xh, rstd, g):
    dxh = dn * g
    return rstd * (dxh - xh * jnp.mean(dxh * xh, axis=-1, keepdims=True))


def _acc_init(step, *refs):
    @pl.when(step == 0)
    def _():
        for r in refs:
            r[...] = jnp.zeros(r.shape, r.dtype)


def _acc_rows(ref, val):
    ref[0:1, :] += jnp.sum(val, axis=0, keepdims=True)


def _pw1_glu(x, g, w, b):
    t, d = x.shape
    tm = _tile(t, 256)

    nb_w = w.shape[0]

    def body(x_ref, g_ref, w_hbm, b_ref, n_ref, u_ref, gl_ref, w_ref, sems):
        _fetch_blocks(pl.program_id(0), w_hbm, w_ref, sems, True)
        n, _, _ = _rms(x_ref[...], g_ref[...])
        nb = n.astype(bf16)
        n_ref[...] = nb
        u = jnp.dot(nb, w_ref[...], preferred_element_type=f32) + b_ref[...]
        u_ref[...] = u.astype(bf16)
        gl_ref[...] = u[:, :d] * _sigmoid(u[:, d:])

    return _pc(body, name="pw1_glu", grid=(t // tm,),
               in_specs=[_rows(tm, d), _const((1, d)), ANY, _const((1, 2 * d))],
               out_specs=[_rows(tm, d), _rows(tm, 2 * d), _rows(tm, d)],
               out_shape=[jax.ShapeDtypeStruct((t, d), bf16), jax.ShapeDtypeStruct((t, 2 * d), bf16),
                          jax.ShapeDtypeStruct((t, d), f32)],
               scratch=[pltpu.VMEM((d, 2 * d), bf16), pltpu.SemaphoreType.DMA((nb_w,))],
               sem=("arbitrary",))(x, g, w, b)


def _ln_silu_mm_res(dc, ln_g, ln_b, w, b, res):
    t, d = dc.shape
    tm = _tile(t, 256)

    def body(x_ref, g_ref, bb_ref, w_ref, b_ref, r_ref, s_ref, o_ref):
        x = x_ref[...]
        xc = x - jnp.mean(x, axis=-1, keepdims=True)
        rstd = lax.rsqrt(jnp.mean(xc * xc, axis=-1, keepdims=True) + NORM_EPS)
        ln = xc * rstd * g_ref[...] + bb_ref[...]
        sb = (ln * _sigmoid(ln)).astype(bf16)
        s_ref[...] = sb
        o_ref[...] = r_ref[...] + jnp.dot(sb, w_ref[...], preferred_element_type=f32) + b_ref[...]

    return _pc(body, name="ln_silu_pw2", grid=(t // tm,),
               in_specs=[_rows(tm, d), _const((1, d)), _const((1, d)), _resident((d, d)), _const((1, d)), _rows(tm, d)],
               out_specs=[_rows(tm, d), _rows(tm, d)],
               out_shape=[jax.ShapeDtypeStruct((t, d), bf16), jax.ShapeDtypeStruct((t, d), f32)],
               sem=("parallel",))(dc, ln_g, ln_b, w, b, res)


def _fetch_blocks(step, w_hbm, dst, sems, by_cols, layer=None):
    nb_w = w_hbm.shape[0]
    step_rows, step_cols = w_hbm.shape[-2], w_hbm.shape[-1]

    @pl.when(step == 0)
    def _():
        copies = []
        for j in range(nb_w):
            src = w_hbm.at[j] if layer is None else w_hbm.at[j, layer]
            if by_cols:
                part = dst.at[:, pl.ds(j * step_cols, step_cols)]
            else:
                part = dst.at[pl.ds(j * step_rows, step_rows), :]
            copies.append(pltpu.make_async_copy(src, part, sems.at[j]))
        for cp in copies:
            cp.start()
        for cp in copies:
            cp.wait()


def _mlp_fwd(h, g, w1g, w2g, name):
    t, d = h.shape
    nb_w, _, bn = w1g.shape
    ff = nb_w * bn
    tm = _tile(t, 256)

    def body(h_ref, g_ref, w1_hbm, w2_hbm, n_ref, f_ref, r_ref, o_ref, w1_ref, w2_ref, sem1, sem2):
        _fetch_blocks(pl.program_id(0), w1_hbm, w1_ref, sem1, True)
        _fetch_blocks(pl.program_id(0), w2_hbm, w2_ref, sem2, False)
        hv = h_ref[...]
        n, _, _ = _rms(hv, g_ref[...])
        nb = n.astype(bf16)
        n_ref[...] = nb
        f = jnp.dot(nb, w1_ref[...], preferred_element_type=f32)
        f_ref[...] = f.astype(bf16)
        rb = jnp.square(jnp.maximum(f, 0.0)).astype(bf16)
        r_ref[...] = rb
        o_ref[...] = hv + jnp.dot(rb, w2_ref[...], preferred_element_type=f32)

    return _pc(body, name=name, grid=(t // tm,),
               in_specs=[_rows(tm, d), _const((1, d)), ANY, ANY],
               out_specs=[_rows(tm, d), _rows(tm, ff), _rows(tm, ff), _rows(tm, d)],
               out_shape=[jax.ShapeDtypeStruct((t, d), bf16), jax.ShapeDtypeStruct((t, ff), bf16),
                          jax.ShapeDtypeStruct((t, ff), bf16), jax.ShapeDtypeStruct((t, d), f32)],
               scratch=[pltpu.VMEM((d, ff), bf16), pltpu.VMEM((ff, d), bf16),
                        pltpu.SemaphoreType.DMA((nb_w,)), pltpu.SemaphoreType.DMA((nb_w,))],
               sem=("arbitrary",))(h, g, w1g, w2g)


def _softplus(x):
    return jnp.maximum(x, 0.0) + jnp.log(1.0 + jnp.exp(-jnp.abs(x)))


def _gdn_in(h, g, w_main, w_ab, a_log_pad, dt_pad, n_heads):
    t, d = h.shape
    tm = _tile(t, 256)

    def body(h_ref, g_ref, wm_ref, wab_ref, al_ref, dt_ref, n_ref, qkv_ref, z_ref, ab_ref, gb_ref):
        n, _, _ = _rms(h_ref[...], g_ref[...])
        nb = n.astype(bf16)
        n_ref[...] = nb
        p = jnp.dot(nb, wm_ref[...], preferred_element_type=f32)
        qkv_ref[...] = p[:, :3 * d]
        z_ref[...] = p[:, 3 * d:]
        ab = jnp.dot(nb, wab_ref[...], preferred_element_type=f32)
        ab_ref[...] = ab
        lane = lax.broadcasted_iota(jnp.int32, ab.shape, 1)
        decay = -jnp.exp(al_ref[...]) * _softplus(ab + dt_ref[...])
        gb_ref[...] = jnp.where(lane < n_heads, decay, jnp.where(lane < 2 * n_heads, _sigmoid(ab), 0.0))

    return _pc(body, name="gdn_in", grid=(t // tm,),
               in_specs=[_rows(tm, d), _const((1, d)), _resident((d, 4 * d)), _resident((d, LANES)),
                         _const((1, LANES)), _const((1, LANES))],
               out_specs=[_rows(tm, d), _rows(tm, 3 * d), _rows(tm, d), _rows(tm, LANES), _rows(tm, LANES)],
               out_shape=[jax.ShapeDtypeStruct((t, d), bf16), jax.ShapeDtypeStruct((t, 3 * d), f32),
                          jax.ShapeDtypeStruct((t, d), f32), jax.ShapeDtypeStruct((t, LANES), f32),
                          jax.ShapeDtypeStruct((t, LANES), f32)],
               sem=("parallel",))(h, g, w_main, w_ab, a_log_pad, dt_pad)


def _gated_norm_mm_res(o, z, ng, w, res, n_heads):
    t, d = o.shape
    tm = _tile(t, 256)

    def body(o_ref, z_ref, ng_ref, w_ref, r_ref, on_ref, out_ref):
        for hd in range(n_heads):
            sl = slice(hd * LANES, (hd + 1) * LANES)
            rn, _, _ = _rms(o_ref[:, sl], ng_ref[...])
            zz = z_ref[:, sl]
            on_ref[:, sl] = (rn * (zz * _sigmoid(zz))).astype(bf16)
        out_ref[...] = r_ref[...] + jnp.dot(on_ref[...], w_ref[...], preferred_element_type=f32)

    return _pc(body, name="gated_norm_wout", grid=(t // tm,),
               in_specs=[_rows(tm, d), _rows(tm, d), _const((1, LANES)), _resident((d, d)), _rows(tm, d)],
               out_specs=[_rows(tm, d), _rows(tm, d)],
               out_shape=[jax.ShapeDtypeStruct((t, d), bf16), jax.ShapeDtypeStruct((t, d), f32)],
               sem=("parallel",))(o, z, ng, w, res)


def _loss_head(h, g, tgt):
    t, d = h.shape
    tm = _tile(t, 256)

    def body(h_ref, g_ref, t_ref, loss_ref, dh_ref, dg_ref):
        _acc_init(pl.program_id(0), loss_ref, dg_ref)
        gv = g_ref[...]
        y, xh, rstd = _rms(h_ref[...], gv)
        e = y - t_ref[...]
        loss_ref[...] += 0.5 * jnp.sum(jnp.mean(e * e, axis=-1, keepdims=True))
        dy = e * (1.0 / d)
        _acc_rows(dg_ref, dy * xh)
        dh_ref[...] = _rms_bwd(dy, xh, rstd, gv)

    return _pc(body, name="loss_head", grid=(t // tm,),
               in_specs=[_rows(tm, d), _const((1, d)), _rows(tm, d)],
               out_specs=[_const((SUBLANES, LANES)), _rows(tm, d), _const((SUBLANES, d))],
               out_shape=[jax.ShapeDtypeStruct((SUBLANES, LANES), f32), jax.ShapeDtypeStruct((t, d), f32),
                          jax.ShapeDtypeStruct((SUBLANES, d), f32)],
               sem=("arbitrary",))(h, g, tgt)


def _rms_bwd_res(dn, h, g, dres, name):
    t, d = h.shape
    tm = _tile(t, 256)

    def body(dn_ref, h_ref, g_ref, dr_ref, dh_ref, dg_ref):
        _acc_init(pl.program_id(0), dg_ref)
        gv = g_ref[...]
        _, xh, rstd = _rms(h_ref[...], gv)
        dn = dn_ref[...]
        _acc_rows(dg_ref, dn * xh)
        dh_ref[...] = dr_ref[...] + _rms_bwd(dn, xh, rstd, gv)

    return _pc(body, name=name, grid=(t // tm,),
               in_specs=[_rows(tm, d), _rows(tm, d), _const((1, d)), _rows(tm, d)],
               out_specs=[_rows(tm, d), _const((SUBLANES, d))],
               out_shape=[jax.ShapeDtypeStruct((t, d), f32), jax.ShapeDtypeStruct((SUBLANES, d), f32)],
               sem=("arbitrary",))(dn, h, g, dres)


def _mlp_bwd(dho, h, g, fb, w1g, w2g, name):
    t, d = h.shape
    nb_w, _, bn = w1g.shape
    ff = nb_w * bn
    tm = _tile(t, 256)

    def body(do_ref, h_ref, g_ref, f_ref, w1_hbm, w2_hbm, df_ref, dh_ref, dg_ref, cs_ref, w1_ref, w2_ref, sem1, sem2):
        _fetch_blocks(pl.program_id(0), w1_hbm, w1_ref, sem1, True)
        _fetch_blocks(pl.program_id(0), w2_hbm, w2_ref, sem2, False)
        _acc_init(pl.program_id(0), dg_ref, cs_ref)
        do = do_ref[...]
        dr = lax.dot_general(do.astype(bf16), w2_ref[...], NT, preferred_element_type=f32)
        dfb = (dr * (2.0 * jnp.maximum(f_ref[...].astype(f32), 0.0))).astype(bf16)
        df_ref[...] = dfb
        dn = lax.dot_general(dfb, w1_ref[...], NT, preferred_element_type=f32)
        gv = g_ref[...]
        _, xh, rstd = _rms(h_ref[...], gv)
        _acc_rows(dg_ref, dn * xh)
        dh = do + _rms_bwd(dn, xh, rstd, gv)
        dh_ref[...] = dh
        _acc_rows(cs_ref, dh)

    return _pc(body, name=name, grid=(t // tm,),
               in_specs=[_rows(tm, d), _rows(tm, d), _const((1, d)), _rows(tm, ff), ANY, ANY],
               out_specs=[_rows(tm, ff), _rows(tm, d), _const((SUBLANES, d)), _const((SUBLANES, d))],
               out_shape=[jax.ShapeDtypeStruct((t, ff), bf16), jax.ShapeDtypeStruct((t, d), f32),
                          jax.ShapeDtypeStruct((SUBLANES, d), f32), jax.ShapeDtypeStruct((SUBLANES, d), f32)],
               scratch=[pltpu.VMEM((d, ff), bf16), pltpu.VMEM((ff, d), bf16),
                        pltpu.SemaphoreType.DMA((nb_w,)), pltpu.SemaphoreType.DMA((nb_w,))],
               sem=("arbitrary",))(dho, h, g, fb, w1g, w2g)


def _mm_nt(pairs, name):
    t = pairs[0][0].shape[0]
    w0 = pairs[0][1]
    k = (w0[0] if isinstance(w0, tuple) else w0).shape[0]
    tm = _tile(t, 256)
    npair = len(pairs)

    def body(*refs):
        o_ref = refs[2 * npair]
        acc = None
        for p in range(npair):
            part = lax.dot_general(refs[2 * p][...].astype(bf16), refs[2 * p + 1][...], NT, preferred_element_type=f32)
            acc = part if acc is None else acc + part
        o_ref[...] = acc

    in_specs, args = [], []
    for dy, w in pairs:
        nn = dy.shape[1]
        if isinstance(w, tuple):
            w, idx = w
            wspec = pl.BlockSpec((k, nn), lambda *_, idx=idx: (0, idx), pipeline_mode=pl.Buffered(1))
        else:
            wspec = _resident(w.shape)
        in_specs += [_rows(tm, nn), wspec]
        args += [dy, w]
    return _pc(body, name=name, grid=(t // tm,), in_specs=in_specs, out_specs=_rows(tm, k),
               out_shape=jax.ShapeDtypeStruct((t, k), f32), sem=("parallel",))(*args)


def _mm_nt_blocked(dy, wb, name):
    t = dy.shape[0]
    nb_w, k, bn = wb.shape
    tm = _tile(t, 256)

    def body(dy_ref, w_hbm, o_ref, w_ref, sems):
        _fetch_blocks(pl.program_id(0), w_hbm, w_ref, sems, True)
        o_ref[...] = lax.dot_general(dy_ref[...].astype(bf16), w_ref[...], NT, preferred_element_type=f32)

    return _pc(body, name=name, grid=(t // tm,), in_specs=[_rows(tm, nb_w * bn), ANY],
               out_specs=_rows(tm, k), out_shape=jax.ShapeDtypeStruct((t, k), f32),
               scratch=[pltpu.VMEM((k, nb_w * bn), bf16), pltpu.SemaphoreType.DMA((nb_w,))],
               sem=("arbitrary",))(dy, wb)


def _mm_tn_blocked(x, dy, name, out_dtype=f32):
    t, k = x.shape
    bn = dy.shape[1] // N_DEV
    tm = _tile(t, 512)
    jb = N_DEV
    while jb > 1 and k * jb * bn * 4 > 8 * 1024 * 1024:
        jb //= 2
    nt = t // tm

    def body(x_ref, dy_ref, o_ref, *acc):
        acc_ref = acc[0] if acc else o_ref
        _acc_init(pl.program_id(1), acc_ref)
        xt = x_ref[...].astype(bf16).T
        for jj in range(jb):
            acc_ref[jj] += jnp.dot(xt, dy_ref[:, jj * bn:(jj + 1) * bn].astype(bf16), preferred_element_type=f32)
        if acc:
            @pl.when(pl.program_id(1) == nt - 1)
            def _():
                o_ref[...] = acc_ref[...].astype(out_dtype)

    return _pc(body, name=name, grid=(N_DEV // jb, nt),
               in_specs=[pl.BlockSpec((tm, k), lambda j, i: (i, 0)), pl.BlockSpec((tm, jb * bn), lambda j, i: (i, j))],
               out_specs=pl.BlockSpec((jb, k, bn), lambda j, i: (j, 0, 0)),
               out_shape=jax.ShapeDtypeStruct((N_DEV, k, bn), out_dtype),
               scratch=[] if out_dtype == f32 else [pltpu.VMEM((jb, k, bn), f32)],
               sem=("parallel", "arbitrary"))(x, dy)


def _mm_tn(x, dy, name, out_dtype=f32):
    t, k = x.shape
    n = dy.shape[1]
    tm = _tile(t, 512)
    cap = max(LANES, (2 * 1024 * 1024) // k)
    tn = n
    if n > cap:
        tn = max(c for c in range(LANES, cap + 1, LANES) if n % c == 0)
    nt = t // tm

    def body(x_ref, dy_ref, o_ref, *acc):
        acc_ref = acc[0] if acc else o_ref
        _acc_init(pl.program_id(1), acc_ref)
        acc_ref[...] += lax.dot_general(x_ref[...].astype(bf16), dy_ref[...].astype(bf16), TN, preferred_element_type=f32)
        if acc:
            @pl.when(pl.program_id(1) == nt - 1)
            def _():
                o_ref[...] = acc_ref[...].astype(out_dtype)

    return _pc(body, name=name, grid=(n // tn, nt),
               in_specs=[pl.BlockSpec((tm, k), lambda j, i: (i, 0)), pl.BlockSpec((tm, tn), lambda j, i: (i, j))],
               out_specs=pl.BlockSpec((k, tn), lambda j, i: (0, j)),
               out_shape=jax.ShapeDtypeStruct((k, n), out_dtype),
               scratch=[] if out_dtype == f32 else [pltpu.VMEM((k, tn), f32)],
               sem=("parallel", "arbitrary"))(x, dy)


def _gated_norm_bwd(don, o, z, ng, n_heads):
    t, d = o.shape
    tm = _tile(t, 256)

    def body(don_ref, o_ref, z_ref, ng_ref, do_ref, dz_ref, dng_ref):
        _acc_init(pl.program_id(0), dng_ref)
        gv = ng_ref[...]
        for hd in range(n_heads):
            sl = slice(hd * LANES, (hd + 1) * LANES)
            rn, xh, rstd = _rms(o_ref[:, sl], gv)
            zz = z_ref[:, sl]
            don = don_ref[:, sl]
            dz_ref[:, sl] = don * rn * _silu_grad(zz)
            drn = don * (zz * _sigmoid(zz))
            _acc_rows(dng_ref, drn * xh)
            do_ref[:, sl] = _rms_bwd(drn, xh, rstd, gv)

    return _pc(body, name="gated_norm_bwd", grid=(t // tm,),
               in_specs=[_rows(tm, d), _rows(tm, d), _rows(tm, d), _const((1, LANES))],
               out_specs=[_rows(tm, d), _rows(tm, d), _const((SUBLANES, LANES))],
               out_shape=[jax.ShapeDtypeStruct((t, d), f32), jax.ShapeDtypeStruct((t, d), f32),
                          jax.ShapeDtypeStruct((SUBLANES, LANES), f32)],
               sem=("arbitrary",))(don, o, z, ng)


def _gates_bwd(dgb, ab, a_log_pad, dt_pad, n_heads):
    t = ab.shape[0]
    tm = _tile(t, 256)

    def body(dgb_ref, ab_ref, al_ref, dt_ref, dab_ref, dal_ref, ddt_ref):
        _acc_init(pl.program_id(0), dal_ref, ddt_ref)
        ab = ab_ref[...]
        dgb = dgb_ref[...]
        lane = lax.broadcasted_iota(jnp.int32, ab.shape, 1)
        is_a = lane < n_heads
        is_b = jnp.logical_and(lane >= n_heads, lane < 2 * n_heads)
        xa = ab + dt_ref[...]
        neg_a = -jnp.exp(al_ref[...])
        dg_da = neg_a * _sigmoid(xa)
        beta = _sigmoid(ab)
        da = jnp.where(is_a, dgb * dg_da, 0.0)
        dab_ref[...] = da + jnp.where(is_b, dgb * beta * (1.0 - beta), 0.0)
        _acc_rows(dal_ref, jnp.where(is_a, dgb * neg_a * _softplus(xa), 0.0))
        _acc_rows(ddt_ref, da)

    return _pc(body, name="gates_bwd", grid=(t // tm,),
               in_specs=[_rows(tm, LANES), _rows(tm, LANES), _const((1, LANES)), _const((1, LANES))],
               out_specs=[_rows(tm, LANES), _const((SUBLANES, LANES)), _const((SUBLANES, LANES))],
               out_shape=[jax.ShapeDtypeStruct((t, LANES), f32), jax.ShapeDtypeStruct((SUBLANES, LANES), f32),
                          jax.ShapeDtypeStruct((SUBLANES, LANES), f32)],
               sem=("arbitrary",))(dgb, ab, a_log_pad, dt_pad)


def _ln_silu_bwd(ds, dc, ln_g, ln_b):
    t, d = dc.shape
    tm = _tile(t, 256)

    def body(ds_ref, x_ref, g_ref, b_ref, dx_ref, dg_ref, db_ref, cs_ref):
        _acc_init(pl.program_id(0), dg_ref, db_ref, cs_ref)
        x = x_ref[...]
        gv = g_ref[...]
        xc = x - jnp.mean(x, axis=-1, keepdims=True)
        rstd = lax.rsqrt(jnp.mean(xc * xc, axis=-1, keepdims=True) + NORM_EPS)
        xh = xc * rstd
        ln = xh * gv + b_ref[...]
        dln = ds_ref[...] * _silu_grad(ln)
        _acc_rows(dg_ref, dln * xh)
        _acc_rows(db_ref, dln)
        dxh = dln * gv
        dx = rstd * (dxh - jnp.mean(dxh, axis=-1, keepdims=True) - xh * jnp.mean(dxh * xh, axis=-1, keepdims=True))
        dx_ref[...] = dx
        _acc_rows(cs_ref, dx)

    return _pc(body, name="ln_silu_bwd", grid=(t // tm,),
               in_specs=[_rows(tm, d), _rows(tm, d), _const((1, d)), _const((1, d))],
               out_specs=[_rows(tm, d), _const((SUBLANES, d)), _const((SUBLANES, d)), _const((SUBLANES, d))],
               out_shape=[jax.ShapeDtypeStruct((t, d), f32)] + [jax.ShapeDtypeStruct((SUBLANES, d), f32)] * 3,
               sem=("arbitrary",))(ds, dc, ln_g, ln_b)


def _glu_bwd(dgl, ub):
    t, d = dgl.shape
    tm = _tile(t, 256)

    def body(dgl_ref, u_ref, du_ref, cs_ref):
        _acc_init(pl.program_id(0), cs_ref)
        dgl = dgl_ref[...]
        a = u_ref[:, :d].astype(f32)
        sb = _sigmoid(u_ref[:, d:].astype(f32))
        da = dgl * sb
        db = dgl * a * sb * (1.0 - sb)
        du_ref[:, :d] = da.astype(bf16)
        du_ref[:, d:] = db.astype(bf16)
        cs_ref[0:1, :d] += jnp.sum(da, axis=0, keepdims=True)
        cs_ref[0:1, d:] += jnp.sum(db, axis=0, keepdims=True)

    return _pc(body, name="glu_bwd", grid=(t // tm,),
               in_specs=[_rows(tm, d), _rows(tm, 2 * d)],
               out_specs=[_rows(tm, 2 * d), _const((SUBLANES, 2 * d))],
               out_shape=[jax.ShapeDtypeStruct((t, 2 * d), bf16), jax.ShapeDtypeStruct((SUBLANES, 2 * d), f32)],
               sem=("arbitrary",))(dgl, ub)


def _conv_rows(s):
    return 256 if s % 256 == 0 else s


def _conv_tap_sum(pad_ref, w_ref, base, rows, width):
    acc = jnp.zeros((rows, LANES), f32)
    for j in range(width):
        acc = acc + w_ref[j:j + 1, :] * pad_ref[pl.ds(base + CONV_PAD - (width - 1) + j, rows), :]
    return acc


def _l2_silu_post(c, j, n_heads, scale):
    a = c * _sigmoid(c)
    r = lax.rsqrt(jnp.sum(a * a, axis=-1, keepdims=True) + L2_EPS)
    mult = jnp.where(j < n_heads, r * scale, jnp.where(j < 2 * n_heads, r, 1.0))
    return a, r, a * mult


def _dwconv_fwd(x, w, b, name, qk_heads=None):
    bl, s, cn = x.shape
    width = w.shape[0]
    rows = _conv_rows(s)
    scale = float(LANES) ** -0.5

    def body(x_ref, w_ref, b_ref, o_ref, pad_ref):
        j = pl.program_id(1)
        pad_ref[0:CONV_PAD, :] = jnp.zeros((CONV_PAD, LANES), f32)
        pad_ref[CONV_PAD:, :] = x_ref[0]

        def step(i, carry):
            base = pl.multiple_of(i * rows, rows)
            acc = _conv_tap_sum(pad_ref, w_ref, base, rows, width)
            if qk_heads is None:
                acc = acc + b_ref[...]
            else:
                _, _, acc = _l2_silu_post(acc, j, qk_heads, scale)
            o_ref[0, pl.ds(base, rows), :] = acc
            return carry

        lax.fori_loop(0, s // rows, step, 0)

    return _pc(body, name=name, grid=(bl, cn // LANES),
               in_specs=[pl.BlockSpec((1, s, LANES), lambda bi, j: (bi, 0, j)),
                         pl.BlockSpec((width, LANES), lambda bi, j: (0, j)),
                         pl.BlockSpec((1, LANES), lambda bi, j: (0, j))],
               out_specs=pl.BlockSpec((1, s, LANES), lambda bi, j: (bi, 0, j)),
               out_shape=jax.ShapeDtypeStruct((bl, s, cn), f32),
               scratch=[pltpu.VMEM((s + CONV_PAD, LANES), f32)],
               sem=("parallel", "parallel"))(x, w, b)


def _dwconv_bwd(x, dys, w, name, qk_heads=None):
    bl, s, cn = x.shape
    width = w.shape[0]
    wp = -(-width // SUBLANES) * SUBLANES
    rows = _conv_rows(s)
    scale = float(LANES) ** -0.5
    nblk = s // rows
    ndy = len(dys)

    def body(*refs):
        x_ref, w_ref = refs[0], refs[1]
        dy_refs = refs[2:2 + ndy]
        dx_ref, dw_ref, db_ref, xpad, dypad, acc = refs[2 + ndy:]
        j = pl.program_id(0)
        bi = pl.program_id(1)
        _acc_init(bi, acc, db_ref)
        xpad[0:CONV_PAD, :] = jnp.zeros((CONV_PAD, LANES), f32)
        xpad[CONV_PAD:, :] = x_ref[0]
        dypad[s:, :] = jnp.zeros((CONV_PAD, LANES), f32)
        if qk_heads is None:
            dypad[0:s, :] = dy_refs[0][0]
        else:
            def pre(i, carry):
                base = pl.multiple_of(i * rows, rows)
                c = _conv_tap_sum(xpad, w_ref, base, rows, width)
                a, r, _ = _l2_silu_post(c, j, qk_heads, scale)
                dq = dy_refs[0][0, pl.ds(base, rows), :]
                dk = dy_refs[1][0, pl.ds(base, rows), :]
                dv = dy_refs[2][0, pl.ds(base, rows), :]
                dy = jnp.where(j < qk_heads, dq * scale, jnp.where(j < 2 * qk_heads, dk, dv))
                da_l2 = r * (dy - a * (r * r) * jnp.sum(a * dy, axis=-1, keepdims=True))
                da = jnp.where(j < 2 * qk_heads, da_l2, dy)
                dypad[pl.ds(base, rows), :] = da * _silu_grad(c)
                return carry

            lax.fori_loop(0, nblk, pre, 0)

        def step(i, carry):
            base = pl.multiple_of(i * rows, rows)
            dxa = jnp.zeros((rows, LANES), f32)
            for jj in range(width):
                dxa = dxa + w_ref[jj:jj + 1, :] * dypad[pl.ds(base + (width - 1) - jj, rows), :]
            dx_ref[0, pl.ds(base, rows), :] = dxa
            dyc = dypad[pl.ds(base, rows), :]
            db_ref[...] += dyc.reshape(rows // SUBLANES, SUBLANES, LANES).sum(axis=0)
            for jj in range(width):
                prod = dyc * xpad[pl.ds(base + CONV_PAD - (width - 1) + jj, rows), :]
                acc[jj * SUBLANES:(jj + 1) * SUBLANES, :] += prod.reshape(rows // SUBLANES, SUBLANES, LANES).sum(axis=0)
            return carry

        lax.fori_loop(0, nblk, step, 0)

        @pl.when(bi == bl - 1)
        def _():
            dw_ref[...] = jnp.zeros((wp, LANES), f32)
            for jj in range(width):
                dw_ref[jj:jj + 1, :] = jnp.sum(acc[jj * SUBLANES:(jj + 1) * SUBLANES, :], axis=0, keepdims=True)

    if qk_heads is None:
        dy_specs = [pl.BlockSpec((1, s, LANES), lambda j, bi: (bi, 0, j))]
    else:
        hh = qk_heads
        dy_specs = [pl.BlockSpec((1, s, LANES), lambda j, bi: (bi * hh + jnp.minimum(j, hh - 1), 0, 0)),
                    pl.BlockSpec((1, s, LANES), lambda j, bi: (bi * hh + jnp.clip(j - hh, 0, hh - 1), 0, 0)),
                    pl.BlockSpec((1, s, LANES), lambda j, bi: (bi * hh + jnp.clip(j - 2 * hh, 0, hh - 1), 0, 0))]
    return _pc(body, name=name, grid=(cn // LANES, bl),
               in_specs=[pl.BlockSpec((1, s, LANES), lambda j, bi: (bi, 0, j)),
                         pl.BlockSpec((width, LANES), lambda j, bi: (0, j))] + dy_specs,
               out_specs=[pl.BlockSpec((1, s, LANES), lambda j, bi: (bi, 0, j)),
                          pl.BlockSpec((wp, LANES), lambda j, bi: (0, j)),
                          pl.BlockSpec((SUBLANES, LANES), lambda j, bi: (0, j))],
               out_shape=[jax.ShapeDtypeStruct((bl, s, cn), f32), jax.ShapeDtypeStruct((wp, cn), f32),
                          jax.ShapeDtypeStruct((SUBLANES, cn), f32)],
               scratch=[pltpu.VMEM((s + CONV_PAD, LANES), f32), pltpu.VMEM((s + CONV_PAD, LANES), f32),
                        pltpu.VMEM((width * SUBLANES, LANES), f32)],
               sem=("parallel", "arbitrary"))(x, w, *dys)


def _group_rows(s):
    for rows in (256, 128):
        if s % rows == 0:
            return rows
    return CHUNK


def _group_masks(rows):
    r = lax.broadcasted_iota(jnp.int32, (rows, rows), 0)
    c = lax.broadcasted_iota(jnp.int32, (rows, rows), 1)
    same_chunk = (r >> 6) == (c >> 6)
    return r, c, same_chunk


def _decay(gc_col, gc_row, causal):
    return jnp.exp(jnp.where(causal, gc_col - gc_row, NEG))


def _inv_unit_lower(a, r, c):
    return _inv_unit_lower_many([a], r, c)[0]


def _inv_unit_lower_many(mats, r, c):
    eye = (r == c).astype(f32)
    same16 = (r >> 4) == (c >> 4)
    ads = [jnp.where(same16, a, 0.0) for a in mats]
    aos = [a - ad for a, ad in zip(mats, ads)]
    xs = ads
    tds = [eye - x for x in xs]
    for _ in range(3):
        xs = [_dot(x, x) for x in xs]
        tds = [td + _dot(td, x) for td, x in zip(tds, xs)]
    bs = [_dot(td, ao) for td, ao in zip(tds, aos)]
    b2s = [_dot(b, b) for b in bs]
    b3s = [_dot(b, b2) for b, b2 in zip(bs, b2s)]
    return [_dot(eye - b + b2 - b3, td) for b, b2, b3, td in zip(bs, b2s, b3s, tds)]


def _lane_cumsum(x, y, reverse, name):
    rr = x.shape[0]

    def body(x_ref, y_ref, o_ref):
        i = lax.broadcasted_iota(jnp.int32, (CHUNK, CHUNK), 0)
        j = lax.broadcasted_iota(jnp.int32, (CHUNK, CHUNK), 1)
        tri = ((i >= j) if reverse else (i <= j)).astype(f32)
        o_ref[...] = jnp.dot(x_ref[...] + y_ref[...], tri, precision=HI, preferred_element_type=f32)

    spec = pl.BlockSpec((rr, CHUNK), lambda: (0, 0))
    return pl.pallas_call(body, name=name, in_specs=[spec, spec], out_specs=spec,
                          out_shape=jax.ShapeDtypeStruct((rr, CHUNK), f32))(x, y)


def _gdn_specs(n_heads, nblk, rows, hp=1, rev=False):
    def blk(n):
        return nblk - 1 - n if rev else n

    def qkv(off):
        return pl.BlockSpec((rows, hp * LANES), lambda g, n: (
            lax.div(g * hp, n_heads) * nblk + blk(n), lax.div(off * n_heads + lax.rem(g * hp, n_heads), hp)))

    def per_head(last, mult=1):
        return pl.BlockSpec((hp, rows * mult, last), lambda g, n: (g, blk(n), 0))

    def row_vec():
        return pl.BlockSpec((hp, 1, 1, rows), lambda g, n: (g, blk(n), 0, 0))

    return qkv, per_head, row_vec


def _gdn_prep(qkv, cols, grow, bl, s, n_heads):
    rows = _group_rows(s)
    nblk = s // rows
    bh_n = bl * n_heads
    hp = PREP_HEADS_PER_STEP if n_heads % PREP_HEADS_PER_STEP == 0 else 1
    qkv_spec, ph, rv = _gdn_specs(n_heads, nblk, rows, hp)

    def body(k_ref, v_ref, cols_ref, grow_ref, u_ref, w_ref, t_ref):
        r, c, same = _group_masks(rows)
        causal = jnp.logical_and(same, r >= c)
        mats, rhs = [], []
        for h in range(hp):
            hs = slice(h * LANES, (h + 1) * LANES)
            k = k_ref[:, hs]
            gc = cols_ref[h, :, 0:1]
            beta = cols_ref[h, :, 1:2]
            kb = k * beta
            mats.append(jnp.where(r > c, _dot_nt(kb, k) * _decay(gc, grow_ref[h, 0], causal), 0.0))
            rhs.append((v_ref[:, hs] * beta, kb * jnp.exp(gc)))
        for h, tm in enumerate(_inv_unit_lower_many(mats, r, c)):
            tb = tm.astype(bf16)
            u_ref[h] = jnp.dot(tb, rhs[h][0].astype(bf16), preferred_element_type=f32)
            w_ref[h] = jnp.dot(tb, rhs[h][1].astype(bf16), preferred_element_type=f32)
            t_ref[h] = tb

    return _pc(body, name="gdn_prep", grid=(bh_n // hp, nblk),
               in_specs=[qkv_spec(1), qkv_spec(2), ph(2), rv()],
               out_specs=[ph(LANES), ph(LANES), ph(rows)],
               out_shape=[jax.ShapeDtypeStruct((bh_n, s, LANES), f32), jax.ShapeDtypeStruct((bh_n, s, LANES), f32),
                          jax.ShapeDtypeStruct((bh_n, s, rows), bf16)],
               sem=("parallel", "parallel"))(qkv, qkv, cols, grow)


def _gdn_scan(qkv, u, w, cols, grow, bl, s, n_heads):
    rows = _group_rows(s)
    g_chunks = rows // CHUNK
    nblk = s // rows
    bh_n = bl * n_heads
    d = n_heads * LANES
    hp = HEADS_PER_STEP if n_heads % HEADS_PER_STEP == 0 else 1
    qkv_spec, ph, rv = _gdn_specs(n_heads, nblk, rows, hp)

    def body(q_ref, k_ref, u_ref, w_ref, cols_ref, grow_ref, o_ref, vn_ref, ss_ref, s_scr):
        _acc_init(pl.program_id(1), s_scr)
        r, c, same = _group_masks(rows)
        causal = jnp.logical_and(same, r >= c)
        qs, ks, gcs, qgs, ps = [], [], [], [], []
        for h in range(hp):
            hs = slice(h * LANES, (h + 1) * LANES)
            q, k, gc = q_ref[:, hs], k_ref[:, hs], cols_ref[h, :, 0:1]
            qs.append(q)
            ks.append(k)
            gcs.append(gc)
            qgs.append(q * jnp.exp(gc))
            ps.append(_dot_nt(q, k) * _decay(gc, grow_ref[h, 0], causal))
        st = [s_scr[h] for h in range(hp)]
        o_state = [[] for _ in range(hp)]
        vns = [[] for _ in range(hp)]
        for ci in range(g_chunks):
            sl = slice(ci * CHUNK, (ci + 1) * CHUNK)
            for h in range(hp):
                ss_ref[h, ci * LANES:(ci + 1) * LANES, :] = st[h]
                vn = u_ref[h, sl, :] - _dot(w_ref[h, sl, :], st[h])
                vns[h].append(vn)
                o_state[h].append(_dot(qgs[h][sl], st[h]))
                gc = gcs[h][sl]
                g_last = gc[CHUNK - 1:CHUNK, :]
                st[h] = jnp.exp(g_last) * st[h] + _dot_tn(ks[h][sl] * jnp.exp(g_last - gc), vn)
        for h in range(hp):
            s_scr[h] = st[h]
            vn_all = jnp.concatenate(vns[h], axis=0)
            vn_ref[h] = vn_all
            o_ref[:, h * LANES:(h + 1) * LANES] = jnp.concatenate(o_state[h], axis=0) + _dot(ps[h], vn_all)

    return _pc(body, name="gdn_scan", grid=(bh_n // hp, nblk),
               in_specs=[qkv_spec(0), qkv_spec(1), ph(LANES), ph(LANES), ph(2), rv()],
               out_specs=[qkv_spec(0), ph(LANES), ph(LANES, mult=LANES // CHUNK)],
               out_shape=[jax.ShapeDtypeStruct((bl * s, d), f32), jax.ShapeDtypeStruct((bh_n, s, LANES), f32),
                          jax.ShapeDtypeStruct((bh_n, (s // CHUNK) * LANES, LANES), f32)],
               scratch=[pltpu.VMEM((hp, LANES, LANES), f32)],
               sem=("parallel", "arbitrary"))(qkv, qkv, u, w, cols, grow)


def _gdn_scan_bwd(do, qkv, w, vn, cols, grow, ss, bl, s, n_heads):
    rows = _group_rows(s)
    g_chunks = rows // CHUNK
    nblk = s // rows
    bh_n = bl * n_heads
    hp = HEADS_PER_STEP if n_heads % HEADS_PER_STEP == 0 else 1
    qkv_spec, ph, rv = _gdn_specs(n_heads, nblk, rows, hp, rev=True)

    def body(do_ref, q_ref, k_ref, w_ref, vn_ref, cols_ref, grow_ref, ss_ref,
             du_ref, dw_ref, dq_ref, dk_ref, dcol_ref, drow_ref, ds_scr):
        _acc_init(pl.program_id(1), ds_scr)
        r, c, same = _group_masks(rows)
        causal = jnp.logical_and(same, r >= c)
        last_row = lax.broadcasted_iota(jnp.int32, (CHUNK, 1), 0) == CHUNK - 1
        pre = []
        for h in range(hp):
            hs = slice(h * LANES, (h + 1) * LANES)
            do, q, k = do_ref[:, hs], q_ref[:, hs], k_ref[:, hs]
            vn = vn_ref[h]
            gc = cols_ref[h, :, 0:1]
            dmat = _decay(gc, grow_ref[h, 0], causal)
            gam = jnp.exp(gc)
            qk = _dot_nt(q, k)
            dpd = _dot_nt(do, vn) * dmat
            ep = dpd * qk
            drow_ref[h, 0] = -jnp.sum(ep, axis=0, keepdims=True)
            pre.append(dict(do=do, k=k, vn=vn, gc=gc, gam=gam, qg=q * gam,
                            dvn_intra=_dot_tn(qk * dmat, do), dq_intra=_dot(dpd, k), dk_intra=_dot_tn(dpd, q),
                            ep_rows=jnp.sum(ep, axis=-1, keepdims=True)))
        ds = [ds_scr[h] for h in range(hp)]
        for ci in reversed(range(g_chunks)):
            sl = slice(ci * CHUNK, (ci + 1) * CHUNK)
            for h in range(hp):
                pr = pre[h]
                st = ss_ref[h, ci * LANES:(ci + 1) * LANES, :]
                gc = pr["gc"][sl]
                g_last = gc[CHUNK - 1:CHUNK, :]
                gam_last = jnp.exp(g_last)
                kd_scale = jnp.exp(g_last - gc)
                kdec = pr["k"][sl] * kd_scale
                do_c = pr["do"][sl]
                qg_c = pr["qg"][sl]
                dvn = pr["dvn_intra"][sl] + _dot(kdec, ds[h])
                dkdec = _dot_nt(pr["vn"][sl], ds[h])
                du_ref[h, sl, :] = dvn
                dw_ref[h, sl, :] = -_dot_nt(dvn, st)
                dqg = _dot_nt(do_c, st)
                dq_ref[h, sl, :] = dqg * pr["gam"][sl] + pr["dq_intra"][sl]
                dk_ref[h, sl, :] = pr["dk_intra"][sl] + dkdec * kd_scale
                kd_rows = jnp.sum(dkdec * kdec, axis=-1, keepdims=True)
                extra = jnp.sum(kd_rows) + gam_last * jnp.sum(st * ds[h])
                dcol_ref[h, sl, :] = (jnp.sum(dqg * qg_c, axis=-1, keepdims=True) + pr["ep_rows"][sl] - kd_rows
                                      + jnp.where(last_row, extra, 0.0))
                ds[h] = gam_last * ds[h] + _dot_tn(qg_c, do_c) - _dot_tn(w_ref[h, sl, :], dvn)
        for h in range(hp):
            ds_scr[h] = ds[h]

    return _pc(body, name="gdn_scan_bwd", grid=(bh_n // hp, nblk),
               in_specs=[qkv_spec(0), qkv_spec(0), qkv_spec(1), ph(LANES), ph(LANES), ph(2), rv(),
                         ph(LANES, mult=LANES // CHUNK)],
               out_specs=[ph(LANES), ph(LANES), ph(LANES), ph(LANES), ph(1), rv()],
               out_shape=[jax.ShapeDtypeStruct((bh_n, s, LANES), f32)] * 4
               + [jax.ShapeDtypeStruct((bh_n, s, 1), f32), jax.ShapeDtypeStruct((bh_n, nblk, 1, rows), f32)],
               scratch=[pltpu.VMEM((hp, LANES, LANES), f32)],
               sem=("parallel", "arbitrary"))(do, qkv, qkv, w, vn, cols, grow, ss)


def _gdn_prep_bwd(qkv, cols, grow, tmat, du, dw, dk_scan, dcol_scan, drow_scan, bl, s, n_heads):
    rows = _group_rows(s)
    nblk = s // rows
    bh_n = bl * n_heads
    hp = PREP_HEADS_PER_STEP if n_heads % PREP_HEADS_PER_STEP == 0 else 1
    qkv_spec, ph, rv = _gdn_specs(n_heads, nblk, rows, hp)

    def body(k_ref, v_ref, cols_ref, grow_ref, t_ref, du_ref, dw_ref, dks_ref, dcs_ref, drs_ref,
             dk_ref, dv_ref, dcols_ref, drow_ref):
        r, c, same = _group_masks(rows)
        causal = jnp.logical_and(same, r >= c)
        hh = range(hp)
        ks = [k_ref[:, h * LANES:(h + 1) * LANES] for h in hh]
        vs = [v_ref[:, h * LANES:(h + 1) * LANES] for h in hh]
        gcs = [cols_ref[h, :, 0:1] for h in hh]
        betas = [cols_ref[h, :, 1:2] for h in hh]
        tms = [t_ref[h] for h in hh]
        dus = [du_ref[h] for h in hh]
        dws = [dw_ref[h] for h in hh]
        gams = [jnp.exp(gc) for gc in gcs]
        kbs = [k * b for k, b in zip(ks, betas)]
        kbgs = [kb * g for kb, g in zip(kbs, gams)]
        dts = [jnp.where(same, _dot_nt(dus[h], vs[h] * betas[h]) + _dot_nt(dws[h], kbgs[h]), 0.0) for h in hh]
        dvbs = [_dot_tn(tms[h], dus[h]) for h in hh]
        dkbgs = [_dot_tn(tms[h], dws[h]) for h in hh]
        kks = [_dot_nt(kbs[h], ks[h]) for h in hh]
        inner = [_dot_nt(dts[h], tms[h]) for h in hh]
        dads = [jnp.where(r > c, -_dot_tn(tms[h], inner[h]), 0.0) * _decay(gcs[h], grow_ref[h, 0], causal) for h in hh]
        dkbs = [dkbgs[h] * gams[h] + _dot(dads[h], ks[h]) for h in hh]
        dk2 = [_dot_tn(dads[h], kbs[h]) for h in hh]
        for h in hh:
            dk_ref[h] = dks_ref[h] + dk2[h] + dkbs[h] * betas[h]
            dv_ref[h] = dvbs[h] * betas[h]
            ea = dads[h] * kks[h]
            dcols_ref[h, :, 0:1] = (dcs_ref[h] + jnp.sum(dkbgs[h] * kbgs[h], axis=-1, keepdims=True)
                                    + jnp.sum(ea, axis=-1, keepdims=True))
            dcols_ref[h, :, 1:2] = (jnp.sum(dvbs[h] * vs[h], axis=-1, keepdims=True)
                                    + jnp.sum(dkbs[h] * ks[h], axis=-1, keepdims=True))
            drow_ref[h, 0] = drs_ref[h, 0] - jnp.sum(ea, axis=0, keepdims=True)

    return _pc(body, name="gdn_prep_bwd", grid=(bh_n // hp, nblk),
               in_specs=[qkv_spec(1), qkv_spec(2), ph(2), rv(), ph(rows), ph(LANES), ph(LANES), ph(LANES), ph(1), rv()],
               out_specs=[ph(LANES), ph(LANES), ph(2), rv()],
               out_shape=[jax.ShapeDtypeStruct((bh_n, s, LANES), f32), jax.ShapeDtypeStruct((bh_n, s, LANES), f32),
                          jax.ShapeDtypeStruct((bh_n, s, 2), f32), jax.ShapeDtypeStruct((bh_n, nblk, 1, rows), f32)],
               sem=("parallel", "parallel"))(qkv, qkv, cols, grow, tmat, du, dw, dk_scan, dcol_scan, drow_scan)


def _row(v):
    return v.reshape(1, -1).astype(f32)


def _pad_lanes(v):
    v = v.reshape(1, -1).astype(f32)
    return jnp.pad(v, ((0, 0), (0, LANES - v.shape[1])))


def _local_step(x, tgt, p, need, emit):
    bl, s, d = x.shape
    t = bl * s
    n_heads = p["gdn_a_log"].shape[-1]
    assert d == n_heads * LANES and s % CHUNK == 0
    x2 = x.reshape(t, d)
    tgt2 = tgt.reshape(t, d)
    gr = {}

    n0, ub, gl = _pw1_glu(x2, _row(p["norm_mix_g"][0]), p["cv_w_pw1"], _row(p["cv_b_pw1"]))
    dc = _dwconv_fwd(gl.reshape(bl, s, d), p["cv_w_dw"], _row(p["cv_b_dw"]), "dwconv_fwd").reshape(t, d)
    sb, h1 = _ln_silu_mm_res(dc, _row(p["cv_ln_g"]), _row(p["cv_ln_b"]), p["cv_w_pw2"], _row(p["cv_b_pw2"]), x2)
    m0 = need("mlp0", h1)
    n1, f0, r0, h2 = _mlp_fwd(h1, _row(p["norm_ffn_g"][0]), m0["w1"], m0["w2"], "mlp_fwd0")

    gd = need("gdn", h2)
    w_in = gd["w_in"]
    w_ab = jnp.pad(w_in[:, 4 * d:], ((0, 0), (0, LANES - 2 * n_heads)))
    a_log_pad = _pad_lanes(p["gdn_a_log"])
    dt_pad = _pad_lanes(p["gdn_dt_bias"])
    n2, qkv_pre, z, ab, gbeta = _gdn_in(h2, _row(p["norm_mix_g"][1]), w_in, w_ab, a_log_pad, dt_pad, n_heads)
    zero_bias = jnp.zeros((1, 3 * d), f32)
    qkv = _dwconv_fwd(qkv_pre.reshape(bl, s, 3 * d), p["gdn_conv_w"], zero_bias, "sconv_fwd", qk_heads=n_heads).reshape(t, 3 * d)
    bh_n, rows = bl * n_heads, _group_rows(s)
    gates = gbeta[:, :2 * n_heads].reshape(bl, s, 2, n_heads).transpose(2, 0, 3, 1).reshape(2, bh_n, s)
    g_lanes = gates[0].reshape(bh_n * (s // CHUNK), CHUNK)
    gc_lanes = _lane_cumsum(g_lanes, jnp.zeros_like(g_lanes), False, "gdn_gate_cumsum")
    grow = gc_lanes.reshape(bh_n, s // rows, 1, rows)
    cols = jnp.stack([gc_lanes.reshape(bh_n, s), gates[1]], axis=-1)
    u, w, tmat = _gdn_prep(qkv, cols, grow, bl, s, n_heads)
    o, vn, ss = _gdn_scan(qkv, u, w, cols, grow, bl, s, n_heads)
    onb, h3 = _gated_norm_mm_res(o, z, _row(p["gdn_norm_g"]), gd["w_out"], h2, n_heads)
    m1 = need("mlp1", h3)
    n3, f1, r1, h4 = _mlp_fwd(h3, _row(p["norm_ffn_g"][1]), m1["w1"], m1["w2"], "mlp_fwd1")
    loss_acc, dh4, dgf = _loss_head(h4, _row(p["final_norm_g"]), tgt2)
    loss = loss_acc[0, 0]
    gr["final_norm_g"] = dgf[0]

    df1, dh3, dg_ffn1, _ = _mlp_bwd(dh4, h3, _row(p["norm_ffn_g"][1]), f1, m1["w1"], m1["w2"], "mlp_bwd1")
    dw2_1 = _mm_tn(r1, dh4, "dw_mlp2_1", bf16)
    dw1_1 = _mm_tn_blocked(n3, df1, "dw_mlp1_1", bf16)

    dw_out = _mm_tn(onb, dh3, "dw_gdn_out", bf16)
    tie = emit("late", {"mlp_w2_1": dw2_1, "mlp_w1_1": dw1_1, "gdn_w_out": dw_out})
    don = _mm_nt([(dh3, gd["w_out"])], "dx_gdn_out")
    do, dz, dng = _gated_norm_bwd(don, o, z, _row(p["gdn_norm_g"]) + tie, n_heads)
    gr["gdn_norm_g"] = dng[0]
    du, dw_, dq, dk_scan, dcol_scan, drow_scan = _gdn_scan_bwd(do, qkv, w, vn, cols, grow, ss, bl, s, n_heads)
    dk, dv, dcols, drow = _gdn_prep_bwd(qkv, cols, grow, tmat, du, dw_, dk_scan, dcol_scan, drow_scan, bl, s, n_heads)
    dg_lanes = _lane_cumsum(dcols[..., 0].reshape(g_lanes.shape), drow.reshape(g_lanes.shape), True, "gdn_gate_cumsum_bwd")
    dgb2 = jnp.stack([dg_lanes.reshape(bh_n, s), dcols[..., 1]], axis=-1)
    dqkv_pre, dconv_w, _ = _dwconv_bwd(qkv_pre.reshape(bl, s, 3 * d), [dq, dk, dv], p["gdn_conv_w"], "sconv_bwd", qk_heads=n_heads)
    gr["gdn_conv_w"] = dconv_w[:p["gdn_conv_w"].shape[0]]
    dgb = dgb2.reshape(bl, n_heads, s, 2).transpose(0, 2, 3, 1).reshape(t, 2 * n_heads)
    dgb = jnp.pad(dgb, ((0, 0), (0, LANES - 2 * n_heads)))
    dab, dal, ddt = _gates_bwd(dgb, ab, a_log_pad, dt_pad, n_heads)
    gr["gdn_a_log"] = dal[0, :n_heads]
    gr["gdn_dt_bias"] = ddt[0, :n_heads]
    dqkv2 = dqkv_pre.reshape(t, 3 * d)
    dw_in = jnp.concatenate(
        [_mm_tn(n2, dqkv2, "dw_gdn_in_qkv"), _mm_tn(n2, dz, "dw_gdn_in_z"), _mm_tn(n2, dab, "dw_gdn_in_ab")[:, :2 * n_heads]], axis=1)
    tie = emit("gdn_in", {"gdn_w_in": dw_in})
    dn2 = _mm_nt([(dqkv2, (w_in, 0)), (dz, (w_in, 3)), (dab, w_ab)], "dx_gdn_in")
    dh2, dg_mix1 = _rms_bwd_res(dn2, h2, _row(p["norm_mix_g"][1]) + tie, dh3, "rms_bwd_gdn")

    df0, dh1, dg_ffn0, cs_h1 = _mlp_bwd(dh2, h1, _row(p["norm_ffn_g"][0]), f0, m0["w1"], m0["w2"], "mlp_bwd0")
    dw2_0 = _mm_tn(r0, dh2, "dw_mlp2_0", bf16)
    dw1_0 = _mm_tn_blocked(n1, df0, "dw_mlp1_0", bf16)
    tie = emit("mlp0", {"mlp_w2_0": dw2_0, "mlp_w1_0": dw1_0})
    gr["norm_ffn_g"] = jnp.stack([dg_ffn0[0], dg_ffn1[0]])

    gr["cv_b_pw2"] = cs_h1[0]
    gr["cv_w_pw2"] = _mm_tn(sb, dh1, "dw_pw2", bf16)
    ds = _mm_nt([(dh1, p["cv_w_pw2"])], "dx_pw2")
    ddc, dlng, dlnb, cs_dc = _ln_silu_bwd(ds, dc, _row(p["cv_ln_g"]) + tie, _row(p["cv_ln_b"]))
    gr["cv_ln_g"] = dlng[0]
    gr["cv_ln_b"] = dlnb[0]
    gr["cv_b_dw"] = cs_dc[0]
    dgl, dw_dw, _ = _dwconv_bwd(gl.reshape(bl, s, d), [ddc.reshape(bl, s, d)], p["cv_w_dw"], "dwconv_bwd")
    gr["cv_w_dw"] = dw_dw[:p["cv_w_dw"].shape[0]]
    dub, cs_u = _glu_bwd(dgl.reshape(t, d), ub)
    gr["cv_b_pw1"] = cs_u[0]
    gr["cv_w_pw1"] = _mm_tn_blocked(n0, dub, "dw_pw1", bf16)
    dn0 = _mm_nt_blocked(dub, p["cv_w_pw1"], "dx_pw1")
    dx, dg_mix0 = _rms_bwd_res(dn0, x2, _row(p["norm_mix_g"][0]), dh1, "rms_bwd_conv")
    gr["norm_mix_g"] = jnp.stack([dg_mix0[0], dg_mix1[0]])
    return loss, dx.reshape(bl, s, d), gr


ANY = pl.BlockSpec(memory_space=pl.ANY)
MESH = pl.DeviceIdType.MESH


def _flip(v, bit):
    return 1 - v if bit else v


def _all_gather_many(shards):
    na = len(shards)

    def body(*refs):
        x_refs, o_refs = refs[:na], refs[na:2 * na]
        send_sems, recv_sems, local_sems = refs[2 * na:]
        x, y, c = lax.axis_index("x"), lax.axis_index("y"), lax.axis_index("c")
        me, sibling = (x, y, c), (x, y, 1 - c)
        chips = [(1 - x, y), (x, 1 - y), (1 - x, 1 - y)]

        def copy(a, k, block, to, src=None):
            px, py, pc = block
            dst = o_refs[a].at[4 * px + 2 * py + pc]
            return pltpu.make_async_remote_copy(
                src_ref=dst if src is None else src, dst_ref=dst,
                send_sem=send_sems.at[a, k], recv_sem=recv_sems.at[a, k], device_id=to, device_id_type=MESH)

        mine = [pltpu.make_async_copy(x_refs[a], o_refs[a].at[4 * x + 2 * y + c], local_sems.at[a]) for a in range(na)]
        first = []
        for a in range(na):
            first.append(copy(a, 0, me, sibling, src=x_refs[a]))
            first += [copy(a, 1 + j, me, (*chip, c), src=x_refs[a]) for j, chip in enumerate(chips)]
        for cp in mine + first:
            cp.start()
        passed = []
        for j, chip in enumerate(chips):
            for a in range(na):
                copy(a, 1 + j, (*chip, c), me).wait_recv()
                fwd = copy(a, 4 + j, (*chip, c), sibling)
                fwd.start()
                passed.append(fwd)
        for a in range(na):
            copy(a, 0, sibling, me).wait_recv()
        for j, chip in enumerate(chips):
            for a in range(na):
                copy(a, 4 + j, (*chip, 1 - c), me).wait_recv()
        for cp in first + passed:
            cp.wait_send()
        for cp in mine:
            cp.wait()

    return pl.pallas_call(
        body, name="weights_all_gather",
        out_shape=[jax.ShapeDtypeStruct((N_DEV,) + a.shape, a.dtype) for a in shards],
        in_specs=[ANY] * na, out_specs=[ANY] * na,
        scratch_shapes=[pltpu.SemaphoreType.DMA((na, 7)), pltpu.SemaphoreType.DMA((na, 7)), pltpu.SemaphoreType.DMA((na,))],
        compiler_params=pltpu.CompilerParams(has_side_effects=True),
    )(*shards)


HBM = pl.BlockSpec(memory_space=pltpu.HBM)
SEM = pl.BlockSpec(memory_space=pltpu.SEMAPHORE)
EFFECT = pltpu.SideEffectType.DATAFLOW_SIDE_EFFECTING
N_PEERS = N_DEV - 1


def _exchange_copies(src_refs, land_refs, send_sems, recv_sems, scatter):
    x, y, c = lax.axis_index("x"), lax.axis_index("y"), lax.axis_index("c")
    me = 4 * x + 2 * y + c
    copies = []
    for a, (src, land) in enumerate(zip(src_refs, land_refs)):
        for k in range(1, N_DEV):
            px, py, pc = _flip(x, k & 4), _flip(y, k & 2), _flip(c, k & 1)
            i = a * N_PEERS + k - 1
            copies.append(pltpu.make_async_remote_copy(
                src_ref=src.at[4 * px + 2 * py + pc] if scatter[a] else src, dst_ref=land.at[me],
                send_sem=send_sems.at[i], recv_sem=recv_sems.at[i], device_id=(px, py, pc), device_id_type=MESH))
    return copies


def _exchange_start(srcs, scatter, name):
    na = len(srcs)
    lands = [lax.empty(s.shape if sc else (N_DEV,) + s.shape, s.dtype) for s, sc in zip(srcs, scatter)]

    def body(*refs):
        copies = _exchange_copies(refs[:na], refs[na:2 * na], refs[2 * na], refs[2 * na + 1], scatter)
        for cp in copies:
            cp.start()
        token = refs[-1]
        token[...] = jnp.zeros_like(token)

    outs = pl.pallas_call(
        body, name=name,
        out_shape=(pltpu.SemaphoreType.DMA((na * N_PEERS,)), pltpu.SemaphoreType.DMA((na * N_PEERS,)))
        + tuple(pltpu.HBM(a.shape, a.dtype) for a in srcs + lands) + (jax.ShapeDtypeStruct((SUBLANES, LANES), f32),),
        in_specs=[HBM] * (2 * na),
        out_specs=(SEM, SEM) + (HBM,) * (2 * na) + (pl.BlockSpec(memory_space=pltpu.VMEM),),
        input_output_aliases={i: 2 + i for i in range(2 * na)},
        compiler_params=pltpu.CompilerParams(has_side_effects=EFFECT),
    )(*[pltpu.with_memory_space_constraint(a, pltpu.HBM) for a in srcs + lands])
    return outs[0], outs[1], list(outs[2:2 + na]), list(outs[2 + na:2 + 2 * na]), outs[-1]


def _exchange_wait(started, after, scatter, name):
    send_sems, recv_sems, srcs, lands, _ = started
    na = len(srcs)

    def body(*refs):
        for cp in _exchange_copies(refs[:na], refs[na:2 * na], refs[2 * na], refs[2 * na + 1], scatter):
            cp.wait_send()
            cp.wait_recv()

    outs = pl.pallas_call(
        body, name=name,
        out_shape=tuple(pltpu.HBM(a.shape, a.dtype) for a in srcs + lands),
        in_specs=[HBM] * (2 * na) + [SEM, SEM, ANY], out_specs=(HBM,) * (2 * na),
        input_output_aliases={i: i for i in range(2 * na)},
        compiler_params=pltpu.CompilerParams(has_side_effects=EFFECT),
    )(*srcs, *lands, send_sems, recv_sems, after)
    return list(outs[na:])


def _own_block(land, block, me):
    return lax.dynamic_update_index_in_dim(land, block, me, 0)


def _sum8_adamw(r2, w, m, v, layer, name):
    _, rr, cc = r2.shape
    tr = _tile(rr, 256)
    bc1 = 1.0 - ADAM_B1 ** ADAM_STEP
    bc2 = 1.0 - ADAM_B2 ** ADAM_STEP

    def body(r_ref, w_ref, m_ref, v_ref, g_ref, d_ref, nm_ref, nv_ref):
        gv = r_ref[0].astype(f32)
        for q in range(1, N_DEV):
            gv = gv + r_ref[q].astype(f32)
        g_ref[...] = gv
        nm = ADAM_B1 * m_ref[...] + (1.0 - ADAM_B1) * gv
        nv = ADAM_B2 * v_ref[...] + (1.0 - ADAM_B2) * (gv * gv)
        nm_ref[...] = nm
        nv_ref[...] = nv
        d_ref[...] = -ADAM_LR * ((nm / bc1) / (jnp.sqrt(nv / bc2) + ADAM_EPS) + ADAM_WD * w_ref[...])

    lspec = pl.BlockSpec((None, tr, cc), lambda i: (layer, i, 0))
    return _pc(body, name=name, grid=(rr // tr,),
               in_specs=[pl.BlockSpec((N_DEV, tr, cc), lambda i: (0, i, 0)), lspec, lspec, lspec],
               out_specs=[_rows(tr, cc)] * 4, out_shape=[jax.ShapeDtypeStruct((rr, cc), f32)] * 4,
               sem=("parallel",))(r2, w, m, v)


def _sum_devices(recv):
    _, rr, _ = recv.shape
    tr = _tile(rr, 512)

    def body(r_ref, o_ref):
        acc = r_ref[0]
        for i in range(1, N_DEV):
            acc = acc + r_ref[i]
        o_ref[...] = acc

    return _pc(body, name="grads_sum", grid=(rr // tr,),
               in_specs=[pl.BlockSpec((N_DEV, tr, LANES), lambda i: (0, i, 0))],
               out_specs=_rows(tr, LANES), out_shape=jax.ShapeDtypeStruct((rr, LANES), f32), sem=("parallel",))(recv)


def _adamw(w, g, m, v):
    rr = w.shape[0]
    tr = _tile(rr, 512)
    bc1 = 1.0 - ADAM_B1 ** ADAM_STEP
    bc2 = 1.0 - ADAM_B2 ** ADAM_STEP

    def body(w_ref, g_ref, m_ref, v_ref, d_ref, nm_ref, nv_ref):
        gv = g_ref[...]
        nm = ADAM_B1 * m_ref[...] + (1.0 - ADAM_B1) * gv
        nv = ADAM_B2 * v_ref[...] + (1.0 - ADAM_B2) * (gv * gv)
        nm_ref[...] = nm
        nv_ref[...] = nv
        d_ref[...] = -ADAM_LR * ((nm / bc1) / (jnp.sqrt(nv / bc2) + ADAM_EPS) + ADAM_WD * w_ref[...])

    spec = _rows(tr, LANES)
    return _pc(body, name="adamw", grid=(rr // tr,), in_specs=[spec] * 4, out_specs=[spec] * 3,
               out_shape=[jax.ShapeDtypeStruct((rr, LANES), f32)] * 3, sem=("parallel",))(w, g, m, v)


PACK_ROWS = 512
PART_ROWS = 16


def _pack(arrs, lead=()):
    nl = len(lead)
    parts, sizes = [], []
    for a in arrs:
        flat = a.reshape(lead + (-1,))
        n = flat.shape[-1]
        rows = -(-n // (LANES * PART_ROWS)) * PART_ROWS
        flat = jnp.pad(flat, [(0, 0)] * nl + [(0, rows * LANES - n)])
        parts.append(flat.reshape(lead + (rows, LANES)))
        sizes.append((rows, n))
    total = sum(r for r, _ in sizes)
    padded = -(-total // PACK_ROWS) * PACK_ROWS
    if padded > total:
        parts.append(jnp.zeros(lead + (padded - total, LANES), parts[0].dtype))
    return jnp.concatenate(parts, axis=nl), sizes


def _unpack(packed, sizes, shapes, lead=()):
    nl = len(lead)
    out, off = [], 0
    for (rows, n), shp in zip(sizes, shapes):
        piece = lax.slice_in_dim(packed, off, off + rows, axis=nl).reshape(lead + (rows * LANES,))
        out.append(lax.slice_in_dim(piece, 0, n, axis=nl).reshape(lead + tuple(shp)))
        off += rows
    return out


def _cols_to_blocks(a):
    n = a.shape[-1] // N_DEV
    a = a.reshape(a.shape[:-1] + (N_DEV, n))
    return jnp.moveaxis(a, -2, 0)


def _blocks_to_cols(a):
    a = jnp.moveaxis(a, 0, -2)
    return a.reshape(a.shape[:-2] + (a.shape[-2] * a.shape[-1],))


def _rows_to_blocks(a):
    k = a.shape[-2] // N_DEV
    a = a.reshape(a.shape[:-2] + (N_DEV, k, a.shape[-1]))
    return jnp.moveaxis(a, -3, 0)


def _blocks_to_rows(a):
    a = jnp.moveaxis(a, 0, -3)
    return a.reshape(a.shape[:-3] + (a.shape[-3] * a.shape[-2], a.shape[-1]))


COL_SHARDED = ("cv_w_pw1", "gdn_w_in", "mlp_w1")
ROW_SHARDED = ("cv_w_pw2", "gdn_w_out", "mlp_w2")
CONV_SHARDED = ("cv_w_dw", "gdn_conv_w")
REPLICATED = ("norm_mix_g", "norm_ffn_g", "final_norm_g", "cv_b_pw1", "cv_b_dw", "cv_ln_g", "cv_ln_b", "cv_b_pw2",
              "gdn_a_log", "gdn_dt_bias", "gdn_norm_g")
WEIGHTS = ("norm_mix_g", "norm_ffn_g", "final_norm_g", "cv_w_pw1", "cv_b_pw1", "cv_w_dw", "cv_b_dw", "cv_ln_g",
           "cv_ln_b", "cv_w_pw2", "cv_b_pw2", "gdn_w_in", "gdn_conv_w", "gdn_a_log", "gdn_dt_bias", "gdn_norm_g",
           "gdn_w_out", "mlp_w1", "mlp_w2")
MATMUL_SHARDED = COL_SHARDED + ROW_SHARDED


def _squeeze_layer(name, a):
    if name in ("norm_mix_g", "norm_ffn_g", "final_norm_g", "mlp_w1", "mlp_w2"):
        return a
    return a[0]


def _gather_weights(shards):
    me = 4 * lax.axis_index("x") + 2 * lax.axis_index("y") + lax.axis_index("c")
    first = ("cv_w_pw1", "cv_w_pw2") + CONV_SHARDED
    got = dict(zip(first, _all_gather_many([shards[n] if n in CONV_SHARDED else shards[n].astype(bf16) for n in first])))
    now = {"cv_w_pw1": got["cv_w_pw1"], "cv_w_pw2": _blocks_to_rows(got["cv_w_pw2"])}
    for n in CONV_SHARDED:
        now[n] = _blocks_to_cols(got[n])

    def cast(a, tie):
        return (a + tie).astype(bf16)

    later, started, token = {}, {}, jnp.zeros((), f32)
    for group in ("mlp0", "gdn", "mlp1"):
        if group == "gdn":
            srcs = [cast(shards["gdn_w_in"], token).reshape(-1, LANES), cast(shards["gdn_w_out"], token)]
        else:
            layer = int(group[-1])
            srcs = [cast(shards["mlp_w1"][layer], token), cast(shards["mlp_w2"][layer], token)]
        later[group] = srcs
        started[group] = _exchange_start(srcs, [False] * len(srcs), f"weights_{group}_start")
        token = started[group][4][0, 0]

    def need(group, after):
        lands = _exchange_wait(started[group], after, [False] * len(later[group]), f"weights_{group}_wait")
        lands = [_own_block(ld, own, me) for ld, own in zip(lands, later[group])]
        if group == "gdn":
            w_in = _blocks_to_cols(lands[0].reshape((N_DEV,) + shards["gdn_w_in"].shape))
            return {"w_in": w_in, "w_out": _blocks_to_rows(lands[1])}
        return {"w1": lands[0], "w2": lands[1]}

    return now, need, token


def kernel(x, norm_mix_g, norm_ffn_g, final_norm_g, cv_w_pw1, cv_b_pw1, cv_w_dw, cv_b_dw, cv_ln_g, cv_ln_b, cv_w_pw2, cv_b_pw2, gdn_w_in, gdn_conv_w, gdn_a_log, gdn_dt_bias, gdn_norm_g, gdn_w_out, mlp_w1, mlp_w2, loss_target, m_norm_mix_g, m_norm_ffn_g, m_final_norm_g, m_cv_w_pw1, m_cv_b_pw1, m_cv_w_dw, m_cv_b_dw, m_cv_ln_g, m_cv_ln_b, m_cv_w_pw2, m_cv_b_pw2, m_gdn_w_in, m_gdn_conv_w, m_gdn_a_log, m_gdn_dt_bias, m_gdn_norm_g, m_gdn_w_out, m_mlp_w1, m_mlp_w2, v_norm_mix_g, v_norm_ffn_g, v_final_norm_g, v_cv_w_pw1, v_cv_b_pw1, v_cv_w_dw, v_cv_b_dw, v_cv_ln_g, v_cv_ln_b, v_cv_w_pw2, v_cv_b_pw2, v_gdn_w_in, v_gdn_conv_w, v_gdn_a_log, v_gdn_dt_bias, v_gdn_norm_g, v_gdn_w_out, v_mlp_w1, v_mlp_w2):
    w_in = dict(zip(WEIGHTS, (norm_mix_g, norm_ffn_g, final_norm_g, cv_w_pw1, cv_b_pw1, cv_w_dw, cv_b_dw, cv_ln_g, cv_ln_b, cv_w_pw2, cv_b_pw2, gdn_w_in, gdn_conv_w, gdn_a_log, gdn_dt_bias, gdn_norm_g, gdn_w_out, mlp_w1, mlp_w2)))
    m_in = dict(zip(WEIGHTS, (m_norm_mix_g, m_norm_ffn_g, m_final_norm_g, m_cv_w_pw1, m_cv_b_pw1, m_cv_w_dw, m_cv_b_dw, m_cv_ln_g, m_cv_ln_b, m_cv_w_pw2, m_cv_b_pw2, m_gdn_w_in, m_gdn_conv_w, m_gdn_a_log, m_gdn_dt_bias, m_gdn_norm_g, m_gdn_w_out, m_mlp_w1, m_mlp_w2)))
    v_in = dict(zip(WEIGHTS, (v_norm_mix_g, v_norm_ffn_g, v_final_norm_g, v_cv_w_pw1, v_cv_b_pw1, v_cv_w_dw, v_cv_b_dw, v_cv_ln_g, v_cv_ln_b, v_cv_w_pw2, v_cv_b_pw2, v_gdn_w_in, v_gdn_conv_w, v_gdn_a_log, v_gdn_dt_bias, v_gdn_norm_g, v_gdn_w_out, v_mlp_w1, v_mlp_w2)))
    me = 4 * lax.axis_index("x") + 2 * lax.axis_index("y") + lax.axis_index("c")

    shards = {n: _squeeze_layer(n, w_in[n]) for n in WEIGHTS}
    first, need, token = _gather_weights(shards)
    params = {n: shards[n] for n in REPLICATED}
    params.update(first)
    params["norm_mix_g"] = params["norm_mix_g"] + token

    def row_blocks(a):
        return a.reshape(N_DEV, a.shape[0] // N_DEV, a.shape[1])

    def flat_blocks(a):
        k, n8 = a.shape
        return _cols_to_blocks(a).reshape(N_DEV, k * (n8 // N_DEV) // LANES, LANES)

    def as_blocks(n, a):
        if n == "gdn_w_in":
            return flat_blocks(a).astype(bf16)
        return a if a.ndim == 3 else row_blocks(a)

    sent = []

    def emit(group, grads_out):
        names = list(grads_out)
        blocks = [as_blocks(n, grads_out[n]) for n in names]
        st = _exchange_start(blocks, [True] * len(blocks), f"grads_{group}_start")
        sent.append((names, blocks, st, [True] * len(blocks)))
        return st[4][0, 0]

    loss_part, grad_x, gr = _local_step(x, loss_target, params, need, emit)
    loss = lax.psum(loss_part, ("x", "y", "c"))

    small = REPLICATED + CONV_SHARDED
    small_send, ssz = _pack([gr[n] for n in small])
    last_blocks = [gr["cv_w_pw1"], row_blocks(gr["cv_w_pw2"]), small_send]
    last_scatter = [True, True, False]
    sent.append((["cv_w_pw1", "cv_w_pw2", "small"], last_blocks, _exchange_start(last_blocks, last_scatter, "grads_last_start"), last_scatter))

    recv = {}
    after = grad_x
    for names, blocks, st, scatter in sent:
        lands = _exchange_wait(st, after, scatter, f"grads_{names[0]}_wait")
        for n, ld, blk, sc in zip(names, lands, blocks, scatter):
            own = lax.dynamic_index_in_dim(blk, me, 0, keepdims=False) if sc else blk
            recv[n] = _own_block(ld, own, me)
        after = lands[0]

    small_sum = _sum_devices(recv["small"])
    grads = dict(zip(small, _unpack(small_sum, ssz, [gr[n].shape for n in small])))
    for n in CONV_SHARDED:
        cn = shards[n].shape[-1]
        grads[n] = lax.dynamic_slice_in_dim(grads[n], me * cn, cn, axis=1)

    def as3d(n, a):
        if n == "gdn_w_in":
            return a.reshape(a.shape[0], -1, LANES)
        return a

    big = [("cv_w_pw1", 0, "cv_w_pw1"), ("cv_w_pw2", 0, "cv_w_pw2"), ("gdn_w_in", 0, "gdn_w_in"), ("gdn_w_out", 0, "gdn_w_out"),
           ("mlp_w1", 0, "mlp_w1_0"), ("mlp_w1", 1, "mlp_w1_1"), ("mlp_w2", 0, "mlp_w2_0"), ("mlp_w2", 1, "mlp_w2_1")]
    res = {n: {} for n in MATMUL_SHARDED}
    for n, layer, key in big:
        res[n][layer] = _sum8_adamw(recv[key], as3d(n, w_in[n]), as3d(n, m_in[n]), as3d(n, v_in[n]), layer, f"adamw_{key}")
    out_groups = {n: [] for n in WEIGHTS}
    for n in MATMUL_SHARDED:
        layers = sorted(res[n])
        for k in range(4):
            pieces = [res[n][layer][k] for layer in layers]
            out_groups[n].append(jnp.stack(pieces).reshape(w_in[n].shape))

    sm_w = [shards[n] for n in small]
    sm_g = [grads[n].reshape(shards[n].shape) for n in small]
    sm_m = [_squeeze_layer(n, m_in[n]) for n in small]
    sm_v = [_squeeze_layer(n, v_in[n]) for n in small]
    wp, psz = _pack(sm_w)
    gp, _ = _pack(sm_g)
    mp, _ = _pack(sm_m)
    vp, _ = _pack(sm_v)
    dp, nmp, nvp = _adamw(wp, gp, mp, vp)
    shp = [a.shape for a in sm_w]
    for n, g, dl, nm, nv in zip(small, sm_g, _unpack(dp, psz, shp), _unpack(nmp, psz, shp), _unpack(nvp, psz, shp)):
        out_groups[n] = [a.reshape(w_in[n].shape) for a in (g, dl, nm, nv)]

    outs = [loss, grad_x]
    for k in range(4):
        outs += [out_groups[n][k] for n in WEIGHTS]
    return tuple(outs)
```

```python
import functools

import jax
import jax.numpy as jnp
from jax import lax
from jax.experimental import pallas as pl
from jax.experimental.pallas import tpu as pltpu

f32, bf16 = jnp.float32, jnp.bfloat16

NORM_EPS = 1e-6
L2_EPS = 1e-6
CHUNK = 64
LANES = 128
SUBLANES = 8
N_DEV = 8
VMEM_LIMIT = 56 * 1024 * 1024
CONV_PAD = 32
HEADS_PER_STEP = 4
PREP_HEADS_PER_STEP = 2
NEG = -1e30

ADAM_LR, ADAM_B1, ADAM_B2, ADAM_EPS, ADAM_WD, ADAM_STEP = 0.001, 0.9, 0.999, 1e-08, 0.01, 10

NT = (((1,), (1,)), ((), ()))
TN = (((0,), (0,)), ((), ()))
HI = lax.Precision.HIGHEST


def _pc(body, *, name, grid, in_specs, out_specs, out_shape, scratch=(), sem=None):
    return pl.pallas_call(
        body, name=name, grid=grid, in_specs=in_specs, out_specs=out_specs, out_shape=out_shape,
        scratch_shapes=list(scratch),
        compiler_params=pltpu.CompilerParams(dimension_semantics=sem, vmem_limit_bytes=VMEM_LIMIT))


def _rows(tm, n):
    return pl.BlockSpec((tm, n), lambda i: (i, 0))


def _const(shape):
    return pl.BlockSpec(shape, lambda *_: (0,) * len(shape))


def _resident(shape):
    return pl.BlockSpec(shape, lambda *_: (0,) * len(shape), pipeline_mode=pl.Buffered(1))


def _tile(t, pref):
    return pref if t % pref == 0 else t


def _dot(a, b):
    return jnp.dot(a.astype(bf16), b.astype(bf16), preferred_element_type=f32)


def _dot_nt(a, b):
    return lax.dot_general(a.astype(bf16), b.astype(bf16), NT, preferred_element_type=f32)


def _dot_tn(a, b):
    return lax.dot_general(a.astype(bf16), b.astype(bf16), TN, preferred_element_type=f32)


def _sigmoid(x):
    return 1.0 / (1.0 + jnp.exp(-x))


def _silu_grad(x):
    s = _sigmoid(x)
    return s * (1.0 + x * (1.0 - s))


def _rms(x, g):
    rstd = lax.rsqrt(jnp.mean(x * x, axis=-1, keepdims=True) + NORM_EPS)
    xh = x * rstd
    return xh * g, xh, rstd


def _rms_bwd(dn, xh, rstd, g):
    dxh = dn * g
    return rstd * (dxh - xh * jnp.mean(dxh * xh, axis=-1, keepdims=True))


def _acc_init(step, *refs):
    @pl.when(step == 0)
    def _():
        for r in refs:
            r[...] = jnp.zeros(r.shape, r.dtype)


def _acc_rows(ref, val):
    ref[0:1, :] += jnp.sum(val, axis=0, keepdims=True)


def _pw1_glu(x, g, w, b):
    t, d = x.shape
    tm = _tile(t, 256)

    nb_w = w.shape[0]

    def body(x_ref, g_ref, w_hbm, b_ref, n_ref, u_ref, gl_ref, w_ref, sems):
        _fetch_blocks(pl.program_id(0), w_hbm, w_ref, sems, True)
        n, _, _ = _rms(x_ref[...], g_ref[...])
        nb = n.astype(bf16)
        n_ref[...] = nb
        u = jnp.dot(nb, w_ref[...], preferred_element_type=f32) + b_ref[...]
        u_ref[...] = u.astype(bf16)
        gl_ref[...] = u[:, :d] * _sigmoid(u[:, d:])

    return _pc(body, name="pw1_glu", grid=(t // tm,),
               in_specs=[_rows(tm, d), _const((1, d)), ANY, _const((1, 2 * d))],
               out_specs=[_rows(tm, d), _rows(tm, 2 * d), _rows(tm, d)],
               out_shape=[jax.ShapeDtypeStruct((t, d), bf16), jax.ShapeDtypeStruct((t, 2 * d), bf16),
                          jax.ShapeDtypeStruct((t, d), f32)],
               scratch=[pltpu.VMEM((d, 2 * d), bf16), pltpu.SemaphoreType.DMA((nb_w,))],
               sem=("arbitrary",))(x, g, w, b)


def _ln_silu_mm_res(dc, ln_g, ln_b, w, b, res):
    t, d = dc.shape
    tm = _tile(t, 256)

    def body(x_ref, g_ref, bb_ref, w_ref, b_ref, r_ref, s_ref, o_ref):
        x = x_ref[...]
        xc = x - jnp.mean(x, axis=-1, keepdims=True)
        rstd = lax.rsqrt(jnp.mean(xc * xc, axis=-1, keepdims=True) + NORM_EPS)
        ln = xc * rstd * g_ref[...] + bb_ref[...]
        sb = (ln * _sigmoid(ln)).astype(bf16)
        s_ref[...] = sb
        o_ref[...] = r_ref[...] + jnp.dot(sb, w_ref[...], preferred_element_type=f32) + b_ref[...]

    return _pc(body, name="ln_silu_pw2", grid=(t // tm,),
               in_specs=[_rows(tm, d), _const((1, d)), _const((1, d)), _resident((d, d)), _const((1, d)), _rows(tm, d)],
               out_specs=[_rows(tm, d), _rows(tm, d)],
               out_shape=[jax.ShapeDtypeStruct((t, d), bf16), jax.ShapeDtypeStruct((t, d), f32)],
               sem=("parallel",))(dc, ln_g, ln_b, w, b, res)


def _fetch_blocks(step, w_hbm, dst, sems, by_cols, layer=None):
    nb_w = w_hbm.shape[0]
    step_rows, step_cols = w_hbm.shape[-2], w_hbm.shape[-1]

    @pl.when(step == 0)
    def _():
        copies = []
        for j in range(nb_w):
            src = w_hbm.at[j] if layer is None else w_hbm.at[j, layer]
            if by_cols:
                part = dst.at[:, pl.ds(j * step_cols, step_cols)]
            else:
                part = dst.at[pl.ds(j * step_rows, step_rows), :]
            copies.append(pltpu.make_async_copy(src, part, sems.at[j]))
        for cp in copies:
            cp.start()
        for cp in copies:
            cp.wait()


def _mlp_fwd(h, g, w1g, w2g, name):
    t, d = h.shape
    nb_w, _, bn = w1g.shape
    ff = nb_w * bn
    tm = _tile(t, 256)

    def body(h_ref, g_ref, w1_hbm, w2_hbm, n_ref, f_ref, r_ref, o_ref, w1_ref, w2_ref, sem1, sem2):
        _fetch_blocks(pl.program_id(0), w1_hbm, w1_ref, sem1, True)
        _fetch_blocks(pl.program_id(0), w2_hbm, w2_ref, sem2, False)
        hv = h_ref[...]
        n, _, _ = _rms(hv, g_ref[...])
        nb = n.astype(bf16)
        n_ref[...] = nb
        f = jnp.dot(nb, w1_ref[...], preferred_element_type=f32)
        f_ref[...] = f.astype(bf16)
        rb = jnp.square(jnp.maximum(f, 0.0)).astype(bf16)
        r_ref[...] = rb
        o_ref[...] = hv + jnp.dot(rb, w2_ref[...], preferred_element_type=f32)

    return _pc(body, name=name, grid=(t // tm,),
               in_specs=[_rows(tm, d), _const((1, d)), ANY, ANY],
               out_specs=[_rows(tm, d), _rows(tm, ff), _rows(tm, ff), _rows(tm, d)],
               out_shape=[jax.ShapeDtypeStruct((t, d), bf16), jax.ShapeDtypeStruct((t, ff), bf16),
                          jax.ShapeDtypeStruct((t, ff), bf16), jax.ShapeDtypeStruct((t, d), f32)],
               scratch=[pltpu.VMEM((d, ff), bf16), pltpu.VMEM((ff, d), bf16),
                        pltpu.SemaphoreType.DMA((nb_w,)), pltpu.SemaphoreType.DMA((nb_w,))],
               sem=("arbitrary",))(h, g, w1g, w2g)


def _softplus(x):
    return jnp.maximum(x, 0.0) + jnp.log(1.0 + jnp.exp(-jnp.abs(x)))


def _gdn_in(h, g, w_main, w_ab, a_log_pad, dt_pad, n_heads):
    t, d = h.shape
    tm = _tile(t, 256)

    def body(h_ref, g_ref, wm_ref, wab_ref, al_ref, dt_ref, n_ref, qkv_ref, z_ref, ab_ref, gb_ref):
        n, _, _ = _rms(h_ref[...], g_ref[...])
        nb = n.astype(bf16)
        n_ref[...] = nb
        p = jnp.dot(nb, wm_ref[...], preferred_element_type=f32)
        qkv_ref[...] = p[:, :3 * d]
        z_ref[...] = p[:, 3 * d:]
        ab = jnp.dot(nb, wab_ref[...], preferred_element_type=f32)
        ab_ref[...] = ab
        lane = lax.broadcasted_iota(jnp.int32, ab.shape, 1)
        decay = -jnp.exp(al_ref[...]) * _softplus(ab + dt_ref[...])
        gb_ref[...] = jnp.where(lane < n_heads, decay, jnp.where(lane < 2 * n_heads, _sigmoid(ab), 0.0))

    return _pc(body, name="gdn_in", grid=(t // tm,),
               in_specs=[_rows(tm, d), _const((1, d)), _resident((d, 4 * d)), _resident((d, LANES)),
                         _const((1, LANES)), _const((1, LANES))],
               out_specs=[_rows(tm, d), _rows(tm, 3 * d), _rows(tm, d), _rows(tm, LANES), _rows(tm, LANES)],
               out_shape=[jax.ShapeDtypeStruct((t, d), bf16), jax.ShapeDtypeStruct((t, 3 * d), f32),
                          jax.ShapeDtypeStruct((t, d), f32), jax.ShapeDtypeStruct((t, LANES), f32),
                          jax.ShapeDtypeStruct((t, LANES), f32)],
               sem=("parallel",))(h, g, w_main, w_ab, a_log_pad, dt_pad)


def _gated_norm_mm_res(o, z, ng, w, res, n_heads):
    t, d = o.shape
    tm = _tile(t, 256)

    def body(o_ref, z_ref, ng_ref, w_ref, r_ref, on_ref, out_ref):
        for hd in range(n_heads):
            sl = slice(hd * LANES, (hd + 1) * LANES)
            rn, _, _ = _rms(o_ref[:, sl], ng_ref[...])
            zz = z_ref[:, sl]
            on_ref[:, sl] = (rn * (zz * _sigmoid(zz))).astype(bf16)
        out_ref[...] = r_ref[...] + jnp.dot(on_ref[...], w_ref[...], preferred_element_type=f32)

    return _pc(body, name="gated_norm_wout", grid=(t // tm,),
               in_specs=[_rows(tm, d), _rows(tm, d), _const((1, LANES)), _resident((d, d)), _rows(tm, d)],
               out_specs=[_rows(tm, d), _rows(tm, d)],
               out_shape=[jax.ShapeDtypeStruct((t, d), bf16), jax.ShapeDtypeStruct((t, d), f32)],
               sem=("parallel",))(o, z, ng, w, res)


def _loss_head(h, g, tgt):
    t, d = h.shape
    tm = _tile(t, 256)

    def body(h_ref, g_ref, t_ref, loss_ref, dh_ref, dg_ref):
        _acc_init(pl.program_id(0), loss_ref, dg_ref)
        gv = g_ref[...]
        y, xh, rstd = _rms(h_ref[...], gv)
        e = y - t_ref[...]
        loss_ref[...] += 0.5 * jnp.sum(jnp.mean(e * e, axis=-1, keepdims=True))
        dy = e * (1.0 / d)
        _acc_rows(dg_ref, dy * xh)
        dh_ref[...] = _rms_bwd(dy, xh, rstd, gv)

    return _pc(body, name="loss_head", grid=(t // tm,),
               in_specs=[_rows(tm, d), _const((1, d)), _rows(tm, d)],
               out_specs=[_const((SUBLANES, LANES)), _rows(tm, d), _const((SUBLANES, d))],
               out_shape=[jax.ShapeDtypeStruct((SUBLANES, LANES), f32), jax.ShapeDtypeStruct((t, d), f32),
                          jax.ShapeDtypeStruct((SUBLANES, d), f32)],
               sem=("arbitrary",))(h, g, tgt)


def _rms_bwd_res(dn, h, g, dres, name):
    t, d = h.shape
    tm = _tile(t, 256)

    def body(dn_ref, h_ref, g_ref, dr_ref, dh_ref, dg_ref):
        _acc_init(pl.program_id(0), dg_ref)
        gv = g_ref[...]
        _, xh, rstd = _rms(h_ref[...], gv)
        dn = dn_ref[...]
        _acc_rows(dg_ref, dn * xh)
        dh_ref[...] = dr_ref[...] + _rms_bwd(dn, xh, rstd, gv)

    return _pc(body, name=name, grid=(t // tm,),
               in_specs=[_rows(tm, d), _rows(tm, d), _const((1, d)), _rows(tm, d)],
               out_specs=[_rows(tm, d), _const((SUBLANES, d))],
               out_shape=[jax.ShapeDtypeStruct((t, d), f32), jax.ShapeDtypeStruct((SUBLANES, d), f32)],
               sem=("arbitrary",))(dn, h, g, dres)


def _mlp_bwd(dho, h, g, fb, w1g, w2g, name):
    t, d = h.shape
    nb_w, _, bn = w1g.shape
    ff = nb_w * bn
    tm = _tile(t, 256)

    def body(do_ref, h_ref, g_ref, f_ref, w1_hbm, w2_hbm, df_ref, dh_ref, dg_ref, cs_ref, w1_ref, w2_ref, sem1, sem2):
        _fetch_blocks(pl.program_id(0), w1_hbm, w1_ref, sem1, True)
        _fetch_blocks(pl.program_id(0), w2_hbm, w2_ref, sem2, False)
        _acc_init(pl.program_id(0), dg_ref, cs_ref)
        do = do_ref[...]
        dr = lax.dot_general(do.astype(bf16), w2_ref[...], NT, preferred_element_type=f32)
        dfb = (dr * (2.0 * jnp.maximum(f_ref[...].astype(f32), 0.0))).astype(bf16)
        df_ref[...] = dfb
        dn = lax.dot_general(dfb, w1_ref[...], NT, preferred_element_type=f32)
        gv = g_ref[...]
        _, xh, rstd = _rms(h_ref[...], gv)
        _acc_rows(dg_ref, dn * xh)
        dh = do + _rms_bwd(dn, xh, rstd, gv)
        dh_ref[...] = dh
        _acc_rows(cs_ref, dh)

    return _pc(body, name=name, grid=(t // tm,),
               in_specs=[_rows(tm, d), _rows(tm, d), _const((1, d)), _rows(tm, ff), ANY, ANY],
               out_specs=[_rows(tm, ff), _rows(tm, d), _const((SUBLANES, d)), _const((SUBLANES, d))],
               out_shape=[jax.ShapeDtypeStruct((t, ff), bf16), jax.ShapeDtypeStruct((t, d), f32),
                          jax.ShapeDtypeStruct((SUBLANES, d), f32), jax.ShapeDtypeStruct((SUBLANES, d), f32)],
               scratch=[pltpu.VMEM((d, ff), bf16), pltpu.VMEM((ff, d), bf16),
                        pltpu.SemaphoreType.DMA((nb_w,)), pltpu.SemaphoreType.DMA((nb_w,))],
               sem=("arbitrary",))(dho, h, g, fb, w1g, w2g)


def _mm_nt(pairs, name):
    t = pairs[0][0].shape[0]
    w0 = pairs[0][1]
    k = (w0[0] if isinstance(w0, tuple) else w0).shape[0]
    tm = _tile(t, 256)
    npair = len(pairs)

    def body(*refs):
        o_ref = refs[2 * npair]
        acc = None
        for p in range(npair):
            part = lax.dot_general(refs[2 * p][...].astype(bf16), refs[2 * p + 1][...], NT, preferred_element_type=f32)
            acc = part if acc is None else acc + part
        o_ref[...] = acc

    in_specs, args = [], []
    for dy, w in pairs:
        nn = dy.shape[1]
        if isinstance(w, tuple):
            w, idx = w
            wspec = pl.BlockSpec((k, nn), lambda *_, idx=idx: (0, idx), pipeline_mode=pl.Buffered(1))
        else:
            wspec = _resident(w.shape)
        in_specs += [_rows(tm, nn), wspec]
        args += [dy, w]
    return _pc(body, name=name, grid=(t // tm,), in_specs=in_specs, out_specs=_rows(tm, k),
               out_shape=jax.ShapeDtypeStruct((t, k), f32), sem=("parallel",))(*args)


def _mm_nt_blocked(dy, wb, name):
    t = dy.shape[0]
    nb_w, k, bn = wb.shape
    tm = _tile(t, 256)

    def body(dy_ref, w_hbm, o_ref, w_ref, sems):
        _fetch_blocks(pl.program_id(0), w_hbm, w_ref, sems, True)
        o_ref[...] = lax.dot_general(dy_ref[...].astype(bf16), w_ref[...], NT, preferred_element_type=f32)

    return _pc(body, name=name, grid=(t // tm,), in_specs=[_rows(tm, nb_w * bn), ANY],
               out_specs=_rows(tm, k), out_shape=jax.ShapeDtypeStruct((t, k), f32),
               scratch=[pltpu.VMEM((k, nb_w * bn), bf16), pltpu.SemaphoreType.DMA((nb_w,))],
               sem=("arbitrary",))(dy, wb)


def _mm_tn_blocked(x, dy, name, out_dtype=f32):
    t, k = x.shape
    bn = dy.shape[1] // N_DEV
    tm = _tile(t, 512)
    jb = N_DEV
    while jb > 1 and k * jb * bn * 4 > 8 * 1024 * 1024:
        jb //= 2
    nt = t // tm

    def body(x_ref, dy_ref, o_ref, *acc):
        acc_ref = acc[0] if acc else o_ref
        _acc_init(pl.program_id(1), acc_ref)
        xt = x_ref[...].astype(bf16).T
        for jj in range(jb):
            acc_ref[jj] += jnp.dot(xt, dy_ref[:, jj * bn:(jj + 1) * bn].astype(bf16), preferred_element_type=f32)
        if acc:
            @pl.when(pl.program_id(1) == nt - 1)
            def _():
                o_ref[...] = acc_ref[...].astype(out_dtype)

    return _pc(body, name=name, grid=(N_DEV // jb, nt),
               in_specs=[pl.BlockSpec((tm, k), lambda j, i: (i, 0)), pl.BlockSpec((tm, jb * bn), lambda j, i: (i, j))],
               out_specs=pl.BlockSpec((jb, k, bn), lambda j, i: (j, 0, 0)),
               out_shape=jax.ShapeDtypeStruct((N_DEV, k, bn), out_dtype),
               scratch=[] if out_dtype == f32 else [pltpu.VMEM((jb, k, bn), f32)],
               sem=("parallel", "arbitrary"))(x, dy)


def _mm_tn(x, dy, name, out_dtype=f32):
    t, k = x.shape
    n = dy.shape[1]
    tm = _tile(t, 512)
    cap = max(LANES, (2 * 1024 * 1024) // k)
    tn = n
    if n > cap:
        tn = max(c for c in range(LANES, cap + 1, LANES) if n % c == 0)
    nt = t // tm

    def body(x_ref, dy_ref, o_ref, *acc):
        acc_ref = acc[0] if acc else o_ref
        _acc_init(pl.program_id(1), acc_ref)
        acc_ref[...] += lax.dot_general(x_ref[...].astype(bf16), dy_ref[...].astype(bf16), TN, preferred_element_type=f32)
        if acc:
            @pl.when(pl.program_id(1) == nt - 1)
            def _():
                o_ref[...] = acc_ref[...].astype(out_dtype)

    return _pc(body, name=name, grid=(n // tn, nt),
               in_specs=[pl.BlockSpec((tm, k), lambda j, i: (i, 0)), pl.BlockSpec((tm, tn), lambda j, i: (i, j))],
               out_specs=pl.BlockSpec((k, tn), lambda j, i: (0, j)),
               out_shape=jax.ShapeDtypeStruct((k, n), out_dtype),
               scratch=[] if out_dtype == f32 else [pltpu.VMEM((k, tn), f32)],
               sem=("parallel", "arbitrary"))(x, dy)


def _gated_norm_bwd(don, o, z, ng, n_heads):
    t, d = o.shape
    tm = _tile(t, 256)

    def body(don_ref, o_ref, z_ref, ng_ref, do_ref, dz_ref, dng_ref):
        _acc_init(pl.program_id(0), dng_ref)
        gv = ng_ref[...]
        for hd in range(n_heads):
            sl = slice(hd * LANES, (hd + 1) * LANES)
            rn, xh, rstd = _rms(o_ref[:, sl], gv)
            zz = z_ref[:, sl]
            don = don_ref[:, sl]
            dz_ref[:, sl] = don * rn * _silu_grad(zz)
            drn = don * (zz * _sigmoid(zz))
            _acc_rows(dng_ref, drn * xh)
            do_ref[:, sl] = _rms_bwd(drn, xh, rstd, gv)

    return _pc(body, name="gated_norm_bwd", grid=(t // tm,),
               in_specs=[_rows(tm, d), _rows(tm, d), _rows(tm, d), _const((1, LANES))],
               out_specs=[_rows(tm, d), _rows(tm, d), _const((SUBLANES, LANES))],
               out_shape=[jax.ShapeDtypeStruct((t, d), f32), jax.ShapeDtypeStruct((t, d), f32),
                          jax.ShapeDtypeStruct((SUBLANES, LANES), f32)],
               sem=("arbitrary",))(don, o, z, ng)


def _gates_bwd(dgb, ab, a_log_pad, dt_pad, n_heads):
    t = ab.shape[0]
    tm = _tile(t, 256)

    def body(dgb_ref, ab_ref, al_ref, dt_ref, dab_ref, dal_ref, ddt_ref):
        _acc_init(pl.program_id(0), dal_ref, ddt_ref)
        ab = ab_ref[...]
        dgb = dgb_ref[...]
        lane = lax.broadcasted_iota(jnp.int32, ab.shape, 1)
        is_a = lane < n_heads
        is_b = jnp.logical_and(lane >= n_heads, lane < 2 * n_heads)
        xa = ab + dt_ref[...]
        neg_a = -jnp.exp(al_ref[...])
        dg_da = neg_a * _sigmoid(xa)
        beta = _sigmoid(ab)
        da = jnp.where(is_a, dgb * dg_da, 0.0)
        dab_ref[...] = da + jnp.where(is_b, dgb * beta * (1.0 - beta), 0.0)
        _acc_rows(dal_ref, jnp.where(is_a, dgb * neg_a * _softplus(xa), 0.0))
        _acc_rows(ddt_ref, da)

    return _pc(body, name="gates_bwd", grid=(t // tm,),
               in_specs=[_rows(tm, LANES), _rows(tm, LANES), _const((1, LANES)), _const((1, LANES))],
               out_specs=[_rows(tm, LANES), _const((SUBLANES, LANES)), _const((SUBLANES, LANES))],
               out_shape=[jax.ShapeDtypeStruct((t, LANES), f32), jax.ShapeDtypeStruct((SUBLANES, LANES), f32),
                          jax.ShapeDtypeStruct((SUBLANES, LANES), f32)],
               sem=("arbitrary",))(dgb, ab, a_log_pad, dt_pad)


def _ln_silu_bwd(ds, dc, ln_g, ln_b):
    t, d = dc.shape
    tm = _tile(t, 256)

    def body(ds_ref, x_ref, g_ref, b_ref, dx_ref, dg_ref, db_ref, cs_ref):
        _acc_init(pl.program_id(0), dg_ref, db_ref, cs_ref)
        x = x_ref[...]
        gv = g_ref[...]
        xc = x - jnp.mean(x, axis=-1, keepdims=True)
        rstd = lax.rsqrt(jnp.mean(xc * xc, axis=-1, keepdims=True) + NORM_EPS)
        xh = xc * rstd
        ln = xh * gv + b_ref[...]
        dln = ds_ref[...] * _silu_grad(ln)
        _acc_rows(dg_ref, dln * xh)
        _acc_rows(db_ref, dln)
        dxh = dln * gv
        dx = rstd * (dxh - jnp.mean(dxh, axis=-1, keepdims=True) - xh * jnp.mean(dxh * xh, axis=-1, keepdims=True))
        dx_ref[...] = dx
        _acc_rows(cs_ref, dx)

    return _pc(body, name="ln_silu_bwd", grid=(t // tm,),
               in_specs=[_rows(tm, d), _rows(tm, d), _const((1, d)), _const((1, d))],
               out_specs=[_rows(tm, d), _const((SUBLANES, d)), _const((SUBLANES, d)), _const((SUBLANES, d))],
               out_shape=[jax.ShapeDtypeStruct((t, d), f32)] + [jax.ShapeDtypeStruct((SUBLANES, d), f32)] * 3,
               sem=("arbitrary",))(ds, dc, ln_g, ln_b)


def _glu_bwd(dgl, ub):
    t, d = dgl.shape
    tm = _tile(t, 256)

    def body(dgl_ref, u_ref, du_ref, cs_ref):
        _acc_init(pl.program_id(0), cs_ref)
        dgl = dgl_ref[...]
        a = u_ref[:, :d].astype(f32)
        sb = _sigmoid(u_ref[:, d:].astype(f32))
        da = dgl * sb
        db = dgl * a * sb * (1.0 - sb)
        du_ref[:, :d] = da.astype(bf16)
        du_ref[:, d:] = db.astype(bf16)
        cs_ref[0:1, :d] += jnp.sum(da, axis=0, keepdims=True)
        cs_ref[0:1, d:] += jnp.sum(db, axis=0, keepdims=True)

    return _pc(body, name="glu_bwd", grid=(t // tm,),
               in_specs=[_rows(tm, d), _rows(tm, 2 * d)],
               out_specs=[_rows(tm, 2 * d), _const((SUBLANES, 2 * d))],
               out_shape=[jax.ShapeDtypeStruct((t, 2 * d), bf16), jax.ShapeDtypeStruct((SUBLANES, 2 * d), f32)],
               sem=("arbitrary",))(dgl, ub)


def _conv_rows(s):
    return 256 if s % 256 == 0 else s


def _conv_tap_sum(pad_ref, w_ref, base, rows, width):
    acc = jnp.zeros((rows, LANES), f32)
    for j in range(width):
        acc = acc + w_ref[j:j + 1, :] * pad_ref[pl.ds(base + CONV_PAD - (width - 1) + j, rows), :]
    return acc


def _l2_silu_post(c, j, n_heads, scale):
    a = c * _sigmoid(c)
    r = lax.rsqrt(jnp.sum(a * a, axis=-1, keepdims=True) + L2_EPS)
    mult = jnp.where(j < n_heads, r * scale, jnp.where(j < 2 * n_heads, r, 1.0))
    return a, r, a * mult


def _dwconv_fwd(x, w, b, name, qk_heads=None):
    bl, s, cn = x.shape
    width = w.shape[0]
    rows = _conv_rows(s)
    scale = float(LANES) ** -0.5

    def body(x_ref, w_ref, b_ref, o_ref, pad_ref):
        j = pl.program_id(1)
        pad_ref[0:CONV_PAD, :] = jnp.zeros((CONV_PAD, LANES), f32)
        pad_ref[CONV_PAD:, :] = x_ref[0]

        def step(i, carry):
            base = pl.multiple_of(i * rows, rows)
            acc = _conv_tap_sum(pad_ref, w_ref, base, rows, width)
            if qk_heads is None:
                acc = acc + b_ref[...]
            else:
                _, _, acc = _l2_silu_post(acc, j, qk_heads, scale)
            o_ref[0, pl.ds(base, rows), :] = acc
            return carry

        lax.fori_loop(0, s // rows, step, 0)

    return _pc(body, name=name, grid=(bl, cn // LANES),
               in_specs=[pl.BlockSpec((1, s, LANES), lambda bi, j: (bi, 0, j)),
                         pl.BlockSpec((width, LANES), lambda bi, j: (0, j)),
                         pl.BlockSpec((1, LANES), lambda bi, j: (0, j))],
               out_specs=pl.BlockSpec((1, s, LANES), lambda bi, j: (bi, 0, j)),
               out_shape=jax.ShapeDtypeStruct((bl, s, cn), f32),
               scratch=[pltpu.VMEM((s + CONV_PAD, LANES), f32)],
               sem=("parallel", "parallel"))(x, w, b)


def _dwconv_bwd(x, dys, w, name, qk_heads=None):
    bl, s, cn = x.shape
    width = w.shape[0]
    wp = -(-width // SUBLANES) * SUBLANES
    rows = _conv_rows(s)
    scale = float(LANES) ** -0.5
    nblk = s // rows
    ndy = len(dys)

    def body(*refs):
        x_ref, w_ref = refs[0], refs[1]
        dy_refs = refs[2:2 + ndy]
        dx_ref, dw_ref, db_ref, xpad, dypad, acc = refs[2 + ndy:]
        j = pl.program_id(0)
        bi = pl.program_id(1)
        _acc_init(bi, acc, db_ref)
        xpad[0:CONV_PAD, :] = jnp.zeros((CONV_PAD, LANES), f32)
        xpad[CONV_PAD:, :] = x_ref[0]
        dypad[s:, :] = jnp.zeros((CONV_PAD, LANES), f32)
        if qk_heads is None:
            dypad[0:s, :] = dy_refs[0][0]
        else:
            def pre(i, carry):
                base = pl.multiple_of(i * rows, rows)
                c = _conv_tap_sum(xpad, w_ref, base, rows, width)
                a, r, _ = _l2_silu_post(c, j, qk_heads, scale)
                dq = dy_refs[0][0, pl.ds(base, rows), :]
                dk = dy_refs[1][0, pl.ds(base, rows), :]
                dv = dy_refs[2][0, pl.ds(base, rows), :]
                dy = jnp.where(j < qk_heads, dq * scale, jnp.where(j < 2 * qk_heads, dk, dv))
                da_l2 = r * (dy - a * (r * r) * jnp.sum(a * dy, axis=-1, keepdims=True))
                da = jnp.where(j < 2 * qk_heads, da_l2, dy)
                dypad[pl.ds(base, rows), :] = da * _silu_grad(c)
                return carry

            lax.fori_loop(0, nblk, pre, 0)

        def step(i, carry):
            base = pl.multiple_of(i * rows, rows)
            dxa = jnp.zeros((rows, LANES), f32)
            for jj in range(width):
                dxa = dxa + w_ref[jj:jj + 1, :] * dypad[pl.ds(base + (width - 1) - jj, rows), :]
            dx_ref[0, pl.ds(base, rows), :] = dxa
            dyc = dypad[pl.ds(base, rows), :]
            db_ref[...] += dyc.reshape(rows // SUBLANES, SUBLANES, LANES).sum(axis=0)
            for jj in range(width):
                prod = dyc * xpad[pl.ds(base + CONV_PAD - (width - 1) + jj, rows), :]
                acc[jj * SUBLANES:(jj + 1) * SUBLANES, :] += prod.reshape(rows // SUBLANES, SUBLANES, LANES).sum(axis=0)
            return carry

        lax.fori_loop(0, nblk, step, 0)

        @pl.when(bi == bl - 1)
        def _():
            dw_ref[...] = jnp.zeros((wp, LANES), f32)
            for jj in range(width):
                dw_ref[jj:jj + 1, :] = jnp.sum(acc[jj * SUBLANES:(jj + 1) * SUBLANES, :], axis=0, keepdims=True)

    if qk_heads is None:
        dy_specs = [pl.BlockSpec((1, s, LANES), lambda j, bi: (bi, 0, j))]
    else:
        hh = qk_heads
        dy_specs = [pl.BlockSpec((1, s, LANES), lambda j, bi: (bi * hh + jnp.minimum(j, hh - 1), 0, 0)),
                    pl.BlockSpec((1, s, LANES), lambda j, bi: (bi * hh + jnp.clip(j - hh, 0, hh - 1), 0, 0)),
                    pl.BlockSpec((1, s, LANES), lambda j, bi: (bi * hh + jnp.clip(j - 2 * hh, 0, hh - 1), 0, 0))]
    return _pc(body, name=name, grid=(cn // LANES, bl),
               in_specs=[pl.BlockSpec((1, s, LANES), lambda j, bi: (bi, 0, j)),
                         pl.BlockSpec((width, LANES), lambda j, bi: (0, j))] + dy_specs,
               out_specs=[pl.BlockSpec((1, s, LANES), lambda j, bi: (bi, 0, j)),
                          pl.BlockSpec((wp, LANES), lambda j, bi: (0, j)),
                          pl.BlockSpec((SUBLANES, LANES), lambda j, bi: (0, j))],
               out_shape=[jax.ShapeDtypeStruct((bl, s, cn), f32), jax.ShapeDtypeStruct((wp, cn), f32),
                          jax.ShapeDtypeStruct((SUBLANES, cn), f32)],
               scratch=[pltpu.VMEM((s + CONV_PAD, LANES), f32), pltpu.VMEM((s + CONV_PAD, LANES), f32),
                        pltpu.VMEM((width * SUBLANES, LANES), f32)],
               sem=("parallel", "arbitrary"))(x, w, *dys)


def _group_rows(s):
    for rows in (256, 128):
        if s % rows == 0:
            return rows
    return CHUNK


def _group_masks(rows):
    r = lax.broadcasted_iota(jnp.int32, (rows, rows), 0)
    c = lax.broadcasted_iota(jnp.int32, (rows, rows), 1)
    same_chunk = (r >> 6) == (c >> 6)
    return r, c, same_chunk


def _decay(gc_col, gc_row, causal):
    return jnp.exp(jnp.where(causal, gc_col - gc_row, NEG))


def _inv_unit_lower(a, r, c):
    return _inv_unit_lower_many([a], r, c)[0]


def _inv_unit_lower_many(mats, r, c):
    eye = (r == c).astype(f32)
    same16 = (r >> 4) == (c >> 4)
    ads = [jnp.where(same16, a, 0.0) for a in mats]
    aos = [a - ad for a, ad in zip(mats, ads)]
    xs = ads
    tds = [eye - x for x in xs]
    for _ in range(3):
        xs = [_dot(x, x) for x in xs]
        tds = [td + _dot(td, x) for td, x in zip(tds, xs)]
    bs = [_dot(td, ao) for td, ao in zip(tds, aos)]
    b2s = [_dot(b, b) for b in bs]
    b3s = [_dot(b, b2) for b, b2 in zip(bs, b2s)]
    return [_dot(eye - b + b2 - b3, td) for b, b2, b3, td in zip(bs, b2s, b3s, tds)]


def _lane_cumsum(x, y, reverse, name):
    rr = x.shape[0]

    def body(x_ref, y_ref, o_ref):
        i = lax.broadcasted_iota(jnp.int32, (CHUNK, CHUNK), 0)
        j = lax.broadcasted_iota(jnp.int32, (CHUNK, CHUNK), 1)
        tri = ((i >= j) if reverse else (i <= j)).astype(f32)
        o_ref[...] = jnp.dot(x_ref[...] + y_ref[...], tri, precision=HI, preferred_element_type=f32)

    spec = pl.BlockSpec((rr, CHUNK), lambda: (0, 0))
    return pl.pallas_call(body, name=name, in_specs=[spec, spec], out_specs=spec,
                          out_shape=jax.ShapeDtypeStruct((rr, CHUNK), f32))(x, y)


def _gdn_specs(n_heads, nblk, rows, hp=1, rev=False):
    def blk(n):
        return nblk - 1 - n if rev else n

    def qkv(off):
        return pl.BlockSpec((rows, hp * LANES), lambda g, n: (
            lax.div(g * hp, n_heads) * nblk + blk(n), lax.div(off * n_heads + lax.rem(g * hp, n_heads), hp)))

    def per_head(last, mult=1):
        return pl.BlockSpec((hp, rows * mult, last), lambda g, n: (g, blk(n), 0))

    def row_vec():
        return pl.BlockSpec((hp, 1, 1, rows), lambda g, n: (g, blk(n), 0, 0))

    return qkv, per_head, row_vec


def _gdn_prep(qkv, cols, grow, bl, s, n_heads):
    rows = _group_rows(s)
    nblk = s // rows
    bh_n = bl * n_heads
    hp = PREP_HEADS_PER_STEP if n_heads % PREP_HEADS_PER_STEP == 0 else 1
    qkv_spec, ph, rv = _gdn_specs(n_heads, nblk, rows, hp)

    def body(k_ref, v_ref, cols_ref, grow_ref, u_ref, w_ref, t_ref):
        r, c, same = _group_masks(rows)
        causal = jnp.logical_and(same, r >= c)
        mats, rhs = [], []
        for h in range(hp):
            hs = slice(h * LANES, (h + 1) * LANES)
            k = k_ref[:, hs]
            gc = cols_ref[h, :, 0:1]
            beta = cols_ref[h, :, 1:2]
            kb = k * beta
            mats.append(jnp.where(r > c, _dot_nt(kb, k) * _decay(gc, grow_ref[h, 0], causal), 0.0))
            rhs.append((v_ref[:, hs] * beta, kb * jnp.exp(gc)))
        for h, tm in enumerate(_inv_unit_lower_many(mats, r, c)):
            tb = tm.astype(bf16)
            u_ref[h] = jnp.dot(tb, rhs[h][0].astype(bf16), preferred_element_type=f32)
            w_ref[h] = jnp.dot(tb, rhs[h][1].astype(bf16), preferred_element_type=f32)
            t_ref[h] = tb

    return _pc(body, name="gdn_prep", grid=(bh_n // hp, nblk),
               in_specs=[qkv_spec(1), qkv_spec(2), ph(2), rv()],
               out_specs=[ph(LANES), ph(LANES), ph(rows)],
               out_shape=[jax.ShapeDtypeStruct((bh_n, s, LANES), f32), jax.ShapeDtypeStruct((bh_n, s, LANES), f32),
                          jax.ShapeDtypeStruct((bh_n, s, rows), bf16)],
               sem=("parallel", "parallel"))(qkv, qkv, cols, grow)


def _gdn_scan(qkv, u, w, cols, grow, bl, s, n_heads):
    rows = _group_rows(s)
    g_chunks = rows // CHUNK
    nblk = s // rows
    bh_n = bl * n_heads
    d = n_heads * LANES
    hp = HEADS_PER_STEP if n_heads % HEADS_PER_STEP == 0 else 1
    qkv_spec, ph, rv = _gdn_specs(n_heads, nblk, rows, hp)

    def body(q_ref, k_ref, u_ref, w_ref, cols_ref, grow_ref, o_ref, vn_ref, ss_ref, s_scr):
        _acc_init(pl.program_id(1), s_scr)
        r, c, same = _group_masks(rows)
        causal = jnp.logical_and(same, r >= c)
        qs, ks, gcs, qgs, ps = [], [], [], [], []
        for h in range(hp):
            hs = slice(h * LANES, (h + 1) * LANES)
            q, k, gc = q_ref[:, hs], k_ref[:, hs], cols_ref[h, :, 0:1]
            qs.append(q)
            ks.append(k)
            gcs.append(gc)
            qgs.append(q * jnp.exp(gc))
            ps.append(_dot_nt(q, k) * _decay(gc, grow_ref[h, 0], causal))
        st = [s_scr[h] for h in range(hp)]
        o_state = [[] for _ in range(hp)]
        vns = [[] for _ in range(hp)]
        for ci in range(g_chunks):
            sl = slice(ci * CHUNK, (ci + 1) * CHUNK)
            for h in range(hp):
                ss_ref[h, ci * LANES:(ci + 1) * LANES, :] = st[h]
                vn = u_ref[h, sl, :] - _dot(w_ref[h, sl, :], st[h])
                vns[h].append(vn)
                o_state[h].append(_dot(qgs[h][sl], st[h]))
                gc = gcs[h][sl]
                g_last = gc[CHUNK - 1:CHUNK, :]
                st[h] = jnp.exp(g_last) * st[h] + _dot_tn(ks[h][sl] * jnp.exp(g_last - gc), vn)
        for h in range(hp):
            s_scr[h] = st[h]
            vn_all = jnp.concatenate(vns[h], axis=0)
            vn_ref[h] = vn_all
            o_ref[:, h * LANES:(h + 1) * LANES] = jnp.concatenate(o_state[h], axis=0) + _dot(ps[h], vn_all)

    return _pc(body, name="gdn_scan", grid=(bh_n // hp, nblk),
               in_specs=[qkv_spec(0), qkv_spec(1), ph(LANES), ph(LANES), ph(2), rv()],
               out_specs=[qkv_spec(0), ph(LANES), ph(LANES, mult=LANES // CHUNK)],
               out_shape=[jax.ShapeDtypeStruct((bl * s, d), f32), jax.ShapeDtypeStruct((bh_n, s, LANES), f32),
                          jax.ShapeDtypeStruct((bh_n, (s // CHUNK) * LANES, LANES), f32)],
               scratch=[pltpu.VMEM((hp, LANES, LANES), f32)],
               sem=("parallel", "arbitrary"))(qkv, qkv, u, w, cols, grow)


def _gdn_scan_bwd(do, qkv, w, vn, cols, grow, ss, bl, s, n_heads):
    rows = _group_rows(s)
    g_chunks = rows // CHUNK
    nblk = s // rows
    bh_n = bl * n_heads
    hp = HEADS_PER_STEP if n_heads % HEADS_PER_STEP == 0 else 1
    qkv_spec, ph, rv = _gdn_specs(n_heads, nblk, rows, hp, rev=True)

    def body(do_ref, q_ref, k_ref, w_ref, vn_ref, cols_ref, grow_ref, ss_ref,
             du_ref, dw_ref, dq_ref, dk_ref, dcol_ref, drow_ref, ds_scr):
        _acc_init(pl.program_id(1), ds_scr)
        r, c, same = _group_masks(rows)
        causal = jnp.logical_and(same, r >= c)
        last_row = lax.broadcasted_iota(jnp.int32, (CHUNK, 1), 0) == CHUNK - 1
        pre = []
        for h in range(hp):
            hs = slice(h * LANES, (h + 1) * LANES)
            do, q, k = do_ref[:, hs], q_ref[:, hs], k_ref[:, hs]
            vn = vn_ref[h]
            gc = cols_ref[h, :, 0:1]
            dmat = _decay(gc, grow_ref[h, 0], causal)
            gam = jnp.exp(gc)
            qk = _dot_nt(q, k)
            dpd = _dot_nt(do, vn) * dmat
            ep = dpd * qk
            drow_ref[h, 0] = -jnp.sum(ep, axis=0, keepdims=True)
            pre.append(dict(do=do, k=k, vn=vn, gc=gc, gam=gam, qg=q * gam,
                            dvn_intra=_dot_tn(qk * dmat, do), dq_intra=_dot(dpd, k), dk_intra=_dot_tn(dpd, q),
                            ep_rows=jnp.sum(ep, axis=-1, keepdims=True)))
        ds = [ds_scr[h] for h in range(hp)]
        for ci in reversed(range(g_chunks)):
            sl = slice(ci * CHUNK, (ci + 1) * CHUNK)
            for h in range(hp):
                pr = pre[h]
                st = ss_ref[h, ci * LANES:(ci + 1) * LANES, :]
                gc = pr["gc"][sl]
                g_last = gc[CHUNK - 1:CHUNK, :]
                gam_last = jnp.exp(g_last)
                kd_scale = jnp.exp(g_last - gc)
                kdec = pr["k"][sl] * kd_scale
                do_c = pr["do"][sl]
                qg_c = pr["qg"][sl]
                dvn = pr["dvn_intra"][sl] + _dot(kdec, ds[h])
                dkdec = _dot_nt(pr["vn"][sl], ds[h])
                du_ref[h, sl, :] = dvn
                dw_ref[h, sl, :] = -_dot_nt(dvn, st)
                dqg = _dot_nt(do_c, st)
                dq_ref[h, sl, :] = dqg * pr["gam"][sl] + pr["dq_intra"][sl]
                dk_ref[h, sl, :] = pr["dk_intra"][sl] + dkdec * kd_scale
                kd_rows = jnp.sum(dkdec * kdec, axis=-1, keepdims=True)
                extra = jnp.sum(kd_rows) + gam_last * jnp.sum(st * ds[h])
                dcol_ref[h, sl, :] = (jnp.sum(dqg * qg_c, axis=-1, keepdims=True) + pr["ep_rows"][sl] - kd_rows
                                      + jnp.where(last_row, extra, 0.0))
                ds[h] = gam_last * ds[h] + _dot_tn(qg_c, do_c) - _dot_tn(w_ref[h, sl, :], dvn)
        for h in range(hp):
            ds_scr[h] = ds[h]

    return _pc(body, name="gdn_scan_bwd", grid=(bh_n // hp, nblk),
               in_specs=[qkv_spec(0), qkv_spec(0), qkv_spec(1), ph(LANES), ph(LANES), ph(2), rv(),
                         ph(LANES, mult=LANES // CHUNK)],
               out_specs=[ph(LANES), ph(LANES), ph(LANES), ph(LANES), ph(1), rv()],
               out_shape=[jax.ShapeDtypeStruct((bh_n, s, LANES), f32)] * 4
               + [jax.ShapeDtypeStruct((bh_n, s, 1), f32), jax.ShapeDtypeStruct((bh_n, nblk, 1, rows), f32)],
               scratch=[pltpu.VMEM((hp, LANES, LANES), f32)],
               sem=("parallel", "arbitrary"))(do, qkv, qkv, w, vn, cols, grow, ss)


def _gdn_prep_bwd(qkv, cols, grow, tmat, du, dw, dk_scan, dcol_scan, drow_scan, bl, s, n_heads):
    rows = _group_rows(s)
    nblk = s // rows
    bh_n = bl * n_heads
    hp = PREP_HEADS_PER_STEP if n_heads % PREP_HEADS_PER_STEP == 0 else 1
    qkv_spec, ph, rv = _gdn_specs(n_heads, nblk, rows, hp)

    def body(k_ref, v_ref, cols_ref, grow_ref, t_ref, du_ref, dw_ref, dks_ref, dcs_ref, drs_ref,
             dk_ref, dv_ref, dcols_ref, drow_ref):
        r, c, same = _group_masks(rows)
        causal = jnp.logical_and(same, r >= c)
        hh = range(hp)
        ks = [k_ref[:, h * LANES:(h + 1) * LANES] for h in hh]
        vs = [v_ref[:, h * LANES:(h + 1) * LANES] for h in hh]
        gcs = [cols_ref[h, :, 0:1] for h in hh]
        betas = [cols_ref[h, :, 1:2] for h in hh]
        tms = [t_ref[h] for h in hh]
        dus = [du_ref[h] for h in hh]
        dws = [dw_ref[h] for h in hh]
        gams = [jnp.exp(gc) for gc in gcs]
        kbs = [k * b for k, b in zip(ks, betas)]
        kbgs = [kb * g for kb, g in zip(kbs, gams)]
        dts = [jnp.where(same, _dot_nt(dus[h], vs[h] * betas[h]) + _dot_nt(dws[h], kbgs[h]), 0.0) for h in hh]
        dvbs = [_dot_tn(tms[h], dus[h]) for h in hh]
        dkbgs = [_dot_tn(tms[h], dws[h]) for h in hh]
        kks = [_dot_nt(kbs[h], ks[h]) for h in hh]
        inner = [_dot_nt(dts[h], tms[h]) for h in hh]
        dads = [jnp.where(r > c, -_dot_tn(tms[h], inner[h]), 0.0) * _decay(gcs[h], grow_ref[h, 0], causal) for h in hh]
        dkbs = [dkbgs[h] * gams[h] + _dot(dads[h], ks[h]) for h in hh]
        dk2 = [_dot_tn(dads[h], kbs[h]) for h in hh]
        for h in hh:
            dk_ref[h] = dks_ref[h] + dk2[h] + dkbs[h] * betas[h]
            dv_ref[h] = dvbs[h] * betas[h]
            ea = dads[h] * kks[h]
            dcols_ref[h, :, 0:1] = (dcs_ref[h] + jnp.sum(dkbgs[h] * kbgs[h], axis=-1, keepdims=True)
                                    + jnp.sum(ea, axis=-1, keepdims=True))
            dcols_ref[h, :, 1:2] = (jnp.sum(dvbs[h] * vs[h], axis=-1, keepdims=True)
                                    + jnp.sum(dkbs[h] * ks[h], axis=-1, keepdims=True))
            drow_ref[h, 0] = drs_ref[h, 0] - jnp.sum(ea, axis=0, keepdims=True)

    return _pc(body, name="gdn_prep_bwd", grid=(bh_n // hp, nblk),
               in_specs=[qkv_spec(1), qkv_spec(2), ph(2), rv(), ph(rows), ph(LANES), ph(LANES), ph(LANES), ph(1), rv()],
               out_specs=[ph(LANES), ph(LANES), ph(2), rv()],
               out_shape=[jax.ShapeDtypeStruct((bh_n, s, LANES), f32), jax.ShapeDtypeStruct((bh_n, s, LANES), f32),
                          jax.ShapeDtypeStruct((bh_n, s, 2), f32), jax.ShapeDtypeStruct((bh_n, nblk, 1, rows), f32)],
               sem=("parallel", "parallel"))(qkv, qkv, cols, grow, tmat, du, dw, dk_scan, dcol_scan, drow_scan)


def _row(v):
    return v.reshape(1, -1).astype(f32)


def _pad_lanes(v):
    v = v.reshape(1, -1).astype(f32)
    return jnp.pad(v, ((0, 0), (0, LANES - v.shape[1])))


def _local_step(x, tgt, p, need, emit):
    bl, s, d = x.shape
    t = bl * s
    n_heads = p["gdn_a_log"].shape[-1]
    assert d == n_heads * LANES and s % CHUNK == 0
    x2 = x.reshape(t, d)
    tgt2 = tgt.reshape(t, d)
    gr = {}

    n0, ub, gl = _pw1_glu(x2, _row(p["norm_mix_g"][0]), p["cv_w_pw1"], _row(p["cv_b_pw1"]))
    dc = _dwconv_fwd(gl.reshape(bl, s, d), p["cv_w_dw"], _row(p["cv_b_dw"]), "dwconv_fwd").reshape(t, d)
    sb, h1 = _ln_silu_mm_res(dc, _row(p["cv_ln_g"]), _row(p["cv_ln_b"]), p["cv_w_pw2"], _row(p["cv_b_pw2"]), x2)
    m0 = need("mlp0", h1)
    n1, f0, r0, h2 = _mlp_fwd(h1, _row(p["norm_ffn_g"][0]), m0["w1"], m0["w2"], "mlp_fwd0")

    gd = need("gdn", h2)
    w_in = gd["w_in"]
    w_ab = jnp.pad(w_in[:, 4 * d:], ((0, 0), (0, LANES - 2 * n_heads)))
    a_log_pad = _pad_lanes(p["gdn_a_log"])
    dt_pad = _pad_lanes(p["gdn_dt_bias"])
    n2, qkv_pre, z, ab, gbeta = _gdn_in(h2, _row(p["norm_mix_g"][1]), w_in, w_ab, a_log_pad, dt_pad, n_heads)
    zero_bias = jnp.zeros((1, 3 * d), f32)
    qkv = _dwconv_fwd(qkv_pre.reshape(bl, s, 3 * d), p["gdn_conv_w"], zero_bias, "sconv_fwd", qk_heads=n_heads).reshape(t, 3 * d)
    bh_n, rows = bl * n_heads, _group_rows(s)
    gates = gbeta[:, :2 * n_heads].reshape(bl, s, 2, n_heads).transpose(2, 0, 3, 1).reshape(2, bh_n, s)
    g_lanes = gates[0].reshape(bh_n * (s // CHUNK), CHUNK)
    gc_lanes = _lane_cumsum(g_lanes, jnp.zeros_like(g_lanes), False, "gdn_gate_cumsum")
    grow = gc_lanes.reshape(bh_n, s // rows, 1, rows)
    cols = jnp.stack([gc_lanes.reshape(bh_n, s), gates[1]], axis=-1)
    u, w, tmat = _gdn_prep(qkv, cols, grow, bl, s, n_heads)
    o, vn, ss = _gdn_scan(qkv, u, w, cols, grow, bl, s, n_heads)
    onb, h3 = _gated_norm_mm_res(o, z, _row(p["gdn_norm_g"]), gd["w_out"], h2, n_heads)
    m1 = need("mlp1", h3)
    n3, f1, r1, h4 = _mlp_fwd(h3, _row(p["norm_ffn_g"][1]), m1["w1"], m1["w2"], "mlp_fwd1")
    loss_acc, dh4, dgf = _loss_head(h4, _row(p["final_norm_g"]), tgt2)
    loss = loss_acc[0, 0]
    gr["final_norm_g"] = dgf[0]

    df1, dh3, dg_ffn1, _ = _mlp_bwd(dh4, h3, _row(p["norm_ffn_g"][1]), f1, m1["w1"], m1["w2"], "mlp_bwd1")
    dw2_1 = _mm_tn(r1, dh4, "dw_mlp2_1", bf16)
    dw1_1 = _mm_tn_blocked(n3, df1, "dw_mlp1_1", bf16)

    dw_out = _mm_tn(onb, dh3, "dw_gdn_out", bf16)
    tie = emit("late", {"mlp_w2_1": dw2_1, "mlp_w1_1": dw1_1, "gdn_w_out": dw_out})
    don = _mm_nt([(dh3, gd["w_out"])], "dx_gdn_out")
    do, dz, dng = _gated_norm_bwd(don, o, z, _row(p["gdn_norm_g"]) + tie, n_heads)
    gr["gdn_norm_g"] = dng[0]
    du, dw_, dq, dk_scan, dcol_scan, drow_scan = _gdn_scan_bwd(do, qkv, w, vn, cols, grow, ss, bl, s, n_heads)
    dk, dv, dcols, drow = _gdn_prep_bwd(qkv, cols, grow, tmat, du, dw_, dk_scan, dcol_scan, drow_scan, bl, s, n_heads)
    dg_lanes = _lane_cumsum(dcols[..., 0].reshape(g_lanes.shape), drow.reshape(g_lanes.shape), True, "gdn_gate_cumsum_bwd")
    dgb2 = jnp.stack([dg_lanes.reshape(bh_n, s), dcols[..., 1]], axis=-1)
    dqkv_pre, dconv_w, _ = _dwconv_bwd(qkv_pre.reshape(bl, s, 3 * d), [dq, dk, dv], p["gdn_conv_w"], "sconv_bwd", qk_heads=n_heads)
    gr["gdn_conv_w"] = dconv_w[:p["gdn_conv_w"].shape[0]]
    dgb = dgb2.reshape(bl, n_heads, s, 2).transpose(0, 2, 3, 1).reshape(t, 2 * n_heads)
    dgb = jnp.pad(dgb, ((0, 0), (0, LANES - 2 * n_heads)))
    dab, dal, ddt = _gates_bwd(dgb, ab, a_log_pad, dt_pad, n_heads)
    gr["gdn_a_log"] = dal[0, :n_heads]
    gr["gdn_dt_bias"] = ddt[0, :n_heads]
    dqkv2 = dqkv_pre.reshape(t, 3 * d)
    dw_in = jnp.concatenate(
        [_mm_tn(n2, dqkv2, "dw_gdn_in_qkv"), _mm_tn(n2, dz, "dw_gdn_in_z"), _mm_tn(n2, dab, "dw_gdn_in_ab")[:, :2 * n_heads]], axis=1)
    tie = emit("gdn_in", {"gdn_w_in": dw_in})
    dn2 = _mm_nt([(dqkv2, (w_in, 0)), (dz, (w_in, 3)), (dab, w_ab)], "dx_gdn_in")
    dh2, dg_mix1 = _rms_bwd_res(dn2, h2, _row(p["norm_mix_g"][1]) + tie, dh3, "rms_bwd_gdn")

    df0, dh1, dg_ffn0, cs_h1 = _mlp_bwd(dh2, h1, _row(p["norm_ffn_g"][0]), f0, m0["w1"], m0["w2"], "mlp_bwd0")
    dw2_0 = _mm_tn(r0, dh2, "dw_mlp2_0", bf16)
    dw1_0 = _mm_tn_blocked(n1, df0, "dw_mlp1_0", bf16)
    dw_pw2 = _mm_tn(sb, dh1, "dw_pw2", bf16)
    tie = emit("mlp0", {"mlp_w2_0": dw2_0, "mlp_w1_0": dw1_0, "cv_w_pw2": dw_pw2})
    gr["norm_ffn_g"] = jnp.stack([dg_ffn0[0], dg_ffn1[0]])

    gr["cv_b_pw2"] = cs_h1[0]
    ds = _mm_nt([(dh1, p["cv_w_pw2"])], "dx_pw2")
    ddc, dlng, dlnb, cs_dc = _ln_silu_bwd(ds, dc, _row(p["cv_ln_g"]) + tie, _row(p["cv_ln_b"]))
    gr["cv_ln_g"] = dlng[0]
    gr["cv_ln_b"] = dlnb[0]
    gr["cv_b_dw"] = cs_dc[0]
    dgl, dw_dw, _ = _dwconv_bwd(gl.reshape(bl, s, d), [ddc.reshape(bl, s, d)], p["cv_w_dw"], "dwconv_bwd")
    gr["cv_w_dw"] = dw_dw[:p["cv_w_dw"].shape[0]]
    dub, cs_u = _glu_bwd(dgl.reshape(t, d), ub)
    gr["cv_b_pw1"] = cs_u[0]
    gr["cv_w_pw1"] = _mm_tn_blocked(n0, dub, "dw_pw1", bf16)
    dn0 = _mm_nt_blocked(dub, p["cv_w_pw1"], "dx_pw1")
    dx, dg_mix0 = _rms_bwd_res(dn0, x2, _row(p["norm_mix_g"][0]), dh1, "rms_bwd_conv")
    gr["norm_mix_g"] = jnp.stack([dg_mix0[0], dg_mix1[0]])
    return loss, dx.reshape(bl, s, d), gr


ANY = pl.BlockSpec(memory_space=pl.ANY)
MESH = pl.DeviceIdType.MESH


def _flip(v, bit):
    return 1 - v if bit else v


def _all_gather_many(shards):
    na = len(shards)

    def body(*refs):
        x_refs, o_refs = refs[:na], refs[na:2 * na]
        send_sems, recv_sems, local_sems = refs[2 * na:]
        x, y, c = lax.axis_index("x"), lax.axis_index("y"), lax.axis_index("c")
        me, sibling = (x, y, c), (x, y, 1 - c)
        chips = [(1 - x, y), (x, 1 - y), (1 - x, 1 - y)]

        def copy(a, k, block, to, src=None):
            px, py, pc = block
            dst = o_refs[a].at[4 * px + 2 * py + pc]
            return pltpu.make_async_remote_copy(
                src_ref=dst if src is None else src, dst_ref=dst,
                send_sem=send_sems.at[a, k], recv_sem=recv_sems.at[a, k], device_id=to, device_id_type=MESH)

        mine = [pltpu.make_async_copy(x_refs[a], o_refs[a].at[4 * x + 2 * y + c], local_sems.at[a]) for a in range(na)]
        first = []
        for a in range(na):
            first.append(copy(a, 0, me, sibling, src=x_refs[a]))
            first += [copy(a, 1 + j, me, (*chip, c), src=x_refs[a]) for j, chip in enumerate(chips)]
        for cp in mine + first:
            cp.start()
        passed = []
        for j, chip in enumerate(chips):
            for a in range(na):
                copy(a, 1 + j, (*chip, c), me).wait_recv()
                fwd = copy(a, 4 + j, (*chip, c), sibling)
                fwd.start()
                passed.append(fwd)
        for a in range(na):
            copy(a, 0, sibling, me).wait_recv()
        for j, chip in enumerate(chips):
            for a in range(na):
                copy(a, 4 + j, (*chip, 1 - c), me).wait_recv()
        for cp in first + passed:
            cp.wait_send()
        for cp in mine:
            cp.wait()

    return pl.pallas_call(
        body, name="weights_all_gather",
        out_shape=[jax.ShapeDtypeStruct((N_DEV,) + a.shape, a.dtype) for a in shards],
        in_specs=[ANY] * na, out_specs=[ANY] * na,
        scratch_shapes=[pltpu.SemaphoreType.DMA((na, 7)), pltpu.SemaphoreType.DMA((na, 7)), pltpu.SemaphoreType.DMA((na,))],
        compiler_params=pltpu.CompilerParams(has_side_effects=True),
    )(*shards)


HBM = pl.BlockSpec(memory_space=pltpu.HBM)
SEM = pl.BlockSpec(memory_space=pltpu.SEMAPHORE)
EFFECT = pltpu.SideEffectType.DATAFLOW_SIDE_EFFECTING
N_PEERS = N_DEV - 1


def _exchange_copies(src_refs, land_refs, send_sems, recv_sems, scatter):
    x, y, c = lax.axis_index("x"), lax.axis_index("y"), lax.axis_index("c")
    me = 4 * x + 2 * y + c
    copies = []
    for a, (src, land) in enumerate(zip(src_refs, land_refs)):
        for k in range(1, N_DEV):
            px, py, pc = _flip(x, k & 4), _flip(y, k & 2), _flip(c, k & 1)
            i = a * N_PEERS + k - 1
            copies.append(pltpu.make_async_remote_copy(
                src_ref=src.at[4 * px + 2 * py + pc] if scatter[a] else src, dst_ref=land.at[me],
                send_sem=send_sems.at[i], recv_sem=recv_sems.at[i], device_id=(px, py, pc), device_id_type=MESH))
    return copies


def _exchange_start(srcs, scatter, name):
    na = len(srcs)
    lands = [lax.empty(s.shape if sc else (N_DEV,) + s.shape, s.dtype) for s, sc in zip(srcs, scatter)]

    def body(*refs):
        copies = _exchange_copies(refs[:na], refs[na:2 * na], refs[2 * na], refs[2 * na + 1], scatter)
        for cp in copies:
            cp.start()
        token = refs[-1]
        token[...] = jnp.zeros_like(token)

    outs = pl.pallas_call(
        body, name=name,
        out_shape=(pltpu.SemaphoreType.DMA((na * N_PEERS,)), pltpu.SemaphoreType.DMA((na * N_PEERS,)))
        + tuple(pltpu.HBM(a.shape, a.dtype) for a in srcs + lands) + (jax.ShapeDtypeStruct((SUBLANES, LANES), f32),),
        in_specs=[HBM] * (2 * na),
        out_specs=(SEM, SEM) + (HBM,) * (2 * na) + (pl.BlockSpec(memory_space=pltpu.VMEM),),
        input_output_aliases={i: 2 + i for i in range(2 * na)},
        compiler_params=pltpu.CompilerParams(has_side_effects=EFFECT),
    )(*[pltpu.with_memory_space_constraint(a, pltpu.HBM) for a in srcs + lands])
    return outs[0], outs[1], list(outs[2:2 + na]), list(outs[2 + na:2 + 2 * na]), outs[-1]


def _exchange_wait(started, after, scatter, name):
    send_sems, recv_sems, srcs, lands, _ = started
    na = len(srcs)

    def body(*refs):
        for cp in _exchange_copies(refs[:na], refs[na:2 * na], refs[2 * na], refs[2 * na + 1], scatter):
            cp.wait_send()
            cp.wait_recv()

    outs = pl.pallas_call(
        body, name=name,
        out_shape=tuple(pltpu.HBM(a.shape, a.dtype) for a in srcs + lands),
        in_specs=[HBM] * (2 * na) + [SEM, SEM, ANY], out_specs=(HBM,) * (2 * na),
        input_output_aliases={i: i for i in range(2 * na)},
        compiler_params=pltpu.CompilerParams(has_side_effects=EFFECT),
    )(*srcs, *lands, send_sems, recv_sems, after)
    return list(outs[na:])


def _own_block(land, block, me):
    return lax.dynamic_update_index_in_dim(land, block, me, 0)


def _sum8_adamw(r2, w, m, v, layer, name):
    _, rr, cc = r2.shape
    tr = _tile(rr, 256)
    bc1 = 1.0 - ADAM_B1 ** ADAM_STEP
    bc2 = 1.0 - ADAM_B2 ** ADAM_STEP

    def body(r_ref, w_ref, m_ref, v_ref, g_ref, d_ref, nm_ref, nv_ref):
        gv = r_ref[0].astype(f32)
        for q in range(1, N_DEV):
            gv = gv + r_ref[q].astype(f32)
        g_ref[...] = gv
        nm = ADAM_B1 * m_ref[...] + (1.0 - ADAM_B1) * gv
        nv = ADAM_B2 * v_ref[...] + (1.0 - ADAM_B2) * (gv * gv)
        nm_ref[...] = nm
        nv_ref[...] = nv
        d_ref[...] = -ADAM_LR * ((nm / bc1) / (jnp.sqrt(nv / bc2) + ADAM_EPS) + ADAM_WD * w_ref[...])

    lspec = pl.BlockSpec((None, tr, cc), lambda i: (layer, i, 0))
    return _pc(body, name=name, grid=(rr // tr,),
               in_specs=[pl.BlockSpec((N_DEV, tr, cc), lambda i: (0, i, 0)), lspec, lspec, lspec],
               out_specs=[_rows(tr, cc)] * 4, out_shape=[jax.ShapeDtypeStruct((rr, cc), f32)] * 4,
               sem=("parallel",))(r2, w, m, v)


def _sum_devices(recv):
    _, rr, _ = recv.shape
    tr = _tile(rr, 512)

    def body(r_ref, o_ref):
        acc = r_ref[0]
        for i in range(1, N_DEV):
            acc = acc + r_ref[i]
        o_ref[...] = acc

    return _pc(body, name="grads_sum", grid=(rr // tr,),
               in_specs=[pl.BlockSpec((N_DEV, tr, LANES), lambda i: (0, i, 0))],
               out_specs=_rows(tr, LANES), out_shape=jax.ShapeDtypeStruct((rr, LANES), f32), sem=("parallel",))(recv)


def _adamw(w, g, m, v):
    rr = w.shape[0]
    tr = _tile(rr, 512)
    bc1 = 1.0 - ADAM_B1 ** ADAM_STEP
    bc2 = 1.0 - ADAM_B2 ** ADAM_STEP

    def body(w_ref, g_ref, m_ref, v_ref, d_ref, nm_ref, nv_ref):
        gv = g_ref[...]
        nm = ADAM_B1 * m_ref[...] + (1.0 - ADAM_B1) * gv
        nv = ADAM_B2 * v_ref[...] + (1.0 - ADAM_B2) * (gv * gv)
        nm_ref[...] = nm
        nv_ref[...] = nv
        d_ref[...] = -ADAM_LR * ((nm / bc1) / (jnp.sqrt(nv / bc2) + ADAM_EPS) + ADAM_WD * w_ref[...])

    spec = _rows(tr, LANES)
    return _pc(body, name="adamw", grid=(rr // tr,), in_specs=[spec] * 4, out_specs=[spec] * 3,
               out_shape=[jax.ShapeDtypeStruct((rr, LANES), f32)] * 3, sem=("parallel",))(w, g, m, v)


PACK_ROWS = 512
PART_ROWS = 16


def _pack(arrs, lead=()):
    nl = len(lead)
    parts, sizes = [], []
    for a in arrs:
        flat = a.reshape(lead + (-1,))
        n = flat.shape[-1]
        rows = -(-n // (LANES * PART_ROWS)) * PART_ROWS
        flat = jnp.pad(flat, [(0, 0)] * nl + [(0, rows * LANES - n)])
        parts.append(flat.reshape(lead + (rows, LANES)))
        sizes.append((rows, n))
    total = sum(r for r, _ in sizes)
    padded = -(-total // PACK_ROWS) * PACK_ROWS
    if padded > total:
        parts.append(jnp.zeros(lead + (padded - total, LANES), parts[0].dtype))
    return jnp.concatenate(parts, axis=nl), sizes


def _unpack(packed, sizes, shapes, lead=()):
    nl = len(lead)
    out, off = [], 0
    for (rows, n), shp in zip(sizes, shapes):
        piece = lax.slice_in_dim(packed, off, off + rows, axis=nl).reshape(lead + (rows * LANES,))
        out.append(lax.slice_in_dim(piece, 0, n, axis=nl).reshape(lead + tuple(shp)))
        off += rows
    return out


def _cols_to_blocks(a):
    n = a.shape[-1] // N_DEV
    a = a.reshape(a.shape[:-1] + (N_DEV, n))
    return jnp.moveaxis(a, -2, 0)


def _blocks_to_cols(a):
    a = jnp.moveaxis(a, 0, -2)
    return a.reshape(a.shape[:-2] + (a.shape[-2] * a.shape[-1],))


def _rows_to_blocks(a):
    k = a.shape[-2] // N_DEV
    a = a.reshape(a.shape[:-2] + (N_DEV, k, a.shape[-1]))
    return jnp.moveaxis(a, -3, 0)


def _blocks_to_rows(a):
    a = jnp.moveaxis(a, 0, -3)
    return a.reshape(a.shape[:-3] + (a.shape[-3] * a.shape[-2], a.shape[-1]))


COL_SHARDED = ("cv_w_pw1", "gdn_w_in", "mlp_w1")
ROW_SHARDED = ("cv_w_pw2", "gdn_w_out", "mlp_w2")
CONV_SHARDED = ("cv_w_dw", "gdn_conv_w")
REPLICATED = ("norm_mix_g", "norm_ffn_g", "final_norm_g", "cv_b_pw1", "cv_b_dw", "cv_ln_g", "cv_ln_b", "cv_b_pw2",
              "gdn_a_log", "gdn_dt_bias", "gdn_norm_g")
WEIGHTS = ("norm_mix_g", "norm_ffn_g", "final_norm_g", "cv_w_pw1", "cv_b_pw1", "cv_w_dw", "cv_b_dw", "cv_ln_g",
           "cv_ln_b", "cv_w_pw2", "cv_b_pw2", "gdn_w_in", "gdn_conv_w", "gdn_a_log", "gdn_dt_bias", "gdn_norm_g",
           "gdn_w_out", "mlp_w1", "mlp_w2")
MATMUL_SHARDED = COL_SHARDED + ROW_SHARDED


def _squeeze_layer(name, a):
    if name in ("norm_mix_g", "norm_ffn_g", "final_norm_g", "mlp_w1", "mlp_w2"):
        return a
    return a[0]


def _gather_weights(shards):
    me = 4 * lax.axis_index("x") + 2 * lax.axis_index("y") + lax.axis_index("c")
    first = ("cv_w_pw1", "cv_w_pw2") + CONV_SHARDED
    got = dict(zip(first, _all_gather_many([shards[n] if n in CONV_SHARDED else shards[n].astype(bf16) for n in first])))
    now = {"cv_w_pw1": got["cv_w_pw1"], "cv_w_pw2": _blocks_to_rows(got["cv_w_pw2"])}
    for n in CONV_SHARDED:
        now[n] = _blocks_to_cols(got[n])

    def cast(a, tie):
        return (a + tie).astype(bf16)

    token, _ = lax.optimization_barrier((jnp.zeros((), f32), got["cv_w_pw1"]))
    later, started = {}, {}
    for group in ("mlp0", "gdn", "mlp1"):
        if group == "gdn":
            srcs = [cast(shards["gdn_w_in"], token).reshape(-1, LANES), cast(shards["gdn_w_out"], token)]
        else:
            layer = int(group[-1])
            srcs = [cast(shards["mlp_w1"][layer], token), cast(shards["mlp_w2"][layer], token)]
        later[group] = srcs
        started[group] = _exchange_start(srcs, [False] * len(srcs), f"weights_{group}_start")
        token = started[group][4][0, 0]

    def need(group, after):
        lands = _exchange_wait(started[group], after, [False] * len(later[group]), f"weights_{group}_wait")
        lands = [_own_block(ld, own, me) for ld, own in zip(lands, later[group])]
        if group == "gdn":
            w_in = _blocks_to_cols(lands[0].reshape((N_DEV,) + shards["gdn_w_in"].shape))
            return {"w_in": w_in, "w_out": _blocks_to_rows(lands[1])}
        return {"w1": lands[0], "w2": lands[1]}

    return now, need, token


def kernel(x, norm_mix_g, norm_ffn_g, final_norm_g, cv_w_pw1, cv_b_pw1, cv_w_dw, cv_b_dw, cv_ln_g, cv_ln_b, cv_w_pw2, cv_b_pw2, gdn_w_in, gdn_conv_w, gdn_a_log, gdn_dt_bias, gdn_norm_g, gdn_w_out, mlp_w1, mlp_w2, loss_target, m_norm_mix_g, m_norm_ffn_g, m_final_norm_g, m_cv_w_pw1, m_cv_b_pw1, m_cv_w_dw, m_cv_b_dw, m_cv_ln_g, m_cv_ln_b, m_cv_w_pw2, m_cv_b_pw2, m_gdn_w_in, m_gdn_conv_w, m_gdn_a_log, m_gdn_dt_bias, m_gdn_norm_g, m_gdn_w_out, m_mlp_w1, m_mlp_w2, v_norm_mix_g, v_norm_ffn_g, v_final_norm_g, v_cv_w_pw1, v_cv_b_pw1, v_cv_w_dw, v_cv_b_dw, v_cv_ln_g, v_cv_ln_b, v_cv_w_pw2, v_cv_b_pw2, v_gdn_w_in, v_gdn_conv_w, v_gdn_a_log, v_gdn_dt_bias, v_gdn_norm_g, v_gdn_w_out, v_mlp_w1, v_mlp_w2):
    w_in = dict(zip(WEIGHTS, (norm_mix_g, norm_ffn_g, final_norm_g, cv_w_pw1, cv_b_pw1, cv_w_dw, cv_b_dw, cv_ln_g, cv_ln_b, cv_w_pw2, cv_b_pw2, gdn_w_in, gdn_conv_w, gdn_a_log, gdn_dt_bias, gdn_norm_g, gdn_w_out, mlp_w1, mlp_w2)))
    m_in = dict(zip(WEIGHTS, (m_norm_mix_g, m_norm_ffn_g, m_final_norm_g, m_cv_w_pw1, m_cv_b_pw1, m_cv_w_dw, m_cv_b_dw, m_cv_ln_g, m_cv_ln_b, m_cv_w_pw2, m_cv_b_pw2, m_gdn_w_in, m_gdn_conv_w, m_gdn_a_log, m_gdn_dt_bias, m_gdn_norm_g, m_gdn_w_out, m_mlp_w1, m_mlp_w2)))
    v_in = dict(zip(WEIGHTS, (v_norm_mix_g, v_norm_ffn_g, v_final_norm_g, v_cv_w_pw1, v_cv_b_pw1, v_cv_w_dw, v_cv_b_dw, v_cv_ln_g, v_cv_ln_b, v_cv_w_pw2, v_cv_b_pw2, v_gdn_w_in, v_gdn_conv_w, v_gdn_a_log, v_gdn_dt_bias, v_gdn_norm_g, v_gdn_w_out, v_mlp_w1, v_mlp_w2)))
    me = 4 * lax.axis_index("x") + 2 * lax.axis_index("y") + lax.axis_index("c")

    shards = {n: _squeeze_layer(n, w_in[n]) for n in WEIGHTS}
    first, need, token = _gather_weights(shards)
    params = {n: shards[n] for n in REPLICATED}
    params.update(first)
    params["norm_mix_g"] = params["norm_mix_g"] + token

    def row_blocks(a):
        return a.reshape(N_DEV, a.shape[0] // N_DEV, a.shape[1])

    def flat_blocks(a):
        k, n8 = a.shape
        return _cols_to_blocks(a).reshape(N_DEV, k * (n8 // N_DEV) // LANES, LANES)

    def as_blocks(n, a):
        if n == "gdn_w_in":
            return flat_blocks(a).astype(bf16)
        return a if a.ndim == 3 else row_blocks(a)

    sent = []

    def emit(group, grads_out):
        names = list(grads_out)
        blocks = [as_blocks(n, grads_out[n]) for n in names]
        st = _exchange_start(blocks, [True] * len(blocks), f"grads_{group}_start")
        sent.append((names, blocks, st, [True] * len(blocks)))
        return st[4][0, 0]

    loss_part, grad_x, gr = _local_step(x, loss_target, params, need, emit)
    loss = lax.psum(loss_part, ("x", "y", "c"))

    small = REPLICATED + CONV_SHARDED
    small_send, ssz = _pack([gr[n] for n in small])
    last_blocks = [gr["cv_w_pw1"], small_send]
    last_scatter = [True, False]
    sent.append((["cv_w_pw1", "small"], last_blocks, _exchange_start(last_blocks, last_scatter, "grads_last_start"), last_scatter))

    recv = {}
    after = grad_x
    for names, blocks, st, scatter in sent:
        lands = _exchange_wait(st, after, scatter, f"grads_{names[0]}_wait")
        for n, ld, blk, sc in zip(names, lands, blocks, scatter):
            own = lax.dynamic_index_in_dim(blk, me, 0, keepdims=False) if sc else blk
            recv[n] = _own_block(ld, own, me)
        after = lands[0]

    small_sum = _sum_devices(recv["small"])
    grads = dict(zip(small, _unpack(small_sum, ssz, [gr[n].shape for n in small])))
    for n in CONV_SHARDED:
        cn = shards[n].shape[-1]
        grads[n] = lax.dynamic_slice_in_dim(grads[n], me * cn, cn, axis=1)

    def as3d(n, a):
        if n == "gdn_w_in":
            return a.reshape(a.shape[0], -1, LANES)
        return a

    big = [("cv_w_pw1", 0, "cv_w_pw1"), ("cv_w_pw2", 0, "cv_w_pw2"), ("gdn_w_in", 0, "gdn_w_in"), ("gdn_w_out", 0, "gdn_w_out"),
           ("mlp_w1", 0, "mlp_w1_0"), ("mlp_w1", 1, "mlp_w1_1"), ("mlp_w2", 0, "mlp_w2_0"), ("mlp_w2", 1, "mlp_w2_1")]
    res = {n: {} for n in MATMUL_SHARDED}
    for n, layer, key in big:
        res[n][layer] = _sum8_adamw(recv[key], as3d(n, w_in[n]), as3d(n, m_in[n]), as3d(n, v_in[n]), layer, f"adamw_{key}")
    out_groups = {n: [] for n in WEIGHTS}
    for n in MATMUL_SHARDED:
        layers = sorted(res[n])
        for k in range(4):
            pieces = [res[n][layer][k] for layer in layers]
            out_groups[n].append(jnp.stack(pieces).reshape(w_in[n].shape))

    sm_w = [shards[n] for n in small]
    sm_g = [grads[n].reshape(shards[n].shape) for n in small]
    sm_m = [_squeeze_layer(n, m_in[n]) for n in small]
    sm_v = [_squeeze_layer(n, v_in[n]) for n in small]
    wp, psz = _pack(sm_w)
    gp, _ = _pack(sm_g)
    mp, _ = _pack(sm_m)
    vp, _ = _pack(sm_v)
    dp, nmp, nvp = _adamw(wp, gp, mp, vp)
    shp = [a.shape for a in sm_w]
    for n, g, dl, nm, nv in zip(small, sm_g, _unpack(dp, psz, shp), _unpack(nmp, psz, shp), _unpack(nvp, psz, shp)):
        out_groups[n] = [a.reshape(w_in[n].shape) for a in (g, dl, nm, nv)]

    outs = [loss, grad_x]
    for k in range(4):
        outs += [out_groups[n][k] for n in WEIGHTS]
    return tuple(outs)
```

```python
import functools

import jax
import jax.numpy as jnp
from jax import lax
from jax.experimental import pallas as pl
from jax.experimental.pallas import tpu as pltpu

f32, bf16 = jnp.float32, jnp.bfloat16

NORM_EPS = 1e-6
L2_EPS = 1e-6
CHUNK = 64
LANES = 128
SUBLANES = 8
N_DEV = 8
VMEM_LIMIT = 56 * 1024 * 1024
CONV_PAD = 32
HEADS_PER_STEP = 4
PREP_HEADS_PER_STEP = 2
NEG = -1e30

ADAM_LR, ADAM_B1, ADAM_B2, ADAM_EPS, ADAM_WD, ADAM_STEP = 0.001, 0.9, 0.999, 1e-08, 0.01, 10

NT = (((1,), (1,)), ((), ()))
TN = (((0,), (0,)), ((), ()))
HI = lax.Precision.HIGHEST


def _pc(body, *, name, grid, in_specs, out_specs, out_shape, scratch=(), sem=None):
    return pl.pallas_call(
        body, name=name, grid=grid, in_specs=in_specs, out_specs=out_specs, out_shape=out_shape,
        scratch_shapes=list(scratch),
        compiler_params=pltpu.CompilerParams(dimension_semantics=sem, vmem_limit_bytes=VMEM_LIMIT))


def _rows(tm, n):
    return pl.BlockSpec((tm, n), lambda i: (i, 0))


def _const(shape):
    return pl.BlockSpec(shape, lambda *_: (0,) * len(shape))


def _resident(shape):
    return pl.BlockSpec(shape, lambda *_: (0,) * len(shape), pipeline_mode=pl.Buffered(1))


def _tile(t, pref):
    return pref if t % pref == 0 else t


def _dot(a, b):
    return jnp.dot(a.astype(bf16), b.astype(bf16), preferred_element_type=f32)


def _dot_nt(a, b):
    return lax.dot_general(a.astype(bf16), b.astype(bf16), NT, preferred_element_type=f32)


def _dot_tn(a, b):
    return lax.dot_general(a.astype(bf16), b.astype(bf16), TN, preferred_element_type=f32)


def _sigmoid(x):
    return 1.0 / (1.0 + jnp.exp(-x))


def _silu_grad(x):
    s = _sigmoid(x)
    return s * (1.0 + x * (1.0 - s))


def _rms(x, g):
    rstd = lax.rsqrt(jnp.mean(x * x, axis=-1, keepdims=True) + NORM_EPS)
    xh = x * rstd
    return xh * g, xh, rstd


def _rms_bwd(dn, xh, rstd, g):
    dxh = dn * g
    return rstd * (dxh - xh * jnp.mean(dxh * xh, axis=-1, keepdims=True))


def _acc_init(step, *refs):
    @pl.when(step == 0)
    def _():
        for r in refs:
            r[...] = jnp.zeros(r.shape, r.dtype)


def _acc_rows(ref, val):
    ref[0:1, :] += jnp.sum(val, axis=0, keepdims=True)


def _pw1_glu(x, g, w, b):
    t, d = x.shape
    tm = _tile(t, 256)

    nb_w = w.shape[0]

    def body(x_ref, g_ref, w_hbm, b_ref, n_ref, u_ref, gl_ref, w_ref, sems):
        _fetch_blocks(pl.program_id(0), w_hbm, w_ref, sems, True)
        n, _, _ = _rms(x_ref[...], g_ref[...])
        nb = n.astype(bf16)
        n_ref[...] = nb
        u = jnp.dot(nb, w_ref[...], preferred_element_type=f32) + b_ref[...]
        u_ref[...] = u.astype(bf16)
        gl_ref[...] = u[:, :d] * _sigmoid(u[:, d:])

    return _pc(body, name="pw1_glu", grid=(t // tm,),
               in_specs=[_rows(tm, d), _const((1, d)), ANY, _const((1, 2 * d))],
               out_specs=[_rows(tm, d), _rows(tm, 2 * d), _rows(tm, d)],
               out_shape=[jax.ShapeDtypeStruct((t, d), bf16), jax.ShapeDtypeStruct((t, 2 * d), bf16),
                          jax.ShapeDtypeStruct((t, d), f32)],
               scratch=[pltpu.VMEM((d, 2 * d), bf16), pltpu.SemaphoreType.DMA((nb_w,))],
               sem=("arbitrary",))(x, g, w, b)


def _ln_silu_mm_res(dc, ln_g, ln_b, w, b, res):
    t, d = dc.shape
    tm = _tile(t, 256)

    def body(x_ref, g_ref, bb_ref, w_ref, b_ref, r_ref, s_ref, o_ref):
        x = x_ref[...]
        xc = x - jnp.mean(x, axis=-1, keepdims=True)
        rstd = lax.rsqrt(jnp.mean(xc * xc, axis=-1, keepdims=True) + NORM_EPS)
        ln = xc * rstd * g_ref[...] + bb_ref[...]
        sb = (ln * _sigmoid(ln)).astype(bf16)
        s_ref[...] = sb
        o_ref[...] = r_ref[...] + jnp.dot(sb, w_ref[...], preferred_element_type=f32) + b_ref[...]

    return _pc(body, name="ln_silu_pw2", grid=(t // tm,),
               in_specs=[_rows(tm, d), _const((1, d)), _const((1, d)), _resident((d, d)), _const((1, d)), _rows(tm, d)],
               out_specs=[_rows(tm, d), _rows(tm, d)],
               out_shape=[jax.ShapeDtypeStruct((t, d), bf16), jax.ShapeDtypeStruct((t, d), f32)],
               sem=("parallel",))(dc, ln_g, ln_b, w, b, res)


def _fetch_blocks(step, w_hbm, dst, sems, by_cols, layer=None):
    nb_w = w_hbm.shape[0]
    step_rows, step_cols = w_hbm.shape[-2], w_hbm.shape[-1]

    @pl.when(step == 0)
    def _():
        copies = []
        for j in range(nb_w):
            src = w_hbm.at[j] if layer is None else w_hbm.at[j, layer]
            if by_cols:
                part = dst.at[:, pl.ds(j * step_cols, step_cols)]
            else:
                part = dst.at[pl.ds(j * step_rows, step_rows), :]
            copies.append(pltpu.make_async_copy(src, part, sems.at[j]))
        for cp in copies:
            cp.start()
        for cp in copies:
            cp.wait()


def _mlp_fwd(h, g, w1g, w2g, name):
    t, d = h.shape
    nb_w, _, bn = w1g.shape
    ff = nb_w * bn
    tm = _tile(t, 256)

    def body(h_ref, g_ref, w1_hbm, w2_hbm, n_ref, f_ref, r_ref, o_ref, w1_ref, w2_ref, sem1, sem2):
        _fetch_blocks(pl.program_id(0), w1_hbm, w1_ref, sem1, True)
        _fetch_blocks(pl.program_id(0), w2_hbm, w2_ref, sem2, False)
        hv = h_ref[...]
        n, _, _ = _rms(hv, g_ref[...])
        nb = n.astype(bf16)
        n_ref[...] = nb
        f = jnp.dot(nb, w1_ref[...], preferred_element_type=f32)
        f_ref[...] = f.astype(bf16)
        rb = jnp.square(jnp.maximum(f, 0.0)).astype(bf16)
        r_ref[...] = rb
        o_ref[...] = hv + jnp.dot(rb, w2_ref[...], preferred_element_type=f32)

    return _pc(body, name=name, grid=(t // tm,),
               in_specs=[_rows(tm, d), _const((1, d)), ANY, ANY],
               out_specs=[_rows(tm, d), _rows(tm, ff), _rows(tm, ff), _rows(tm, d)],
               out_shape=[jax.ShapeDtypeStruct((t, d), bf16), jax.ShapeDtypeStruct((t, ff), bf16),
                          jax.ShapeDtypeStruct((t, ff), bf16), jax.ShapeDtypeStruct((t, d), f32)],
               scratch=[pltpu.VMEM((d, ff), bf16), pltpu.VMEM((ff, d), bf16),
                        pltpu.SemaphoreType.DMA((nb_w,)), pltpu.SemaphoreType.DMA((nb_w,))],
               sem=("arbitrary",))(h, g, w1g, w2g)


def _softplus(x):
    return jnp.maximum(x, 0.0) + jnp.log(1.0 + jnp.exp(-jnp.abs(x)))


def _gdn_in(h, g, w_main, w_ab, a_log_pad, dt_pad, n_heads):
    t, d = h.shape
    tm = _tile(t, 256)

    def body(h_ref, g_ref, wm_ref, wab_ref, al_ref, dt_ref, n_ref, qkv_ref, z_ref, ab_ref, gb_ref):
        n, _, _ = _rms(h_ref[...], g_ref[...])
        nb = n.astype(bf16)
        n_ref[...] = nb
        p = jnp.dot(nb, wm_ref[...], preferred_element_type=f32)
        qkv_ref[...] = p[:, :3 * d]
        z_ref[...] = p[:, 3 * d:]
        ab = jnp.dot(nb, wab_ref[...], preferred_element_type=f32)
        ab_ref[...] = ab
        lane = lax.broadcasted_iota(jnp.int32, ab.shape, 1)
        decay = -jnp.exp(al_ref[...]) * _softplus(ab + dt_ref[...])
        gb_ref[...] = jnp.where(lane < n_heads, decay, jnp.where(lane < 2 * n_heads, _sigmoid(ab), 0.0))

    return _pc(body, name="gdn_in", grid=(t // tm,),
               in_specs=[_rows(tm, d), _const((1, d)), _resident((d, 4 * d)), _resident((d, LANES)),
                         _const((1, LANES)), _const((1, LANES))],
               out_specs=[_rows(tm, d), _rows(tm, 3 * d), _rows(tm, d), _rows(tm, LANES), _rows(tm, LANES)],
               out_shape=[jax.ShapeDtypeStruct((t, d), bf16), jax.ShapeDtypeStruct((t, 3 * d), f32),
                          jax.ShapeDtypeStruct((t, d), f32), jax.ShapeDtypeStruct((t, LANES), f32),
                          jax.ShapeDtypeStruct((t, LANES), f32)],
               sem=("parallel",))(h, g, w_main, w_ab, a_log_pad, dt_pad)


def _gated_norm_mm_res(o, z, ng, w, res, n_heads):
    t, d = o.shape
    tm = _tile(t, 256)

    def body(o_ref, z_ref, ng_ref, w_ref, r_ref, on_ref, out_ref):
        for hd in range(n_heads):
            sl = slice(hd * LANES, (hd + 1) * LANES)
            rn, _, _ = _rms(o_ref[:, sl], ng_ref[...])
            zz = z_ref[:, sl]
            on_ref[:, sl] = (rn * (zz * _sigmoid(zz))).astype(bf16)
        out_ref[...] = r_ref[...] + jnp.dot(on_ref[...], w_ref[...], preferred_element_type=f32)

    return _pc(body, name="gated_norm_wout", grid=(t // tm,),
               in_specs=[_rows(tm, d), _rows(tm, d), _const((1, LANES)), _resident((d, d)), _rows(tm, d)],
               out_specs=[_rows(tm, d), _rows(tm, d)],
               out_shape=[jax.ShapeDtypeStruct((t, d), bf16), jax.ShapeDtypeStruct((t, d), f32)],
               sem=("parallel",))(o, z, ng, w, res)


def _loss_head(h, g, tgt):
    t, d = h.shape
    tm = _tile(t, 256)

    def body(h_ref, g_ref, t_ref, loss_ref, dh_ref, dg_ref):
        _acc_init(pl.program_id(0), loss_ref, dg_ref)
        gv = g_ref[...]
        y, xh, rstd = _rms(h_ref[...], gv)
        e = y - t_ref[...]
        loss_ref[...] += 0.5 * jnp.sum(jnp.mean(e * e, axis=-1, keepdims=True))
        dy = e * (1.0 / d)
        _acc_rows(dg_ref, dy * xh)
        dh_ref[...] = _rms_bwd(dy, xh, rstd, gv)

    return _pc(body, name="loss_head", grid=(t // tm,),
               in_specs=[_rows(tm, d), _const((1, d)), _rows(tm, d)],
               out_specs=[_const((SUBLANES, LANES)), _rows(tm, d), _const((SUBLANES, d))],
               out_shape=[jax.ShapeDtypeStruct((SUBLANES, LANES), f32), jax.ShapeDtypeStruct((t, d), f32),
                          jax.ShapeDtypeStruct((SUBLANES, d), f32)],
               sem=("arbitrary",))(h, g, tgt)


def _mlp_bwd(dho, h, g, fb, w1g, w2g, name):
    t, d = h.shape
    nb_w, _, bn = w1g.shape
    ff = nb_w * bn
    tm = _tile(t, 256)

    def body(do_ref, h_ref, g_ref, f_ref, w1_hbm, w2_hbm, df_ref, dh_ref, dg_ref, cs_ref, w1_ref, w2_ref, sem1, sem2):
        _fetch_blocks(pl.program_id(0), w1_hbm, w1_ref, sem1, True)
        _fetch_blocks(pl.program_id(0), w2_hbm, w2_ref, sem2, False)
        _acc_init(pl.program_id(0), dg_ref, cs_ref)
        do = do_ref[...]
        dr = lax.dot_general(do.astype(bf16), w2_ref[...], NT, preferred_element_type=f32)
        dfb = (dr * (2.0 * jnp.maximum(f_ref[...].astype(f32), 0.0))).astype(bf16)
        df_ref[...] = dfb
        dn = lax.dot_general(dfb, w1_ref[...], NT, preferred_element_type=f32)
        gv = g_ref[...]
        _, xh, rstd = _rms(h_ref[...], gv)
        _acc_rows(dg_ref, dn * xh)
        dh = do + _rms_bwd(dn, xh, rstd, gv)
        dh_ref[...] = dh
        _acc_rows(cs_ref, dh)

    return _pc(body, name=name, grid=(t // tm,),
               in_specs=[_rows(tm, d), _rows(tm, d), _const((1, d)), _rows(tm, ff), ANY, ANY],
               out_specs=[_rows(tm, ff), _rows(tm, d), _const((SUBLANES, d)), _const((SUBLANES, d))],
               out_shape=[jax.ShapeDtypeStruct((t, ff), bf16), jax.ShapeDtypeStruct((t, d), f32),
                          jax.ShapeDtypeStruct((SUBLANES, d), f32), jax.ShapeDtypeStruct((SUBLANES, d), f32)],
               scratch=[pltpu.VMEM((d, ff), bf16), pltpu.VMEM((ff, d), bf16),
                        pltpu.SemaphoreType.DMA((nb_w,)), pltpu.SemaphoreType.DMA((nb_w,))],
               sem=("arbitrary",))(dho, h, g, fb, w1g, w2g)


class _Tail:
    def __init__(self, fn, ins, outs):
        self.fn, self.ins, self.outs = fn, ins, outs


def _mm_nt(pairs, name, tail=None):
    t = pairs[0][0].shape[0]
    tm = _tile(t, 256)
    npair = len(pairs)
    in_specs, args, scratch, blocked = [], [], [], []
    k = None
    for dy, w in pairs:
        nn = dy.shape[1]
        if isinstance(w, tuple) and isinstance(w[0], str):
            w = w[1]
            k = w.shape[1]
            wspec = ANY
            blocked.append(True)
            scratch += [pltpu.VMEM((k, nn), bf16), pltpu.SemaphoreType.DMA((w.shape[0],))]
        elif isinstance(w, tuple):
            w, idx = w
            k = w.shape[0]
            wspec = pl.BlockSpec((k, nn), lambda *_, idx=idx: (0, idx), pipeline_mode=pl.Buffered(1))
            blocked.append(False)
        else:
            k = w.shape[0]
            wspec = _resident(w.shape)
            blocked.append(False)
        in_specs += [_rows(tm, nn), wspec]
        args += [dy, w]
    n_tin = len(tail.ins) if tail else 0
    n_out = len(tail.outs) if tail else 1
    if tail:
        for arr, kind in tail.ins:
            in_specs.append(_rows(tm, arr.shape[1]) if kind == "rows" else _const(arr.shape))
            args.append(arr)
        out_specs = [_rows(tm, c) if kind == "rows" else _const((SUBLANES, c)) for c, kind in tail.outs]
        out_shape = [jax.ShapeDtypeStruct((t, c) if kind == "rows" else (SUBLANES, c), f32) for c, kind in tail.outs]
    else:
        out_specs = _rows(tm, k)
        out_shape = jax.ShapeDtypeStruct((t, k), f32)

    def body(*refs):
        step = pl.program_id(0)
        tin = refs[2 * npair:2 * npair + n_tin]
        outs = refs[2 * npair + n_tin:2 * npair + n_tin + n_out]
        scr = list(refs[2 * npair + n_tin + n_out:])
        acc = None
        for p in range(npair):
            w_ref = refs[2 * p + 1]
            if blocked[p]:
                w_vmem, sems = scr.pop(0), scr.pop(0)
                _fetch_blocks(step, w_ref, w_vmem, sems, True)
                w_ref = w_vmem
            part = lax.dot_general(refs[2 * p][...].astype(bf16), w_ref[...], NT, preferred_element_type=f32)
            acc = part if acc is None else acc + part
        if tail is None:
            outs[0][...] = acc
        else:
            _acc_init(step, *[o for o, (_, kind) in zip(outs, tail.outs) if kind == "acc"])
            tail.fn(acc, tin, outs)

    sequential = tail is not None or any(blocked)
    return _pc(body, name=name, grid=(t // tm,), in_specs=in_specs, out_specs=out_specs, out_shape=out_shape,
               scratch=scratch, sem=("arbitrary",) if sequential else ("parallel",))(*args)


def _rms_bwd_tail(h, g, dres):
    def fn(dn, ins, outs):
        h_ref, g_ref, dr_ref = ins
        dh_ref, dg_ref = outs
        gv = g_ref[...]
        _, xh, rstd = _rms(h_ref[...], gv)
        _acc_rows(dg_ref, dn * xh)
        dh_ref[...] = dr_ref[...] + _rms_bwd(dn, xh, rstd, gv)

    d = h.shape[1]
    return _Tail(fn, [(h, "rows"), (g, "const"), (dres, "rows")], [(d, "rows"), (d, "acc")])


def _ln_silu_bwd_tail(dc, ln_g, ln_b):
    def fn(ds, ins, outs):
        x_ref, g_ref, b_ref = ins
        dx_ref, dg_ref, db_ref, cs_ref = outs
        x = x_ref[...]
        gv = g_ref[...]
        xc = x - jnp.mean(x, axis=-1, keepdims=True)
        rstd = lax.rsqrt(jnp.mean(xc * xc, axis=-1, keepdims=True) + NORM_EPS)
        xh = xc * rstd
        dln = ds * _silu_grad(xh * gv + b_ref[...])
        _acc_rows(dg_ref, dln * xh)
        _acc_rows(db_ref, dln)
        dxh = dln * gv
        dx = rstd * (dxh - jnp.mean(dxh, axis=-1, keepdims=True) - xh * jnp.mean(dxh * xh, axis=-1, keepdims=True))
        dx_ref[...] = dx
        _acc_rows(cs_ref, dx)

    d = dc.shape[1]
    return _Tail(fn, [(dc, "rows"), (ln_g, "const"), (ln_b, "const")], [(d, "rows"), (d, "acc"), (d, "acc"), (d, "acc")])


def _gated_norm_bwd_tail(o, z, ng, n_heads):
    def fn(don_all, ins, outs):
        o_ref, z_ref, ng_ref = ins
        do_ref, dz_ref, dng_ref = outs
        gv = ng_ref[...]
        for hd in range(n_heads):
            sl = slice(hd * LANES, (hd + 1) * LANES)
            rn, xh, rstd = _rms(o_ref[:, sl], gv)
            zz = z_ref[:, sl]
            don = don_all[:, sl]
            dz_ref[:, sl] = don * rn * _silu_grad(zz)
            drn = don * (zz * _sigmoid(zz))
            _acc_rows(dng_ref, drn * xh)
            do_ref[:, sl] = _rms_bwd(drn, xh, rstd, gv)

    d = o.shape[1]
    return _Tail(fn, [(o, "rows"), (z, "rows"), (ng, "const")], [(d, "rows"), (d, "rows"), (LANES, "acc")])


def _mm_tn_blocked(x, dy, name, out_dtype=f32):
    t, k = x.shape
    bn = dy.shape[1] // N_DEV
    tm = _tile(t, 512)
    jb = N_DEV
    while jb > 1 and k * jb * bn * 4 > 8 * 1024 * 1024:
        jb //= 2
    nt = t // tm

    def body(x_ref, dy_ref, o_ref, *acc):
        acc_ref = acc[0] if acc else o_ref
        _acc_init(pl.program_id(1), acc_ref)
        xt = x_ref[...].astype(bf16).T
        for jj in range(jb):
            acc_ref[jj] += jnp.dot(xt, dy_ref[:, jj * bn:(jj + 1) * bn].astype(bf16), preferred_element_type=f32)
        if acc:
            @pl.when(pl.program_id(1) == nt - 1)
            def _():
                o_ref[...] = acc_ref[...].astype(out_dtype)

    return _pc(body, name=name, grid=(N_DEV // jb, nt),
               in_specs=[pl.BlockSpec((tm, k), lambda j, i: (i, 0)), pl.BlockSpec((tm, jb * bn), lambda j, i: (i, j))],
               out_specs=pl.BlockSpec((jb, k, bn), lambda j, i: (j, 0, 0)),
               out_shape=jax.ShapeDtypeStruct((N_DEV, k, bn), out_dtype),
               scratch=[] if out_dtype == f32 else [pltpu.VMEM((jb, k, bn), f32)],
               sem=("parallel", "arbitrary"))(x, dy)


def _mm_tn(x, dy, name, out_dtype=f32):
    t, k = x.shape
    n = dy.shape[1]
    tm = _tile(t, 512)
    cap = max(LANES, (2 * 1024 * 1024) // k)
    tn = n
    if n > cap:
        tn = max(c for c in range(LANES, cap + 1, LANES) if n % c == 0)
    nt = t // tm

    def body(x_ref, dy_ref, o_ref, *acc):
        acc_ref = acc[0] if acc else o_ref
        _acc_init(pl.program_id(1), acc_ref)
        acc_ref[...] += lax.dot_general(x_ref[...].astype(bf16), dy_ref[...].astype(bf16), TN, preferred_element_type=f32)
        if acc:
            @pl.when(pl.program_id(1) == nt - 1)
            def _():
                o_ref[...] = acc_ref[...].astype(out_dtype)

    return _pc(body, name=name, grid=(n // tn, nt),
               in_specs=[pl.BlockSpec((tm, k), lambda j, i: (i, 0)), pl.BlockSpec((tm, tn), lambda j, i: (i, j))],
               out_specs=pl.BlockSpec((k, tn), lambda j, i: (0, j)),
               out_shape=jax.ShapeDtypeStruct((k, n), out_dtype),
               scratch=[] if out_dtype == f32 else [pltpu.VMEM((k, tn), f32)],
               sem=("parallel", "arbitrary"))(x, dy)


def _gates_bwd(dgb, ab, a_log_pad, dt_pad, n_heads):
    t = ab.shape[0]
    tm = _tile(t, 256)

    def body(dgb_ref, ab_ref, al_ref, dt_ref, dab_ref, dal_ref, ddt_ref):
        _acc_init(pl.program_id(0), dal_ref, ddt_ref)
        ab = ab_ref[...]
        dgb = dgb_ref[...]
        lane = lax.broadcasted_iota(jnp.int32, ab.shape, 1)
        is_a = lane < n_heads
        is_b = jnp.logical_and(lane >= n_heads, lane < 2 * n_heads)
        xa = ab + dt_ref[...]
        neg_a = -jnp.exp(al_ref[...])
        dg_da = neg_a * _sigmoid(xa)
        beta = _sigmoid(ab)
        da = jnp.where(is_a, dgb * dg_da, 0.0)
        dab_ref[...] = da + jnp.where(is_b, dgb * beta * (1.0 - beta), 0.0)
        _acc_rows(dal_ref, jnp.where(is_a, dgb * neg_a * _softplus(xa), 0.0))
        _acc_rows(ddt_ref, da)

    return _pc(body, name="gates_bwd", grid=(t // tm,),
               in_specs=[_rows(tm, LANES), _rows(tm, LANES), _const((1, LANES)), _const((1, LANES))],
               out_specs=[_rows(tm, LANES), _const((SUBLANES, LANES)), _const((SUBLANES, LANES))],
               out_shape=[jax.ShapeDtypeStruct((t, LANES), f32), jax.ShapeDtypeStruct((SUBLANES, LANES), f32),
                          jax.ShapeDtypeStruct((SUBLANES, LANES), f32)],
               sem=("arbitrary",))(dgb, ab, a_log_pad, dt_pad)


def _glu_bwd(dgl, ub):
    t, d = dgl.shape
    tm = _tile(t, 256)

    def body(dgl_ref, u_ref, du_ref, cs_ref):
        _acc_init(pl.program_id(0), cs_ref)
        dgl = dgl_ref[...]
        a = u_ref[:, :d].astype(f32)
        sb = _sigmoid(u_ref[:, d:].astype(f32))
        da = dgl * sb
        db = dgl * a * sb * (1.0 - sb)
        du_ref[:, :d] = da.astype(bf16)
        du_ref[:, d:] = db.astype(bf16)
        cs_ref[0:1, :d] += jnp.sum(da, axis=0, keepdims=True)
        cs_ref[0:1, d:] += jnp.sum(db, axis=0, keepdims=True)

    return _pc(body, name="glu_bwd", grid=(t // tm,),
               in_specs=[_rows(tm, d), _rows(tm, 2 * d)],
               out_specs=[_rows(tm, 2 * d), _const((SUBLANES, 2 * d))],
               out_shape=[jax.ShapeDtypeStruct((t, 2 * d), bf16), jax.ShapeDtypeStruct((SUBLANES, 2 * d), f32)],
               sem=("arbitrary",))(dgl, ub)


def _conv_rows(s):
    return 256 if s % 256 == 0 else s


def _conv_tap_sum(pad_ref, w_ref, base, rows, width):
    acc = jnp.zeros((rows, LANES), f32)
    for j in range(width):
        acc = acc + w_ref[j:j + 1, :] * pad_ref[pl.ds(base + CONV_PAD - (width - 1) + j, rows), :]
    return acc


def _l2_silu_post(c, j, n_heads, scale):
    a = c * _sigmoid(c)
    r = lax.rsqrt(jnp.sum(a * a, axis=-1, keepdims=True) + L2_EPS)
    mult = jnp.where(j < n_heads, r * scale, jnp.where(j < 2 * n_heads, r, 1.0))
    return a, r, a * mult


def _dwconv_fwd(x, w, b, name, qk_heads=None):
    bl, s, cn = x.shape
    width = w.shape[0]
    rows = _conv_rows(s)
    scale = float(LANES) ** -0.5

    def body(x_ref, w_ref, b_ref, o_ref, pad_ref):
        j = pl.program_id(1)
        pad_ref[0:CONV_PAD, :] = jnp.zeros((CONV_PAD, LANES), f32)
        pad_ref[CONV_PAD:, :] = x_ref[0]

        def step(i, carry):
            base = pl.multiple_of(i * rows, rows)
            acc = _conv_tap_sum(pad_ref, w_ref, base, rows, width)
            if qk_heads is None:
                acc = acc + b_ref[...]
            else:
                _, _, acc = _l2_silu_post(acc, j, qk_heads, scale)
            o_ref[0, pl.ds(base, rows), :] = acc
            return carry

        lax.fori_loop(0, s // rows, step, 0)

    return _pc(body, name=name, grid=(bl, cn // LANES),
               in_specs=[pl.BlockSpec((1, s, LANES), lambda bi, j: (bi, 0, j)),
                         pl.BlockSpec((width, LANES), lambda bi, j: (0, j)),
                         pl.BlockSpec((1, LANES), lambda bi, j: (0, j))],
               out_specs=pl.BlockSpec((1, s, LANES), lambda bi, j: (bi, 0, j)),
               out_shape=jax.ShapeDtypeStruct((bl, s, cn), f32),
               scratch=[pltpu.VMEM((s + CONV_PAD, LANES), f32)],
               sem=("parallel", "parallel"))(x, w, b)


def _dwconv_bwd(x, dys, w, name, qk_heads=None):
    bl, s, cn = x.shape
    width = w.shape[0]
    wp = -(-width // SUBLANES) * SUBLANES
    rows = _conv_rows(s)
    scale = float(LANES) ** -0.5
    nblk = s // rows
    ndy = len(dys)

    def body(*refs):
        x_ref, w_ref = refs[0], refs[1]
        dy_refs = refs[2:2 + ndy]
        dx_ref, dw_ref, db_ref, xpad, dypad, acc = refs[2 + ndy:]
        j = pl.program_id(0)
        bi = pl.program_id(1)
        _acc_init(bi, acc, db_ref)
        xpad[0:CONV_PAD, :] = jnp.zeros((CONV_PAD, LANES), f32)
        xpad[CONV_PAD:, :] = x_ref[0]
        dypad[s:, :] = jnp.zeros((CONV_PAD, LANES), f32)
        if qk_heads is None:
            dypad[0:s, :] = dy_refs[0][0]
        else:
            def pre(i, carry):
                base = pl.multiple_of(i * rows, rows)
                c = _conv_tap_sum(xpad, w_ref, base, rows, width)
                a, r, _ = _l2_silu_post(c, j, qk_heads, scale)
                dq = dy_refs[0][0, pl.ds(base, rows), :]
                dk = dy_refs[1][0, pl.ds(base, rows), :]
                dv = dy_refs[2][0, pl.ds(base, rows), :]
                dy = jnp.where(j < qk_heads, dq * scale, jnp.where(j < 2 * qk_heads, dk, dv))
                da_l2 = r * (dy - a * (r * r) * jnp.sum(a * dy, axis=-1, keepdims=True))
                da = jnp.where(j < 2 * qk_heads, da_l2, dy)
                dypad[pl.ds(base, rows), :] = da * _silu_grad(c)
                return carry

            lax.fori_loop(0, nblk, pre, 0)

        def step(i, carry):
            base = pl.multiple_of(i * rows, rows)
            dxa = jnp.zeros((rows, LANES), f32)
            for jj in range(width):
                dxa = dxa + w_ref[jj:jj + 1, :] * dypad[pl.ds(base + (width - 1) - jj, rows), :]
            dx_ref[0, pl.ds(base, rows), :] = dxa
            dyc = dypad[pl.ds(base, rows), :]
            db_ref[...] += dyc.reshape(rows // SUBLANES, SUBLANES, LANES).sum(axis=0)
            for jj in range(width):
                prod = dyc * xpad[pl.ds(base + CONV_PAD - (width - 1) + jj, rows), :]
                acc[jj * SUBLANES:(jj + 1) * SUBLANES, :] += prod.reshape(rows // SUBLANES, SUBLANES, LANES).sum(axis=0)
            return carry

        lax.fori_loop(0, nblk, step, 0)

        @pl.when(bi == bl - 1)
        def _():
            dw_ref[...] = jnp.zeros((wp, LANES), f32)
            for jj in range(width):
                dw_ref[jj:jj + 1, :] = jnp.sum(acc[jj * SUBLANES:(jj + 1) * SUBLANES, :], axis=0, keepdims=True)

    if qk_heads is None:
        dy_specs = [pl.BlockSpec((1, s, LANES), lambda j, bi: (bi, 0, j))]
    else:
        hh = qk_heads
        dy_specs = [pl.BlockSpec((1, s, LANES), lambda j, bi: (bi * hh + jnp.minimum(j, hh - 1), 0, 0)),
                    pl.BlockSpec((1, s, LANES), lambda j, bi: (bi * hh + jnp.clip(j - hh, 0, hh - 1), 0, 0)),
                    pl.BlockSpec((1, s, LANES), lambda j, bi: (bi * hh + jnp.clip(j - 2 * hh, 0, hh - 1), 0, 0))]
    return _pc(body, name=name, grid=(cn // LANES, bl),
               in_specs=[pl.BlockSpec((1, s, LANES), lambda j, bi: (bi, 0, j)),
                         pl.BlockSpec((width, LANES), lambda j, bi: (0, j))] + dy_specs,
               out_specs=[pl.BlockSpec((1, s, LANES), lambda j, bi: (bi, 0, j)),
                          pl.BlockSpec((wp, LANES), lambda j, bi: (0, j)),
                          pl.BlockSpec((SUBLANES, LANES), lambda j, bi: (0, j))],
               out_shape=[jax.ShapeDtypeStruct((bl, s, cn), f32), jax.ShapeDtypeStruct((wp, cn), f32),
                          jax.ShapeDtypeStruct((SUBLANES, cn), f32)],
               scratch=[pltpu.VMEM((s + CONV_PAD, LANES), f32), pltpu.VMEM((s + CONV_PAD, LANES), f32),
                        pltpu.VMEM((width * SUBLANES, LANES), f32)],
               sem=("parallel", "arbitrary"))(x, w, *dys)


def _group_rows(s):
    for rows in (256, 128):
        if s % rows == 0:
            return rows
    return CHUNK


def _group_masks(rows):
    r = lax.broadcasted_iota(jnp.int32, (rows, rows), 0)
    c = lax.broadcasted_iota(jnp.int32, (rows, rows), 1)
    same_chunk = (r >> 6) == (c >> 6)
    return r, c, same_chunk


def _decay(gc_col, gc_row, causal):
    return jnp.exp(jnp.where(causal, gc_col - gc_row, NEG))


def _inv_unit_lower(a, r, c):
    return _inv_unit_lower_many([a], r, c)[0]


def _inv_unit_lower_many(mats, r, c):
    eye = (r == c).astype(f32)
    same16 = (r >> 4) == (c >> 4)
    ads = [jnp.where(same16, a, 0.0) for a in mats]
    aos = [a - ad for a, ad in zip(mats, ads)]
    xs = ads
    tds = [eye - x for x in xs]
    for _ in range(3):
        xs = [_dot(x, x) for x in xs]
        tds = [td + _dot(td, x) for td, x in zip(tds, xs)]
    bs = [_dot(td, ao) for td, ao in zip(tds, aos)]
    b2s = [_dot(b, b) for b in bs]
    b3s = [_dot(b, b2) for b, b2 in zip(bs, b2s)]
    return [_dot(eye - b + b2 - b3, td) for b, b2, b3, td in zip(bs, b2s, b3s, tds)]


def _lane_cumsum(x, y, reverse, name):
    rr = x.shape[0]

    def body(x_ref, y_ref, o_ref):
        i = lax.broadcasted_iota(jnp.int32, (CHUNK, CHUNK), 0)
        j = lax.broadcasted_iota(jnp.int32, (CHUNK, CHUNK), 1)
        tri = ((i >= j) if reverse else (i <= j)).astype(f32)
        o_ref[...] = jnp.dot(x_ref[...] + y_ref[...], tri, precision=HI, preferred_element_type=f32)

    spec = pl.BlockSpec((rr, CHUNK), lambda: (0, 0))
    return pl.pallas_call(body, name=name, in_specs=[spec, spec], out_specs=spec,
                          out_shape=jax.ShapeDtypeStruct((rr, CHUNK), f32))(x, y)


def _gdn_specs(n_heads, nblk, rows, hp=1, rev=False):
    def blk(n):
        return nblk - 1 - n if rev else n

    def qkv(off):
        return pl.BlockSpec((rows, hp * LANES), lambda g, n: (
            lax.div(g * hp, n_heads) * nblk + blk(n), lax.div(off * n_heads + lax.rem(g * hp, n_heads), hp)))

    def per_head(last, mult=1):
        return pl.BlockSpec((hp, rows * mult, last), lambda g, n: (g, blk(n), 0))

    def row_vec():
        return pl.BlockSpec((hp, 1, 1, rows), lambda g, n: (g, blk(n), 0, 0))

    return qkv, per_head, row_vec


def _gdn_prep(qkv, cols, grow, bl, s, n_heads):
    rows = _group_rows(s)
    nblk = s // rows
    bh_n = bl * n_heads
    hp = PREP_HEADS_PER_STEP if n_heads % PREP_HEADS_PER_STEP == 0 else 1
    qkv_spec, ph, rv = _gdn_specs(n_heads, nblk, rows, hp)

    def body(k_ref, v_ref, cols_ref, grow_ref, u_ref, w_ref, t_ref):
        r, c, same = _group_masks(rows)
        causal = jnp.logical_and(same, r >= c)
        mats, rhs = [], []
        for h in range(hp):
            hs = slice(h * LANES, (h + 1) * LANES)
            k = k_ref[:, hs]
            gc = cols_ref[h, :, 0:1]
            beta = cols_ref[h, :, 1:2]
            kb = k * beta
            mats.append(jnp.where(r > c, _dot_nt(kb, k) * _decay(gc, grow_ref[h, 0], causal), 0.0))
            rhs.append((v_ref[:, hs] * beta, kb * jnp.exp(gc)))
        for h, tm in enumerate(_inv_unit_lower_many(mats, r, c)):
            tb = tm.astype(bf16)
            u_ref[h] = jnp.dot(tb, rhs[h][0].astype(bf16), preferred_element_type=f32)
            w_ref[h] = jnp.dot(tb, rhs[h][1].astype(bf16), preferred_element_type=f32)
            t_ref[h] = tb

    return _pc(body, name="gdn_prep", grid=(bh_n // hp, nblk),
               in_specs=[qkv_spec(1), qkv_spec(2), ph(2), rv()],
               out_specs=[ph(LANES), ph(LANES), ph(rows)],
               out_shape=[jax.ShapeDtypeStruct((bh_n, s, LANES), f32), jax.ShapeDtypeStruct((bh_n, s, LANES), f32),
                          jax.ShapeDtypeStruct((bh_n, s, rows), bf16)],
               sem=("parallel", "parallel"))(qkv, qkv, cols, grow)


def _gdn_scan(qkv, u, w, cols, grow, bl, s, n_heads):
    rows = _group_rows(s)
    g_chunks = rows // CHUNK
    nblk = s // rows
    bh_n = bl * n_heads
    d = n_heads * LANES
    hp = HEADS_PER_STEP if n_heads % HEADS_PER_STEP == 0 else 1
    qkv_spec, ph, rv = _gdn_specs(n_heads, nblk, rows, hp)

    def body(q_ref, k_ref, u_ref, w_ref, cols_ref, grow_ref, o_ref, vn_ref, ss_ref, s_scr):
        _acc_init(pl.program_id(1), s_scr)
        r, c, same = _group_masks(rows)
        causal = jnp.logical_and(same, r >= c)
        qs, ks, gcs, qgs, ps = [], [], [], [], []
        for h in range(hp):
            hs = slice(h * LANES, (h + 1) * LANES)
            q, k, gc = q_ref[:, hs], k_ref[:, hs], cols_ref[h, :, 0:1]
            qs.append(q)
            ks.append(k)
            gcs.append(gc)
            qgs.append(q * jnp.exp(gc))
            ps.append(_dot_nt(q, k) * _decay(gc, grow_ref[h, 0], causal))
        st = [s_scr[h] for h in range(hp)]
        o_state = [[] for _ in range(hp)]
        vns = [[] for _ in range(hp)]
        for ci in range(g_chunks):
            sl = slice(ci * CHUNK, (ci + 1) * CHUNK)
            for h in range(hp):
                ss_ref[h, ci * LANES:(ci + 1) * LANES, :] = st[h]
                vn = u_ref[h, sl, :] - _dot(w_ref[h, sl, :], st[h])
                vns[h].append(vn)
                o_state[h].append(_dot(qgs[h][sl], st[h]))
                gc = gcs[h][sl]
                g_last = gc[CHUNK - 1:CHUNK, :]
                st[h] = jnp.exp(g_last) * st[h] + _dot_tn(ks[h][sl] * jnp.exp(g_last - gc), vn)
        for h in range(hp):
            s_scr[h] = st[h]
            vn_all = jnp.concatenate(vns[h], axis=0)
            vn_ref[h] = vn_all
            o_ref[:, h * LANES:(h + 1) * LANES] = jnp.concatenate(o_state[h], axis=0) + _dot(ps[h], vn_all)

    return _pc(body, name="gdn_scan", grid=(bh_n // hp, nblk),
               in_specs=[qkv_spec(0), qkv_spec(1), ph(LANES), ph(LANES), ph(2), rv()],
               out_specs=[qkv_spec(0), ph(LANES), ph(LANES, mult=LANES // CHUNK)],
               out_shape=[jax.ShapeDtypeStruct((bl * s, d), f32), jax.ShapeDtypeStruct((bh_n, s, LANES), f32),
                          jax.ShapeDtypeStruct((bh_n, (s // CHUNK) * LANES, LANES), f32)],
               scratch=[pltpu.VMEM((hp, LANES, LANES), f32)],
               sem=("parallel", "arbitrary"))(qkv, qkv, u, w, cols, grow)


def _gdn_scan_bwd(do, qkv, w, vn, cols, grow, ss, bl, s, n_heads):
    rows = _group_rows(s)
    g_chunks = rows // CHUNK
    nblk = s // rows
    bh_n = bl * n_heads
    hp = HEADS_PER_STEP if n_heads % HEADS_PER_STEP == 0 else 1
    qkv_spec, ph, rv = _gdn_specs(n_heads, nblk, rows, hp, rev=True)

    def body(do_ref, q_ref, k_ref, w_ref, vn_ref, cols_ref, grow_ref, ss_ref,
             du_ref, dw_ref, dq_ref, dk_ref, dcol_ref, drow_ref, ds_scr):
        _acc_init(pl.program_id(1), ds_scr)
        r, c, same = _group_masks(rows)
        causal = jnp.logical_and(same, r >= c)
        last_row = lax.broadcasted_iota(jnp.int32, (CHUNK, 1), 0) == CHUNK - 1
        pre = []
        for h in range(hp):
            hs = slice(h * LANES, (h + 1) * LANES)
            do, q, k = do_ref[:, hs], q_ref[:, hs], k_ref[:, hs]
            vn = vn_ref[h]
            gc = cols_ref[h, :, 0:1]
            dmat = _decay(gc, grow_ref[h, 0], causal)
            gam = jnp.exp(gc)
            qk = _dot_nt(q, k)
            dpd = _dot_nt(do, vn) * dmat
            ep = dpd * qk
            drow_ref[h, 0] = -jnp.sum(ep, axis=0, keepdims=True)
            pre.append(dict(do=do, k=k, vn=vn, gc=gc, gam=gam, qg=q * gam,
                            dvn_intra=_dot_tn(qk * dmat, do), dq_intra=_dot(dpd, k), dk_intra=_dot_tn(dpd, q),
                            ep_rows=jnp.sum(ep, axis=-1, keepdims=True)))
        ds = [ds_scr[h] for h in range(hp)]
        for ci in reversed(range(g_chunks)):
            sl = slice(ci * CHUNK, (ci + 1) * CHUNK)
            for h in range(hp):
                pr = pre[h]
                st = ss_ref[h, ci * LANES:(ci + 1) * LANES, :]
                gc = pr["gc"][sl]
                g_last = gc[CHUNK - 1:CHUNK, :]
                gam_last = jnp.exp(g_last)
                kd_scale = jnp.exp(g_last - gc)
                kdec = pr["k"][sl] * kd_scale
                do_c = pr["do"][sl]
                qg_c = pr["qg"][sl]
                dvn = pr["dvn_intra"][sl] + _dot(kdec, ds[h])
                dkdec = _dot_nt(pr["vn"][sl], ds[h])
                du_ref[h, sl, :] = dvn
                dw_ref[h, sl, :] = -_dot_nt(dvn, st)
                dqg = _dot_nt(do_c, st)
                dq_ref[h, sl, :] = dqg * pr["gam"][sl] + pr["dq_intra"][sl]
                dk_ref[h, sl, :] = pr["dk_intra"][sl] + dkdec * kd_scale
                kd_rows = jnp.sum(dkdec * kdec, axis=-1, keepdims=True)
                extra = jnp.sum(kd_rows) + gam_last * jnp.sum(st * ds[h])
                dcol_ref[h, sl, :] = (jnp.sum(dqg * qg_c, axis=-1, keepdims=True) + pr["ep_rows"][sl] - kd_rows
                                      + jnp.where(last_row, extra, 0.0))
                ds[h] = gam_last * ds[h] + _dot_tn(qg_c, do_c) - _dot_tn(w_ref[h, sl, :], dvn)
        for h in range(hp):
            ds_scr[h] = ds[h]

    return _pc(body, name="gdn_scan_bwd", grid=(bh_n // hp, nblk),
               in_specs=[qkv_spec(0), qkv_spec(0), qkv_spec(1), ph(LANES), ph(LANES), ph(2), rv(),
                         ph(LANES, mult=LANES // CHUNK)],
               out_specs=[ph(LANES), ph(LANES), ph(LANES), ph(LANES), ph(1), rv()],
               out_shape=[jax.ShapeDtypeStruct((bh_n, s, LANES), f32)] * 4
               + [jax.ShapeDtypeStruct((bh_n, s, 1), f32), jax.ShapeDtypeStruct((bh_n, nblk, 1, rows), f32)],
               scratch=[pltpu.VMEM((hp, LANES, LANES), f32)],
               sem=("parallel", "arbitrary"))(do, qkv, qkv, w, vn, cols, grow, ss)


def _gdn_prep_bwd(qkv, cols, grow, tmat, du, dw, dk_scan, dcol_scan, drow_scan, bl, s, n_heads):
    rows = _group_rows(s)
    nblk = s // rows
    bh_n = bl * n_heads
    hp = PREP_HEADS_PER_STEP if n_heads % PREP_HEADS_PER_STEP == 0 else 1
    qkv_spec, ph, rv = _gdn_specs(n_heads, nblk, rows, hp)

    def body(k_ref, v_ref, cols_ref, grow_ref, t_ref, du_ref, dw_ref, dks_ref, dcs_ref, drs_ref,
             dk_ref, dv_ref, dcols_ref, drow_ref):
        r, c, same = _group_masks(rows)
        causal = jnp.logical_and(same, r >= c)
        hh = range(hp)
        ks = [k_ref[:, h * LANES:(h + 1) * LANES] for h in hh]
        vs = [v_ref[:, h * LANES:(h + 1) * LANES] for h in hh]
        gcs = [cols_ref[h, :, 0:1] for h in hh]
        betas = [cols_ref[h, :, 1:2] for h in hh]
        tms = [t_ref[h] for h in hh]
        dus = [du_ref[h] for h in hh]
        dws = [dw_ref[h] for h in hh]
        gams = [jnp.exp(gc) for gc in gcs]
        kbs = [k * b for k, b in zip(ks, betas)]
        kbgs = [kb * g for kb, g in zip(kbs, gams)]
        dts = [jnp.where(same, _dot_nt(dus[h], vs[h] * betas[h]) + _dot_nt(dws[h], kbgs[h]), 0.0) for h in hh]
        dvbs = [_dot_tn(tms[h], dus[h]) for h in hh]
        dkbgs = [_dot_tn(tms[h], dws[h]) for h in hh]
        kks = [_dot_nt(kbs[h], ks[h]) for h in hh]
        inner = [_dot_nt(dts[h], tms[h]) for h in hh]
        dads = [jnp.where(r > c, -_dot_tn(tms[h], inner[h]), 0.0) * _decay(gcs[h], grow_ref[h, 0], causal) for h in hh]
        dkbs = [dkbgs[h] * gams[h] + _dot(dads[h], ks[h]) for h in hh]
        dk2 = [_dot_tn(dads[h], kbs[h]) for h in hh]
        for h in hh:
            dk_ref[h] = dks_ref[h] + dk2[h] + dkbs[h] * betas[h]
            dv_ref[h] = dvbs[h] * betas[h]
            ea = dads[h] * kks[h]
            dcols_ref[h, :, 0:1] = (dcs_ref[h] + jnp.sum(dkbgs[h] * kbgs[h], axis=-1, keepdims=True)
                                    + jnp.sum(ea, axis=-1, keepdims=True))
            dcols_ref[h, :, 1:2] = (jnp.sum(dvbs[h] * vs[h], axis=-1, keepdims=True)
                                    + jnp.sum(dkbs[h] * ks[h], axis=-1, keepdims=True))
            drow_ref[h, 0] = drs_ref[h, 0] - jnp.sum(ea, axis=0, keepdims=True)

    return _pc(body, name="gdn_prep_bwd", grid=(bh_n // hp, nblk),
               in_specs=[qkv_spec(1), qkv_spec(2), ph(2), rv(), ph(rows), ph(LANES), ph(LANES), ph(LANES), ph(1), rv()],
               out_specs=[ph(LANES), ph(LANES), ph(2), rv()],
               out_shape=[jax.ShapeDtypeStruct((bh_n, s, LANES), f32), jax.ShapeDtypeStruct((bh_n, s, LANES), f32),
                          jax.ShapeDtypeStruct((bh_n, s, 2), f32), jax.ShapeDtypeStruct((bh_n, nblk, 1, rows), f32)],
               sem=("parallel", "parallel"))(qkv, qkv, cols, grow, tmat, du, dw, dk_scan, dcol_scan, drow_scan)


def _row(v):
    return v.reshape(1, -1).astype(f32)


def _pad_lanes(v):
    v = v.reshape(1, -1).astype(f32)
    return jnp.pad(v, ((0, 0), (0, LANES - v.shape[1])))


def _local_step(x, tgt, p, need, emit):
    bl, s, d = x.shape
    t = bl * s
    n_heads = p["gdn_a_log"].shape[-1]
    assert d == n_heads * LANES and s % CHUNK == 0
    x2 = x.reshape(t, d)
    tgt2 = tgt.reshape(t, d)
    gr = {}

    n0, ub, gl = _pw1_glu(x2, _row(p["norm_mix_g"][0]), p["cv_w_pw1"], _row(p["cv_b_pw1"]))
    dc = _dwconv_fwd(gl.reshape(bl, s, d), p["cv_w_dw"], _row(p["cv_b_dw"]), "dwconv_fwd").reshape(t, d)
    sb, h1 = _ln_silu_mm_res(dc, _row(p["cv_ln_g"]), _row(p["cv_ln_b"]), p["cv_w_pw2"], _row(p["cv_b_pw2"]), x2)
    m0 = need("mlp0", h1)
    n1, f0, r0, h2 = _mlp_fwd(h1, _row(p["norm_ffn_g"][0]), m0["w1"], m0["w2"], "mlp_fwd0")

    gd = need("gdn", h2)
    w_in = gd["w_in"]
    w_ab = jnp.pad(w_in[:, 4 * d:], ((0, 0), (0, LANES - 2 * n_heads)))
    a_log_pad = _pad_lanes(p["gdn_a_log"])
    dt_pad = _pad_lanes(p["gdn_dt_bias"])
    n2, qkv_pre, z, ab, gbeta = _gdn_in(h2, _row(p["norm_mix_g"][1]), w_in, w_ab, a_log_pad, dt_pad, n_heads)
    zero_bias = jnp.zeros((1, 3 * d), f32)
    qkv = _dwconv_fwd(qkv_pre.reshape(bl, s, 3 * d), p["gdn_conv_w"], zero_bias, "sconv_fwd", qk_heads=n_heads).reshape(t, 3 * d)
    bh_n, rows = bl * n_heads, _group_rows(s)
    gates = gbeta[:, :2 * n_heads].reshape(bl, s, 2, n_heads).transpose(2, 0, 3, 1).reshape(2, bh_n, s)
    g_lanes = gates[0].reshape(bh_n * (s // CHUNK), CHUNK)
    gc_lanes = _lane_cumsum(g_lanes, jnp.zeros_like(g_lanes), False, "gdn_gate_cumsum")
    grow = gc_lanes.reshape(bh_n, s // rows, 1, rows)
    cols = jnp.stack([gc_lanes.reshape(bh_n, s), gates[1]], axis=-1)
    u, w, tmat = _gdn_prep(qkv, cols, grow, bl, s, n_heads)
    o, vn, ss = _gdn_scan(qkv, u, w, cols, grow, bl, s, n_heads)
    onb, h3 = _gated_norm_mm_res(o, z, _row(p["gdn_norm_g"]), gd["w_out"], h2, n_heads)
    m1 = need("mlp1", h3)
    n3, f1, r1, h4 = _mlp_fwd(h3, _row(p["norm_ffn_g"][1]), m1["w1"], m1["w2"], "mlp_fwd1")
    loss_acc, dh4, dgf = _loss_head(h4, _row(p["final_norm_g"]), tgt2)
    loss = loss_acc[0, 0]
    gr["final_norm_g"] = dgf[0]

    df1, dh3, dg_ffn1, _ = _mlp_bwd(dh4, h3, _row(p["norm_ffn_g"][1]), f1, m1["w1"], m1["w2"], "mlp_bwd1")
    dw2_1 = _mm_tn(r1, dh4, "dw_mlp2_1", bf16)
    dw1_1 = _mm_tn_blocked(n3, df1, "dw_mlp1_1", bf16)

    dw_out = _mm_tn(onb, dh3, "dw_gdn_out", bf16)
    tie = emit("late", {"mlp_w2_1": dw2_1, "mlp_w1_1": dw1_1, "gdn_w_out": dw_out})
    do, dz, dng = _mm_nt([(dh3, gd["w_out"])], "dx_gdn_out",
                         _gated_norm_bwd_tail(o, z, _row(p["gdn_norm_g"]) + tie, n_heads))
    gr["gdn_norm_g"] = dng[0]
    du, dw_, dq, dk_scan, dcol_scan, drow_scan = _gdn_scan_bwd(do, qkv, w, vn, cols, grow, ss, bl, s, n_heads)
    dk, dv, dcols, drow = _gdn_prep_bwd(qkv, cols, grow, tmat, du, dw_, dk_scan, dcol_scan, drow_scan, bl, s, n_heads)
    dg_lanes = _lane_cumsum(dcols[..., 0].reshape(g_lanes.shape), drow.reshape(g_lanes.shape), True, "gdn_gate_cumsum_bwd")
    dgb2 = jnp.stack([dg_lanes.reshape(bh_n, s), dcols[..., 1]], axis=-1)
    dqkv_pre, dconv_w, _ = _dwconv_bwd(qkv_pre.reshape(bl, s, 3 * d), [dq, dk, dv], p["gdn_conv_w"], "sconv_bwd", qk_heads=n_heads)
    gr["gdn_conv_w"] = dconv_w[:p["gdn_conv_w"].shape[0]]
    dgb = dgb2.reshape(bl, n_heads, s, 2).transpose(0, 2, 3, 1).reshape(t, 2 * n_heads)
    dgb = jnp.pad(dgb, ((0, 0), (0, LANES - 2 * n_heads)))
    dab, dal, ddt = _gates_bwd(dgb, ab, a_log_pad, dt_pad, n_heads)
    gr["gdn_a_log"] = dal[0, :n_heads]
    gr["gdn_dt_bias"] = ddt[0, :n_heads]
    dqkv2 = dqkv_pre.reshape(t, 3 * d)
    dw_in = jnp.concatenate(
        [_mm_tn(n2, dqkv2, "dw_gdn_in_qkv"), _mm_tn(n2, dz, "dw_gdn_in_z"), _mm_tn(n2, dab, "dw_gdn_in_ab")[:, :2 * n_heads]], axis=1)
    tie = emit("gdn_in", {"gdn_w_in": dw_in})
    dh2, dg_mix1 = _mm_nt([(dqkv2, (w_in, 0)), (dz, (w_in, 3)), (dab, w_ab)], "dx_gdn_in",
                          _rms_bwd_tail(h2, _row(p["norm_mix_g"][1]) + tie, dh3))

    df0, dh1, dg_ffn0, cs_h1 = _mlp_bwd(dh2, h1, _row(p["norm_ffn_g"][0]), f0, m0["w1"], m0["w2"], "mlp_bwd0")
    dw2_0 = _mm_tn(r0, dh2, "dw_mlp2_0", bf16)
    dw1_0 = _mm_tn_blocked(n1, df0, "dw_mlp1_0", bf16)
    dw_pw2 = _mm_tn(sb, dh1, "dw_pw2", bf16)
    tie = emit("mlp0", {"mlp_w2_0": dw2_0, "mlp_w1_0": dw1_0, "cv_w_pw2": dw_pw2})
    gr["norm_ffn_g"] = jnp.stack([dg_ffn0[0], dg_ffn1[0]])

    gr["cv_b_pw2"] = cs_h1[0]
    ddc, dlng, dlnb, cs_dc = _mm_nt([(dh1, p["cv_w_pw2"])], "dx_pw2",
                                    _ln_silu_bwd_tail(dc, _row(p["cv_ln_g"]) + tie, _row(p["cv_ln_b"])))
    gr["cv_ln_g"] = dlng[0]
    gr["cv_ln_b"] = dlnb[0]
    gr["cv_b_dw"] = cs_dc[0]
    dgl, dw_dw, _ = _dwconv_bwd(gl.reshape(bl, s, d), [ddc.reshape(bl, s, d)], p["cv_w_dw"], "dwconv_bwd")
    gr["cv_w_dw"] = dw_dw[:p["cv_w_dw"].shape[0]]
    dub, cs_u = _glu_bwd(dgl.reshape(t, d), ub)
    gr["cv_b_pw1"] = cs_u[0]
    gr["cv_w_pw1"] = _mm_tn_blocked(n0, dub, "dw_pw1", bf16)
    dx, dg_mix0 = _mm_nt([(dub, ("blocks", p["cv_w_pw1"]))], "dx_pw1", _rms_bwd_tail(x2, _row(p["norm_mix_g"][0]), dh1))
    gr["norm_mix_g"] = jnp.stack([dg_mix0[0], dg_mix1[0]])
    return loss, dx.reshape(bl, s, d), gr


ANY = pl.BlockSpec(memory_space=pl.ANY)
MESH = pl.DeviceIdType.MESH


def _flip(v, bit):
    return 1 - v if bit else v


def _all_gather_many(shards):
    na = len(shards)

    def body(*refs):
        x_refs, o_refs = refs[:na], refs[na:2 * na]
        send_sems, recv_sems, local_sems = refs[2 * na:]
        x, y, c = lax.axis_index("x"), lax.axis_index("y"), lax.axis_index("c")
        me, sibling = (x, y, c), (x, y, 1 - c)
        chips = [(1 - x, y), (x, 1 - y), (1 - x, 1 - y)]

        def copy(a, k, block, to, src=None):
            px, py, pc = block
            dst = o_refs[a].at[4 * px + 2 * py + pc]
            return pltpu.make_async_remote_copy(
                src_ref=dst if src is None else src, dst_ref=dst,
                send_sem=send_sems.at[a, k], recv_sem=recv_sems.at[a, k], device_id=to, device_id_type=MESH)

        mine = [pltpu.make_async_copy(x_refs[a], o_refs[a].at[4 * x + 2 * y + c], local_sems.at[a]) for a in range(na)]
        first = []
        for a in range(na):
            first.append(copy(a, 0, me, sibling, src=x_refs[a]))
            first += [copy(a, 1 + j, me, (*chip, c), src=x_refs[a]) for j, chip in enumerate(chips)]
        for cp in mine + first:
            cp.start()
        passed = []
        for j, chip in enumerate(chips):
            for a in range(na):
                copy(a, 1 + j, (*chip, c), me).wait_recv()
                fwd = copy(a, 4 + j, (*chip, c), sibling)
                fwd.start()
                passed.append(fwd)
        for a in range(na):
            copy(a, 0, sibling, me).wait_recv()
        for j, chip in enumerate(chips):
            for a in range(na):
                copy(a, 4 + j, (*chip, 1 - c), me).wait_recv()
        for cp in first + passed:
            cp.wait_send()
        for cp in mine:
            cp.wait()

    return pl.pallas_call(
        body, name="weights_all_gather",
        out_shape=[jax.ShapeDtypeStruct((N_DEV,) + a.shape, a.dtype) for a in shards],
        in_specs=[ANY] * na, out_specs=[ANY] * na,
        scratch_shapes=[pltpu.SemaphoreType.DMA((na, 7)), pltpu.SemaphoreType.DMA((na, 7)), pltpu.SemaphoreType.DMA((na,))],
        compiler_params=pltpu.CompilerParams(has_side_effects=True),
    )(*shards)


HBM = pl.BlockSpec(memory_space=pltpu.HBM)
SEM = pl.BlockSpec(memory_space=pltpu.SEMAPHORE)
EFFECT = pltpu.SideEffectType.DATAFLOW_SIDE_EFFECTING
N_PEERS = N_DEV - 1


def _exchange_copies(src_refs, land_refs, send_sems, recv_sems, scatter):
    x, y, c = lax.axis_index("x"), lax.axis_index("y"), lax.axis_index("c")
    me = 4 * x + 2 * y + c
    copies = []
    for a, (src, land) in enumerate(zip(src_refs, land_refs)):
        for k in range(1, N_DEV):
            px, py, pc = _flip(x, k & 4), _flip(y, k & 2), _flip(c, k & 1)
            i = a * N_PEERS + k - 1
            copies.append(pltpu.make_async_remote_copy(
                src_ref=src.at[4 * px + 2 * py + pc] if scatter[a] else src, dst_ref=land.at[me],
                send_sem=send_sems.at[i], recv_sem=recv_sems.at[i], device_id=(px, py, pc), device_id_type=MESH))
    return copies


def _exchange_start(srcs, scatter, name):
    na = len(srcs)
    lands = [lax.empty(s.shape if sc else (N_DEV,) + s.shape, s.dtype) for s, sc in zip(srcs, scatter)]

    def body(*refs):
        copies = _exchange_copies(refs[:na], refs[na:2 * na], refs[2 * na], refs[2 * na + 1], scatter)
        for cp in copies:
            cp.start()
        token = refs[-1]
        token[...] = jnp.zeros_like(token)

    outs = pl.pallas_call(
        body, name=name,
        out_shape=(pltpu.SemaphoreType.DMA((na * N_PEERS,)), pltpu.SemaphoreType.DMA((na * N_PEERS,)))
        + tuple(pltpu.HBM(a.shape, a.dtype) for a in srcs + lands) + (jax.ShapeDtypeStruct((SUBLANES, LANES), f32),),
        in_specs=[HBM] * (2 * na),
        out_specs=(SEM, SEM) + (HBM,) * (2 * na) + (pl.BlockSpec(memory_space=pltpu.VMEM),),
        input_output_aliases={i: 2 + i for i in range(2 * na)},
        compiler_params=pltpu.CompilerParams(has_side_effects=EFFECT),
    )(*[pltpu.with_memory_space_constraint(a, pltpu.HBM) for a in srcs + lands])
    return outs[0], outs[1], list(outs[2:2 + na]), list(outs[2 + na:2 + 2 * na]), outs[-1]


def _exchange_wait(started, after, scatter, name):
    send_sems, recv_sems, srcs, lands, _ = started
    na = len(srcs)

    def body(*refs):
        for cp in _exchange_copies(refs[:na], refs[na:2 * na], refs[2 * na], refs[2 * na + 1], scatter):
            cp.wait_send()
            cp.wait_recv()

    outs = pl.pallas_call(
        body, name=name,
        out_shape=tuple(pltpu.HBM(a.shape, a.dtype) for a in srcs + lands),
        in_specs=[HBM] * (2 * na) + [SEM, SEM, ANY], out_specs=(HBM,) * (2 * na),
        input_output_aliases={i: i for i in range(2 * na)},
        compiler_params=pltpu.CompilerParams(has_side_effects=EFFECT),
    )(*srcs, *lands, send_sems, recv_sems, after)
    return list(outs[na:])


def _own_block(land, block, me):
    return lax.dynamic_update_index_in_dim(land, block, me, 0)


def _sum8_adamw(r2, w, m, v, layer, name):
    _, rr, cc = r2.shape
    tr = _tile(rr, 256)
    bc1 = 1.0 - ADAM_B1 ** ADAM_STEP
    bc2 = 1.0 - ADAM_B2 ** ADAM_STEP

    def body(r_ref, w_ref, m_ref, v_ref, g_ref, d_ref, nm_ref, nv_ref):
        gv = r_ref[0].astype(f32)
        for q in range(1, N_DEV):
            gv = gv + r_ref[q].astype(f32)
        g_ref[...] = gv
        nm = ADAM_B1 * m_ref[...] + (1.0 - ADAM_B1) * gv
        nv = ADAM_B2 * v_ref[...] + (1.0 - ADAM_B2) * (gv * gv)
        nm_ref[...] = nm
        nv_ref[...] = nv
        d_ref[...] = -ADAM_LR * ((nm / bc1) / (jnp.sqrt(nv / bc2) + ADAM_EPS) + ADAM_WD * w_ref[...])

    lspec = pl.BlockSpec((None, tr, cc), lambda i: (layer, i, 0))
    return _pc(body, name=name, grid=(rr // tr,),
               in_specs=[pl.BlockSpec((N_DEV, tr, cc), lambda i: (0, i, 0)), lspec, lspec, lspec],
               out_specs=[_rows(tr, cc)] * 4, out_shape=[jax.ShapeDtypeStruct((rr, cc), f32)] * 4,
               sem=("parallel",))(r2, w, m, v)


def _sum_devices(recv):
    _, rr, _ = recv.shape
    tr = _tile(rr, 512)

    def body(r_ref, o_ref):
        acc = r_ref[0]
        for i in range(1, N_DEV):
            acc = acc + r_ref[i]
        o_ref[...] = acc

    return _pc(body, name="grads_sum", grid=(rr // tr,),
               in_specs=[pl.BlockSpec((N_DEV, tr, LANES), lambda i: (0, i, 0))],
               out_specs=_rows(tr, LANES), out_shape=jax.ShapeDtypeStruct((rr, LANES), f32), sem=("parallel",))(recv)


def _adamw(w, g, m, v):
    rr = w.shape[0]
    tr = _tile(rr, 512)
    bc1 = 1.0 - ADAM_B1 ** ADAM_STEP
    bc2 = 1.0 - ADAM_B2 ** ADAM_STEP

    def body(w_ref, g_ref, m_ref, v_ref, d_ref, nm_ref, nv_ref):
        gv = g_ref[...]
        nm = ADAM_B1 * m_ref[...] + (1.0 - ADAM_B1) * gv
        nv = ADAM_B2 * v_ref[...] + (1.0 - ADAM_B2) * (gv * gv)
        nm_ref[...] = nm
        nv_ref[...] = nv
        d_ref[...] = -ADAM_LR * ((nm / bc1) / (jnp.sqrt(nv / bc2) + ADAM_EPS) + ADAM_WD * w_ref[...])

    spec = _rows(tr, LANES)
    return _pc(body, name="adamw", grid=(rr // tr,), in_specs=[spec] * 4, out_specs=[spec] * 3,
               out_shape=[jax.ShapeDtypeStruct((rr, LANES), f32)] * 3, sem=("parallel",))(w, g, m, v)


PACK_ROWS = 512
PART_ROWS = SUBLANES


def _pack(arrs):
    parts, sizes = [], []
    for a in arrs:
        flat = a.reshape(-1)
        n = flat.shape[0]
        rows = -(-n // (LANES * PART_ROWS)) * PART_ROWS
        if rows * LANES != n:
            flat = jnp.pad(flat, (0, rows * LANES - n))
        parts.append(flat.reshape(rows, LANES))
        sizes.append((rows, n))
    total = sum(r for r, _ in sizes)
    padded = -(-total // PACK_ROWS) * PACK_ROWS
    if padded > total:
        parts.append(jnp.zeros((padded - total, LANES), parts[0].dtype))
    return jnp.concatenate(parts, axis=0), sizes


def _unpack(packed, sizes, shapes):
    out, off = [], 0
    for (rows, n), shp in zip(sizes, shapes):
        piece = lax.slice_in_dim(packed, off, off + rows, axis=0)
        if rows * LANES != n:
            piece = lax.slice_in_dim(piece.reshape(-1), 0, n, axis=0)
        out.append(piece.reshape(tuple(shp)))
        off += rows
    return out


def _cols_to_blocks(a):
    n = a.shape[-1] // N_DEV
    a = a.reshape(a.shape[:-1] + (N_DEV, n))
    return jnp.moveaxis(a, -2, 0)


def _blocks_to_cols(a):
    a = jnp.moveaxis(a, 0, -2)
    return a.reshape(a.shape[:-2] + (a.shape[-2] * a.shape[-1],))


def _rows_to_blocks(a):
    k = a.shape[-2] // N_DEV
    a = a.reshape(a.shape[:-2] + (N_DEV, k, a.shape[-1]))
    return jnp.moveaxis(a, -3, 0)


def _blocks_to_rows(a):
    a = jnp.moveaxis(a, 0, -3)
    return a.reshape(a.shape[:-3] + (a.shape[-3] * a.shape[-2], a.shape[-1]))


COL_SHARDED = ("cv_w_pw1", "gdn_w_in", "mlp_w1")
ROW_SHARDED = ("cv_w_pw2", "gdn_w_out", "mlp_w2")
CONV_SHARDED = ("cv_w_dw", "gdn_conv_w")
REPLICATED = ("norm_mix_g", "norm_ffn_g", "final_norm_g", "cv_b_pw1", "cv_b_dw", "cv_ln_g", "cv_ln_b", "cv_b_pw2",
              "gdn_a_log", "gdn_dt_bias", "gdn_norm_g")
WEIGHTS = ("norm_mix_g", "norm_ffn_g", "final_norm_g", "cv_w_pw1", "cv_b_pw1", "cv_w_dw", "cv_b_dw", "cv_ln_g",
           "cv_ln_b", "cv_w_pw2", "cv_b_pw2", "gdn_w_in", "gdn_conv_w", "gdn_a_log", "gdn_dt_bias", "gdn_norm_g",
           "gdn_w_out", "mlp_w1", "mlp_w2")
MATMUL_SHARDED = COL_SHARDED + ROW_SHARDED


def _squeeze_layer(name, a):
    if name in ("norm_mix_g", "norm_ffn_g", "final_norm_g", "mlp_w1", "mlp_w2"):
        return a
    return a[0]


def _gather_weights(shards):
    me = 4 * lax.axis_index("x") + 2 * lax.axis_index("y") + lax.axis_index("c")
    first = ("cv_w_pw1", "cv_w_pw2") + CONV_SHARDED
    got = dict(zip(first, _all_gather_many([shards[n] if n in CONV_SHARDED else shards[n].astype(bf16) for n in first])))
    now = {"cv_w_pw1": got["cv_w_pw1"], "cv_w_pw2": _blocks_to_rows(got["cv_w_pw2"])}
    for n in CONV_SHARDED:
        now[n] = _blocks_to_cols(got[n])

    def cast(a, tie):
        return (a + tie).astype(bf16)

    token, _ = lax.optimization_barrier((jnp.zeros((), f32), got["cv_w_pw1"]))
    later, started = {}, {}
    for group in ("mlp0", "gdn", "mlp1"):
        if group == "gdn":
            srcs = [cast(shards["gdn_w_in"], token).reshape(-1, LANES), cast(shards["gdn_w_out"], token)]
        else:
            layer = int(group[-1])
            srcs = [cast(shards["mlp_w1"][layer], token), cast(shards["mlp_w2"][layer], token)]
        later[group] = srcs
        started[group] = _exchange_start(srcs, [False] * len(srcs), f"weights_{group}_start")
        token = started[group][4][0, 0]

    def need(group, after):
        lands = _exchange_wait(started[group], after, [False] * len(later[group]), f"weights_{group}_wait")
        lands = [_own_block(ld, own, me) for ld, own in zip(lands, later[group])]
        if group == "gdn":
            w_in = _blocks_to_cols(lands[0].reshape((N_DEV,) + shards["gdn_w_in"].shape))
            return {"w_in": w_in, "w_out": _blocks_to_rows(lands[1])}
        return {"w1": lands[0], "w2": lands[1]}

    return now, need, token


def kernel(x, norm_mix_g, norm_ffn_g, final_norm_g, cv_w_pw1, cv_b_pw1, cv_w_dw, cv_b_dw, cv_ln_g, cv_ln_b, cv_w_pw2, cv_b_pw2, gdn_w_in, gdn_conv_w, gdn_a_log, gdn_dt_bias, gdn_norm_g, gdn_w_out, mlp_w1, mlp_w2, loss_target, m_norm_mix_g, m_norm_ffn_g, m_final_norm_g, m_cv_w_pw1, m_cv_b_pw1, m_cv_w_dw, m_cv_b_dw, m_cv_ln_g, m_cv_ln_b, m_cv_w_pw2, m_cv_b_pw2, m_gdn_w_in, m_gdn_conv_w, m_gdn_a_log, m_gdn_dt_bias, m_gdn_norm_g, m_gdn_w_out, m_mlp_w1, m_mlp_w2, v_norm_mix_g, v_norm_ffn_g, v_final_norm_g, v_cv_w_pw1, v_cv_b_pw1, v_cv_w_dw, v_cv_b_dw, v_cv_ln_g, v_cv_ln_b, v_cv_w_pw2, v_cv_b_pw2, v_gdn_w_in, v_gdn_conv_w, v_gdn_a_log, v_gdn_dt_bias, v_gdn_norm_g, v_gdn_w_out, v_mlp_w1, v_mlp_w2):
    w_in = dict(zip(WEIGHTS, (norm_mix_g, norm_ffn_g, final_norm_g, cv_w_pw1, cv_b_pw1, cv_w_dw, cv_b_dw, cv_ln_g, cv_ln_b, cv_w_pw2, cv_b_pw2, gdn_w_in, gdn_conv_w, gdn_a_log, gdn_dt_bias, gdn_norm_g, gdn_w_out, mlp_w1, mlp_w2)))
    m_in = dict(zip(WEIGHTS, (m_norm_mix_g, m_norm_ffn_g, m_final_norm_g, m_cv_w_pw1, m_cv_b_pw1, m_cv_w_dw, m_cv_b_dw, m_cv_ln_g, m_cv_ln_b, m_cv_w_pw2, m_cv_b_pw2, m_gdn_w_in, m_gdn_conv_w, m_gdn_a_log, m_gdn_dt_bias, m_gdn_norm_g, m_gdn_w_out, m_mlp_w1, m_mlp_w2)))
    v_in = dict(zip(WEIGHTS, (v_norm_mix_g, v_norm_ffn_g, v_final_norm_g, v_cv_w_pw1, v_cv_b_pw1, v_cv_w_dw, v_cv_b_dw, v_cv_ln_g, v_cv_ln_b, v_cv_w_pw2, v_cv_b_pw2, v_gdn_w_in, v_gdn_conv_w, v_gdn_a_log, v_gdn_dt_bias, v_gdn_norm_g, v_gdn_w_out, v_mlp_w1, v_mlp_w2)))
    me = 4 * lax.axis_index("x") + 2 * lax.axis_index("y") + lax.axis_index("c")

    shards = {n: _squeeze_layer(n, w_in[n]) for n in WEIGHTS}
    first, need, token = _gather_weights(shards)
    params = {n: shards[n] for n in REPLICATED}
    params.update(first)
    params["norm_mix_g"] = params["norm_mix_g"] + token

    def row_blocks(a):
        return a.reshape(N_DEV, a.shape[0] // N_DEV, a.shape[1])

    def flat_blocks(a):
        k, n8 = a.shape
        return _cols_to_blocks(a).reshape(N_DEV, k * (n8 // N_DEV) // LANES, LANES)

    def as_blocks(n, a):
        if n == "gdn_w_in":
            return flat_blocks(a).astype(bf16)
        return a if a.ndim == 3 else row_blocks(a)

    sent = []

    def emit(group, grads_out):
        names = list(grads_out)
        blocks = [as_blocks(n, grads_out[n]) for n in names]
        st = _exchange_start(blocks, [True] * len(blocks), f"grads_{group}_start")
        sent.append((names, blocks, st, [True] * len(blocks)))
        return st[4][0, 0]

    loss_part, grad_x, gr = _local_step(x, loss_target, params, need, emit)
    loss = lax.psum(loss_part, ("x", "y", "c"))

    small = REPLICATED + CONV_SHARDED
    small_send, ssz = _pack([gr[n] for n in small])
    last_blocks = [gr["cv_w_pw1"], small_send]
    last_scatter = [True, False]
    sent.append((["cv_w_pw1", "small"], last_blocks, _exchange_start(last_blocks, last_scatter, "grads_last_start"), last_scatter))

    recv = {}

    def finish(entry, after):
        names, blocks, st, scatter = entry
        lands = _exchange_wait(st, after, scatter, f"grads_{names[0]}_wait")
        for n, ld, blk, sc in zip(names, lands, blocks, scatter):
            own = lax.dynamic_index_in_dim(blk, me, 0, keepdims=False) if sc else blk
            recv[n] = _own_block(ld, own, me)
        return lands[0]

    def as3d(n, a):
        if n == "gdn_w_in":
            return a.reshape(a.shape[0], -1, LANES)
        return a

    big = [("cv_w_pw2", 0, "cv_w_pw2"), ("gdn_w_in", 0, "gdn_w_in"), ("gdn_w_out", 0, "gdn_w_out"),
           ("mlp_w1", 0, "mlp_w1_0"), ("mlp_w1", 1, "mlp_w1_1"), ("mlp_w2", 0, "mlp_w2_0"), ("mlp_w2", 1, "mlp_w2_1"),
           ("cv_w_pw1", 0, "cv_w_pw1")]
    res = {n: {} for n in MATMUL_SHARDED}
    after = grad_x
    for entry in sent[:-1]:
        after = finish(entry, after)
    for n, layer, key in big[:-1]:
        res[n][layer] = _sum8_adamw(recv[key], as3d(n, w_in[n]), as3d(n, m_in[n]), as3d(n, v_in[n]), layer, f"adamw_{key}")
        after = res[n][layer][0]
    finish(sent[-1], after)
    n, layer, key = big[-1]
    res[n][layer] = _sum8_adamw(recv[key], as3d(n, w_in[n]), as3d(n, m_in[n]), as3d(n, v_in[n]), layer, f"adamw_{key}")

    small_sum = _sum_devices(recv["small"])
    grads = dict(zip(small, _unpack(small_sum, ssz, [gr[n].shape for n in small])))
    for n in CONV_SHARDED:
        cn = shards[n].shape[-1]
        grads[n] = lax.dynamic_slice_in_dim(grads[n], me * cn, cn, axis=1)

    out_groups = {n: [] for n in WEIGHTS}
    for n in MATMUL_SHARDED:
        layers = sorted(res[n])
        for k in range(4):
            pieces = [res[n][layer][k] for layer in layers]
            out_groups[n].append(jnp.stack(pieces).reshape(w_in[n].shape))

    sm_w = [shards[n] for n in small]
    sm_g = [grads[n].reshape(shards[n].shape) for n in small]
    sm_m = [_squeeze_layer(n, m_in[n]) for n in small]
    sm_v = [_squeeze_layer(n, v_in[n]) for n in small]
    wp, psz = _pack(sm_w)
    gp, _ = _pack(sm_g)
    mp, _ = _pack(sm_m)
    vp, _ = _pack(sm_v)
    dp, nmp, nvp = _adamw(wp, gp, mp, vp)
    shp = [a.shape for a in sm_w]
    for n, g, dl, nm, nv in zip(small, sm_g, _unpack(dp, psz, shp), _unpack(nmp, psz, shp), _unpack(nvp, psz, shp)):
        out_groups[n] = [a.reshape(w_in[n].shape) for a in (g, dl, nm, nv)]

    outs = [loss, grad_x]
    for k in range(4):
        outs += [out_groups[n][k] for n in WEIGHTS]
    return tuple(outs)
```

```python
import functools

import jax
import jax.numpy as jnp
from jax import lax
from jax.experimental import pallas as pl
from jax.experimental.pallas import tpu as pltpu

f32, bf16 = jnp.float32, jnp.bfloat16

NORM_EPS = 1e-6
L2_EPS = 1e-6
CHUNK = 64
LANES = 128
SUBLANES = 8
N_DEV = 8
VMEM_LIMIT = 56 * 1024 * 1024
CONV_PAD = 32
HEADS_PER_STEP = 4
PREP_HEADS_PER_STEP = 2
NEG = -1e30

ADAM_LR, ADAM_B1, ADAM_B2, ADAM_EPS, ADAM_WD, ADAM_STEP = 0.001, 0.9, 0.999, 1e-08, 0.01, 10

NT = (((1,), (1,)), ((), ()))
TN = (((0,), (0,)), ((), ()))
HI = lax.Precision.HIGHEST


def _pc(body, *, name, grid, in_specs, out_specs, out_shape, scratch=(), sem=None):
    return pl.pallas_call(
        body, name=name, grid=grid, in_specs=in_specs, out_specs=out_specs, out_shape=out_shape,
        scratch_shapes=list(scratch),
        compiler_params=pltpu.CompilerParams(dimension_semantics=sem, vmem_limit_bytes=VMEM_LIMIT))


def _rows(tm, n):
    return pl.BlockSpec((tm, n), lambda i: (i, 0))


def _const(shape):
    return pl.BlockSpec(shape, lambda *_: (0,) * len(shape))


def _resident(shape):
    return pl.BlockSpec(shape, lambda *_: (0,) * len(shape), pipeline_mode=pl.Buffered(1))


def _tile(t, pref):
    return pref if t % pref == 0 else t


def _dot(a, b):
    return jnp.dot(a.astype(bf16), b.astype(bf16), preferred_element_type=f32)


def _dot_nt(a, b):
    return lax.dot_general(a.astype(bf16), b.astype(bf16), NT, preferred_element_type=f32)


def _dot_tn(a, b):
    return lax.dot_general(a.astype(bf16), b.astype(bf16), TN, preferred_element_type=f32)


def _sigmoid(x):
    return 1.0 / (1.0 + jnp.exp(-x))


def _silu_grad(x):
    s = _sigmoid(x)
    return s * (1.0 + x * (1.0 - s))


def _rms(x, g):
    rstd = lax.rsqrt(jnp.mean(x * x, axis=-1, keepdims=True) + NORM_EPS)
    xh = x * rstd
    return xh * g, xh, rstd


def _rms_bwd(dn, xh, rstd, g):
    dxh = dn * g
    return rstd * (dxh - xh * jnp.mean(dxh * xh, axis=-1, keepdims=True))


def _acc_init(step, *refs):
    @pl.when(step == 0)
    def _():
        for r in refs:
            r[...] = jnp.zeros(r.shape, r.dtype)


def _acc_rows(ref, val):
    ref[0:1, :] += jnp.sum(val, axis=0, keepdims=True)


def _pw1_glu(x, g, w, b):
    t, d = x.shape
    tm = _tile(t, 256)

    nb_w = w.shape[0]

    def body(x_ref, g_ref, w_hbm, b_ref, n_ref, u_ref, gl_ref, w_ref, sems):
        _fetch_blocks(pl.program_id(0), w_hbm, w_ref, sems, True)
        n, _, _ = _rms(x_ref[...], g_ref[...])
        nb = n.astype(bf16)
        n_ref[...] = nb
        u = jnp.dot(nb, w_ref[...], preferred_element_type=f32) + b_ref[...]
        u_ref[...] = u.astype(bf16)
        gl_ref[...] = u[:, :d] * _sigmoid(u[:, d:])

    return _pc(body, name="pw1_glu", grid=(t // tm,),
               in_specs=[_rows(tm, d), _const((1, d)), ANY, _const((1, 2 * d))],
               out_specs=[_rows(tm, d), _rows(tm, 2 * d), _rows(tm, d)],
               out_shape=[jax.ShapeDtypeStruct((t, d), bf16), jax.ShapeDtypeStruct((t, 2 * d), bf16),
                          jax.ShapeDtypeStruct((t, d), f32)],
               scratch=[pltpu.VMEM((d, 2 * d), bf16), pltpu.SemaphoreType.DMA((nb_w,))],
               sem=("arbitrary",))(x, g, w, b)


def _ln_silu_mm_res(dc, ln_g, ln_b, w, b, res):
    t, d = dc.shape
    tm = _tile(t, 256)

    def body(x_ref, g_ref, bb_ref, w_ref, b_ref, r_ref, s_ref, o_ref):
        x = x_ref[...]
        xc = x - jnp.mean(x, axis=-1, keepdims=True)
        rstd = lax.rsqrt(jnp.mean(xc * xc, axis=-1, keepdims=True) + NORM_EPS)
        ln = xc * rstd * g_ref[...] + bb_ref[...]
        sb = (ln * _sigmoid(ln)).astype(bf16)
        s_ref[...] = sb
        o_ref[...] = r_ref[...] + jnp.dot(sb, w_ref[...], preferred_element_type=f32) + b_ref[...]

    return _pc(body, name="ln_silu_pw2", grid=(t // tm,),
               in_specs=[_rows(tm, d), _const((1, d)), _const((1, d)), _resident((d, d)), _const((1, d)), _rows(tm, d)],
               out_specs=[_rows(tm, d), _rows(tm, d)],
               out_shape=[jax.ShapeDtypeStruct((t, d), bf16), jax.ShapeDtypeStruct((t, d), f32)],
               sem=("parallel",))(dc, ln_g, ln_b, w, b, res)


def _fetch_blocks(step, w_hbm, dst, sems, by_cols, layer=None):
    nb_w = w_hbm.shape[0]
    step_rows, step_cols = w_hbm.shape[-2], w_hbm.shape[-1]

    @pl.when(step == 0)
    def _():
        copies = []
        for j in range(nb_w):
            src = w_hbm.at[j] if layer is None else w_hbm.at[j, layer]
            if by_cols:
                part = dst.at[:, pl.ds(j * step_cols, step_cols)]
            else:
                part = dst.at[pl.ds(j * step_rows, step_rows), :]
            copies.append(pltpu.make_async_copy(src, part, sems.at[j]))
        for cp in copies:
            cp.start()
        for cp in copies:
            cp.wait()


def _mlp_fwd(h, g, w1g, w2g, name, loss=None):
    t, d = h.shape
    nb_w, _, bn = w1g.shape
    ff = nb_w * bn
    tm = _tile(t, 256)
    n_in = 4 if loss is None else 6
    n_out = 4 if loss is None else 6

    def body(*refs):
        h_ref, g_ref, w1_hbm, w2_hbm = refs[:4]
        n_ref, f_ref, r_ref = refs[n_in:n_in + 3]
        w1_ref, w2_ref, sem1, sem2 = refs[n_in + n_out:]
        _fetch_blocks(pl.program_id(0), w1_hbm, w1_ref, sem1, True)
        _fetch_blocks(pl.program_id(0), w2_hbm, w2_ref, sem2, False)
        hv = h_ref[...]
        n, _, _ = _rms(hv, g_ref[...])
        nb = n.astype(bf16)
        n_ref[...] = nb
        f = jnp.dot(nb, w1_ref[...], preferred_element_type=f32)
        f_ref[...] = f.astype(bf16)
        rb = jnp.square(jnp.maximum(f, 0.0)).astype(bf16)
        r_ref[...] = rb
        out = hv + jnp.dot(rb, w2_ref[...], preferred_element_type=f32)
        if loss is None:
            refs[n_in + 3][...] = out
        else:
            gf_ref, t_ref = refs[4:6]
            loss_ref, dh_ref, dg_ref = refs[n_in + 3:n_in + 6]
            _acc_init(pl.program_id(0), loss_ref, dg_ref)
            gv = gf_ref[...]
            y, xh, rstd = _rms(out, gv)
            e = y - t_ref[...]
            loss_ref[...] += 0.5 * jnp.sum(jnp.mean(e * e, axis=-1, keepdims=True))
            dy = e * (1.0 / d)
            _acc_rows(dg_ref, dy * xh)
            dh_ref[...] = _rms_bwd(dy, xh, rstd, gv)

    in_specs = [_rows(tm, d), _const((1, d)), ANY, ANY]
    out_specs = [_rows(tm, d), _rows(tm, ff), _rows(tm, ff)]
    out_shape = [jax.ShapeDtypeStruct((t, d), bf16), jax.ShapeDtypeStruct((t, ff), bf16), jax.ShapeDtypeStruct((t, ff), bf16)]
    args = [h, g, w1g, w2g]
    if loss is None:
        out_specs.append(_rows(tm, d))
        out_shape.append(jax.ShapeDtypeStruct((t, d), f32))
    else:
        in_specs += [_const((1, d)), _rows(tm, d)]
        args += list(loss)
        out_specs += [_const((SUBLANES, LANES)), _rows(tm, d), _const((SUBLANES, d))]
        out_shape += [jax.ShapeDtypeStruct((SUBLANES, LANES), f32), jax.ShapeDtypeStruct((t, d), f32),
                      jax.ShapeDtypeStruct((SUBLANES, d), f32)]
    return _pc(body, name=name, grid=(t // tm,), in_specs=in_specs, out_specs=out_specs, out_shape=out_shape,
               scratch=[pltpu.VMEM((d, ff), bf16), pltpu.VMEM((ff, d), bf16),
                        pltpu.SemaphoreType.DMA((nb_w,)), pltpu.SemaphoreType.DMA((nb_w,))],
               sem=("arbitrary",))(*args)


def _softplus(x):
    return jnp.maximum(x, 0.0) + jnp.log(1.0 + jnp.exp(-jnp.abs(x)))


def _gdn_in(h, g, w_main, w_ab, a_log_pad, dt_pad, n_heads):
    t, d = h.shape
    tm = _tile(t, 256)

    def body(h_ref, g_ref, wm_ref, wab_ref, al_ref, dt_ref, n_ref, qkv_ref, z_ref, ab_ref, gb_ref):
        n, _, _ = _rms(h_ref[...], g_ref[...])
        nb = n.astype(bf16)
        n_ref[...] = nb
        p = jnp.dot(nb, wm_ref[...], preferred_element_type=f32)
        qkv_ref[...] = p[:, :3 * d]
        z_ref[...] = p[:, 3 * d:]
        ab = jnp.dot(nb, wab_ref[...], preferred_element_type=f32)
        ab_ref[...] = ab
        lane = lax.broadcasted_iota(jnp.int32, ab.shape, 1)
        decay = -jnp.exp(al_ref[...]) * _softplus(ab + dt_ref[...])
        gb_ref[...] = jnp.where(lane < n_heads, decay, jnp.where(lane < 2 * n_heads, _sigmoid(ab), 0.0))

    return _pc(body, name="gdn_in", grid=(t // tm,),
               in_specs=[_rows(tm, d), _const((1, d)), _resident((d, 4 * d)), _resident((d, LANES)),
                         _const((1, LANES)), _const((1, LANES))],
               out_specs=[_rows(tm, d), _rows(tm, 3 * d), _rows(tm, d), _rows(tm, LANES), _rows(tm, LANES)],
               out_shape=[jax.ShapeDtypeStruct((t, d), bf16), jax.ShapeDtypeStruct((t, 3 * d), f32),
                          jax.ShapeDtypeStruct((t, d), f32), jax.ShapeDtypeStruct((t, LANES), f32),
                          jax.ShapeDtypeStruct((t, LANES), f32)],
               sem=("parallel",))(h, g, w_main, w_ab, a_log_pad, dt_pad)


def _gated_norm_mm_res(o, z, ng, w, res, n_heads):
    t, d = o.shape
    tm = _tile(t, 256)

    def body(o_ref, z_ref, ng_ref, w_ref, r_ref, on_ref, out_ref):
        for hd in range(n_heads):
            sl = slice(hd * LANES, (hd + 1) * LANES)
            rn, _, _ = _rms(o_ref[:, sl], ng_ref[...])
            zz = z_ref[:, sl]
            on_ref[:, sl] = (rn * (zz * _sigmoid(zz))).astype(bf16)
        out_ref[...] = r_ref[...] + jnp.dot(on_ref[...], w_ref[...], preferred_element_type=f32)

    return _pc(body, name="gated_norm_wout", grid=(t // tm,),
               in_specs=[_rows(tm, d), _rows(tm, d), _const((1, LANES)), _resident((d, d)), _rows(tm, d)],
               out_specs=[_rows(tm, d), _rows(tm, d)],
               out_shape=[jax.ShapeDtypeStruct((t, d), bf16), jax.ShapeDtypeStruct((t, d), f32)],
               sem=("parallel",))(o, z, ng, w, res)


def _mlp_bwd(dho, h, g, fb, w1g, w2g, name):
    t, d = h.shape
    nb_w, _, bn = w1g.shape
    ff = nb_w * bn
    tm = _tile(t, 256)

    def body(do_ref, h_ref, g_ref, f_ref, w1_hbm, w2_hbm, df_ref, dh_ref, dg_ref, cs_ref, w1_ref, w2_ref, sem1, sem2):
        _fetch_blocks(pl.program_id(0), w1_hbm, w1_ref, sem1, True)
        _fetch_blocks(pl.program_id(0), w2_hbm, w2_ref, sem2, False)
        _acc_init(pl.program_id(0), dg_ref, cs_ref)
        do = do_ref[...]
        dr = lax.dot_general(do.astype(bf16), w2_ref[...], NT, preferred_element_type=f32)
        dfb = (dr * (2.0 * jnp.maximum(f_ref[...].astype(f32), 0.0))).astype(bf16)
        df_ref[...] = dfb
        dn = lax.dot_general(dfb, w1_ref[...], NT, preferred_element_type=f32)
        gv = g_ref[...]
        _, xh, rstd = _rms(h_ref[...], gv)
        _acc_rows(dg_ref, dn * xh)
        dh = do + _rms_bwd(dn, xh, rstd, gv)
        dh_ref[...] = dh
        _acc_rows(cs_ref, dh)

    return _pc(body, name=name, grid=(t // tm,),
               in_specs=[_rows(tm, d), _rows(tm, d), _const((1, d)), _rows(tm, ff), ANY, ANY],
               out_specs=[_rows(tm, ff), _rows(tm, d), _const((SUBLANES, d)), _const((SUBLANES, d))],
               out_shape=[jax.ShapeDtypeStruct((t, ff), bf16), jax.ShapeDtypeStruct((t, d), f32),
                          jax.ShapeDtypeStruct((SUBLANES, d), f32), jax.ShapeDtypeStruct((SUBLANES, d), f32)],
               scratch=[pltpu.VMEM((d, ff), bf16), pltpu.VMEM((ff, d), bf16),
                        pltpu.SemaphoreType.DMA((nb_w,)), pltpu.SemaphoreType.DMA((nb_w,))],
               sem=("arbitrary",))(dho, h, g, fb, w1g, w2g)


class _Tail:
    def __init__(self, fn, ins, outs):
        self.fn, self.ins, self.outs = fn, ins, outs


def _mm_nt(pairs, name, tail=None):
    t = pairs[0][0].shape[0]
    tm = _tile(t, 256)
    npair = len(pairs)
    in_specs, args, scratch, blocked = [], [], [], []
    k = None
    for dy, w in pairs:
        nn = dy.shape[1]
        if isinstance(w, tuple) and isinstance(w[0], str):
            w = w[1]
            k = w.shape[1]
            wspec = ANY
            blocked.append(True)
            scratch += [pltpu.VMEM((k, nn), bf16), pltpu.SemaphoreType.DMA((w.shape[0],))]
        elif isinstance(w, tuple):
            w, idx = w
            k = w.shape[0]
            wspec = pl.BlockSpec((k, nn), lambda *_, idx=idx: (0, idx), pipeline_mode=pl.Buffered(1))
            blocked.append(False)
        else:
            k = w.shape[0]
            wspec = _resident(w.shape)
            blocked.append(False)
        in_specs += [_rows(tm, nn), wspec]
        args += [dy, w]
    n_tin = len(tail.ins) if tail else 0
    n_out = len(tail.outs) if tail else 1
    if tail:
        for arr, kind in tail.ins:
            in_specs.append(_rows(tm, arr.shape[1]) if kind == "rows" else _const(arr.shape))
            args.append(arr)
        out_specs = [_rows(tm, c) if kind == "rows" else _const((SUBLANES, c)) for c, kind in tail.outs]
        out_shape = [jax.ShapeDtypeStruct((t, c) if kind == "rows" else (SUBLANES, c), f32) for c, kind in tail.outs]
    else:
        out_specs = _rows(tm, k)
        out_shape = jax.ShapeDtypeStruct((t, k), f32)

    def body(*refs):
        step = pl.program_id(0)
        tin = refs[2 * npair:2 * npair + n_tin]
        outs = refs[2 * npair + n_tin:2 * npair + n_tin + n_out]
        scr = list(refs[2 * npair + n_tin + n_out:])
        acc = None
        for p in range(npair):
            w_ref = refs[2 * p + 1]
            if blocked[p]:
                w_vmem, sems = scr.pop(0), scr.pop(0)
                _fetch_blocks(step, w_ref, w_vmem, sems, True)
                w_ref = w_vmem
            part = lax.dot_general(refs[2 * p][...].astype(bf16), w_ref[...], NT, preferred_element_type=f32)
            acc = part if acc is None else acc + part
        if tail is None:
            outs[0][...] = acc
        else:
            _acc_init(step, *[o for o, (_, kind) in zip(outs, tail.outs) if kind == "acc"])
            tail.fn(acc, tin, outs)

    sequential = tail is not None or any(blocked)
    return _pc(body, name=name, grid=(t // tm,), in_specs=in_specs, out_specs=out_specs, out_shape=out_shape,
               scratch=scratch, sem=("arbitrary",) if sequential else ("parallel",))(*args)


def _rms_bwd_tail(h, g, dres):
    def fn(dn, ins, outs):
        h_ref, g_ref, dr_ref = ins
        dh_ref, dg_ref = outs
        gv = g_ref[...]
        _, xh, rstd = _rms(h_ref[...], gv)
        _acc_rows(dg_ref, dn * xh)
        dh_ref[...] = dr_ref[...] + _rms_bwd(dn, xh, rstd, gv)

    d = h.shape[1]
    return _Tail(fn, [(h, "rows"), (g, "const"), (dres, "rows")], [(d, "rows"), (d, "acc")])


def _ln_silu_bwd_tail(dc, ln_g, ln_b):
    def fn(ds, ins, outs):
        x_ref, g_ref, b_ref = ins
        dx_ref, dg_ref, db_ref, cs_ref = outs
        x = x_ref[...]
        gv = g_ref[...]
        xc = x - jnp.mean(x, axis=-1, keepdims=True)
        rstd = lax.rsqrt(jnp.mean(xc * xc, axis=-1, keepdims=True) + NORM_EPS)
        xh = xc * rstd
        dln = ds * _silu_grad(xh * gv + b_ref[...])
        _acc_rows(dg_ref, dln * xh)
        _acc_rows(db_ref, dln)
        dxh = dln * gv
        dx = rstd * (dxh - jnp.mean(dxh, axis=-1, keepdims=True) - xh * jnp.mean(dxh * xh, axis=-1, keepdims=True))
        dx_ref[...] = dx
        _acc_rows(cs_ref, dx)

    d = dc.shape[1]
    return _Tail(fn, [(dc, "rows"), (ln_g, "const"), (ln_b, "const")], [(d, "rows"), (d, "acc"), (d, "acc"), (d, "acc")])


def _gated_norm_bwd_tail(o, z, ng, n_heads):
    def fn(don_all, ins, outs):
        o_ref, z_ref, ng_ref = ins
        do_ref, dz_ref, dng_ref = outs
        gv = ng_ref[...]
        for hd in range(n_heads):
            sl = slice(hd * LANES, (hd + 1) * LANES)
            rn, xh, rstd = _rms(o_ref[:, sl], gv)
            zz = z_ref[:, sl]
            don = don_all[:, sl]
            dz_ref[:, sl] = don * rn * _silu_grad(zz)
            drn = don * (zz * _sigmoid(zz))
            _acc_rows(dng_ref, drn * xh)
            do_ref[:, sl] = _rms_bwd(drn, xh, rstd, gv)

    d = o.shape[1]
    return _Tail(fn, [(o, "rows"), (z, "rows"), (ng, "const")], [(d, "rows"), (d, "rows"), (LANES, "acc")])


def _mm_tn_blocked(x, dy, name, out_dtype=f32):
    t, k = x.shape
    bn = dy.shape[1] // N_DEV
    tm = _tile(t, 512)
    jb = N_DEV
    while jb > 1 and k * jb * bn * 4 > 8 * 1024 * 1024:
        jb //= 2
    nt = t // tm

    def body(x_ref, dy_ref, o_ref, *acc):
        acc_ref = acc[0] if acc else o_ref
        _acc_init(pl.program_id(1), acc_ref)
        xt = x_ref[...].astype(bf16).T
        for jj in range(jb):
            acc_ref[jj] += jnp.dot(xt, dy_ref[:, jj * bn:(jj + 1) * bn].astype(bf16), preferred_element_type=f32)
        if acc:
            @pl.when(pl.program_id(1) == nt - 1)
            def _():
                o_ref[...] = acc_ref[...].astype(out_dtype)

    return _pc(body, name=name, grid=(N_DEV // jb, nt),
               in_specs=[pl.BlockSpec((tm, k), lambda j, i: (i, 0)), pl.BlockSpec((tm, jb * bn), lambda j, i: (i, j))],
               out_specs=pl.BlockSpec((jb, k, bn), lambda j, i: (j, 0, 0)),
               out_shape=jax.ShapeDtypeStruct((N_DEV, k, bn), out_dtype),
               scratch=[] if out_dtype == f32 else [pltpu.VMEM((jb, k, bn), f32)],
               sem=("parallel", "arbitrary"))(x, dy)


def _mm_tn(x, dy, name, out_dtype=f32):
    t, k = x.shape
    n = dy.shape[1]
    tm = _tile(t, 512)
    cap = max(LANES, (2 * 1024 * 1024) // k)
    tn = n
    if n > cap:
        tn = max(c for c in range(LANES, cap + 1, LANES) if n % c == 0)
    nt = t // tm

    def body(x_ref, dy_ref, o_ref, *acc):
        acc_ref = acc[0] if acc else o_ref
        _acc_init(pl.program_id(1), acc_ref)
        acc_ref[...] += lax.dot_general(x_ref[...].astype(bf16), dy_ref[...].astype(bf16), TN, preferred_element_type=f32)
        if acc:
            @pl.when(pl.program_id(1) == nt - 1)
            def _():
                o_ref[...] = acc_ref[...].astype(out_dtype)

    return _pc(body, name=name, grid=(n // tn, nt),
               in_specs=[pl.BlockSpec((tm, k), lambda j, i: (i, 0)), pl.BlockSpec((tm, tn), lambda j, i: (i, j))],
               out_specs=pl.BlockSpec((k, tn), lambda j, i: (0, j)),
               out_shape=jax.ShapeDtypeStruct((k, n), out_dtype),
               scratch=[] if out_dtype == f32 else [pltpu.VMEM((k, tn), f32)],
               sem=("parallel", "arbitrary"))(x, dy)


def _gates_bwd(dgb, ab, a_log_pad, dt_pad, n_heads):
    t = ab.shape[0]
    tm = _tile(t, 256)

    def body(dgb_ref, ab_ref, al_ref, dt_ref, dab_ref, dal_ref, ddt_ref):
        _acc_init(pl.program_id(0), dal_ref, ddt_ref)
        ab = ab_ref[...]
        dgb = dgb_ref[...]
        lane = lax.broadcasted_iota(jnp.int32, ab.shape, 1)
        is_a = lane < n_heads
        is_b = jnp.logical_and(lane >= n_heads, lane < 2 * n_heads)
        xa = ab + dt_ref[...]
        neg_a = -jnp.exp(al_ref[...])
        dg_da = neg_a * _sigmoid(xa)
        beta = _sigmoid(ab)
        da = jnp.where(is_a, dgb * dg_da, 0.0)
        dab_ref[...] = da + jnp.where(is_b, dgb * beta * (1.0 - beta), 0.0)
        _acc_rows(dal_ref, jnp.where(is_a, dgb * neg_a * _softplus(xa), 0.0))
        _acc_rows(ddt_ref, da)

    return _pc(body, name="gates_bwd", grid=(t // tm,),
               in_specs=[_rows(tm, LANES), _rows(tm, LANES), _const((1, LANES)), _const((1, LANES))],
               out_specs=[_rows(tm, LANES), _const((SUBLANES, LANES)), _const((SUBLANES, LANES))],
               out_shape=[jax.ShapeDtypeStruct((t, LANES), f32), jax.ShapeDtypeStruct((SUBLANES, LANES), f32),
                          jax.ShapeDtypeStruct((SUBLANES, LANES), f32)],
               sem=("arbitrary",))(dgb, ab, a_log_pad, dt_pad)


def _glu_bwd(dgl, ub):
    t, d = dgl.shape
    tm = _tile(t, 256)

    def body(dgl_ref, u_ref, du_ref, cs_ref):
        _acc_init(pl.program_id(0), cs_ref)
        dgl = dgl_ref[...]
        a = u_ref[:, :d].astype(f32)
        sb = _sigmoid(u_ref[:, d:].astype(f32))
        da = dgl * sb
        db = dgl * a * sb * (1.0 - sb)
        du_ref[:, :d] = da.astype(bf16)
        du_ref[:, d:] = db.astype(bf16)
        cs_ref[0:1, :d] += jnp.sum(da, axis=0, keepdims=True)
        cs_ref[0:1, d:] += jnp.sum(db, axis=0, keepdims=True)

    return _pc(body, name="glu_bwd", grid=(t // tm,),
               in_specs=[_rows(tm, d), _rows(tm, 2 * d)],
               out_specs=[_rows(tm, 2 * d), _const((SUBLANES, 2 * d))],
               out_shape=[jax.ShapeDtypeStruct((t, 2 * d), bf16), jax.ShapeDtypeStruct((SUBLANES, 2 * d), f32)],
               sem=("arbitrary",))(dgl, ub)


def _conv_rows(s):
    return 256 if s % 256 == 0 else s


def _conv_tap_sum(pad_ref, w_ref, base, rows, width):
    acc = jnp.zeros((rows, LANES), f32)
    for j in range(width):
        acc = acc + w_ref[j:j + 1, :] * pad_ref[pl.ds(base + CONV_PAD - (width - 1) + j, rows), :]
    return acc


def _l2_silu_post(c, j, n_heads, scale):
    a = c * _sigmoid(c)
    r = lax.rsqrt(jnp.sum(a * a, axis=-1, keepdims=True) + L2_EPS)
    mult = jnp.where(j < n_heads, r * scale, jnp.where(j < 2 * n_heads, r, 1.0))
    return a, r, a * mult


def _dwconv_fwd(x, w, b, name, qk_heads=None):
    bl, s, cn = x.shape
    width = w.shape[0]
    rows = _conv_rows(s)
    scale = float(LANES) ** -0.5

    def body(x_ref, w_ref, b_ref, o_ref, pad_ref):
        j = pl.program_id(1)
        pad_ref[0:CONV_PAD, :] = jnp.zeros((CONV_PAD, LANES), f32)
        pad_ref[CONV_PAD:, :] = x_ref[0]

        def step(i, carry):
            base = pl.multiple_of(i * rows, rows)
            acc = _conv_tap_sum(pad_ref, w_ref, base, rows, width)
            if qk_heads is None:
                acc = acc + b_ref[...]
            else:
                _, _, acc = _l2_silu_post(acc, j, qk_heads, scale)
            o_ref[0, pl.ds(base, rows), :] = acc
            return carry

        lax.fori_loop(0, s // rows, step, 0)

    return _pc(body, name=name, grid=(bl, cn // LANES),
               in_specs=[pl.BlockSpec((1, s, LANES), lambda bi, j: (bi, 0, j)),
                         pl.BlockSpec((width, LANES), lambda bi, j: (0, j)),
                         pl.BlockSpec((1, LANES), lambda bi, j: (0, j))],
               out_specs=pl.BlockSpec((1, s, LANES), lambda bi, j: (bi, 0, j)),
               out_shape=jax.ShapeDtypeStruct((bl, s, cn), f32),
               scratch=[pltpu.VMEM((s + CONV_PAD, LANES), f32)],
               sem=("parallel", "parallel"))(x, w, b)


def _dwconv_bwd(x, dys, w, name, qk_heads=None):
    bl, s, cn = x.shape
    width = w.shape[0]
    wp = -(-width // SUBLANES) * SUBLANES
    rows = _conv_rows(s)
    scale = float(LANES) ** -0.5
    nblk = s // rows
    ndy = len(dys)

    def body(*refs):
        x_ref, w_ref = refs[0], refs[1]
        dy_refs = refs[2:2 + ndy]
        dx_ref, dw_ref, db_ref, xpad, dypad, acc = refs[2 + ndy:]
        j = pl.program_id(0)
        bi = pl.program_id(1)
        _acc_init(bi, acc, db_ref)
        xpad[0:CONV_PAD, :] = jnp.zeros((CONV_PAD, LANES), f32)
        xpad[CONV_PAD:, :] = x_ref[0]
        dypad[s:, :] = jnp.zeros((CONV_PAD, LANES), f32)
        if qk_heads is None:
            dypad[0:s, :] = dy_refs[0][0]
        else:
            def pre(i, carry):
                base = pl.multiple_of(i * rows, rows)
                c = _conv_tap_sum(xpad, w_ref, base, rows, width)
                a, r, _ = _l2_silu_post(c, j, qk_heads, scale)
                dq = dy_refs[0][0, pl.ds(base, rows), :]
                dk = dy_refs[1][0, pl.ds(base, rows), :]
                dv = dy_refs[2][0, pl.ds(base, rows), :]
                dy = jnp.where(j < qk_heads, dq * scale, jnp.where(j < 2 * qk_heads, dk, dv))
                da_l2 = r * (dy - a * (r * r) * jnp.sum(a * dy, axis=-1, keepdims=True))
                da = jnp.where(j < 2 * qk_heads, da_l2, dy)
                dypad[pl.ds(base, rows), :] = da * _silu_grad(c)
                return carry

            lax.fori_loop(0, nblk, pre, 0)

        def step(i, carry):
            base = pl.multiple_of(i * rows, rows)
            dxa = jnp.zeros((rows, LANES), f32)
            for jj in range(width):
                dxa = dxa + w_ref[jj:jj + 1, :] * dypad[pl.ds(base + (width - 1) - jj, rows), :]
            dx_ref[0, pl.ds(base, rows), :] = dxa
            dyc = dypad[pl.ds(base, rows), :]
            db_ref[...] += dyc.reshape(rows // SUBLANES, SUBLANES, LANES).sum(axis=0)
            for jj in range(width):
                prod = dyc * xpad[pl.ds(base + CONV_PAD - (width - 1) + jj, rows), :]
                acc[jj * SUBLANES:(jj + 1) * SUBLANES, :] += prod.reshape(rows // SUBLANES, SUBLANES, LANES).sum(axis=0)
            return carry

        lax.fori_loop(0, nblk, step, 0)

        @pl.when(bi == bl - 1)
        def _():
            dw_ref[...] = jnp.zeros((wp, LANES), f32)
            for jj in range(width):
                dw_ref[jj:jj + 1, :] = jnp.sum(acc[jj * SUBLANES:(jj + 1) * SUBLANES, :], axis=0, keepdims=True)

    if qk_heads is None:
        dy_specs = [pl.BlockSpec((1, s, LANES), lambda j, bi: (bi, 0, j))]
    else:
        hh = qk_heads
        def dy_spec(part):
            def index(j, bi):
                mine = jnp.logical_and(j >= part * hh, j < (part + 1) * hh)
                return (jnp.where(mine, bi * hh + j - part * hh, 0), 0, 0)
            return pl.BlockSpec((1, s, LANES), index)

        dy_specs = [dy_spec(0), dy_spec(1), dy_spec(2)]
    return _pc(body, name=name, grid=(cn // LANES, bl),
               in_specs=[pl.BlockSpec((1, s, LANES), lambda j, bi: (bi, 0, j)),
                         pl.BlockSpec((width, LANES), lambda j, bi: (0, j))] + dy_specs,
               out_specs=[pl.BlockSpec((1, s, LANES), lambda j, bi: (bi, 0, j)),
                          pl.BlockSpec((wp, LANES), lambda j, bi: (0, j)),
                          pl.BlockSpec((SUBLANES, LANES), lambda j, bi: (0, j))],
               out_shape=[jax.ShapeDtypeStruct((bl, s, cn), f32), jax.ShapeDtypeStruct((wp, cn), f32),
                          jax.ShapeDtypeStruct((SUBLANES, cn), f32)],
               scratch=[pltpu.VMEM((s + CONV_PAD, LANES), f32), pltpu.VMEM((s + CONV_PAD, LANES), f32),
                        pltpu.VMEM((width * SUBLANES, LANES), f32)],
               sem=("parallel", "arbitrary"))(x, w, *dys)


def _group_rows(s):
    for rows in (256, 128):
        if s % rows == 0:
            return rows
    return CHUNK


def _group_masks(rows):
    r = lax.broadcasted_iota(jnp.int32, (rows, rows), 0)
    c = lax.broadcasted_iota(jnp.int32, (rows, rows), 1)
    same_chunk = (r >> 6) == (c >> 6)
    return r, c, same_chunk


def _decay(gc_col, gc_row, causal):
    return jnp.exp(jnp.where(causal, gc_col - gc_row, NEG))


def _inv_unit_lower(a, r, c):
    return _inv_unit_lower_many([a], r, c)[0]


def _inv_unit_lower_many(mats, r, c):
    eye = (r == c).astype(f32)
    same16 = (r >> 4) == (c >> 4)
    ads = [jnp.where(same16, a, 0.0) for a in mats]
    aos = [a - ad for a, ad in zip(mats, ads)]
    xs = ads
    tds = [eye - x for x in xs]
    for _ in range(3):
        xs = [_dot(x, x) for x in xs]
        tds = [td + _dot(td, x) for td, x in zip(tds, xs)]
    bs = [_dot(td, ao) for td, ao in zip(tds, aos)]
    b2s = [_dot(b, b) for b in bs]
    b3s = [_dot(b, b2) for b, b2 in zip(bs, b2s)]
    return [_dot(eye - b + b2 - b3, td) for b, b2, b3, td in zip(bs, b2s, b3s, tds)]


def _lane_cumsum(x, y, reverse, name):
    rr = x.shape[0]

    def body(x_ref, y_ref, o_ref):
        i = lax.broadcasted_iota(jnp.int32, (CHUNK, CHUNK), 0)
        j = lax.broadcasted_iota(jnp.int32, (CHUNK, CHUNK), 1)
        tri = ((i >= j) if reverse else (i <= j)).astype(f32)
        o_ref[...] = jnp.dot(x_ref[...] + y_ref[...], tri, precision=HI, preferred_element_type=f32)

    spec = pl.BlockSpec((rr, CHUNK), lambda: (0, 0))
    return pl.pallas_call(body, name=name, in_specs=[spec, spec], out_specs=spec,
                          out_shape=jax.ShapeDtypeStruct((rr, CHUNK), f32))(x, y)


def _gdn_specs(n_heads, nblk, rows, hp=1, rev=False):
    def blk(n):
        return nblk - 1 - n if rev else n

    def qkv(off):
        return pl.BlockSpec((rows, hp * LANES), lambda g, n: (
            lax.div(g * hp, n_heads) * nblk + blk(n), lax.div(off * n_heads + lax.rem(g * hp, n_heads), hp)))

    def per_head(last, mult=1):
        return pl.BlockSpec((hp, rows * mult, last), lambda g, n: (g, blk(n), 0))

    def row_vec():
        return pl.BlockSpec((hp, 1, 1, rows), lambda g, n: (g, blk(n), 0, 0))

    return qkv, per_head, row_vec


def _gdn_prep(qkv, cols, grow, bl, s, n_heads):
    rows = _group_rows(s)
    nblk = s // rows
    bh_n = bl * n_heads
    hp = PREP_HEADS_PER_STEP if n_heads % PREP_HEADS_PER_STEP == 0 else 1
    qkv_spec, ph, rv = _gdn_specs(n_heads, nblk, rows, hp)

    def body(k_ref, v_ref, cols_ref, grow_ref, u_ref, w_ref, t_ref):
        r, c, same = _group_masks(rows)
        causal = jnp.logical_and(same, r >= c)
        mats, rhs = [], []
        for h in range(hp):
            hs = slice(h * LANES, (h + 1) * LANES)
            k = k_ref[:, hs]
            gc = cols_ref[h, :, 0:1]
            beta = cols_ref[h, :, 1:2]
            kb = k * beta
            mats.append(jnp.where(r > c, _dot_nt(kb, k) * _decay(gc, grow_ref[h, 0], causal), 0.0))
            rhs.append((v_ref[:, hs] * beta, kb * jnp.exp(gc)))
        for h, tm in enumerate(_inv_unit_lower_many(mats, r, c)):
            tb = tm.astype(bf16)
            u_ref[h] = jnp.dot(tb, rhs[h][0].astype(bf16), preferred_element_type=f32)
            w_ref[h] = jnp.dot(tb, rhs[h][1].astype(bf16), preferred_element_type=f32)
            t_ref[h] = tb

    return _pc(body, name="gdn_prep", grid=(bh_n // hp, nblk),
               in_specs=[qkv_spec(1), qkv_spec(2), ph(2), rv()],
               out_specs=[ph(LANES), ph(LANES), ph(rows)],
               out_shape=[jax.ShapeDtypeStruct((bh_n, s, LANES), f32), jax.ShapeDtypeStruct((bh_n, s, LANES), f32),
                          jax.ShapeDtypeStruct((bh_n, s, rows), bf16)],
               sem=("parallel", "parallel"))(qkv, qkv, cols, grow)


def _gdn_scan(qkv, u, w, cols, grow, bl, s, n_heads):
    rows = _group_rows(s)
    g_chunks = rows // CHUNK
    nblk = s // rows
    bh_n = bl * n_heads
    d = n_heads * LANES
    hp = HEADS_PER_STEP if n_heads % HEADS_PER_STEP == 0 else 1
    qkv_spec, ph, rv = _gdn_specs(n_heads, nblk, rows, hp)

    def body(q_ref, k_ref, u_ref, w_ref, cols_ref, grow_ref, o_ref, vn_ref, ss_ref, s_scr):
        _acc_init(pl.program_id(1), s_scr)
        r, c, same = _group_masks(rows)
        causal = jnp.logical_and(same, r >= c)
        qs, ks, gcs, qgs, ps = [], [], [], [], []
        for h in range(hp):
            hs = slice(h * LANES, (h + 1) * LANES)
            q, k, gc = q_ref[:, hs], k_ref[:, hs], cols_ref[h, :, 0:1]
            qs.append(q)
            ks.append(k)
            gcs.append(gc)
            qgs.append(q * jnp.exp(gc))
            ps.append(_dot_nt(q, k) * _decay(gc, grow_ref[h, 0], causal))
        st = [s_scr[h] for h in range(hp)]
        o_state = [[] for _ in range(hp)]
        vns = [[] for _ in range(hp)]
        for ci in range(g_chunks):
            sl = slice(ci * CHUNK, (ci + 1) * CHUNK)
            for h in range(hp):
                ss_ref[h, ci * LANES:(ci + 1) * LANES, :] = st[h]
                vn = u_ref[h, sl, :] - _dot(w_ref[h, sl, :], st[h])
                vns[h].append(vn)
                o_state[h].append(_dot(qgs[h][sl], st[h]))
                gc = gcs[h][sl]
                g_last = gc[CHUNK - 1:CHUNK, :]
                st[h] = jnp.exp(g_last) * st[h] + _dot_tn(ks[h][sl] * jnp.exp(g_last - gc), vn)
        for h in range(hp):
            s_scr[h] = st[h]
            vn_all = jnp.concatenate(vns[h], axis=0)
            vn_ref[h] = vn_all
            o_ref[:, h * LANES:(h + 1) * LANES] = jnp.concatenate(o_state[h], axis=0) + _dot(ps[h], vn_all)

    return _pc(body, name="gdn_scan", grid=(bh_n // hp, nblk),
               in_specs=[qkv_spec(0), qkv_spec(1), ph(LANES), ph(LANES), ph(2), rv()],
               out_specs=[qkv_spec(0), ph(LANES), ph(LANES, mult=LANES // CHUNK)],
               out_shape=[jax.ShapeDtypeStruct((bl * s, d), f32), jax.ShapeDtypeStruct((bh_n, s, LANES), f32),
                          jax.ShapeDtypeStruct((bh_n, (s // CHUNK) * LANES, LANES), f32)],
               scratch=[pltpu.VMEM((hp, LANES, LANES), f32)],
               sem=("parallel", "arbitrary"))(qkv, qkv, u, w, cols, grow)


def _gdn_scan_bwd(do, qkv, w, vn, cols, grow, ss, bl, s, n_heads):
    rows = _group_rows(s)
    g_chunks = rows // CHUNK
    nblk = s // rows
    bh_n = bl * n_heads
    hp = HEADS_PER_STEP if n_heads % HEADS_PER_STEP == 0 else 1
    qkv_spec, ph, rv = _gdn_specs(n_heads, nblk, rows, hp, rev=True)

    def body(do_ref, q_ref, k_ref, w_ref, vn_ref, cols_ref, grow_ref, ss_ref,
             du_ref, dw_ref, dq_ref, dk_ref, dcol_ref, drow_ref, ds_scr):
        _acc_init(pl.program_id(1), ds_scr)
        r, c, same = _group_masks(rows)
        causal = jnp.logical_and(same, r >= c)
        last_row = lax.broadcasted_iota(jnp.int32, (CHUNK, 1), 0) == CHUNK - 1
        pre = []
        for h in range(hp):
            hs = slice(h * LANES, (h + 1) * LANES)
            do, q, k = do_ref[:, hs], q_ref[:, hs], k_ref[:, hs]
            vn = vn_ref[h]
            gc = cols_ref[h, :, 0:1]
            dmat = _decay(gc, grow_ref[h, 0], causal)
            gam = jnp.exp(gc)
            qk = _dot_nt(q, k)
            dpd = _dot_nt(do, vn) * dmat
            ep = dpd * qk
            drow_ref[h, 0] = -jnp.sum(ep, axis=0, keepdims=True)
            pre.append(dict(do=do, k=k, vn=vn, gc=gc, gam=gam, qg=q * gam,
                            dvn_intra=_dot_tn(qk * dmat, do), dq_intra=_dot(dpd, k), dk_intra=_dot_tn(dpd, q),
                            ep_rows=jnp.sum(ep, axis=-1, keepdims=True)))
        ds = [ds_scr[h] for h in range(hp)]
        for ci in reversed(range(g_chunks)):
            sl = slice(ci * CHUNK, (ci + 1) * CHUNK)
            for h in range(hp):
                pr = pre[h]
                st = ss_ref[h, ci * LANES:(ci + 1) * LANES, :]
                gc = pr["gc"][sl]
                g_last = gc[CHUNK - 1:CHUNK, :]
                gam_last = jnp.exp(g_last)
                kd_scale = jnp.exp(g_last - gc)
                kdec = pr["k"][sl] * kd_scale
                do_c = pr["do"][sl]
                qg_c = pr["qg"][sl]
                dvn = pr["dvn_intra"][sl] + _dot(kdec, ds[h])
                dkdec = _dot_nt(pr["vn"][sl], ds[h])
                du_ref[h, sl, :] = dvn
                dw_ref[h, sl, :] = -_dot_nt(dvn, st)
                dqg = _dot_nt(do_c, st)
                dq_ref[h, sl, :] = dqg * pr["gam"][sl] + pr["dq_intra"][sl]
                dk_ref[h, sl, :] = pr["dk_intra"][sl] + dkdec * kd_scale
                kd_rows = jnp.sum(dkdec * kdec, axis=-1, keepdims=True)
                extra = jnp.sum(kd_rows) + gam_last * jnp.sum(st * ds[h])
                dcol_ref[h, sl, :] = (jnp.sum(dqg * qg_c, axis=-1, keepdims=True) + pr["ep_rows"][sl] - kd_rows
                                      + jnp.where(last_row, extra, 0.0))
                ds[h] = gam_last * ds[h] + _dot_tn(qg_c, do_c) - _dot_tn(w_ref[h, sl, :], dvn)
        for h in range(hp):
            ds_scr[h] = ds[h]

    return _pc(body, name="gdn_scan_bwd", grid=(bh_n // hp, nblk),
               in_specs=[qkv_spec(0), qkv_spec(0), qkv_spec(1), ph(LANES), ph(LANES), ph(2), rv(),
                         ph(LANES, mult=LANES // CHUNK)],
               out_specs=[ph(LANES), ph(LANES), ph(LANES), ph(LANES), ph(1), rv()],
               out_shape=[jax.ShapeDtypeStruct((bh_n, s, LANES), f32)] * 4
               + [jax.ShapeDtypeStruct((bh_n, s, 1), f32), jax.ShapeDtypeStruct((bh_n, nblk, 1, rows), f32)],
               scratch=[pltpu.VMEM((hp, LANES, LANES), f32)],
               sem=("parallel", "arbitrary"))(do, qkv, qkv, w, vn, cols, grow, ss)


def _gdn_prep_bwd(qkv, cols, grow, tmat, du, dw, dk_scan, dcol_scan, drow_scan, bl, s, n_heads):
    rows = _group_rows(s)
    nblk = s // rows
    bh_n = bl * n_heads
    hp = PREP_HEADS_PER_STEP if n_heads % PREP_HEADS_PER_STEP == 0 else 1
    qkv_spec, ph, rv = _gdn_specs(n_heads, nblk, rows, hp)

    def body(k_ref, v_ref, cols_ref, grow_ref, t_ref, du_ref, dw_ref, dks_ref, dcs_ref, drs_ref,
             dk_ref, dv_ref, dcols_ref, drow_ref):
        r, c, same = _group_masks(rows)
        causal = jnp.logical_and(same, r >= c)
        hh = range(hp)
        ks = [k_ref[:, h * LANES:(h + 1) * LANES] for h in hh]
        vs = [v_ref[:, h * LANES:(h + 1) * LANES] for h in hh]
        gcs = [cols_ref[h, :, 0:1] for h in hh]
        betas = [cols_ref[h, :, 1:2] for h in hh]
        tms = [t_ref[h] for h in hh]
        dus = [du_ref[h] for h in hh]
        dws = [dw_ref[h] for h in hh]
        gams = [jnp.exp(gc) for gc in gcs]
        kbs = [k * b for k, b in zip(ks, betas)]
        kbgs = [kb * g for kb, g in zip(kbs, gams)]
        dts = [jnp.where(same, _dot_nt(dus[h], vs[h] * betas[h]) + _dot_nt(dws[h], kbgs[h]), 0.0) for h in hh]
        dvbs = [_dot_tn(tms[h], dus[h]) for h in hh]
        dkbgs = [_dot_tn(tms[h], dws[h]) for h in hh]
        kks = [_dot_nt(kbs[h], ks[h]) for h in hh]
        inner = [_dot_nt(dts[h], tms[h]) for h in hh]
        dads = [jnp.where(r > c, -_dot_tn(tms[h], inner[h]), 0.0) * _decay(gcs[h], grow_ref[h, 0], causal) for h in hh]
        dkbs = [dkbgs[h] * gams[h] + _dot(dads[h], ks[h]) for h in hh]
        dk2 = [_dot_tn(dads[h], kbs[h]) for h in hh]
        for h in hh:
            dk_ref[h] = dks_ref[h] + dk2[h] + dkbs[h] * betas[h]
            dv_ref[h] = dvbs[h] * betas[h]
            ea = dads[h] * kks[h]
            dcols_ref[h, :, 0:1] = (dcs_ref[h] + jnp.sum(dkbgs[h] * kbgs[h], axis=-1, keepdims=True)
                                    + jnp.sum(ea, axis=-1, keepdims=True))
            dcols_ref[h, :, 1:2] = (jnp.sum(dvbs[h] * vs[h], axis=-1, keepdims=True)
                                    + jnp.sum(dkbs[h] * ks[h], axis=-1, keepdims=True))
            drow_ref[h, 0] = drs_ref[h, 0] - jnp.sum(ea, axis=0, keepdims=True)

    return _pc(body, name="gdn_prep_bwd", grid=(bh_n // hp, nblk),
               in_specs=[qkv_spec(1), qkv_spec(2), ph(2), rv(), ph(rows), ph(LANES), ph(LANES), ph(LANES), ph(1), rv()],
               out_specs=[ph(LANES), ph(LANES), ph(2), rv()],
               out_shape=[jax.ShapeDtypeStruct((bh_n, s, LANES), f32), jax.ShapeDtypeStruct((bh_n, s, LANES), f32),
                          jax.ShapeDtypeStruct((bh_n, s, 2), f32), jax.ShapeDtypeStruct((bh_n, nblk, 1, rows), f32)],
               sem=("parallel", "parallel"))(qkv, qkv, cols, grow, tmat, du, dw, dk_scan, dcol_scan, drow_scan)


def _row(v):
    return v.reshape(1, -1).astype(f32)


def _pad_lanes(v):
    v = v.reshape(1, -1).astype(f32)
    return jnp.pad(v, ((0, 0), (0, LANES - v.shape[1])))


def _local_step(x, tgt, p, need, emit):
    bl, s, d = x.shape
    t = bl * s
    n_heads = p["gdn_a_log"].shape[-1]
    assert d == n_heads * LANES and s % CHUNK == 0
    x2 = x.reshape(t, d)
    tgt2 = tgt.reshape(t, d)
    gr = {}

    n0, ub, gl = _pw1_glu(x2, _row(p["norm_mix_g"][0]), p["cv_w_pw1"], _row(p["cv_b_pw1"]))
    dc = _dwconv_fwd(gl.reshape(bl, s, d), p["cv_w_dw"], _row(p["cv_b_dw"]), "dwconv_fwd").reshape(t, d)
    sb, h1 = _ln_silu_mm_res(dc, _row(p["cv_ln_g"]), _row(p["cv_ln_b"]), p["cv_w_pw2"], _row(p["cv_b_pw2"]), x2)
    m0 = need("mlp0", h1)
    n1, f0, r0, h2 = _mlp_fwd(h1, _row(p["norm_ffn_g"][0]), m0["w1"], m0["w2"], "mlp_fwd0")

    gd = need("gdn", h2)
    w_in = gd["w_in"]
    w_ab = jnp.pad(w_in[:, 4 * d:], ((0, 0), (0, LANES - 2 * n_heads)))
    a_log_pad = _pad_lanes(p["gdn_a_log"])
    dt_pad = _pad_lanes(p["gdn_dt_bias"])
    n2, qkv_pre, z, ab, gbeta = _gdn_in(h2, _row(p["norm_mix_g"][1]), w_in, w_ab, a_log_pad, dt_pad, n_heads)
    zero_bias = jnp.zeros((1, 3 * d), f32)
    qkv = _dwconv_fwd(qkv_pre.reshape(bl, s, 3 * d), p["gdn_conv_w"], zero_bias, "sconv_fwd", qk_heads=n_heads).reshape(t, 3 * d)
    bh_n, rows = bl * n_heads, _group_rows(s)
    gates = gbeta[:, :2 * n_heads].reshape(bl, s, 2, n_heads).transpose(2, 0, 3, 1).reshape(2, bh_n, s)
    g_lanes = gates[0].reshape(bh_n * (s // CHUNK), CHUNK)
    gc_lanes = _lane_cumsum(g_lanes, jnp.zeros_like(g_lanes), False, "gdn_gate_cumsum")
    grow = gc_lanes.reshape(bh_n, s // rows, 1, rows)
    cols = jnp.stack([gc_lanes.reshape(bh_n, s), gates[1]], axis=-1)
    u, w, tmat = _gdn_prep(qkv, cols, grow, bl, s, n_heads)
    o, vn, ss = _gdn_scan(qkv, u, w, cols, grow, bl, s, n_heads)
    onb, h3 = _gated_norm_mm_res(o, z, _row(p["gdn_norm_g"]), gd["w_out"], h2, n_heads)
    m1 = need("mlp1", h3)
    n3, f1, r1, loss_acc, dh4, dgf = _mlp_fwd(h3, _row(p["norm_ffn_g"][1]), m1["w1"], m1["w2"], "mlp_fwd1_loss",
                                              loss=(_row(p["final_norm_g"]), tgt2))
    loss = loss_acc[0, 0]
    gr["final_norm_g"] = dgf[0]

    df1, dh3, dg_ffn1, _ = _mlp_bwd(dh4, h3, _row(p["norm_ffn_g"][1]), f1, m1["w1"], m1["w2"], "mlp_bwd1")
    dw2_1 = _mm_tn(r1, dh4, "dw_mlp2_1", bf16)
    dw1_1 = _mm_tn_blocked(n3, df1, "dw_mlp1_1", bf16)

    dw_out = _mm_tn(onb, dh3, "dw_gdn_out", bf16)
    tie = emit("late", {"mlp_w2_1": dw2_1, "mlp_w1_1": dw1_1, "gdn_w_out": dw_out})
    do, dz, dng = _mm_nt([(dh3, gd["w_out"])], "dx_gdn_out",
                         _gated_norm_bwd_tail(o, z, _row(p["gdn_norm_g"]) + tie, n_heads))
    gr["gdn_norm_g"] = dng[0]
    du, dw_, dq, dk_scan, dcol_scan, drow_scan = _gdn_scan_bwd(do, qkv, w, vn, cols, grow, ss, bl, s, n_heads)
    dk, dv, dcols, drow = _gdn_prep_bwd(qkv, cols, grow, tmat, du, dw_, dk_scan, dcol_scan, drow_scan, bl, s, n_heads)
    dg_lanes = _lane_cumsum(dcols[..., 0].reshape(g_lanes.shape), drow.reshape(g_lanes.shape), True, "gdn_gate_cumsum_bwd")
    dgb2 = jnp.stack([dg_lanes.reshape(bh_n, s), dcols[..., 1]], axis=-1)
    dqkv_pre, dconv_w, _ = _dwconv_bwd(qkv_pre.reshape(bl, s, 3 * d), [dq, dk, dv], p["gdn_conv_w"], "sconv_bwd", qk_heads=n_heads)
    gr["gdn_conv_w"] = dconv_w[:p["gdn_conv_w"].shape[0]]
    dgb = dgb2.reshape(bl, n_heads, s, 2).transpose(0, 2, 3, 1).reshape(t, 2 * n_heads)
    dgb = jnp.pad(dgb, ((0, 0), (0, LANES - 2 * n_heads)))
    dab, dal, ddt = _gates_bwd(dgb, ab, a_log_pad, dt_pad, n_heads)
    gr["gdn_a_log"] = dal[0, :n_heads]
    gr["gdn_dt_bias"] = ddt[0, :n_heads]
    dqkv2 = dqkv_pre.reshape(t, 3 * d)
    dw_in = jnp.concatenate(
        [_mm_tn(n2, dqkv2, "dw_gdn_in_qkv"), _mm_tn(n2, dz, "dw_gdn_in_z"), _mm_tn(n2, dab, "dw_gdn_in_ab")[:, :2 * n_heads]], axis=1)
    tie = emit("gdn_in", {"gdn_w_in": dw_in})
    dh2, dg_mix1 = _mm_nt([(dqkv2, (w_in, 0)), (dz, (w_in, 3)), (dab, w_ab)], "dx_gdn_in",
                          _rms_bwd_tail(h2, _row(p["norm_mix_g"][1]) + tie, dh3))

    df0, dh1, dg_ffn0, cs_h1 = _mlp_bwd(dh2, h1, _row(p["norm_ffn_g"][0]), f0, m0["w1"], m0["w2"], "mlp_bwd0")
    dw2_0 = _mm_tn(r0, dh2, "dw_mlp2_0", bf16)
    dw1_0 = _mm_tn_blocked(n1, df0, "dw_mlp1_0", bf16)
    dw_pw2 = _mm_tn(sb, dh1, "dw_pw2", bf16)
    tie = emit("mlp0", {"mlp_w2_0": dw2_0, "mlp_w1_0": dw1_0, "cv_w_pw2": dw_pw2})
    gr["norm_ffn_g"] = jnp.stack([dg_ffn0[0], dg_ffn1[0]])

    gr["cv_b_pw2"] = cs_h1[0]
    ddc, dlng, dlnb, cs_dc = _mm_nt([(dh1, p["cv_w_pw2"])], "dx_pw2",
                                    _ln_silu_bwd_tail(dc, _row(p["cv_ln_g"]) + tie, _row(p["cv_ln_b"])))
    gr["cv_ln_g"] = dlng[0]
    gr["cv_ln_b"] = dlnb[0]
    gr["cv_b_dw"] = cs_dc[0]
    dgl, dw_dw, _ = _dwconv_bwd(gl.reshape(bl, s, d), [ddc.reshape(bl, s, d)], p["cv_w_dw"], "dwconv_bwd")
    gr["cv_w_dw"] = dw_dw[:p["cv_w_dw"].shape[0]]
    dub, cs_u = _glu_bwd(dgl.reshape(t, d), ub)
    gr["cv_b_pw1"] = cs_u[0]
    gr["cv_w_pw1"] = _mm_tn_blocked(n0, dub, "dw_pw1", bf16)
    dx, dg_mix0 = _mm_nt([(dub, ("blocks", p["cv_w_pw1"]))], "dx_pw1", _rms_bwd_tail(x2, _row(p["norm_mix_g"][0]), dh1))
    gr["norm_mix_g"] = jnp.stack([dg_mix0[0], dg_mix1[0]])
    return loss, dx.reshape(bl, s, d), gr


ANY = pl.BlockSpec(memory_space=pl.ANY)
MESH = pl.DeviceIdType.MESH


def _flip(v, bit):
    return 1 - v if bit else v


def _all_gather_many(shards):
    na = len(shards)

    def body(*refs):
        x_refs, o_refs = refs[:na], refs[na:2 * na]
        send_sems, recv_sems, local_sems = refs[2 * na:]
        x, y, c = lax.axis_index("x"), lax.axis_index("y"), lax.axis_index("c")
        me, sibling = (x, y, c), (x, y, 1 - c)
        chips = [(1 - x, y), (x, 1 - y), (1 - x, 1 - y)]

        def copy(a, k, block, to, src=None):
            px, py, pc = block
            dst = o_refs[a].at[4 * px + 2 * py + pc]
            return pltpu.make_async_remote_copy(
                src_ref=dst if src is None else src, dst_ref=dst,
                send_sem=send_sems.at[a, k], recv_sem=recv_sems.at[a, k], device_id=to, device_id_type=MESH)

        mine = [pltpu.make_async_copy(x_refs[a], o_refs[a].at[4 * x + 2 * y + c], local_sems.at[a]) for a in range(na)]
        first = []
        for a in range(na):
            first.append(copy(a, 0, me, sibling, src=x_refs[a]))
            first += [copy(a, 1 + j, me, (*chip, c), src=x_refs[a]) for j, chip in enumerate(chips)]
        for cp in mine + first:
            cp.start()
        passed = []
        for j, chip in enumerate(chips):
            for a in range(na):
                copy(a, 1 + j, (*chip, c), me).wait_recv()
                fwd = copy(a, 4 + j, (*chip, c), sibling)
                fwd.start()
                passed.append(fwd)
        for a in range(na):
            copy(a, 0, sibling, me).wait_recv()
        for j, chip in enumerate(chips):
            for a in range(na):
                copy(a, 4 + j, (*chip, 1 - c), me).wait_recv()
        for cp in first + passed:
            cp.wait_send()
        for cp in mine:
            cp.wait()

    return pl.pallas_call(
        body, name="weights_all_gather",
        out_shape=[jax.ShapeDtypeStruct((N_DEV,) + a.shape, a.dtype) for a in shards],
        in_specs=[ANY] * na, out_specs=[ANY] * na,
        scratch_shapes=[pltpu.SemaphoreType.DMA((na, 7)), pltpu.SemaphoreType.DMA((na, 7)), pltpu.SemaphoreType.DMA((na,))],
        compiler_params=pltpu.CompilerParams(has_side_effects=True),
    )(*shards)


HBM = pl.BlockSpec(memory_space=pltpu.HBM)
SEM = pl.BlockSpec(memory_space=pltpu.SEMAPHORE)
EFFECT = pltpu.SideEffectType.DATAFLOW_SIDE_EFFECTING
N_PEERS = N_DEV - 1


def _exchange_copies(src_refs, land_refs, send_sems, recv_sems, scatter):
    x, y, c = lax.axis_index("x"), lax.axis_index("y"), lax.axis_index("c")
    me = 4 * x + 2 * y + c
    copies = []
    for a, (src, land) in enumerate(zip(src_refs, land_refs)):
        for k in range(1, N_DEV):
            px, py, pc = _flip(x, k & 4), _flip(y, k & 2), _flip(c, k & 1)
            i = a * N_PEERS + k - 1
            copies.append(pltpu.make_async_remote_copy(
                src_ref=src.at[4 * px + 2 * py + pc] if scatter[a] else src, dst_ref=land.at[me],
                send_sem=send_sems.at[i], recv_sem=recv_sems.at[i], device_id=(px, py, pc), device_id_type=MESH))
    return copies


def _exchange_start(srcs, scatter, name):
    na = len(srcs)
    lands = [lax.empty(s.shape if sc else (N_DEV,) + s.shape, s.dtype) for s, sc in zip(srcs, scatter)]

    def body(*refs):
        copies = _exchange_copies(refs[:na], refs[na:2 * na], refs[2 * na], refs[2 * na + 1], scatter)
        for cp in copies:
            cp.start()
        token = refs[-1]
        token[...] = jnp.zeros_like(token)

    outs = pl.pallas_call(
        body, name=name,
        out_shape=(pltpu.SemaphoreType.DMA((na * N_PEERS,)), pltpu.SemaphoreType.DMA((na * N_PEERS,)))
        + tuple(pltpu.HBM(a.shape, a.dtype) for a in srcs + lands) + (jax.ShapeDtypeStruct((SUBLANES, LANES), f32),),
        in_specs=[HBM] * (2 * na),
        out_specs=(SEM, SEM) + (HBM,) * (2 * na) + (pl.BlockSpec(memory_space=pltpu.VMEM),),
        input_output_aliases={i: 2 + i for i in range(2 * na)},
        compiler_params=pltpu.CompilerParams(has_side_effects=EFFECT),
    )(*[pltpu.with_memory_space_constraint(a, pltpu.HBM) for a in srcs + lands])
    return outs[0], outs[1], list(outs[2:2 + na]), list(outs[2 + na:2 + 2 * na]), outs[-1]


def _exchange_wait(started, after, scatter, name):
    send_sems, recv_sems, srcs, lands, _ = started
    na = len(srcs)

    def body(*refs):
        for cp in _exchange_copies(refs[:na], refs[na:2 * na], refs[2 * na], refs[2 * na + 1], scatter):
            cp.wait_send()
            cp.wait_recv()

    outs = pl.pallas_call(
        body, name=name,
        out_shape=tuple(pltpu.HBM(a.shape, a.dtype) for a in srcs + lands),
        in_specs=[HBM] * (2 * na) + [SEM, SEM, ANY], out_specs=(HBM,) * (2 * na),
        input_output_aliases={i: i for i in range(2 * na)},
        compiler_params=pltpu.CompilerParams(has_side_effects=EFFECT),
    )(*srcs, *lands, send_sems, recv_sems, after)
    return list(outs[:na]), list(outs[na:])


def _own_block(land, block, me):
    return lax.dynamic_update_index_in_dim(land, block, me, 0)


def _sum8_adamw(r2, w, m, v, layer, name):
    _, rr, cc = r2.shape
    tr = _tile(rr, 256)
    bc1 = 1.0 - ADAM_B1 ** ADAM_STEP
    bc2 = 1.0 - ADAM_B2 ** ADAM_STEP

    def body(r_ref, w_ref, m_ref, v_ref, g_ref, d_ref, nm_ref, nv_ref):
        gv = r_ref[0].astype(f32)
        for q in range(1, N_DEV):
            gv = gv + r_ref[q].astype(f32)
        g_ref[...] = gv
        nm = ADAM_B1 * m_ref[...] + (1.0 - ADAM_B1) * gv
        nv = ADAM_B2 * v_ref[...] + (1.0 - ADAM_B2) * (gv * gv)
        nm_ref[...] = nm
        nv_ref[...] = nv
        d_ref[...] = -ADAM_LR * ((nm / bc1) / (jnp.sqrt(nv / bc2) + ADAM_EPS) + ADAM_WD * w_ref[...])

    lspec = pl.BlockSpec((None, tr, cc), lambda i: (layer, i, 0))
    return _pc(body, name=name, grid=(rr // tr,),
               in_specs=[pl.BlockSpec((N_DEV, tr, cc), lambda i: (0, i, 0)), lspec, lspec, lspec],
               out_specs=[_rows(tr, cc)] * 4, out_shape=[jax.ShapeDtypeStruct((rr, cc), f32)] * 4,
               sem=("parallel",))(r2, w, m, v)


def _sum_devices(recv):
    _, rr, _ = recv.shape
    tr = _tile(rr, 512)

    def body(r_ref, o_ref):
        acc = r_ref[0]
        for i in range(1, N_DEV):
            acc = acc + r_ref[i]
        o_ref[...] = acc

    return _pc(body, name="grads_sum", grid=(rr // tr,),
               in_specs=[pl.BlockSpec((N_DEV, tr, LANES), lambda i: (0, i, 0))],
               out_specs=_rows(tr, LANES), out_shape=jax.ShapeDtypeStruct((rr, LANES), f32), sem=("parallel",))(recv)


def _adamw(w, g, m, v):
    rr = w.shape[0]
    tr = _tile(rr, 512)
    bc1 = 1.0 - ADAM_B1 ** ADAM_STEP
    bc2 = 1.0 - ADAM_B2 ** ADAM_STEP

    def body(w_ref, g_ref, m_ref, v_ref, d_ref, nm_ref, nv_ref):
        gv = g_ref[...]
        nm = ADAM_B1 * m_ref[...] + (1.0 - ADAM_B1) * gv
        nv = ADAM_B2 * v_ref[...] + (1.0 - ADAM_B2) * (gv * gv)
        nm_ref[...] = nm
        nv_ref[...] = nv
        d_ref[...] = -ADAM_LR * ((nm / bc1) / (jnp.sqrt(nv / bc2) + ADAM_EPS) + ADAM_WD * w_ref[...])

    spec = _rows(tr, LANES)
    return _pc(body, name="adamw", grid=(rr // tr,), in_specs=[spec] * 4, out_specs=[spec] * 3,
               out_shape=[jax.ShapeDtypeStruct((rr, LANES), f32)] * 3, sem=("parallel",))(w, g, m, v)


PACK_ROWS = 512
PART_ROWS = SUBLANES


def _pack(arrs):
    parts, sizes = [], []
    for a in arrs:
        flat = a.reshape(-1)
        n = flat.shape[0]
        rows = -(-n // (LANES * PART_ROWS)) * PART_ROWS
        if rows * LANES != n:
            flat = jnp.pad(flat, (0, rows * LANES - n))
        parts.append(flat.reshape(rows, LANES))
        sizes.append((rows, n))
    total = sum(r for r, _ in sizes)
    padded = -(-total // PACK_ROWS) * PACK_ROWS
    if padded > total:
        parts.append(jnp.zeros((padded - total, LANES), parts[0].dtype))
    return jnp.concatenate(parts, axis=0), sizes


def _unpack(packed, sizes, shapes):
    out, off = [], 0
    for (rows, n), shp in zip(sizes, shapes):
        piece = lax.slice_in_dim(packed, off, off + rows, axis=0)
        if rows * LANES != n:
            piece = lax.slice_in_dim(piece.reshape(-1), 0, n, axis=0)
        out.append(piece.reshape(tuple(shp)))
        off += rows
    return out


def _cols_to_blocks(a):
    n = a.shape[-1] // N_DEV
    a = a.reshape(a.shape[:-1] + (N_DEV, n))
    return jnp.moveaxis(a, -2, 0)


def _blocks_to_cols(a):
    a = jnp.moveaxis(a, 0, -2)
    return a.reshape(a.shape[:-2] + (a.shape[-2] * a.shape[-1],))


def _rows_to_blocks(a):
    k = a.shape[-2] // N_DEV
    a = a.reshape(a.shape[:-2] + (N_DEV, k, a.shape[-1]))
    return jnp.moveaxis(a, -3, 0)


def _blocks_to_rows(a):
    a = jnp.moveaxis(a, 0, -3)
    return a.reshape(a.shape[:-3] + (a.shape[-3] * a.shape[-2], a.shape[-1]))


COL_SHARDED = ("cv_w_pw1", "gdn_w_in", "mlp_w1")
ROW_SHARDED = ("cv_w_pw2", "gdn_w_out", "mlp_w2")
CONV_SHARDED = ("cv_w_dw", "gdn_conv_w")
REPLICATED = ("norm_mix_g", "norm_ffn_g", "final_norm_g", "cv_b_pw1", "cv_b_dw", "cv_ln_g", "cv_ln_b", "cv_b_pw2",
              "gdn_a_log", "gdn_dt_bias", "gdn_norm_g")
WEIGHTS = ("norm_mix_g", "norm_ffn_g", "final_norm_g", "cv_w_pw1", "cv_b_pw1", "cv_w_dw", "cv_b_dw", "cv_ln_g",
           "cv_ln_b", "cv_w_pw2", "cv_b_pw2", "gdn_w_in", "gdn_conv_w", "gdn_a_log", "gdn_dt_bias", "gdn_norm_g",
           "gdn_w_out", "mlp_w1", "mlp_w2")
MATMUL_SHARDED = COL_SHARDED + ROW_SHARDED


def _squeeze_layer(name, a):
    if name in ("norm_mix_g", "norm_ffn_g", "final_norm_g", "mlp_w1", "mlp_w2"):
        return a
    return a[0]


def _gather_weights(shards):
    me = 4 * lax.axis_index("x") + 2 * lax.axis_index("y") + lax.axis_index("c")
    first = ("cv_w_pw1", "cv_w_pw2") + CONV_SHARDED
    got = dict(zip(first, _all_gather_many([shards[n] if n in CONV_SHARDED else shards[n].astype(bf16) for n in first])))
    now = {"cv_w_pw1": got["cv_w_pw1"], "cv_w_pw2": _blocks_to_rows(got["cv_w_pw2"])}
    for n in CONV_SHARDED:
        now[n] = _blocks_to_cols(got[n])

    def cast(a, tie):
        return (a + tie).astype(bf16)

    token, _ = lax.optimization_barrier((jnp.zeros((), f32), got["cv_w_pw1"]))
    later, started = {}, {}
    for group in ("mlp0", "gdn", "mlp1"):
        if group == "gdn":
            srcs = [cast(shards["gdn_w_in"], token).reshape(-1, LANES), cast(shards["gdn_w_out"], token)]
        else:
            layer = int(group[-1])
            srcs = [cast(shards["mlp_w1"][layer], token), cast(shards["mlp_w2"][layer], token)]
        later[group] = srcs
        started[group] = _exchange_start(srcs, [False] * len(srcs), f"weights_{group}_start")
        token = started[group][4][0, 0]

    def need(group, after):
        srcs, lands = _exchange_wait(started[group], after, [False] * len(later[group]), f"weights_{group}_wait")
        lands = [_own_block(ld, own, me) for ld, own in zip(lands, srcs)]
        if group == "gdn":
            w_in = _blocks_to_cols(lands[0].reshape((N_DEV,) + shards["gdn_w_in"].shape))
            return {"w_in": w_in, "w_out": _blocks_to_rows(lands[1])}
        return {"w1": lands[0], "w2": lands[1]}

    return now, need, token


def kernel(x, norm_mix_g, norm_ffn_g, final_norm_g, cv_w_pw1, cv_b_pw1, cv_w_dw, cv_b_dw, cv_ln_g, cv_ln_b, cv_w_pw2, cv_b_pw2, gdn_w_in, gdn_conv_w, gdn_a_log, gdn_dt_bias, gdn_norm_g, gdn_w_out, mlp_w1, mlp_w2, loss_target, m_norm_mix_g, m_norm_ffn_g, m_final_norm_g, m_cv_w_pw1, m_cv_b_pw1, m_cv_w_dw, m_cv_b_dw, m_cv_ln_g, m_cv_ln_b, m_cv_w_pw2, m_cv_b_pw2, m_gdn_w_in, m_gdn_conv_w, m_gdn_a_log, m_gdn_dt_bias, m_gdn_norm_g, m_gdn_w_out, m_mlp_w1, m_mlp_w2, v_norm_mix_g, v_norm_ffn_g, v_final_norm_g, v_cv_w_pw1, v_cv_b_pw1, v_cv_w_dw, v_cv_b_dw, v_cv_ln_g, v_cv_ln_b, v_cv_w_pw2, v_cv_b_pw2, v_gdn_w_in, v_gdn_conv_w, v_gdn_a_log, v_gdn_dt_bias, v_gdn_norm_g, v_gdn_w_out, v_mlp_w1, v_mlp_w2):
    w_in = dict(zip(WEIGHTS, (norm_mix_g, norm_ffn_g, final_norm_g, cv_w_pw1, cv_b_pw1, cv_w_dw, cv_b_dw, cv_ln_g, cv_ln_b, cv_w_pw2, cv_b_pw2, gdn_w_in, gdn_conv_w, gdn_a_log, gdn_dt_bias, gdn_norm_g, gdn_w_out, mlp_w1, mlp_w2)))
    m_in = dict(zip(WEIGHTS, (m_norm_mix_g, m_norm_ffn_g, m_final_norm_g, m_cv_w_pw1, m_cv_b_pw1, m_cv_w_dw, m_cv_b_dw, m_cv_ln_g, m_cv_ln_b, m_cv_w_pw2, m_cv_b_pw2, m_gdn_w_in, m_gdn_conv_w, m_gdn_a_log, m_gdn_dt_bias, m_gdn_norm_g, m_gdn_w_out, m_mlp_w1, m_mlp_w2)))
    v_in = dict(zip(WEIGHTS, (v_norm_mix_g, v_norm_ffn_g, v_final_norm_g, v_cv_w_pw1, v_cv_b_pw1, v_cv_w_dw, v_cv_b_dw, v_cv_ln_g, v_cv_ln_b, v_cv_w_pw2, v_cv_b_pw2, v_gdn_w_in, v_gdn_conv_w, v_gdn_a_log, v_gdn_dt_bias, v_gdn_norm_g, v_gdn_w_out, v_mlp_w1, v_mlp_w2)))
    me = 4 * lax.axis_index("x") + 2 * lax.axis_index("y") + lax.axis_index("c")

    shards = {n: _squeeze_layer(n, w_in[n]) for n in WEIGHTS}
    first, need, token = _gather_weights(shards)
    params = {n: shards[n] for n in REPLICATED}
    params.update(first)
    params["norm_mix_g"] = params["norm_mix_g"] + token

    def row_blocks(a):
        return a.reshape(N_DEV, a.shape[0] // N_DEV, a.shape[1])

    def flat_blocks(a):
        k, n8 = a.shape
        return _cols_to_blocks(a).reshape(N_DEV, k * (n8 // N_DEV) // LANES, LANES)

    def as_blocks(n, a):
        if n == "gdn_w_in":
            return flat_blocks(a).astype(bf16)
        return a if a.ndim == 3 else row_blocks(a)

    sent = []

    def emit(group, grads_out):
        names = list(grads_out)
        blocks = [as_blocks(n, grads_out[n]) for n in names]
        st = _exchange_start(blocks, [True] * len(blocks), f"grads_{group}_start")
        sent.append((names, blocks, st, [True] * len(blocks)))
        return st[4][0, 0]

    loss_part, grad_x, gr = _local_step(x, loss_target, params, need, emit)
    loss = lax.psum(loss_part, ("x", "y", "c"))

    small = REPLICATED + CONV_SHARDED
    small_send, ssz = _pack([gr[n] for n in small])
    last_blocks = [gr["cv_w_pw1"], small_send]
    last_scatter = [True, False]
    sent.append((["cv_w_pw1", "small"], last_blocks, _exchange_start(last_blocks, last_scatter, "grads_last_start"), last_scatter))

    recv = {}

    def finish(entry, after):
        names, blocks, st, scatter = entry
        srcs, lands = _exchange_wait(st, after, scatter, f"grads_{names[0]}_wait")
        for n, ld, blk, sc in zip(names, lands, srcs, scatter):
            own = lax.dynamic_index_in_dim(blk, me, 0, keepdims=False) if sc else blk
            recv[n] = _own_block(ld, own, me)
        return lands[0]

    def as3d(n, a):
        if n == "gdn_w_in":
            return a.reshape(a.shape[0], -1, LANES)
        return a

    big = [("cv_w_pw2", 0, "cv_w_pw2"), ("gdn_w_in", 0, "gdn_w_in"), ("gdn_w_out", 0, "gdn_w_out"),
           ("mlp_w1", 0, "mlp_w1_0"), ("mlp_w1", 1, "mlp_w1_1"), ("mlp_w2", 0, "mlp_w2_0"), ("mlp_w2", 1, "mlp_w2_1"),
           ("cv_w_pw1", 0, "cv_w_pw1")]
    res = {n: {} for n in MATMUL_SHARDED}
    after = grad_x
    for entry in sent[:-1]:
        after = finish(entry, after)
    for n, layer, key in big[:-1]:
        res[n][layer] = _sum8_adamw(recv[key], as3d(n, w_in[n]), as3d(n, m_in[n]), as3d(n, v_in[n]), layer, f"adamw_{key}")
        after = res[n][layer][0]
    finish(sent[-1], after)
    n, layer, key = big[-1]
    res[n][layer] = _sum8_adamw(recv[key], as3d(n, w_in[n]), as3d(n, m_in[n]), as3d(n, v_in[n]), layer, f"adamw_{key}")

    small_sum = _sum_devices(recv["small"])
    grads = dict(zip(small, _unpack(small_sum, ssz, [gr[n].shape for n in small])))
    for n in CONV_SHARDED:
        cn = shards[n].shape[-1]
        grads[n] = lax.dynamic_slice_in_dim(grads[n], me * cn, cn, axis=1)

    out_groups = {n: [] for n in WEIGHTS}
    for n in MATMUL_SHARDED:
        layers = sorted(res[n])
        for k in range(4):
            pieces = [res[n][layer][k] for layer in layers]
            out_groups[n].append(jnp.stack(pieces).reshape(w_in[n].shape))

    sm_w = [shards[n] for n in small]
    sm_g = [grads[n].reshape(shards[n].shape) for n in small]
    sm_m = [_squeeze_layer(n, m_in[n]) for n in small]
    sm_v = [_squeeze_layer(n, v_in[n]) for n in small]
    wp, psz = _pack(sm_w)
    gp, _ = _pack(sm_g)
    mp, _ = _pack(sm_m)
    vp, _ = _pack(sm_v)
    dp, nmp, nvp = _adamw(wp, gp, mp, vp)
    shp = [a.shape for a in sm_w]
    for n, g, dl, nm, nv in zip(small, sm_g, _unpack(dp, psz, shp), _unpack(nmp, psz, shp), _unpack(nvp, psz, shp)):
        out_groups[n] = [a.reshape(w_in[n].shape) for a in (g, dl, nm, nv)]

    outs = [loss, grad_x]
    for k in range(4):
        outs += [out_groups[n][k] for n in WEIGHTS]
    return tuple(outs)
```

```python
import functools

import jax
import jax.numpy as jnp
from jax import lax
from jax.experimental import pallas as pl
from jax.experimental.pallas import tpu as pltpu

f32, bf16 = jnp.float32, jnp.bfloat16

NORM_EPS = 1e-6
L2_EPS = 1e-6
CHUNK = 64
LANES = 128
SUBLANES = 8
N_DEV = 8
VMEM_LIMIT = 56 * 1024 * 1024
CONV_PAD = 32
HEADS_PER_STEP = 4
PREP_HEADS_PER_STEP = 4
NEG = -1e30

ADAM_LR, ADAM_B1, ADAM_B2, ADAM_EPS, ADAM_WD, ADAM_STEP = 0.001, 0.9, 0.999, 1e-08, 0.01, 10

NT = (((1,), (1,)), ((), ()))
TN = (((0,), (0,)), ((), ()))
HI = lax.Precision.HIGHEST


def _pc(body, *, name, grid, in_specs, out_specs, out_shape, scratch=(), sem=None):
    return pl.pallas_call(
        body, name=name, grid=grid, in_specs=in_specs, out_specs=out_specs, out_shape=out_shape,
        scratch_shapes=list(scratch),
        compiler_params=pltpu.CompilerParams(dimension_semantics=sem, vmem_limit_bytes=VMEM_LIMIT))


def _rows(tm, n):
    return pl.BlockSpec((tm, n), lambda i: (i, 0))


def _const(shape):
    return pl.BlockSpec(shape, lambda *_: (0,) * len(shape))


def _resident(shape):
    return pl.BlockSpec(shape, lambda *_: (0,) * len(shape), pipeline_mode=pl.Buffered(1))


def _tile(t, pref):
    return pref if t % pref == 0 else t


def _dot(a, b):
    return jnp.dot(a.astype(bf16), b.astype(bf16), preferred_element_type=f32)


def _dot_nt(a, b):
    return lax.dot_general(a.astype(bf16), b.astype(bf16), NT, preferred_element_type=f32)


def _dot_tn(a, b):
    return lax.dot_general(a.astype(bf16), b.astype(bf16), TN, preferred_element_type=f32)


def _sigmoid(x):
    return 1.0 / (1.0 + jnp.exp(-x))


def _silu_grad(x):
    s = _sigmoid(x)
    return s * (1.0 + x * (1.0 - s))


def _rms(x, g):
    rstd = lax.rsqrt(jnp.mean(x * x, axis=-1, keepdims=True) + NORM_EPS)
    xh = x * rstd
    return xh * g, xh, rstd


def _rms_bwd(dn, xh, rstd, g):
    dxh = dn * g
    return rstd * (dxh - xh * jnp.mean(dxh * xh, axis=-1, keepdims=True))


def _acc_init(step, *refs):
    @pl.when(step == 0)
    def _():
        for r in refs:
            r[...] = jnp.zeros(r.shape, r.dtype)


def _acc_rows(ref, val):
    ref[0:1, :] += jnp.sum(val, axis=0, keepdims=True)


def _pw1_glu(x, g, w, b):
    t, d = x.shape
    tm = _tile(t, 256)

    nb_w = w.shape[0]

    def body(x_ref, g_ref, w_hbm, b_ref, n_ref, u_ref, gl_ref, w_ref, sems):
        _fetch_blocks(pl.program_id(0), w_hbm, w_ref, sems, True)
        n, _, _ = _rms(x_ref[...], g_ref[...])
        nb = n.astype(bf16)
        n_ref[...] = nb
        u = jnp.dot(nb, w_ref[...], preferred_element_type=f32) + b_ref[...]
        u_ref[...] = u.astype(bf16)
        gl_ref[...] = u[:, :d] * _sigmoid(u[:, d:])

    return _pc(body, name="pw1_glu", grid=(t // tm,),
               in_specs=[_rows(tm, d), _const((1, d)), ANY, _const((1, 2 * d))],
               out_specs=[_rows(tm, d), _rows(tm, 2 * d), _rows(tm, d)],
               out_shape=[jax.ShapeDtypeStruct((t, d), bf16), jax.ShapeDtypeStruct((t, 2 * d), bf16),
                          jax.ShapeDtypeStruct((t, d), f32)],
               scratch=[pltpu.VMEM((d, 2 * d), bf16), pltpu.SemaphoreType.DMA((nb_w,))],
               sem=("arbitrary",))(x, g, w, b)


def _ln_silu_mm_res(dc, ln_g, ln_b, w, b, res):
    t, d = dc.shape
    tm = _tile(t, 256)

    def body(x_ref, g_ref, bb_ref, w_ref, b_ref, r_ref, s_ref, o_ref):
        x = x_ref[...]
        xc = x - jnp.mean(x, axis=-1, keepdims=True)
        rstd = lax.rsqrt(jnp.mean(xc * xc, axis=-1, keepdims=True) + NORM_EPS)
        ln = xc * rstd * g_ref[...] + bb_ref[...]
        sb = (ln * _sigmoid(ln)).astype(bf16)
        s_ref[...] = sb
        o_ref[...] = r_ref[...] + jnp.dot(sb, w_ref[...], preferred_element_type=f32) + b_ref[...]

    return _pc(body, name="ln_silu_pw2", grid=(t // tm,),
               in_specs=[_rows(tm, d), _const((1, d)), _const((1, d)), _resident((d, d)), _const((1, d)), _rows(tm, d)],
               out_specs=[_rows(tm, d), _rows(tm, d)],
               out_shape=[jax.ShapeDtypeStruct((t, d), bf16), jax.ShapeDtypeStruct((t, d), f32)],
               sem=("parallel",))(dc, ln_g, ln_b, w, b, res)


def _fetch_blocks(step, w_hbm, dst, sems, by_cols, layer=None):
    nb_w = w_hbm.shape[0]
    step_rows, step_cols = w_hbm.shape[-2], w_hbm.shape[-1]

    @pl.when(step == 0)
    def _():
        copies = []
        for j in range(nb_w):
            src = w_hbm.at[j] if layer is None else w_hbm.at[j, layer]
            if by_cols:
                part = dst.at[:, pl.ds(j * step_cols, step_cols)]
            else:
                part = dst.at[pl.ds(j * step_rows, step_rows), :]
            copies.append(pltpu.make_async_copy(src, part, sems.at[j]))
        for cp in copies:
            cp.start()
        for cp in copies:
            cp.wait()


def _mlp_fwd(h, g, w1g, w2g, name, loss=None):
    t, d = h.shape
    nb_w, _, bn = w1g.shape
    ff = nb_w * bn
    tm = _tile(t, 256)
    n_in = 4 if loss is None else 6
    n_out = 4 if loss is None else 6

    def body(*refs):
        h_ref, g_ref, w1_hbm, w2_hbm = refs[:4]
        n_ref, f_ref, r_ref = refs[n_in:n_in + 3]
        w1_ref, w2_ref, sem1, sem2 = refs[n_in + n_out:]
        _fetch_blocks(pl.program_id(0), w1_hbm, w1_ref, sem1, True)
        _fetch_blocks(pl.program_id(0), w2_hbm, w2_ref, sem2, False)
        hv = h_ref[...]
        n, _, _ = _rms(hv, g_ref[...])
        nb = n.astype(bf16)
        n_ref[...] = nb
        f = jnp.dot(nb, w1_ref[...], preferred_element_type=f32)
        f_ref[...] = f.astype(bf16)
        rb = jnp.square(jnp.maximum(f, 0.0)).astype(bf16)
        r_ref[...] = rb
        out = hv + jnp.dot(rb, w2_ref[...], preferred_element_type=f32)
        if loss is None:
            refs[n_in + 3][...] = out
        else:
            gf_ref, t_ref = refs[4:6]
            loss_ref, dh_ref, dg_ref = refs[n_in + 3:n_in + 6]
            _acc_init(pl.program_id(0), loss_ref, dg_ref)
            gv = gf_ref[...]
            y, xh, rstd = _rms(out, gv)
            e = y - t_ref[...]
            loss_ref[...] += 0.5 * jnp.sum(jnp.mean(e * e, axis=-1, keepdims=True))
            dy = e * (1.0 / d)
            _acc_rows(dg_ref, dy * xh)
            dh_ref[...] = _rms_bwd(dy, xh, rstd, gv)

    in_specs = [_rows(tm, d), _const((1, d)), ANY, ANY]
    out_specs = [_rows(tm, d), _rows(tm, ff), _rows(tm, ff)]
    out_shape = [jax.ShapeDtypeStruct((t, d), bf16), jax.ShapeDtypeStruct((t, ff), bf16), jax.ShapeDtypeStruct((t, ff), bf16)]
    args = [h, g, w1g, w2g]
    if loss is None:
        out_specs.append(_rows(tm, d))
        out_shape.append(jax.ShapeDtypeStruct((t, d), f32))
    else:
        in_specs += [_const((1, d)), _rows(tm, d)]
        args += list(loss)
        out_specs += [_const((SUBLANES, LANES)), _rows(tm, d), _const((SUBLANES, d))]
        out_shape += [jax.ShapeDtypeStruct((SUBLANES, LANES), f32), jax.ShapeDtypeStruct((t, d), f32),
                      jax.ShapeDtypeStruct((SUBLANES, d), f32)]
    return _pc(body, name=name, grid=(t // tm,), in_specs=in_specs, out_specs=out_specs, out_shape=out_shape,
               scratch=[pltpu.VMEM((d, ff), bf16), pltpu.VMEM((ff, d), bf16),
                        pltpu.SemaphoreType.DMA((nb_w,)), pltpu.SemaphoreType.DMA((nb_w,))],
               sem=("arbitrary",))(*args)


def _softplus(x):
    return jnp.maximum(x, 0.0) + jnp.log(1.0 + jnp.exp(-jnp.abs(x)))


def _gdn_in(h, g, w_main, w_ab, a_log_pad, dt_pad, n_heads):
    t, d = h.shape
    tm = _tile(t, 256)

    def body(h_ref, g_ref, wm_ref, wab_ref, al_ref, dt_ref, n_ref, qkv_ref, z_ref, ab_ref, gb_ref):
        n, _, _ = _rms(h_ref[...], g_ref[...])
        nb = n.astype(bf16)
        n_ref[...] = nb
        p = jnp.dot(nb, wm_ref[...], preferred_element_type=f32)
        qkv_ref[...] = p[:, :3 * d]
        z_ref[...] = p[:, 3 * d:]
        ab = jnp.dot(nb, wab_ref[...], preferred_element_type=f32)
        ab_ref[...] = ab
        lane = lax.broadcasted_iota(jnp.int32, ab.shape, 1)
        decay = -jnp.exp(al_ref[...]) * _softplus(ab + dt_ref[...])
        gb_ref[...] = jnp.where(lane < n_heads, decay, jnp.where(lane < 2 * n_heads, _sigmoid(ab), 0.0))

    return _pc(body, name="gdn_in", grid=(t // tm,),
               in_specs=[_rows(tm, d), _const((1, d)), _resident((d, 4 * d)), _resident((d, LANES)),
                         _const((1, LANES)), _const((1, LANES))],
               out_specs=[_rows(tm, d), _rows(tm, 3 * d), _rows(tm, d), _rows(tm, LANES), _rows(tm, LANES)],
               out_shape=[jax.ShapeDtypeStruct((t, d), bf16), jax.ShapeDtypeStruct((t, 3 * d), f32),
                          jax.ShapeDtypeStruct((t, d), f32), jax.ShapeDtypeStruct((t, LANES), f32),
                          jax.ShapeDtypeStruct((t, LANES), f32)],
               sem=("parallel",))(h, g, w_main, w_ab, a_log_pad, dt_pad)


def _gated_norm_mm_res(o, z, ng, w, res, n_heads):
    t, d = o.shape
    tm = _tile(t, 256)

    def body(o_ref, z_ref, ng_ref, w_ref, r_ref, on_ref, out_ref):
        for hd in range(n_heads):
            sl = slice(hd * LANES, (hd + 1) * LANES)
            rn, _, _ = _rms(o_ref[:, sl], ng_ref[...])
            zz = z_ref[:, sl]
            on_ref[:, sl] = (rn * (zz * _sigmoid(zz))).astype(bf16)
        out_ref[...] = r_ref[...] + jnp.dot(on_ref[...], w_ref[...], preferred_element_type=f32)

    return _pc(body, name="gated_norm_wout", grid=(t // tm,),
               in_specs=[_rows(tm, d), _rows(tm, d), _const((1, LANES)), _resident((d, d)), _rows(tm, d)],
               out_specs=[_rows(tm, d), _rows(tm, d)],
               out_shape=[jax.ShapeDtypeStruct((t, d), bf16), jax.ShapeDtypeStruct((t, d), f32)],
               sem=("parallel",))(o, z, ng, w, res)


def _mlp_bwd(dho, h, g, fb, w1g, w2g, name):
    t, d = h.shape
    nb_w, _, bn = w1g.shape
    ff = nb_w * bn
    tm = _tile(t, 256)

    def body(do_ref, h_ref, g_ref, f_ref, w1_hbm, w2_hbm, df_ref, dh_ref, dg_ref, cs_ref, w1_ref, w2_ref, sem1, sem2):
        _fetch_blocks(pl.program_id(0), w1_hbm, w1_ref, sem1, True)
        _fetch_blocks(pl.program_id(0), w2_hbm, w2_ref, sem2, False)
        _acc_init(pl.program_id(0), dg_ref, cs_ref)
        do = do_ref[...]
        dr = lax.dot_general(do.astype(bf16), w2_ref[...], NT, preferred_element_type=f32)
        dfb = (dr * (2.0 * jnp.maximum(f_ref[...].astype(f32), 0.0))).astype(bf16)
        df_ref[...] = dfb
        dn = lax.dot_general(dfb, w1_ref[...], NT, preferred_element_type=f32)
        gv = g_ref[...]
        _, xh, rstd = _rms(h_ref[...], gv)
        _acc_rows(dg_ref, dn * xh)
        dh = do + _rms_bwd(dn, xh, rstd, gv)
        dh_ref[...] = dh
        _acc_rows(cs_ref, dh)

    return _pc(body, name=name, grid=(t // tm,),
               in_specs=[_rows(tm, d), _rows(tm, d), _const((1, d)), _rows(tm, ff), ANY, ANY],
               out_specs=[_rows(tm, ff), _rows(tm, d), _const((SUBLANES, d)), _const((SUBLANES, d))],
               out_shape=[jax.ShapeDtypeStruct((t, ff), bf16), jax.ShapeDtypeStruct((t, d), f32),
                          jax.ShapeDtypeStruct((SUBLANES, d), f32), jax.ShapeDtypeStruct((SUBLANES, d), f32)],
               scratch=[pltpu.VMEM((d, ff), bf16), pltpu.VMEM((ff, d), bf16),
                        pltpu.SemaphoreType.DMA((nb_w,)), pltpu.SemaphoreType.DMA((nb_w,))],
               sem=("arbitrary",))(dho, h, g, fb, w1g, w2g)


class _Tail:
    def __init__(self, fn, ins, outs):
        self.fn, self.ins, self.outs = fn, ins, outs


def _mm_nt(pairs, name, tail=None):
    t = pairs[0][0].shape[0]
    tm = _tile(t, 256)
    npair = len(pairs)
    in_specs, args, scratch, blocked = [], [], [], []
    k = None
    for dy, w in pairs:
        nn = dy.shape[1]
        if isinstance(w, tuple) and isinstance(w[0], str):
            w = w[1]
            k = w.shape[1]
            wspec = ANY
            blocked.append(True)
            scratch += [pltpu.VMEM((k, nn), bf16), pltpu.SemaphoreType.DMA((w.shape[0],))]
        elif isinstance(w, tuple):
            w, idx = w
            k = w.shape[0]
            wspec = pl.BlockSpec((k, nn), lambda *_, idx=idx: (0, idx), pipeline_mode=pl.Buffered(1))
            blocked.append(False)
        else:
            k = w.shape[0]
            wspec = _resident(w.shape)
            blocked.append(False)
        in_specs += [_rows(tm, nn), wspec]
        args += [dy, w]
    n_tin = len(tail.ins) if tail else 0
    n_out = len(tail.outs) if tail else 1
    if tail:
        for arr, kind in tail.ins:
            in_specs.append(_rows(tm, arr.shape[1]) if kind == "rows" else _const(arr.shape))
            args.append(arr)
        out_specs = [_rows(tm, c) if kind == "rows" else _const((SUBLANES, c)) for c, kind in tail.outs]
        out_shape = [jax.ShapeDtypeStruct((t, c) if kind == "rows" else (SUBLANES, c), f32) for c, kind in tail.outs]
    else:
        out_specs = _rows(tm, k)
        out_shape = jax.ShapeDtypeStruct((t, k), f32)

    def body(*refs):
        step = pl.program_id(0)
        tin = refs[2 * npair:2 * npair + n_tin]
        outs = refs[2 * npair + n_tin:2 * npair + n_tin + n_out]
        scr = list(refs[2 * npair + n_tin + n_out:])
        acc = None
        for p in range(npair):
            w_ref = refs[2 * p + 1]
            if blocked[p]:
                w_vmem, sems = scr.pop(0), scr.pop(0)
                _fetch_blocks(step, w_ref, w_vmem, sems, True)
                w_ref = w_vmem
            part = lax.dot_general(refs[2 * p][...].astype(bf16), w_ref[...], NT, preferred_element_type=f32)
            acc = part if acc is None else acc + part
        if tail is None:
            outs[0][...] = acc
        else:
            _acc_init(step, *[o for o, (_, kind) in zip(outs, tail.outs) if kind == "acc"])
            tail.fn(acc, tin, outs)

    sequential = tail is not None or any(blocked)
    return _pc(body, name=name, grid=(t // tm,), in_specs=in_specs, out_specs=out_specs, out_shape=out_shape,
               scratch=scratch, sem=("arbitrary",) if sequential else ("parallel",))(*args)


def _rms_bwd_tail(h, g, dres):
    def fn(dn, ins, outs):
        h_ref, g_ref, dr_ref = ins
        dh_ref, dg_ref = outs
        gv = g_ref[...]
        _, xh, rstd = _rms(h_ref[...], gv)
        _acc_rows(dg_ref, dn * xh)
        dh_ref[...] = dr_ref[...] + _rms_bwd(dn, xh, rstd, gv)

    d = h.shape[1]
    return _Tail(fn, [(h, "rows"), (g, "const"), (dres, "rows")], [(d, "rows"), (d, "acc")])


def _ln_silu_bwd_tail(dc, ln_g, ln_b):
    def fn(ds, ins, outs):
        x_ref, g_ref, b_ref = ins
        dx_ref, dg_ref, db_ref, cs_ref = outs
        x = x_ref[...]
        gv = g_ref[...]
        xc = x - jnp.mean(x, axis=-1, keepdims=True)
        rstd = lax.rsqrt(jnp.mean(xc * xc, axis=-1, keepdims=True) + NORM_EPS)
        xh = xc * rstd
        dln = ds * _silu_grad(xh * gv + b_ref[...])
        _acc_rows(dg_ref, dln * xh)
        _acc_rows(db_ref, dln)
        dxh = dln * gv
        dx = rstd * (dxh - jnp.mean(dxh, axis=-1, keepdims=True) - xh * jnp.mean(dxh * xh, axis=-1, keepdims=True))
        dx_ref[...] = dx
        _acc_rows(cs_ref, dx)

    d = dc.shape[1]
    return _Tail(fn, [(dc, "rows"), (ln_g, "const"), (ln_b, "const")], [(d, "rows"), (d, "acc"), (d, "acc"), (d, "acc")])


def _gated_norm_bwd_tail(o, z, ng, n_heads):
    def fn(don_all, ins, outs):
        o_ref, z_ref, ng_ref = ins
        do_ref, dz_ref, dng_ref = outs
        gv = ng_ref[...]
        for hd in range(n_heads):
            sl = slice(hd * LANES, (hd + 1) * LANES)
            rn, xh, rstd = _rms(o_ref[:, sl], gv)
            zz = z_ref[:, sl]
            don = don_all[:, sl]
            dz_ref[:, sl] = don * rn * _silu_grad(zz)
            drn = don * (zz * _sigmoid(zz))
            _acc_rows(dng_ref, drn * xh)
            do_ref[:, sl] = _rms_bwd(drn, xh, rstd, gv)

    d = o.shape[1]
    return _Tail(fn, [(o, "rows"), (z, "rows"), (ng, "const")], [(d, "rows"), (d, "rows"), (LANES, "acc")])


def _mm_tn_blocked(x, dy, name, out_dtype=f32):
    t, k = x.shape
    bn = dy.shape[1] // N_DEV
    tm = _tile(t, 512)
    jb = N_DEV
    while jb > 1 and k * jb * bn * 4 > 8 * 1024 * 1024:
        jb //= 2
    nt = t // tm

    def body(x_ref, dy_ref, o_ref, *acc):
        acc_ref = acc[0] if acc else o_ref
        _acc_init(pl.program_id(1), acc_ref)
        xt = x_ref[...].astype(bf16).T
        for jj in range(jb):
            acc_ref[jj] += jnp.dot(xt, dy_ref[:, jj * bn:(jj + 1) * bn].astype(bf16), preferred_element_type=f32)
        if acc:
            @pl.when(pl.program_id(1) == nt - 1)
            def _():
                o_ref[...] = acc_ref[...].astype(out_dtype)

    return _pc(body, name=name, grid=(N_DEV // jb, nt),
               in_specs=[pl.BlockSpec((tm, k), lambda j, i: (i, 0)), pl.BlockSpec((tm, jb * bn), lambda j, i: (i, j))],
               out_specs=pl.BlockSpec((jb, k, bn), lambda j, i: (j, 0, 0)),
               out_shape=jax.ShapeDtypeStruct((N_DEV, k, bn), out_dtype),
               scratch=[] if out_dtype == f32 else [pltpu.VMEM((jb, k, bn), f32)],
               sem=("parallel", "arbitrary"))(x, dy)


def _mm_tn(x, dy, name, out_dtype=f32):
    t, k = x.shape
    n = dy.shape[1]
    tm = _tile(t, 512)
    cap = max(LANES, (2 * 1024 * 1024) // k)
    tn = n
    if n > cap:
        tn = max(c for c in range(LANES, cap + 1, LANES) if n % c == 0)
    nt = t // tm

    def body(x_ref, dy_ref, o_ref, *acc):
        acc_ref = acc[0] if acc else o_ref
        _acc_init(pl.program_id(1), acc_ref)
        acc_ref[...] += lax.dot_general(x_ref[...].astype(bf16), dy_ref[...].astype(bf16), TN, preferred_element_type=f32)
        if acc:
            @pl.when(pl.program_id(1) == nt - 1)
            def _():
                o_ref[...] = acc_ref[...].astype(out_dtype)

    return _pc(body, name=name, grid=(n // tn, nt),
               in_specs=[pl.BlockSpec((tm, k), lambda j, i: (i, 0)), pl.BlockSpec((tm, tn), lambda j, i: (i, j))],
               out_specs=pl.BlockSpec((k, tn), lambda j, i: (0, j)),
               out_shape=jax.ShapeDtypeStruct((k, n), out_dtype),
               scratch=[] if out_dtype == f32 else [pltpu.VMEM((k, tn), f32)],
               sem=("parallel", "arbitrary"))(x, dy)


def _gates_bwd(dgb, ab, a_log_pad, dt_pad, n_heads):
    t = ab.shape[0]
    tm = _tile(t, 256)

    def body(dgb_ref, ab_ref, al_ref, dt_ref, dab_ref, dal_ref, ddt_ref):
        _acc_init(pl.program_id(0), dal_ref, ddt_ref)
        ab = ab_ref[...]
        dgb = dgb_ref[...]
        lane = lax.broadcasted_iota(jnp.int32, ab.shape, 1)
        is_a = lane < n_heads
        is_b = jnp.logical_and(lane >= n_heads, lane < 2 * n_heads)
        xa = ab + dt_ref[...]
        neg_a = -jnp.exp(al_ref[...])
        dg_da = neg_a * _sigmoid(xa)
        beta = _sigmoid(ab)
        da = jnp.where(is_a, dgb * dg_da, 0.0)
        dab_ref[...] = da + jnp.where(is_b, dgb * beta * (1.0 - beta), 0.0)
        _acc_rows(dal_ref, jnp.where(is_a, dgb * neg_a * _softplus(xa), 0.0))
        _acc_rows(ddt_ref, da)

    return _pc(body, name="gates_bwd", grid=(t // tm,),
               in_specs=[_rows(tm, LANES), _rows(tm, LANES), _const((1, LANES)), _const((1, LANES))],
               out_specs=[_rows(tm, LANES), _const((SUBLANES, LANES)), _const((SUBLANES, LANES))],
               out_shape=[jax.ShapeDtypeStruct((t, LANES), f32), jax.ShapeDtypeStruct((SUBLANES, LANES), f32),
                          jax.ShapeDtypeStruct((SUBLANES, LANES), f32)],
               sem=("arbitrary",))(dgb, ab, a_log_pad, dt_pad)


def _glu_bwd(dgl, ub):
    t, d = dgl.shape
    tm = _tile(t, 256)

    def body(dgl_ref, u_ref, du_ref, cs_ref):
        _acc_init(pl.program_id(0), cs_ref)
        dgl = dgl_ref[...]
        a = u_ref[:, :d].astype(f32)
        sb = _sigmoid(u_ref[:, d:].astype(f32))
        da = dgl * sb
        db = dgl * a * sb * (1.0 - sb)
        du_ref[:, :d] = da.astype(bf16)
        du_ref[:, d:] = db.astype(bf16)
        cs_ref[0:1, :d] += jnp.sum(da, axis=0, keepdims=True)
        cs_ref[0:1, d:] += jnp.sum(db, axis=0, keepdims=True)

    return _pc(body, name="glu_bwd", grid=(t // tm,),
               in_specs=[_rows(tm, d), _rows(tm, 2 * d)],
               out_specs=[_rows(tm, 2 * d), _const((SUBLANES, 2 * d))],
               out_shape=[jax.ShapeDtypeStruct((t, 2 * d), bf16), jax.ShapeDtypeStruct((SUBLANES, 2 * d), f32)],
               sem=("arbitrary",))(dgl, ub)


def _conv_rows(s):
    return 256 if s % 256 == 0 else s


def _conv_tap_sum(pad_ref, w_ref, base, rows, width):
    acc = jnp.zeros((rows, LANES), f32)
    for j in range(width):
        acc = acc + w_ref[j:j + 1, :] * pad_ref[pl.ds(base + CONV_PAD - (width - 1) + j, rows), :]
    return acc


def _l2_silu_post(c, j, n_heads, scale):
    a = c * _sigmoid(c)
    r = lax.rsqrt(jnp.sum(a * a, axis=-1, keepdims=True) + L2_EPS)
    mult = jnp.where(j < n_heads, r * scale, jnp.where(j < 2 * n_heads, r, 1.0))
    return a, r, a * mult


def _dwconv_fwd(x, w, b, name, qk_heads=None):
    bl, s, cn = x.shape
    width = w.shape[0]
    rows = _conv_rows(s)
    scale = float(LANES) ** -0.5

    def body(x_ref, w_ref, b_ref, o_ref, pad_ref):
        j = pl.program_id(1)
        pad_ref[0:CONV_PAD, :] = jnp.zeros((CONV_PAD, LANES), f32)
        pad_ref[CONV_PAD:, :] = x_ref[0]

        def step(i, carry):
            base = pl.multiple_of(i * rows, rows)
            acc = _conv_tap_sum(pad_ref, w_ref, base, rows, width)
            if qk_heads is None:
                acc = acc + b_ref[...]
            else:
                _, _, acc = _l2_silu_post(acc, j, qk_heads, scale)
            o_ref[0, pl.ds(base, rows), :] = acc
            return carry

        lax.fori_loop(0, s // rows, step, 0)

    return _pc(body, name=name, grid=(bl, cn // LANES),
               in_specs=[pl.BlockSpec((1, s, LANES), lambda bi, j: (bi, 0, j)),
                         pl.BlockSpec((width, LANES), lambda bi, j: (0, j)),
                         pl.BlockSpec((1, LANES), lambda bi, j: (0, j))],
               out_specs=pl.BlockSpec((1, s, LANES), lambda bi, j: (bi, 0, j)),
               out_shape=jax.ShapeDtypeStruct((bl, s, cn), f32),
               scratch=[pltpu.VMEM((s + CONV_PAD, LANES), f32)],
               sem=("parallel", "parallel"))(x, w, b)


def _dwconv_bwd(x, dys, w, name, qk_heads=None):
    bl, s, cn = x.shape
    width = w.shape[0]
    wp = -(-width // SUBLANES) * SUBLANES
    rows = _conv_rows(s)
    scale = float(LANES) ** -0.5
    nblk = s // rows
    ndy = len(dys)

    def body(*refs):
        x_ref, w_ref = refs[0], refs[1]
        dy_refs = refs[2:2 + ndy]
        dx_ref, dw_ref, db_ref, xpad, dypad, acc = refs[2 + ndy:]
        j = pl.program_id(0)
        bi = pl.program_id(1)
        _acc_init(bi, acc, db_ref)
        xpad[0:CONV_PAD, :] = jnp.zeros((CONV_PAD, LANES), f32)
        xpad[CONV_PAD:, :] = x_ref[0]
        dypad[s:, :] = jnp.zeros((CONV_PAD, LANES), f32)
        if qk_heads is None:
            dypad[0:s, :] = dy_refs[0][0]
        else:
            def pre(i, carry):
                base = pl.multiple_of(i * rows, rows)
                c = _conv_tap_sum(xpad, w_ref, base, rows, width)
                a, r, _ = _l2_silu_post(c, j, qk_heads, scale)
                dq = dy_refs[0][0, pl.ds(base, rows), :]
                dk = dy_refs[1][0, pl.ds(base, rows), :]
                dv = dy_refs[2][0, pl.ds(base, rows), :]
                dy = jnp.where(j < qk_heads, dq * scale, jnp.where(j < 2 * qk_heads, dk, dv))
                da_l2 = r * (dy - a * (r * r) * jnp.sum(a * dy, axis=-1, keepdims=True))
                da = jnp.where(j < 2 * qk_heads, da_l2, dy)
                dypad[pl.ds(base, rows), :] = da * _silu_grad(c)
                return carry

            lax.fori_loop(0, nblk, pre, 0)

        def step(i, carry):
            base = pl.multiple_of(i * rows, rows)
            dxa = jnp.zeros((rows, LANES), f32)
            for jj in range(width):
                dxa = dxa + w_ref[jj:jj + 1, :] * dypad[pl.ds(base + (width - 1) - jj, rows), :]
            dx_ref[0, pl.ds(base, rows), :] = dxa
            dyc = dypad[pl.ds(base, rows), :]
            db_ref[...] += dyc.reshape(rows // SUBLANES, SUBLANES, LANES).sum(axis=0)
            for jj in range(width):
                prod = dyc * xpad[pl.ds(base + CONV_PAD - (width - 1) + jj, rows), :]
                acc[jj * SUBLANES:(jj + 1) * SUBLANES, :] += prod.reshape(rows // SUBLANES, SUBLANES, LANES).sum(axis=0)
            return carry

        lax.fori_loop(0, nblk, step, 0)

        @pl.when(bi == bl - 1)
        def _():
            dw_ref[...] = jnp.zeros((wp, LANES), f32)
            for jj in range(width):
                dw_ref[jj:jj + 1, :] = jnp.sum(acc[jj * SUBLANES:(jj + 1) * SUBLANES, :], axis=0, keepdims=True)

    if qk_heads is None:
        dy_specs = [pl.BlockSpec((1, s, LANES), lambda j, bi: (bi, 0, j))]
    else:
        hh = qk_heads
        def dy_spec(part):
            def index(j, bi):
                mine = jnp.logical_and(j >= part * hh, j < (part + 1) * hh)
                return (jnp.where(mine, bi * hh + j - part * hh, 0), 0, 0)
            return pl.BlockSpec((1, s, LANES), index)

        dy_specs = [dy_spec(0), dy_spec(1), dy_spec(2)]
    return _pc(body, name=name, grid=(cn // LANES, bl),
               in_specs=[pl.BlockSpec((1, s, LANES), lambda j, bi: (bi, 0, j)),
                         pl.BlockSpec((width, LANES), lambda j, bi: (0, j))] + dy_specs,
               out_specs=[pl.BlockSpec((1, s, LANES), lambda j, bi: (bi, 0, j)),
                          pl.BlockSpec((wp, LANES), lambda j, bi: (0, j)),
                          pl.BlockSpec((SUBLANES, LANES), lambda j, bi: (0, j))],
               out_shape=[jax.ShapeDtypeStruct((bl, s, cn), f32), jax.ShapeDtypeStruct((wp, cn), f32),
                          jax.ShapeDtypeStruct((SUBLANES, cn), f32)],
               scratch=[pltpu.VMEM((s + CONV_PAD, LANES), f32), pltpu.VMEM((s + CONV_PAD, LANES), f32),
                        pltpu.VMEM((width * SUBLANES, LANES), f32)],
               sem=("parallel", "arbitrary"))(x, w, *dys)


def _group_rows(s):
    for rows in (256, 128):
        if s % rows == 0:
            return rows
    return CHUNK


def _group_masks(rows):
    r = lax.broadcasted_iota(jnp.int32, (rows, rows), 0)
    c = lax.broadcasted_iota(jnp.int32, (rows, rows), 1)
    same_chunk = (r >> 6) == (c >> 6)
    return r, c, same_chunk


def _decay(gc_col, gc_row, causal):
    return jnp.exp(jnp.where(causal, gc_col - gc_row, NEG))


def _inv_unit_lower(a, r, c):
    return _inv_unit_lower_many([a], r, c)[0]


def _inv_unit_lower_many(mats, r, c):
    eye = (r == c).astype(f32)
    same16 = (r >> 4) == (c >> 4)
    ads = [jnp.where(same16, a, 0.0) for a in mats]
    aos = [a - ad for a, ad in zip(mats, ads)]
    xs = ads
    tds = [eye - x for x in xs]
    for _ in range(3):
        xs = [_dot(x, x) for x in xs]
        tds = [td + _dot(td, x) for td, x in zip(tds, xs)]
    bs = [_dot(td, ao) for td, ao in zip(tds, aos)]
    b2s = [_dot(b, b) for b in bs]
    b3s = [_dot(b, b2) for b, b2 in zip(bs, b2s)]
    return [_dot(eye - b + b2 - b3, td) for b, b2, b3, td in zip(bs, b2s, b3s, tds)]


def _lane_cumsum(x, y, reverse, name):
    rr = x.shape[0]

    def body(x_ref, y_ref, o_ref):
        i = lax.broadcasted_iota(jnp.int32, (CHUNK, CHUNK), 0)
        j = lax.broadcasted_iota(jnp.int32, (CHUNK, CHUNK), 1)
        tri = ((i >= j) if reverse else (i <= j)).astype(f32)
        o_ref[...] = jnp.dot(x_ref[...] + y_ref[...], tri, precision=HI, preferred_element_type=f32)

    spec = pl.BlockSpec((rr, CHUNK), lambda: (0, 0))
    return pl.pallas_call(body, name=name, in_specs=[spec, spec], out_specs=spec,
                          out_shape=jax.ShapeDtypeStruct((rr, CHUNK), f32))(x, y)


def _gdn_specs(n_heads, nblk, rows, hp=1, rev=False):
    def blk(n):
        return nblk - 1 - n if rev else n

    def qkv(off):
        return pl.BlockSpec((rows, hp * LANES), lambda g, n: (
            lax.div(g * hp, n_heads) * nblk + blk(n), lax.div(off * n_heads + lax.rem(g * hp, n_heads), hp)))

    def per_head(last, mult=1):
        return pl.BlockSpec((hp, rows * mult, last), lambda g, n: (g, blk(n), 0))

    def row_vec():
        return pl.BlockSpec((hp, 1, 1, rows), lambda g, n: (g, blk(n), 0, 0))

    return qkv, per_head, row_vec


def _gdn_prep(qkv, cols, grow, bl, s, n_heads):
    rows = _group_rows(s)
    nblk = s // rows
    bh_n = bl * n_heads
    hp = PREP_HEADS_PER_STEP if n_heads % PREP_HEADS_PER_STEP == 0 else 1
    qkv_spec, ph, rv = _gdn_specs(n_heads, nblk, rows, hp)

    def body(k_ref, v_ref, cols_ref, grow_ref, u_ref, w_ref, t_ref):
        r, c, same = _group_masks(rows)
        causal = jnp.logical_and(same, r >= c)
        mats, rhs = [], []
        for h in range(hp):
            hs = slice(h * LANES, (h + 1) * LANES)
            k = k_ref[:, hs]
            gc = cols_ref[h, :, 0:1]
            beta = cols_ref[h, :, 1:2]
            kb = k * beta
            mats.append(jnp.where(r > c, _dot_nt(kb, k) * _decay(gc, grow_ref[h, 0], causal), 0.0))
            rhs.append((v_ref[:, hs] * beta, kb * jnp.exp(gc)))
        for h, tm in enumerate(_inv_unit_lower_many(mats, r, c)):
            tb = tm.astype(bf16)
            u_ref[h] = jnp.dot(tb, rhs[h][0].astype(bf16), preferred_element_type=f32)
            w_ref[h] = jnp.dot(tb, rhs[h][1].astype(bf16), preferred_element_type=f32)
            t_ref[h] = tb

    return _pc(body, name="gdn_prep", grid=(bh_n // hp, nblk),
               in_specs=[qkv_spec(1), qkv_spec(2), ph(2), rv()],
               out_specs=[ph(LANES), ph(LANES), ph(rows)],
               out_shape=[jax.ShapeDtypeStruct((bh_n, s, LANES), f32), jax.ShapeDtypeStruct((bh_n, s, LANES), f32),
                          jax.ShapeDtypeStruct((bh_n, s, rows), bf16)],
               sem=("parallel", "parallel"))(qkv, qkv, cols, grow)


def _gdn_scan(qkv, u, w, cols, grow, bl, s, n_heads):
    rows = _group_rows(s)
    g_chunks = rows // CHUNK
    nblk = s // rows
    bh_n = bl * n_heads
    d = n_heads * LANES
    hp = HEADS_PER_STEP if n_heads % HEADS_PER_STEP == 0 else 1
    qkv_spec, ph, rv = _gdn_specs(n_heads, nblk, rows, hp)

    def body(q_ref, k_ref, u_ref, w_ref, cols_ref, grow_ref, o_ref, vn_ref, ss_ref, s_scr):
        _acc_init(pl.program_id(1), s_scr)
        r, c, same = _group_masks(rows)
        causal = jnp.logical_and(same, r >= c)
        qs, ks, gcs, qgs, ps = [], [], [], [], []
        for h in range(hp):
            hs = slice(h * LANES, (h + 1) * LANES)
            q, k, gc = q_ref[:, hs], k_ref[:, hs], cols_ref[h, :, 0:1]
            qs.append(q)
            ks.append(k)
            gcs.append(gc)
            qgs.append(q * jnp.exp(gc))
            ps.append(_dot_nt(q, k) * _decay(gc, grow_ref[h, 0], causal))
        st = [s_scr[h] for h in range(hp)]
        o_state = [[] for _ in range(hp)]
        vns = [[] for _ in range(hp)]
        for ci in range(g_chunks):
            sl = slice(ci * CHUNK, (ci + 1) * CHUNK)
            for h in range(hp):
                ss_ref[h, ci * LANES:(ci + 1) * LANES, :] = st[h]
                vn = u_ref[h, sl, :] - _dot(w_ref[h, sl, :], st[h])
                vns[h].append(vn)
                o_state[h].append(_dot(qgs[h][sl], st[h]))
                gc = gcs[h][sl]
                g_last = gc[CHUNK - 1:CHUNK, :]
                st[h] = jnp.exp(g_last) * st[h] + _dot_tn(ks[h][sl] * jnp.exp(g_last - gc), vn)
        for h in range(hp):
            s_scr[h] = st[h]
            vn_all = jnp.concatenate(vns[h], axis=0)
            vn_ref[h] = vn_all
            o_ref[:, h * LANES:(h + 1) * LANES] = jnp.concatenate(o_state[h], axis=0) + _dot(ps[h], vn_all)

    return _pc(body, name="gdn_scan", grid=(bh_n // hp, nblk),
               in_specs=[qkv_spec(0), qkv_spec(1), ph(LANES), ph(LANES), ph(2), rv()],
               out_specs=[qkv_spec(0), ph(LANES), ph(LANES, mult=LANES // CHUNK)],
               out_shape=[jax.ShapeDtypeStruct((bl * s, d), f32), jax.ShapeDtypeStruct((bh_n, s, LANES), f32),
                          jax.ShapeDtypeStruct((bh_n, (s // CHUNK) * LANES, LANES), f32)],
               scratch=[pltpu.VMEM((hp, LANES, LANES), f32)],
               sem=("parallel", "arbitrary"))(qkv, qkv, u, w, cols, grow)


def _gdn_scan_bwd(do, qkv, w, vn, cols, grow, ss, bl, s, n_heads):
    rows = _group_rows(s)
    g_chunks = rows // CHUNK
    nblk = s // rows
    bh_n = bl * n_heads
    hp = HEADS_PER_STEP if n_heads % HEADS_PER_STEP == 0 else 1
    qkv_spec, ph, rv = _gdn_specs(n_heads, nblk, rows, hp, rev=True)

    def body(do_ref, q_ref, k_ref, w_ref, vn_ref, cols_ref, grow_ref, ss_ref,
             du_ref, dw_ref, dq_ref, dk_ref, dcol_ref, drow_ref, ds_scr):
        _acc_init(pl.program_id(1), ds_scr)
        r, c, same = _group_masks(rows)
        causal = jnp.logical_and(same, r >= c)
        last_row = lax.broadcasted_iota(jnp.int32, (CHUNK, 1), 0) == CHUNK - 1
        pre = []
        for h in range(hp):
            hs = slice(h * LANES, (h + 1) * LANES)
            do, q, k = do_ref[:, hs], q_ref[:, hs], k_ref[:, hs]
            vn = vn_ref[h]
            gc = cols_ref[h, :, 0:1]
            dmat = _decay(gc, grow_ref[h, 0], causal)
            gam = jnp.exp(gc)
            qk = _dot_nt(q, k)
            dpd = _dot_nt(do, vn) * dmat
            ep = dpd * qk
            drow_ref[h, 0] = -jnp.sum(ep, axis=0, keepdims=True)
            pre.append(dict(do=do, k=k, vn=vn, gc=gc, gam=gam, qg=q * gam,
                            dvn_intra=_dot_tn(qk * dmat, do), dq_intra=_dot(dpd, k), dk_intra=_dot_tn(dpd, q),
                            ep_rows=jnp.sum(ep, axis=-1, keepdims=True)))
        ds = [ds_scr[h] for h in range(hp)]
        for ci in reversed(range(g_chunks)):
            sl = slice(ci * CHUNK, (ci + 1) * CHUNK)
            for h in range(hp):
                pr = pre[h]
                st = ss_ref[h, ci * LANES:(ci + 1) * LANES, :]
                gc = pr["gc"][sl]
                g_last = gc[CHUNK - 1:CHUNK, :]
                gam_last = jnp.exp(g_last)
                kd_scale = jnp.exp(g_last - gc)
                kdec = pr["k"][sl] * kd_scale
                do_c = pr["do"][sl]
                qg_c = pr["qg"][sl]
                dvn = pr["dvn_intra"][sl] + _dot(kdec, ds[h])
                dkdec = _dot_nt(pr["vn"][sl], ds[h])
                du_ref[h, sl, :] = dvn
                dw_ref[h, sl, :] = -_dot_nt(dvn, st)
                dqg = _dot_nt(do_c, st)
                dq_ref[h, sl, :] = dqg * pr["gam"][sl] + pr["dq_intra"][sl]
                dk_ref[h, sl, :] = pr["dk_intra"][sl] + dkdec * kd_scale
                kd_rows = jnp.sum(dkdec * kdec, axis=-1, keepdims=True)
                extra = jnp.sum(kd_rows) + gam_last * jnp.sum(st * ds[h])
                dcol_ref[h, sl, :] = (jnp.sum(dqg * qg_c, axis=-1, keepdims=True) + pr["ep_rows"][sl] - kd_rows
                                      + jnp.where(last_row, extra, 0.0))
                ds[h] = gam_last * ds[h] + _dot_tn(qg_c, do_c) - _dot_tn(w_ref[h, sl, :], dvn)
        for h in range(hp):
            ds_scr[h] = ds[h]

    return _pc(body, name="gdn_scan_bwd", grid=(bh_n // hp, nblk),
               in_specs=[qkv_spec(0), qkv_spec(0), qkv_spec(1), ph(LANES), ph(LANES), ph(2), rv(),
                         ph(LANES, mult=LANES // CHUNK)],
               out_specs=[ph(LANES), ph(LANES), ph(LANES), ph(LANES), ph(1), rv()],
               out_shape=[jax.ShapeDtypeStruct((bh_n, s, LANES), f32)] * 4
               + [jax.ShapeDtypeStruct((bh_n, s, 1), f32), jax.ShapeDtypeStruct((bh_n, nblk, 1, rows), f32)],
               scratch=[pltpu.VMEM((hp, LANES, LANES), f32)],
               sem=("parallel", "arbitrary"))(do, qkv, qkv, w, vn, cols, grow, ss)


def _gdn_prep_bwd(qkv, cols, grow, tmat, du, dw, dk_scan, dcol_scan, drow_scan, bl, s, n_heads):
    rows = _group_rows(s)
    nblk = s // rows
    bh_n = bl * n_heads
    hp = PREP_HEADS_PER_STEP if n_heads % PREP_HEADS_PER_STEP == 0 else 1
    qkv_spec, ph, rv = _gdn_specs(n_heads, nblk, rows, hp)

    def body(k_ref, v_ref, cols_ref, grow_ref, t_ref, du_ref, dw_ref, dks_ref, dcs_ref, drs_ref,
             dk_ref, dv_ref, dcols_ref, drow_ref):
        r, c, same = _group_masks(rows)
        causal = jnp.logical_and(same, r >= c)
        hh = range(hp)
        ks = [k_ref[:, h * LANES:(h + 1) * LANES] for h in hh]
        vs = [v_ref[:, h * LANES:(h + 1) * LANES] for h in hh]
        gcs = [cols_ref[h, :, 0:1] for h in hh]
        betas = [cols_ref[h, :, 1:2] for h in hh]
        tms = [t_ref[h] for h in hh]
        dus = [du_ref[h] for h in hh]
        dws = [dw_ref[h] for h in hh]
        gams = [jnp.exp(gc) for gc in gcs]
        kbs = [k * b for k, b in zip(ks, betas)]
        kbgs = [kb * g for kb, g in zip(kbs, gams)]
        dts = [jnp.where(same, _dot_nt(dus[h], vs[h] * betas[h]) + _dot_nt(dws[h], kbgs[h]), 0.0) for h in hh]
        dvbs = [_dot_tn(tms[h], dus[h]) for h in hh]
        dkbgs = [_dot_tn(tms[h], dws[h]) for h in hh]
        kks = [_dot_nt(kbs[h], ks[h]) for h in hh]
        inner = [_dot_nt(dts[h], tms[h]) for h in hh]
        dads = [jnp.where(r > c, -_dot_tn(tms[h], inner[h]), 0.0) * _decay(gcs[h], grow_ref[h, 0], causal) for h in hh]
        dkbs = [dkbgs[h] * gams[h] + _dot(dads[h], ks[h]) for h in hh]
        dk2 = [_dot_tn(dads[h], kbs[h]) for h in hh]
        for h in hh:
            dk_ref[h] = dks_ref[h] + dk2[h] + dkbs[h] * betas[h]
            dv_ref[h] = dvbs[h] * betas[h]
            ea = dads[h] * kks[h]
            dcols_ref[h, :, 0:1] = (dcs_ref[h] + jnp.sum(dkbgs[h] * kbgs[h], axis=-1, keepdims=True)
                                    + jnp.sum(ea, axis=-1, keepdims=True))
            dcols_ref[h, :, 1:2] = (jnp.sum(dvbs[h] * vs[h], axis=-1, keepdims=True)
                                    + jnp.sum(dkbs[h] * ks[h], axis=-1, keepdims=True))
            drow_ref[h, 0] = drs_ref[h, 0] - jnp.sum(ea, axis=0, keepdims=True)

    return _pc(body, name="gdn_prep_bwd", grid=(bh_n // hp, nblk),
               in_specs=[qkv_spec(1), qkv_spec(2), ph(2), rv(), ph(rows), ph(LANES), ph(LANES), ph(LANES), ph(1), rv()],
               out_specs=[ph(LANES), ph(LANES), ph(2), rv()],
               out_shape=[jax.ShapeDtypeStruct((bh_n, s, LANES), f32), jax.ShapeDtypeStruct((bh_n, s, LANES), f32),
                          jax.ShapeDtypeStruct((bh_n, s, 2), f32), jax.ShapeDtypeStruct((bh_n, nblk, 1, rows), f32)],
               sem=("parallel", "parallel"))(qkv, qkv, cols, grow, tmat, du, dw, dk_scan, dcol_scan, drow_scan)


def _row(v):
    return v.reshape(1, -1).astype(f32)


def _pad_lanes(v):
    v = v.reshape(1, -1).astype(f32)
    return jnp.pad(v, ((0, 0), (0, LANES - v.shape[1])))


def _local_step(x, tgt, p, need, emit):
    bl, s, d = x.shape
    t = bl * s
    n_heads = p["gdn_a_log"].shape[-1]
    assert d == n_heads * LANES and s % CHUNK == 0
    x2 = x.reshape(t, d)
    tgt2 = tgt.reshape(t, d)
    gr = {}

    n0, ub, gl = _pw1_glu(x2, _row(p["norm_mix_g"][0]), p["cv_w_pw1"], _row(p["cv_b_pw1"]))
    dc = _dwconv_fwd(gl.reshape(bl, s, d), p["cv_w_dw"], _row(p["cv_b_dw"]), "dwconv_fwd").reshape(t, d)
    sb, h1 = _ln_silu_mm_res(dc, _row(p["cv_ln_g"]), _row(p["cv_ln_b"]), p["cv_w_pw2"], _row(p["cv_b_pw2"]), x2)
    m0 = need("mlp0", h1)
    n1, f0, r0, h2 = _mlp_fwd(h1, _row(p["norm_ffn_g"][0]), m0["w1"], m0["w2"], "mlp_fwd0")

    gd = need("gdn", h2)
    w_in = gd["w_in"]
    w_ab = jnp.pad(w_in[:, 4 * d:], ((0, 0), (0, LANES - 2 * n_heads)))
    a_log_pad = _pad_lanes(p["gdn_a_log"])
    dt_pad = _pad_lanes(p["gdn_dt_bias"])
    n2, qkv_pre, z, ab, gbeta = _gdn_in(h2, _row(p["norm_mix_g"][1]), w_in, w_ab, a_log_pad, dt_pad, n_heads)
    zero_bias = jnp.zeros((1, 3 * d), f32)
    qkv = _dwconv_fwd(qkv_pre.reshape(bl, s, 3 * d), p["gdn_conv_w"], zero_bias, "sconv_fwd", qk_heads=n_heads).reshape(t, 3 * d)
    bh_n, rows = bl * n_heads, _group_rows(s)
    gates = gbeta[:, :2 * n_heads].reshape(bl, s, 2, n_heads).transpose(2, 0, 3, 1).reshape(2, bh_n, s)
    g_lanes = gates[0].reshape(bh_n * (s // CHUNK), CHUNK)
    gc_lanes = _lane_cumsum(g_lanes, jnp.zeros_like(g_lanes), False, "gdn_gate_cumsum")
    grow = gc_lanes.reshape(bh_n, s // rows, 1, rows)
    cols = jnp.stack([gc_lanes.reshape(bh_n, s), gates[1]], axis=-1)
    u, w, tmat = _gdn_prep(qkv, cols, grow, bl, s, n_heads)
    o, vn, ss = _gdn_scan(qkv, u, w, cols, grow, bl, s, n_heads)
    onb, h3 = _gated_norm_mm_res(o, z, _row(p["gdn_norm_g"]), gd["w_out"], h2, n_heads)
    m1 = need("mlp1", h3)
    n3, f1, r1, loss_acc, dh4, dgf = _mlp_fwd(h3, _row(p["norm_ffn_g"][1]), m1["w1"], m1["w2"], "mlp_fwd1_loss",
                                              loss=(_row(p["final_norm_g"]), tgt2))
    loss = loss_acc[0, 0]
    gr["final_norm_g"] = dgf[0]

    df1, dh3, dg_ffn1, _ = _mlp_bwd(dh4, h3, _row(p["norm_ffn_g"][1]), f1, m1["w1"], m1["w2"], "mlp_bwd1")
    dw2_1 = _mm_tn(r1, dh4, "dw_mlp2_1", bf16)
    dw1_1 = _mm_tn_blocked(n3, df1, "dw_mlp1_1", bf16)

    dw_out = _mm_tn(onb, dh3, "dw_gdn_out", bf16)
    tie = emit("late", {"mlp_w2_1": dw2_1, "mlp_w1_1": dw1_1, "gdn_w_out": dw_out})
    do, dz, dng = _mm_nt([(dh3, gd["w_out"])], "dx_gdn_out",
                         _gated_norm_bwd_tail(o, z, _row(p["gdn_norm_g"]) + tie, n_heads))
    gr["gdn_norm_g"] = dng[0]
    du, dw_, dq, dk_scan, dcol_scan, drow_scan = _gdn_scan_bwd(do, qkv, w, vn, cols, grow, ss, bl, s, n_heads)
    dk, dv, dcols, drow = _gdn_prep_bwd(qkv, cols, grow, tmat, du, dw_, dk_scan, dcol_scan, drow_scan, bl, s, n_heads)
    dg_lanes = _lane_cumsum(dcols[..., 0].reshape(g_lanes.shape), drow.reshape(g_lanes.shape), True, "gdn_gate_cumsum_bwd")
    dgb2 = jnp.stack([dg_lanes.reshape(bh_n, s), dcols[..., 1]], axis=-1)
    dqkv_pre, dconv_w, _ = _dwconv_bwd(qkv_pre.reshape(bl, s, 3 * d), [dq, dk, dv], p["gdn_conv_w"], "sconv_bwd", qk_heads=n_heads)
    gr["gdn_conv_w"] = dconv_w[:p["gdn_conv_w"].shape[0]]
    dgb = dgb2.reshape(bl, n_heads, s, 2).transpose(0, 2, 3, 1).reshape(t, 2 * n_heads)
    dgb = jnp.pad(dgb, ((0, 0), (0, LANES - 2 * n_heads)))
    dab, dal, ddt = _gates_bwd(dgb, ab, a_log_pad, dt_pad, n_heads)
    gr["gdn_a_log"] = dal[0, :n_heads]
    gr["gdn_dt_bias"] = ddt[0, :n_heads]
    dqkv2 = dqkv_pre.reshape(t, 3 * d)
    dw_in = jnp.concatenate(
        [_mm_tn(n2, dqkv2, "dw_gdn_in_qkv"), _mm_tn(n2, dz, "dw_gdn_in_z"), _mm_tn(n2, dab, "dw_gdn_in_ab")[:, :2 * n_heads]], axis=1)
    tie = emit("gdn_in", {"gdn_w_in": dw_in})
    dh2, dg_mix1 = _mm_nt([(dqkv2, (w_in, 0)), (dz, (w_in, 3)), (dab, w_ab)], "dx_gdn_in",
                          _rms_bwd_tail(h2, _row(p["norm_mix_g"][1]) + tie, dh3))

    df0, dh1, dg_ffn0, cs_h1 = _mlp_bwd(dh2, h1, _row(p["norm_ffn_g"][0]), f0, m0["w1"], m0["w2"], "mlp_bwd0")
    dw2_0 = _mm_tn(r0, dh2, "dw_mlp2_0", bf16)
    dw1_0 = _mm_tn_blocked(n1, df0, "dw_mlp1_0", bf16)
    dw_pw2 = _mm_tn(sb, dh1, "dw_pw2", bf16)
    tie = emit("mlp0", {"mlp_w2_0": dw2_0, "mlp_w1_0": dw1_0, "cv_w_pw2": dw_pw2})
    gr["norm_ffn_g"] = jnp.stack([dg_ffn0[0], dg_ffn1[0]])

    gr["cv_b_pw2"] = cs_h1[0]
    ddc, dlng, dlnb, cs_dc = _mm_nt([(dh1, p["cv_w_pw2"])], "dx_pw2",
                                    _ln_silu_bwd_tail(dc, _row(p["cv_ln_g"]) + tie, _row(p["cv_ln_b"])))
    gr["cv_ln_g"] = dlng[0]
    gr["cv_ln_b"] = dlnb[0]
    gr["cv_b_dw"] = cs_dc[0]
    dgl, dw_dw, _ = _dwconv_bwd(gl.reshape(bl, s, d), [ddc.reshape(bl, s, d)], p["cv_w_dw"], "dwconv_bwd")
    gr["cv_w_dw"] = dw_dw[:p["cv_w_dw"].shape[0]]
    dub, cs_u = _glu_bwd(dgl.reshape(t, d), ub)
    gr["cv_b_pw1"] = cs_u[0]
    dw_pw1 = _mm_tn_blocked(n0, dub, "dw_pw1", bf16)
    tie = emit("last", {"cv_w_pw1": dw_pw1, "small": gr})
    dx, dg_mix0 = _mm_nt([(dub, ("blocks", p["cv_w_pw1"]))], "dx_pw1",
                         _rms_bwd_tail(x2, _row(p["norm_mix_g"][0]) + tie, dh1))
    return loss, dx.reshape(bl, s, d), {"norm_mix_g": jnp.stack([dg_mix0[0], dg_mix1[0]])}


ANY = pl.BlockSpec(memory_space=pl.ANY)
MESH = pl.DeviceIdType.MESH


def _flip(v, bit):
    return 1 - v if bit else v


def _all_gather_many(shards):
    na = len(shards)

    def body(*refs):
        x_refs, o_refs = refs[:na], refs[na:2 * na]
        send_sems, recv_sems, local_sems = refs[2 * na:]
        x, y, c = lax.axis_index("x"), lax.axis_index("y"), lax.axis_index("c")
        me, sibling = (x, y, c), (x, y, 1 - c)
        chips = [(1 - x, y), (x, 1 - y), (1 - x, 1 - y)]

        def copy(a, k, block, to, src=None):
            px, py, pc = block
            dst = o_refs[a].at[4 * px + 2 * py + pc]
            return pltpu.make_async_remote_copy(
                src_ref=dst if src is None else src, dst_ref=dst,
                send_sem=send_sems.at[a, k], recv_sem=recv_sems.at[a, k], device_id=to, device_id_type=MESH)

        mine = [pltpu.make_async_copy(x_refs[a], o_refs[a].at[4 * x + 2 * y + c], local_sems.at[a]) for a in range(na)]
        first = []
        for a in range(na):
            first.append(copy(a, 0, me, sibling, src=x_refs[a]))
            first += [copy(a, 1 + j, me, (*chip, c), src=x_refs[a]) for j, chip in enumerate(chips)]
        for cp in mine + first:
            cp.start()
        passed = []
        for j, chip in enumerate(chips):
            for a in range(na):
                copy(a, 1 + j, (*chip, c), me).wait_recv()
                fwd = copy(a, 4 + j, (*chip, c), sibling)
                fwd.start()
                passed.append(fwd)
        for a in range(na):
            copy(a, 0, sibling, me).wait_recv()
        for j, chip in enumerate(chips):
            for a in range(na):
                copy(a, 4 + j, (*chip, 1 - c), me).wait_recv()
        for cp in first + passed:
            cp.wait_send()
        for cp in mine:
            cp.wait()

    return pl.pallas_call(
        body, name="weights_all_gather",
        out_shape=[jax.ShapeDtypeStruct((N_DEV,) + a.shape, a.dtype) for a in shards],
        in_specs=[ANY] * na, out_specs=[ANY] * na,
        scratch_shapes=[pltpu.SemaphoreType.DMA((na, 7)), pltpu.SemaphoreType.DMA((na, 7)), pltpu.SemaphoreType.DMA((na,))],
        compiler_params=pltpu.CompilerParams(has_side_effects=True),
    )(*shards)


HBM = pl.BlockSpec(memory_space=pltpu.HBM)
SEM = pl.BlockSpec(memory_space=pltpu.SEMAPHORE)
EFFECT = pltpu.SideEffectType.DATAFLOW_SIDE_EFFECTING
N_PEERS = N_DEV - 1


def _exchange_copies(src_refs, land_refs, send_sems, recv_sems, scatter):
    x, y, c = lax.axis_index("x"), lax.axis_index("y"), lax.axis_index("c")
    me = 4 * x + 2 * y + c
    copies = []
    for a, (src, land) in enumerate(zip(src_refs, land_refs)):
        for k in range(1, N_DEV):
            px, py, pc = _flip(x, k & 4), _flip(y, k & 2), _flip(c, k & 1)
            i = a * N_PEERS + k - 1
            copies.append(pltpu.make_async_remote_copy(
                src_ref=src.at[4 * px + 2 * py + pc] if scatter[a] else src, dst_ref=land.at[me],
                send_sem=send_sems.at[i], recv_sem=recv_sems.at[i], device_id=(px, py, pc), device_id_type=MESH))
    return copies


def _exchange_start(srcs, scatter, name):
    na = len(srcs)
    lands = [lax.empty(s.shape if sc else (N_DEV,) + s.shape, s.dtype) for s, sc in zip(srcs, scatter)]

    def body(*refs):
        copies = _exchange_copies(refs[:na], refs[na:2 * na], refs[2 * na], refs[2 * na + 1], scatter)
        for cp in copies:
            cp.start()
        token = refs[-1]
        token[...] = jnp.zeros_like(token)

    outs = pl.pallas_call(
        body, name=name,
        out_shape=(pltpu.SemaphoreType.DMA((na * N_PEERS,)), pltpu.SemaphoreType.DMA((na * N_PEERS,)))
        + tuple(pltpu.HBM(a.shape, a.dtype) for a in srcs + lands) + (jax.ShapeDtypeStruct((SUBLANES, LANES), f32),),
        in_specs=[HBM] * (2 * na),
        out_specs=(SEM, SEM) + (HBM,) * (2 * na) + (pl.BlockSpec(memory_space=pltpu.VMEM),),
        input_output_aliases={i: 2 + i for i in range(2 * na)},
        compiler_params=pltpu.CompilerParams(has_side_effects=EFFECT),
    )(*[pltpu.with_memory_space_constraint(a, pltpu.HBM) for a in srcs + lands])
    return outs[0], outs[1], list(outs[2:2 + na]), list(outs[2 + na:2 + 2 * na]), outs[-1]


def _exchange_wait(started, after, scatter, name):
    send_sems, recv_sems, srcs, lands, _ = started
    na = len(srcs)

    def body(*refs):
        for cp in _exchange_copies(refs[:na], refs[na:2 * na], refs[2 * na], refs[2 * na + 1], scatter):
            cp.wait_send()
            cp.wait_recv()

    outs = pl.pallas_call(
        body, name=name,
        out_shape=tuple(pltpu.HBM(a.shape, a.dtype) for a in srcs + lands),
        in_specs=[HBM] * (2 * na) + [SEM, SEM, ANY], out_specs=(HBM,) * (2 * na),
        input_output_aliases={i: i for i in range(2 * na)},
        compiler_params=pltpu.CompilerParams(has_side_effects=EFFECT),
    )(*srcs, *lands, send_sems, recv_sems, after)
    return list(outs[:na]), list(outs[na:])


def _own_block(land, block, me):
    return lax.dynamic_update_index_in_dim(land, block, me, 0)


def _sum8_adamw(r2, w, m, v, layer, name):
    _, rr, cc = r2.shape
    tr = _tile(rr, 256)
    bc1 = 1.0 - ADAM_B1 ** ADAM_STEP
    bc2 = 1.0 - ADAM_B2 ** ADAM_STEP

    def body(r_ref, w_ref, m_ref, v_ref, g_ref, d_ref, nm_ref, nv_ref):
        gv = r_ref[0].astype(f32)
        for q in range(1, N_DEV):
            gv = gv + r_ref[q].astype(f32)
        g_ref[...] = gv
        nm = ADAM_B1 * m_ref[...] + (1.0 - ADAM_B1) * gv
        nv = ADAM_B2 * v_ref[...] + (1.0 - ADAM_B2) * (gv * gv)
        nm_ref[...] = nm
        nv_ref[...] = nv
        d_ref[...] = -ADAM_LR * ((nm / bc1) / (jnp.sqrt(nv / bc2) + ADAM_EPS) + ADAM_WD * w_ref[...])

    lspec = pl.BlockSpec((None, tr, cc), lambda i: (layer, i, 0))
    return _pc(body, name=name, grid=(rr // tr,),
               in_specs=[pl.BlockSpec((N_DEV, tr, cc), lambda i: (0, i, 0)), lspec, lspec, lspec],
               out_specs=[_rows(tr, cc)] * 4, out_shape=[jax.ShapeDtypeStruct((rr, cc), f32)] * 4,
               sem=("parallel",))(r2, w, m, v)


def _sum_devices(recv, name):
    _, rr, _ = recv.shape
    tr = _tile(rr, 512)

    def body(r_ref, o_ref):
        acc = r_ref[0]
        for i in range(1, N_DEV):
            acc = acc + r_ref[i]
        o_ref[...] = acc

    return _pc(body, name=name, grid=(rr // tr,),
               in_specs=[pl.BlockSpec((N_DEV, tr, LANES), lambda i: (0, i, 0))],
               out_specs=_rows(tr, LANES), out_shape=jax.ShapeDtypeStruct((rr, LANES), f32), sem=("parallel",))(recv)


def _adamw(w, g, m, v):
    rr = w.shape[0]
    tr = _tile(rr, 512)
    bc1 = 1.0 - ADAM_B1 ** ADAM_STEP
    bc2 = 1.0 - ADAM_B2 ** ADAM_STEP

    def body(w_ref, g_ref, m_ref, v_ref, d_ref, nm_ref, nv_ref):
        gv = g_ref[...]
        nm = ADAM_B1 * m_ref[...] + (1.0 - ADAM_B1) * gv
        nv = ADAM_B2 * v_ref[...] + (1.0 - ADAM_B2) * (gv * gv)
        nm_ref[...] = nm
        nv_ref[...] = nv
        d_ref[...] = -ADAM_LR * ((nm / bc1) / (jnp.sqrt(nv / bc2) + ADAM_EPS) + ADAM_WD * w_ref[...])

    spec = _rows(tr, LANES)
    return _pc(body, name="adamw", grid=(rr // tr,), in_specs=[spec] * 4, out_specs=[spec] * 3,
               out_shape=[jax.ShapeDtypeStruct((rr, LANES), f32)] * 3, sem=("parallel",))(w, g, m, v)


PACK_ROWS = 512
PART_ROWS = SUBLANES


def _pack(arrs):
    parts, sizes = [], []
    for a in arrs:
        flat = a.reshape(-1)
        n = flat.shape[0]
        rows = -(-n // (LANES * PART_ROWS)) * PART_ROWS
        if rows * LANES != n:
            flat = jnp.pad(flat, (0, rows * LANES - n))
        parts.append(flat.reshape(rows, LANES))
        sizes.append((rows, n))
    total = sum(r for r, _ in sizes)
    padded = -(-total // PACK_ROWS) * PACK_ROWS
    if padded > total:
        parts.append(jnp.zeros((padded - total, LANES), parts[0].dtype))
    return jnp.concatenate(parts, axis=0), sizes


def _unpack(packed, sizes, shapes):
    out, off = [], 0
    for (rows, n), shp in zip(sizes, shapes):
        piece = lax.slice_in_dim(packed, off, off + rows, axis=0)
        if rows * LANES != n:
            piece = lax.slice_in_dim(piece.reshape(-1), 0, n, axis=0)
        out.append(piece.reshape(tuple(shp)))
        off += rows
    return out


def _cols_to_blocks(a):
    n = a.shape[-1] // N_DEV
    a = a.reshape(a.shape[:-1] + (N_DEV, n))
    return jnp.moveaxis(a, -2, 0)


def _blocks_to_cols(a):
    a = jnp.moveaxis(a, 0, -2)
    return a.reshape(a.shape[:-2] + (a.shape[-2] * a.shape[-1],))


def _rows_to_blocks(a):
    k = a.shape[-2] // N_DEV
    a = a.reshape(a.shape[:-2] + (N_DEV, k, a.shape[-1]))
    return jnp.moveaxis(a, -3, 0)


def _blocks_to_rows(a):
    a = jnp.moveaxis(a, 0, -3)
    return a.reshape(a.shape[:-3] + (a.shape[-3] * a.shape[-2], a.shape[-1]))


COL_SHARDED = ("cv_w_pw1", "gdn_w_in", "mlp_w1")
ROW_SHARDED = ("cv_w_pw2", "gdn_w_out", "mlp_w2")
CONV_SHARDED = ("cv_w_dw", "gdn_conv_w")
REPLICATED = ("norm_mix_g", "norm_ffn_g", "final_norm_g", "cv_b_pw1", "cv_b_dw", "cv_ln_g", "cv_ln_b", "cv_b_pw2",
              "gdn_a_log", "gdn_dt_bias", "gdn_norm_g")
WEIGHTS = ("norm_mix_g", "norm_ffn_g", "final_norm_g", "cv_w_pw1", "cv_b_pw1", "cv_w_dw", "cv_b_dw", "cv_ln_g",
           "cv_ln_b", "cv_w_pw2", "cv_b_pw2", "gdn_w_in", "gdn_conv_w", "gdn_a_log", "gdn_dt_bias", "gdn_norm_g",
           "gdn_w_out", "mlp_w1", "mlp_w2")
MATMUL_SHARDED = COL_SHARDED + ROW_SHARDED


def _squeeze_layer(name, a):
    if name in ("norm_mix_g", "norm_ffn_g", "final_norm_g", "mlp_w1", "mlp_w2"):
        return a
    return a[0]


def _gather_weights(shards):
    me = 4 * lax.axis_index("x") + 2 * lax.axis_index("y") + lax.axis_index("c")
    first = ("cv_w_pw1", "cv_w_pw2") + CONV_SHARDED
    got = dict(zip(first, _all_gather_many([shards[n] if n in CONV_SHARDED else shards[n].astype(bf16) for n in first])))
    now = {"cv_w_pw1": got["cv_w_pw1"], "cv_w_pw2": _blocks_to_rows(got["cv_w_pw2"])}
    for n in CONV_SHARDED:
        now[n] = _blocks_to_cols(got[n])

    def cast(a, tie):
        return (a + tie).astype(bf16)

    token, _ = lax.optimization_barrier((jnp.zeros((), f32), got["cv_w_pw1"]))
    later, started = {}, {}
    for group in ("mlp0", "gdn", "mlp1"):
        if group == "gdn":
            srcs = [cast(shards["gdn_w_in"], token).reshape(-1, LANES), cast(shards["gdn_w_out"], token)]
        else:
            layer = int(group[-1])
            srcs = [cast(shards["mlp_w1"][layer], token), cast(shards["mlp_w2"][layer], token)]
        later[group] = srcs
        started[group] = _exchange_start(srcs, [False] * len(srcs), f"weights_{group}_start")
        token = started[group][4][0, 0]

    def need(group, after):
        srcs, lands = _exchange_wait(started[group], after, [False] * len(later[group]), f"weights_{group}_wait")
        lands = [_own_block(ld, own, me) for ld, own in zip(lands, srcs)]
        if group == "gdn":
            w_in = _blocks_to_cols(lands[0].reshape((N_DEV,) + shards["gdn_w_in"].shape))
            return {"w_in": w_in, "w_out": _blocks_to_rows(lands[1])}
        return {"w1": lands[0], "w2": lands[1]}

    return now, need, token


def kernel(x, norm_mix_g, norm_ffn_g, final_norm_g, cv_w_pw1, cv_b_pw1, cv_w_dw, cv_b_dw, cv_ln_g, cv_ln_b, cv_w_pw2, cv_b_pw2, gdn_w_in, gdn_conv_w, gdn_a_log, gdn_dt_bias, gdn_norm_g, gdn_w_out, mlp_w1, mlp_w2, loss_target, m_norm_mix_g, m_norm_ffn_g, m_final_norm_g, m_cv_w_pw1, m_cv_b_pw1, m_cv_w_dw, m_cv_b_dw, m_cv_ln_g, m_cv_ln_b, m_cv_w_pw2, m_cv_b_pw2, m_gdn_w_in, m_gdn_conv_w, m_gdn_a_log, m_gdn_dt_bias, m_gdn_norm_g, m_gdn_w_out, m_mlp_w1, m_mlp_w2, v_norm_mix_g, v_norm_ffn_g, v_final_norm_g, v_cv_w_pw1, v_cv_b_pw1, v_cv_w_dw, v_cv_b_dw, v_cv_ln_g, v_cv_ln_b, v_cv_w_pw2, v_cv_b_pw2, v_gdn_w_in, v_gdn_conv_w, v_gdn_a_log, v_gdn_dt_bias, v_gdn_norm_g, v_gdn_w_out, v_mlp_w1, v_mlp_w2):
    w_in = dict(zip(WEIGHTS, (norm_mix_g, norm_ffn_g, final_norm_g, cv_w_pw1, cv_b_pw1, cv_w_dw, cv_b_dw, cv_ln_g, cv_ln_b, cv_w_pw2, cv_b_pw2, gdn_w_in, gdn_conv_w, gdn_a_log, gdn_dt_bias, gdn_norm_g, gdn_w_out, mlp_w1, mlp_w2)))
    m_in = dict(zip(WEIGHTS, (m_norm_mix_g, m_norm_ffn_g, m_final_norm_g, m_cv_w_pw1, m_cv_b_pw1, m_cv_w_dw, m_cv_b_dw, m_cv_ln_g, m_cv_ln_b, m_cv_w_pw2, m_cv_b_pw2, m_gdn_w_in, m_gdn_conv_w, m_gdn_a_log, m_gdn_dt_bias, m_gdn_norm_g, m_gdn_w_out, m_mlp_w1, m_mlp_w2)))
    v_in = dict(zip(WEIGHTS, (v_norm_mix_g, v_norm_ffn_g, v_final_norm_g, v_cv_w_pw1, v_cv_b_pw1, v_cv_w_dw, v_cv_b_dw, v_cv_ln_g, v_cv_ln_b, v_cv_w_pw2, v_cv_b_pw2, v_gdn_w_in, v_gdn_conv_w, v_gdn_a_log, v_gdn_dt_bias, v_gdn_norm_g, v_gdn_w_out, v_mlp_w1, v_mlp_w2)))
    me = 4 * lax.axis_index("x") + 2 * lax.axis_index("y") + lax.axis_index("c")

    shards = {n: _squeeze_layer(n, w_in[n]) for n in WEIGHTS}
    first, need, token = _gather_weights(shards)
    params = {n: shards[n] for n in REPLICATED}
    params.update(first)
    params["norm_mix_g"] = params["norm_mix_g"] + token

    def row_blocks(a):
        return a.reshape(N_DEV, a.shape[0] // N_DEV, a.shape[1])

    def flat_blocks(a):
        k, n8 = a.shape
        return _cols_to_blocks(a).reshape(N_DEV, k * (n8 // N_DEV) // LANES, LANES)

    def as_blocks(n, a):
        if n == "gdn_w_in":
            return flat_blocks(a).astype(bf16)
        return a if a.ndim == 3 else row_blocks(a)

    sent = []

    small = REPLICATED + CONV_SHARDED
    small_early = tuple(n for n in small if n != "norm_mix_g")
    small_info = {}

    def emit(group, grads_out):
        names, blocks, scatter = list(grads_out), [], []
        for n in names:
            if n == "small":
                packed, small_info["sizes"] = _pack([grads_out[n][k] for k in small_early])
                small_info["shapes"] = [grads_out[n][k].shape for k in small_early]
                blocks.append(packed)
                scatter.append(False)
            else:
                blocks.append(as_blocks(n, grads_out[n]))
                scatter.append(True)
        sent.append((names, _exchange_start(blocks, scatter, f"grads_{group}_start"), scatter))
        return sent[-1][1][4][0, 0]

    loss_part, grad_x, gr = _local_step(x, loss_target, params, need, emit)
    loss = lax.psum(loss_part, ("x", "y", "c"))
    nm_shape = gr["norm_mix_g"].shape
    sent.append((["norm_mix_g"], _exchange_start([gr["norm_mix_g"].reshape(-1, LANES)], [False], "grads_norm_mix_start"), [False]))

    recv = {}

    def finish(entry, after):
        names, st, scatter = entry
        srcs, lands = _exchange_wait(st, after, scatter, f"grads_{names[0]}_wait")
        for n, ld, blk, sc in zip(names, lands, srcs, scatter):
            own = lax.dynamic_index_in_dim(blk, me, 0, keepdims=False) if sc else blk
            recv[n] = _own_block(ld, own, me)
        return lands[0]

    def as3d(n, a):
        if n == "gdn_w_in":
            return a.reshape(a.shape[0], -1, LANES)
        return a

    big = [("cv_w_pw2", 0, "cv_w_pw2"), ("gdn_w_in", 0, "gdn_w_in"), ("gdn_w_out", 0, "gdn_w_out"),
           ("mlp_w1", 0, "mlp_w1_0"), ("mlp_w1", 1, "mlp_w1_1"), ("mlp_w2", 0, "mlp_w2_0"), ("mlp_w2", 1, "mlp_w2_1"),
           ("cv_w_pw1", 0, "cv_w_pw1")]
    res = {n: {} for n in MATMUL_SHARDED}
    after = grad_x
    for entry in sent[:-2]:
        after = finish(entry, after)
    for n, layer, key in big[:-1]:
        res[n][layer] = _sum8_adamw(recv[key], as3d(n, w_in[n]), as3d(n, m_in[n]), as3d(n, v_in[n]), layer, f"adamw_{key}")
        after = res[n][layer][0]
    after = finish(sent[-2], after)
    finish(sent[-1], after)
    n, layer, key = big[-1]
    res[n][layer] = _sum8_adamw(recv[key], as3d(n, w_in[n]), as3d(n, m_in[n]), as3d(n, v_in[n]), layer, f"adamw_{key}")

    grads = dict(zip(small_early, _unpack(_sum_devices(recv["small"], "grads_small_sum"), small_info["sizes"], small_info["shapes"])))
    grads["norm_mix_g"] = _sum_devices(recv["norm_mix_g"], "grads_norm_mix_sum").reshape(nm_shape)
    for n in CONV_SHARDED:
        cn = shards[n].shape[-1]
        grads[n] = lax.dynamic_slice_in_dim(grads[n], me * cn, cn, axis=1)

    out_groups = {n: [] for n in WEIGHTS}
    for n in MATMUL_SHARDED:
        layers = sorted(res[n])
        for k in range(4):
            pieces = [res[n][layer][k] for layer in layers]
            out_groups[n].append(jnp.stack(pieces).reshape(w_in[n].shape))

    sm_w = [shards[n] for n in small]
    sm_g = [grads[n].reshape(shards[n].shape) for n in small]
    sm_m = [_squeeze_layer(n, m_in[n]) for n in small]
    sm_v = [_squeeze_layer(n, v_in[n]) for n in small]
    wp, psz = _pack(sm_w)
    gp, _ = _pack(sm_g)
    mp, _ = _pack(sm_m)
    vp, _ = _pack(sm_v)
    dp, nmp, nvp = _adamw(wp, gp, mp, vp)
    shp = [a.shape for a in sm_w]
    for n, g, dl, nm, nv in zip(small, sm_g, _unpack(dp, psz, shp), _unpack(nmp, psz, shp), _unpack(nvp, psz, shp)):
        out_groups[n] = [a.reshape(w_in[n].shape) for a in (g, dl, nm, nv)]

    outs = [loss, grad_x]
    for k in range(4):
        outs += [out_groups[n][k] for n in WEIGHTS]
    return tuple(outs)
```

```python
import functools

import jax
import jax.numpy as jnp
from jax import lax
from jax.experimental import pallas as pl
from jax.experimental.pallas import tpu as pltpu

f32, bf16 = jnp.float32, jnp.bfloat16

NORM_EPS = 1e-6
L2_EPS = 1e-6
CHUNK = 64
LANES = 128
SUBLANES = 8
N_DEV = 8
VMEM_LIMIT = 56 * 1024 * 1024
CONV_PAD = 32
HEADS_PER_STEP = 8
PREP_HEADS_PER_STEP = 4
NEG = -1e30

ADAM_LR, ADAM_B1, ADAM_B2, ADAM_EPS, ADAM_WD, ADAM_STEP = 0.001, 0.9, 0.999, 1e-08, 0.01, 10

NT = (((1,), (1,)), ((), ()))
TN = (((0,), (0,)), ((), ()))
HI = lax.Precision.HIGHEST


def _pc(body, *, name, grid, in_specs, out_specs, out_shape, scratch=(), sem=None):
    return pl.pallas_call(
        body, name=name, grid=grid, in_specs=in_specs, out_specs=out_specs, out_shape=out_shape,
        scratch_shapes=list(scratch),
        compiler_params=pltpu.CompilerParams(dimension_semantics=sem, vmem_limit_bytes=VMEM_LIMIT))


def _rows(tm, n):
    return pl.BlockSpec((tm, n), lambda i: (i, 0))


def _const(shape):
    return pl.BlockSpec(shape, lambda *_: (0,) * len(shape))


def _resident(shape):
    return pl.BlockSpec(shape, lambda *_: (0,) * len(shape), pipeline_mode=pl.Buffered(1))


def _tile(t, pref):
    return pref if t % pref == 0 else t


def _dot(a, b):
    return jnp.dot(a.astype(bf16), b.astype(bf16), preferred_element_type=f32)


def _dot_nt(a, b):
    return lax.dot_general(a.astype(bf16), b.astype(bf16), NT, preferred_element_type=f32)


def _dot_tn(a, b):
    return lax.dot_general(a.astype(bf16), b.astype(bf16), TN, preferred_element_type=f32)


def _sigmoid(x):
    return 1.0 / (1.0 + jnp.exp(-x))


def _silu_grad(x):
    s = _sigmoid(x)
    return s * (1.0 + x * (1.0 - s))


def _rms(x, g):
    rstd = lax.rsqrt(jnp.mean(x * x, axis=-1, keepdims=True) + NORM_EPS)
    xh = x * rstd
    return xh * g, xh, rstd


def _rms_bwd(dn, xh, rstd, g):
    dxh = dn * g
    return rstd * (dxh - xh * jnp.mean(dxh * xh, axis=-1, keepdims=True))


def _acc_init(step, *refs):
    @pl.when(step == 0)
    def _():
        for r in refs:
            r[...] = jnp.zeros(r.shape, r.dtype)


def _acc_rows(ref, val):
    ref[0:1, :] += jnp.sum(val, axis=0, keepdims=True)


def _pw1_glu(x, g, w, b):
    t, d = x.shape
    tm = _tile(t, 256)

    nb_w = w.shape[0]

    def body(x_ref, g_ref, w_hbm, b_ref, n_ref, u_ref, gl_ref, w_ref, sems):
        _fetch_blocks(pl.program_id(0), w_hbm, w_ref, sems, True)
        n, _, _ = _rms(x_ref[...], g_ref[...])
        nb = n.astype(bf16)
        n_ref[...] = nb
        u = jnp.dot(nb, w_ref[...], preferred_element_type=f32) + b_ref[...]
        u_ref[...] = u.astype(bf16)
        gl_ref[...] = u[:, :d] * _sigmoid(u[:, d:])

    return _pc(body, name="pw1_glu", grid=(t // tm,),
               in_specs=[_rows(tm, d), _const((1, d)), ANY, _const((1, 2 * d))],
               out_specs=[_rows(tm, d), _rows(tm, 2 * d), _rows(tm, d)],
               out_shape=[jax.ShapeDtypeStruct((t, d), bf16), jax.ShapeDtypeStruct((t, 2 * d), bf16),
                          jax.ShapeDtypeStruct((t, d), f32)],
               scratch=[pltpu.VMEM((d, 2 * d), bf16), pltpu.SemaphoreType.DMA((nb_w,))],
               sem=("arbitrary",))(x, g, w, b)


def _ln_silu_mm_res(dc, ln_g, ln_b, w, b, res):
    t, d = dc.shape
    tm = _tile(t, 256)

    def body(x_ref, g_ref, bb_ref, w_ref, b_ref, r_ref, s_ref, o_ref):
        x = x_ref[...]
        xc = x - jnp.mean(x, axis=-1, keepdims=True)
        rstd = lax.rsqrt(jnp.mean(xc * xc, axis=-1, keepdims=True) + NORM_EPS)
        ln = xc * rstd * g_ref[...] + bb_ref[...]
        sb = (ln * _sigmoid(ln)).astype(bf16)
        s_ref[...] = sb
        o_ref[...] = r_ref[...] + jnp.dot(sb, w_ref[...], preferred_element_type=f32) + b_ref[...]

    return _pc(body, name="ln_silu_pw2", grid=(t // tm,),
               in_specs=[_rows(tm, d), _const((1, d)), _const((1, d)), _resident((d, d)), _const((1, d)), _rows(tm, d)],
               out_specs=[_rows(tm, d), _rows(tm, d)],
               out_shape=[jax.ShapeDtypeStruct((t, d), bf16), jax.ShapeDtypeStruct((t, d), f32)],
               sem=("parallel",))(dc, ln_g, ln_b, w, b, res)


def _fetch_blocks(step, w_hbm, dst, sems, by_cols, layer=None):
    nb_w = w_hbm.shape[0]
    step_rows, step_cols = w_hbm.shape[-2], w_hbm.shape[-1]

    @pl.when(step == 0)
    def _():
        copies = []
        for j in range(nb_w):
            src = w_hbm.at[j] if layer is None else w_hbm.at[j, layer]
            if by_cols:
                part = dst.at[:, pl.ds(j * step_cols, step_cols)]
            else:
                part = dst.at[pl.ds(j * step_rows, step_rows), :]
            copies.append(pltpu.make_async_copy(src, part, sems.at[j]))
        for cp in copies:
            cp.start()
        for cp in copies:
            cp.wait()


def _mlp_fwd(h, g, w1g, w2g, name, loss=None):
    t, d = h.shape
    nb_w, _, bn = w1g.shape
    ff = nb_w * bn
    tm = _tile(t, 256)
    n_in = 4 if loss is None else 6
    n_out = 4 if loss is None else 6

    def body(*refs):
        h_ref, g_ref, w1_hbm, w2_hbm = refs[:4]
        n_ref, f_ref, r_ref = refs[n_in:n_in + 3]
        w1_ref, w2_ref, sem1, sem2 = refs[n_in + n_out:]
        _fetch_blocks(pl.program_id(0), w1_hbm, w1_ref, sem1, True)
        _fetch_blocks(pl.program_id(0), w2_hbm, w2_ref, sem2, False)
        hv = h_ref[...]
        n, _, _ = _rms(hv, g_ref[...])
        nb = n.astype(bf16)
        n_ref[...] = nb
        f = jnp.dot(nb, w1_ref[...], preferred_element_type=f32)
        f_ref[...] = f.astype(bf16)
        rb = jnp.square(jnp.maximum(f, 0.0)).astype(bf16)
        r_ref[...] = rb
        out = hv + jnp.dot(rb, w2_ref[...], preferred_element_type=f32)
        if loss is None:
            refs[n_in + 3][...] = out
        else:
            gf_ref, t_ref = refs[4:6]
            loss_ref, dh_ref, dg_ref = refs[n_in + 3:n_in + 6]
            _acc_init(pl.program_id(0), loss_ref, dg_ref)
            gv = gf_ref[...]
            y, xh, rstd = _rms(out, gv)
            e = y - t_ref[...]
            loss_ref[...] += 0.5 * jnp.sum(jnp.mean(e * e, axis=-1, keepdims=True))
            dy = e * (1.0 / d)
            _acc_rows(dg_ref, dy * xh)
            dh_ref[...] = _rms_bwd(dy, xh, rstd, gv)

    in_specs = [_rows(tm, d), _const((1, d)), ANY, ANY]
    out_specs = [_rows(tm, d), _rows(tm, ff), _rows(tm, ff)]
    out_shape = [jax.ShapeDtypeStruct((t, d), bf16), jax.ShapeDtypeStruct((t, ff), bf16), jax.ShapeDtypeStruct((t, ff), bf16)]
    args = [h, g, w1g, w2g]
    if loss is None:
        out_specs.append(_rows(tm, d))
        out_shape.append(jax.ShapeDtypeStruct((t, d), f32))
    else:
        in_specs += [_const((1, d)), _rows(tm, d)]
        args += list(loss)
        out_specs += [_const((SUBLANES, LANES)), _rows(tm, d), _const((SUBLANES, d))]
        out_shape += [jax.ShapeDtypeStruct((SUBLANES, LANES), f32), jax.ShapeDtypeStruct((t, d), f32),
                      jax.ShapeDtypeStruct((SUBLANES, d), f32)]
    return _pc(body, name=name, grid=(t // tm,), in_specs=in_specs, out_specs=out_specs, out_shape=out_shape,
               scratch=[pltpu.VMEM((d, ff), bf16), pltpu.VMEM((ff, d), bf16),
                        pltpu.SemaphoreType.DMA((nb_w,)), pltpu.SemaphoreType.DMA((nb_w,))],
               sem=("arbitrary",))(*args)


def _softplus(x):
    return jnp.maximum(x, 0.0) + jnp.log(1.0 + jnp.exp(-jnp.abs(x)))


def _gdn_in(h, g, w_main, w_ab, a_log_pad, dt_pad, n_heads):
    t, d = h.shape
    tm = _tile(t, 256)

    def body(h_ref, g_ref, wm_ref, wab_ref, al_ref, dt_ref, n_ref, qkv_ref, z_ref, ab_ref, gb_ref):
        n, _, _ = _rms(h_ref[...], g_ref[...])
        nb = n.astype(bf16)
        n_ref[...] = nb
        p = jnp.dot(nb, wm_ref[...], preferred_element_type=f32)
        qkv_ref[...] = p[:, :3 * d]
        z_ref[...] = p[:, 3 * d:]
        ab = jnp.dot(nb, wab_ref[...], preferred_element_type=f32)
        ab_ref[...] = ab
        lane = lax.broadcasted_iota(jnp.int32, ab.shape, 1)
        decay = -jnp.exp(al_ref[...]) * _softplus(ab + dt_ref[...])
        gb_ref[...] = jnp.where(lane < n_heads, decay, jnp.where(lane < 2 * n_heads, _sigmoid(ab), 0.0))

    return _pc(body, name="gdn_in", grid=(t // tm,),
               in_specs=[_rows(tm, d), _const((1, d)), _resident((d, 4 * d)), _resident((d, LANES)),
                         _const((1, LANES)), _const((1, LANES))],
               out_specs=[_rows(tm, d), _rows(tm, 3 * d), _rows(tm, d), _rows(tm, LANES), _rows(tm, LANES)],
               out_shape=[jax.ShapeDtypeStruct((t, d), bf16), jax.ShapeDtypeStruct((t, 3 * d), f32),
                          jax.ShapeDtypeStruct((t, d), f32), jax.ShapeDtypeStruct((t, LANES), f32),
                          jax.ShapeDtypeStruct((t, LANES), f32)],
               sem=("parallel",))(h, g, w_main, w_ab, a_log_pad, dt_pad)


def _gated_norm_mm_res(o, z, ng, w, res, n_heads):
    t, d = o.shape
    tm = _tile(t, 256)

    def body(o_ref, z_ref, ng_ref, w_ref, r_ref, on_ref, out_ref):
        for hd in range(n_heads):
            sl = slice(hd * LANES, (hd + 1) * LANES)
            rn, _, _ = _rms(o_ref[:, sl], ng_ref[...])
            zz = z_ref[:, sl]
            on_ref[:, sl] = (rn * (zz * _sigmoid(zz))).astype(bf16)
        out_ref[...] = r_ref[...] + jnp.dot(on_ref[...], w_ref[...], preferred_element_type=f32)

    return _pc(body, name="gated_norm_wout", grid=(t // tm,),
               in_specs=[_rows(tm, d), _rows(tm, d), _const((1, LANES)), _resident((d, d)), _rows(tm, d)],
               out_specs=[_rows(tm, d), _rows(tm, d)],
               out_shape=[jax.ShapeDtypeStruct((t, d), bf16), jax.ShapeDtypeStruct((t, d), f32)],
               sem=("parallel",))(o, z, ng, w, res)


def _mlp_bwd(dho, h, g, fb, w1g, w2g, name):
    t, d = h.shape
    nb_w, _, bn = w1g.shape
    ff = nb_w * bn
    tm = _tile(t, 256)

    def body(do_ref, h_ref, g_ref, f_ref, w1_hbm, w2_hbm, df_ref, dh_ref, dg_ref, cs_ref, w1_ref, w2_ref, sem1, sem2):
        _fetch_blocks(pl.program_id(0), w1_hbm, w1_ref, sem1, True)
        _fetch_blocks(pl.program_id(0), w2_hbm, w2_ref, sem2, False)
        _acc_init(pl.program_id(0), dg_ref, cs_ref)
        do = do_ref[...]
        dr = lax.dot_general(do.astype(bf16), w2_ref[...], NT, preferred_element_type=f32)
        dfb = (dr * (2.0 * jnp.maximum(f_ref[...].astype(f32), 0.0))).astype(bf16)
        df_ref[...] = dfb
        dn = lax.dot_general(dfb, w1_ref[...], NT, preferred_element_type=f32)
        gv = g_ref[...]
        _, xh, rstd = _rms(h_ref[...], gv)
        _acc_rows(dg_ref, dn * xh)
        dh = do + _rms_bwd(dn, xh, rstd, gv)
        dh_ref[...] = dh
        _acc_rows(cs_ref, dh)

    return _pc(body, name=name, grid=(t // tm,),
               in_specs=[_rows(tm, d), _rows(tm, d), _const((1, d)), _rows(tm, ff), ANY, ANY],
               out_specs=[_rows(tm, ff), _rows(tm, d), _const((SUBLANES, d)), _const((SUBLANES, d))],
               out_shape=[jax.ShapeDtypeStruct((t, ff), bf16), jax.ShapeDtypeStruct((t, d), f32),
                          jax.ShapeDtypeStruct((SUBLANES, d), f32), jax.ShapeDtypeStruct((SUBLANES, d), f32)],
               scratch=[pltpu.VMEM((d, ff), bf16), pltpu.VMEM((ff, d), bf16),
                        pltpu.SemaphoreType.DMA((nb_w,)), pltpu.SemaphoreType.DMA((nb_w,))],
               sem=("arbitrary",))(dho, h, g, fb, w1g, w2g)


class _Tail:
    def __init__(self, fn, ins, outs):
        self.fn, self.ins, self.outs = fn, ins, outs


def _mm_nt(pairs, name, tail=None):
    t = pairs[0][0].shape[0]
    tm = _tile(t, 256)
    npair = len(pairs)
    in_specs, args, scratch, blocked = [], [], [], []
    k = None
    for dy, w in pairs:
        nn = dy.shape[1]
        if isinstance(w, tuple) and isinstance(w[0], str):
            w = w[1]
            k = w.shape[1]
            wspec = ANY
            blocked.append(True)
            scratch += [pltpu.VMEM((k, nn), bf16), pltpu.SemaphoreType.DMA((w.shape[0],))]
        elif isinstance(w, tuple):
            w, idx = w
            k = w.shape[0]
            wspec = pl.BlockSpec((k, nn), lambda *_, idx=idx: (0, idx), pipeline_mode=pl.Buffered(1))
            blocked.append(False)
        else:
            k = w.shape[0]
            wspec = _resident(w.shape)
            blocked.append(False)
        in_specs += [_rows(tm, nn), wspec]
        args += [dy, w]
    n_tin = len(tail.ins) if tail else 0
    n_out = len(tail.outs) if tail else 1
    if tail:
        for arr, kind in tail.ins:
            in_specs.append(_rows(tm, arr.shape[1]) if kind == "rows" else _const(arr.shape))
            args.append(arr)
        out_specs = [_rows(tm, c) if kind == "rows" else _const((SUBLANES, c)) for c, kind in tail.outs]
        out_shape = [jax.ShapeDtypeStruct((t, c) if kind == "rows" else (SUBLANES, c), f32) for c, kind in tail.outs]
    else:
        out_specs = _rows(tm, k)
        out_shape = jax.ShapeDtypeStruct((t, k), f32)

    def body(*refs):
        step = pl.program_id(0)
        tin = refs[2 * npair:2 * npair + n_tin]
        outs = refs[2 * npair + n_tin:2 * npair + n_tin + n_out]
        scr = list(refs[2 * npair + n_tin + n_out:])
        acc = None
        for p in range(npair):
            w_ref = refs[2 * p + 1]
            if blocked[p]:
                w_vmem, sems = scr.pop(0), scr.pop(0)
                _fetch_blocks(step, w_ref, w_vmem, sems, True)
                w_ref = w_vmem
            part = lax.dot_general(refs[2 * p][...].astype(bf16), w_ref[...], NT, preferred_element_type=f32)
            acc = part if acc is None else acc + part
        if tail is None:
            outs[0][...] = acc
        else:
            _acc_init(step, *[o for o, (_, kind) in zip(outs, tail.outs) if kind == "acc"])
            tail.fn(acc, tin, outs)

    sequential = tail is not None or any(blocked)
    return _pc(body, name=name, grid=(t // tm,), in_specs=in_specs, out_specs=out_specs, out_shape=out_shape,
               scratch=scratch, sem=("arbitrary",) if sequential else ("parallel",))(*args)


def _rms_bwd_tail(h, g, dres):
    def fn(dn, ins, outs):
        h_ref, g_ref, dr_ref = ins
        dh_ref, dg_ref = outs
        gv = g_ref[...]
        _, xh, rstd = _rms(h_ref[...], gv)
        _acc_rows(dg_ref, dn * xh)
        dh_ref[...] = dr_ref[...] + _rms_bwd(dn, xh, rstd, gv)

    d = h.shape[1]
    return _Tail(fn, [(h, "rows"), (g, "const"), (dres, "rows")], [(d, "rows"), (d, "acc")])


def _ln_silu_bwd_tail(dc, ln_g, ln_b):
    def fn(ds, ins, outs):
        x_ref, g_ref, b_ref = ins
        dx_ref, dg_ref, db_ref, cs_ref = outs
        x = x_ref[...]
        gv = g_ref[...]
        xc = x - jnp.mean(x, axis=-1, keepdims=True)
        rstd = lax.rsqrt(jnp.mean(xc * xc, axis=-1, keepdims=True) + NORM_EPS)
        xh = xc * rstd
        dln = ds * _silu_grad(xh * gv + b_ref[...])
        _acc_rows(dg_ref, dln * xh)
        _acc_rows(db_ref, dln)
        dxh = dln * gv
        dx = rstd * (dxh - jnp.mean(dxh, axis=-1, keepdims=True) - xh * jnp.mean(dxh * xh, axis=-1, keepdims=True))
        dx_ref[...] = dx
        _acc_rows(cs_ref, dx)

    d = dc.shape[1]
    return _Tail(fn, [(dc, "rows"), (ln_g, "const"), (ln_b, "const")], [(d, "rows"), (d, "acc"), (d, "acc"), (d, "acc")])


def _gated_norm_bwd_tail(o, z, ng, n_heads):
    def fn(don_all, ins, outs):
        o_ref, z_ref, ng_ref = ins
        do_ref, dz_ref, dng_ref = outs
        gv = ng_ref[...]
        for hd in range(n_heads):
            sl = slice(hd * LANES, (hd + 1) * LANES)
            rn, xh, rstd = _rms(o_ref[:, sl], gv)
            zz = z_ref[:, sl]
            don = don_all[:, sl]
            dz_ref[:, sl] = don * rn * _silu_grad(zz)
            drn = don * (zz * _sigmoid(zz))
            _acc_rows(dng_ref, drn * xh)
            do_ref[:, sl] = _rms_bwd(drn, xh, rstd, gv)

    d = o.shape[1]
    return _Tail(fn, [(o, "rows"), (z, "rows"), (ng, "const")], [(d, "rows"), (d, "rows"), (LANES, "acc")])


def _mm_tn_blocked(x, dy, name, out_dtype=f32):
    t, k = x.shape
    bn = dy.shape[1] // N_DEV
    tm = _tile(t, 512)
    jb = N_DEV
    while jb > 1 and k * jb * bn * 4 > 8 * 1024 * 1024:
        jb //= 2
    nt = t // tm

    def body(x_ref, dy_ref, o_ref, *acc):
        acc_ref = acc[0] if acc else o_ref
        _acc_init(pl.program_id(1), acc_ref)
        xt = x_ref[...].astype(bf16).T
        for jj in range(jb):
            acc_ref[jj] += jnp.dot(xt, dy_ref[:, jj * bn:(jj + 1) * bn].astype(bf16), preferred_element_type=f32)
        if acc:
            @pl.when(pl.program_id(1) == nt - 1)
            def _():
                o_ref[...] = acc_ref[...].astype(out_dtype)

    return _pc(body, name=name, grid=(N_DEV // jb, nt),
               in_specs=[pl.BlockSpec((tm, k), lambda j, i: (i, 0)), pl.BlockSpec((tm, jb * bn), lambda j, i: (i, j))],
               out_specs=pl.BlockSpec((jb, k, bn), lambda j, i: (j, 0, 0)),
               out_shape=jax.ShapeDtypeStruct((N_DEV, k, bn), out_dtype),
               scratch=[] if out_dtype == f32 else [pltpu.VMEM((jb, k, bn), f32)],
               sem=("parallel", "arbitrary"))(x, dy)


def _mm_tn(x, dy, name, out_dtype=f32):
    t, k = x.shape
    n = dy.shape[1]
    tm = _tile(t, 512)
    cap = max(LANES, (2 * 1024 * 1024) // k)
    tn = n
    if n > cap:
        tn = max(c for c in range(LANES, cap + 1, LANES) if n % c == 0)
    nt = t // tm

    def body(x_ref, dy_ref, o_ref, *acc):
        acc_ref = acc[0] if acc else o_ref
        _acc_init(pl.program_id(1), acc_ref)
        acc_ref[...] += lax.dot_general(x_ref[...].astype(bf16), dy_ref[...].astype(bf16), TN, preferred_element_type=f32)
        if acc:
            @pl.when(pl.program_id(1) == nt - 1)
            def _():
                o_ref[...] = acc_ref[...].astype(out_dtype)

    return _pc(body, name=name, grid=(n // tn, nt),
               in_specs=[pl.BlockSpec((tm, k), lambda j, i: (i, 0)), pl.BlockSpec((tm, tn), lambda j, i: (i, j))],
               out_specs=pl.BlockSpec((k, tn), lambda j, i: (0, j)),
               out_shape=jax.ShapeDtypeStruct((k, n), out_dtype),
               scratch=[] if out_dtype == f32 else [pltpu.VMEM((k, tn), f32)],
               sem=("parallel", "arbitrary"))(x, dy)


def _gates_bwd(dgb, ab, a_log_pad, dt_pad, n_heads):
    t = ab.shape[0]
    tm = _tile(t, 256)

    def body(dgb_ref, ab_ref, al_ref, dt_ref, dab_ref, dal_ref, ddt_ref):
        _acc_init(pl.program_id(0), dal_ref, ddt_ref)
        ab = ab_ref[...]
        dgb = dgb_ref[...]
        lane = lax.broadcasted_iota(jnp.int32, ab.shape, 1)
        is_a = lane < n_heads
        is_b = jnp.logical_and(lane >= n_heads, lane < 2 * n_heads)
        xa = ab + dt_ref[...]
        neg_a = -jnp.exp(al_ref[...])
        dg_da = neg_a * _sigmoid(xa)
        beta = _sigmoid(ab)
        da = jnp.where(is_a, dgb * dg_da, 0.0)
        dab_ref[...] = da + jnp.where(is_b, dgb * beta * (1.0 - beta), 0.0)
        _acc_rows(dal_ref, jnp.where(is_a, dgb * neg_a * _softplus(xa), 0.0))
        _acc_rows(ddt_ref, da)

    return _pc(body, name="gates_bwd", grid=(t // tm,),
               in_specs=[_rows(tm, LANES), _rows(tm, LANES), _const((1, LANES)), _const((1, LANES))],
               out_specs=[_rows(tm, LANES), _const((SUBLANES, LANES)), _const((SUBLANES, LANES))],
               out_shape=[jax.ShapeDtypeStruct((t, LANES), f32), jax.ShapeDtypeStruct((SUBLANES, LANES), f32),
                          jax.ShapeDtypeStruct((SUBLANES, LANES), f32)],
               sem=("arbitrary",))(dgb, ab, a_log_pad, dt_pad)


def _glu_bwd(dgl, ub):
    t, d = dgl.shape
    tm = _tile(t, 256)

    def body(dgl_ref, u_ref, du_ref, cs_ref):
        _acc_init(pl.program_id(0), cs_ref)
        dgl = dgl_ref[...]
        a = u_ref[:, :d].astype(f32)
        sb = _sigmoid(u_ref[:, d:].astype(f32))
        da = dgl * sb
        db = dgl * a * sb * (1.0 - sb)
        du_ref[:, :d] = da.astype(bf16)
        du_ref[:, d:] = db.astype(bf16)
        cs_ref[0:1, :d] += jnp.sum(da, axis=0, keepdims=True)
        cs_ref[0:1, d:] += jnp.sum(db, axis=0, keepdims=True)

    return _pc(body, name="glu_bwd", grid=(t // tm,),
               in_specs=[_rows(tm, d), _rows(tm, 2 * d)],
               out_specs=[_rows(tm, 2 * d), _const((SUBLANES, 2 * d))],
               out_shape=[jax.ShapeDtypeStruct((t, 2 * d), bf16), jax.ShapeDtypeStruct((SUBLANES, 2 * d), f32)],
               sem=("arbitrary",))(dgl, ub)


def _conv_rows(s):
    return 256 if s % 256 == 0 else s


def _conv_tap_sum(pad_ref, w_ref, base, rows, width):
    acc = jnp.zeros((rows, LANES), f32)
    for j in range(width):
        acc = acc + w_ref[j:j + 1, :] * pad_ref[pl.ds(base + CONV_PAD - (width - 1) + j, rows), :]
    return acc


def _l2_silu_post(c, j, n_heads, scale):
    a = c * _sigmoid(c)
    r = lax.rsqrt(jnp.sum(a * a, axis=-1, keepdims=True) + L2_EPS)
    mult = jnp.where(j < n_heads, r * scale, jnp.where(j < 2 * n_heads, r, 1.0))
    return a, r, a * mult


def _dwconv_fwd(x, w, b, name, qk_heads=None):
    bl, s, cn = x.shape
    width = w.shape[0]
    rows = _conv_rows(s)
    scale = float(LANES) ** -0.5

    def body(x_ref, w_ref, b_ref, o_ref, pad_ref):
        j = pl.program_id(1)
        pad_ref[0:CONV_PAD, :] = jnp.zeros((CONV_PAD, LANES), f32)
        pad_ref[CONV_PAD:, :] = x_ref[0]

        def step(i, carry):
            base = pl.multiple_of(i * rows, rows)
            acc = _conv_tap_sum(pad_ref, w_ref, base, rows, width)
            if qk_heads is None:
                acc = acc + b_ref[...]
            else:
                _, _, acc = _l2_silu_post(acc, j, qk_heads, scale)
            o_ref[0, pl.ds(base, rows), :] = acc
            return carry

        lax.fori_loop(0, s // rows, step, 0)

    return _pc(body, name=name, grid=(bl, cn // LANES),
               in_specs=[pl.BlockSpec((1, s, LANES), lambda bi, j: (bi, 0, j)),
                         pl.BlockSpec((width, LANES), lambda bi, j: (0, j)),
                         pl.BlockSpec((1, LANES), lambda bi, j: (0, j))],
               out_specs=pl.BlockSpec((1, s, LANES), lambda bi, j: (bi, 0, j)),
               out_shape=jax.ShapeDtypeStruct((bl, s, cn), f32),
               scratch=[pltpu.VMEM((s + CONV_PAD, LANES), f32)],
               sem=("parallel", "parallel"))(x, w, b)


def _dwconv_bwd(x, dys, w, name, qk_heads=None):
    bl, s, cn = x.shape
    width = w.shape[0]
    wp = -(-width // SUBLANES) * SUBLANES
    rows = _conv_rows(s)
    scale = float(LANES) ** -0.5
    nblk = s // rows
    ndy = len(dys)

    def body(*refs):
        x_ref, w_ref = refs[0], refs[1]
        dy_refs = refs[2:2 + ndy]
        dx_ref, dw_ref, db_ref, xpad, dypad, acc = refs[2 + ndy:]
        j = pl.program_id(0)
        bi = pl.program_id(1)
        _acc_init(bi, acc, db_ref)
        xpad[0:CONV_PAD, :] = jnp.zeros((CONV_PAD, LANES), f32)
        xpad[CONV_PAD:, :] = x_ref[0]
        dypad[s:, :] = jnp.zeros((CONV_PAD, LANES), f32)
        if qk_heads is None:
            dypad[0:s, :] = dy_refs[0][0]
        else:
            def pre(i, carry):
                base = pl.multiple_of(i * rows, rows)
                c = _conv_tap_sum(xpad, w_ref, base, rows, width)
                a, r, _ = _l2_silu_post(c, j, qk_heads, scale)
                dq = dy_refs[0][0, pl.ds(base, rows), :]
                dk = dy_refs[1][0, pl.ds(base, rows), :]
                dv = dy_refs[2][0, pl.ds(base, rows), :]
                dy = jnp.where(j < qk_heads, dq * scale, jnp.where(j < 2 * qk_heads, dk, dv))
                da_l2 = r * (dy - a * (r * r) * jnp.sum(a * dy, axis=-1, keepdims=True))
                da = jnp.where(j < 2 * qk_heads, da_l2, dy)
                dypad[pl.ds(base, rows), :] = da * _silu_grad(c)
                return carry

            lax.fori_loop(0, nblk, pre, 0)

        def step(i, carry):
            base = pl.multiple_of(i * rows, rows)
            dxa = jnp.zeros((rows, LANES), f32)
            for jj in range(width):
                dxa = dxa + w_ref[jj:jj + 1, :] * dypad[pl.ds(base + (width - 1) - jj, rows), :]
            dx_ref[0, pl.ds(base, rows), :] = dxa
            dyc = dypad[pl.ds(base, rows), :]
            db_ref[...] += dyc.reshape(rows // SUBLANES, SUBLANES, LANES).sum(axis=0)
            for jj in range(width):
                prod = dyc * xpad[pl.ds(base + CONV_PAD - (width - 1) + jj, rows), :]
                acc[jj * SUBLANES:(jj + 1) * SUBLANES, :] += prod.reshape(rows // SUBLANES, SUBLANES, LANES).sum(axis=0)
            return carry

        lax.fori_loop(0, nblk, step, 0)

        @pl.when(bi == bl - 1)
        def _():
            dw_ref[...] = jnp.zeros((wp, LANES), f32)
            for jj in range(width):
                dw_ref[jj:jj + 1, :] = jnp.sum(acc[jj * SUBLANES:(jj + 1) * SUBLANES, :], axis=0, keepdims=True)

    if qk_heads is None:
        dy_specs = [pl.BlockSpec((1, s, LANES), lambda j, bi: (bi, 0, j))]
    else:
        hh = qk_heads
        def dy_spec(part):
            def index(j, bi):
                mine = jnp.logical_and(j >= part * hh, j < (part + 1) * hh)
                return (jnp.where(mine, bi * hh + j - part * hh, 0), 0, 0)
            return pl.BlockSpec((1, s, LANES), index)

        dy_specs = [dy_spec(0), dy_spec(1), dy_spec(2)]
    return _pc(body, name=name, grid=(cn // LANES, bl),
               in_specs=[pl.BlockSpec((1, s, LANES), lambda j, bi: (bi, 0, j)),
                         pl.BlockSpec((width, LANES), lambda j, bi: (0, j))] + dy_specs,
               out_specs=[pl.BlockSpec((1, s, LANES), lambda j, bi: (bi, 0, j)),
                          pl.BlockSpec((wp, LANES), lambda j, bi: (0, j)),
                          pl.BlockSpec((SUBLANES, LANES), lambda j, bi: (0, j))],
               out_shape=[jax.ShapeDtypeStruct((bl, s, cn), f32), jax.ShapeDtypeStruct((wp, cn), f32),
                          jax.ShapeDtypeStruct((SUBLANES, cn), f32)],
               scratch=[pltpu.VMEM((s + CONV_PAD, LANES), f32), pltpu.VMEM((s + CONV_PAD, LANES), f32),
                        pltpu.VMEM((width * SUBLANES, LANES), f32)],
               sem=("parallel", "arbitrary"))(x, w, *dys)


def _group_rows(s):
    for rows in (256, 128):
        if s % rows == 0:
            return rows
    return CHUNK


def _group_masks(rows):
    r = lax.broadcasted_iota(jnp.int32, (rows, rows), 0)
    c = lax.broadcasted_iota(jnp.int32, (rows, rows), 1)
    return r, c, r >= c


def _decay(gc_col, gc_row, causal):
    return jnp.exp(jnp.where(causal, gc_col - gc_row, NEG))


def _inv_unit_lower_many(mats, r, c):
    n = r.shape[0]
    eye = (r == c).astype(f32)
    same16 = (r >> 4) == (c >> 4)
    ads = [jnp.where(same16, a, 0.0) for a in mats]
    aos = [a - ad for a, ad in zip(mats, ads)]
    xs = ads
    tds = [eye - x for x in xs]
    for _ in range(3):
        xs = [_dot(x, x) for x in xs]
        tds = [td + _dot(td, x) for td, x in zip(tds, xs)]
    bs = [_dot(td, ao) for td, ao in zip(tds, aos)]
    ps = [eye - b for b in bs]
    span = 2
    while span < n // 16:
        bs = [_dot(b, b) for b in bs]
        ps = [p + _dot(p, b) for p, b in zip(ps, bs)]
        span *= 2
    return [_dot(p, td) for p, td in zip(ps, tds)]


def _lane_cumsum(x, y, reverse, name):
    rr, n = x.shape

    def body(x_ref, y_ref, o_ref):
        i = lax.broadcasted_iota(jnp.int32, (n, n), 0)
        j = lax.broadcasted_iota(jnp.int32, (n, n), 1)
        tri = ((i >= j) if reverse else (i <= j)).astype(f32)
        o_ref[...] = jnp.dot(x_ref[...] + y_ref[...], tri, precision=HI, preferred_element_type=f32)

    spec = pl.BlockSpec((rr, n), lambda: (0, 0))
    return pl.pallas_call(body, name=name, in_specs=[spec, spec], out_specs=spec,
                          out_shape=jax.ShapeDtypeStruct((rr, n), f32))(x, y)


def _gdn_specs(n_heads, nblk, rows, hp=1, rev=False):
    def blk(n):
        return nblk - 1 - n if rev else n

    def qkv(off):
        return pl.BlockSpec((rows, hp * LANES), lambda g, n: (
            lax.div(g * hp, n_heads) * nblk + blk(n), lax.div(off * n_heads + lax.rem(g * hp, n_heads), hp)))

    def per_head(last, block_rows=rows):
        return pl.BlockSpec((hp, block_rows, last), lambda g, n: (g, blk(n), 0))

    def row_vec():
        return pl.BlockSpec((hp, 1, 1, rows), lambda g, n: (g, blk(n), 0, 0))

    return qkv, per_head, row_vec


def _gdn_prep(qkv, cols, grow, bl, s, n_heads):
    rows = _group_rows(s)
    nblk = s // rows
    bh_n = bl * n_heads
    hp = PREP_HEADS_PER_STEP if n_heads % PREP_HEADS_PER_STEP == 0 else 1
    qkv_spec, ph, rv = _gdn_specs(n_heads, nblk, rows, hp)

    def body(k_ref, v_ref, cols_ref, grow_ref, u_ref, w_ref, t_ref):
        r, c, causal = _group_masks(rows)
        mats, rhs = [], []
        for h in range(hp):
            hs = slice(h * LANES, (h + 1) * LANES)
            k = k_ref[:, hs]
            gc = cols_ref[h, :, 0:1]
            beta = cols_ref[h, :, 1:2]
            kb = k * beta
            mats.append(jnp.where(r > c, _dot_nt(kb, k) * _decay(gc, grow_ref[h, 0], causal), 0.0))
            rhs.append((v_ref[:, hs] * beta, kb * jnp.exp(gc)))
        for h, tm in enumerate(_inv_unit_lower_many(mats, r, c)):
            tb = tm.astype(bf16)
            u_ref[h] = jnp.dot(tb, rhs[h][0].astype(bf16), preferred_element_type=f32)
            w_ref[h] = jnp.dot(tb, rhs[h][1].astype(bf16), preferred_element_type=f32)
            t_ref[h] = tb

    return _pc(body, name="gdn_prep", grid=(bh_n // hp, nblk),
               in_specs=[qkv_spec(1), qkv_spec(2), ph(2), rv()],
               out_specs=[ph(LANES), ph(LANES), ph(rows)],
               out_shape=[jax.ShapeDtypeStruct((bh_n, s, LANES), f32), jax.ShapeDtypeStruct((bh_n, s, LANES), f32),
                          jax.ShapeDtypeStruct((bh_n, s, rows), bf16)],
               sem=("parallel", "parallel"))(qkv, qkv, cols, grow)


def _gdn_scan(qkv, u, w, cols, grow, bl, s, n_heads):
    rows = _group_rows(s)
    nblk = s // rows
    bh_n = bl * n_heads
    d = n_heads * LANES
    hp = HEADS_PER_STEP if n_heads % HEADS_PER_STEP == 0 else 1
    qkv_spec, ph, rv = _gdn_specs(n_heads, nblk, rows, hp)

    def body(q_ref, k_ref, u_ref, w_ref, cols_ref, grow_ref, o_ref, vn_ref, ss_ref, s_scr):
        _acc_init(pl.program_id(1), s_scr)
        _, _, causal = _group_masks(rows)
        hh = range(hp)
        qs = [q_ref[:, h * LANES:(h + 1) * LANES] for h in hh]
        ks = [k_ref[:, h * LANES:(h + 1) * LANES] for h in hh]
        gcs = [cols_ref[h, :, 0:1] for h in hh]
        ps = [_dot_nt(qs[h], ks[h]) * _decay(gcs[h], grow_ref[h, 0], causal) for h in hh]
        sts = [s_scr[h] for h in hh]
        for h in hh:
            ss_ref[h] = sts[h]
        vns = [u_ref[h] - _dot(w_ref[h], sts[h]) for h in hh]
        for h in hh:
            vn_ref[h] = vns[h]
        o_state = [_dot(qs[h] * jnp.exp(gcs[h]), sts[h]) for h in hh]
        o_intra = [_dot(ps[h], vns[h]) for h in hh]
        for h in hh:
            o_ref[:, h * LANES:(h + 1) * LANES] = o_state[h] + o_intra[h]
        for h in hh:
            g_last = gcs[h][rows - 1:rows, :]
            s_scr[h] = jnp.exp(g_last) * sts[h] + _dot_tn(ks[h] * jnp.exp(g_last - gcs[h]), vns[h])

    return _pc(body, name="gdn_scan", grid=(bh_n // hp, nblk),
               in_specs=[qkv_spec(0), qkv_spec(1), ph(LANES), ph(LANES), ph(2), rv()],
               out_specs=[qkv_spec(0), ph(LANES), ph(LANES, block_rows=LANES)],
               out_shape=[jax.ShapeDtypeStruct((bl * s, d), f32), jax.ShapeDtypeStruct((bh_n, s, LANES), f32),
                          jax.ShapeDtypeStruct((bh_n, nblk * LANES, LANES), f32)],
               scratch=[pltpu.VMEM((hp, LANES, LANES), f32)],
               sem=("parallel", "arbitrary"))(qkv, qkv, u, w, cols, grow)


def _gdn_scan_bwd(do, qkv, w, vn, cols, grow, ss, bl, s, n_heads):
    rows = _group_rows(s)
    nblk = s // rows
    bh_n = bl * n_heads
    hp = HEADS_PER_STEP if n_heads % HEADS_PER_STEP == 0 else 1
    qkv_spec, ph, rv = _gdn_specs(n_heads, nblk, rows, hp, rev=True)

    def body(do_ref, q_ref, k_ref, w_ref, vn_ref, cols_ref, grow_ref, ss_ref,
             du_ref, dw_ref, dq_ref, dk_ref, dcol_ref, drow_ref, ds_scr):
        _acc_init(pl.program_id(1), ds_scr)
        _, _, causal = _group_masks(rows)
        last_row = lax.broadcasted_iota(jnp.int32, (rows, 1), 0) == rows - 1
        hh = range(hp)
        dos = [do_ref[:, h * LANES:(h + 1) * LANES] for h in hh]
        qs = [q_ref[:, h * LANES:(h + 1) * LANES] for h in hh]
        ks = [k_ref[:, h * LANES:(h + 1) * LANES] for h in hh]
        vns = [vn_ref[h] for h in hh]
        gcs = [cols_ref[h, :, 0:1] for h in hh]
        sts = [ss_ref[h] for h in hh]
        dss = [ds_scr[h] for h in hh]
        dmats = [_decay(gcs[h], grow_ref[h, 0], causal) for h in hh]
        gams = [jnp.exp(gc) for gc in gcs]
        qgs = [qs[h] * gams[h] for h in hh]
        g_lasts = [gc[rows - 1:rows, :] for gc in gcs]
        kd_scales = [jnp.exp(g_lasts[h] - gcs[h]) for h in hh]
        kdecs = [ks[h] * kd_scales[h] for h in hh]
        qks = [_dot_nt(qs[h], ks[h]) for h in hh]
        dpds = [_dot_nt(dos[h], vns[h]) * dmats[h] for h in hh]
        dvns = [_dot_tn(qks[h] * dmats[h], dos[h]) + _dot(kdecs[h], dss[h]) for h in hh]
        for h in hh:
            du_ref[h] = dvns[h]
        dkdecs = [_dot_nt(vns[h], dss[h]) for h in hh]
        for h in hh:
            dw_ref[h] = -_dot_nt(dvns[h], sts[h])
        dqgs = [_dot_nt(dos[h], sts[h]) for h in hh]
        dq_intra = [_dot(dpds[h], ks[h]) for h in hh]
        dk_intra = [_dot_tn(dpds[h], qs[h]) for h in hh]
        for h in hh:
            dq_ref[h] = dqgs[h] * gams[h] + dq_intra[h]
            dk_ref[h] = dk_intra[h] + dkdecs[h] * kd_scales[h]
            ep = dpds[h] * qks[h]
            drow_ref[h, 0] = -jnp.sum(ep, axis=0, keepdims=True)
            kd_rows = jnp.sum(dkdecs[h] * kdecs[h], axis=-1, keepdims=True)
            extra = jnp.sum(kd_rows) + jnp.exp(g_lasts[h]) * jnp.sum(sts[h] * dss[h])
            dcol_ref[h] = (jnp.sum(dqgs[h] * qgs[h], axis=-1, keepdims=True) + jnp.sum(ep, axis=-1, keepdims=True)
                           - kd_rows + jnp.where(last_row, extra, 0.0))
        ds_new = [jnp.exp(g_lasts[h]) * dss[h] + _dot_tn(qgs[h], dos[h]) - _dot_tn(w_ref[h], dvns[h]) for h in hh]
        for h in hh:
            ds_scr[h] = ds_new[h]

    return _pc(body, name="gdn_scan_bwd", grid=(bh_n // hp, nblk),
               in_specs=[qkv_spec(0), qkv_spec(0), qkv_spec(1), ph(LANES), ph(LANES), ph(2), rv(),
                         ph(LANES, block_rows=LANES)],
               out_specs=[ph(LANES), ph(LANES), ph(LANES), ph(LANES), ph(1), rv()],
               out_shape=[jax.ShapeDtypeStruct((bh_n, s, LANES), f32)] * 4
               + [jax.ShapeDtypeStruct((bh_n, s, 1), f32), jax.ShapeDtypeStruct((bh_n, nblk, 1, rows), f32)],
               scratch=[pltpu.VMEM((hp, LANES, LANES), f32)],
               sem=("parallel", "arbitrary"))(do, qkv, qkv, w, vn, cols, grow, ss)


def _gdn_prep_bwd(qkv, cols, grow, tmat, du, dw, dk_scan, dcol_scan, drow_scan, bl, s, n_heads):
    rows = _group_rows(s)
    nblk = s // rows
    bh_n = bl * n_heads
    hp = PREP_HEADS_PER_STEP if n_heads % PREP_HEADS_PER_STEP == 0 else 1
    qkv_spec, ph, rv = _gdn_specs(n_heads, nblk, rows, hp)

    def body(k_ref, v_ref, cols_ref, grow_ref, t_ref, du_ref, dw_ref, dks_ref, dcs_ref, drs_ref,
             dk_ref, dv_ref, dcols_ref, drow_ref):
        r, c, causal = _group_masks(rows)
        hh = range(hp)
        ks = [k_ref[:, h * LANES:(h + 1) * LANES] for h in hh]
        vs = [v_ref[:, h * LANES:(h + 1) * LANES] for h in hh]
        gcs = [cols_ref[h, :, 0:1] for h in hh]
        betas = [cols_ref[h, :, 1:2] for h in hh]
        tms = [t_ref[h] for h in hh]
        dus = [du_ref[h] for h in hh]
        dws = [dw_ref[h] for h in hh]
        gams = [jnp.exp(gc) for gc in gcs]
        kbs = [k * b for k, b in zip(ks, betas)]
        kbgs = [kb * g for kb, g in zip(kbs, gams)]
        dts = [_dot_nt(dus[h], vs[h] * betas[h]) + _dot_nt(dws[h], kbgs[h]) for h in hh]
        dvbs = [_dot_tn(tms[h], dus[h]) for h in hh]
        dkbgs = [_dot_tn(tms[h], dws[h]) for h in hh]
        kks = [_dot_nt(kbs[h], ks[h]) for h in hh]
        inner = [_dot_nt(dts[h], tms[h]) for h in hh]
        dads = [jnp.where(r > c, -_dot_tn(tms[h], inner[h]), 0.0) * _decay(gcs[h], grow_ref[h, 0], causal) for h in hh]
        dkbs = [dkbgs[h] * gams[h] + _dot(dads[h], ks[h]) for h in hh]
        dk2 = [_dot_tn(dads[h], kbs[h]) for h in hh]
        for h in hh:
            dk_ref[h] = dks_ref[h] + dk2[h] + dkbs[h] * betas[h]
            dv_ref[h] = dvbs[h] * betas[h]
            ea = dads[h] * kks[h]
            dcols_ref[h, :, 0:1] = (dcs_ref[h] + jnp.sum(dkbgs[h] * kbgs[h], axis=-1, keepdims=True)
                                    + jnp.sum(ea, axis=-1, keepdims=True))
            dcols_ref[h, :, 1:2] = (jnp.sum(dvbs[h] * vs[h], axis=-1, keepdims=True)
                                    + jnp.sum(dkbs[h] * ks[h], axis=-1, keepdims=True))
            drow_ref[h, 0] = drs_ref[h, 0] - jnp.sum(ea, axis=0, keepdims=True)

    return _pc(body, name="gdn_prep_bwd", grid=(bh_n // hp, nblk),
               in_specs=[qkv_spec(1), qkv_spec(2), ph(2), rv(), ph(rows), ph(LANES), ph(LANES), ph(LANES), ph(1), rv()],
               out_specs=[ph(LANES), ph(LANES), ph(2), rv()],
               out_shape=[jax.ShapeDtypeStruct((bh_n, s, LANES), f32), jax.ShapeDtypeStruct((bh_n, s, LANES), f32),
                          jax.ShapeDtypeStruct((bh_n, s, 2), f32), jax.ShapeDtypeStruct((bh_n, nblk, 1, rows), f32)],
               sem=("parallel", "parallel"))(qkv, qkv, cols, grow, tmat, du, dw, dk_scan, dcol_scan, drow_scan)


def _row(v):
    return v.reshape(1, -1).astype(f32)


def _pad_lanes(v):
    v = v.reshape(1, -1).astype(f32)
    return jnp.pad(v, ((0, 0), (0, LANES - v.shape[1])))


def _local_step(x, tgt, p, need, emit):
    bl, s, d = x.shape
    t = bl * s
    n_heads = p["gdn_a_log"].shape[-1]
    assert d == n_heads * LANES and s % CHUNK == 0
    x2 = x.reshape(t, d)
    tgt2 = tgt.reshape(t, d)
    gr = {}

    n0, ub, gl = _pw1_glu(x2, _row(p["norm_mix_g"][0]), p["cv_w_pw1"], _row(p["cv_b_pw1"]))
    dc = _dwconv_fwd(gl.reshape(bl, s, d), p["cv_w_dw"], _row(p["cv_b_dw"]), "dwconv_fwd").reshape(t, d)
    sb, h1 = _ln_silu_mm_res(dc, _row(p["cv_ln_g"]), _row(p["cv_ln_b"]), p["cv_w_pw2"], _row(p["cv_b_pw2"]), x2)
    m0 = need("mlp0", h1)
    n1, f0, r0, h2 = _mlp_fwd(h1, _row(p["norm_ffn_g"][0]), m0["w1"], m0["w2"], "mlp_fwd0")

    gd = need("gdn", h2)
    w_in = gd["w_in"]
    w_ab = jnp.pad(w_in[:, 4 * d:], ((0, 0), (0, LANES - 2 * n_heads)))
    a_log_pad = _pad_lanes(p["gdn_a_log"])
    dt_pad = _pad_lanes(p["gdn_dt_bias"])
    n2, qkv_pre, z, ab, gbeta = _gdn_in(h2, _row(p["norm_mix_g"][1]), w_in, w_ab, a_log_pad, dt_pad, n_heads)
    zero_bias = jnp.zeros((1, 3 * d), f32)
    qkv = _dwconv_fwd(qkv_pre.reshape(bl, s, 3 * d), p["gdn_conv_w"], zero_bias, "sconv_fwd", qk_heads=n_heads).reshape(t, 3 * d)
    bh_n, rows = bl * n_heads, _group_rows(s)
    gates = gbeta[:, :2 * n_heads].reshape(bl, s, 2, n_heads).transpose(2, 0, 3, 1).reshape(2, bh_n, s)
    g_lanes = gates[0].reshape(bh_n * (s // rows), rows)
    gc_lanes = _lane_cumsum(g_lanes, jnp.zeros_like(g_lanes), False, "gdn_gate_cumsum")
    grow = gc_lanes.reshape(bh_n, s // rows, 1, rows)
    cols = jnp.stack([gc_lanes.reshape(bh_n, s), gates[1]], axis=-1)
    u, w, tmat = _gdn_prep(qkv, cols, grow, bl, s, n_heads)
    o, vn, ss = _gdn_scan(qkv, u, w, cols, grow, bl, s, n_heads)
    onb, h3 = _gated_norm_mm_res(o, z, _row(p["gdn_norm_g"]), gd["w_out"], h2, n_heads)
    m1 = need("mlp1", h3)
    n3, f1, r1, loss_acc, dh4, dgf = _mlp_fwd(h3, _row(p["norm_ffn_g"][1]), m1["w1"], m1["w2"], "mlp_fwd1_loss",
                                              loss=(_row(p["final_norm_g"]), tgt2))
    loss = loss_acc[0, 0]
    gr["final_norm_g"] = dgf[0]

    df1, dh3, dg_ffn1, _ = _mlp_bwd(dh4, h3, _row(p["norm_ffn_g"][1]), f1, m1["w1"], m1["w2"], "mlp_bwd1")
    dw2_1 = _mm_tn(r1, dh4, "dw_mlp2_1", bf16)
    dw1_1 = _mm_tn_blocked(n3, df1, "dw_mlp1_1", bf16)

    dw_out = _mm_tn(onb, dh3, "dw_gdn_out", bf16)
    tie = emit("late", {"mlp_w2_1": dw2_1, "mlp_w1_1": dw1_1, "gdn_w_out": dw_out})
    do, dz, dng = _mm_nt([(dh3, gd["w_out"])], "dx_gdn_out",
                         _gated_norm_bwd_tail(o, z, _row(p["gdn_norm_g"]) + tie, n_heads))
    gr["gdn_norm_g"] = dng[0]
    du, dw_, dq, dk_scan, dcol_scan, drow_scan = _gdn_scan_bwd(do, qkv, w, vn, cols, grow, ss, bl, s, n_heads)
    dk, dv, dcols, drow = _gdn_prep_bwd(qkv, cols, grow, tmat, du, dw_, dk_scan, dcol_scan, drow_scan, bl, s, n_heads)
    dg_lanes = _lane_cumsum(dcols[..., 0].reshape(g_lanes.shape), drow.reshape(g_lanes.shape), True, "gdn_gate_cumsum_bwd")
    dgb2 = jnp.stack([dg_lanes.reshape(bh_n, s), dcols[..., 1]], axis=-1)
    dqkv_pre, dconv_w, _ = _dwconv_bwd(qkv_pre.reshape(bl, s, 3 * d), [dq, dk, dv], p["gdn_conv_w"], "sconv_bwd", qk_heads=n_heads)
    gr["gdn_conv_w"] = dconv_w[:p["gdn_conv_w"].shape[0]]
    dgb = dgb2.reshape(bl, n_heads, s, 2).transpose(0, 2, 3, 1).reshape(t, 2 * n_heads)
    dgb = jnp.pad(dgb, ((0, 0), (0, LANES - 2 * n_heads)))
    dab, dal, ddt = _gates_bwd(dgb, ab, a_log_pad, dt_pad, n_heads)
    gr["gdn_a_log"] = dal[0, :n_heads]
    gr["gdn_dt_bias"] = ddt[0, :n_heads]
    dqkv2 = dqkv_pre.reshape(t, 3 * d)
    dw_in = jnp.concatenate(
        [_mm_tn(n2, dqkv2, "dw_gdn_in_qkv"), _mm_tn(n2, dz, "dw_gdn_in_z"), _mm_tn(n2, dab, "dw_gdn_in_ab")[:, :2 * n_heads]], axis=1)
    tie = emit("gdn_in", {"gdn_w_in": dw_in})
    dh2, dg_mix1 = _mm_nt([(dqkv2, (w_in, 0)), (dz, (w_in, 3)), (dab, w_ab)], "dx_gdn_in",
                          _rms_bwd_tail(h2, _row(p["norm_mix_g"][1]) + tie, dh3))

    df0, dh1, dg_ffn0, cs_h1 = _mlp_bwd(dh2, h1, _row(p["norm_ffn_g"][0]), f0, m0["w1"], m0["w2"], "mlp_bwd0")
    dw2_0 = _mm_tn(r0, dh2, "dw_mlp2_0", bf16)
    dw1_0 = _mm_tn_blocked(n1, df0, "dw_mlp1_0", bf16)
    dw_pw2 = _mm_tn(sb, dh1, "dw_pw2", bf16)
    tie = emit("mlp0", {"mlp_w2_0": dw2_0, "mlp_w1_0": dw1_0, "cv_w_pw2": dw_pw2})
    gr["norm_ffn_g"] = jnp.stack([dg_ffn0[0], dg_ffn1[0]])

    gr["cv_b_pw2"] = cs_h1[0]
    ddc, dlng, dlnb, cs_dc = _mm_nt([(dh1, p["cv_w_pw2"])], "dx_pw2",
                                    _ln_silu_bwd_tail(dc, _row(p["cv_ln_g"]) + tie, _row(p["cv_ln_b"])))
    gr["cv_ln_g"] = dlng[0]
    gr["cv_ln_b"] = dlnb[0]
    gr["cv_b_dw"] = cs_dc[0]
    dgl, dw_dw, _ = _dwconv_bwd(gl.reshape(bl, s, d), [ddc.reshape(bl, s, d)], p["cv_w_dw"], "dwconv_bwd")
    gr["cv_w_dw"] = dw_dw[:p["cv_w_dw"].shape[0]]
    dub, cs_u = _glu_bwd(dgl.reshape(t, d), ub)
    gr["cv_b_pw1"] = cs_u[0]
    dw_pw1 = _mm_tn_blocked(n0, dub, "dw_pw1", bf16)
    tie = emit("last", {"cv_w_pw1": dw_pw1, "small": gr})
    dx, dg_mix0 = _mm_nt([(dub, ("blocks", p["cv_w_pw1"]))], "dx_pw1",
                         _rms_bwd_tail(x2, _row(p["norm_mix_g"][0]) + tie, dh1))
    return loss, dx.reshape(bl, s, d), {"norm_mix_g": jnp.stack([dg_mix0[0], dg_mix1[0]])}


ANY = pl.BlockSpec(memory_space=pl.ANY)
MESH = pl.DeviceIdType.MESH


def _flip(v, bit):
    return 1 - v if bit else v


def _all_gather_many(shards):
    na = len(shards)

    def body(*refs):
        x_refs, o_refs = refs[:na], refs[na:2 * na]
        send_sems, recv_sems, local_sems = refs[2 * na:]
        x, y, c = lax.axis_index("x"), lax.axis_index("y"), lax.axis_index("c")
        me, sibling = (x, y, c), (x, y, 1 - c)
        chips = [(1 - x, y), (x, 1 - y), (1 - x, 1 - y)]

        def copy(a, k, block, to, src=None):
            px, py, pc = block
            dst = o_refs[a].at[4 * px + 2 * py + pc]
            return pltpu.make_async_remote_copy(
                src_ref=dst if src is None else src, dst_ref=dst,
                send_sem=send_sems.at[a, k], recv_sem=recv_sems.at[a, k], device_id=to, device_id_type=MESH)

        mine = [pltpu.make_async_copy(x_refs[a], o_refs[a].at[4 * x + 2 * y + c], local_sems.at[a]) for a in range(na)]
        first = []
        for a in range(na):
            first.append(copy(a, 0, me, sibling, src=x_refs[a]))
            first += [copy(a, 1 + j, me, (*chip, c), src=x_refs[a]) for j, chip in enumerate(chips)]
        for cp in mine + first:
            cp.start()
        passed = []
        for j, chip in enumerate(chips):
            for a in range(na):
                copy(a, 1 + j, (*chip, c), me).wait_recv()
                fwd = copy(a, 4 + j, (*chip, c), sibling)
                fwd.start()
                passed.append(fwd)
        for a in range(na):
            copy(a, 0, sibling, me).wait_recv()
        for j, chip in enumerate(chips):
            for a in range(na):
                copy(a, 4 + j, (*chip, 1 - c), me).wait_recv()
        for cp in first + passed:
            cp.wait_send()
        for cp in mine:
            cp.wait()

    return pl.pallas_call(
        body, name="weights_all_gather",
        out_shape=[jax.ShapeDtypeStruct((N_DEV,) + a.shape, a.dtype) for a in shards],
        in_specs=[ANY] * na, out_specs=[ANY] * na,
        scratch_shapes=[pltpu.SemaphoreType.DMA((na, 7)), pltpu.SemaphoreType.DMA((na, 7)), pltpu.SemaphoreType.DMA((na,))],
        compiler_params=pltpu.CompilerParams(has_side_effects=True),
    )(*shards)


HBM = pl.BlockSpec(memory_space=pltpu.HBM)
SEM = pl.BlockSpec(memory_space=pltpu.SEMAPHORE)
EFFECT = pltpu.SideEffectType.DATAFLOW_SIDE_EFFECTING
N_PEERS = N_DEV - 1


def _exchange_copies(src_refs, land_refs, send_sems, recv_sems, scatter):
    x, y, c = lax.axis_index("x"), lax.axis_index("y"), lax.axis_index("c")
    me = 4 * x + 2 * y + c
    copies = []
    for a, (src, land) in enumerate(zip(src_refs, land_refs)):
        for k in range(1, N_DEV):
            px, py, pc = _flip(x, k & 4), _flip(y, k & 2), _flip(c, k & 1)
            i = a * N_PEERS + k - 1
            copies.append(pltpu.make_async_remote_copy(
                src_ref=src.at[4 * px + 2 * py + pc] if scatter[a] else src, dst_ref=land.at[me],
                send_sem=send_sems.at[i], recv_sem=recv_sems.at[i], device_id=(px, py, pc), device_id_type=MESH))
    return copies


def _exchange_start(srcs, scatter, name):
    na = len(srcs)
    lands = [lax.empty(s.shape if sc else (N_DEV,) + s.shape, s.dtype) for s, sc in zip(srcs, scatter)]

    def body(*refs):
        copies = _exchange_copies(refs[:na], refs[na:2 * na], refs[2 * na], refs[2 * na + 1], scatter)
        for cp in copies:
            cp.start()
        token = refs[-1]
        token[...] = jnp.zeros_like(token)

    outs = pl.pallas_call(
        body, name=name,
        out_shape=(pltpu.SemaphoreType.DMA((na * N_PEERS,)), pltpu.SemaphoreType.DMA((na * N_PEERS,)))
        + tuple(pltpu.HBM(a.shape, a.dtype) for a in srcs + lands) + (jax.ShapeDtypeStruct((SUBLANES, LANES), f32),),
        in_specs=[HBM] * (2 * na),
        out_specs=(SEM, SEM) + (HBM,) * (2 * na) + (pl.BlockSpec(memory_space=pltpu.VMEM),),
        input_output_aliases={i: 2 + i for i in range(2 * na)},
        compiler_params=pltpu.CompilerParams(has_side_effects=EFFECT),
    )(*[pltpu.with_memory_space_constraint(a, pltpu.HBM) for a in srcs + lands])
    return outs[0], outs[1], list(outs[2:2 + na]), list(outs[2 + na:2 + 2 * na]), outs[-1]


def _exchange_wait(started, after, scatter, name):
    send_sems, recv_sems, srcs, lands, _ = started
    na = len(srcs)

    def body(*refs):
        for cp in _exchange_copies(refs[:na], refs[na:2 * na], refs[2 * na], refs[2 * na + 1], scatter):
            cp.wait_send()
            cp.wait_recv()

    outs = pl.pallas_call(
        body, name=name,
        out_shape=tuple(pltpu.HBM(a.shape, a.dtype) for a in srcs + lands),
        in_specs=[HBM] * (2 * na) + [SEM, SEM, ANY], out_specs=(HBM,) * (2 * na),
        input_output_aliases={i: i for i in range(2 * na)},
        compiler_params=pltpu.CompilerParams(has_side_effects=EFFECT),
    )(*srcs, *lands, send_sems, recv_sems, after)
    return list(outs[:na]), list(outs[na:])


def _own_block(land, block, me):
    return lax.dynamic_update_index_in_dim(land, block, me, 0)


def _sum8_adamw(r2, w, m, v, layer, name):
    _, rr, cc = r2.shape
    tr = _tile(rr, 256)
    bc1 = 1.0 - ADAM_B1 ** ADAM_STEP
    bc2 = 1.0 - ADAM_B2 ** ADAM_STEP

    def body(r_ref, w_ref, m_ref, v_ref, g_ref, d_ref, nm_ref, nv_ref):
        gv = r_ref[0].astype(f32)
        for q in range(1, N_DEV):
            gv = gv + r_ref[q].astype(f32)
        g_ref[...] = gv
        nm = ADAM_B1 * m_ref[...] + (1.0 - ADAM_B1) * gv
        nv = ADAM_B2 * v_ref[...] + (1.0 - ADAM_B2) * (gv * gv)
        nm_ref[...] = nm
        nv_ref[...] = nv
        d_ref[...] = -ADAM_LR * ((nm / bc1) / (jnp.sqrt(nv / bc2) + ADAM_EPS) + ADAM_WD * w_ref[...])

    lspec = pl.BlockSpec((None, tr, cc), lambda i: (layer, i, 0))
    return _pc(body, name=name, grid=(rr // tr,),
               in_specs=[pl.BlockSpec((N_DEV, tr, cc), lambda i: (0, i, 0)), lspec, lspec, lspec],
               out_specs=[_rows(tr, cc)] * 4, out_shape=[jax.ShapeDtypeStruct((rr, cc), f32)] * 4,
               sem=("parallel",))(r2, w, m, v)


def _sum_devices(recv, name):
    _, rr, _ = recv.shape
    tr = _tile(rr, 512)

    def body(r_ref, o_ref):
        acc = r_ref[0]
        for i in range(1, N_DEV):
            acc = acc + r_ref[i]
        o_ref[...] = acc

    return _pc(body, name=name, grid=(rr // tr,),
               in_specs=[pl.BlockSpec((N_DEV, tr, LANES), lambda i: (0, i, 0))],
               out_specs=_rows(tr, LANES), out_shape=jax.ShapeDtypeStruct((rr, LANES), f32), sem=("parallel",))(recv)


def _adamw(w, g, m, v):
    rr = w.shape[0]
    tr = _tile(rr, 512)
    bc1 = 1.0 - ADAM_B1 ** ADAM_STEP
    bc2 = 1.0 - ADAM_B2 ** ADAM_STEP

    def body(w_ref, g_ref, m_ref, v_ref, d_ref, nm_ref, nv_ref):
        gv = g_ref[...]
        nm = ADAM_B1 * m_ref[...] + (1.0 - ADAM_B1) * gv
        nv = ADAM_B2 * v_ref[...] + (1.0 - ADAM_B2) * (gv * gv)
        nm_ref[...] = nm
        nv_ref[...] = nv
        d_ref[...] = -ADAM_LR * ((nm / bc1) / (jnp.sqrt(nv / bc2) + ADAM_EPS) + ADAM_WD * w_ref[...])

    spec = _rows(tr, LANES)
    return _pc(body, name="adamw", grid=(rr // tr,), in_specs=[spec] * 4, out_specs=[spec] * 3,
               out_shape=[jax.ShapeDtypeStruct((rr, LANES), f32)] * 3, sem=("parallel",))(w, g, m, v)


PACK_ROWS = 512
PART_ROWS = SUBLANES


def _pack(arrs):
    parts, sizes = [], []
    for a in arrs:
        flat = a.reshape(-1)
        n = flat.shape[0]
        rows = -(-n // (LANES * PART_ROWS)) * PART_ROWS
        if rows * LANES != n:
            flat = jnp.pad(flat, (0, rows * LANES - n))
        parts.append(flat.reshape(rows, LANES))
        sizes.append((rows, n))
    total = sum(r for r, _ in sizes)
    padded = -(-total // PACK_ROWS) * PACK_ROWS
    if padded > total:
        parts.append(jnp.zeros((padded - total, LANES), parts[0].dtype))
    return jnp.concatenate(parts, axis=0), sizes


def _unpack(packed, sizes, shapes):
    out, off = [], 0
    for (rows, n), shp in zip(sizes, shapes):
        piece = lax.slice_in_dim(packed, off, off + rows, axis=0)
        if rows * LANES != n:
            piece = lax.slice_in_dim(piece.reshape(-1), 0, n, axis=0)
        out.append(piece.reshape(tuple(shp)))
        off += rows
    return out


def _cols_to_blocks(a):
    n = a.shape[-1] // N_DEV
    a = a.reshape(a.shape[:-1] + (N_DEV, n))
    return jnp.moveaxis(a, -2, 0)


def _blocks_to_cols(a):
    a = jnp.moveaxis(a, 0, -2)
    return a.reshape(a.shape[:-2] + (a.shape[-2] * a.shape[-1],))


def _rows_to_blocks(a):
    k = a.shape[-2] // N_DEV
    a = a.reshape(a.shape[:-2] + (N_DEV, k, a.shape[-1]))
    return jnp.moveaxis(a, -3, 0)


def _blocks_to_rows(a):
    a = jnp.moveaxis(a, 0, -3)
    return a.reshape(a.shape[:-3] + (a.shape[-3] * a.shape[-2], a.shape[-1]))


COL_SHARDED = ("cv_w_pw1", "gdn_w_in", "mlp_w1")
ROW_SHARDED = ("cv_w_pw2", "gdn_w_out", "mlp_w2")
CONV_SHARDED = ("cv_w_dw", "gdn_conv_w")
REPLICATED = ("norm_mix_g", "norm_ffn_g", "final_norm_g", "cv_b_pw1", "cv_b_dw", "cv_ln_g", "cv_ln_b", "cv_b_pw2",
              "gdn_a_log", "gdn_dt_bias", "gdn_norm_g")
WEIGHTS = ("norm_mix_g", "norm_ffn_g", "final_norm_g", "cv_w_pw1", "cv_b_pw1", "cv_w_dw", "cv_b_dw", "cv_ln_g",
           "cv_ln_b", "cv_w_pw2", "cv_b_pw2", "gdn_w_in", "gdn_conv_w", "gdn_a_log", "gdn_dt_bias", "gdn_norm_g",
           "gdn_w_out", "mlp_w1", "mlp_w2")
MATMUL_SHARDED = COL_SHARDED + ROW_SHARDED


def _squeeze_layer(name, a):
    if name in ("norm_mix_g", "norm_ffn_g", "final_norm_g", "mlp_w1", "mlp_w2"):
        return a
    return a[0]


def _gather_weights(shards):
    me = 4 * lax.axis_index("x") + 2 * lax.axis_index("y") + lax.axis_index("c")
    first = ("cv_w_pw1", "cv_w_pw2") + CONV_SHARDED
    got = dict(zip(first, _all_gather_many([shards[n] if n in CONV_SHARDED else shards[n].astype(bf16) for n in first])))
    now = {"cv_w_pw1": got["cv_w_pw1"], "cv_w_pw2": _blocks_to_rows(got["cv_w_pw2"])}
    for n in CONV_SHARDED:
        now[n] = _blocks_to_cols(got[n])

    def cast(a, tie):
        return (a + tie).astype(bf16)

    token, _ = lax.optimization_barrier((jnp.zeros((), f32), got["cv_w_pw1"]))
    later, started = {}, {}
    for group in ("mlp0", "gdn", "mlp1"):
        if group == "gdn":
            srcs = [cast(shards["gdn_w_in"], token).reshape(-1, LANES), cast(shards["gdn_w_out"], token)]
        else:
            layer = int(group[-1])
            srcs = [cast(shards["mlp_w1"][layer], token), cast(shards["mlp_w2"][layer], token)]
        later[group] = srcs
        started[group] = _exchange_start(srcs, [False] * len(srcs), f"weights_{group}_start")
        token = started[group][4][0, 0]

    def need(group, after):
        srcs, lands = _exchange_wait(started[group], after, [False] * len(later[group]), f"weights_{group}_wait")
        lands = [_own_block(ld, own, me) for ld, own in zip(lands, srcs)]
        if group == "gdn":
            w_in = _blocks_to_cols(lands[0].reshape((N_DEV,) + shards["gdn_w_in"].shape))
            return {"w_in": w_in, "w_out": _blocks_to_rows(lands[1])}
        return {"w1": lands[0], "w2": lands[1]}

    return now, need, token


def kernel(x, norm_mix_g, norm_ffn_g, final_norm_g, cv_w_pw1, cv_b_pw1, cv_w_dw, cv_b_dw, cv_ln_g, cv_ln_b, cv_w_pw2, cv_b_pw2, gdn_w_in, gdn_conv_w, gdn_a_log, gdn_dt_bias, gdn_norm_g, gdn_w_out, mlp_w1, mlp_w2, loss_target, m_norm_mix_g, m_norm_ffn_g, m_final_norm_g, m_cv_w_pw1, m_cv_b_pw1, m_cv_w_dw, m_cv_b_dw, m_cv_ln_g, m_cv_ln_b, m_cv_w_pw2, m_cv_b_pw2, m_gdn_w_in, m_gdn_conv_w, m_gdn_a_log, m_gdn_dt_bias, m_gdn_norm_g, m_gdn_w_out, m_mlp_w1, m_mlp_w2, v_norm_mix_g, v_norm_ffn_g, v_final_norm_g, v_cv_w_pw1, v_cv_b_pw1, v_cv_w_dw, v_cv_b_dw, v_cv_ln_g, v_cv_ln_b, v_cv_w_pw2, v_cv_b_pw2, v_gdn_w_in, v_gdn_conv_w, v_gdn_a_log, v_gdn_dt_bias, v_gdn_norm_g, v_gdn_w_out, v_mlp_w1, v_mlp_w2):
    w_in = dict(zip(WEIGHTS, (norm_mix_g, norm_ffn_g, final_norm_g, cv_w_pw1, cv_b_pw1, cv_w_dw, cv_b_dw, cv_ln_g, cv_ln_b, cv_w_pw2, cv_b_pw2, gdn_w_in, gdn_conv_w, gdn_a_log, gdn_dt_bias, gdn_norm_g, gdn_w_out, mlp_w1, mlp_w2)))
    m_in = dict(zip(WEIGHTS, (m_norm_mix_g, m_norm_ffn_g, m_final_norm_g, m_cv_w_pw1, m_cv_b_pw1, m_cv_w_dw, m_cv_b_dw, m_cv_ln_g, m_cv_ln_b, m_cv_w_pw2, m_cv_b_pw2, m_gdn_w_in, m_gdn_conv_w, m_gdn_a_log, m_gdn_dt_bias, m_gdn_norm_g, m_gdn_w_out, m_mlp_w1, m_mlp_w2)))
    v_in = dict(zip(WEIGHTS, (v_norm_mix_g, v_norm_ffn_g, v_final_norm_g, v_cv_w_pw1, v_cv_b_pw1, v_cv_w_dw, v_cv_b_dw, v_cv_ln_g, v_cv_ln_b, v_cv_w_pw2, v_cv_b_pw2, v_gdn_w_in, v_gdn_conv_w, v_gdn_a_log, v_gdn_dt_bias, v_gdn_norm_g, v_gdn_w_out, v_mlp_w1, v_mlp_w2)))
    me = 4 * lax.axis_index("x") + 2 * lax.axis_index("y") + lax.axis_index("c")

    shards = {n: _squeeze_layer(n, w_in[n]) for n in WEIGHTS}
    first, need, token = _gather_weights(shards)
    params = {n: shards[n] for n in REPLICATED}
    params.update(first)
    params["norm_mix_g"] = params["norm_mix_g"] + token

    def row_blocks(a):
        return a.reshape(N_DEV, a.shape[0] // N_DEV, a.shape[1])

    def flat_blocks(a):
        k, n8 = a.shape
        return _cols_to_blocks(a).reshape(N_DEV, k * (n8 // N_DEV) // LANES, LANES)

    def as_blocks(n, a):
        if n == "gdn_w_in":
            return flat_blocks(a).astype(bf16)
        return a if a.ndim == 3 else row_blocks(a)

    sent = []

    small = REPLICATED + CONV_SHARDED
    small_early = tuple(n for n in small if n != "norm_mix_g")
    small_info = {}

    def emit(group, grads_out):
        names, blocks, scatter = list(grads_out), [], []
        for n in names:
            if n == "small":
                packed, small_info["sizes"] = _pack([grads_out[n][k] for k in small_early])
                small_info["shapes"] = [grads_out[n][k].shape for k in small_early]
                blocks.append(packed)
                scatter.append(False)
            else:
                blocks.append(as_blocks(n, grads_out[n]))
                scatter.append(True)
        sent.append((names, _exchange_start(blocks, scatter, f"grads_{group}_start"), scatter))
        return sent[-1][1][4][0, 0]

    loss_part, grad_x, gr = _local_step(x, loss_target, params, need, emit)
    loss = lax.psum(loss_part, ("x", "y", "c"))
    nm_shape = gr["norm_mix_g"].shape
    sent.append((["norm_mix_g"], _exchange_start([gr["norm_mix_g"].reshape(-1, LANES)], [False], "grads_norm_mix_start"), [False]))

    recv = {}

    def finish(entry, after):
        names, st, scatter = entry
        srcs, lands = _exchange_wait(st, after, scatter, f"grads_{names[0]}_wait")
        for n, ld, blk, sc in zip(names, lands, srcs, scatter):
            own = lax.dynamic_index_in_dim(blk, me, 0, keepdims=False) if sc else blk
            recv[n] = _own_block(ld, own, me)
        return lands[0]

    def as3d(n, a):
        if n == "gdn_w_in":
            return a.reshape(a.shape[0], -1, LANES)
        return a

    big = [("cv_w_pw2", 0, "cv_w_pw2"), ("gdn_w_in", 0, "gdn_w_in"), ("gdn_w_out", 0, "gdn_w_out"),
           ("mlp_w1", 0, "mlp_w1_0"), ("mlp_w1", 1, "mlp_w1_1"), ("mlp_w2", 0, "mlp_w2_0"), ("mlp_w2", 1, "mlp_w2_1"),
           ("cv_w_pw1", 0, "cv_w_pw1")]
    res = {n: {} for n in MATMUL_SHARDED}
    after = grad_x
    for entry in sent[:-2]:
        after = finish(entry, after)
    for n, layer, key in big[:-1]:
        res[n][layer] = _sum8_adamw(recv[key], as3d(n, w_in[n]), as3d(n, m_in[n]), as3d(n, v_in[n]), layer, f"adamw_{key}")
        after = res[n][layer][0]
    after = finish(sent[-2], after)
    finish(sent[-1], after)
    n, layer, key = big[-1]
    res[n][layer] = _sum8_adamw(recv[key], as3d(n, w_in[n]), as3d(n, m_in[n]), as3d(n, v_in[n]), layer, f"adamw_{key}")

    grads = dict(zip(small_early, _unpack(_sum_devices(recv["small"], "grads_small_sum"), small_info["sizes"], small_info["shapes"])))
    grads["norm_mix_g"] = _sum_devices(recv["norm_mix_g"], "grads_norm_mix_sum").reshape(nm_shape)
    for n in CONV_SHARDED:
        cn = shards[n].shape[-1]
        grads[n] = lax.dynamic_slice_in_dim(grads[n], me * cn, cn, axis=1)

    out_groups = {n: [] for n in WEIGHTS}
    for n in MATMUL_SHARDED:
        layers = sorted(res[n])
        for k in range(4):
            pieces = [res[n][layer][k] for layer in layers]
            out_groups[n].append(jnp.stack(pieces).reshape(w_in[n].shape))

    sm_w = [shards[n] for n in small]
    sm_g = [grads[n].reshape(shards[n].shape) for n in small]
    sm_m = [_squeeze_layer(n, m_in[n]) for n in small]
    sm_v = [_squeeze_layer(n, v_in[n]) for n in small]
    wp, psz = _pack(sm_w)
    gp, _ = _pack(sm_g)
    mp, _ = _pack(sm_m)
    vp, _ = _pack(sm_v)
    dp, nmp, nvp = _adamw(wp, gp, mp, vp)
    shp = [a.shape for a in sm_w]
    for n, g, dl, nm, nv in zip(small, sm_g, _unpack(dp, psz, shp), _unpack(nmp, psz, shp), _unpack(nvp, psz, shp)):
        out_groups[n] = [a.reshape(w_in[n].shape) for a in (g, dl, nm, nv)]

    outs = [loss, grad_x]
    for k in range(4):
        outs += [out_groups[n][k] for n in WEIGHTS]
    return tuple(outs)
```

```python
import functools

import jax
import jax.numpy as jnp
from jax import lax
from jax.experimental import pallas as pl
from jax.experimental.pallas import tpu as pltpu

f32, bf16 = jnp.float32, jnp.bfloat16

NORM_EPS = 1e-6
L2_EPS = 1e-6
CHUNK = 64
LANES = 128
SUBLANES = 8
N_DEV = 8
VMEM_LIMIT = 56 * 1024 * 1024
CONV_PAD = 32
HEADS_PER_STEP = 8
PREP_HEADS_PER_STEP = 4
NEG = -1e30

ADAM_LR, ADAM_B1, ADAM_B2, ADAM_EPS, ADAM_WD, ADAM_STEP = 0.001, 0.9, 0.999, 1e-08, 0.01, 10

NT = (((1,), (1,)), ((), ()))
TN = (((0,), (0,)), ((), ()))
HI = lax.Precision.HIGHEST


def _pc(body, *, name, grid, in_specs, out_specs, out_shape, scratch=(), sem=None):
    return pl.pallas_call(
        body, name=name, grid=grid, in_specs=in_specs, out_specs=out_specs, out_shape=out_shape,
        scratch_shapes=list(scratch),
        compiler_params=pltpu.CompilerParams(dimension_semantics=sem, vmem_limit_bytes=VMEM_LIMIT))


def _rows(tm, n):
    return pl.BlockSpec((tm, n), lambda i: (i, 0))


def _const(shape):
    return pl.BlockSpec(shape, lambda *_: (0,) * len(shape))


def _resident(shape):
    return pl.BlockSpec(shape, lambda *_: (0,) * len(shape), pipeline_mode=pl.Buffered(1))


def _tile(t, pref):
    return pref if t % pref == 0 else t


def _dot(a, b):
    return jnp.dot(a.astype(bf16), b.astype(bf16), preferred_element_type=f32)


def _dot_nt(a, b):
    return lax.dot_general(a.astype(bf16), b.astype(bf16), NT, preferred_element_type=f32)


def _dot_tn(a, b):
    return lax.dot_general(a.astype(bf16), b.astype(bf16), TN, preferred_element_type=f32)


def _sigmoid(x):
    return 1.0 / (1.0 + jnp.exp(-x))


def _silu_grad(x):
    s = _sigmoid(x)
    return s * (1.0 + x * (1.0 - s))


def _rms(x, g):
    rstd = lax.rsqrt(jnp.mean(x * x, axis=-1, keepdims=True) + NORM_EPS)
    xh = x * rstd
    return xh * g, xh, rstd


def _rms_bwd(dn, xh, rstd, g):
    dxh = dn * g
    return rstd * (dxh - xh * jnp.mean(dxh * xh, axis=-1, keepdims=True))


def _acc_init(step, *refs):
    @pl.when(step == 0)
    def _():
        for r in refs:
            r[...] = jnp.zeros(r.shape, r.dtype)


def _acc_rows(ref, val):
    ref[0:1, :] += jnp.sum(val, axis=0, keepdims=True)


def _pw1_glu(x, g, w, b):
    t, d = x.shape
    tm = _tile(t, 256)

    nb_w = w.shape[0]

    def body(x_ref, g_ref, w_hbm, b_ref, n_ref, u_ref, gl_ref, w_ref, sems):
        _fetch_blocks(pl.program_id(0), w_hbm, w_ref, sems, True)
        n, _, _ = _rms(x_ref[...], g_ref[...])
        nb = n.astype(bf16)
        n_ref[...] = nb
        u = jnp.dot(nb, w_ref[...], preferred_element_type=f32) + b_ref[...]
        u_ref[...] = u.astype(bf16)
        gl_ref[...] = u[:, :d] * _sigmoid(u[:, d:])

    return _pc(body, name="pw1_glu", grid=(t // tm,),
               in_specs=[_rows(tm, d), _const((1, d)), ANY, _const((1, 2 * d))],
               out_specs=[_rows(tm, d), _rows(tm, 2 * d), _rows(tm, d)],
               out_shape=[jax.ShapeDtypeStruct((t, d), bf16), jax.ShapeDtypeStruct((t, 2 * d), bf16),
                          jax.ShapeDtypeStruct((t, d), f32)],
               scratch=[pltpu.VMEM((d, 2 * d), bf16), pltpu.SemaphoreType.DMA((nb_w,))],
               sem=("arbitrary",))(x, g, w, b)


def _ln_silu_mm_res(dc, ln_g, ln_b, w, b, res):
    t, d = dc.shape
    tm = _tile(t, 256)

    def body(x_ref, g_ref, bb_ref, w_ref, b_ref, r_ref, s_ref, o_ref):
        x = x_ref[...]
        xc = x - jnp.mean(x, axis=-1, keepdims=True)
        rstd = lax.rsqrt(jnp.mean(xc * xc, axis=-1, keepdims=True) + NORM_EPS)
        ln = xc * rstd * g_ref[...] + bb_ref[...]
        sb = (ln * _sigmoid(ln)).astype(bf16)
        s_ref[...] = sb
        o_ref[...] = r_ref[...] + jnp.dot(sb, w_ref[...], preferred_element_type=f32) + b_ref[...]

    return _pc(body, name="ln_silu_pw2", grid=(t // tm,),
               in_specs=[_rows(tm, d), _const((1, d)), _const((1, d)), _resident((d, d)), _const((1, d)), _rows(tm, d)],
               out_specs=[_rows(tm, d), _rows(tm, d)],
               out_shape=[jax.ShapeDtypeStruct((t, d), bf16), jax.ShapeDtypeStruct((t, d), f32)],
               sem=("parallel",))(dc, ln_g, ln_b, w, b, res)


def _fetch_blocks(step, w_hbm, dst, sems, by_cols, layer=None):
    nb_w = w_hbm.shape[0]
    step_rows, step_cols = w_hbm.shape[-2], w_hbm.shape[-1]

    @pl.when(step == 0)
    def _():
        copies = []
        for j in range(nb_w):
            src = w_hbm.at[j] if layer is None else w_hbm.at[j, layer]
            if by_cols:
                part = dst.at[:, pl.ds(j * step_cols, step_cols)]
            else:
                part = dst.at[pl.ds(j * step_rows, step_rows), :]
            copies.append(pltpu.make_async_copy(src, part, sems.at[j]))
        for cp in copies:
            cp.start()
        for cp in copies:
            cp.wait()


def _mlp_fwd(h, g, w1g, w2g, name, loss=None):
    t, d = h.shape
    nb_w, _, bn = w1g.shape
    ff = nb_w * bn
    tm = _tile(t, 256)
    n_in = 4 if loss is None else 6
    n_out = 4 if loss is None else 6

    def body(*refs):
        h_ref, g_ref, w1_hbm, w2_hbm = refs[:4]
        n_ref, f_ref, r_ref = refs[n_in:n_in + 3]
        w1_ref, w2_ref, sem1, sem2 = refs[n_in + n_out:]
        _fetch_blocks(pl.program_id(0), w1_hbm, w1_ref, sem1, True)
        _fetch_blocks(pl.program_id(0), w2_hbm, w2_ref, sem2, False)
        hv = h_ref[...]
        n, _, _ = _rms(hv, g_ref[...])
        nb = n.astype(bf16)
        n_ref[...] = nb
        f = jnp.dot(nb, w1_ref[...], preferred_element_type=f32)
        f_ref[...] = f.astype(bf16)
        rb = jnp.square(jnp.maximum(f, 0.0)).astype(bf16)
        r_ref[...] = rb
        out = hv + jnp.dot(rb, w2_ref[...], preferred_element_type=f32)
        if loss is None:
            refs[n_in + 3][...] = out
        else:
            gf_ref, t_ref = refs[4:6]
            loss_ref, dh_ref, dg_ref = refs[n_in + 3:n_in + 6]
            _acc_init(pl.program_id(0), loss_ref, dg_ref)
            gv = gf_ref[...]
            y, xh, rstd = _rms(out, gv)
            e = y - t_ref[...]
            loss_ref[...] += 0.5 * jnp.sum(jnp.mean(e * e, axis=-1, keepdims=True))
            dy = e * (1.0 / d)
            _acc_rows(dg_ref, dy * xh)
            dh_ref[...] = _rms_bwd(dy, xh, rstd, gv)

    in_specs = [_rows(tm, d), _const((1, d)), ANY, ANY]
    out_specs = [_rows(tm, d), _rows(tm, ff), _rows(tm, ff)]
    out_shape = [jax.ShapeDtypeStruct((t, d), bf16), jax.ShapeDtypeStruct((t, ff), bf16), jax.ShapeDtypeStruct((t, ff), bf16)]
    args = [h, g, w1g, w2g]
    if loss is None:
        out_specs.append(_rows(tm, d))
        out_shape.append(jax.ShapeDtypeStruct((t, d), f32))
    else:
        in_specs += [_const((1, d)), _rows(tm, d)]
        args += list(loss)
        out_specs += [_const((SUBLANES, LANES)), _rows(tm, d), _const((SUBLANES, d))]
        out_shape += [jax.ShapeDtypeStruct((SUBLANES, LANES), f32), jax.ShapeDtypeStruct((t, d), f32),
                      jax.ShapeDtypeStruct((SUBLANES, d), f32)]
    return _pc(body, name=name, grid=(t // tm,), in_specs=in_specs, out_specs=out_specs, out_shape=out_shape,
               scratch=[pltpu.VMEM((d, ff), bf16), pltpu.VMEM((ff, d), bf16),
                        pltpu.SemaphoreType.DMA((nb_w,)), pltpu.SemaphoreType.DMA((nb_w,))],
               sem=("arbitrary",))(*args)


def _softplus(x):
    return jnp.maximum(x, 0.0) + jnp.log(1.0 + jnp.exp(-jnp.abs(x)))


def _gdn_in(h, g, w_main, w_ab, a_log_pad, dt_pad, n_heads):
    t, d = h.shape
    tm = _tile(t, 256)

    def body(h_ref, g_ref, wm_ref, wab_ref, al_ref, dt_ref, n_ref, qkv_ref, z_ref, ab_ref, gb_ref):
        n, _, _ = _rms(h_ref[...], g_ref[...])
        nb = n.astype(bf16)
        n_ref[...] = nb
        p = jnp.dot(nb, wm_ref[...], preferred_element_type=f32)
        qkv_ref[...] = p[:, :3 * d]
        z_ref[...] = p[:, 3 * d:]
        ab = jnp.dot(nb, wab_ref[...], preferred_element_type=f32)
        ab_ref[...] = ab
        lane = lax.broadcasted_iota(jnp.int32, ab.shape, 1)
        decay = -jnp.exp(al_ref[...]) * _softplus(ab + dt_ref[...])
        gb_ref[...] = jnp.where(lane < n_heads, decay, jnp.where(lane < 2 * n_heads, _sigmoid(ab), 0.0))

    return _pc(body, name="gdn_in", grid=(t // tm,),
               in_specs=[_rows(tm, d), _const((1, d)), _resident((d, 4 * d)), _resident((d, LANES)),
                         _const((1, LANES)), _const((1, LANES))],
               out_specs=[_rows(tm, d), _rows(tm, 3 * d), _rows(tm, d), _rows(tm, LANES), _rows(tm, LANES)],
               out_shape=[jax.ShapeDtypeStruct((t, d), bf16), jax.ShapeDtypeStruct((t, 3 * d), f32),
                          jax.ShapeDtypeStruct((t, d), f32), jax.ShapeDtypeStruct((t, LANES), f32),
                          jax.ShapeDtypeStruct((t, LANES), f32)],
               sem=("parallel",))(h, g, w_main, w_ab, a_log_pad, dt_pad)


def _gated_norm_mm_res(o, z, ng, w, res, n_heads):
    t, d = o.shape
    tm = _tile(t, 256)

    def body(o_ref, z_ref, ng_ref, w_ref, r_ref, on_ref, out_ref):
        for hd in range(n_heads):
            sl = slice(hd * LANES, (hd + 1) * LANES)
            rn, _, _ = _rms(o_ref[:, sl], ng_ref[...])
            zz = z_ref[:, sl]
            on_ref[:, sl] = (rn * (zz * _sigmoid(zz))).astype(bf16)
        out_ref[...] = r_ref[...] + jnp.dot(on_ref[...], w_ref[...], preferred_element_type=f32)

    return _pc(body, name="gated_norm_wout", grid=(t // tm,),
               in_specs=[_rows(tm, d), _rows(tm, d), _const((1, LANES)), _resident((d, d)), _rows(tm, d)],
               out_specs=[_rows(tm, d), _rows(tm, d)],
               out_shape=[jax.ShapeDtypeStruct((t, d), bf16), jax.ShapeDtypeStruct((t, d), f32)],
               sem=("parallel",))(o, z, ng, w, res)


def _mlp_bwd(dho, h, g, fb, w1g, w2g, name):
    t, d = h.shape
    nb_w, _, bn = w1g.shape
    ff = nb_w * bn
    tm = _tile(t, 256)

    def body(do_ref, h_ref, g_ref, f_ref, w1_hbm, w2_hbm, df_ref, dh_ref, dg_ref, cs_ref, w1_ref, w2_ref, sem1, sem2):
        _fetch_blocks(pl.program_id(0), w1_hbm, w1_ref, sem1, True)
        _fetch_blocks(pl.program_id(0), w2_hbm, w2_ref, sem2, False)
        _acc_init(pl.program_id(0), dg_ref, cs_ref)
        do = do_ref[...]
        dr = lax.dot_general(do.astype(bf16), w2_ref[...], NT, preferred_element_type=f32)
        dfb = (dr * (2.0 * jnp.maximum(f_ref[...].astype(f32), 0.0))).astype(bf16)
        df_ref[...] = dfb
        dn = lax.dot_general(dfb, w1_ref[...], NT, preferred_element_type=f32)
        gv = g_ref[...]
        _, xh, rstd = _rms(h_ref[...], gv)
        _acc_rows(dg_ref, dn * xh)
        dh = do + _rms_bwd(dn, xh, rstd, gv)
        dh_ref[...] = dh
        _acc_rows(cs_ref, dh)

    return _pc(body, name=name, grid=(t // tm,),
               in_specs=[_rows(tm, d), _rows(tm, d), _const((1, d)), _rows(tm, ff), ANY, ANY],
               out_specs=[_rows(tm, ff), _rows(tm, d), _const((SUBLANES, d)), _const((SUBLANES, d))],
               out_shape=[jax.ShapeDtypeStruct((t, ff), bf16), jax.ShapeDtypeStruct((t, d), f32),
                          jax.ShapeDtypeStruct((SUBLANES, d), f32), jax.ShapeDtypeStruct((SUBLANES, d), f32)],
               scratch=[pltpu.VMEM((d, ff), bf16), pltpu.VMEM((ff, d), bf16),
                        pltpu.SemaphoreType.DMA((nb_w,)), pltpu.SemaphoreType.DMA((nb_w,))],
               sem=("arbitrary",))(dho, h, g, fb, w1g, w2g)


class _Tail:
    def __init__(self, fn, ins, outs):
        self.fn, self.ins, self.outs = fn, ins, outs


def _mm_nt(pairs, name, tail=None):
    t = pairs[0][0].shape[0]
    tm = _tile(t, 256)
    npair = len(pairs)
    in_specs, args, scratch, blocked = [], [], [], []
    k = None
    for dy, w in pairs:
        nn = dy.shape[1]
        if isinstance(w, tuple) and isinstance(w[0], str):
            w = w[1]
            k = w.shape[1]
            wspec = ANY
            blocked.append(True)
            scratch += [pltpu.VMEM((k, nn), bf16), pltpu.SemaphoreType.DMA((w.shape[0],))]
        elif isinstance(w, tuple):
            w, idx = w
            k = w.shape[0]
            wspec = pl.BlockSpec((k, nn), lambda *_, idx=idx: (0, idx), pipeline_mode=pl.Buffered(1))
            blocked.append(False)
        else:
            k = w.shape[0]
            wspec = _resident(w.shape)
            blocked.append(False)
        in_specs += [_rows(tm, nn), wspec]
        args += [dy, w]
    n_tin = len(tail.ins) if tail else 0
    n_out = len(tail.outs) if tail else 1
    if tail:
        for arr, kind in tail.ins:
            in_specs.append(_rows(tm, arr.shape[1]) if kind == "rows" else _const(arr.shape))
            args.append(arr)
        out_specs = [_rows(tm, c) if kind == "rows" else _const((SUBLANES, c)) for c, kind in tail.outs]
        out_shape = [jax.ShapeDtypeStruct((t, c) if kind == "rows" else (SUBLANES, c), f32) for c, kind in tail.outs]
    else:
        out_specs = _rows(tm, k)
        out_shape = jax.ShapeDtypeStruct((t, k), f32)

    def body(*refs):
        step = pl.program_id(0)
        tin = refs[2 * npair:2 * npair + n_tin]
        outs = refs[2 * npair + n_tin:2 * npair + n_tin + n_out]
        scr = list(refs[2 * npair + n_tin + n_out:])
        acc = None
        for p in range(npair):
            w_ref = refs[2 * p + 1]
            if blocked[p]:
                w_vmem, sems = scr.pop(0), scr.pop(0)
                _fetch_blocks(step, w_ref, w_vmem, sems, True)
                w_ref = w_vmem
            part = lax.dot_general(refs[2 * p][...].astype(bf16), w_ref[...], NT, preferred_element_type=f32)
            acc = part if acc is None else acc + part
        if tail is None:
            outs[0][...] = acc
        else:
            _acc_init(step, *[o for o, (_, kind) in zip(outs, tail.outs) if kind == "acc"])
            tail.fn(acc, tin, outs)

    sequential = tail is not None or any(blocked)
    return _pc(body, name=name, grid=(t // tm,), in_specs=in_specs, out_specs=out_specs, out_shape=out_shape,
               scratch=scratch, sem=("arbitrary",) if sequential else ("parallel",))(*args)


def _rms_bwd_tail(h, g, dres):
    def fn(dn, ins, outs):
        h_ref, g_ref, dr_ref = ins
        dh_ref, dg_ref = outs
        gv = g_ref[...]
        _, xh, rstd = _rms(h_ref[...], gv)
        _acc_rows(dg_ref, dn * xh)
        dh_ref[...] = dr_ref[...] + _rms_bwd(dn, xh, rstd, gv)

    d = h.shape[1]
    return _Tail(fn, [(h, "rows"), (g, "const"), (dres, "rows")], [(d, "rows"), (d, "acc")])


def _ln_silu_bwd_tail(dc, ln_g, ln_b):
    def fn(ds, ins, outs):
        x_ref, g_ref, b_ref = ins
        dx_ref, dg_ref, db_ref, cs_ref = outs
        x = x_ref[...]
        gv = g_ref[...]
        xc = x - jnp.mean(x, axis=-1, keepdims=True)
        rstd = lax.rsqrt(jnp.mean(xc * xc, axis=-1, keepdims=True) + NORM_EPS)
        xh = xc * rstd
        dln = ds * _silu_grad(xh * gv + b_ref[...])
        _acc_rows(dg_ref, dln * xh)
        _acc_rows(db_ref, dln)
        dxh = dln * gv
        dx = rstd * (dxh - jnp.mean(dxh, axis=-1, keepdims=True) - xh * jnp.mean(dxh * xh, axis=-1, keepdims=True))
        dx_ref[...] = dx
        _acc_rows(cs_ref, dx)

    d = dc.shape[1]
    return _Tail(fn, [(dc, "rows"), (ln_g, "const"), (ln_b, "const")], [(d, "rows"), (d, "acc"), (d, "acc"), (d, "acc")])


def _gated_norm_bwd_tail(o, z, ng, n_heads):
    def fn(don_all, ins, outs):
        o_ref, z_ref, ng_ref = ins
        do_ref, dz_ref, dng_ref = outs
        gv = ng_ref[...]
        for hd in range(n_heads):
            sl = slice(hd * LANES, (hd + 1) * LANES)
            rn, xh, rstd = _rms(o_ref[:, sl], gv)
            zz = z_ref[:, sl]
            don = don_all[:, sl]
            dz_ref[:, sl] = don * rn * _silu_grad(zz)
            drn = don * (zz * _sigmoid(zz))
            _acc_rows(dng_ref, drn * xh)
            do_ref[:, sl] = _rms_bwd(drn, xh, rstd, gv)

    d = o.shape[1]
    return _Tail(fn, [(o, "rows"), (z, "rows"), (ng, "const")], [(d, "rows"), (d, "rows"), (LANES, "acc")])


def _mm_tn_blocked(x, dy, name, out_dtype=f32):
    t, k = x.shape
    bn = dy.shape[1] // N_DEV
    tm = _tile(t, 512)
    jb = N_DEV
    while jb > 1 and k * jb * bn * 4 > 8 * 1024 * 1024:
        jb //= 2
    nt = t // tm

    def body(x_ref, dy_ref, o_ref, *acc):
        acc_ref = acc[0] if acc else o_ref
        _acc_init(pl.program_id(1), acc_ref)
        xt = x_ref[...].astype(bf16).T
        for jj in range(jb):
            acc_ref[jj] += jnp.dot(xt, dy_ref[:, jj * bn:(jj + 1) * bn].astype(bf16), preferred_element_type=f32)
        if acc:
            @pl.when(pl.program_id(1) == nt - 1)
            def _():
                o_ref[...] = acc_ref[...].astype(out_dtype)

    return _pc(body, name=name, grid=(N_DEV // jb, nt),
               in_specs=[pl.BlockSpec((tm, k), lambda j, i: (i, 0)), pl.BlockSpec((tm, jb * bn), lambda j, i: (i, j))],
               out_specs=pl.BlockSpec((jb, k, bn), lambda j, i: (j, 0, 0)),
               out_shape=jax.ShapeDtypeStruct((N_DEV, k, bn), out_dtype),
               scratch=[] if out_dtype == f32 else [pltpu.VMEM((jb, k, bn), f32)],
               sem=("parallel", "arbitrary"))(x, dy)


def _mm_tn(x, dy, name, out_dtype=f32):
    t, k = x.shape
    n = dy.shape[1]
    tm = _tile(t, 512)
    cap = max(LANES, (2 * 1024 * 1024) // k)
    tn = n
    if n > cap:
        tn = max(c for c in range(LANES, cap + 1, LANES) if n % c == 0)
    nt = t // tm

    def body(x_ref, dy_ref, o_ref, *acc):
        acc_ref = acc[0] if acc else o_ref
        _acc_init(pl.program_id(1), acc_ref)
        acc_ref[...] += lax.dot_general(x_ref[...].astype(bf16), dy_ref[...].astype(bf16), TN, preferred_element_type=f32)
        if acc:
            @pl.when(pl.program_id(1) == nt - 1)
            def _():
                o_ref[...] = acc_ref[...].astype(out_dtype)

    return _pc(body, name=name, grid=(n // tn, nt),
               in_specs=[pl.BlockSpec((tm, k), lambda j, i: (i, 0)), pl.BlockSpec((tm, tn), lambda j, i: (i, j))],
               out_specs=pl.BlockSpec((k, tn), lambda j, i: (0, j)),
               out_shape=jax.ShapeDtypeStruct((k, n), out_dtype),
               scratch=[] if out_dtype == f32 else [pltpu.VMEM((k, tn), f32)],
               sem=("parallel", "arbitrary"))(x, dy)


def _gates_bwd(dgb, ab, a_log_pad, dt_pad, n_heads):
    t = ab.shape[0]
    tm = _tile(t, 256)

    def body(dgb_ref, ab_ref, al_ref, dt_ref, dab_ref, dal_ref, ddt_ref):
        _acc_init(pl.program_id(0), dal_ref, ddt_ref)
        ab = ab_ref[...]
        dgb = dgb_ref[...]
        lane = lax.broadcasted_iota(jnp.int32, ab.shape, 1)
        is_a = lane < n_heads
        is_b = jnp.logical_and(lane >= n_heads, lane < 2 * n_heads)
        xa = ab + dt_ref[...]
        neg_a = -jnp.exp(al_ref[...])
        dg_da = neg_a * _sigmoid(xa)
        beta = _sigmoid(ab)
        da = jnp.where(is_a, dgb * dg_da, 0.0)
        dab_ref[...] = da + jnp.where(is_b, dgb * beta * (1.0 - beta), 0.0)
        _acc_rows(dal_ref, jnp.where(is_a, dgb * neg_a * _softplus(xa), 0.0))
        _acc_rows(ddt_ref, da)

    return _pc(body, name="gates_bwd", grid=(t // tm,),
               in_specs=[_rows(tm, LANES), _rows(tm, LANES), _const((1, LANES)), _const((1, LANES))],
               out_specs=[_rows(tm, LANES), _const((SUBLANES, LANES)), _const((SUBLANES, LANES))],
               out_shape=[jax.ShapeDtypeStruct((t, LANES), f32), jax.ShapeDtypeStruct((SUBLANES, LANES), f32),
                          jax.ShapeDtypeStruct((SUBLANES, LANES), f32)],
               sem=("arbitrary",))(dgb, ab, a_log_pad, dt_pad)


def _glu_bwd(dgl, ub):
    t, d = dgl.shape
    tm = _tile(t, 256)

    def body(dgl_ref, u_ref, du_ref, cs_ref):
        _acc_init(pl.program_id(0), cs_ref)
        dgl = dgl_ref[...]
        a = u_ref[:, :d].astype(f32)
        sb = _sigmoid(u_ref[:, d:].astype(f32))
        da = dgl * sb
        db = dgl * a * sb * (1.0 - sb)
        du_ref[:, :d] = da.astype(bf16)
        du_ref[:, d:] = db.astype(bf16)
        cs_ref[0:1, :d] += jnp.sum(da, axis=0, keepdims=True)
        cs_ref[0:1, d:] += jnp.sum(db, axis=0, keepdims=True)

    return _pc(body, name="glu_bwd", grid=(t // tm,),
               in_specs=[_rows(tm, d), _rows(tm, 2 * d)],
               out_specs=[_rows(tm, 2 * d), _const((SUBLANES, 2 * d))],
               out_shape=[jax.ShapeDtypeStruct((t, 2 * d), bf16), jax.ShapeDtypeStruct((SUBLANES, 2 * d), f32)],
               sem=("arbitrary",))(dgl, ub)


def _conv_rows(s):
    return 256 if s % 256 == 0 else s


def _conv_tap_sum(pad_ref, w_ref, base, rows, width):
    acc = jnp.zeros((rows, LANES), f32)
    for j in range(width):
        acc = acc + w_ref[j:j + 1, :] * pad_ref[pl.ds(base + CONV_PAD - (width - 1) + j, rows), :]
    return acc


def _l2_silu_post(c, j, n_heads, scale):
    a = c * _sigmoid(c)
    r = lax.rsqrt(jnp.sum(a * a, axis=-1, keepdims=True) + L2_EPS)
    mult = jnp.where(j < n_heads, r * scale, jnp.where(j < 2 * n_heads, r, 1.0))
    return a, r, a * mult


def _dwconv_fwd(x, w, b, name, qk_heads=None):
    bl, s, cn = x.shape
    width = w.shape[0]
    rows = _conv_rows(s)
    scale = float(LANES) ** -0.5

    def body(x_ref, w_ref, b_ref, o_ref, pad_ref):
        j = pl.program_id(1)
        pad_ref[0:CONV_PAD, :] = jnp.zeros((CONV_PAD, LANES), f32)
        pad_ref[CONV_PAD:, :] = x_ref[0]

        def step(i, carry):
            base = pl.multiple_of(i * rows, rows)
            acc = _conv_tap_sum(pad_ref, w_ref, base, rows, width)
            if qk_heads is None:
                acc = acc + b_ref[...]
            else:
                _, _, acc = _l2_silu_post(acc, j, qk_heads, scale)
            o_ref[0, pl.ds(base, rows), :] = acc
            return carry

        lax.fori_loop(0, s // rows, step, 0)

    return _pc(body, name=name, grid=(bl, cn // LANES),
               in_specs=[pl.BlockSpec((1, s, LANES), lambda bi, j: (bi, 0, j)),
                         pl.BlockSpec((width, LANES), lambda bi, j: (0, j)),
                         pl.BlockSpec((1, LANES), lambda bi, j: (0, j))],
               out_specs=pl.BlockSpec((1, s, LANES), lambda bi, j: (bi, 0, j)),
               out_shape=jax.ShapeDtypeStruct((bl, s, cn), f32),
               scratch=[pltpu.VMEM((s + CONV_PAD, LANES), f32)],
               sem=("parallel", "parallel"))(x, w, b)


def _dwconv_bwd(x, dys, w, name, qk_heads=None):
    bl, s, cn = x.shape
    width = w.shape[0]
    wp = -(-width // SUBLANES) * SUBLANES
    rows = _conv_rows(s)
    scale = float(LANES) ** -0.5
    nblk = s // rows
    ndy = len(dys)

    def body(*refs):
        x_ref, w_ref = refs[0], refs[1]
        dy_refs = refs[2:2 + ndy]
        dx_ref, dw_ref, db_ref, xpad, dypad, acc = refs[2 + ndy:]
        j = pl.program_id(0)
        bi = pl.program_id(1)
        _acc_init(bi, acc, db_ref)
        xpad[0:CONV_PAD, :] = jnp.zeros((CONV_PAD, LANES), f32)
        xpad[CONV_PAD:, :] = x_ref[0]
        dypad[s:, :] = jnp.zeros((CONV_PAD, LANES), f32)
        if qk_heads is None:
            dypad[0:s, :] = dy_refs[0][0]
        else:
            def pre(i, carry):
                base = pl.multiple_of(i * rows, rows)
                c = _conv_tap_sum(xpad, w_ref, base, rows, width)
                a, r, _ = _l2_silu_post(c, j, qk_heads, scale)
                dq = dy_refs[0][0, pl.ds(base, rows), :]
                dk = dy_refs[1][0, pl.ds(base, rows), :]
                dv = dy_refs[2][0, pl.ds(base, rows), :]
                dy = jnp.where(j < qk_heads, dq * scale, jnp.where(j < 2 * qk_heads, dk, dv))
                da_l2 = r * (dy - a * (r * r) * jnp.sum(a * dy, axis=-1, keepdims=True))
                da = jnp.where(j < 2 * qk_heads, da_l2, dy)
                dypad[pl.ds(base, rows), :] = da * _silu_grad(c)
                return carry

            lax.fori_loop(0, nblk, pre, 0)

        def step(i, carry):
            base = pl.multiple_of(i * rows, rows)
            dxa = jnp.zeros((rows, LANES), f32)
            for jj in range(width):
                dxa = dxa + w_ref[jj:jj + 1, :] * dypad[pl.ds(base + (width - 1) - jj, rows), :]
            dx_ref[0, pl.ds(base, rows), :] = dxa
            dyc = dypad[pl.ds(base, rows), :]
            db_ref[...] += dyc.reshape(rows // SUBLANES, SUBLANES, LANES).sum(axis=0)
            for jj in range(width):
                prod = dyc * xpad[pl.ds(base + CONV_PAD - (width - 1) + jj, rows), :]
                acc[jj * SUBLANES:(jj + 1) * SUBLANES, :] += prod.reshape(rows // SUBLANES, SUBLANES, LANES).sum(axis=0)
            return carry

        lax.fori_loop(0, nblk, step, 0)

        @pl.when(bi == bl - 1)
        def _():
            dw_ref[...] = jnp.zeros((wp, LANES), f32)
            for jj in range(width):
                dw_ref[jj:jj + 1, :] = jnp.sum(acc[jj * SUBLANES:(jj + 1) * SUBLANES, :], axis=0, keepdims=True)

    if qk_heads is None:
        dy_specs = [pl.BlockSpec((1, s, LANES), lambda j, bi: (bi, 0, j))]
    else:
        hh = qk_heads
        def dy_spec(part):
            def index(j, bi):
                mine = jnp.logical_and(j >= part * hh, j < (part + 1) * hh)
                return (jnp.where(mine, bi * hh + j - part * hh, 0), 0, 0)
            return pl.BlockSpec((1, s, LANES), index)

        dy_specs = [dy_spec(0), dy_spec(1), dy_spec(2)]
    return _pc(body, name=name, grid=(cn // LANES, bl),
               in_specs=[pl.BlockSpec((1, s, LANES), lambda j, bi: (bi, 0, j)),
                         pl.BlockSpec((width, LANES), lambda j, bi: (0, j))] + dy_specs,
               out_specs=[pl.BlockSpec((1, s, LANES), lambda j, bi: (bi, 0, j)),
                          pl.BlockSpec((wp, LANES), lambda j, bi: (0, j)),
                          pl.BlockSpec((SUBLANES, LANES), lambda j, bi: (0, j))],
               out_shape=[jax.ShapeDtypeStruct((bl, s, cn), f32), jax.ShapeDtypeStruct((wp, cn), f32),
                          jax.ShapeDtypeStruct((SUBLANES, cn), f32)],
               scratch=[pltpu.VMEM((s + CONV_PAD, LANES), f32), pltpu.VMEM((s + CONV_PAD, LANES), f32),
                        pltpu.VMEM((width * SUBLANES, LANES), f32)],
               sem=("parallel", "arbitrary"))(x, w, *dys)


def _group_rows(s):
    for rows in (256, 128):
        if s % rows == 0:
            return rows
    return CHUNK


def _group_masks(rows):
    r = lax.broadcasted_iota(jnp.int32, (rows, rows), 0)
    c = lax.broadcasted_iota(jnp.int32, (rows, rows), 1)
    return r, c, r >= c


def _decay(gc_col, gc_row, causal):
    return jnp.exp(jnp.where(causal, gc_col - gc_row, NEG))


def _inv_unit_lower_many(mats, r, c):
    n = r.shape[0]
    eye = (r == c).astype(f32)
    same16 = (r >> 4) == (c >> 4)
    ads = [jnp.where(same16, a, 0.0) for a in mats]
    aos = [a - ad for a, ad in zip(mats, ads)]
    xs = ads
    tds = [-x for x in xs]
    for _ in range(3):
        xs = [_dot(x, x) for x in xs]
        tds = [td + x + _dot(td, x) for td, x in zip(tds, xs)]
    bs = [ao + _dot(td, ao) for td, ao in zip(tds, aos)]
    ps = [-b for b in bs]
    span = 2
    while span < n // 16:
        bs = [_dot(b, b) for b in bs]
        ps = [p + b + _dot(p, b) for p, b in zip(ps, bs)]
        span *= 2
    return [p + td + _dot(p, td) for p, td in zip(ps, tds)]


def _lane_cumsum(x, y, reverse, name):
    rr, n = x.shape

    def body(x_ref, y_ref, o_ref):
        i = lax.broadcasted_iota(jnp.int32, (n, n), 0)
        j = lax.broadcasted_iota(jnp.int32, (n, n), 1)
        tri = ((i >= j) if reverse else (i <= j)).astype(f32)
        o_ref[...] = jnp.dot(x_ref[...] + y_ref[...], tri, precision=HI, preferred_element_type=f32)

    spec = pl.BlockSpec((rr, n), lambda: (0, 0))
    return pl.pallas_call(body, name=name, in_specs=[spec, spec], out_specs=spec,
                          out_shape=jax.ShapeDtypeStruct((rr, n), f32))(x, y)


def _gdn_specs(n_heads, nblk, rows, hp=1, rev=False):
    def blk(n):
        return nblk - 1 - n if rev else n

    def qkv(off):
        return pl.BlockSpec((rows, hp * LANES), lambda g, n: (
            lax.div(g * hp, n_heads) * nblk + blk(n), lax.div(off * n_heads + lax.rem(g * hp, n_heads), hp)))

    def per_head(last, block_rows=rows):
        return pl.BlockSpec((hp, block_rows, last), lambda g, n: (g, blk(n), 0))

    def row_vec():
        return pl.BlockSpec((hp, 1, 1, rows), lambda g, n: (g, blk(n), 0, 0))

    return qkv, per_head, row_vec


def _gdn_prep(qkv, cols, grow, bl, s, n_heads):
    rows = _group_rows(s)
    nblk = s // rows
    bh_n = bl * n_heads
    hp = PREP_HEADS_PER_STEP if n_heads % PREP_HEADS_PER_STEP == 0 else 1
    qkv_spec, ph, rv = _gdn_specs(n_heads, nblk, rows, hp)

    def body(k_ref, v_ref, cols_ref, grow_ref, u_ref, w_ref, t_ref):
        r, c, causal = _group_masks(rows)
        mats, rhs = [], []
        for h in range(hp):
            hs = slice(h * LANES, (h + 1) * LANES)
            k = k_ref[:, hs]
            gc = cols_ref[h, :, 0:1]
            beta = cols_ref[h, :, 1:2]
            kb = k * beta
            mats.append(jnp.where(r > c, _dot_nt(kb, k) * _decay(gc, grow_ref[h, 0], causal), 0.0))
            rhs.append((v_ref[:, hs] * beta, kb * jnp.exp(gc)))
        for h, tm in enumerate(_inv_unit_lower_many(mats, r, c)):
            tb = tm.astype(bf16)
            u_ref[h] = rhs[h][0] + jnp.dot(tb, rhs[h][0].astype(bf16), preferred_element_type=f32)
            w_ref[h] = rhs[h][1] + jnp.dot(tb, rhs[h][1].astype(bf16), preferred_element_type=f32)
            t_ref[h] = tb

    return _pc(body, name="gdn_prep", grid=(bh_n // hp, nblk),
               in_specs=[qkv_spec(1), qkv_spec(2), ph(2), rv()],
               out_specs=[ph(LANES), ph(LANES), ph(rows)],
               out_shape=[jax.ShapeDtypeStruct((bh_n, s, LANES), f32), jax.ShapeDtypeStruct((bh_n, s, LANES), f32),
                          jax.ShapeDtypeStruct((bh_n, s, rows), bf16)],
               sem=("parallel", "parallel"))(qkv, qkv, cols, grow)


def _gdn_scan(qkv, u, w, cols, grow, bl, s, n_heads):
    rows = _group_rows(s)
    nblk = s // rows
    bh_n = bl * n_heads
    d = n_heads * LANES
    hp = HEADS_PER_STEP if n_heads % HEADS_PER_STEP == 0 else 1
    qkv_spec, ph, rv = _gdn_specs(n_heads, nblk, rows, hp)

    def body(q_ref, k_ref, u_ref, w_ref, cols_ref, grow_ref, o_ref, vn_ref, ss_ref, s_scr):
        _acc_init(pl.program_id(1), s_scr)
        _, _, causal = _group_masks(rows)
        hh = range(hp)
        qs = [q_ref[:, h * LANES:(h + 1) * LANES] for h in hh]
        ks = [k_ref[:, h * LANES:(h + 1) * LANES] for h in hh]
        gcs = [cols_ref[h, :, 0:1] for h in hh]
        ps = [_dot_nt(qs[h], ks[h]) * _decay(gcs[h], grow_ref[h, 0], causal) for h in hh]
        sts = [s_scr[h] for h in hh]
        for h in hh:
            ss_ref[h] = sts[h]
        vns = [u_ref[h] - _dot(w_ref[h], sts[h]) for h in hh]
        for h in hh:
            vn_ref[h] = vns[h]
        o_state = [_dot(qs[h] * jnp.exp(gcs[h]), sts[h]) for h in hh]
        o_intra = [_dot(ps[h], vns[h]) for h in hh]
        for h in hh:
            o_ref[:, h * LANES:(h + 1) * LANES] = o_state[h] + o_intra[h]
        for h in hh:
            g_last = gcs[h][rows - 1:rows, :]
            s_scr[h] = jnp.exp(g_last) * sts[h] + _dot_tn(ks[h] * jnp.exp(g_last - gcs[h]), vns[h])

    return _pc(body, name="gdn_scan", grid=(bh_n // hp, nblk),
               in_specs=[qkv_spec(0), qkv_spec(1), ph(LANES), ph(LANES), ph(2), rv()],
               out_specs=[qkv_spec(0), ph(LANES), ph(LANES, block_rows=LANES)],
               out_shape=[jax.ShapeDtypeStruct((bl * s, d), f32), jax.ShapeDtypeStruct((bh_n, s, LANES), f32),
                          jax.ShapeDtypeStruct((bh_n, nblk * LANES, LANES), f32)],
               scratch=[pltpu.VMEM((hp, LANES, LANES), f32)],
               sem=("parallel", "arbitrary"))(qkv, qkv, u, w, cols, grow)


def _gdn_scan_bwd(do, qkv, w, vn, cols, grow, ss, bl, s, n_heads):
    rows = _group_rows(s)
    nblk = s // rows
    bh_n = bl * n_heads
    hp = HEADS_PER_STEP if n_heads % HEADS_PER_STEP == 0 else 1
    qkv_spec, ph, rv = _gdn_specs(n_heads, nblk, rows, hp, rev=True)

    def body(do_ref, q_ref, k_ref, w_ref, vn_ref, cols_ref, grow_ref, ss_ref,
             du_ref, dw_ref, dq_ref, dk_ref, dcol_ref, drow_ref, ds_scr):
        _acc_init(pl.program_id(1), ds_scr)
        _, _, causal = _group_masks(rows)
        last_row = lax.broadcasted_iota(jnp.int32, (rows, 1), 0) == rows - 1
        hh = range(hp)
        dos = [do_ref[:, h * LANES:(h + 1) * LANES] for h in hh]
        qs = [q_ref[:, h * LANES:(h + 1) * LANES] for h in hh]
        ks = [k_ref[:, h * LANES:(h + 1) * LANES] for h in hh]
        vns = [vn_ref[h] for h in hh]
        gcs = [cols_ref[h, :, 0:1] for h in hh]
        sts = [ss_ref[h] for h in hh]
        dss = [ds_scr[h] for h in hh]
        dmats = [_decay(gcs[h], grow_ref[h, 0], causal) for h in hh]
        gams = [jnp.exp(gc) for gc in gcs]
        qgs = [qs[h] * gams[h] for h in hh]
        g_lasts = [gc[rows - 1:rows, :] for gc in gcs]
        kd_scales = [jnp.exp(g_lasts[h] - gcs[h]) for h in hh]
        kdecs = [ks[h] * kd_scales[h] for h in hh]
        qks = [_dot_nt(qs[h], ks[h]) for h in hh]
        dpds = [_dot_nt(dos[h], vns[h]) * dmats[h] for h in hh]
        dvns = [_dot_tn(qks[h] * dmats[h], dos[h]) + _dot(kdecs[h], dss[h]) for h in hh]
        for h in hh:
            du_ref[h] = dvns[h]
        dkdecs = [_dot_nt(vns[h], dss[h]) for h in hh]
        for h in hh:
            dw_ref[h] = -_dot_nt(dvns[h], sts[h])
        dqgs = [_dot_nt(dos[h], sts[h]) for h in hh]
        dq_intra = [_dot(dpds[h], ks[h]) for h in hh]
        dk_intra = [_dot_tn(dpds[h], qs[h]) for h in hh]
        for h in hh:
            dq_ref[h] = dqgs[h] * gams[h] + dq_intra[h]
            dk_ref[h] = dk_intra[h] + dkdecs[h] * kd_scales[h]
            ep = dpds[h] * qks[h]
            drow_ref[h, 0] = -jnp.sum(ep, axis=0, keepdims=True)
            kd_rows = jnp.sum(dkdecs[h] * kdecs[h], axis=-1, keepdims=True)
            extra = jnp.sum(kd_rows) + jnp.exp(g_lasts[h]) * jnp.sum(sts[h] * dss[h])
            dcol_ref[h] = (jnp.sum(dqgs[h] * qgs[h], axis=-1, keepdims=True) + jnp.sum(ep, axis=-1, keepdims=True)
                           - kd_rows + jnp.where(last_row, extra, 0.0))
        ds_new = [jnp.exp(g_lasts[h]) * dss[h] + _dot_tn(qgs[h], dos[h]) - _dot_tn(w_ref[h], dvns[h]) for h in hh]
        for h in hh:
            ds_scr[h] = ds_new[h]

    return _pc(body, name="gdn_scan_bwd", grid=(bh_n // hp, nblk),
               in_specs=[qkv_spec(0), qkv_spec(0), qkv_spec(1), ph(LANES), ph(LANES), ph(2), rv(),
                         ph(LANES, block_rows=LANES)],
               out_specs=[ph(LANES), ph(LANES), ph(LANES), ph(LANES), ph(1), rv()],
               out_shape=[jax.ShapeDtypeStruct((bh_n, s, LANES), f32)] * 4
               + [jax.ShapeDtypeStruct((bh_n, s, 1), f32), jax.ShapeDtypeStruct((bh_n, nblk, 1, rows), f32)],
               scratch=[pltpu.VMEM((hp, LANES, LANES), f32)],
               sem=("parallel", "arbitrary"))(do, qkv, qkv, w, vn, cols, grow, ss)


def _gdn_prep_bwd(qkv, cols, grow, tmat, du, dw, dk_scan, dcol_scan, drow_scan, bl, s, n_heads):
    rows = _group_rows(s)
    nblk = s // rows
    bh_n = bl * n_heads
    hp = PREP_HEADS_PER_STEP if n_heads % PREP_HEADS_PER_STEP == 0 else 1
    qkv_spec, ph, rv = _gdn_specs(n_heads, nblk, rows, hp)

    def body(k_ref, v_ref, cols_ref, grow_ref, t_ref, du_ref, dw_ref, dks_ref, dcs_ref, drs_ref,
             dk_ref, dv_ref, dcols_ref, drow_ref):
        r, c, causal = _group_masks(rows)
        hh = range(hp)
        ks = [k_ref[:, h * LANES:(h + 1) * LANES] for h in hh]
        vs = [v_ref[:, h * LANES:(h + 1) * LANES] for h in hh]
        gcs = [cols_ref[h, :, 0:1] for h in hh]
        betas = [cols_ref[h, :, 1:2] for h in hh]
        tms = [t_ref[h] for h in hh]
        dus = [du_ref[h] for h in hh]
        dws = [dw_ref[h] for h in hh]
        gams = [jnp.exp(gc) for gc in gcs]
        kbs = [k * b for k, b in zip(ks, betas)]
        kbgs = [kb * g for kb, g in zip(kbs, gams)]
        dts = [_dot_nt(dus[h], vs[h] * betas[h]) + _dot_nt(dws[h], kbgs[h]) for h in hh]
        dvbs = [dus[h] + _dot_tn(tms[h], dus[h]) for h in hh]
        dkbgs = [dws[h] + _dot_tn(tms[h], dws[h]) for h in hh]
        kks = [_dot_nt(kbs[h], ks[h]) for h in hh]
        inner = [dts[h] + _dot_nt(dts[h], tms[h]) for h in hh]
        dads = [jnp.where(r > c, -(inner[h] + _dot_tn(tms[h], inner[h])), 0.0) * _decay(gcs[h], grow_ref[h, 0], causal)
                for h in hh]
        dkbs = [dkbgs[h] * gams[h] + _dot(dads[h], ks[h]) for h in hh]
        dk2 = [_dot_tn(dads[h], kbs[h]) for h in hh]
        for h in hh:
            dk_ref[h] = dks_ref[h] + dk2[h] + dkbs[h] * betas[h]
            dv_ref[h] = dvbs[h] * betas[h]
            ea = dads[h] * kks[h]
            dcols_ref[h, :, 0:1] = (dcs_ref[h] + jnp.sum(dkbgs[h] * kbgs[h], axis=-1, keepdims=True)
                                    + jnp.sum(ea, axis=-1, keepdims=True))
            dcols_ref[h, :, 1:2] = (jnp.sum(dvbs[h] * vs[h], axis=-1, keepdims=True)
                                    + jnp.sum(dkbs[h] * ks[h], axis=-1, keepdims=True))
            drow_ref[h, 0] = drs_ref[h, 0] - jnp.sum(ea, axis=0, keepdims=True)

    return _pc(body, name="gdn_prep_bwd", grid=(bh_n // hp, nblk),
               in_specs=[qkv_spec(1), qkv_spec(2), ph(2), rv(), ph(rows), ph(LANES), ph(LANES), ph(LANES), ph(1), rv()],
               out_specs=[ph(LANES), ph(LANES), ph(2), rv()],
               out_shape=[jax.ShapeDtypeStruct((bh_n, s, LANES), f32), jax.ShapeDtypeStruct((bh_n, s, LANES), f32),
                          jax.ShapeDtypeStruct((bh_n, s, 2), f32), jax.ShapeDtypeStruct((bh_n, nblk, 1, rows), f32)],
               sem=("parallel", "parallel"))(qkv, qkv, cols, grow, tmat, du, dw, dk_scan, dcol_scan, drow_scan)


def _row(v):
    return v.reshape(1, -1).astype(f32)


def _pad_lanes(v):
    v = v.reshape(1, -1).astype(f32)
    return jnp.pad(v, ((0, 0), (0, LANES - v.shape[1])))


def _local_step(x, tgt, p, need, emit):
    bl, s, d = x.shape
    t = bl * s
    n_heads = p["gdn_a_log"].shape[-1]
    assert d == n_heads * LANES and s % CHUNK == 0
    x2 = x.reshape(t, d)
    tgt2 = tgt.reshape(t, d)
    gr = {}

    n0, ub, gl = _pw1_glu(x2, _row(p["norm_mix_g"][0]), p["cv_w_pw1"], _row(p["cv_b_pw1"]))
    cv = need("conv", n0)
    dc = _dwconv_fwd(gl.reshape(bl, s, d), cv["cv_w_dw"], _row(p["cv_b_dw"]), "dwconv_fwd").reshape(t, d)
    sb, h1 = _ln_silu_mm_res(dc, _row(p["cv_ln_g"]), _row(p["cv_ln_b"]), cv["cv_w_pw2"], _row(p["cv_b_pw2"]), x2)
    m0 = need("mlp0", h1)
    n1, f0, r0, h2 = _mlp_fwd(h1, _row(p["norm_ffn_g"][0]), m0["w1"], m0["w2"], "mlp_fwd0")

    gd = need("gdn", h2)
    w_in = gd["w_in"]
    w_ab = jnp.pad(w_in[:, 4 * d:], ((0, 0), (0, LANES - 2 * n_heads)))
    a_log_pad = _pad_lanes(p["gdn_a_log"])
    dt_pad = _pad_lanes(p["gdn_dt_bias"])
    n2, qkv_pre, z, ab, gbeta = _gdn_in(h2, _row(p["norm_mix_g"][1]), w_in, w_ab, a_log_pad, dt_pad, n_heads)
    zero_bias = jnp.zeros((1, 3 * d), f32)
    qkv = _dwconv_fwd(qkv_pre.reshape(bl, s, 3 * d), cv["gdn_conv_w"], zero_bias, "sconv_fwd", qk_heads=n_heads).reshape(t, 3 * d)
    bh_n, rows = bl * n_heads, _group_rows(s)
    gates = gbeta[:, :2 * n_heads].reshape(bl, s, 2, n_heads).transpose(2, 0, 3, 1).reshape(2, bh_n, s)
    g_lanes = gates[0].reshape(bh_n * (s // rows), rows)
    gc_lanes = _lane_cumsum(g_lanes, jnp.zeros_like(g_lanes), False, "gdn_gate_cumsum")
    grow = gc_lanes.reshape(bh_n, s // rows, 1, rows)
    cols = jnp.stack([gc_lanes.reshape(bh_n, s), gates[1]], axis=-1)
    u, w, tmat = _gdn_prep(qkv, cols, grow, bl, s, n_heads)
    o, vn, ss = _gdn_scan(qkv, u, w, cols, grow, bl, s, n_heads)
    onb, h3 = _gated_norm_mm_res(o, z, _row(p["gdn_norm_g"]), gd["w_out"], h2, n_heads)
    m1 = need("mlp1", h3)
    n3, f1, r1, loss_acc, dh4, dgf = _mlp_fwd(h3, _row(p["norm_ffn_g"][1]), m1["w1"], m1["w2"], "mlp_fwd1_loss",
                                              loss=(_row(p["final_norm_g"]), tgt2))
    loss = loss_acc[0, 0]
    gr["loss"] = loss_acc[0, 0:1]
    gr["final_norm_g"] = dgf[0]

    df1, dh3, dg_ffn1, _ = _mlp_bwd(dh4, h3, _row(p["norm_ffn_g"][1]), f1, m1["w1"], m1["w2"], "mlp_bwd1")
    dw2_1 = _mm_tn(r1, dh4, "dw_mlp2_1", bf16)
    dw1_1 = _mm_tn_blocked(n3, df1, "dw_mlp1_1", bf16)

    dw_out = _mm_tn(onb, dh3, "dw_gdn_out", bf16)
    tie = emit("late", {"mlp_w2_1": dw2_1, "mlp_w1_1": dw1_1, "gdn_w_out": dw_out})
    do, dz, dng = _mm_nt([(dh3, gd["w_out"])], "dx_gdn_out",
                         _gated_norm_bwd_tail(o, z, _row(p["gdn_norm_g"]) + tie, n_heads))
    gr["gdn_norm_g"] = dng[0]
    du, dw_, dq, dk_scan, dcol_scan, drow_scan = _gdn_scan_bwd(do, qkv, w, vn, cols, grow, ss, bl, s, n_heads)
    dk, dv, dcols, drow = _gdn_prep_bwd(qkv, cols, grow, tmat, du, dw_, dk_scan, dcol_scan, drow_scan, bl, s, n_heads)
    dg_lanes = _lane_cumsum(dcols[..., 0].reshape(g_lanes.shape), drow.reshape(g_lanes.shape), True, "gdn_gate_cumsum_bwd")
    dgb2 = jnp.stack([dg_lanes.reshape(bh_n, s), dcols[..., 1]], axis=-1)
    dqkv_pre, dconv_w, _ = _dwconv_bwd(qkv_pre.reshape(bl, s, 3 * d), [dq, dk, dv], cv["gdn_conv_w"], "sconv_bwd", qk_heads=n_heads)
    gr["gdn_conv_w"] = dconv_w[:cv["gdn_conv_w"].shape[0]]
    dgb = dgb2.reshape(bl, n_heads, s, 2).transpose(0, 2, 3, 1).reshape(t, 2 * n_heads)
    dgb = jnp.pad(dgb, ((0, 0), (0, LANES - 2 * n_heads)))
    dab, dal, ddt = _gates_bwd(dgb, ab, a_log_pad, dt_pad, n_heads)
    gr["gdn_a_log"] = dal[0, :n_heads]
    gr["gdn_dt_bias"] = ddt[0, :n_heads]
    dqkv2 = dqkv_pre.reshape(t, 3 * d)
    dw_in = jnp.concatenate(
        [_mm_tn(n2, dqkv2, "dw_gdn_in_qkv"), _mm_tn(n2, dz, "dw_gdn_in_z"), _mm_tn(n2, dab, "dw_gdn_in_ab")[:, :2 * n_heads]], axis=1)
    tie = emit("gdn_in", {"gdn_w_in": dw_in})
    dh2, dg_mix1 = _mm_nt([(dqkv2, (w_in, 0)), (dz, (w_in, 3)), (dab, w_ab)], "dx_gdn_in",
                          _rms_bwd_tail(h2, _row(p["norm_mix_g"][1]) + tie, dh3))

    df0, dh1, dg_ffn0, cs_h1 = _mlp_bwd(dh2, h1, _row(p["norm_ffn_g"][0]), f0, m0["w1"], m0["w2"], "mlp_bwd0")
    dw2_0 = _mm_tn(r0, dh2, "dw_mlp2_0", bf16)
    dw1_0 = _mm_tn_blocked(n1, df0, "dw_mlp1_0", bf16)
    dw_pw2 = _mm_tn(sb, dh1, "dw_pw2", bf16)
    tie = emit("mlp0", {"mlp_w2_0": dw2_0, "mlp_w1_0": dw1_0, "cv_w_pw2": dw_pw2})
    gr["norm_ffn_g"] = jnp.stack([dg_ffn0[0], dg_ffn1[0]])

    gr["cv_b_pw2"] = cs_h1[0]
    ddc, dlng, dlnb, cs_dc = _mm_nt([(dh1, cv["cv_w_pw2"])], "dx_pw2",
                                    _ln_silu_bwd_tail(dc, _row(p["cv_ln_g"]) + tie, _row(p["cv_ln_b"])))
    gr["cv_ln_g"] = dlng[0]
    gr["cv_ln_b"] = dlnb[0]
    gr["cv_b_dw"] = cs_dc[0]
    dgl, dw_dw, _ = _dwconv_bwd(gl.reshape(bl, s, d), [ddc.reshape(bl, s, d)], cv["cv_w_dw"], "dwconv_bwd")
    gr["cv_w_dw"] = dw_dw[:cv["cv_w_dw"].shape[0]]
    dub, cs_u = _glu_bwd(dgl.reshape(t, d), ub)
    gr["cv_b_pw1"] = cs_u[0]
    dw_pw1 = _mm_tn_blocked(n0, dub, "dw_pw1", bf16)
    tie = emit("last", {"cv_w_pw1": dw_pw1, "small": gr})
    dx, dg_mix0 = _mm_nt([(dub, ("blocks", p["cv_w_pw1"]))], "dx_pw1",
                         _rms_bwd_tail(x2, _row(p["norm_mix_g"][0]) + tie, dh1))
    return loss, dx.reshape(bl, s, d), {"norm_mix_g": jnp.stack([dg_mix0[0], dg_mix1[0]])}


ANY = pl.BlockSpec(memory_space=pl.ANY)
MESH = pl.DeviceIdType.MESH


def _flip(v, bit):
    return 1 - v if bit else v


def _all_gather_many(shards):
    na = len(shards)

    def body(*refs):
        x_refs, o_refs = refs[:na], refs[na:2 * na]
        send_sems, recv_sems, local_sems = refs[2 * na:]
        x, y, c = lax.axis_index("x"), lax.axis_index("y"), lax.axis_index("c")
        me, sibling = (x, y, c), (x, y, 1 - c)
        chips = [(1 - x, y), (x, 1 - y), (1 - x, 1 - y)]

        def copy(a, k, block, to, src=None):
            px, py, pc = block
            dst = o_refs[a].at[4 * px + 2 * py + pc]
            return pltpu.make_async_remote_copy(
                src_ref=dst if src is None else src, dst_ref=dst,
                send_sem=send_sems.at[a, k], recv_sem=recv_sems.at[a, k], device_id=to, device_id_type=MESH)

        mine = [pltpu.make_async_copy(x_refs[a], o_refs[a].at[4 * x + 2 * y + c], local_sems.at[a]) for a in range(na)]
        first = []
        for a in range(na):
            first.append(copy(a, 0, me, sibling, src=x_refs[a]))
            first += [copy(a, 1 + j, me, (*chip, c), src=x_refs[a]) for j, chip in enumerate(chips)]
        for cp in mine + first:
            cp.start()
        passed = []
        for j, chip in enumerate(chips):
            for a in range(na):
                copy(a, 1 + j, (*chip, c), me).wait_recv()
                fwd = copy(a, 4 + j, (*chip, c), sibling)
                fwd.start()
                passed.append(fwd)
        for a in range(na):
            copy(a, 0, sibling, me).wait_recv()
        for j, chip in enumerate(chips):
            for a in range(na):
                copy(a, 4 + j, (*chip, 1 - c), me).wait_recv()
        for cp in first + passed:
            cp.wait_send()
        for cp in mine:
            cp.wait()

    return pl.pallas_call(
        body, name="weights_all_gather",
        out_shape=[jax.ShapeDtypeStruct((N_DEV,) + a.shape, a.dtype) for a in shards],
        in_specs=[ANY] * na, out_specs=[ANY] * na,
        scratch_shapes=[pltpu.SemaphoreType.DMA((na, 7)), pltpu.SemaphoreType.DMA((na, 7)), pltpu.SemaphoreType.DMA((na,))],
        compiler_params=pltpu.CompilerParams(has_side_effects=True),
    )(*shards)


HBM = pl.BlockSpec(memory_space=pltpu.HBM)
SEM = pl.BlockSpec(memory_space=pltpu.SEMAPHORE)
EFFECT = pltpu.SideEffectType.DATAFLOW_SIDE_EFFECTING
N_PEERS = N_DEV - 1


def _exchange_copies(src_refs, land_refs, send_sems, recv_sems, scatter):
    x, y, c = lax.axis_index("x"), lax.axis_index("y"), lax.axis_index("c")
    me = 4 * x + 2 * y + c
    copies = []
    for a, (src, land) in enumerate(zip(src_refs, land_refs)):
        for k in range(1, N_DEV):
            px, py, pc = _flip(x, k & 4), _flip(y, k & 2), _flip(c, k & 1)
            i = a * N_PEERS + k - 1
            copies.append(pltpu.make_async_remote_copy(
                src_ref=src.at[4 * px + 2 * py + pc] if scatter[a] else src, dst_ref=land.at[me],
                send_sem=send_sems.at[i], recv_sem=recv_sems.at[i], device_id=(px, py, pc), device_id_type=MESH))
    return copies


def _exchange_start(srcs, scatter, name):
    na = len(srcs)
    lands = [lax.empty(s.shape if sc else (N_DEV,) + s.shape, s.dtype) for s, sc in zip(srcs, scatter)]

    def body(*refs):
        copies = _exchange_copies(refs[:na], refs[na:2 * na], refs[2 * na], refs[2 * na + 1], scatter)
        for cp in copies:
            cp.start()
        token = refs[-1]
        token[...] = jnp.zeros_like(token)

    outs = pl.pallas_call(
        body, name=name,
        out_shape=(pltpu.SemaphoreType.DMA((na * N_PEERS,)), pltpu.SemaphoreType.DMA((na * N_PEERS,)))
        + tuple(pltpu.HBM(a.shape, a.dtype) for a in srcs + lands) + (jax.ShapeDtypeStruct((SUBLANES, LANES), f32),),
        in_specs=[HBM] * (2 * na),
        out_specs=(SEM, SEM) + (HBM,) * (2 * na) + (pl.BlockSpec(memory_space=pltpu.VMEM),),
        input_output_aliases={i: 2 + i for i in range(2 * na)},
        compiler_params=pltpu.CompilerParams(has_side_effects=EFFECT),
    )(*[pltpu.with_memory_space_constraint(a, pltpu.HBM) for a in srcs + lands])
    return outs[0], outs[1], list(outs[2:2 + na]), list(outs[2 + na:2 + 2 * na]), outs[-1]


def _exchange_wait(started, after, scatter, name):
    send_sems, recv_sems, srcs, lands, _ = started
    na = len(srcs)

    def body(*refs):
        for cp in _exchange_copies(refs[:na], refs[na:2 * na], refs[2 * na], refs[2 * na + 1], scatter):
            cp.wait_send()
            cp.wait_recv()

    outs = pl.pallas_call(
        body, name=name,
        out_shape=tuple(pltpu.HBM(a.shape, a.dtype) for a in srcs + lands),
        in_specs=[HBM] * (2 * na) + [SEM, SEM, ANY], out_specs=(HBM,) * (2 * na),
        input_output_aliases={i: i for i in range(2 * na)},
        compiler_params=pltpu.CompilerParams(has_side_effects=EFFECT),
    )(*srcs, *lands, send_sems, recv_sems, after)
    return list(outs[:na]), list(outs[na:])


def _own_block(land, block, me):
    return lax.dynamic_update_index_in_dim(land, block, me, 0)


def _sum8_adamw(r2, w, m, v, layer, name):
    _, rr, cc = r2.shape
    tr = _tile(rr, 256)
    bc1 = 1.0 - ADAM_B1 ** ADAM_STEP
    bc2 = 1.0 - ADAM_B2 ** ADAM_STEP

    def body(r_ref, w_ref, m_ref, v_ref, g_ref, d_ref, nm_ref, nv_ref):
        gv = r_ref[0].astype(f32)
        for q in range(1, N_DEV):
            gv = gv + r_ref[q].astype(f32)
        g_ref[...] = gv
        nm = ADAM_B1 * m_ref[...] + (1.0 - ADAM_B1) * gv
        nv = ADAM_B2 * v_ref[...] + (1.0 - ADAM_B2) * (gv * gv)
        nm_ref[...] = nm
        nv_ref[...] = nv
        d_ref[...] = -ADAM_LR * ((nm / bc1) / (jnp.sqrt(nv / bc2) + ADAM_EPS) + ADAM_WD * w_ref[...])

    lspec = pl.BlockSpec((None, tr, cc), lambda i: (layer, i, 0))
    return _pc(body, name=name, grid=(rr // tr,),
               in_specs=[pl.BlockSpec((N_DEV, tr, cc), lambda i: (0, i, 0)), lspec, lspec, lspec],
               out_specs=[_rows(tr, cc)] * 4, out_shape=[jax.ShapeDtypeStruct((rr, cc), f32)] * 4,
               sem=("parallel",))(r2, w, m, v)


def _sum_devices(recv, name):
    _, rr, _ = recv.shape
    tr = _tile(rr, 512)

    def body(r_ref, o_ref):
        acc = r_ref[0]
        for i in range(1, N_DEV):
            acc = acc + r_ref[i]
        o_ref[...] = acc

    return _pc(body, name=name, grid=(rr // tr,),
               in_specs=[pl.BlockSpec((N_DEV, tr, LANES), lambda i: (0, i, 0))],
               out_specs=_rows(tr, LANES), out_shape=jax.ShapeDtypeStruct((rr, LANES), f32), sem=("parallel",))(recv)


def _adamw(w, g, m, v):
    rr = w.shape[0]
    tr = _tile(rr, 512)
    bc1 = 1.0 - ADAM_B1 ** ADAM_STEP
    bc2 = 1.0 - ADAM_B2 ** ADAM_STEP

    def body(w_ref, g_ref, m_ref, v_ref, d_ref, nm_ref, nv_ref):
        gv = g_ref[...]
        nm = ADAM_B1 * m_ref[...] + (1.0 - ADAM_B1) * gv
        nv = ADAM_B2 * v_ref[...] + (1.0 - ADAM_B2) * (gv * gv)
        nm_ref[...] = nm
        nv_ref[...] = nv
        d_ref[...] = -ADAM_LR * ((nm / bc1) / (jnp.sqrt(nv / bc2) + ADAM_EPS) + ADAM_WD * w_ref[...])

    spec = _rows(tr, LANES)
    return _pc(body, name="adamw", grid=(rr // tr,), in_specs=[spec] * 4, out_specs=[spec] * 3,
               out_shape=[jax.ShapeDtypeStruct((rr, LANES), f32)] * 3, sem=("parallel",))(w, g, m, v)


PACK_ROWS = 512
PART_ROWS = SUBLANES


def _pack(arrs):
    parts, sizes = [], []
    for a in arrs:
        flat = a.reshape(-1)
        n = flat.shape[0]
        rows = -(-n // (LANES * PART_ROWS)) * PART_ROWS
        if rows * LANES != n:
            flat = jnp.pad(flat, (0, rows * LANES - n))
        parts.append(flat.reshape(rows, LANES))
        sizes.append((rows, n))
    total = sum(r for r, _ in sizes)
    padded = -(-total // PACK_ROWS) * PACK_ROWS
    if padded > total:
        parts.append(jnp.zeros((padded - total, LANES), parts[0].dtype))
    return jnp.concatenate(parts, axis=0), sizes


def _unpack(packed, sizes, shapes):
    out, off = [], 0
    for (rows, n), shp in zip(sizes, shapes):
        piece = lax.slice_in_dim(packed, off, off + rows, axis=0)
        if rows * LANES != n:
            piece = lax.slice_in_dim(piece.reshape(-1), 0, n, axis=0)
        out.append(piece.reshape(tuple(shp)))
        off += rows
    return out


def _cols_to_blocks(a):
    n = a.shape[-1] // N_DEV
    a = a.reshape(a.shape[:-1] + (N_DEV, n))
    return jnp.moveaxis(a, -2, 0)


def _blocks_to_cols(a):
    a = jnp.moveaxis(a, 0, -2)
    return a.reshape(a.shape[:-2] + (a.shape[-2] * a.shape[-1],))


def _rows_to_blocks(a):
    k = a.shape[-2] // N_DEV
    a = a.reshape(a.shape[:-2] + (N_DEV, k, a.shape[-1]))
    return jnp.moveaxis(a, -3, 0)


def _blocks_to_rows(a):
    a = jnp.moveaxis(a, 0, -3)
    return a.reshape(a.shape[:-3] + (a.shape[-3] * a.shape[-2], a.shape[-1]))


COL_SHARDED = ("cv_w_pw1", "gdn_w_in", "mlp_w1")
ROW_SHARDED = ("cv_w_pw2", "gdn_w_out", "mlp_w2")
CONV_SHARDED = ("cv_w_dw", "gdn_conv_w")
REPLICATED = ("norm_mix_g", "norm_ffn_g", "final_norm_g", "cv_b_pw1", "cv_b_dw", "cv_ln_g", "cv_ln_b", "cv_b_pw2",
              "gdn_a_log", "gdn_dt_bias", "gdn_norm_g")
WEIGHTS = ("norm_mix_g", "norm_ffn_g", "final_norm_g", "cv_w_pw1", "cv_b_pw1", "cv_w_dw", "cv_b_dw", "cv_ln_g",
           "cv_ln_b", "cv_w_pw2", "cv_b_pw2", "gdn_w_in", "gdn_conv_w", "gdn_a_log", "gdn_dt_bias", "gdn_norm_g",
           "gdn_w_out", "mlp_w1", "mlp_w2")
MATMUL_SHARDED = COL_SHARDED + ROW_SHARDED


def _squeeze_layer(name, a):
    if name in ("norm_mix_g", "norm_ffn_g", "final_norm_g", "mlp_w1", "mlp_w2"):
        return a
    return a[0]


def _gather_weights(shards):
    me = 4 * lax.axis_index("x") + 2 * lax.axis_index("y") + lax.axis_index("c")
    pw1 = _all_gather_many([shards["cv_w_pw1"].astype(bf16)])[0]
    now = {"cv_w_pw1": pw1}

    def cast(a, tie):
        return (a + tie).astype(bf16)

    token, _ = lax.optimization_barrier((jnp.zeros((), f32), pw1))
    later, started = {}, {}
    for group in ("conv", "mlp0", "gdn", "mlp1"):
        if group == "conv":
            srcs = [cast(shards["cv_w_pw2"], token)] + [shards[n] + token for n in CONV_SHARDED]
        elif group == "gdn":
            srcs = [cast(shards["gdn_w_in"], token).reshape(-1, LANES), cast(shards["gdn_w_out"], token)]
        else:
            layer = int(group[-1])
            srcs = [cast(shards["mlp_w1"][layer], token), cast(shards["mlp_w2"][layer], token)]
        later[group] = srcs
        started[group] = _exchange_start(srcs, [False] * len(srcs), f"weights_{group}_start")
        token = started[group][4][0, 0]

    def need(group, after):
        srcs, lands = _exchange_wait(started[group], after, [False] * len(later[group]), f"weights_{group}_wait")
        lands = [_own_block(ld, own, me) for ld, own in zip(lands, srcs)]
        if group == "conv":
            out = {"cv_w_pw2": _blocks_to_rows(lands[0])}
            out.update({n: _blocks_to_cols(ld) for n, ld in zip(CONV_SHARDED, lands[1:])})
            return out
        if group == "gdn":
            w_in = _blocks_to_cols(lands[0].reshape((N_DEV,) + shards["gdn_w_in"].shape))
            return {"w_in": w_in, "w_out": _blocks_to_rows(lands[1])}
        return {"w1": lands[0], "w2": lands[1]}

    return now, need, token


def kernel(x, norm_mix_g, norm_ffn_g, final_norm_g, cv_w_pw1, cv_b_pw1, cv_w_dw, cv_b_dw, cv_ln_g, cv_ln_b, cv_w_pw2, cv_b_pw2, gdn_w_in, gdn_conv_w, gdn_a_log, gdn_dt_bias, gdn_norm_g, gdn_w_out, mlp_w1, mlp_w2, loss_target, m_norm_mix_g, m_norm_ffn_g, m_final_norm_g, m_cv_w_pw1, m_cv_b_pw1, m_cv_w_dw, m_cv_b_dw, m_cv_ln_g, m_cv_ln_b, m_cv_w_pw2, m_cv_b_pw2, m_gdn_w_in, m_gdn_conv_w, m_gdn_a_log, m_gdn_dt_bias, m_gdn_norm_g, m_gdn_w_out, m_mlp_w1, m_mlp_w2, v_norm_mix_g, v_norm_ffn_g, v_final_norm_g, v_cv_w_pw1, v_cv_b_pw1, v_cv_w_dw, v_cv_b_dw, v_cv_ln_g, v_cv_ln_b, v_cv_w_pw2, v_cv_b_pw2, v_gdn_w_in, v_gdn_conv_w, v_gdn_a_log, v_gdn_dt_bias, v_gdn_norm_g, v_gdn_w_out, v_mlp_w1, v_mlp_w2):
    w_in = dict(zip(WEIGHTS, (norm_mix_g, norm_ffn_g, final_norm_g, cv_w_pw1, cv_b_pw1, cv_w_dw, cv_b_dw, cv_ln_g, cv_ln_b, cv_w_pw2, cv_b_pw2, gdn_w_in, gdn_conv_w, gdn_a_log, gdn_dt_bias, gdn_norm_g, gdn_w_out, mlp_w1, mlp_w2)))
    m_in = dict(zip(WEIGHTS, (m_norm_mix_g, m_norm_ffn_g, m_final_norm_g, m_cv_w_pw1, m_cv_b_pw1, m_cv_w_dw, m_cv_b_dw, m_cv_ln_g, m_cv_ln_b, m_cv_w_pw2, m_cv_b_pw2, m_gdn_w_in, m_gdn_conv_w, m_gdn_a_log, m_gdn_dt_bias, m_gdn_norm_g, m_gdn_w_out, m_mlp_w1, m_mlp_w2)))
    v_in = dict(zip(WEIGHTS, (v_norm_mix_g, v_norm_ffn_g, v_final_norm_g, v_cv_w_pw1, v_cv_b_pw1, v_cv_w_dw, v_cv_b_dw, v_cv_ln_g, v_cv_ln_b, v_cv_w_pw2, v_cv_b_pw2, v_gdn_w_in, v_gdn_conv_w, v_gdn_a_log, v_gdn_dt_bias, v_gdn_norm_g, v_gdn_w_out, v_mlp_w1, v_mlp_w2)))
    me = 4 * lax.axis_index("x") + 2 * lax.axis_index("y") + lax.axis_index("c")

    shards = {n: _squeeze_layer(n, w_in[n]) for n in WEIGHTS}
    first, need, token = _gather_weights(shards)
    params = {n: shards[n] for n in REPLICATED}
    params.update(first)
    params["norm_mix_g"] = params["norm_mix_g"] + token

    def row_blocks(a):
        return a.reshape(N_DEV, a.shape[0] // N_DEV, a.shape[1])

    def flat_blocks(a):
        k, n8 = a.shape
        return _cols_to_blocks(a).reshape(N_DEV, k * (n8 // N_DEV) // LANES, LANES)

    def as_blocks(n, a):
        if n == "gdn_w_in":
            return flat_blocks(a).astype(bf16)
        return a if a.ndim == 3 else row_blocks(a)

    sent = []

    small = REPLICATED + CONV_SHARDED
    small_early = tuple(n for n in small if n != "norm_mix_g") + ("loss",)
    small_info = {}

    def emit(group, grads_out):
        names, blocks, scatter = list(grads_out), [], []
        for n in names:
            if n == "small":
                packed, small_info["sizes"] = _pack([grads_out[n][k] for k in small_early])
                small_info["shapes"] = [grads_out[n][k].shape for k in small_early]
                blocks.append(packed)
                scatter.append(False)
            else:
                blocks.append(as_blocks(n, grads_out[n]))
                scatter.append(True)
        sent.append((names, _exchange_start(blocks, scatter, f"grads_{group}_start"), scatter))
        return sent[-1][1][4][0, 0]

    _, grad_x, gr = _local_step(x, loss_target, params, need, emit)
    nm_shape = gr["norm_mix_g"].shape
    sent.append((["norm_mix_g"], _exchange_start([gr["norm_mix_g"].reshape(-1, LANES)], [False], "grads_norm_mix_start"), [False]))

    recv = {}

    def finish(entry, after):
        names, st, scatter = entry
        srcs, lands = _exchange_wait(st, after, scatter, f"grads_{names[0]}_wait")
        for n, ld, blk, sc in zip(names, lands, srcs, scatter):
            own = lax.dynamic_index_in_dim(blk, me, 0, keepdims=False) if sc else blk
            recv[n] = _own_block(ld, own, me)
        return lands[0]

    def as3d(n, a):
        if n == "gdn_w_in":
            return a.reshape(a.shape[0], -1, LANES)
        return a

    big = [("cv_w_pw2", 0, "cv_w_pw2"), ("gdn_w_in", 0, "gdn_w_in"), ("gdn_w_out", 0, "gdn_w_out"),
           ("mlp_w1", 0, "mlp_w1_0"), ("mlp_w1", 1, "mlp_w1_1"), ("mlp_w2", 0, "mlp_w2_0"), ("mlp_w2", 1, "mlp_w2_1"),
           ("cv_w_pw1", 0, "cv_w_pw1")]
    res = {n: {} for n in MATMUL_SHARDED}
    after = grad_x
    for entry in sent[:-2]:
        after = finish(entry, after)
    for n, layer, key in big[:-1]:
        res[n][layer] = _sum8_adamw(recv[key], as3d(n, w_in[n]), as3d(n, m_in[n]), as3d(n, v_in[n]), layer, f"adamw_{key}")
        after = res[n][layer][0]
    after = finish(sent[-2], after)
    finish(sent[-1], after)
    n, layer, key = big[-1]
    res[n][layer] = _sum8_adamw(recv[key], as3d(n, w_in[n]), as3d(n, m_in[n]), as3d(n, v_in[n]), layer, f"adamw_{key}")

    grads = dict(zip(small_early, _unpack(_sum_devices(recv["small"], "grads_small_sum"), small_info["sizes"], small_info["shapes"])))
    grads["norm_mix_g"] = _sum_devices(recv["norm_mix_g"], "grads_norm_mix_sum").reshape(nm_shape)
    loss = grads["loss"].reshape(())
    for n in CONV_SHARDED:
        cn = shards[n].shape[-1]
        grads[n] = lax.dynamic_slice_in_dim(grads[n], me * cn, cn, axis=1)

    out_groups = {n: [] for n in WEIGHTS}
    for n in MATMUL_SHARDED:
        layers = sorted(res[n])
        for k in range(4):
            pieces = [res[n][layer][k] for layer in layers]
            out_groups[n].append(jnp.stack(pieces).reshape(w_in[n].shape))

    sm_w = [shards[n] for n in small]
    sm_g = [grads[n].reshape(shards[n].shape) for n in small]
    sm_m = [_squeeze_layer(n, m_in[n]) for n in small]
    sm_v = [_squeeze_layer(n, v_in[n]) for n in small]
    wp, psz = _pack(sm_w)
    gp, _ = _pack(sm_g)
    mp, _ = _pack(sm_m)
    vp, _ = _pack(sm_v)
    dp, nmp, nvp = _adamw(wp, gp, mp, vp)
    shp = [a.shape for a in sm_w]
    for n, g, dl, nm, nv in zip(small, sm_g, _unpack(dp, psz, shp), _unpack(nmp, psz, shp), _unpack(nvp, psz, shp)):
        out_groups[n] = [a.reshape(w_in[n].shape) for a in (g, dl, nm, nv)]

    outs = [loss, grad_x]
    for k in range(4):
        outs += [out_groups[n][k] for n in WEIGHTS]
    return tuple(outs)
```

```python
import functools

import jax
import jax.numpy as jnp
from jax import lax
from jax.experimental import pallas as pl
from jax.experimental.pallas import tpu as pltpu

f32, bf16 = jnp.float32, jnp.bfloat16

NORM_EPS = 1e-6
L2_EPS = 1e-6
CHUNK = 64
LANES = 128
SUBLANES = 8
N_DEV = 8
VMEM_LIMIT = 56 * 1024 * 1024
CONV_PAD = 32
HEADS_PER_STEP = 8
PREP_HEADS_PER_STEP = 4
NEG = -1e30

ADAM_LR, ADAM_B1, ADAM_B2, ADAM_EPS, ADAM_WD, ADAM_STEP = 0.001, 0.9, 0.999, 1e-08, 0.01, 10

NT = (((1,), (1,)), ((), ()))
TN = (((0,), (0,)), ((), ()))
HI = lax.Precision.HIGHEST


def _pc(body, *, name, grid, in_specs, out_specs, out_shape, scratch=(), sem=None):
    return pl.pallas_call(
        body, name=name, grid=grid, in_specs=in_specs, out_specs=out_specs, out_shape=out_shape,
        scratch_shapes=list(scratch),
        compiler_params=pltpu.CompilerParams(dimension_semantics=sem, vmem_limit_bytes=VMEM_LIMIT))


def _rows(tm, n):
    return pl.BlockSpec((tm, n), lambda i: (i, 0))


def _const(shape):
    return pl.BlockSpec(shape, lambda *_: (0,) * len(shape))


def _resident(shape):
    return pl.BlockSpec(shape, lambda *_: (0,) * len(shape), pipeline_mode=pl.Buffered(1))


def _tile(t, pref):
    return pref if t % pref == 0 else t


def _dot(a, b):
    return jnp.dot(a.astype(bf16), b.astype(bf16), preferred_element_type=f32)


def _dot_nt(a, b):
    return lax.dot_general(a.astype(bf16), b.astype(bf16), NT, preferred_element_type=f32)


def _dot_tn(a, b):
    return lax.dot_general(a.astype(bf16), b.astype(bf16), TN, preferred_element_type=f32)


def _sigmoid(x):
    return 1.0 / (1.0 + jnp.exp(-x))


def _silu_grad(x):
    s = _sigmoid(x)
    return s * (1.0 + x * (1.0 - s))


def _rms(x, g):
    rstd = lax.rsqrt(jnp.mean(x * x, axis=-1, keepdims=True) + NORM_EPS)
    xh = x * rstd
    return xh * g, xh, rstd


def _rms_bwd(dn, xh, rstd, g):
    dxh = dn * g
    return rstd * (dxh - xh * jnp.mean(dxh * xh, axis=-1, keepdims=True))


def _acc_init(step, *refs):
    @pl.when(step == 0)
    def _():
        for r in refs:
            r[...] = jnp.zeros(r.shape, r.dtype)


def _acc_rows(ref, val):
    ref[0:1, :] += jnp.sum(val, axis=0, keepdims=True)


def _pw1_glu(x, g, w, b):
    t, d = x.shape
    tm = _tile(t, 256)

    nb_w = w.shape[0]

    def body(x_ref, g_ref, w_hbm, b_ref, n_ref, u_ref, gl_ref, w_ref, sems):
        _fetch_blocks(pl.program_id(0), w_hbm, w_ref, sems, True)
        n, _, _ = _rms(x_ref[...], g_ref[...])
        nb = n.astype(bf16)
        n_ref[...] = nb
        u = jnp.dot(nb, w_ref[...], preferred_element_type=f32) + b_ref[...]
        u_ref[...] = u.astype(bf16)
        gl_ref[...] = u[:, :d] * _sigmoid(u[:, d:])

    return _pc(body, name="pw1_glu", grid=(t // tm,),
               in_specs=[_rows(tm, d), _const((1, d)), ANY, _const((1, 2 * d))],
               out_specs=[_rows(tm, d), _rows(tm, 2 * d), _rows(tm, d)],
               out_shape=[jax.ShapeDtypeStruct((t, d), bf16), jax.ShapeDtypeStruct((t, 2 * d), bf16),
                          jax.ShapeDtypeStruct((t, d), f32)],
               scratch=[pltpu.VMEM((d, 2 * d), bf16), pltpu.SemaphoreType.DMA((nb_w,))],
               sem=("arbitrary",))(x, g, w, b)


def _ln_silu_mm_res(dc, ln_g, ln_b, w, b, res):
    t, d = dc.shape
    tm = _tile(t, 256)

    def body(x_ref, g_ref, bb_ref, w_ref, b_ref, r_ref, s_ref, o_ref):
        x = x_ref[...]
        xc = x - jnp.mean(x, axis=-1, keepdims=True)
        rstd = lax.rsqrt(jnp.mean(xc * xc, axis=-1, keepdims=True) + NORM_EPS)
        ln = xc * rstd * g_ref[...] + bb_ref[...]
        sb = (ln * _sigmoid(ln)).astype(bf16)
        s_ref[...] = sb
        o_ref[...] = r_ref[...] + jnp.dot(sb, w_ref[...], preferred_element_type=f32) + b_ref[...]

    return _pc(body, name="ln_silu_pw2", grid=(t // tm,),
               in_specs=[_rows(tm, d), _const((1, d)), _const((1, d)), _resident((d, d)), _const((1, d)), _rows(tm, d)],
               out_specs=[_rows(tm, d), _rows(tm, d)],
               out_shape=[jax.ShapeDtypeStruct((t, d), bf16), jax.ShapeDtypeStruct((t, d), f32)],
               sem=("parallel",))(dc, ln_g, ln_b, w, b, res)


def _fetch_blocks(step, w_hbm, dst, sems, by_cols, layer=None):
    nb_w = w_hbm.shape[0]
    step_rows, step_cols = w_hbm.shape[-2], w_hbm.shape[-1]

    @pl.when(step == 0)
    def _():
        copies = []
        for j in range(nb_w):
            src = w_hbm.at[j] if layer is None else w_hbm.at[j, layer]
            if by_cols:
                part = dst.at[:, pl.ds(j * step_cols, step_cols)]
            else:
                part = dst.at[pl.ds(j * step_rows, step_rows), :]
            copies.append(pltpu.make_async_copy(src, part, sems.at[j]))
        for cp in copies:
            cp.start()
        for cp in copies:
            cp.wait()


def _mlp_fwd(h, g, w1g, w2g, name, loss=None):
    t, d = h.shape
    nb_w, _, bn = w1g.shape
    ff = nb_w * bn
    tm = _tile(t, 256)
    n_in = 4 if loss is None else 6
    n_out = 4 if loss is None else 6

    def body(*refs):
        h_ref, g_ref, w1_hbm, w2_hbm = refs[:4]
        n_ref, f_ref, r_ref = refs[n_in:n_in + 3]
        w1_ref, w2_ref, sem1, sem2 = refs[n_in + n_out:]
        _fetch_blocks(pl.program_id(0), w1_hbm, w1_ref, sem1, True)
        _fetch_blocks(pl.program_id(0), w2_hbm, w2_ref, sem2, False)
        hv = h_ref[...]
        n, _, _ = _rms(hv, g_ref[...])
        nb = n.astype(bf16)
        n_ref[...] = nb
        f = jnp.dot(nb, w1_ref[...], preferred_element_type=f32)
        f_ref[...] = f.astype(bf16)
        rb = jnp.square(jnp.maximum(f, 0.0)).astype(bf16)
        r_ref[...] = rb
        out = hv + jnp.dot(rb, w2_ref[...], preferred_element_type=f32)
        if loss is None:
            refs[n_in + 3][...] = out
        else:
            gf_ref, t_ref = refs[4:6]
            loss_ref, dh_ref, dg_ref = refs[n_in + 3:n_in + 6]
            _acc_init(pl.program_id(0), loss_ref, dg_ref)
            gv = gf_ref[...]
            y, xh, rstd = _rms(out, gv)
            e = y - t_ref[...]
            loss_ref[...] += 0.5 * jnp.sum(jnp.mean(e * e, axis=-1, keepdims=True))
            dy = e * (1.0 / d)
            _acc_rows(dg_ref, dy * xh)
            dh_ref[...] = _rms_bwd(dy, xh, rstd, gv)

    in_specs = [_rows(tm, d), _const((1, d)), ANY, ANY]
    out_specs = [_rows(tm, d), _rows(tm, ff), _rows(tm, ff)]
    out_shape = [jax.ShapeDtypeStruct((t, d), bf16), jax.ShapeDtypeStruct((t, ff), bf16), jax.ShapeDtypeStruct((t, ff), bf16)]
    args = [h, g, w1g, w2g]
    if loss is None:
        out_specs.append(_rows(tm, d))
        out_shape.append(jax.ShapeDtypeStruct((t, d), f32))
    else:
        in_specs += [_const((1, d)), _rows(tm, d)]
        args += list(loss)
        out_specs += [_const((SUBLANES, LANES)), _rows(tm, d), _const((SUBLANES, d))]
        out_shape += [jax.ShapeDtypeStruct((SUBLANES, LANES), f32), jax.ShapeDtypeStruct((t, d), f32),
                      jax.ShapeDtypeStruct((SUBLANES, d), f32)]
    return _pc(body, name=name, grid=(t // tm,), in_specs=in_specs, out_specs=out_specs, out_shape=out_shape,
               scratch=[pltpu.VMEM((d, ff), bf16), pltpu.VMEM((ff, d), bf16),
                        pltpu.SemaphoreType.DMA((nb_w,)), pltpu.SemaphoreType.DMA((nb_w,))],
               sem=("arbitrary",))(*args)


def _softplus(x):
    return jnp.maximum(x, 0.0) + jnp.log(1.0 + jnp.exp(-jnp.abs(x)))


def _gdn_in(h, g, w_main, w_ab, a_log_pad, dt_pad, n_heads):
    t, d = h.shape
    tm = _tile(t, 256)
    gate_rows = 2 * n_heads

    def body(h_ref, g_ref, wm_ref, wab_ref, al_ref, dt_ref, n_ref, qkv_ref, z_ref, ab_ref, gb_ref):
        n, _, _ = _rms(h_ref[...], g_ref[...])
        nb = n.astype(bf16)
        n_ref[...] = nb
        p = jnp.dot(nb, wm_ref[...], preferred_element_type=f32)
        qkv_ref[...] = p[:, :3 * d]
        z_ref[...] = p[:, 3 * d:]
        ab = jnp.dot(nb, wab_ref[...], preferred_element_type=f32)
        ab_ref[...] = ab
        lane = lax.broadcasted_iota(jnp.int32, ab.shape, 1)
        decay = -jnp.exp(al_ref[...]) * _softplus(ab + dt_ref[...])
        gates = jnp.where(lane < n_heads, decay, jnp.where(lane < gate_rows, _sigmoid(ab), 0.0))
        gb_ref[...] = gates.T[:gate_rows, :]

    return _pc(body, name="gdn_in", grid=(t // tm,),
               in_specs=[_rows(tm, d), _const((1, d)), _resident((d, 4 * d)), _resident((d, LANES)),
                         _const((1, LANES)), _const((1, LANES))],
               out_specs=[_rows(tm, d), _rows(tm, 3 * d), _rows(tm, d), _rows(tm, LANES),
                          pl.BlockSpec((gate_rows, tm), lambda i: (0, i))],
               out_shape=[jax.ShapeDtypeStruct((t, d), bf16), jax.ShapeDtypeStruct((t, 3 * d), f32),
                          jax.ShapeDtypeStruct((t, d), f32), jax.ShapeDtypeStruct((t, LANES), f32),
                          jax.ShapeDtypeStruct((gate_rows, t), f32)],
               sem=("parallel",))(h, g, w_main, w_ab, a_log_pad, dt_pad)


def _gated_norm_mm_res(o, z, ng, w, res, n_heads):
    t, d = o.shape
    tm = _tile(t, 256)

    def body(o_ref, z_ref, ng_ref, w_ref, r_ref, on_ref, out_ref):
        for hd in range(n_heads):
            sl = slice(hd * LANES, (hd + 1) * LANES)
            rn, _, _ = _rms(o_ref[:, sl], ng_ref[...])
            zz = z_ref[:, sl]
            on_ref[:, sl] = (rn * (zz * _sigmoid(zz))).astype(bf16)
        out_ref[...] = r_ref[...] + jnp.dot(on_ref[...], w_ref[...], preferred_element_type=f32)

    return _pc(body, name="gated_norm_wout", grid=(t // tm,),
               in_specs=[_rows(tm, d), _rows(tm, d), _const((1, LANES)), _resident((d, d)), _rows(tm, d)],
               out_specs=[_rows(tm, d), _rows(tm, d)],
               out_shape=[jax.ShapeDtypeStruct((t, d), bf16), jax.ShapeDtypeStruct((t, d), f32)],
               sem=("parallel",))(o, z, ng, w, res)


def _mlp_bwd(dho, h, g, fb, w1g, w2g, name):
    t, d = h.shape
    nb_w, _, bn = w1g.shape
    ff = nb_w * bn
    tm = _tile(t, 256)

    def body(do_ref, h_ref, g_ref, f_ref, w1_hbm, w2_hbm, df_ref, dh_ref, dg_ref, cs_ref, w1_ref, w2_ref, sem1, sem2):
        _fetch_blocks(pl.program_id(0), w1_hbm, w1_ref, sem1, True)
        _fetch_blocks(pl.program_id(0), w2_hbm, w2_ref, sem2, False)
        _acc_init(pl.program_id(0), dg_ref, cs_ref)
        do = do_ref[...]
        dr = lax.dot_general(do.astype(bf16), w2_ref[...], NT, preferred_element_type=f32)
        dfb = (dr * (2.0 * jnp.maximum(f_ref[...].astype(f32), 0.0))).astype(bf16)
        df_ref[...] = dfb
        dn = lax.dot_general(dfb, w1_ref[...], NT, preferred_element_type=f32)
        gv = g_ref[...]
        _, xh, rstd = _rms(h_ref[...], gv)
        _acc_rows(dg_ref, dn * xh)
        dh = do + _rms_bwd(dn, xh, rstd, gv)
        dh_ref[...] = dh
        _acc_rows(cs_ref, dh)

    return _pc(body, name=name, grid=(t // tm,),
               in_specs=[_rows(tm, d), _rows(tm, d), _const((1, d)), _rows(tm, ff), ANY, ANY],
               out_specs=[_rows(tm, ff), _rows(tm, d), _const((SUBLANES, d)), _const((SUBLANES, d))],
               out_shape=[jax.ShapeDtypeStruct((t, ff), bf16), jax.ShapeDtypeStruct((t, d), f32),
                          jax.ShapeDtypeStruct((SUBLANES, d), f32), jax.ShapeDtypeStruct((SUBLANES, d), f32)],
               scratch=[pltpu.VMEM((d, ff), bf16), pltpu.VMEM((ff, d), bf16),
                        pltpu.SemaphoreType.DMA((nb_w,)), pltpu.SemaphoreType.DMA((nb_w,))],
               sem=("arbitrary",))(dho, h, g, fb, w1g, w2g)


class _Tail:
    def __init__(self, fn, ins, outs):
        self.fn, self.ins, self.outs = fn, ins, outs


def _mm_nt(pairs, name, tail=None):
    t = pairs[0][0].shape[0]
    tm = _tile(t, 256)
    npair = len(pairs)
    in_specs, args, scratch, blocked = [], [], [], []
    k = None
    for dy, w in pairs:
        nn = dy.shape[1]
        if isinstance(w, tuple) and isinstance(w[0], str):
            w = w[1]
            k = w.shape[1]
            wspec = ANY
            blocked.append(True)
            scratch += [pltpu.VMEM((k, nn), bf16), pltpu.SemaphoreType.DMA((w.shape[0],))]
        elif isinstance(w, tuple):
            w, idx = w
            k = w.shape[0]
            wspec = pl.BlockSpec((k, nn), lambda *_, idx=idx: (0, idx), pipeline_mode=pl.Buffered(1))
            blocked.append(False)
        else:
            k = w.shape[0]
            wspec = _resident(w.shape)
            blocked.append(False)
        in_specs += [_rows(tm, nn), wspec]
        args += [dy, w]
    n_tin = len(tail.ins) if tail else 0
    n_out = len(tail.outs) if tail else 1
    if tail:
        for arr, kind in tail.ins:
            in_specs.append(_rows(tm, arr.shape[1]) if kind == "rows" else _const(arr.shape))
            args.append(arr)
        out_specs = [_rows(tm, c) if kind == "rows" else _const((SUBLANES, c)) for c, kind in tail.outs]
        out_shape = [jax.ShapeDtypeStruct((t, c) if kind == "rows" else (SUBLANES, c), f32) for c, kind in tail.outs]
    else:
        out_specs = _rows(tm, k)
        out_shape = jax.ShapeDtypeStruct((t, k), f32)

    def body(*refs):
        step = pl.program_id(0)
        tin = refs[2 * npair:2 * npair + n_tin]
        outs = refs[2 * npair + n_tin:2 * npair + n_tin + n_out]
        scr = list(refs[2 * npair + n_tin + n_out:])
        acc = None
        for p in range(npair):
            w_ref = refs[2 * p + 1]
            if blocked[p]:
                w_vmem, sems = scr.pop(0), scr.pop(0)
                _fetch_blocks(step, w_ref, w_vmem, sems, True)
                w_ref = w_vmem
            part = lax.dot_general(refs[2 * p][...].astype(bf16), w_ref[...], NT, preferred_element_type=f32)
            acc = part if acc is None else acc + part
        if tail is None:
            outs[0][...] = acc
        else:
            _acc_init(step, *[o for o, (_, kind) in zip(outs, tail.outs) if kind == "acc"])
            tail.fn(acc, tin, outs)

    sequential = tail is not None or any(blocked)
    return _pc(body, name=name, grid=(t // tm,), in_specs=in_specs, out_specs=out_specs, out_shape=out_shape,
               scratch=scratch, sem=("arbitrary",) if sequential else ("parallel",))(*args)


def _rms_bwd_tail(h, g, dres):
    def fn(dn, ins, outs):
        h_ref, g_ref, dr_ref = ins
        dh_ref, dg_ref = outs
        gv = g_ref[...]
        _, xh, rstd = _rms(h_ref[...], gv)
        _acc_rows(dg_ref, dn * xh)
        dh_ref[...] = dr_ref[...] + _rms_bwd(dn, xh, rstd, gv)

    d = h.shape[1]
    return _Tail(fn, [(h, "rows"), (g, "const"), (dres, "rows")], [(d, "rows"), (d, "acc")])


def _ln_silu_bwd_tail(dc, ln_g, ln_b):
    def fn(ds, ins, outs):
        x_ref, g_ref, b_ref = ins
        dx_ref, dg_ref, db_ref, cs_ref = outs
        x = x_ref[...]
        gv = g_ref[...]
        xc = x - jnp.mean(x, axis=-1, keepdims=True)
        rstd = lax.rsqrt(jnp.mean(xc * xc, axis=-1, keepdims=True) + NORM_EPS)
        xh = xc * rstd
        dln = ds * _silu_grad(xh * gv + b_ref[...])
        _acc_rows(dg_ref, dln * xh)
        _acc_rows(db_ref, dln)
        dxh = dln * gv
        dx = rstd * (dxh - jnp.mean(dxh, axis=-1, keepdims=True) - xh * jnp.mean(dxh * xh, axis=-1, keepdims=True))
        dx_ref[...] = dx
        _acc_rows(cs_ref, dx)

    d = dc.shape[1]
    return _Tail(fn, [(dc, "rows"), (ln_g, "const"), (ln_b, "const")], [(d, "rows"), (d, "acc"), (d, "acc"), (d, "acc")])


def _gated_norm_bwd_tail(o, z, ng, n_heads):
    def fn(don_all, ins, outs):
        o_ref, z_ref, ng_ref = ins
        do_ref, dz_ref, dng_ref = outs
        gv = ng_ref[...]
        for hd in range(n_heads):
            sl = slice(hd * LANES, (hd + 1) * LANES)
            rn, xh, rstd = _rms(o_ref[:, sl], gv)
            zz = z_ref[:, sl]
            don = don_all[:, sl]
            dz_ref[:, sl] = don * rn * _silu_grad(zz)
            drn = don * (zz * _sigmoid(zz))
            _acc_rows(dng_ref, drn * xh)
            do_ref[:, sl] = _rms_bwd(drn, xh, rstd, gv)

    d = o.shape[1]
    return _Tail(fn, [(o, "rows"), (z, "rows"), (ng, "const")], [(d, "rows"), (d, "rows"), (LANES, "acc")])


def _mm_tn_blocked(x, dy, name, out_dtype=f32):
    t, k = x.shape
    bn = dy.shape[1] // N_DEV
    tm = _tile(t, 512)
    jb = N_DEV
    while jb > 1 and k * jb * bn * 4 > 8 * 1024 * 1024:
        jb //= 2
    nt = t // tm

    def body(x_ref, dy_ref, o_ref, *acc):
        acc_ref = acc[0] if acc else o_ref
        _acc_init(pl.program_id(1), acc_ref)
        xt = x_ref[...].astype(bf16).T
        for jj in range(jb):
            acc_ref[jj] += jnp.dot(xt, dy_ref[:, jj * bn:(jj + 1) * bn].astype(bf16), preferred_element_type=f32)
        if acc:
            @pl.when(pl.program_id(1) == nt - 1)
            def _():
                o_ref[...] = acc_ref[...].astype(out_dtype)

    return _pc(body, name=name, grid=(N_DEV // jb, nt),
               in_specs=[pl.BlockSpec((tm, k), lambda j, i: (i, 0)), pl.BlockSpec((tm, jb * bn), lambda j, i: (i, j))],
               out_specs=pl.BlockSpec((jb, k, bn), lambda j, i: (j, 0, 0)),
               out_shape=jax.ShapeDtypeStruct((N_DEV, k, bn), out_dtype),
               scratch=[] if out_dtype == f32 else [pltpu.VMEM((jb, k, bn), f32)],
               sem=("parallel", "arbitrary"))(x, dy)


def _mm_tn(x, dy, name, out_dtype=f32):
    t, k = x.shape
    n = dy.shape[1]
    tm = _tile(t, 512)
    cap = max(LANES, (2 * 1024 * 1024) // k)
    tn = n
    if n > cap:
        tn = max(c for c in range(LANES, cap + 1, LANES) if n % c == 0)
    nt = t // tm

    def body(x_ref, dy_ref, o_ref, *acc):
        acc_ref = acc[0] if acc else o_ref
        _acc_init(pl.program_id(1), acc_ref)
        acc_ref[...] += lax.dot_general(x_ref[...].astype(bf16), dy_ref[...].astype(bf16), TN, preferred_element_type=f32)
        if acc:
            @pl.when(pl.program_id(1) == nt - 1)
            def _():
                o_ref[...] = acc_ref[...].astype(out_dtype)

    return _pc(body, name=name, grid=(n // tn, nt),
               in_specs=[pl.BlockSpec((tm, k), lambda j, i: (i, 0)), pl.BlockSpec((tm, tn), lambda j, i: (i, j))],
               out_specs=pl.BlockSpec((k, tn), lambda j, i: (0, j)),
               out_shape=jax.ShapeDtypeStruct((k, n), out_dtype),
               scratch=[] if out_dtype == f32 else [pltpu.VMEM((k, tn), f32)],
               sem=("parallel", "arbitrary"))(x, dy)


def _gates_bwd(dgb_t, ab, a_log_pad, dt_pad, n_heads):
    t = ab.shape[0]
    tm = _tile(t, 256)
    gate_rows = 2 * n_heads

    def body(dgb_ref, ab_ref, al_ref, dt_ref, dab_ref, dal_ref, ddt_ref):
        _acc_init(pl.program_id(0), dal_ref, ddt_ref)
        ab = ab_ref[...]
        dgb = jnp.concatenate([dgb_ref[...], jnp.zeros((LANES - gate_rows, tm), f32)], axis=0).T
        lane = lax.broadcasted_iota(jnp.int32, ab.shape, 1)
        is_a = lane < n_heads
        is_b = jnp.logical_and(lane >= n_heads, lane < 2 * n_heads)
        xa = ab + dt_ref[...]
        neg_a = -jnp.exp(al_ref[...])
        dg_da = neg_a * _sigmoid(xa)
        beta = _sigmoid(ab)
        da = jnp.where(is_a, dgb * dg_da, 0.0)
        dab_ref[...] = da + jnp.where(is_b, dgb * beta * (1.0 - beta), 0.0)
        _acc_rows(dal_ref, jnp.where(is_a, dgb * neg_a * _softplus(xa), 0.0))
        _acc_rows(ddt_ref, da)

    return _pc(body, name="gates_bwd", grid=(t // tm,),
               in_specs=[pl.BlockSpec((gate_rows, tm), lambda i: (0, i)), _rows(tm, LANES), _const((1, LANES)), _const((1, LANES))],
               out_specs=[_rows(tm, LANES), _const((SUBLANES, LANES)), _const((SUBLANES, LANES))],
               out_shape=[jax.ShapeDtypeStruct((t, LANES), f32), jax.ShapeDtypeStruct((SUBLANES, LANES), f32),
                          jax.ShapeDtypeStruct((SUBLANES, LANES), f32)],
               sem=("arbitrary",))(dgb_t, ab, a_log_pad, dt_pad)


def _glu_bwd(dgl, ub):
    t, d = dgl.shape
    tm = _tile(t, 256)

    def body(dgl_ref, u_ref, du_ref, cs_ref):
        _acc_init(pl.program_id(0), cs_ref)
        dgl = dgl_ref[...]
        a = u_ref[:, :d].astype(f32)
        sb = _sigmoid(u_ref[:, d:].astype(f32))
        da = dgl * sb
        db = dgl * a * sb * (1.0 - sb)
        du_ref[:, :d] = da.astype(bf16)
        du_ref[:, d:] = db.astype(bf16)
        cs_ref[0:1, :d] += jnp.sum(da, axis=0, keepdims=True)
        cs_ref[0:1, d:] += jnp.sum(db, axis=0, keepdims=True)

    return _pc(body, name="glu_bwd", grid=(t // tm,),
               in_specs=[_rows(tm, d), _rows(tm, 2 * d)],
               out_specs=[_rows(tm, 2 * d), _const((SUBLANES, 2 * d))],
               out_shape=[jax.ShapeDtypeStruct((t, 2 * d), bf16), jax.ShapeDtypeStruct((SUBLANES, 2 * d), f32)],
               sem=("arbitrary",))(dgl, ub)


def _conv_rows(s):
    return 256 if s % 256 == 0 else s


def _conv_tap_sum(pad_ref, w_ref, base, rows, width):
    acc = jnp.zeros((rows, LANES), f32)
    for j in range(width):
        acc = acc + w_ref[j:j + 1, :] * pad_ref[pl.ds(base + CONV_PAD - (width - 1) + j, rows), :]
    return acc


def _l2_silu_post(c, j, n_heads, scale):
    a = c * _sigmoid(c)
    r = lax.rsqrt(jnp.sum(a * a, axis=-1, keepdims=True) + L2_EPS)
    mult = jnp.where(j < n_heads, r * scale, jnp.where(j < 2 * n_heads, r, 1.0))
    return a, r, a * mult


def _dwconv_fwd(x, w, b, name, qk_heads=None):
    bl, s, cn = x.shape
    width = w.shape[0]
    rows = _conv_rows(s)
    scale = float(LANES) ** -0.5

    def body(x_ref, w_ref, b_ref, o_ref, pad_ref):
        j = pl.program_id(1)
        pad_ref[0:CONV_PAD, :] = jnp.zeros((CONV_PAD, LANES), f32)
        pad_ref[CONV_PAD:, :] = x_ref[0]

        def step(i, carry):
            base = pl.multiple_of(i * rows, rows)
            acc = _conv_tap_sum(pad_ref, w_ref, base, rows, width)
            if qk_heads is None:
                acc = acc + b_ref[...]
            else:
                _, _, acc = _l2_silu_post(acc, j, qk_heads, scale)
            o_ref[0, pl.ds(base, rows), :] = acc
            return carry

        lax.fori_loop(0, s // rows, step, 0)

    return _pc(body, name=name, grid=(bl, cn // LANES),
               in_specs=[pl.BlockSpec((1, s, LANES), lambda bi, j: (bi, 0, j)),
                         pl.BlockSpec((width, LANES), lambda bi, j: (0, j)),
                         pl.BlockSpec((1, LANES), lambda bi, j: (0, j))],
               out_specs=pl.BlockSpec((1, s, LANES), lambda bi, j: (bi, 0, j)),
               out_shape=jax.ShapeDtypeStruct((bl, s, cn), f32),
               scratch=[pltpu.VMEM((s + CONV_PAD, LANES), f32)],
               sem=("parallel", "parallel"))(x, w, b)


def _dwconv_bwd(x, dys, w, name, qk_heads=None):
    bl, s, cn = x.shape
    width = w.shape[0]
    wp = -(-width // SUBLANES) * SUBLANES
    rows = _conv_rows(s)
    scale = float(LANES) ** -0.5
    nblk = s // rows
    ndy = len(dys)

    def body(*refs):
        x_ref, w_ref = refs[0], refs[1]
        dy_refs = refs[2:2 + ndy]
        dx_ref, dw_ref, db_ref, xpad, dypad, acc = refs[2 + ndy:]
        j = pl.program_id(0)
        bi = pl.program_id(1)
        _acc_init(bi, acc, db_ref)
        xpad[0:CONV_PAD, :] = jnp.zeros((CONV_PAD, LANES), f32)
        xpad[CONV_PAD:, :] = x_ref[0]
        dypad[s:, :] = jnp.zeros((CONV_PAD, LANES), f32)
        if qk_heads is None:
            dypad[0:s, :] = dy_refs[0][0]
        else:
            def pre(i, carry):
                base = pl.multiple_of(i * rows, rows)
                c = _conv_tap_sum(xpad, w_ref, base, rows, width)
                a, r, _ = _l2_silu_post(c, j, qk_heads, scale)
                dq = dy_refs[0][0, pl.ds(base, rows), :]
                dk = dy_refs[1][0, pl.ds(base, rows), :]
                dv = dy_refs[2][0, pl.ds(base, rows), :]
                dy = jnp.where(j < qk_heads, dq * scale, jnp.where(j < 2 * qk_heads, dk, dv))
                da_l2 = r * (dy - a * (r * r) * jnp.sum(a * dy, axis=-1, keepdims=True))
                da = jnp.where(j < 2 * qk_heads, da_l2, dy)
                dypad[pl.ds(base, rows), :] = da * _silu_grad(c)
                return carry

            lax.fori_loop(0, nblk, pre, 0)

        def step(i, carry):
            base = pl.multiple_of(i * rows, rows)
            dxa = jnp.zeros((rows, LANES), f32)
            for jj in range(width):
                dxa = dxa + w_ref[jj:jj + 1, :] * dypad[pl.ds(base + (width - 1) - jj, rows), :]
            dx_ref[0, pl.ds(base, rows), :] = dxa
            dyc = dypad[pl.ds(base, rows), :]
            db_ref[...] += dyc.reshape(rows // SUBLANES, SUBLANES, LANES).sum(axis=0)
            for jj in range(width):
                prod = dyc * xpad[pl.ds(base + CONV_PAD - (width - 1) + jj, rows), :]
                acc[jj * SUBLANES:(jj + 1) * SUBLANES, :] += prod.reshape(rows // SUBLANES, SUBLANES, LANES).sum(axis=0)
            return carry

        lax.fori_loop(0, nblk, step, 0)

        @pl.when(bi == bl - 1)
        def _():
            dw_ref[...] = jnp.zeros((wp, LANES), f32)
            for jj in range(width):
                dw_ref[jj:jj + 1, :] = jnp.sum(acc[jj * SUBLANES:(jj + 1) * SUBLANES, :], axis=0, keepdims=True)

    if qk_heads is None:
        dy_specs = [pl.BlockSpec((1, s, LANES), lambda j, bi: (bi, 0, j))]
    else:
        hh = qk_heads
        def dy_spec(part):
            def index(j, bi):
                mine = jnp.logical_and(j >= part * hh, j < (part + 1) * hh)
                return (jnp.where(mine, bi * hh + j - part * hh, 0), 0, 0)
            return pl.BlockSpec((1, s, LANES), index)

        dy_specs = [dy_spec(0), dy_spec(1), dy_spec(2)]
    return _pc(body, name=name, grid=(cn // LANES, bl),
               in_specs=[pl.BlockSpec((1, s, LANES), lambda j, bi: (bi, 0, j)),
                         pl.BlockSpec((width, LANES), lambda j, bi: (0, j))] + dy_specs,
               out_specs=[pl.BlockSpec((1, s, LANES), lambda j, bi: (bi, 0, j)),
                          pl.BlockSpec((wp, LANES), lambda j, bi: (0, j)),
                          pl.BlockSpec((SUBLANES, LANES), lambda j, bi: (0, j))],
               out_shape=[jax.ShapeDtypeStruct((bl, s, cn), f32), jax.ShapeDtypeStruct((wp, cn), f32),
                          jax.ShapeDtypeStruct((SUBLANES, cn), f32)],
               scratch=[pltpu.VMEM((s + CONV_PAD, LANES), f32), pltpu.VMEM((s + CONV_PAD, LANES), f32),
                        pltpu.VMEM((width * SUBLANES, LANES), f32)],
               sem=("parallel", "arbitrary"))(x, w, *dys)


def _group_rows(s):
    for rows in (256, 128):
        if s % rows == 0:
            return rows
    return CHUNK


def _group_masks(rows):
    r = lax.broadcasted_iota(jnp.int32, (rows, rows), 0)
    c = lax.broadcasted_iota(jnp.int32, (rows, rows), 1)
    return r, c, r >= c


def _decay(gc_col, gc_row, causal):
    return jnp.exp(jnp.where(causal, gc_col - gc_row, NEG))


def _inv_unit_lower_many(mats, r, c):
    n = r.shape[0]
    eye = (r == c).astype(f32)
    same16 = (r >> 4) == (c >> 4)
    ads = [jnp.where(same16, a, 0.0) for a in mats]
    aos = [a - ad for a, ad in zip(mats, ads)]
    xs = ads
    tds = [-x for x in xs]
    for _ in range(3):
        xs = [_dot(x, x) for x in xs]
        tds = [td + x + _dot(td, x) for td, x in zip(tds, xs)]
    bs = [ao + _dot(td, ao) for td, ao in zip(tds, aos)]
    ps = [-b for b in bs]
    span = 2
    while span < n // 16:
        bs = [_dot(b, b) for b in bs]
        ps = [p + b + _dot(p, b) for p, b in zip(ps, bs)]
        span *= 2
    return [p + td + _dot(p, td) for p, td in zip(ps, tds)]


def _lane_cumsum(x, y, reverse, name):
    rr, n = x.shape

    def body(x_ref, y_ref, o_ref):
        i = lax.broadcasted_iota(jnp.int32, (n, n), 0)
        j = lax.broadcasted_iota(jnp.int32, (n, n), 1)
        tri = ((i >= j) if reverse else (i <= j)).astype(f32)
        o_ref[...] = jnp.dot(x_ref[...] + y_ref[...], tri, precision=HI, preferred_element_type=f32)

    spec = pl.BlockSpec((rr, n), lambda: (0, 0))
    return pl.pallas_call(body, name=name, in_specs=[spec, spec], out_specs=spec,
                          out_shape=jax.ShapeDtypeStruct((rr, n), f32))(x, y)


def _gdn_specs(n_heads, nblk, rows, hp=1, rev=False):
    def blk(n):
        return nblk - 1 - n if rev else n

    def qkv(off):
        return pl.BlockSpec((rows, hp * LANES), lambda g, n: (
            lax.div(g * hp, n_heads) * nblk + blk(n), lax.div(off * n_heads + lax.rem(g * hp, n_heads), hp)))

    def per_head(last, block_rows=rows):
        return pl.BlockSpec((hp, block_rows, last), lambda g, n: (g, blk(n), 0))

    def row_vec():
        return pl.BlockSpec((hp, 1, 1, rows), lambda g, n: (g, blk(n), 0, 0))

    return qkv, per_head, row_vec


def _gdn_prep(qkv, cols, grow, bl, s, n_heads):
    rows = _group_rows(s)
    nblk = s // rows
    bh_n = bl * n_heads
    hp = PREP_HEADS_PER_STEP if n_heads % PREP_HEADS_PER_STEP == 0 else 1
    qkv_spec, ph, rv = _gdn_specs(n_heads, nblk, rows, hp)

    def body(k_ref, v_ref, cols_ref, grow_ref, u_ref, w_ref, t_ref):
        r, c, causal = _group_masks(rows)
        mats, rhs = [], []
        for h in range(hp):
            hs = slice(h * LANES, (h + 1) * LANES)
            k = k_ref[:, hs]
            gc = cols_ref[h, :, 0:1]
            beta = cols_ref[h, :, 1:2]
            kb = k * beta
            mats.append(jnp.where(r > c, _dot_nt(kb, k) * _decay(gc, grow_ref[h, 0], causal), 0.0))
            rhs.append((v_ref[:, hs] * beta, kb * jnp.exp(gc)))
        for h, tm in enumerate(_inv_unit_lower_many(mats, r, c)):
            tb = tm.astype(bf16)
            u_ref[h] = rhs[h][0] + jnp.dot(tb, rhs[h][0].astype(bf16), preferred_element_type=f32)
            w_ref[h] = rhs[h][1] + jnp.dot(tb, rhs[h][1].astype(bf16), preferred_element_type=f32)
            t_ref[h] = tb

    return _pc(body, name="gdn_prep", grid=(bh_n // hp, nblk),
               in_specs=[qkv_spec(1), qkv_spec(2), ph(2), rv()],
               out_specs=[ph(LANES), ph(LANES), ph(rows)],
               out_shape=[jax.ShapeDtypeStruct((bh_n, s, LANES), f32), jax.ShapeDtypeStruct((bh_n, s, LANES), f32),
                          jax.ShapeDtypeStruct((bh_n, s, rows), bf16)],
               sem=("parallel", "parallel"))(qkv, qkv, cols, grow)


def _gdn_scan(qkv, u, w, cols, grow, bl, s, n_heads):
    rows = _group_rows(s)
    nblk = s // rows
    bh_n = bl * n_heads
    d = n_heads * LANES
    hp = HEADS_PER_STEP if n_heads % HEADS_PER_STEP == 0 else 1
    qkv_spec, ph, rv = _gdn_specs(n_heads, nblk, rows, hp)

    def body(q_ref, k_ref, u_ref, w_ref, cols_ref, grow_ref, o_ref, vn_ref, ss_ref, s_scr):
        _acc_init(pl.program_id(1), s_scr)
        _, _, causal = _group_masks(rows)
        hh = range(hp)
        qs = [q_ref[:, h * LANES:(h + 1) * LANES] for h in hh]
        ks = [k_ref[:, h * LANES:(h + 1) * LANES] for h in hh]
        gcs = [cols_ref[h, :, 0:1] for h in hh]
        ps = [_dot_nt(qs[h], ks[h]) * _decay(gcs[h], grow_ref[h, 0], causal) for h in hh]
        sts = [s_scr[h] for h in hh]
        for h in hh:
            ss_ref[h] = sts[h]
        vns = [u_ref[h] - _dot(w_ref[h], sts[h]) for h in hh]
        for h in hh:
            vn_ref[h] = vns[h]
        o_state = [_dot(qs[h] * jnp.exp(gcs[h]), sts[h]) for h in hh]
        o_intra = [_dot(ps[h], vns[h]) for h in hh]
        for h in hh:
            o_ref[:, h * LANES:(h + 1) * LANES] = o_state[h] + o_intra[h]
        for h in hh:
            g_last = gcs[h][rows - 1:rows, :]
            s_scr[h] = jnp.exp(g_last) * sts[h] + _dot_tn(ks[h] * jnp.exp(g_last - gcs[h]), vns[h])

    return _pc(body, name="gdn_scan", grid=(bh_n // hp, nblk),
               in_specs=[qkv_spec(0), qkv_spec(1), ph(LANES), ph(LANES), ph(2), rv()],
               out_specs=[qkv_spec(0), ph(LANES), ph(LANES, block_rows=LANES)],
               out_shape=[jax.ShapeDtypeStruct((bl * s, d), f32), jax.ShapeDtypeStruct((bh_n, s, LANES), f32),
                          jax.ShapeDtypeStruct((bh_n, nblk * LANES, LANES), f32)],
               scratch=[pltpu.VMEM((hp, LANES, LANES), f32)],
               sem=("parallel", "arbitrary"))(qkv, qkv, u, w, cols, grow)


def _gdn_scan_bwd(do, qkv, w, vn, cols, grow, ss, bl, s, n_heads):
    rows = _group_rows(s)
    nblk = s // rows
    bh_n = bl * n_heads
    hp = HEADS_PER_STEP if n_heads % HEADS_PER_STEP == 0 else 1
    qkv_spec, ph, rv = _gdn_specs(n_heads, nblk, rows, hp, rev=True)

    def body(do_ref, q_ref, k_ref, w_ref, vn_ref, cols_ref, grow_ref, ss_ref,
             du_ref, dw_ref, dq_ref, dk_ref, dcol_ref, drow_ref, ds_scr):
        _acc_init(pl.program_id(1), ds_scr)
        _, _, causal = _group_masks(rows)
        last_row = lax.broadcasted_iota(jnp.int32, (rows, 1), 0) == rows - 1
        hh = range(hp)
        dos = [do_ref[:, h * LANES:(h + 1) * LANES] for h in hh]
        qs = [q_ref[:, h * LANES:(h + 1) * LANES] for h in hh]
        ks = [k_ref[:, h * LANES:(h + 1) * LANES] for h in hh]
        vns = [vn_ref[h] for h in hh]
        gcs = [cols_ref[h, :, 0:1] for h in hh]
        sts = [ss_ref[h] for h in hh]
        dss = [ds_scr[h] for h in hh]
        dmats = [_decay(gcs[h], grow_ref[h, 0], causal) for h in hh]
        gams = [jnp.exp(gc) for gc in gcs]
        qgs = [qs[h] * gams[h] for h in hh]
        g_lasts = [gc[rows - 1:rows, :] for gc in gcs]
        kd_scales = [jnp.exp(g_lasts[h] - gcs[h]) for h in hh]
        kdecs = [ks[h] * kd_scales[h] for h in hh]
        qks = [_dot_nt(qs[h], ks[h]) for h in hh]
        dpds = [_dot_nt(dos[h], vns[h]) * dmats[h] for h in hh]
        dvns = [_dot_tn(qks[h] * dmats[h], dos[h]) + _dot(kdecs[h], dss[h]) for h in hh]
        for h in hh:
            du_ref[h] = dvns[h]
        dkdecs = [_dot_nt(vns[h], dss[h]) for h in hh]
        for h in hh:
            dw_ref[h] = -_dot_nt(dvns[h], sts[h])
        dqgs = [_dot_nt(dos[h], sts[h]) for h in hh]
        dq_intra = [_dot(dpds[h], ks[h]) for h in hh]
        dk_intra = [_dot_tn(dpds[h], qs[h]) for h in hh]
        for h in hh:
            dq_ref[h] = dqgs[h] * gams[h] + dq_intra[h]
            dk_ref[h] = dk_intra[h] + dkdecs[h] * kd_scales[h]
            ep = dpds[h] * qks[h]
            drow_ref[h, 0] = -jnp.sum(ep, axis=0, keepdims=True)
            kd_rows = jnp.sum(dkdecs[h] * kdecs[h], axis=-1, keepdims=True)
            extra = jnp.sum(kd_rows) + jnp.exp(g_lasts[h]) * jnp.sum(sts[h] * dss[h])
            dcol_ref[h] = (jnp.sum(dqgs[h] * qgs[h], axis=-1, keepdims=True) + jnp.sum(ep, axis=-1, keepdims=True)
                           - kd_rows + jnp.where(last_row, extra, 0.0))
        ds_new = [jnp.exp(g_lasts[h]) * dss[h] + _dot_tn(qgs[h], dos[h]) - _dot_tn(w_ref[h], dvns[h]) for h in hh]
        for h in hh:
            ds_scr[h] = ds_new[h]

    return _pc(body, name="gdn_scan_bwd", grid=(bh_n // hp, nblk),
               in_specs=[qkv_spec(0), qkv_spec(0), qkv_spec(1), ph(LANES), ph(LANES), ph(2), rv(),
                         ph(LANES, block_rows=LANES)],
               out_specs=[ph(LANES), ph(LANES), ph(LANES), ph(LANES), ph(1), rv()],
               out_shape=[jax.ShapeDtypeStruct((bh_n, s, LANES), f32)] * 4
               + [jax.ShapeDtypeStruct((bh_n, s, 1), f32), jax.ShapeDtypeStruct((bh_n, nblk, 1, rows), f32)],
               scratch=[pltpu.VMEM((hp, LANES, LANES), f32)],
               sem=("parallel", "arbitrary"))(do, qkv, qkv, w, vn, cols, grow, ss)


def _gdn_prep_bwd(qkv, cols, grow, tmat, du, dw, dk_scan, dcol_scan, drow_scan, bl, s, n_heads):
    rows = _group_rows(s)
    nblk = s // rows
    bh_n = bl * n_heads
    hp = PREP_HEADS_PER_STEP if n_heads % PREP_HEADS_PER_STEP == 0 else 1
    qkv_spec, ph, rv = _gdn_specs(n_heads, nblk, rows, hp)

    def body(k_ref, v_ref, cols_ref, grow_ref, t_ref, du_ref, dw_ref, dks_ref, dcs_ref, drs_ref,
             dk_ref, dv_ref, dcols_ref, drow_ref):
        r, c, causal = _group_masks(rows)
        hh = range(hp)
        ks = [k_ref[:, h * LANES:(h + 1) * LANES] for h in hh]
        vs = [v_ref[:, h * LANES:(h + 1) * LANES] for h in hh]
        gcs = [cols_ref[h, :, 0:1] for h in hh]
        betas = [cols_ref[h, :, 1:2] for h in hh]
        tms = [t_ref[h] for h in hh]
        dus = [du_ref[h] for h in hh]
        dws = [dw_ref[h] for h in hh]
        gams = [jnp.exp(gc) for gc in gcs]
        kbs = [k * b for k, b in zip(ks, betas)]
        kbgs = [kb * g for kb, g in zip(kbs, gams)]
        dts = [_dot_nt(dus[h], vs[h] * betas[h]) + _dot_nt(dws[h], kbgs[h]) for h in hh]
        dvbs = [dus[h] + _dot_tn(tms[h], dus[h]) for h in hh]
        dkbgs = [dws[h] + _dot_tn(tms[h], dws[h]) for h in hh]
        kks = [_dot_nt(kbs[h], ks[h]) for h in hh]
        inner = [dts[h] + _dot_nt(dts[h], tms[h]) for h in hh]
        dads = [jnp.where(r > c, -(inner[h] + _dot_tn(tms[h], inner[h])), 0.0) * _decay(gcs[h], grow_ref[h, 0], causal)
                for h in hh]
        dkbs = [dkbgs[h] * gams[h] + _dot(dads[h], ks[h]) for h in hh]
        dk2 = [_dot_tn(dads[h], kbs[h]) for h in hh]
        for h in hh:
            dk_ref[h] = dks_ref[h] + dk2[h] + dkbs[h] * betas[h]
            dv_ref[h] = dvbs[h] * betas[h]
            ea = dads[h] * kks[h]
            dcols_ref[h, :, 0:1] = (dcs_ref[h] + jnp.sum(dkbgs[h] * kbgs[h], axis=-1, keepdims=True)
                                    + jnp.sum(ea, axis=-1, keepdims=True))
            dcols_ref[h, :, 1:2] = (jnp.sum(dvbs[h] * vs[h], axis=-1, keepdims=True)
                                    + jnp.sum(dkbs[h] * ks[h], axis=-1, keepdims=True))
            drow_ref[h, 0] = drs_ref[h, 0] - jnp.sum(ea, axis=0, keepdims=True)

    return _pc(body, name="gdn_prep_bwd", grid=(bh_n // hp, nblk),
               in_specs=[qkv_spec(1), qkv_spec(2), ph(2), rv(), ph(rows), ph(LANES), ph(LANES), ph(LANES), ph(1), rv()],
               out_specs=[ph(LANES), ph(LANES), ph(2), rv()],
               out_shape=[jax.ShapeDtypeStruct((bh_n, s, LANES), f32), jax.ShapeDtypeStruct((bh_n, s, LANES), f32),
                          jax.ShapeDtypeStruct((bh_n, s, 2), f32), jax.ShapeDtypeStruct((bh_n, nblk, 1, rows), f32)],
               sem=("parallel", "parallel"))(qkv, qkv, cols, grow, tmat, du, dw, dk_scan, dcol_scan, drow_scan)


def _row(v):
    return v.reshape(1, -1).astype(f32)


def _pad_lanes(v):
    v = v.reshape(1, -1).astype(f32)
    return jnp.pad(v, ((0, 0), (0, LANES - v.shape[1])))


def _local_step(x, tgt, p, need, emit):
    bl, s, d = x.shape
    t = bl * s
    n_heads = p["gdn_a_log"].shape[-1]
    assert d == n_heads * LANES and s % CHUNK == 0
    x2 = x.reshape(t, d)
    tgt2 = tgt.reshape(t, d)
    gr = {}

    n0, ub, gl = _pw1_glu(x2, _row(p["norm_mix_g"][0]), p["cv_w_pw1"], _row(p["cv_b_pw1"]))
    cv = need("conv", n0)
    dc = _dwconv_fwd(gl.reshape(bl, s, d), cv["cv_w_dw"], _row(p["cv_b_dw"]), "dwconv_fwd").reshape(t, d)
    sb, h1 = _ln_silu_mm_res(dc, _row(p["cv_ln_g"]), _row(p["cv_ln_b"]), cv["cv_w_pw2"], _row(p["cv_b_pw2"]), x2)
    m0 = need("mlp0", h1)
    n1, f0, r0, h2 = _mlp_fwd(h1, _row(p["norm_ffn_g"][0]), m0["w1"], m0["w2"], "mlp_fwd0")

    gd = need("gdn", h2)
    w_in = gd["w_in"]
    w_ab = jnp.pad(w_in[:, 4 * d:], ((0, 0), (0, LANES - 2 * n_heads)))
    a_log_pad = _pad_lanes(p["gdn_a_log"])
    dt_pad = _pad_lanes(p["gdn_dt_bias"])
    n2, qkv_pre, z, ab, gbeta = _gdn_in(h2, _row(p["norm_mix_g"][1]), w_in, w_ab, a_log_pad, dt_pad, n_heads)
    zero_bias = jnp.zeros((1, 3 * d), f32)
    qkv = _dwconv_fwd(qkv_pre.reshape(bl, s, 3 * d), cv["gdn_conv_w"], zero_bias, "sconv_fwd", qk_heads=n_heads).reshape(t, 3 * d)
    bh_n, rows = bl * n_heads, _group_rows(s)
    gates = gbeta.reshape(2, n_heads, bl, s).transpose(0, 2, 1, 3).reshape(2, bh_n * (s // rows), rows)
    gc_lanes = _lane_cumsum(gates[0], jnp.zeros_like(gates[0]), False, "gdn_gate_cumsum")
    grow = gc_lanes.reshape(bh_n, s // rows, 1, rows)
    cols = jnp.stack([gc_lanes.reshape(bh_n, s), gates[1].reshape(bh_n, s)], axis=-1)
    u, w, tmat = _gdn_prep(qkv, cols, grow, bl, s, n_heads)
    o, vn, ss = _gdn_scan(qkv, u, w, cols, grow, bl, s, n_heads)
    onb, h3 = _gated_norm_mm_res(o, z, _row(p["gdn_norm_g"]), gd["w_out"], h2, n_heads)
    m1 = need("mlp1", h3)
    n3, f1, r1, loss_acc, dh4, dgf = _mlp_fwd(h3, _row(p["norm_ffn_g"][1]), m1["w1"], m1["w2"], "mlp_fwd1_loss",
                                              loss=(_row(p["final_norm_g"]), tgt2))
    loss = loss_acc[0, 0]
    gr["loss"] = loss_acc[0, 0:1]
    gr["final_norm_g"] = dgf[0]

    df1, dh3, dg_ffn1, _ = _mlp_bwd(dh4, h3, _row(p["norm_ffn_g"][1]), f1, m1["w1"], m1["w2"], "mlp_bwd1")
    dw2_1 = _mm_tn(r1, dh4, "dw_mlp2_1", bf16)
    dw1_1 = _mm_tn_blocked(n3, df1, "dw_mlp1_1", bf16)

    dw_out = _mm_tn(onb, dh3, "dw_gdn_out", bf16)
    tie = emit("late", {"mlp_w2_1": dw2_1, "mlp_w1_1": dw1_1, "gdn_w_out": dw_out})
    do, dz, dng = _mm_nt([(dh3, gd["w_out"])], "dx_gdn_out",
                         _gated_norm_bwd_tail(o, z, _row(p["gdn_norm_g"]) + tie, n_heads))
    gr["gdn_norm_g"] = dng[0]
    du, dw_, dq, dk_scan, dcol_scan, drow_scan = _gdn_scan_bwd(do, qkv, w, vn, cols, grow, ss, bl, s, n_heads)
    dk, dv, dcols, drow = _gdn_prep_bwd(qkv, cols, grow, tmat, du, dw_, dk_scan, dcol_scan, drow_scan, bl, s, n_heads)
    dg_lanes = _lane_cumsum(dcols[..., 0].reshape(gc_lanes.shape), drow.reshape(gc_lanes.shape), True, "gdn_gate_cumsum_bwd")
    dqkv_pre, dconv_w, _ = _dwconv_bwd(qkv_pre.reshape(bl, s, 3 * d), [dq, dk, dv], cv["gdn_conv_w"], "sconv_bwd", qk_heads=n_heads)
    gr["gdn_conv_w"] = dconv_w[:cv["gdn_conv_w"].shape[0]]
    dgb_t = jnp.stack([dg_lanes.reshape(bl, n_heads, s), dcols[..., 1].reshape(bl, n_heads, s)])
    dgb_t = dgb_t.transpose(0, 2, 1, 3).reshape(2 * n_heads, t)
    dab, dal, ddt = _gates_bwd(dgb_t, ab, a_log_pad, dt_pad, n_heads)
    gr["gdn_a_log"] = dal[0, :n_heads]
    gr["gdn_dt_bias"] = ddt[0, :n_heads]
    dqkv2 = dqkv_pre.reshape(t, 3 * d)
    dw_in = jnp.concatenate(
        [_mm_tn(n2, dqkv2, "dw_gdn_in_qkv"), _mm_tn(n2, dz, "dw_gdn_in_z"), _mm_tn(n2, dab, "dw_gdn_in_ab")[:, :2 * n_heads]], axis=1)
    tie = emit("gdn_in", {"gdn_w_in": dw_in})
    dh2, dg_mix1 = _mm_nt([(dqkv2, (w_in, 0)), (dz, (w_in, 3)), (dab, w_ab)], "dx_gdn_in",
                          _rms_bwd_tail(h2, _row(p["norm_mix_g"][1]) + tie, dh3))

    df0, dh1, dg_ffn0, cs_h1 = _mlp_bwd(dh2, h1, _row(p["norm_ffn_g"][0]), f0, m0["w1"], m0["w2"], "mlp_bwd0")
    dw2_0 = _mm_tn(r0, dh2, "dw_mlp2_0", bf16)
    dw1_0 = _mm_tn_blocked(n1, df0, "dw_mlp1_0", bf16)
    dw_pw2 = _mm_tn(sb, dh1, "dw_pw2", bf16)
    tie = emit("mlp0", {"mlp_w2_0": dw2_0, "mlp_w1_0": dw1_0, "cv_w_pw2": dw_pw2})
    gr["norm_ffn_g"] = jnp.stack([dg_ffn0[0], dg_ffn1[0]])

    gr["cv_b_pw2"] = cs_h1[0]
    ddc, dlng, dlnb, cs_dc = _mm_nt([(dh1, cv["cv_w_pw2"])], "dx_pw2",
                                    _ln_silu_bwd_tail(dc, _row(p["cv_ln_g"]) + tie, _row(p["cv_ln_b"])))
    gr["cv_ln_g"] = dlng[0]
    gr["cv_ln_b"] = dlnb[0]
    gr["cv_b_dw"] = cs_dc[0]
    dgl, dw_dw, _ = _dwconv_bwd(gl.reshape(bl, s, d), [ddc.reshape(bl, s, d)], cv["cv_w_dw"], "dwconv_bwd")
    gr["cv_w_dw"] = dw_dw[:cv["cv_w_dw"].shape[0]]
    dub, cs_u = _glu_bwd(dgl.reshape(t, d), ub)
    gr["cv_b_pw1"] = cs_u[0]
    dw_pw1 = _mm_tn_blocked(n0, dub, "dw_pw1", bf16)
    tie = emit("last", {"cv_w_pw1": dw_pw1, "small": gr})
    dx, dg_mix0 = _mm_nt([(dub, ("blocks", p["cv_w_pw1"]))], "dx_pw1",
                         _rms_bwd_tail(x2, _row(p["norm_mix_g"][0]) + tie, dh1))
    return loss, dx.reshape(bl, s, d), {"norm_mix_g": jnp.stack([dg_mix0[0], dg_mix1[0]])}


ANY = pl.BlockSpec(memory_space=pl.ANY)
MESH = pl.DeviceIdType.MESH


def _flip(v, bit):
    return 1 - v if bit else v


def _all_gather_many(shards):
    na = len(shards)

    def body(*refs):
        x_refs, o_refs = refs[:na], refs[na:2 * na]
        send_sems, recv_sems, local_sems = refs[2 * na:]
        x, y, c = lax.axis_index("x"), lax.axis_index("y"), lax.axis_index("c")
        me, sibling = (x, y, c), (x, y, 1 - c)
        chips = [(1 - x, y), (x, 1 - y), (1 - x, 1 - y)]

        def copy(a, k, block, to, src=None):
            px, py, pc = block
            dst = o_refs[a].at[4 * px + 2 * py + pc]
            return pltpu.make_async_remote_copy(
                src_ref=dst if src is None else src, dst_ref=dst,
                send_sem=send_sems.at[a, k], recv_sem=recv_sems.at[a, k], device_id=to, device_id_type=MESH)

        mine = [pltpu.make_async_copy(x_refs[a], o_refs[a].at[4 * x + 2 * y + c], local_sems.at[a]) for a in range(na)]
        first = []
        for a in range(na):
            first.append(copy(a, 0, me, sibling, src=x_refs[a]))
            first += [copy(a, 1 + j, me, (*chip, c), src=x_refs[a]) for j, chip in enumerate(chips)]
        for cp in mine + first:
            cp.start()
        passed = []
        for j, chip in enumerate(chips):
            for a in range(na):
                copy(a, 1 + j, (*chip, c), me).wait_recv()
                fwd = copy(a, 4 + j, (*chip, c), sibling)
                fwd.start()
                passed.append(fwd)
        for a in range(na):
            copy(a, 0, sibling, me).wait_recv()
        for j, chip in enumerate(chips):
            for a in range(na):
                copy(a, 4 + j, (*chip, 1 - c), me).wait_recv()
        for cp in first + passed:
            cp.wait_send()
        for cp in mine:
            cp.wait()

    return pl.pallas_call(
        body, name="weights_all_gather",
        out_shape=[jax.ShapeDtypeStruct((N_DEV,) + a.shape, a.dtype) for a in shards],
        in_specs=[ANY] * na, out_specs=[ANY] * na,
        scratch_shapes=[pltpu.SemaphoreType.DMA((na, 7)), pltpu.SemaphoreType.DMA((na, 7)), pltpu.SemaphoreType.DMA((na,))],
        compiler_params=pltpu.CompilerParams(has_side_effects=True),
    )(*shards)


HBM = pl.BlockSpec(memory_space=pltpu.HBM)
SEM = pl.BlockSpec(memory_space=pltpu.SEMAPHORE)
EFFECT = pltpu.SideEffectType.DATAFLOW_SIDE_EFFECTING
N_PEERS = N_DEV - 1


def _exchange_copies(src_refs, land_refs, send_sems, recv_sems, scatter):
    x, y, c = lax.axis_index("x"), lax.axis_index("y"), lax.axis_index("c")
    me = 4 * x + 2 * y + c
    copies = []
    for a, (src, land) in enumerate(zip(src_refs, land_refs)):
        for k in range(1, N_DEV):
            px, py, pc = _flip(x, k & 4), _flip(y, k & 2), _flip(c, k & 1)
            i = a * N_PEERS + k - 1
            copies.append(pltpu.make_async_remote_copy(
                src_ref=src.at[4 * px + 2 * py + pc] if scatter[a] else src, dst_ref=land.at[me],
                send_sem=send_sems.at[i], recv_sem=recv_sems.at[i], device_id=(px, py, pc), device_id_type=MESH))
    return copies


def _exchange_start(srcs, scatter, name):
    na = len(srcs)
    lands = [lax.empty(s.shape if sc else (N_DEV,) + s.shape, s.dtype) for s, sc in zip(srcs, scatter)]

    def body(*refs):
        copies = _exchange_copies(refs[:na], refs[na:2 * na], refs[2 * na], refs[2 * na + 1], scatter)
        for cp in copies:
            cp.start()
        token = refs[-1]
        token[...] = jnp.zeros_like(token)

    outs = pl.pallas_call(
        body, name=name,
        out_shape=(pltpu.SemaphoreType.DMA((na * N_PEERS,)), pltpu.SemaphoreType.DMA((na * N_PEERS,)))
        + tuple(pltpu.HBM(a.shape, a.dtype) for a in srcs + lands) + (jax.ShapeDtypeStruct((SUBLANES, LANES), f32),),
        in_specs=[HBM] * (2 * na),
        out_specs=(SEM, SEM) + (HBM,) * (2 * na) + (pl.BlockSpec(memory_space=pltpu.VMEM),),
        input_output_aliases={i: 2 + i for i in range(2 * na)},
        compiler_params=pltpu.CompilerParams(has_side_effects=EFFECT),
    )(*[pltpu.with_memory_space_constraint(a, pltpu.HBM) for a in srcs + lands])
    return outs[0], outs[1], list(outs[2:2 + na]), list(outs[2 + na:2 + 2 * na]), outs[-1]


def _exchange_wait(started, after, scatter, name):
    send_sems, recv_sems, srcs, lands, _ = started
    na = len(srcs)

    def body(*refs):
        for cp in _exchange_copies(refs[:na], refs[na:2 * na], refs[2 * na], refs[2 * na + 1], scatter):
            cp.wait_send()
            cp.wait_recv()

    outs = pl.pallas_call(
        body, name=name,
        out_shape=tuple(pltpu.HBM(a.shape, a.dtype) for a in srcs + lands),
        in_specs=[HBM] * (2 * na) + [SEM, SEM, ANY], out_specs=(HBM,) * (2 * na),
        input_output_aliases={i: i for i in range(2 * na)},
        compiler_params=pltpu.CompilerParams(has_side_effects=EFFECT),
    )(*srcs, *lands, send_sems, recv_sems, after)
    return list(outs[:na]), list(outs[na:])


def _own_block(land, block, me):
    return lax.dynamic_update_index_in_dim(land, block, me, 0)


def _sum8_adamw(r2, w, m, v, layer, name):
    _, rr, cc = r2.shape
    tr = _tile(rr, 256)
    bc1 = 1.0 - ADAM_B1 ** ADAM_STEP
    bc2 = 1.0 - ADAM_B2 ** ADAM_STEP

    def body(r_ref, w_ref, m_ref, v_ref, g_ref, d_ref, nm_ref, nv_ref):
        gv = r_ref[0].astype(f32)
        for q in range(1, N_DEV):
            gv = gv + r_ref[q].astype(f32)
        g_ref[...] = gv
        nm = ADAM_B1 * m_ref[...] + (1.0 - ADAM_B1) * gv
        nv = ADAM_B2 * v_ref[...] + (1.0 - ADAM_B2) * (gv * gv)
        nm_ref[...] = nm
        nv_ref[...] = nv
        d_ref[...] = -ADAM_LR * ((nm / bc1) / (jnp.sqrt(nv / bc2) + ADAM_EPS) + ADAM_WD * w_ref[...])

    lspec = pl.BlockSpec((None, tr, cc), lambda i: (layer, i, 0))
    return _pc(body, name=name, grid=(rr // tr,),
               in_specs=[pl.BlockSpec((N_DEV, tr, cc), lambda i: (0, i, 0)), lspec, lspec, lspec],
               out_specs=[_rows(tr, cc)] * 4, out_shape=[jax.ShapeDtypeStruct((rr, cc), f32)] * 4,
               sem=("parallel",))(r2, w, m, v)


def _sum_devices(recv, name):
    _, rr, _ = recv.shape
    tr = _tile(rr, 512)

    def body(r_ref, o_ref):
        acc = r_ref[0]
        for i in range(1, N_DEV):
            acc = acc + r_ref[i]
        o_ref[...] = acc

    return _pc(body, name=name, grid=(rr // tr,),
               in_specs=[pl.BlockSpec((N_DEV, tr, LANES), lambda i: (0, i, 0))],
               out_specs=_rows(tr, LANES), out_shape=jax.ShapeDtypeStruct((rr, LANES), f32), sem=("parallel",))(recv)


def _adamw(w, g, m, v):
    rr = w.shape[0]
    tr = _tile(rr, 512)
    bc1 = 1.0 - ADAM_B1 ** ADAM_STEP
    bc2 = 1.0 - ADAM_B2 ** ADAM_STEP

    def body(w_ref, g_ref, m_ref, v_ref, d_ref, nm_ref, nv_ref):
        gv = g_ref[...]
        nm = ADAM_B1 * m_ref[...] + (1.0 - ADAM_B1) * gv
        nv = ADAM_B2 * v_ref[...] + (1.0 - ADAM_B2) * (gv * gv)
        nm_ref[...] = nm
        nv_ref[...] = nv
        d_ref[...] = -ADAM_LR * ((nm / bc1) / (jnp.sqrt(nv / bc2) + ADAM_EPS) + ADAM_WD * w_ref[...])

    spec = _rows(tr, LANES)
    return _pc(body, name="adamw", grid=(rr // tr,), in_specs=[spec] * 4, out_specs=[spec] * 3,
               out_shape=[jax.ShapeDtypeStruct((rr, LANES), f32)] * 3, sem=("parallel",))(w, g, m, v)


PACK_ROWS = 512
PART_ROWS = SUBLANES


def _pack(arrs):
    parts, sizes = [], []
    for a in arrs:
        flat = a.reshape(-1)
        n = flat.shape[0]
        rows = -(-n // (LANES * PART_ROWS)) * PART_ROWS
        if rows * LANES != n:
            flat = jnp.pad(flat, (0, rows * LANES - n))
        parts.append(flat.reshape(rows, LANES))
        sizes.append((rows, n))
    total = sum(r for r, _ in sizes)
    padded = -(-total // PACK_ROWS) * PACK_ROWS
    if padded > total:
        parts.append(jnp.zeros((padded - total, LANES), parts[0].dtype))
    return jnp.concatenate(parts, axis=0), sizes


def _unpack(packed, sizes, shapes):
    out, off = [], 0
    for (rows, n), shp in zip(sizes, shapes):
        piece = lax.slice_in_dim(packed, off, off + rows, axis=0)
        if rows * LANES != n:
            piece = lax.slice_in_dim(piece.reshape(-1), 0, n, axis=0)
        out.append(piece.reshape(tuple(shp)))
        off += rows
    return out


def _cols_to_blocks(a):
    n = a.shape[-1] // N_DEV
    a = a.reshape(a.shape[:-1] + (N_DEV, n))
    return jnp.moveaxis(a, -2, 0)


def _blocks_to_cols(a):
    a = jnp.moveaxis(a, 0, -2)
    return a.reshape(a.shape[:-2] + (a.shape[-2] * a.shape[-1],))


def _rows_to_blocks(a):
    k = a.shape[-2] // N_DEV
    a = a.reshape(a.shape[:-2] + (N_DEV, k, a.shape[-1]))
    return jnp.moveaxis(a, -3, 0)


def _blocks_to_rows(a):
    a = jnp.moveaxis(a, 0, -3)
    return a.reshape(a.shape[:-3] + (a.shape[-3] * a.shape[-2], a.shape[-1]))


COL_SHARDED = ("cv_w_pw1", "gdn_w_in", "mlp_w1")
ROW_SHARDED = ("cv_w_pw2", "gdn_w_out", "mlp_w2")
CONV_SHARDED = ("cv_w_dw", "gdn_conv_w")
REPLICATED = ("norm_mix_g", "norm_ffn_g", "final_norm_g", "cv_b_pw1", "cv_b_dw", "cv_ln_g", "cv_ln_b", "cv_b_pw2",
              "gdn_a_log", "gdn_dt_bias", "gdn_norm_g")
WEIGHTS = ("norm_mix_g", "norm_ffn_g", "final_norm_g", "cv_w_pw1", "cv_b_pw1", "cv_w_dw", "cv_b_dw", "cv_ln_g",
           "cv_ln_b", "cv_w_pw2", "cv_b_pw2", "gdn_w_in", "gdn_conv_w", "gdn_a_log", "gdn_dt_bias", "gdn_norm_g",
           "gdn_w_out", "mlp_w1", "mlp_w2")
MATMUL_SHARDED = COL_SHARDED + ROW_SHARDED


def _squeeze_layer(name, a):
    if name in ("norm_mix_g", "norm_ffn_g", "final_norm_g", "mlp_w1", "mlp_w2"):
        return a
    return a[0]


def _gather_weights(shards):
    me = 4 * lax.axis_index("x") + 2 * lax.axis_index("y") + lax.axis_index("c")
    pw1 = _all_gather_many([shards["cv_w_pw1"].astype(bf16)])[0]
    now = {"cv_w_pw1": pw1}

    def cast(a, tie):
        return (a + tie).astype(bf16)

    token, _ = lax.optimization_barrier((jnp.zeros((), f32), pw1))
    later, started = {}, {}
    for group in ("conv", "mlp0", "gdn", "mlp1"):
        if group == "conv":
            srcs = [cast(shards["cv_w_pw2"], token)] + [shards[n] + token for n in CONV_SHARDED]
        elif group == "gdn":
            srcs = [cast(shards["gdn_w_in"], token).reshape(-1, LANES), cast(shards["gdn_w_out"], token)]
        else:
            layer = int(group[-1])
            srcs = [cast(shards["mlp_w1"][layer], token), cast(shards["mlp_w2"][layer], token)]
        later[group] = srcs
        started[group] = _exchange_start(srcs, [False] * len(srcs), f"weights_{group}_start")
        token = started[group][4][0, 0]

    def need(group, after):
        srcs, lands = _exchange_wait(started[group], after, [False] * len(later[group]), f"weights_{group}_wait")
        lands = [_own_block(ld, own, me) for ld, own in zip(lands, srcs)]
        if group == "conv":
            out = {"cv_w_pw2": _blocks_to_rows(lands[0])}
            out.update({n: _blocks_to_cols(ld) for n, ld in zip(CONV_SHARDED, lands[1:])})
            return out
        if group == "gdn":
            w_in = _blocks_to_cols(lands[0].reshape((N_DEV,) + shards["gdn_w_in"].shape))
            return {"w_in": w_in, "w_out": _blocks_to_rows(lands[1])}
        return {"w1": lands[0], "w2": lands[1]}

    return now, need, token


def kernel(x, norm_mix_g, norm_ffn_g, final_norm_g, cv_w_pw1, cv_b_pw1, cv_w_dw, cv_b_dw, cv_ln_g, cv_ln_b, cv_w_pw2, cv_b_pw2, gdn_w_in, gdn_conv_w, gdn_a_log, gdn_dt_bias, gdn_norm_g, gdn_w_out, mlp_w1, mlp_w2, loss_target, m_norm_mix_g, m_norm_ffn_g, m_final_norm_g, m_cv_w_pw1, m_cv_b_pw1, m_cv_w_dw, m_cv_b_dw, m_cv_ln_g, m_cv_ln_b, m_cv_w_pw2, m_cv_b_pw2, m_gdn_w_in, m_gdn_conv_w, m_gdn_a_log, m_gdn_dt_bias, m_gdn_norm_g, m_gdn_w_out, m_mlp_w1, m_mlp_w2, v_norm_mix_g, v_norm_ffn_g, v_final_norm_g, v_cv_w_pw1, v_cv_b_pw1, v_cv_w_dw, v_cv_b_dw, v_cv_ln_g, v_cv_ln_b, v_cv_w_pw2, v_cv_b_pw2, v_gdn_w_in, v_gdn_conv_w, v_gdn_a_log, v_gdn_dt_bias, v_gdn_norm_g, v_gdn_w_out, v_mlp_w1, v_mlp_w2):
    w_in = dict(zip(WEIGHTS, (norm_mix_g, norm_ffn_g, final_norm_g, cv_w_pw1, cv_b_pw1, cv_w_dw, cv_b_dw, cv_ln_g, cv_ln_b, cv_w_pw2, cv_b_pw2, gdn_w_in, gdn_conv_w, gdn_a_log, gdn_dt_bias, gdn_norm_g, gdn_w_out, mlp_w1, mlp_w2)))
    m_in = dict(zip(WEIGHTS, (m_norm_mix_g, m_norm_ffn_g, m_final_norm_g, m_cv_w_pw1, m_cv_b_pw1, m_cv_w_dw, m_cv_b_dw, m_cv_ln_g, m_cv_ln_b, m_cv_w_pw2, m_cv_b_pw2, m_gdn_w_in, m_gdn_conv_w, m_gdn_a_log, m_gdn_dt_bias, m_gdn_norm_g, m_gdn_w_out, m_mlp_w1, m_mlp_w2)))
    v_in = dict(zip(WEIGHTS, (v_norm_mix_g, v_norm_ffn_g, v_final_norm_g, v_cv_w_pw1, v_cv_b_pw1, v_cv_w_dw, v_cv_b_dw, v_cv_ln_g, v_cv_ln_b, v_cv_w_pw2, v_cv_b_pw2, v_gdn_w_in, v_gdn_conv_w, v_gdn_a_log, v_gdn_dt_bias, v_gdn_norm_g, v_gdn_w_out, v_mlp_w1, v_mlp_w2)))
    me = 4 * lax.axis_index("x") + 2 * lax.axis_index("y") + lax.axis_index("c")

    shards = {n: _squeeze_layer(n, w_in[n]) for n in WEIGHTS}
    first, need, token = _gather_weights(shards)
    params = {n: shards[n] for n in REPLICATED}
    params.update(first)
    params["norm_mix_g"] = params["norm_mix_g"] + token

    def row_blocks(a):
        return a.reshape(N_DEV, a.shape[0] // N_DEV, a.shape[1])

    def flat_blocks(a):
        k, n8 = a.shape
        return _cols_to_blocks(a).reshape(N_DEV, k * (n8 // N_DEV) // LANES, LANES)

    def as_blocks(n, a):
        if n == "gdn_w_in":
            return flat_blocks(a).astype(bf16)
        return a if a.ndim == 3 else row_blocks(a)

    sent = []

    small = REPLICATED + CONV_SHARDED
    small_early = tuple(n for n in small if n != "norm_mix_g") + ("loss",)
    small_info = {}

    def emit(group, grads_out):
        names, blocks, scatter = list(grads_out), [], []
        for n in names:
            if n == "small":
                packed, small_info["sizes"] = _pack([grads_out[n][k] for k in small_early])
                small_info["shapes"] = [grads_out[n][k].shape for k in small_early]
                blocks.append(packed)
                scatter.append(False)
            else:
                blocks.append(as_blocks(n, grads_out[n]))
                scatter.append(True)
        sent.append((names, _exchange_start(blocks, scatter, f"grads_{group}_start"), scatter))
        return sent[-1][1][4][0, 0]

    _, grad_x, gr = _local_step(x, loss_target, params, need, emit)
    nm_shape = gr["norm_mix_g"].shape
    sent.append((["norm_mix_g"], _exchange_start([gr["norm_mix_g"].reshape(-1, LANES)], [False], "grads_norm_mix_start"), [False]))

    recv = {}

    def finish(entry, after):
        names, st, scatter = entry
        srcs, lands = _exchange_wait(st, after, scatter, f"grads_{names[0]}_wait")
        for n, ld, blk, sc in zip(names, lands, srcs, scatter):
            own = lax.dynamic_index_in_dim(blk, me, 0, keepdims=False) if sc else blk
            recv[n] = _own_block(ld, own, me)
        return lands[0]

    def as3d(n, a):
        if n == "gdn_w_in":
            return a.reshape(a.shape[0], -1, LANES)
        return a

    big = [("cv_w_pw2", 0, "cv_w_pw2"), ("gdn_w_in", 0, "gdn_w_in"), ("gdn_w_out", 0, "gdn_w_out"),
           ("mlp_w1", 0, "mlp_w1_0"), ("mlp_w1", 1, "mlp_w1_1"), ("mlp_w2", 0, "mlp_w2_0"), ("mlp_w2", 1, "mlp_w2_1"),
           ("cv_w_pw1", 0, "cv_w_pw1")]
    res = {n: {} for n in MATMUL_SHARDED}
    after = grad_x
    for entry in sent[:-2]:
        after = finish(entry, after)
    for n, layer, key in big[:-1]:
        res[n][layer] = _sum8_adamw(recv[key], as3d(n, w_in[n]), as3d(n, m_in[n]), as3d(n, v_in[n]), layer, f"adamw_{key}")
        after = res[n][layer][0]
    after = finish(sent[-2], after)
    finish(sent[-1], after)
    n, layer, key = big[-1]
    res[n][layer] = _sum8_adamw(recv[key], as3d(n, w_in[n]), as3d(n, m_in[n]), as3d(n, v_in[n]), layer, f"adamw_{key}")

    grads = dict(zip(small_early, _unpack(_sum_devices(recv["small"], "grads_small_sum"), small_info["sizes"], small_info["shapes"])))
    grads["norm_mix_g"] = _sum_devices(recv["norm_mix_g"], "grads_norm_mix_sum").reshape(nm_shape)
    loss = grads["loss"].reshape(())
    for n in CONV_SHARDED:
        cn = shards[n].shape[-1]
        grads[n] = lax.dynamic_slice_in_dim(grads[n], me * cn, cn, axis=1)

    out_groups = {n: [] for n in WEIGHTS}
    for n in MATMUL_SHARDED:
        layers = sorted(res[n])
        for k in range(4):
            pieces = [res[n][layer][k] for layer in layers]
            out_groups[n].append(jnp.stack(pieces).reshape(w_in[n].shape))

    sm_w = [shards[n] for n in small]
    sm_g = [grads[n].reshape(shards[n].shape) for n in small]
    sm_m = [_squeeze_layer(n, m_in[n]) for n in small]
    sm_v = [_squeeze_layer(n, v_in[n]) for n in small]
    wp, psz = _pack(sm_w)
    gp, _ = _pack(sm_g)
    mp, _ = _pack(sm_m)
    vp, _ = _pack(sm_v)
    dp, nmp, nvp = _adamw(wp, gp, mp, vp)
    shp = [a.shape for a in sm_w]
    for n, g, dl, nm, nv in zip(small, sm_g, _unpack(dp, psz, shp), _unpack(nmp, psz, shp), _unpack(nvp, psz, shp)):
        out_groups[n] = [a.reshape(w_in[n].shape) for a in (g, dl, nm, nv)]

    outs = [loss, grad_x]
    for k in range(4):
        outs += [out_groups[n][k] for n in WEIGHTS]
    return tuple(outs)
```

```python
import jax
import jax.numpy as jnp
from jax import lax
from jax.experimental import pallas as pl
from jax.experimental.pallas import tpu as pltpu

f32, bf16 = jnp.float32, jnp.bfloat16

NORM_EPS = 1e-6
L2_EPS = 1e-6
CHUNK = 64
LANES = 128
SUBLANES = 8
N_DEV = 8
VMEM_LIMIT = 56 * 1024 * 1024
CONV_PAD = 32
HEADS_PER_STEP = 8
PREP_HEADS_PER_STEP = 4
NEG = -1e30

ADAM_LR, ADAM_B1, ADAM_B2, ADAM_EPS, ADAM_WD, ADAM_STEP = 0.001, 0.9, 0.999, 1e-08, 0.01, 10

NT = (((1,), (1,)), ((), ()))
TN = (((0,), (0,)), ((), ()))
HI = lax.Precision.HIGHEST


def _pc(body, *, name, grid, in_specs, out_specs, out_shape, scratch=(), sem=None):
    return pl.pallas_call(
        body, name=name, grid=grid, in_specs=in_specs, out_specs=out_specs, out_shape=out_shape,
        scratch_shapes=list(scratch),
        compiler_params=pltpu.CompilerParams(dimension_semantics=sem, vmem_limit_bytes=VMEM_LIMIT))


def _rows(tm, n):
    return pl.BlockSpec((tm, n), lambda i: (i, 0))


def _const(shape):
    return pl.BlockSpec(shape, lambda *_: (0,) * len(shape))


def _resident(shape):
    return pl.BlockSpec(shape, lambda *_: (0,) * len(shape), pipeline_mode=pl.Buffered(1))


def _tile(t, pref):
    return pref if t % pref == 0 else t


def _dot(a, b):
    return jnp.dot(a.astype(bf16), b.astype(bf16), preferred_element_type=f32)


def _dot_nt(a, b):
    return lax.dot_general(a.astype(bf16), b.astype(bf16), NT, preferred_element_type=f32)


def _dot_tn(a, b):
    return lax.dot_general(a.astype(bf16), b.astype(bf16), TN, preferred_element_type=f32)


def _sigmoid(x):
    return 1.0 / (1.0 + jnp.exp(-x))


def _silu_grad(x):
    s = _sigmoid(x)
    return s * (1.0 + x * (1.0 - s))


def _rms(x, g):
    rstd = lax.rsqrt(jnp.mean(x * x, axis=-1, keepdims=True) + NORM_EPS)
    xh = x * rstd
    return xh * g, xh, rstd


def _rms_bwd(dn, xh, rstd, g):
    dxh = dn * g
    return rstd * (dxh - xh * jnp.mean(dxh * xh, axis=-1, keepdims=True))


def _acc_init(step, *refs):
    @pl.when(step == 0)
    def _():
        for r in refs:
            r[...] = jnp.zeros(r.shape, r.dtype)


def _acc_rows(ref, val):
    ref[0:1, :] += jnp.sum(val, axis=0, keepdims=True)


def _pw1_glu(x, g, w, b):
    t, d = x.shape
    tm = _tile(t, 256)

    nb_w = w.shape[0]

    def body(x_ref, g_ref, w_hbm, b_ref, n_ref, u_ref, gl_ref, w_ref, sems):
        _fetch_blocks(pl.program_id(0), w_hbm, w_ref, sems, True)
        n, _, _ = _rms(x_ref[...], g_ref[...])
        nb = n.astype(bf16)
        n_ref[...] = nb
        u = jnp.dot(nb, w_ref[...], preferred_element_type=f32) + b_ref[...]
        u_ref[...] = u.astype(bf16)
        gl_ref[...] = u[:, :d] * _sigmoid(u[:, d:])

    return _pc(body, name="pw1_glu", grid=(t // tm,),
               in_specs=[_rows(tm, d), _const((1, d)), ANY, _const((1, 2 * d))],
               out_specs=[_rows(tm, d), _rows(tm, 2 * d), _rows(tm, d)],
               out_shape=[jax.ShapeDtypeStruct((t, d), bf16), jax.ShapeDtypeStruct((t, 2 * d), bf16),
                          jax.ShapeDtypeStruct((t, d), f32)],
               scratch=[pltpu.VMEM((d, 2 * d), bf16), pltpu.SemaphoreType.DMA((nb_w,))],
               sem=("arbitrary",))(x, g, w, b)


def _ln_silu_mm_res(dc, ln_g, ln_b, w, b, res):
    t, d = dc.shape
    tm = _tile(t, 256)

    def body(x_ref, g_ref, bb_ref, w_ref, b_ref, r_ref, s_ref, o_ref):
        x = x_ref[...]
        xc = x - jnp.mean(x, axis=-1, keepdims=True)
        rstd = lax.rsqrt(jnp.mean(xc * xc, axis=-1, keepdims=True) + NORM_EPS)
        ln = xc * rstd * g_ref[...] + bb_ref[...]
        sb = (ln * _sigmoid(ln)).astype(bf16)
        s_ref[...] = sb
        o_ref[...] = r_ref[...] + jnp.dot(sb, w_ref[...], preferred_element_type=f32) + b_ref[...]

    return _pc(body, name="ln_silu_pw2", grid=(t // tm,),
               in_specs=[_rows(tm, d), _const((1, d)), _const((1, d)), _resident((d, d)), _const((1, d)), _rows(tm, d)],
               out_specs=[_rows(tm, d), _rows(tm, d)],
               out_shape=[jax.ShapeDtypeStruct((t, d), bf16), jax.ShapeDtypeStruct((t, d), f32)],
               sem=("parallel",))(dc, ln_g, ln_b, w, b, res)


def _fetch_blocks(step, w_hbm, dst, sems, by_cols, layer=None):
    nb_w = w_hbm.shape[0]
    step_rows, step_cols = w_hbm.shape[-2], w_hbm.shape[-1]

    @pl.when(step == 0)
    def _():
        copies = []
        for j in range(nb_w):
            src = w_hbm.at[j] if layer is None else w_hbm.at[j, layer]
            if by_cols:
                part = dst.at[:, pl.ds(j * step_cols, step_cols)]
            else:
                part = dst.at[pl.ds(j * step_rows, step_rows), :]
            copies.append(pltpu.make_async_copy(src, part, sems.at[j]))
        for cp in copies:
            cp.start()
        for cp in copies:
            cp.wait()


def _mlp_fwd(h, g, w1g, w2g, name, loss=None):
    t, d = h.shape
    nb_w, _, bn = w1g.shape
    ff = nb_w * bn
    tm = _tile(t, 256)
    n_in = 4 if loss is None else 6
    n_out = 4 if loss is None else 6

    def body(*refs):
        h_ref, g_ref, w1_hbm, w2_hbm = refs[:4]
        n_ref, f_ref, r_ref = refs[n_in:n_in + 3]
        w1_ref, w2_ref, sem1, sem2 = refs[n_in + n_out:]
        _fetch_blocks(pl.program_id(0), w1_hbm, w1_ref, sem1, True)
        _fetch_blocks(pl.program_id(0), w2_hbm, w2_ref, sem2, False)
        hv = h_ref[...]
        n, _, _ = _rms(hv, g_ref[...])
        nb = n.astype(bf16)
        n_ref[...] = nb
        f = jnp.dot(nb, w1_ref[...], preferred_element_type=f32)
        f_ref[...] = f.astype(bf16)
        rb = jnp.square(jnp.maximum(f, 0.0)).astype(bf16)
        r_ref[...] = rb
        out = hv + jnp.dot(rb, w2_ref[...], preferred_element_type=f32)
        if loss is None:
            refs[n_in + 3][...] = out
        else:
            gf_ref, t_ref = refs[4:6]
            loss_ref, dh_ref, dg_ref = refs[n_in + 3:n_in + 6]
            _acc_init(pl.program_id(0), loss_ref, dg_ref)
            gv = gf_ref[...]
            y, xh, rstd = _rms(out, gv)
            e = y - t_ref[...]
            loss_ref[...] += 0.5 * jnp.sum(jnp.mean(e * e, axis=-1, keepdims=True))
            dy = e * (1.0 / d)
            _acc_rows(dg_ref, dy * xh)
            dh_ref[...] = _rms_bwd(dy, xh, rstd, gv)

    in_specs = [_rows(tm, d), _const((1, d)), ANY, ANY]
    out_specs = [_rows(tm, d), _rows(tm, ff), _rows(tm, ff)]
    out_shape = [jax.ShapeDtypeStruct((t, d), bf16), jax.ShapeDtypeStruct((t, ff), bf16), jax.ShapeDtypeStruct((t, ff), bf16)]
    args = [h, g, w1g, w2g]
    if loss is None:
        out_specs.append(_rows(tm, d))
        out_shape.append(jax.ShapeDtypeStruct((t, d), f32))
    else:
        in_specs += [_const((1, d)), _rows(tm, d)]
        args += list(loss)
        out_specs += [_const((SUBLANES, LANES)), _rows(tm, d), _const((SUBLANES, d))]
        out_shape += [jax.ShapeDtypeStruct((SUBLANES, LANES), f32), jax.ShapeDtypeStruct((t, d), f32),
                      jax.ShapeDtypeStruct((SUBLANES, d), f32)]
    return _pc(body, name=name, grid=(t // tm,), in_specs=in_specs, out_specs=out_specs, out_shape=out_shape,
               scratch=[pltpu.VMEM((d, ff), bf16), pltpu.VMEM((ff, d), bf16),
                        pltpu.SemaphoreType.DMA((nb_w,)), pltpu.SemaphoreType.DMA((nb_w,))],
               sem=("arbitrary",))(*args)


def _softplus(x):
    return jnp.maximum(x, 0.0) + jnp.log(1.0 + jnp.exp(-jnp.abs(x)))


def _gdn_in(h, g, w_main, w_ab, a_log_pad, dt_pad, n_heads):
    t, d = h.shape
    tm = _tile(t, 256)
    gate_rows = 2 * n_heads

    def body(h_ref, g_ref, wm_ref, wab_ref, al_ref, dt_ref, n_ref, qkv_ref, z_ref, ab_ref, gb_ref):
        n, _, _ = _rms(h_ref[...], g_ref[...])
        nb = n.astype(bf16)
        n_ref[...] = nb
        p = jnp.dot(nb, wm_ref[...], preferred_element_type=f32)
        qkv_ref[...] = p[:, :3 * d]
        z_ref[...] = p[:, 3 * d:]
        ab = jnp.dot(nb, wab_ref[...], preferred_element_type=f32)
        ab_ref[...] = ab
        lane = lax.broadcasted_iota(jnp.int32, ab.shape, 1)
        decay = -jnp.exp(al_ref[...]) * _softplus(ab + dt_ref[...])
        gates = jnp.where(lane < n_heads, decay, jnp.where(lane < gate_rows, _sigmoid(ab), 0.0))
        gb_ref[...] = gates.T[:gate_rows, :]

    return _pc(body, name="gdn_in", grid=(t // tm,),
               in_specs=[_rows(tm, d), _const((1, d)), _resident((d, 4 * d)), _resident((d, LANES)),
                         _const((1, LANES)), _const((1, LANES))],
               out_specs=[_rows(tm, d), _rows(tm, 3 * d), _rows(tm, d), _rows(tm, LANES),
                          pl.BlockSpec((gate_rows, tm), lambda i: (0, i))],
               out_shape=[jax.ShapeDtypeStruct((t, d), bf16), jax.ShapeDtypeStruct((t, 3 * d), f32),
                          jax.ShapeDtypeStruct((t, d), f32), jax.ShapeDtypeStruct((t, LANES), f32),
                          jax.ShapeDtypeStruct((gate_rows, t), f32)],
               sem=("parallel",))(h, g, w_main, w_ab, a_log_pad, dt_pad)


def _gated_norm_mm_res(o, z, ng, w, res, n_heads):
    t, d = o.shape
    tm = _tile(t, 256)

    def body(o_ref, z_ref, ng_ref, w_ref, r_ref, on_ref, out_ref):
        for hd in range(n_heads):
            sl = slice(hd * LANES, (hd + 1) * LANES)
            rn, _, _ = _rms(o_ref[:, sl], ng_ref[...])
            zz = z_ref[:, sl]
            on_ref[:, sl] = (rn * (zz * _sigmoid(zz))).astype(bf16)
        out_ref[...] = r_ref[...] + jnp.dot(on_ref[...], w_ref[...], preferred_element_type=f32)

    return _pc(body, name="gated_norm_wout", grid=(t // tm,),
               in_specs=[_rows(tm, d), _rows(tm, d), _const((1, LANES)), _resident((d, d)), _rows(tm, d)],
               out_specs=[_rows(tm, d), _rows(tm, d)],
               out_shape=[jax.ShapeDtypeStruct((t, d), bf16), jax.ShapeDtypeStruct((t, d), f32)],
               sem=("parallel",))(o, z, ng, w, res)


def _mlp_bwd(dho, h, g, fb, w1g, w2g, name):
    t, d = h.shape
    nb_w, _, bn = w1g.shape
    ff = nb_w * bn
    tm = _tile(t, 256)

    def body(do_ref, h_ref, g_ref, f_ref, w1_hbm, w2_hbm, df_ref, dh_ref, dg_ref, cs_ref, w1_ref, w2_ref, sem1, sem2):
        _fetch_blocks(pl.program_id(0), w1_hbm, w1_ref, sem1, True)
        _fetch_blocks(pl.program_id(0), w2_hbm, w2_ref, sem2, False)
        _acc_init(pl.program_id(0), dg_ref, cs_ref)
        do = do_ref[...]
        dr = lax.dot_general(do.astype(bf16), w2_ref[...], NT, preferred_element_type=f32)
        dfb = (dr * (2.0 * jnp.maximum(f_ref[...].astype(f32), 0.0))).astype(bf16)
        df_ref[...] = dfb
        dn = lax.dot_general(dfb, w1_ref[...], NT, preferred_element_type=f32)
        gv = g_ref[...]
        _, xh, rstd = _rms(h_ref[...], gv)
        _acc_rows(dg_ref, dn * xh)
        dh = do + _rms_bwd(dn, xh, rstd, gv)
        dh_ref[...] = dh
        _acc_rows(cs_ref, dh)

    return _pc(body, name=name, grid=(t // tm,),
               in_specs=[_rows(tm, d), _rows(tm, d), _const((1, d)), _rows(tm, ff), ANY, ANY],
               out_specs=[_rows(tm, ff), _rows(tm, d), _const((SUBLANES, d)), _const((SUBLANES, d))],
               out_shape=[jax.ShapeDtypeStruct((t, ff), bf16), jax.ShapeDtypeStruct((t, d), f32),
                          jax.ShapeDtypeStruct((SUBLANES, d), f32), jax.ShapeDtypeStruct((SUBLANES, d), f32)],
               scratch=[pltpu.VMEM((d, ff), bf16), pltpu.VMEM((ff, d), bf16),
                        pltpu.SemaphoreType.DMA((nb_w,)), pltpu.SemaphoreType.DMA((nb_w,))],
               sem=("arbitrary",))(dho, h, g, fb, w1g, w2g)


class _Tail:
    def __init__(self, fn, ins, outs):
        self.fn, self.ins, self.outs = fn, ins, outs


def _mm_nt(pairs, name, tail=None):
    t = pairs[0][0].shape[0]
    tm = _tile(t, 256)
    npair = len(pairs)
    in_specs, args, scratch, blocked = [], [], [], []
    k = None
    for dy, w in pairs:
        nn = dy.shape[1]
        if isinstance(w, tuple) and isinstance(w[0], str):
            w = w[1]
            k = w.shape[1]
            wspec = ANY
            blocked.append(True)
            scratch += [pltpu.VMEM((k, nn), bf16), pltpu.SemaphoreType.DMA((w.shape[0],))]
        elif isinstance(w, tuple):
            w, idx = w
            k = w.shape[0]
            wspec = pl.BlockSpec((k, nn), lambda *_, idx=idx: (0, idx), pipeline_mode=pl.Buffered(1))
            blocked.append(False)
        else:
            k = w.shape[0]
            wspec = _resident(w.shape)
            blocked.append(False)
        in_specs += [_rows(tm, nn), wspec]
        args += [dy, w]
    n_tin = len(tail.ins) if tail else 0
    n_out = len(tail.outs) if tail else 1
    if tail:
        for arr, kind in tail.ins:
            in_specs.append(_rows(tm, arr.shape[1]) if kind == "rows" else _const(arr.shape))
            args.append(arr)
        out_specs = [_rows(tm, c) if kind == "rows" else _const((SUBLANES, c)) for c, kind in tail.outs]
        out_shape = [jax.ShapeDtypeStruct((t, c) if kind == "rows" else (SUBLANES, c), f32) for c, kind in tail.outs]
    else:
        out_specs = _rows(tm, k)
        out_shape = jax.ShapeDtypeStruct((t, k), f32)

    def body(*refs):
        step = pl.program_id(0)
        tin = refs[2 * npair:2 * npair + n_tin]
        outs = refs[2 * npair + n_tin:2 * npair + n_tin + n_out]
        scr = list(refs[2 * npair + n_tin + n_out:])
        acc = None
        for p in range(npair):
            w_ref = refs[2 * p + 1]
            if blocked[p]:
                w_vmem, sems = scr.pop(0), scr.pop(0)
                _fetch_blocks(step, w_ref, w_vmem, sems, True)
                w_ref = w_vmem
            part = lax.dot_general(refs[2 * p][...].astype(bf16), w_ref[...], NT, preferred_element_type=f32)
            acc = part if acc is None else acc + part
        if tail is None:
            outs[0][...] = acc
        else:
            _acc_init(step, *[o for o, (_, kind) in zip(outs, tail.outs) if kind == "acc"])
            tail.fn(acc, tin, outs)

    sequential = tail is not None or any(blocked)
    return _pc(body, name=name, grid=(t // tm,), in_specs=in_specs, out_specs=out_specs, out_shape=out_shape,
               scratch=scratch, sem=("arbitrary",) if sequential else ("parallel",))(*args)


def _rms_bwd_tail(h, g, dres):
    def fn(dn, ins, outs):
        h_ref, g_ref, dr_ref = ins
        dh_ref, dg_ref = outs
        gv = g_ref[...]
        _, xh, rstd = _rms(h_ref[...], gv)
        _acc_rows(dg_ref, dn * xh)
        dh_ref[...] = dr_ref[...] + _rms_bwd(dn, xh, rstd, gv)

    d = h.shape[1]
    return _Tail(fn, [(h, "rows"), (g, "const"), (dres, "rows")], [(d, "rows"), (d, "acc")])


def _ln_silu_bwd_tail(dc, ln_g, ln_b):
    def fn(ds, ins, outs):
        x_ref, g_ref, b_ref = ins
        dx_ref, dg_ref, db_ref, cs_ref = outs
        x = x_ref[...]
        gv = g_ref[...]
        xc = x - jnp.mean(x, axis=-1, keepdims=True)
        rstd = lax.rsqrt(jnp.mean(xc * xc, axis=-1, keepdims=True) + NORM_EPS)
        xh = xc * rstd
        dln = ds * _silu_grad(xh * gv + b_ref[...])
        _acc_rows(dg_ref, dln * xh)
        _acc_rows(db_ref, dln)
        dxh = dln * gv
        dx = rstd * (dxh - jnp.mean(dxh, axis=-1, keepdims=True) - xh * jnp.mean(dxh * xh, axis=-1, keepdims=True))
        dx_ref[...] = dx
        _acc_rows(cs_ref, dx)

    d = dc.shape[1]
    return _Tail(fn, [(dc, "rows"), (ln_g, "const"), (ln_b, "const")], [(d, "rows"), (d, "acc"), (d, "acc"), (d, "acc")])


def _gated_norm_bwd_tail(o, z, ng, n_heads):
    def fn(don_all, ins, outs):
        o_ref, z_ref, ng_ref = ins
        do_ref, dz_ref, dng_ref = outs
        gv = ng_ref[...]
        for hd in range(n_heads):
            sl = slice(hd * LANES, (hd + 1) * LANES)
            rn, xh, rstd = _rms(o_ref[:, sl], gv)
            zz = z_ref[:, sl]
            don = don_all[:, sl]
            dz_ref[:, sl] = don * rn * _silu_grad(zz)
            drn = don * (zz * _sigmoid(zz))
            _acc_rows(dng_ref, drn * xh)
            do_ref[:, sl] = _rms_bwd(drn, xh, rstd, gv)

    d = o.shape[1]
    return _Tail(fn, [(o, "rows"), (z, "rows"), (ng, "const")], [(d, "rows"), (d, "rows"), (LANES, "acc")])


def _mm_tn_blocked(x, dy, name, out_dtype=f32):
    t, k = x.shape
    bn = dy.shape[1] // N_DEV
    tm = _tile(t, 1024)
    jb = N_DEV
    while jb > 1 and k * jb * bn * 4 > 8 * 1024 * 1024:
        jb //= 2
    nt = t // tm

    def body(x_ref, dy_ref, o_ref, *acc):
        acc_ref = acc[0] if acc else o_ref
        _acc_init(pl.program_id(1), acc_ref)
        xt = x_ref[...].astype(bf16).T
        for jj in range(jb):
            acc_ref[jj] += jnp.dot(xt, dy_ref[:, jj * bn:(jj + 1) * bn].astype(bf16), preferred_element_type=f32)
        if acc:
            @pl.when(pl.program_id(1) == nt - 1)
            def _():
                o_ref[...] = acc_ref[...].astype(out_dtype)

    return _pc(body, name=name, grid=(N_DEV // jb, nt),
               in_specs=[pl.BlockSpec((tm, k), lambda j, i: (i, 0)), pl.BlockSpec((tm, jb * bn), lambda j, i: (i, j))],
               out_specs=pl.BlockSpec((jb, k, bn), lambda j, i: (j, 0, 0)),
               out_shape=jax.ShapeDtypeStruct((N_DEV, k, bn), out_dtype),
               scratch=[] if out_dtype == f32 else [pltpu.VMEM((jb, k, bn), f32)],
               sem=("parallel", "arbitrary"))(x, dy)


def _mm_tn(x, dy, name, out_dtype=f32):
    t, k = x.shape
    n = dy.shape[1]
    tm = _tile(t, 1024)
    cap = max(LANES, (2 * 1024 * 1024) // k)
    tn = n
    if n > cap:
        tn = max(c for c in range(LANES, cap + 1, LANES) if n % c == 0)
    nt = t // tm

    def body(x_ref, dy_ref, o_ref, *acc):
        acc_ref = acc[0] if acc else o_ref
        _acc_init(pl.program_id(1), acc_ref)
        acc_ref[...] += lax.dot_general(x_ref[...].astype(bf16), dy_ref[...].astype(bf16), TN, preferred_element_type=f32)
        if acc:
            @pl.when(pl.program_id(1) == nt - 1)
            def _():
                o_ref[...] = acc_ref[...].astype(out_dtype)

    return _pc(body, name=name, grid=(n // tn, nt),
               in_specs=[pl.BlockSpec((tm, k), lambda j, i: (i, 0)), pl.BlockSpec((tm, tn), lambda j, i: (i, j))],
               out_specs=pl.BlockSpec((k, tn), lambda j, i: (0, j)),
               out_shape=jax.ShapeDtypeStruct((k, n), out_dtype),
               scratch=[] if out_dtype == f32 else [pltpu.VMEM((k, tn), f32)],
               sem=("parallel", "arbitrary"))(x, dy)


def _gates_bwd(dgb_t, ab, a_log_pad, dt_pad, n_heads):
    t = ab.shape[0]
    tm = _tile(t, 256)
    gate_rows = 2 * n_heads

    def body(dgb_ref, ab_ref, al_ref, dt_ref, dab_ref, dal_ref, ddt_ref):
        _acc_init(pl.program_id(0), dal_ref, ddt_ref)
        ab = ab_ref[...]
        dgb = jnp.concatenate([dgb_ref[...], jnp.zeros((LANES - gate_rows, tm), f32)], axis=0).T
        lane = lax.broadcasted_iota(jnp.int32, ab.shape, 1)
        is_a = lane < n_heads
        is_b = jnp.logical_and(lane >= n_heads, lane < 2 * n_heads)
        xa = ab + dt_ref[...]
        neg_a = -jnp.exp(al_ref[...])
        dg_da = neg_a * _sigmoid(xa)
        beta = _sigmoid(ab)
        da = jnp.where(is_a, dgb * dg_da, 0.0)
        dab_ref[...] = da + jnp.where(is_b, dgb * beta * (1.0 - beta), 0.0)
        _acc_rows(dal_ref, jnp.where(is_a, dgb * neg_a * _softplus(xa), 0.0))
        _acc_rows(ddt_ref, da)

    return _pc(body, name="gates_bwd", grid=(t // tm,),
               in_specs=[pl.BlockSpec((gate_rows, tm), lambda i: (0, i)), _rows(tm, LANES), _const((1, LANES)), _const((1, LANES))],
               out_specs=[_rows(tm, LANES), _const((SUBLANES, LANES)), _const((SUBLANES, LANES))],
               out_shape=[jax.ShapeDtypeStruct((t, LANES), f32), jax.ShapeDtypeStruct((SUBLANES, LANES), f32),
                          jax.ShapeDtypeStruct((SUBLANES, LANES), f32)],
               sem=("arbitrary",))(dgb_t, ab, a_log_pad, dt_pad)


def _glu_bwd(dgl, ub):
    t, d = dgl.shape
    tm = _tile(t, 256)

    def body(dgl_ref, u_ref, du_ref, cs_ref):
        _acc_init(pl.program_id(0), cs_ref)
        dgl = dgl_ref[...]
        a = u_ref[:, :d].astype(f32)
        sb = _sigmoid(u_ref[:, d:].astype(f32))
        da = dgl * sb
        db = dgl * a * sb * (1.0 - sb)
        du_ref[:, :d] = da.astype(bf16)
        du_ref[:, d:] = db.astype(bf16)
        cs_ref[0:1, :d] += jnp.sum(da, axis=0, keepdims=True)
        cs_ref[0:1, d:] += jnp.sum(db, axis=0, keepdims=True)

    return _pc(body, name="glu_bwd", grid=(t // tm,),
               in_specs=[_rows(tm, d), _rows(tm, 2 * d)],
               out_specs=[_rows(tm, 2 * d), _const((SUBLANES, 2 * d))],
               out_shape=[jax.ShapeDtypeStruct((t, 2 * d), bf16), jax.ShapeDtypeStruct((SUBLANES, 2 * d), f32)],
               sem=("arbitrary",))(dgl, ub)


def _conv_rows(s):
    return 256 if s % 256 == 0 else s


def _conv_tap_sum(pad_ref, w_ref, base, rows, width):
    acc = jnp.zeros((rows, LANES), f32)
    for j in range(width):
        acc = acc + w_ref[j:j + 1, :] * pad_ref[pl.ds(base + CONV_PAD - (width - 1) + j, rows), :]
    return acc


def _l2_silu_post(c, j, n_heads, scale):
    a = c * _sigmoid(c)
    r = lax.rsqrt(jnp.sum(a * a, axis=-1, keepdims=True) + L2_EPS)
    mult = jnp.where(j < n_heads, r * scale, jnp.where(j < 2 * n_heads, r, 1.0))
    return a, r, a * mult


def _dwconv_fwd(x, w, b, name, qk_heads=None):
    bl, s, cn = x.shape
    width = w.shape[0]
    rows = _conv_rows(s)
    scale = float(LANES) ** -0.5

    def body(x_ref, w_ref, b_ref, o_ref, pad_ref):
        j = pl.program_id(1)
        pad_ref[0:CONV_PAD, :] = jnp.zeros((CONV_PAD, LANES), f32)
        pad_ref[CONV_PAD:, :] = x_ref[0]

        def step(i, carry):
            base = pl.multiple_of(i * rows, rows)
            acc = _conv_tap_sum(pad_ref, w_ref, base, rows, width)
            if qk_heads is None:
                acc = acc + b_ref[...]
            else:
                _, _, acc = _l2_silu_post(acc, j, qk_heads, scale)
            o_ref[0, pl.ds(base, rows), :] = acc
            return carry

        lax.fori_loop(0, s // rows, step, 0)

    return _pc(body, name=name, grid=(bl, cn // LANES),
               in_specs=[pl.BlockSpec((1, s, LANES), lambda bi, j: (bi, 0, j)),
                         pl.BlockSpec((width, LANES), lambda bi, j: (0, j)),
                         pl.BlockSpec((1, LANES), lambda bi, j: (0, j))],
               out_specs=pl.BlockSpec((1, s, LANES), lambda bi, j: (bi, 0, j)),
               out_shape=jax.ShapeDtypeStruct((bl, s, cn), f32),
               scratch=[pltpu.VMEM((s + CONV_PAD, LANES), f32)],
               sem=("parallel", "parallel"))(x, w, b)


def _dwconv_bwd(x, dys, w, name, qk_heads=None):
    bl, s, cn = x.shape
    width = w.shape[0]
    wp = -(-width // SUBLANES) * SUBLANES
    rows = _conv_rows(s)
    scale = float(LANES) ** -0.5
    nblk = s // rows
    ndy = len(dys)

    def body(*refs):
        x_ref, w_ref = refs[0], refs[1]
        dy_refs = refs[2:2 + ndy]
        dx_ref, dw_ref, db_ref, xpad, dypad, acc = refs[2 + ndy:]
        j = pl.program_id(0)
        bi = pl.program_id(1)
        _acc_init(bi, acc, db_ref)
        xpad[0:CONV_PAD, :] = jnp.zeros((CONV_PAD, LANES), f32)
        xpad[CONV_PAD:, :] = x_ref[0]
        dypad[s:, :] = jnp.zeros((CONV_PAD, LANES), f32)
        if qk_heads is None:
            dypad[0:s, :] = dy_refs[0][0]
        else:
            def pre(i, carry):
                base = pl.multiple_of(i * rows, rows)
                c = _conv_tap_sum(xpad, w_ref, base, rows, width)
                a, r, _ = _l2_silu_post(c, j, qk_heads, scale)
                dq = dy_refs[0][0, pl.ds(base, rows), :]
                dk = dy_refs[1][0, pl.ds(base, rows), :]
                dv = dy_refs[2][0, pl.ds(base, rows), :]
                dy = jnp.where(j < qk_heads, dq * scale, jnp.where(j < 2 * qk_heads, dk, dv))
                da_l2 = r * (dy - a * (r * r) * jnp.sum(a * dy, axis=-1, keepdims=True))
                da = jnp.where(j < 2 * qk_heads, da_l2, dy)
                dypad[pl.ds(base, rows), :] = da * _silu_grad(c)
                return carry

            lax.fori_loop(0, nblk, pre, 0)

        def step(i, carry):
            base = pl.multiple_of(i * rows, rows)
            dxa = jnp.zeros((rows, LANES), f32)
            for jj in range(width):
                dxa = dxa + w_ref[jj:jj + 1, :] * dypad[pl.ds(base + (width - 1) - jj, rows), :]
            dx_ref[0, pl.ds(base, rows), :] = dxa
            dyc = dypad[pl.ds(base, rows), :]
            db_ref[...] += dyc.reshape(rows // SUBLANES, SUBLANES, LANES).sum(axis=0)
            for jj in range(width):
                prod = dyc * xpad[pl.ds(base + CONV_PAD - (width - 1) + jj, rows), :]
                acc[jj * SUBLANES:(jj + 1) * SUBLANES, :] += prod.reshape(rows // SUBLANES, SUBLANES, LANES).sum(axis=0)
            return carry

        lax.fori_loop(0, nblk, step, 0)

        @pl.when(bi == bl - 1)
        def _():
            dw_ref[...] = jnp.zeros((wp, LANES), f32)
            for jj in range(width):
                dw_ref[jj:jj + 1, :] = jnp.sum(acc[jj * SUBLANES:(jj + 1) * SUBLANES, :], axis=0, keepdims=True)

    if qk_heads is None:
        dy_specs = [pl.BlockSpec((1, s, LANES), lambda j, bi: (bi, 0, j))]
    else:
        hh = qk_heads
        def dy_spec(part):
            def index(j, bi):
                mine = jnp.logical_and(j >= part * hh, j < (part + 1) * hh)
                return (jnp.where(mine, bi * hh + j - part * hh, 0), 0, 0)
            return pl.BlockSpec((1, s, LANES), index)

        dy_specs = [dy_spec(0), dy_spec(1), dy_spec(2)]
    return _pc(body, name=name, grid=(cn // LANES, bl),
               in_specs=[pl.BlockSpec((1, s, LANES), lambda j, bi: (bi, 0, j)),
                         pl.BlockSpec((width, LANES), lambda j, bi: (0, j))] + dy_specs,
               out_specs=[pl.BlockSpec((1, s, LANES), lambda j, bi: (bi, 0, j)),
                          pl.BlockSpec((wp, LANES), lambda j, bi: (0, j)),
                          pl.BlockSpec((SUBLANES, LANES), lambda j, bi: (0, j))],
               out_shape=[jax.ShapeDtypeStruct((bl, s, cn), f32), jax.ShapeDtypeStruct((wp, cn), f32),
                          jax.ShapeDtypeStruct((SUBLANES, cn), f32)],
               scratch=[pltpu.VMEM((s + CONV_PAD, LANES), f32), pltpu.VMEM((s + CONV_PAD, LANES), f32),
                        pltpu.VMEM((width * SUBLANES, LANES), f32)],
               sem=("parallel", "arbitrary"))(x, w, *dys)


def _group_rows(s):
    for rows in (256, 128):
        if s % rows == 0:
            return rows
    return CHUNK


def _group_masks(rows):
    r = lax.broadcasted_iota(jnp.int32, (rows, rows), 0)
    c = lax.broadcasted_iota(jnp.int32, (rows, rows), 1)
    return r, c, r >= c


def _decay(gc_col, gc_row, causal):
    return jnp.exp(jnp.where(causal, gc_col - gc_row, NEG))


def _inv_unit_lower_many(mats, r, c):
    n = r.shape[0]
    eye = (r == c).astype(f32)
    same16 = (r >> 4) == (c >> 4)
    ads = [jnp.where(same16, a, 0.0) for a in mats]
    aos = [a - ad for a, ad in zip(mats, ads)]
    xs = ads
    tds = [-x for x in xs]
    for _ in range(3):
        xs = [_dot(x, x) for x in xs]
        tds = [td + x + _dot(td, x) for td, x in zip(tds, xs)]
    bs = [ao + _dot(td, ao) for td, ao in zip(tds, aos)]
    ps = [-b for b in bs]
    span = 2
    while span < n // 16:
        bs = [_dot(b, b) for b in bs]
        ps = [p + b + _dot(p, b) for p, b in zip(ps, bs)]
        span *= 2
    return [p + td + _dot(p, td) for p, td in zip(ps, tds)]


def _lane_cumsum(x, y, reverse, name):
    rr, n = x.shape

    def body(x_ref, y_ref, o_ref):
        i = lax.broadcasted_iota(jnp.int32, (n, n), 0)
        j = lax.broadcasted_iota(jnp.int32, (n, n), 1)
        tri = ((i >= j) if reverse else (i <= j)).astype(f32)
        o_ref[...] = jnp.dot(x_ref[...] + y_ref[...], tri, precision=HI, preferred_element_type=f32)

    spec = pl.BlockSpec((rr, n), lambda: (0, 0))
    return pl.pallas_call(body, name=name, in_specs=[spec, spec], out_specs=spec,
                          out_shape=jax.ShapeDtypeStruct((rr, n), f32))(x, y)


def _gdn_specs(n_heads, nblk, rows, hp=1, rev=False):
    def blk(n):
        return nblk - 1 - n if rev else n

    def qkv(off):
        return pl.BlockSpec((rows, hp * LANES), lambda g, n: (
            lax.div(g * hp, n_heads) * nblk + blk(n), lax.div(off * n_heads + lax.rem(g * hp, n_heads), hp)))

    def per_head(last, block_rows=rows):
        return pl.BlockSpec((hp, block_rows, last), lambda g, n: (g, blk(n), 0))

    def row_vec():
        return pl.BlockSpec((hp, 1, 1, rows), lambda g, n: (g, blk(n), 0, 0))

    return qkv, per_head, row_vec


def _gdn_prep(qkv, cols, grow, bl, s, n_heads):
    rows = _group_rows(s)
    nblk = s // rows
    bh_n = bl * n_heads
    hp = PREP_HEADS_PER_STEP if n_heads % PREP_HEADS_PER_STEP == 0 else 1
    qkv_spec, ph, rv = _gdn_specs(n_heads, nblk, rows, hp)

    def body(k_ref, v_ref, cols_ref, grow_ref, u_ref, w_ref, t_ref):
        r, c, causal = _group_masks(rows)
        mats, rhs = [], []
        for h in range(hp):
            hs = slice(h * LANES, (h + 1) * LANES)
            k = k_ref[:, hs]
            gc = cols_ref[h, :, 0:1]
            beta = cols_ref[h, :, 1:2]
            kb = k * beta
            mats.append(jnp.where(r > c, _dot_nt(kb, k) * _decay(gc, grow_ref[h, 0], causal), 0.0))
            rhs.append((v_ref[:, hs] * beta, kb * jnp.exp(gc)))
        for h, tm in enumerate(_inv_unit_lower_many(mats, r, c)):
            tb = tm.astype(bf16)
            u_ref[h] = rhs[h][0] + jnp.dot(tb, rhs[h][0].astype(bf16), preferred_element_type=f32)
            w_ref[h] = (rhs[h][1] + jnp.dot(tb, rhs[h][1].astype(bf16), preferred_element_type=f32)).astype(bf16)
            t_ref[h] = tb

    return _pc(body, name="gdn_prep", grid=(bh_n // hp, nblk),
               in_specs=[qkv_spec(1), qkv_spec(2), ph(2), rv()],
               out_specs=[ph(LANES), ph(LANES), ph(rows)],
               out_shape=[jax.ShapeDtypeStruct((bh_n, s, LANES), f32), jax.ShapeDtypeStruct((bh_n, s, LANES), bf16),
                          jax.ShapeDtypeStruct((bh_n, s, rows), bf16)],
               sem=("parallel", "parallel"))(qkv, qkv, cols, grow)


def _gdn_scan(qkv, u, w, cols, grow, bl, s, n_heads):
    rows = _group_rows(s)
    nblk = s // rows
    bh_n = bl * n_heads
    d = n_heads * LANES
    hp = HEADS_PER_STEP if n_heads % HEADS_PER_STEP == 0 else 1
    qkv_spec, ph, rv = _gdn_specs(n_heads, nblk, rows, hp)

    def body(q_ref, k_ref, u_ref, w_ref, cols_ref, grow_ref, o_ref, vn_ref, ss_ref, s_scr):
        _acc_init(pl.program_id(1), s_scr)
        _, _, causal = _group_masks(rows)
        hh = range(hp)
        qs = [q_ref[:, h * LANES:(h + 1) * LANES] for h in hh]
        ks = [k_ref[:, h * LANES:(h + 1) * LANES] for h in hh]
        gcs = [cols_ref[h, :, 0:1] for h in hh]
        ps = [_dot_nt(qs[h], ks[h]) * _decay(gcs[h], grow_ref[h, 0], causal) for h in hh]
        sts = [s_scr[h] for h in hh]
        for h in hh:
            ss_ref[h] = sts[h].astype(bf16)
        vns = [u_ref[h] - _dot(w_ref[h], sts[h]) for h in hh]
        for h in hh:
            vn_ref[h] = vns[h].astype(bf16)
        o_state = [_dot(qs[h] * jnp.exp(gcs[h]), sts[h]) for h in hh]
        o_intra = [_dot(ps[h], vns[h]) for h in hh]
        for h in hh:
            o_ref[:, h * LANES:(h + 1) * LANES] = o_state[h] + o_intra[h]
        for h in hh:
            g_last = gcs[h][rows - 1:rows, :]
            s_scr[h] = jnp.exp(g_last) * sts[h] + _dot_tn(ks[h] * jnp.exp(g_last - gcs[h]), vns[h])

    return _pc(body, name="gdn_scan", grid=(bh_n // hp, nblk),
               in_specs=[qkv_spec(0), qkv_spec(1), ph(LANES), ph(LANES), ph(2), rv()],
               out_specs=[qkv_spec(0), ph(LANES), ph(LANES, block_rows=LANES)],
               out_shape=[jax.ShapeDtypeStruct((bl * s, d), f32), jax.ShapeDtypeStruct((bh_n, s, LANES), bf16),
                          jax.ShapeDtypeStruct((bh_n, nblk * LANES, LANES), bf16)],
               scratch=[pltpu.VMEM((hp, LANES, LANES), f32)],
               sem=("parallel", "arbitrary"))(qkv, qkv, u, w, cols, grow)


def _gdn_scan_bwd(do, qkv, w, vn, cols, grow, ss, bl, s, n_heads):
    rows = _group_rows(s)
    nblk = s // rows
    bh_n = bl * n_heads
    hp = HEADS_PER_STEP if n_heads % HEADS_PER_STEP == 0 else 1
    qkv_spec, ph, rv = _gdn_specs(n_heads, nblk, rows, hp, rev=True)

    def body(do_ref, q_ref, k_ref, w_ref, vn_ref, cols_ref, grow_ref, ss_ref,
             du_ref, dw_ref, dq_ref, dk_ref, dcol_ref, drow_ref, ds_scr):
        _acc_init(pl.program_id(1), ds_scr)
        _, _, causal = _group_masks(rows)
        last_row = lax.broadcasted_iota(jnp.int32, (rows, 1), 0) == rows - 1
        hh = range(hp)
        dos = [do_ref[:, h * LANES:(h + 1) * LANES] for h in hh]
        qs = [q_ref[:, h * LANES:(h + 1) * LANES] for h in hh]
        ks = [k_ref[:, h * LANES:(h + 1) * LANES] for h in hh]
        vns = [vn_ref[h] for h in hh]
        gcs = [cols_ref[h, :, 0:1] for h in hh]
        sts = [ss_ref[h] for h in hh]
        dss = [ds_scr[h] for h in hh]
        dmats = [_decay(gcs[h], grow_ref[h, 0], causal) for h in hh]
        gams = [jnp.exp(gc) for gc in gcs]
        qgs = [qs[h] * gams[h] for h in hh]
        g_lasts = [gc[rows - 1:rows, :] for gc in gcs]
        kd_scales = [jnp.exp(g_lasts[h] - gcs[h]) for h in hh]
        kdecs = [ks[h] * kd_scales[h] for h in hh]
        qks = [_dot_nt(qs[h], ks[h]) for h in hh]
        dpds = [_dot_nt(dos[h], vns[h]) * dmats[h] for h in hh]
        dvns = [_dot_tn(qks[h] * dmats[h], dos[h]) + _dot(kdecs[h], dss[h]) for h in hh]
        for h in hh:
            du_ref[h] = dvns[h]
        dkdecs = [_dot_nt(vns[h], dss[h]) for h in hh]
        for h in hh:
            dw_ref[h] = -_dot_nt(dvns[h], sts[h])
        dqgs = [_dot_nt(dos[h], sts[h]) for h in hh]
        dq_intra = [_dot(dpds[h], ks[h]) for h in hh]
        dk_intra = [_dot_tn(dpds[h], qs[h]) for h in hh]
        for h in hh:
            dq_ref[h] = dqgs[h] * gams[h] + dq_intra[h]
            dk_ref[h] = dk_intra[h] + dkdecs[h] * kd_scales[h]
            ep = dpds[h] * qks[h]
            drow_ref[h, 0] = -jnp.sum(ep, axis=0, keepdims=True)
            kd_rows = jnp.sum(dkdecs[h] * kdecs[h], axis=-1, keepdims=True)
            extra = jnp.sum(kd_rows) + jnp.exp(g_lasts[h]) * jnp.sum(sts[h] * dss[h])
            dcol_ref[h] = (jnp.sum(dqgs[h] * qgs[h], axis=-1, keepdims=True) + jnp.sum(ep, axis=-1, keepdims=True)
                           - kd_rows + jnp.where(last_row, extra, 0.0))
        ds_new = [jnp.exp(g_lasts[h]) * dss[h] + _dot_tn(qgs[h], dos[h]) - _dot_tn(w_ref[h], dvns[h]) for h in hh]
        for h in hh:
            ds_scr[h] = ds_new[h]

    return _pc(body, name="gdn_scan_bwd", grid=(bh_n // hp, nblk),
               in_specs=[qkv_spec(0), qkv_spec(0), qkv_spec(1), ph(LANES), ph(LANES), ph(2), rv(),
                         ph(LANES, block_rows=LANES)],
               out_specs=[ph(LANES), ph(LANES), ph(LANES), ph(LANES), ph(1), rv()],
               out_shape=[jax.ShapeDtypeStruct((bh_n, s, LANES), f32)] * 4
               + [jax.ShapeDtypeStruct((bh_n, s, 1), f32), jax.ShapeDtypeStruct((bh_n, nblk, 1, rows), f32)],
               scratch=[pltpu.VMEM((hp, LANES, LANES), f32)],
               sem=("parallel", "arbitrary"))(do, qkv, qkv, w, vn, cols, grow, ss)


def _gdn_prep_bwd(qkv, cols, grow, tmat, du, dw, dk_scan, dcol_scan, drow_scan, bl, s, n_heads):
    rows = _group_rows(s)
    nblk = s // rows
    bh_n = bl * n_heads
    hp = PREP_HEADS_PER_STEP if n_heads % PREP_HEADS_PER_STEP == 0 else 1
    qkv_spec, ph, rv = _gdn_specs(n_heads, nblk, rows, hp)

    def body(k_ref, v_ref, cols_ref, grow_ref, t_ref, du_ref, dw_ref, dks_ref, dcs_ref, drs_ref,
             dk_ref, dv_ref, dcols_ref, drow_ref):
        r, c, causal = _group_masks(rows)
        hh = range(hp)
        ks = [k_ref[:, h * LANES:(h + 1) * LANES] for h in hh]
        vs = [v_ref[:, h * LANES:(h + 1) * LANES] for h in hh]
        gcs = [cols_ref[h, :, 0:1] for h in hh]
        betas = [cols_ref[h, :, 1:2] for h in hh]
        tms = [t_ref[h] for h in hh]
        dus = [du_ref[h] for h in hh]
        dws = [dw_ref[h] for h in hh]
        gams = [jnp.exp(gc) for gc in gcs]
        kbs = [k * b for k, b in zip(ks, betas)]
        kbgs = [kb * g for kb, g in zip(kbs, gams)]
        dts = [_dot_nt(dus[h], vs[h] * betas[h]) + _dot_nt(dws[h], kbgs[h]) for h in hh]
        dvbs = [dus[h] + _dot_tn(tms[h], dus[h]) for h in hh]
        dkbgs = [dws[h] + _dot_tn(tms[h], dws[h]) for h in hh]
        kks = [_dot_nt(kbs[h], ks[h]) for h in hh]
        inner = [dts[h] + _dot_nt(dts[h], tms[h]) for h in hh]
        dads = [jnp.where(r > c, -(inner[h] + _dot_tn(tms[h], inner[h])), 0.0) * _decay(gcs[h], grow_ref[h, 0], causal)
                for h in hh]
        dkbs = [dkbgs[h] * gams[h] + _dot(dads[h], ks[h]) for h in hh]
        dk2 = [_dot_tn(dads[h], kbs[h]) for h in hh]
        for h in hh:
            dk_ref[h] = dks_ref[h] + dk2[h] + dkbs[h] * betas[h]
            dv_ref[h] = dvbs[h] * betas[h]
            ea = dads[h] * kks[h]
            dcols_ref[h, :, 0:1] = (dcs_ref[h] + jnp.sum(dkbgs[h] * kbgs[h], axis=-1, keepdims=True)
                                    + jnp.sum(ea, axis=-1, keepdims=True))
            dcols_ref[h, :, 1:2] = (jnp.sum(dvbs[h] * vs[h], axis=-1, keepdims=True)
                                    + jnp.sum(dkbs[h] * ks[h], axis=-1, keepdims=True))
            drow_ref[h, 0] = drs_ref[h, 0] - jnp.sum(ea, axis=0, keepdims=True)

    return _pc(body, name="gdn_prep_bwd", grid=(bh_n // hp, nblk),
               in_specs=[qkv_spec(1), qkv_spec(2), ph(2), rv(), ph(rows), ph(LANES), ph(LANES), ph(LANES), ph(1), rv()],
               out_specs=[ph(LANES), ph(LANES), ph(2), rv()],
               out_shape=[jax.ShapeDtypeStruct((bh_n, s, LANES), f32), jax.ShapeDtypeStruct((bh_n, s, LANES), f32),
                          jax.ShapeDtypeStruct((bh_n, s, 2), f32), jax.ShapeDtypeStruct((bh_n, nblk, 1, rows), f32)],
               sem=("parallel", "parallel"))(qkv, qkv, cols, grow, tmat, du, dw, dk_scan, dcol_scan, drow_scan)


def _row(v):
    return v.reshape(1, -1).astype(f32)


def _pad_lanes(v):
    v = v.reshape(1, -1).astype(f32)
    return jnp.pad(v, ((0, 0), (0, LANES - v.shape[1])))


def _local_step(x, tgt, p, need, emit):
    bl, s, d = x.shape
    t = bl * s
    n_heads = p["gdn_a_log"].shape[-1]
    assert d == n_heads * LANES and s % CHUNK == 0
    x2 = x.reshape(t, d)
    tgt2 = tgt.reshape(t, d)
    gr = {}

    n0, ub, gl = _pw1_glu(x2, _row(p["norm_mix_g"][0]), p["cv_w_pw1"], _row(p["cv_b_pw1"]))
    cv = need("conv", n0)
    dc = _dwconv_fwd(gl.reshape(bl, s, d), cv["cv_w_dw"], _row(p["cv_b_dw"]), "dwconv_fwd").reshape(t, d)
    sb, h1 = _ln_silu_mm_res(dc, _row(p["cv_ln_g"]), _row(p["cv_ln_b"]), cv["cv_w_pw2"], _row(p["cv_b_pw2"]), x2)
    m0 = need("mlp0", h1)
    n1, f0, r0, h2 = _mlp_fwd(h1, _row(p["norm_ffn_g"][0]), m0["w1"], m0["w2"], "mlp_fwd0")

    gd = need("gdn", h2)
    w_in = gd["w_in"]
    w_ab = jnp.pad(w_in[:, 4 * d:], ((0, 0), (0, LANES - 2 * n_heads)))
    a_log_pad = _pad_lanes(p["gdn_a_log"])
    dt_pad = _pad_lanes(p["gdn_dt_bias"])
    n2, qkv_pre, z, ab, gbeta = _gdn_in(h2, _row(p["norm_mix_g"][1]), w_in, w_ab, a_log_pad, dt_pad, n_heads)
    zero_bias = jnp.zeros((1, 3 * d), f32)
    qkv = _dwconv_fwd(qkv_pre.reshape(bl, s, 3 * d), cv["gdn_conv_w"], zero_bias, "sconv_fwd", qk_heads=n_heads).reshape(t, 3 * d)
    bh_n, rows = bl * n_heads, _group_rows(s)
    gates = gbeta.reshape(2, n_heads, bl, s).transpose(0, 2, 1, 3).reshape(2, bh_n * (s // rows), rows)
    gc_lanes = _lane_cumsum(gates[0], jnp.zeros_like(gates[0]), False, "gdn_gate_cumsum")
    grow = gc_lanes.reshape(bh_n, s // rows, 1, rows)
    cols = jnp.stack([gc_lanes.reshape(bh_n, s), gates[1].reshape(bh_n, s)], axis=-1)
    u, w, tmat = _gdn_prep(qkv, cols, grow, bl, s, n_heads)
    o, vn, ss = _gdn_scan(qkv, u, w, cols, grow, bl, s, n_heads)
    onb, h3 = _gated_norm_mm_res(o, z, _row(p["gdn_norm_g"]), gd["w_out"], h2, n_heads)
    m1 = need("mlp1", h3)
    n3, f1, r1, loss_acc, dh4, dgf = _mlp_fwd(h3, _row(p["norm_ffn_g"][1]), m1["w1"], m1["w2"], "mlp_fwd1_loss",
                                              loss=(_row(p["final_norm_g"]), tgt2))
    loss = loss_acc[0, 0]
    gr["loss"] = loss_acc[0, 0:1]
    gr["final_norm_g"] = dgf[0]

    df1, dh3, dg_ffn1, _ = _mlp_bwd(dh4, h3, _row(p["norm_ffn_g"][1]), f1, m1["w1"], m1["w2"], "mlp_bwd1")
    dw2_1 = _mm_tn(r1, dh4, "dw_mlp2_1", bf16)
    dw1_1 = _mm_tn_blocked(n3, df1, "dw_mlp1_1", bf16)

    dw_out = _mm_tn(onb, dh3, "dw_gdn_out", bf16)
    tie = emit("late", {"mlp_w2_1": dw2_1, "mlp_w1_1": dw1_1, "gdn_w_out": dw_out})
    do, dz, dng = _mm_nt([(dh3, gd["w_out"])], "dx_gdn_out",
                         _gated_norm_bwd_tail(o, z, _row(p["gdn_norm_g"]) + tie, n_heads))
    gr["gdn_norm_g"] = dng[0]
    du, dw_, dq, dk_scan, dcol_scan, drow_scan = _gdn_scan_bwd(do, qkv, w, vn, cols, grow, ss, bl, s, n_heads)
    dk, dv, dcols, drow = _gdn_prep_bwd(qkv, cols, grow, tmat, du, dw_, dk_scan, dcol_scan, drow_scan, bl, s, n_heads)
    dg_lanes = _lane_cumsum(dcols[..., 0].reshape(gc_lanes.shape), drow.reshape(gc_lanes.shape), True, "gdn_gate_cumsum_bwd")
    dqkv_pre, dconv_w, _ = _dwconv_bwd(qkv_pre.reshape(bl, s, 3 * d), [dq, dk, dv], cv["gdn_conv_w"], "sconv_bwd", qk_heads=n_heads)
    gr["gdn_conv_w"] = dconv_w[:cv["gdn_conv_w"].shape[0]]
    dgb_t = jnp.stack([dg_lanes.reshape(bl, n_heads, s), dcols[..., 1].reshape(bl, n_heads, s)])
    dgb_t = dgb_t.transpose(0, 2, 1, 3).reshape(2 * n_heads, t)
    dab, dal, ddt = _gates_bwd(dgb_t, ab, a_log_pad, dt_pad, n_heads)
    gr["gdn_a_log"] = dal[0, :n_heads]
    gr["gdn_dt_bias"] = ddt[0, :n_heads]
    dqkv2 = dqkv_pre.reshape(t, 3 * d)
    dw_in = jnp.concatenate(
        [_mm_tn(n2, dqkv2, "dw_gdn_in_qkv"), _mm_tn(n2, dz, "dw_gdn_in_z"), _mm_tn(n2, dab, "dw_gdn_in_ab")[:, :2 * n_heads]], axis=1)
    tie = emit("gdn_in", {"gdn_w_in": dw_in})
    dh2, dg_mix1 = _mm_nt([(dqkv2, (w_in, 0)), (dz, (w_in, 3)), (dab, w_ab)], "dx_gdn_in",
                          _rms_bwd_tail(h2, _row(p["norm_mix_g"][1]) + tie, dh3))

    df0, dh1, dg_ffn0, cs_h1 = _mlp_bwd(dh2, h1, _row(p["norm_ffn_g"][0]), f0, m0["w1"], m0["w2"], "mlp_bwd0")
    dw2_0 = _mm_tn(r0, dh2, "dw_mlp2_0", bf16)
    dw1_0 = _mm_tn_blocked(n1, df0, "dw_mlp1_0", bf16)
    dw_pw2 = _mm_tn(sb, dh1, "dw_pw2", bf16)
    tie = emit("mlp0", {"mlp_w2_0": dw2_0, "mlp_w1_0": dw1_0, "cv_w_pw2": dw_pw2})
    gr["norm_ffn_g"] = jnp.stack([dg_ffn0[0], dg_ffn1[0]])

    gr["cv_b_pw2"] = cs_h1[0]
    ddc, dlng, dlnb, cs_dc = _mm_nt([(dh1, cv["cv_w_pw2"])], "dx_pw2",
                                    _ln_silu_bwd_tail(dc, _row(p["cv_ln_g"]) + tie, _row(p["cv_ln_b"])))
    gr["cv_ln_g"] = dlng[0]
    gr["cv_ln_b"] = dlnb[0]
    gr["cv_b_dw"] = cs_dc[0]
    dgl, dw_dw, _ = _dwconv_bwd(gl.reshape(bl, s, d), [ddc.reshape(bl, s, d)], cv["cv_w_dw"], "dwconv_bwd")
    gr["cv_w_dw"] = dw_dw[:cv["cv_w_dw"].shape[0]]
    dub, cs_u = _glu_bwd(dgl.reshape(t, d), ub)
    gr["cv_b_pw1"] = cs_u[0]
    dw_pw1 = _mm_tn_blocked(n0, dub, "dw_pw1", bf16)
    tie = emit("last", {"cv_w_pw1": dw_pw1, "small": gr})
    dx, dg_mix0 = _mm_nt([(dub, ("blocks", p["cv_w_pw1"]))], "dx_pw1",
                         _rms_bwd_tail(x2, _row(p["norm_mix_g"][0]) + tie, dh1))
    return loss, dx.reshape(bl, s, d), {"norm_mix_g": jnp.stack([dg_mix0[0], dg_mix1[0]])}


ANY = pl.BlockSpec(memory_space=pl.ANY)
MESH = pl.DeviceIdType.MESH


def _flip(v, bit):
    return 1 - v if bit else v


def _all_gather_many(shards):
    na = len(shards)

    def body(*refs):
        x_refs, o_refs = refs[:na], refs[na:2 * na]
        send_sems, recv_sems, local_sems = refs[2 * na:]
        x, y, c = lax.axis_index("x"), lax.axis_index("y"), lax.axis_index("c")
        me, sibling = (x, y, c), (x, y, 1 - c)
        chips = [(1 - x, y), (x, 1 - y), (1 - x, 1 - y)]

        def copy(a, k, block, to, src=None):
            px, py, pc = block
            dst = o_refs[a].at[4 * px + 2 * py + pc]
            return pltpu.make_async_remote_copy(
                src_ref=dst if src is None else src, dst_ref=dst,
                send_sem=send_sems.at[a, k], recv_sem=recv_sems.at[a, k], device_id=to, device_id_type=MESH)

        mine = [pltpu.make_async_copy(x_refs[a], o_refs[a].at[4 * x + 2 * y + c], local_sems.at[a]) for a in range(na)]
        first = []
        for a in range(na):
            first.append(copy(a, 0, me, sibling, src=x_refs[a]))
            first += [copy(a, 1 + j, me, (*chip, c), src=x_refs[a]) for j, chip in enumerate(chips)]
        for cp in mine + first:
            cp.start()
        passed = []
        for j, chip in enumerate(chips):
            for a in range(na):
                copy(a, 1 + j, (*chip, c), me).wait_recv()
                fwd = copy(a, 4 + j, (*chip, c), sibling)
                fwd.start()
                passed.append(fwd)
        for a in range(na):
            copy(a, 0, sibling, me).wait_recv()
        for j, chip in enumerate(chips):
            for a in range(na):
                copy(a, 4 + j, (*chip, 1 - c), me).wait_recv()
        for cp in first + passed:
            cp.wait_send()
        for cp in mine:
            cp.wait()

    return pl.pallas_call(
        body, name="weights_all_gather",
        out_shape=[jax.ShapeDtypeStruct((N_DEV,) + a.shape, a.dtype) for a in shards],
        in_specs=[ANY] * na, out_specs=[ANY] * na,
        scratch_shapes=[pltpu.SemaphoreType.DMA((na, 7)), pltpu.SemaphoreType.DMA((na, 7)), pltpu.SemaphoreType.DMA((na,))],
        compiler_params=pltpu.CompilerParams(has_side_effects=True),
    )(*shards)


HBM = pl.BlockSpec(memory_space=pltpu.HBM)
SEM = pl.BlockSpec(memory_space=pltpu.SEMAPHORE)
EFFECT = pltpu.SideEffectType.DATAFLOW_SIDE_EFFECTING
N_PEERS = N_DEV - 1


def _exchange_copies(src_refs, land_refs, send_sems, recv_sems, scatter):
    x, y, c = lax.axis_index("x"), lax.axis_index("y"), lax.axis_index("c")
    me = 4 * x + 2 * y + c
    copies = []
    for a, (src, land) in enumerate(zip(src_refs, land_refs)):
        for k in range(1, N_DEV):
            px, py, pc = _flip(x, k & 4), _flip(y, k & 2), _flip(c, k & 1)
            i = a * N_PEERS + k - 1
            copies.append(pltpu.make_async_remote_copy(
                src_ref=src.at[4 * px + 2 * py + pc] if scatter[a] else src, dst_ref=land.at[me],
                send_sem=send_sems.at[i], recv_sem=recv_sems.at[i], device_id=(px, py, pc), device_id_type=MESH))
    return copies


def _exchange_start(srcs, scatter, name):
    na = len(srcs)
    lands = [lax.empty(s.shape if sc else (N_DEV,) + s.shape, s.dtype) for s, sc in zip(srcs, scatter)]

    def body(*refs):
        copies = _exchange_copies(refs[:na], refs[na:2 * na], refs[2 * na], refs[2 * na + 1], scatter)
        for cp in copies:
            cp.start()
        token = refs[-1]
        token[...] = jnp.zeros_like(token)

    outs = pl.pallas_call(
        body, name=name,
        out_shape=(pltpu.SemaphoreType.DMA((na * N_PEERS,)), pltpu.SemaphoreType.DMA((na * N_PEERS,)))
        + tuple(pltpu.HBM(a.shape, a.dtype) for a in srcs + lands) + (jax.ShapeDtypeStruct((SUBLANES, LANES), f32),),
        in_specs=[HBM] * (2 * na),
        out_specs=(SEM, SEM) + (HBM,) * (2 * na) + (pl.BlockSpec(memory_space=pltpu.VMEM),),
        input_output_aliases={i: 2 + i for i in range(2 * na)},
        compiler_params=pltpu.CompilerParams(has_side_effects=EFFECT),
    )(*[pltpu.with_memory_space_constraint(a, pltpu.HBM) for a in srcs + lands])
    return outs[0], outs[1], list(outs[2:2 + na]), list(outs[2 + na:2 + 2 * na]), outs[-1]


def _exchange_wait(started, after, scatter, name):
    send_sems, recv_sems, srcs, lands, _ = started
    na = len(srcs)

    def body(*refs):
        for cp in _exchange_copies(refs[:na], refs[na:2 * na], refs[2 * na], refs[2 * na + 1], scatter):
            cp.wait_send()
            cp.wait_recv()

    outs = pl.pallas_call(
        body, name=name,
        out_shape=tuple(pltpu.HBM(a.shape, a.dtype) for a in srcs + lands),
        in_specs=[HBM] * (2 * na) + [SEM, SEM, ANY], out_specs=(HBM,) * (2 * na),
        input_output_aliases={i: i for i in range(2 * na)},
        compiler_params=pltpu.CompilerParams(has_side_effects=EFFECT),
    )(*srcs, *lands, send_sems, recv_sems, after)
    return list(outs[:na]), list(outs[na:])


def _own_block(land, block, me):
    return lax.dynamic_update_index_in_dim(land, block, me, 0)


def _sum8_adamw(r2, w, m, v, layer, name):
    _, rr, cc = r2.shape
    tr = _tile(rr, 256)
    bc1 = 1.0 - ADAM_B1 ** ADAM_STEP
    bc2 = 1.0 - ADAM_B2 ** ADAM_STEP

    def body(r_ref, w_ref, m_ref, v_ref, g_ref, d_ref, nm_ref, nv_ref):
        gv = r_ref[0].astype(f32)
        for q in range(1, N_DEV):
            gv = gv + r_ref[q].astype(f32)
        g_ref[...] = gv
        nm = ADAM_B1 * m_ref[...] + (1.0 - ADAM_B1) * gv
        nv = ADAM_B2 * v_ref[...] + (1.0 - ADAM_B2) * (gv * gv)
        nm_ref[...] = nm
        nv_ref[...] = nv
        d_ref[...] = -ADAM_LR * ((nm / bc1) / (jnp.sqrt(nv / bc2) + ADAM_EPS) + ADAM_WD * w_ref[...])

    lspec = pl.BlockSpec((None, tr, cc), lambda i: (layer, i, 0))
    return _pc(body, name=name, grid=(rr // tr,),
               in_specs=[pl.BlockSpec((N_DEV, tr, cc), lambda i: (0, i, 0)), lspec, lspec, lspec],
               out_specs=[_rows(tr, cc)] * 4, out_shape=[jax.ShapeDtypeStruct((rr, cc), f32)] * 4,
               sem=("parallel",))(r2, w, m, v)


def _sum_devices(recv, name):
    _, rr, _ = recv.shape
    tr = _tile(rr, 512)

    def body(r_ref, o_ref):
        acc = r_ref[0]
        for i in range(1, N_DEV):
            acc = acc + r_ref[i]
        o_ref[...] = acc

    return _pc(body, name=name, grid=(rr // tr,),
               in_specs=[pl.BlockSpec((N_DEV, tr, LANES), lambda i: (0, i, 0))],
               out_specs=_rows(tr, LANES), out_shape=jax.ShapeDtypeStruct((rr, LANES), f32), sem=("parallel",))(recv)


def _adamw(w, g, m, v):
    rr = w.shape[0]
    tr = _tile(rr, 512)
    bc1 = 1.0 - ADAM_B1 ** ADAM_STEP
    bc2 = 1.0 - ADAM_B2 ** ADAM_STEP

    def body(w_ref, g_ref, m_ref, v_ref, d_ref, nm_ref, nv_ref):
        gv = g_ref[...]
        nm = ADAM_B1 * m_ref[...] + (1.0 - ADAM_B1) * gv
        nv = ADAM_B2 * v_ref[...] + (1.0 - ADAM_B2) * (gv * gv)
        nm_ref[...] = nm
        nv_ref[...] = nv
        d_ref[...] = -ADAM_LR * ((nm / bc1) / (jnp.sqrt(nv / bc2) + ADAM_EPS) + ADAM_WD * w_ref[...])

    spec = _rows(tr, LANES)
    return _pc(body, name="adamw", grid=(rr // tr,), in_specs=[spec] * 4, out_specs=[spec] * 3,
               out_shape=[jax.ShapeDtypeStruct((rr, LANES), f32)] * 3, sem=("parallel",))(w, g, m, v)


PACK_ROWS = 512
PART_ROWS = SUBLANES


def _pack(arrs):
    parts, sizes = [], []
    for a in arrs:
        flat = a.reshape(-1)
        n = flat.shape[0]
        rows = -(-n // (LANES * PART_ROWS)) * PART_ROWS
        if rows * LANES != n:
            flat = jnp.pad(flat, (0, rows * LANES - n))
        parts.append(flat.reshape(rows, LANES))
        sizes.append((rows, n))
    total = sum(r for r, _ in sizes)
    padded = -(-total // PACK_ROWS) * PACK_ROWS
    if padded > total:
        parts.append(jnp.zeros((padded - total, LANES), parts[0].dtype))
    return jnp.concatenate(parts, axis=0), sizes


def _unpack(packed, sizes, shapes):
    out, off = [], 0
    for (rows, n), shp in zip(sizes, shapes):
        piece = lax.slice_in_dim(packed, off, off + rows, axis=0)
        if rows * LANES != n:
            piece = lax.slice_in_dim(piece.reshape(-1), 0, n, axis=0)
        out.append(piece.reshape(tuple(shp)))
        off += rows
    return out


def _cols_to_blocks(a):
    n = a.shape[-1] // N_DEV
    a = a.reshape(a.shape[:-1] + (N_DEV, n))
    return jnp.moveaxis(a, -2, 0)


def _blocks_to_cols(a):
    a = jnp.moveaxis(a, 0, -2)
    return a.reshape(a.shape[:-2] + (a.shape[-2] * a.shape[-1],))


def _rows_to_blocks(a):
    k = a.shape[-2] // N_DEV
    a = a.reshape(a.shape[:-2] + (N_DEV, k, a.shape[-1]))
    return jnp.moveaxis(a, -3, 0)


def _blocks_to_rows(a):
    a = jnp.moveaxis(a, 0, -3)
    return a.reshape(a.shape[:-3] + (a.shape[-3] * a.shape[-2], a.shape[-1]))


COL_SHARDED = ("cv_w_pw1", "gdn_w_in", "mlp_w1")
ROW_SHARDED = ("cv_w_pw2", "gdn_w_out", "mlp_w2")
CONV_SHARDED = ("cv_w_dw", "gdn_conv_w")
REPLICATED = ("norm_mix_g", "norm_ffn_g", "final_norm_g", "cv_b_pw1", "cv_b_dw", "cv_ln_g", "cv_ln_b", "cv_b_pw2",
              "gdn_a_log", "gdn_dt_bias", "gdn_norm_g")
WEIGHTS = ("norm_mix_g", "norm_ffn_g", "final_norm_g", "cv_w_pw1", "cv_b_pw1", "cv_w_dw", "cv_b_dw", "cv_ln_g",
           "cv_ln_b", "cv_w_pw2", "cv_b_pw2", "gdn_w_in", "gdn_conv_w", "gdn_a_log", "gdn_dt_bias", "gdn_norm_g",
           "gdn_w_out", "mlp_w1", "mlp_w2")
MATMUL_SHARDED = COL_SHARDED + ROW_SHARDED


def _squeeze_layer(name, a):
    if name in ("norm_mix_g", "norm_ffn_g", "final_norm_g", "mlp_w1", "mlp_w2"):
        return a
    return a[0]


def _gather_weights(shards):
    me = 4 * lax.axis_index("x") + 2 * lax.axis_index("y") + lax.axis_index("c")
    pw1 = _all_gather_many([shards["cv_w_pw1"].astype(bf16)])[0]
    now = {"cv_w_pw1": pw1}

    def cast(a, tie):
        return (a + tie).astype(bf16)

    token, _ = lax.optimization_barrier((jnp.zeros((), f32), pw1))
    later, started = {}, {}
    for group in ("conv", "mlp0", "gdn", "mlp1"):
        if group == "conv":
            srcs = [cast(shards["cv_w_pw2"], token)] + [shards[n] + token for n in CONV_SHARDED]
        elif group == "gdn":
            srcs = [cast(shards["gdn_w_in"], token).reshape(-1, LANES), cast(shards["gdn_w_out"], token)]
        else:
            layer = int(group[-1])
            srcs = [cast(shards["mlp_w1"][layer], token), cast(shards["mlp_w2"][layer], token)]
        later[group] = srcs
        started[group] = _exchange_start(srcs, [False] * len(srcs), f"weights_{group}_start")
        token = started[group][4][0, 0]

    def need(group, after):
        srcs, lands = _exchange_wait(started[group], after, [False] * len(later[group]), f"weights_{group}_wait")
        lands = [_own_block(ld, own, me) for ld, own in zip(lands, srcs)]
        if group == "conv":
            out = {"cv_w_pw2": _blocks_to_rows(lands[0])}
            out.update({n: _blocks_to_cols(ld) for n, ld in zip(CONV_SHARDED, lands[1:])})
            return out
        if group == "gdn":
            w_in = _blocks_to_cols(lands[0].reshape((N_DEV,) + shards["gdn_w_in"].shape))
            return {"w_in": w_in, "w_out": _blocks_to_rows(lands[1])}
        return {"w1": lands[0], "w2": lands[1]}

    return now, need, token


def kernel(x, norm_mix_g, norm_ffn_g, final_norm_g, cv_w_pw1, cv_b_pw1, cv_w_dw, cv_b_dw, cv_ln_g, cv_ln_b, cv_w_pw2, cv_b_pw2, gdn_w_in, gdn_conv_w, gdn_a_log, gdn_dt_bias, gdn_norm_g, gdn_w_out, mlp_w1, mlp_w2, loss_target, m_norm_mix_g, m_norm_ffn_g, m_final_norm_g, m_cv_w_pw1, m_cv_b_pw1, m_cv_w_dw, m_cv_b_dw, m_cv_ln_g, m_cv_ln_b, m_cv_w_pw2, m_cv_b_pw2, m_gdn_w_in, m_gdn_conv_w, m_gdn_a_log, m_gdn_dt_bias, m_gdn_norm_g, m_gdn_w_out, m_mlp_w1, m_mlp_w2, v_norm_mix_g, v_norm_ffn_g, v_final_norm_g, v_cv_w_pw1, v_cv_b_pw1, v_cv_w_dw, v_cv_b_dw, v_cv_ln_g, v_cv_ln_b, v_cv_w_pw2, v_cv_b_pw2, v_gdn_w_in, v_gdn_conv_w, v_gdn_a_log, v_gdn_dt_bias, v_gdn_norm_g, v_gdn_w_out, v_mlp_w1, v_mlp_w2):
    w_in = dict(zip(WEIGHTS, (norm_mix_g, norm_ffn_g, final_norm_g, cv_w_pw1, cv_b_pw1, cv_w_dw, cv_b_dw, cv_ln_g, cv_ln_b, cv_w_pw2, cv_b_pw2, gdn_w_in, gdn_conv_w, gdn_a_log, gdn_dt_bias, gdn_norm_g, gdn_w_out, mlp_w1, mlp_w2)))
    m_in = dict(zip(WEIGHTS, (m_norm_mix_g, m_norm_ffn_g, m_final_norm_g, m_cv_w_pw1, m_cv_b_pw1, m_cv_w_dw, m_cv_b_dw, m_cv_ln_g, m_cv_ln_b, m_cv_w_pw2, m_cv_b_pw2, m_gdn_w_in, m_gdn_conv_w, m_gdn_a_log, m_gdn_dt_bias, m_gdn_norm_g, m_gdn_w_out, m_mlp_w1, m_mlp_w2)))
    v_in = dict(zip(WEIGHTS, (v_norm_mix_g, v_norm_ffn_g, v_final_norm_g, v_cv_w_pw1, v_cv_b_pw1, v_cv_w_dw, v_cv_b_dw, v_cv_ln_g, v_cv_ln_b, v_cv_w_pw2, v_cv_b_pw2, v_gdn_w_in, v_gdn_conv_w, v_gdn_a_log, v_gdn_dt_bias, v_gdn_norm_g, v_gdn_w_out, v_mlp_w1, v_mlp_w2)))
    me = 4 * lax.axis_index("x") + 2 * lax.axis_index("y") + lax.axis_index("c")

    shards = {n: _squeeze_layer(n, w_in[n]) for n in WEIGHTS}
    first, need, token = _gather_weights(shards)
    params = {n: shards[n] for n in REPLICATED}
    params.update(first)
    params["norm_mix_g"] = params["norm_mix_g"] + token

    def row_blocks(a):
        return a.reshape(N_DEV, a.shape[0] // N_DEV, a.shape[1])

    def flat_blocks(a):
        k, n8 = a.shape
        return _cols_to_blocks(a).reshape(N_DEV, k * (n8 // N_DEV) // LANES, LANES)

    def as_blocks(n, a):
        if n == "gdn_w_in":
            return flat_blocks(a).astype(bf16)
        return a if a.ndim == 3 else row_blocks(a)

    sent = []

    small = REPLICATED + CONV_SHARDED
    small_early = tuple(n for n in small if n != "norm_mix_g") + ("loss",)
    small_info = {}

    def emit(group, grads_out):
        names, blocks, scatter = list(grads_out), [], []
        for n in names:
            if n == "small":
                packed, small_info["sizes"] = _pack([grads_out[n][k] for k in small_early])
                small_info["shapes"] = [grads_out[n][k].shape for k in small_early]
                blocks.append(packed)
                scatter.append(False)
            else:
                blocks.append(as_blocks(n, grads_out[n]))
                scatter.append(True)
        sent.append((names, _exchange_start(blocks, scatter, f"grads_{group}_start"), scatter))
        return sent[-1][1][4][0, 0]

    _, grad_x, gr = _local_step(x, loss_target, params, need, emit)
    nm_shape = gr["norm_mix_g"].shape
    sent.append((["norm_mix_g"], _exchange_start([gr["norm_mix_g"].reshape(-1, LANES)], [False], "grads_norm_mix_start"), [False]))

    recv = {}

    def finish(entry, after):
        names, st, scatter = entry
        srcs, lands = _exchange_wait(st, after, scatter, f"grads_{names[0]}_wait")
        for n, ld, blk, sc in zip(names, lands, srcs, scatter):
            own = lax.dynamic_index_in_dim(blk, me, 0, keepdims=False) if sc else blk
            recv[n] = _own_block(ld, own, me)
        return lands[0]

    def as3d(n, a):
        if n == "gdn_w_in":
            return a.reshape(a.shape[0], -1, LANES)
        return a

    big = [("cv_w_pw2", 0, "cv_w_pw2"), ("gdn_w_in", 0, "gdn_w_in"), ("gdn_w_out", 0, "gdn_w_out"),
           ("mlp_w1", 0, "mlp_w1_0"), ("mlp_w1", 1, "mlp_w1_1"), ("mlp_w2", 0, "mlp_w2_0"), ("mlp_w2", 1, "mlp_w2_1"),
           ("cv_w_pw1", 0, "cv_w_pw1")]
    res = {n: {} for n in MATMUL_SHARDED}
    after = grad_x
    for entry in sent[:-2]:
        after = finish(entry, after)
    for n, layer, key in big[:-1]:
        res[n][layer] = _sum8_adamw(recv[key], as3d(n, w_in[n]), as3d(n, m_in[n]), as3d(n, v_in[n]), layer, f"adamw_{key}")
        after = res[n][layer][0]
    after = finish(sent[-2], after)
    finish(sent[-1], after)
    n, layer, key = big[-1]
    res[n][layer] = _sum8_adamw(recv[key], as3d(n, w_in[n]), as3d(n, m_in[n]), as3d(n, v_in[n]), layer, f"adamw_{key}")

    grads = dict(zip(small_early, _unpack(_sum_devices(recv["small"], "grads_small_sum"), small_info["sizes"], small_info["shapes"])))
    grads["norm_mix_g"] = _sum_devices(recv["norm_mix_g"], "grads_norm_mix_sum").reshape(nm_shape)
    loss = grads["loss"].reshape(())
    for n in CONV_SHARDED:
        cn = shards[n].shape[-1]
        grads[n] = lax.dynamic_slice_in_dim(grads[n], me * cn, cn, axis=1)

    out_groups = {n: [] for n in WEIGHTS}
    for n in MATMUL_SHARDED:
        layers = sorted(res[n])
        for k in range(4):
            pieces = [res[n][layer][k] for layer in layers]
            out_groups[n].append(jnp.stack(pieces).reshape(w_in[n].shape))

    sm_w = [shards[n] for n in small]
    sm_g = [grads[n].reshape(shards[n].shape) for n in small]
    sm_m = [_squeeze_layer(n, m_in[n]) for n in small]
    sm_v = [_squeeze_layer(n, v_in[n]) for n in small]
    wp, psz = _pack(sm_w)
    gp, _ = _pack(sm_g)
    mp, _ = _pack(sm_m)
    vp, _ = _pack(sm_v)
    dp, nmp, nvp = _adamw(wp, gp, mp, vp)
    shp = [a.shape for a in sm_w]
    for n, g, dl, nm, nv in zip(small, sm_g, _unpack(dp, psz, shp), _unpack(nmp, psz, shp), _unpack(nvp, psz, shp)):
        out_groups[n] = [a.reshape(w_in[n].shape) for a in (g, dl, nm, nv)]

    outs = [loss, grad_x]
    for k in range(4):
        outs += [out_groups[n][k] for n in WEIGHTS]
    return tuple(outs)
```

```python
import jax
import jax.numpy as jnp
from jax import lax
from jax.experimental import pallas as pl
from jax.experimental.pallas import tpu as pltpu

f32, bf16 = jnp.float32, jnp.bfloat16

NORM_EPS = 1e-6
L2_EPS = 1e-6
CHUNK = 64
LANES = 128
SUBLANES = 8
N_DEV = 8
VMEM_LIMIT = 56 * 1024 * 1024
ROW_TILE = 512
CONV_PAD = 32
HEADS_PER_STEP = 8
PREP_HEADS_PER_STEP = 4
NEG = -1e30

ADAM_LR, ADAM_B1, ADAM_B2, ADAM_EPS, ADAM_WD, ADAM_STEP = 0.001, 0.9, 0.999, 1e-08, 0.01, 10

NT = (((1,), (1,)), ((), ()))
TN = (((0,), (0,)), ((), ()))
HI = lax.Precision.HIGHEST


def _pc(body, *, name, grid, in_specs, out_specs, out_shape, scratch=(), sem=None):
    return pl.pallas_call(
        body, name=name, grid=grid, in_specs=in_specs, out_specs=out_specs, out_shape=out_shape,
        scratch_shapes=list(scratch),
        compiler_params=pltpu.CompilerParams(dimension_semantics=sem, vmem_limit_bytes=VMEM_LIMIT))


def _rows(tm, n):
    return pl.BlockSpec((tm, n), lambda i: (i, 0))


def _const(shape):
    return pl.BlockSpec(shape, lambda *_: (0,) * len(shape))


def _resident(shape):
    return pl.BlockSpec(shape, lambda *_: (0,) * len(shape), pipeline_mode=pl.Buffered(1))


def _tile(t, pref):
    return pref if t % pref == 0 else t


def _dot(a, b):
    return jnp.dot(a.astype(bf16), b.astype(bf16), preferred_element_type=f32)


def _dot_nt(a, b):
    return lax.dot_general(a.astype(bf16), b.astype(bf16), NT, preferred_element_type=f32)


def _dot_tn(a, b):
    return lax.dot_general(a.astype(bf16), b.astype(bf16), TN, preferred_element_type=f32)


def _sigmoid(x):
    return 1.0 / (1.0 + jnp.exp(-x))


def _silu_grad(x):
    s = _sigmoid(x)
    return s * (1.0 + x * (1.0 - s))


def _rms(x, g):
    rstd = lax.rsqrt(jnp.mean(x * x, axis=-1, keepdims=True) + NORM_EPS)
    xh = x * rstd
    return xh * g, xh, rstd


def _rms_bwd(dn, xh, rstd, g):
    dxh = dn * g
    return rstd * (dxh - xh * jnp.mean(dxh * xh, axis=-1, keepdims=True))


def _acc_init(step, *refs):
    @pl.when(step == 0)
    def _():
        for r in refs:
            r[...] = jnp.zeros(r.shape, r.dtype)


def _acc_rows(ref, val):
    ref[0:1, :] += jnp.sum(val, axis=0, keepdims=True)


def _pw1_glu(x, g, w, b):
    t, d = x.shape
    tm = _tile(t, ROW_TILE)

    nb_w = w.shape[0]

    def body(x_ref, g_ref, w_hbm, b_ref, n_ref, u_ref, gl_ref, w_ref, sems):
        _fetch_blocks(pl.program_id(0), w_hbm, w_ref, sems, True)
        n, _, _ = _rms(x_ref[...], g_ref[...])
        nb = n.astype(bf16)
        n_ref[...] = nb
        u = jnp.dot(nb, w_ref[...], preferred_element_type=f32) + b_ref[...]
        u_ref[...] = u.astype(bf16)
        gl_ref[...] = u[:, :d] * _sigmoid(u[:, d:])

    return _pc(body, name="pw1_glu", grid=(t // tm,),
               in_specs=[_rows(tm, d), _const((1, d)), ANY, _const((1, 2 * d))],
               out_specs=[_rows(tm, d), _rows(tm, 2 * d), _rows(tm, d)],
               out_shape=[jax.ShapeDtypeStruct((t, d), bf16), jax.ShapeDtypeStruct((t, 2 * d), bf16),
                          jax.ShapeDtypeStruct((t, d), f32)],
               scratch=[pltpu.VMEM((d, 2 * d), bf16), pltpu.SemaphoreType.DMA((nb_w,))],
               sem=("arbitrary",))(x, g, w, b)


def _ln_silu_mm_res(dc, ln_g, ln_b, w, b, res):
    t, d = dc.shape
    tm = _tile(t, ROW_TILE)

    def body(x_ref, g_ref, bb_ref, w_ref, b_ref, r_ref, s_ref, o_ref):
        x = x_ref[...]
        xc = x - jnp.mean(x, axis=-1, keepdims=True)
        rstd = lax.rsqrt(jnp.mean(xc * xc, axis=-1, keepdims=True) + NORM_EPS)
        ln = xc * rstd * g_ref[...] + bb_ref[...]
        sb = (ln * _sigmoid(ln)).astype(bf16)
        s_ref[...] = sb
        o_ref[...] = r_ref[...] + jnp.dot(sb, w_ref[...], preferred_element_type=f32) + b_ref[...]

    return _pc(body, name="ln_silu_pw2", grid=(t // tm,),
               in_specs=[_rows(tm, d), _const((1, d)), _const((1, d)), _resident((d, d)), _const((1, d)), _rows(tm, d)],
               out_specs=[_rows(tm, d), _rows(tm, d)],
               out_shape=[jax.ShapeDtypeStruct((t, d), bf16), jax.ShapeDtypeStruct((t, d), f32)],
               sem=("parallel",))(dc, ln_g, ln_b, w, b, res)


def _fetch_blocks(step, w_hbm, dst, sems, by_cols, layer=None):
    nb_w = w_hbm.shape[0]
    step_rows, step_cols = w_hbm.shape[-2], w_hbm.shape[-1]

    @pl.when(step == 0)
    def _():
        copies = []
        for j in range(nb_w):
            src = w_hbm.at[j] if layer is None else w_hbm.at[j, layer]
            if by_cols:
                part = dst.at[:, pl.ds(j * step_cols, step_cols)]
            else:
                part = dst.at[pl.ds(j * step_rows, step_rows), :]
            copies.append(pltpu.make_async_copy(src, part, sems.at[j]))
        for cp in copies:
            cp.start()
        for cp in copies:
            cp.wait()


def _mlp_fwd(h, g, w1g, w2g, name, loss=None):
    t, d = h.shape
    nb_w, _, bn = w1g.shape
    ff = nb_w * bn
    tm = _tile(t, 256)
    n_in = 4 if loss is None else 6
    n_out = 4 if loss is None else 6

    def body(*refs):
        h_ref, g_ref, w1_hbm, w2_hbm = refs[:4]
        n_ref, f_ref, r_ref = refs[n_in:n_in + 3]
        w1_ref, w2_ref, sem1, sem2 = refs[n_in + n_out:]
        _fetch_blocks(pl.program_id(0), w1_hbm, w1_ref, sem1, True)
        _fetch_blocks(pl.program_id(0), w2_hbm, w2_ref, sem2, False)
        hv = h_ref[...]
        n, _, _ = _rms(hv, g_ref[...])
        nb = n.astype(bf16)
        n_ref[...] = nb
        f = jnp.dot(nb, w1_ref[...], preferred_element_type=f32)
        f_ref[...] = f.astype(bf16)
        rb = jnp.square(jnp.maximum(f, 0.0)).astype(bf16)
        r_ref[...] = rb
        out = hv + jnp.dot(rb, w2_ref[...], preferred_element_type=f32)
        if loss is None:
            refs[n_in + 3][...] = out
        else:
            gf_ref, t_ref = refs[4:6]
            loss_ref, dh_ref, dg_ref = refs[n_in + 3:n_in + 6]
            _acc_init(pl.program_id(0), loss_ref, dg_ref)
            gv = gf_ref[...]
            y, xh, rstd = _rms(out, gv)
            e = y - t_ref[...]
            loss_ref[...] += 0.5 * jnp.sum(jnp.mean(e * e, axis=-1, keepdims=True))
            dy = e * (1.0 / d)
            _acc_rows(dg_ref, dy * xh)
            dh_ref[...] = _rms_bwd(dy, xh, rstd, gv)

    in_specs = [_rows(tm, d), _const((1, d)), ANY, ANY]
    out_specs = [_rows(tm, d), _rows(tm, ff), _rows(tm, ff)]
    out_shape = [jax.ShapeDtypeStruct((t, d), bf16), jax.ShapeDtypeStruct((t, ff), bf16), jax.ShapeDtypeStruct((t, ff), bf16)]
    args = [h, g, w1g, w2g]
    if loss is None:
        out_specs.append(_rows(tm, d))
        out_shape.append(jax.ShapeDtypeStruct((t, d), f32))
    else:
        in_specs += [_const((1, d)), _rows(tm, d)]
        args += list(loss)
        out_specs += [_const((SUBLANES, LANES)), _rows(tm, d), _const((SUBLANES, d))]
        out_shape += [jax.ShapeDtypeStruct((SUBLANES, LANES), f32), jax.ShapeDtypeStruct((t, d), f32),
                      jax.ShapeDtypeStruct((SUBLANES, d), f32)]
    return _pc(body, name=name, grid=(t // tm,), in_specs=in_specs, out_specs=out_specs, out_shape=out_shape,
               scratch=[pltpu.VMEM((d, ff), bf16), pltpu.VMEM((ff, d), bf16),
                        pltpu.SemaphoreType.DMA((nb_w,)), pltpu.SemaphoreType.DMA((nb_w,))],
               sem=("arbitrary",))(*args)


def _softplus(x):
    return jnp.maximum(x, 0.0) + jnp.log(1.0 + jnp.exp(-jnp.abs(x)))


def _gdn_in(h, g, w_main, w_ab, a_log_pad, dt_pad, n_heads):
    t, d = h.shape
    tm = _tile(t, ROW_TILE)
    gate_rows = 2 * n_heads

    def body(h_ref, g_ref, wm_ref, wab_ref, al_ref, dt_ref, n_ref, qkv_ref, z_ref, ab_ref, gb_ref):
        n, _, _ = _rms(h_ref[...], g_ref[...])
        nb = n.astype(bf16)
        n_ref[...] = nb
        p = jnp.dot(nb, wm_ref[...], preferred_element_type=f32)
        qkv_ref[...] = p[:, :3 * d]
        z_ref[...] = p[:, 3 * d:]
        ab = jnp.dot(nb, wab_ref[...], preferred_element_type=f32)
        ab_ref[...] = ab
        lane = lax.broadcasted_iota(jnp.int32, ab.shape, 1)
        decay = -jnp.exp(al_ref[...]) * _softplus(ab + dt_ref[...])
        gates = jnp.where(lane < n_heads, decay, jnp.where(lane < gate_rows, _sigmoid(ab), 0.0))
        gb_ref[...] = gates.T[:gate_rows, :]

    return _pc(body, name="gdn_in", grid=(t // tm,),
               in_specs=[_rows(tm, d), _const((1, d)), _resident((d, 4 * d)), _resident((d, LANES)),
                         _const((1, LANES)), _const((1, LANES))],
               out_specs=[_rows(tm, d), _rows(tm, 3 * d), _rows(tm, d), _rows(tm, LANES),
                          pl.BlockSpec((gate_rows, tm), lambda i: (0, i))],
               out_shape=[jax.ShapeDtypeStruct((t, d), bf16), jax.ShapeDtypeStruct((t, 3 * d), f32),
                          jax.ShapeDtypeStruct((t, d), f32), jax.ShapeDtypeStruct((t, LANES), f32),
                          jax.ShapeDtypeStruct((gate_rows, t), f32)],
               sem=("parallel",))(h, g, w_main, w_ab, a_log_pad, dt_pad)


def _gated_norm_mm_res(o, z, ng, w, res, n_heads):
    t, d = o.shape
    tm = _tile(t, ROW_TILE)

    def body(o_ref, z_ref, ng_ref, w_ref, r_ref, on_ref, out_ref):
        for hd in range(n_heads):
            sl = slice(hd * LANES, (hd + 1) * LANES)
            rn, _, _ = _rms(o_ref[:, sl], ng_ref[...])
            zz = z_ref[:, sl]
            on_ref[:, sl] = (rn * (zz * _sigmoid(zz))).astype(bf16)
        out_ref[...] = r_ref[...] + jnp.dot(on_ref[...], w_ref[...], preferred_element_type=f32)

    return _pc(body, name="gated_norm_wout", grid=(t // tm,),
               in_specs=[_rows(tm, d), _rows(tm, d), _const((1, LANES)), _resident((d, d)), _rows(tm, d)],
               out_specs=[_rows(tm, d), _rows(tm, d)],
               out_shape=[jax.ShapeDtypeStruct((t, d), bf16), jax.ShapeDtypeStruct((t, d), f32)],
               sem=("parallel",))(o, z, ng, w, res)


def _mlp_bwd(dho, h, g, fb, w1g, w2g, name):
    t, d = h.shape
    nb_w, _, bn = w1g.shape
    ff = nb_w * bn
    tm = _tile(t, 256)

    def body(do_ref, h_ref, g_ref, f_ref, w1_hbm, w2_hbm, df_ref, dh_ref, dg_ref, cs_ref, w1_ref, w2_ref, sem1, sem2):
        _fetch_blocks(pl.program_id(0), w1_hbm, w1_ref, sem1, True)
        _fetch_blocks(pl.program_id(0), w2_hbm, w2_ref, sem2, False)
        _acc_init(pl.program_id(0), dg_ref, cs_ref)
        do = do_ref[...]
        dr = lax.dot_general(do.astype(bf16), w2_ref[...], NT, preferred_element_type=f32)
        dfb = (dr * (2.0 * jnp.maximum(f_ref[...].astype(f32), 0.0))).astype(bf16)
        df_ref[...] = dfb
        dn = lax.dot_general(dfb, w1_ref[...], NT, preferred_element_type=f32)
        gv = g_ref[...]
        _, xh, rstd = _rms(h_ref[...], gv)
        _acc_rows(dg_ref, dn * xh)
        dh = do + _rms_bwd(dn, xh, rstd, gv)
        dh_ref[...] = dh
        _acc_rows(cs_ref, dh)

    return _pc(body, name=name, grid=(t // tm,),
               in_specs=[_rows(tm, d), _rows(tm, d), _const((1, d)), _rows(tm, ff), ANY, ANY],
               out_specs=[_rows(tm, ff), _rows(tm, d), _const((SUBLANES, d)), _const((SUBLANES, d))],
               out_shape=[jax.ShapeDtypeStruct((t, ff), bf16), jax.ShapeDtypeStruct((t, d), f32),
                          jax.ShapeDtypeStruct((SUBLANES, d), f32), jax.ShapeDtypeStruct((SUBLANES, d), f32)],
               scratch=[pltpu.VMEM((d, ff), bf16), pltpu.VMEM((ff, d), bf16),
                        pltpu.SemaphoreType.DMA((nb_w,)), pltpu.SemaphoreType.DMA((nb_w,))],
               sem=("arbitrary",))(dho, h, g, fb, w1g, w2g)


class _Tail:
    def __init__(self, fn, ins, outs):
        self.fn, self.ins, self.outs = fn, ins, outs


def _mm_nt(pairs, name, tail=None):
    t = pairs[0][0].shape[0]
    tm = _tile(t, ROW_TILE)
    npair = len(pairs)
    in_specs, args, scratch, blocked = [], [], [], []
    k = None
    for dy, w in pairs:
        nn = dy.shape[1]
        if isinstance(w, tuple) and isinstance(w[0], str):
            w = w[1]
            k = w.shape[1]
            wspec = ANY
            blocked.append(True)
            scratch += [pltpu.VMEM((k, nn), bf16), pltpu.SemaphoreType.DMA((w.shape[0],))]
        elif isinstance(w, tuple):
            w, idx = w
            k = w.shape[0]
            wspec = pl.BlockSpec((k, nn), lambda *_, idx=idx: (0, idx), pipeline_mode=pl.Buffered(1))
            blocked.append(False)
        else:
            k = w.shape[0]
            wspec = _resident(w.shape)
            blocked.append(False)
        in_specs += [_rows(tm, nn), wspec]
        args += [dy, w]
    n_tin = len(tail.ins) if tail else 0
    n_out = len(tail.outs) if tail else 1
    if tail:
        for arr, kind in tail.ins:
            in_specs.append(_rows(tm, arr.shape[1]) if kind == "rows" else _const(arr.shape))
            args.append(arr)
        out_specs = [_rows(tm, c) if kind == "rows" else _const((SUBLANES, c)) for c, kind in tail.outs]
        out_shape = [jax.ShapeDtypeStruct((t, c) if kind == "rows" else (SUBLANES, c), f32) for c, kind in tail.outs]
    else:
        out_specs = _rows(tm, k)
        out_shape = jax.ShapeDtypeStruct((t, k), f32)

    def body(*refs):
        step = pl.program_id(0)
        tin = refs[2 * npair:2 * npair + n_tin]
        outs = refs[2 * npair + n_tin:2 * npair + n_tin + n_out]
        scr = list(refs[2 * npair + n_tin + n_out:])
        acc = None
        for p in range(npair):
            w_ref = refs[2 * p + 1]
            if blocked[p]:
                w_vmem, sems = scr.pop(0), scr.pop(0)
                _fetch_blocks(step, w_ref, w_vmem, sems, True)
                w_ref = w_vmem
            part = lax.dot_general(refs[2 * p][...].astype(bf16), w_ref[...], NT, preferred_element_type=f32)
            acc = part if acc is None else acc + part
        if tail is None:
            outs[0][...] = acc
        else:
            _acc_init(step, *[o for o, (_, kind) in zip(outs, tail.outs) if kind == "acc"])
            tail.fn(acc, tin, outs)

    sequential = tail is not None or any(blocked)
    return _pc(body, name=name, grid=(t // tm,), in_specs=in_specs, out_specs=out_specs, out_shape=out_shape,
               scratch=scratch, sem=("arbitrary",) if sequential else ("parallel",))(*args)


def _rms_bwd_tail(h, g, dres):
    def fn(dn, ins, outs):
        h_ref, g_ref, dr_ref = ins
        dh_ref, dg_ref = outs
        gv = g_ref[...]
        _, xh, rstd = _rms(h_ref[...], gv)
        _acc_rows(dg_ref, dn * xh)
        dh_ref[...] = dr_ref[...] + _rms_bwd(dn, xh, rstd, gv)

    d = h.shape[1]
    return _Tail(fn, [(h, "rows"), (g, "const"), (dres, "rows")], [(d, "rows"), (d, "acc")])


def _ln_silu_bwd_tail(dc, ln_g, ln_b):
    def fn(ds, ins, outs):
        x_ref, g_ref, b_ref = ins
        dx_ref, dg_ref, db_ref, cs_ref = outs
        x = x_ref[...]
        gv = g_ref[...]
        xc = x - jnp.mean(x, axis=-1, keepdims=True)
        rstd = lax.rsqrt(jnp.mean(xc * xc, axis=-1, keepdims=True) + NORM_EPS)
        xh = xc * rstd
        dln = ds * _silu_grad(xh * gv + b_ref[...])
        _acc_rows(dg_ref, dln * xh)
        _acc_rows(db_ref, dln)
        dxh = dln * gv
        dx = rstd * (dxh - jnp.mean(dxh, axis=-1, keepdims=True) - xh * jnp.mean(dxh * xh, axis=-1, keepdims=True))
        dx_ref[...] = dx
        _acc_rows(cs_ref, dx)

    d = dc.shape[1]
    return _Tail(fn, [(dc, "rows"), (ln_g, "const"), (ln_b, "const")], [(d, "rows"), (d, "acc"), (d, "acc"), (d, "acc")])


def _gated_norm_bwd_tail(o, z, ng, n_heads):
    def fn(don_all, ins, outs):
        o_ref, z_ref, ng_ref = ins
        do_ref, dz_ref, dng_ref = outs
        gv = ng_ref[...]
        for hd in range(n_heads):
            sl = slice(hd * LANES, (hd + 1) * LANES)
            rn, xh, rstd = _rms(o_ref[:, sl], gv)
            zz = z_ref[:, sl]
            don = don_all[:, sl]
            dz_ref[:, sl] = don * rn * _silu_grad(zz)
            drn = don * (zz * _sigmoid(zz))
            _acc_rows(dng_ref, drn * xh)
            do_ref[:, sl] = _rms_bwd(drn, xh, rstd, gv)

    d = o.shape[1]
    return _Tail(fn, [(o, "rows"), (z, "rows"), (ng, "const")], [(d, "rows"), (d, "rows"), (LANES, "acc")])


def _mm_tn_blocked(x, dy, name, out_dtype=f32):
    t, k = x.shape
    bn = dy.shape[1] // N_DEV
    tm = _tile(t, 2048 if k <= 1024 else 1024)
    jb = N_DEV
    while jb > 1 and k * jb * bn * 4 > 8 * 1024 * 1024:
        jb //= 2
    nt = t // tm

    def body(x_ref, dy_ref, o_ref, *acc):
        acc_ref = acc[0] if acc else o_ref
        _acc_init(pl.program_id(1), acc_ref)
        xt = x_ref[...].astype(bf16).T
        for jj in range(jb):
            acc_ref[jj] += jnp.dot(xt, dy_ref[:, jj * bn:(jj + 1) * bn].astype(bf16), preferred_element_type=f32)
        if acc:
            @pl.when(pl.program_id(1) == nt - 1)
            def _():
                o_ref[...] = acc_ref[...].astype(out_dtype)

    return _pc(body, name=name, grid=(N_DEV // jb, nt),
               in_specs=[pl.BlockSpec((tm, k), lambda j, i: (i, 0)), pl.BlockSpec((tm, jb * bn), lambda j, i: (i, j))],
               out_specs=pl.BlockSpec((jb, k, bn), lambda j, i: (j, 0, 0)),
               out_shape=jax.ShapeDtypeStruct((N_DEV, k, bn), out_dtype),
               scratch=[] if out_dtype == f32 else [pltpu.VMEM((jb, k, bn), f32)],
               sem=("parallel", "arbitrary"))(x, dy)


def _mm_tn(x, dy, name, out_dtype=f32):
    t, k = x.shape
    n = dy.shape[1]
    tm = _tile(t, 2048 if k <= 1024 else 1024)
    cap = max(LANES, (2 * 1024 * 1024) // k)
    tn = n
    if n > cap:
        tn = max(c for c in range(LANES, cap + 1, LANES) if n % c == 0)
    nt = t // tm

    def body(x_ref, dy_ref, o_ref, *acc):
        acc_ref = acc[0] if acc else o_ref
        _acc_init(pl.program_id(1), acc_ref)
        acc_ref[...] += lax.dot_general(x_ref[...].astype(bf16), dy_ref[...].astype(bf16), TN, preferred_element_type=f32)
        if acc:
            @pl.when(pl.program_id(1) == nt - 1)
            def _():
                o_ref[...] = acc_ref[...].astype(out_dtype)

    return _pc(body, name=name, grid=(n // tn, nt),
               in_specs=[pl.BlockSpec((tm, k), lambda j, i: (i, 0)), pl.BlockSpec((tm, tn), lambda j, i: (i, j))],
               out_specs=pl.BlockSpec((k, tn), lambda j, i: (0, j)),
               out_shape=jax.ShapeDtypeStruct((k, n), out_dtype),
               scratch=[] if out_dtype == f32 else [pltpu.VMEM((k, tn), f32)],
               sem=("parallel", "arbitrary"))(x, dy)


def _gates_bwd(dgb_t, ab, a_log_pad, dt_pad, n_heads):
    t = ab.shape[0]
    tm = _tile(t, ROW_TILE)
    gate_rows = 2 * n_heads

    def body(dgb_ref, ab_ref, al_ref, dt_ref, dab_ref, dal_ref, ddt_ref):
        _acc_init(pl.program_id(0), dal_ref, ddt_ref)
        ab = ab_ref[...]
        dgb = jnp.concatenate([dgb_ref[...], jnp.zeros((LANES - gate_rows, tm), f32)], axis=0).T
        lane = lax.broadcasted_iota(jnp.int32, ab.shape, 1)
        is_a = lane < n_heads
        is_b = jnp.logical_and(lane >= n_heads, lane < 2 * n_heads)
        xa = ab + dt_ref[...]
        neg_a = -jnp.exp(al_ref[...])
        dg_da = neg_a * _sigmoid(xa)
        beta = _sigmoid(ab)
        da = jnp.where(is_a, dgb * dg_da, 0.0)
        dab_ref[...] = da + jnp.where(is_b, dgb * beta * (1.0 - beta), 0.0)
        _acc_rows(dal_ref, jnp.where(is_a, dgb * neg_a * _softplus(xa), 0.0))
        _acc_rows(ddt_ref, da)

    return _pc(body, name="gates_bwd", grid=(t // tm,),
               in_specs=[pl.BlockSpec((gate_rows, tm), lambda i: (0, i)), _rows(tm, LANES), _const((1, LANES)), _const((1, LANES))],
               out_specs=[_rows(tm, LANES), _const((SUBLANES, LANES)), _const((SUBLANES, LANES))],
               out_shape=[jax.ShapeDtypeStruct((t, LANES), f32), jax.ShapeDtypeStruct((SUBLANES, LANES), f32),
                          jax.ShapeDtypeStruct((SUBLANES, LANES), f32)],
               sem=("arbitrary",))(dgb_t, ab, a_log_pad, dt_pad)


def _glu_bwd(dgl, ub):
    t, d = dgl.shape
    tm = _tile(t, ROW_TILE)

    def body(dgl_ref, u_ref, du_ref, cs_ref):
        _acc_init(pl.program_id(0), cs_ref)
        dgl = dgl_ref[...]
        a = u_ref[:, :d].astype(f32)
        sb = _sigmoid(u_ref[:, d:].astype(f32))
        da = dgl * sb
        db = dgl * a * sb * (1.0 - sb)
        du_ref[:, :d] = da.astype(bf16)
        du_ref[:, d:] = db.astype(bf16)
        cs_ref[0:1, :d] += jnp.sum(da, axis=0, keepdims=True)
        cs_ref[0:1, d:] += jnp.sum(db, axis=0, keepdims=True)

    return _pc(body, name="glu_bwd", grid=(t // tm,),
               in_specs=[_rows(tm, d), _rows(tm, 2 * d)],
               out_specs=[_rows(tm, 2 * d), _const((SUBLANES, 2 * d))],
               out_shape=[jax.ShapeDtypeStruct((t, 2 * d), bf16), jax.ShapeDtypeStruct((SUBLANES, 2 * d), f32)],
               sem=("arbitrary",))(dgl, ub)


def _conv_rows(s):
    return 256 if s % 256 == 0 else s


def _conv_tap_sum(pad_ref, w_ref, base, rows, width):
    acc = jnp.zeros((rows, LANES), f32)
    for j in range(width):
        acc = acc + w_ref[j:j + 1, :] * pad_ref[pl.ds(base + CONV_PAD - (width - 1) + j, rows), :]
    return acc


def _l2_silu_post(c, j, n_heads, scale):
    a = c * _sigmoid(c)
    r = lax.rsqrt(jnp.sum(a * a, axis=-1, keepdims=True) + L2_EPS)
    mult = jnp.where(j < n_heads, r * scale, jnp.where(j < 2 * n_heads, r, 1.0))
    return a, r, a * mult


def _dwconv_fwd(x, w, b, name, qk_heads=None):
    bl, s, cn = x.shape
    width = w.shape[0]
    rows = _conv_rows(s)
    scale = float(LANES) ** -0.5

    def body(x_ref, w_ref, b_ref, o_ref, pad_ref):
        j = pl.program_id(1)
        pad_ref[0:CONV_PAD, :] = jnp.zeros((CONV_PAD, LANES), f32)
        pad_ref[CONV_PAD:, :] = x_ref[0]

        def step(i, carry):
            base = pl.multiple_of(i * rows, rows)
            acc = _conv_tap_sum(pad_ref, w_ref, base, rows, width)
            if qk_heads is None:
                acc = acc + b_ref[...]
            else:
                _, _, acc = _l2_silu_post(acc, j, qk_heads, scale)
            o_ref[0, pl.ds(base, rows), :] = acc
            return carry

        lax.fori_loop(0, s // rows, step, 0)

    return _pc(body, name=name, grid=(bl, cn // LANES),
               in_specs=[pl.BlockSpec((1, s, LANES), lambda bi, j: (bi, 0, j)),
                         pl.BlockSpec((width, LANES), lambda bi, j: (0, j)),
                         pl.BlockSpec((1, LANES), lambda bi, j: (0, j))],
               out_specs=pl.BlockSpec((1, s, LANES), lambda bi, j: (bi, 0, j)),
               out_shape=jax.ShapeDtypeStruct((bl, s, cn), f32),
               scratch=[pltpu.VMEM((s + CONV_PAD, LANES), f32)],
               sem=("parallel", "parallel"))(x, w, b)


def _dwconv_bwd(x, dys, w, name, qk_heads=None):
    bl, s, cn = x.shape
    width = w.shape[0]
    wp = -(-width // SUBLANES) * SUBLANES
    rows = _conv_rows(s)
    scale = float(LANES) ** -0.5
    nblk = s // rows
    ndy = len(dys)

    def body(*refs):
        x_ref, w_ref = refs[0], refs[1]
        dy_refs = refs[2:2 + ndy]
        dx_ref, dw_ref, db_ref, xpad, dypad, acc = refs[2 + ndy:]
        j = pl.program_id(0)
        bi = pl.program_id(1)
        _acc_init(bi, acc, db_ref)
        xpad[0:CONV_PAD, :] = jnp.zeros((CONV_PAD, LANES), f32)
        xpad[CONV_PAD:, :] = x_ref[0]
        dypad[s:, :] = jnp.zeros((CONV_PAD, LANES), f32)
        if qk_heads is None:
            dypad[0:s, :] = dy_refs[0][0]
        else:
            def pre(i, carry):
                base = pl.multiple_of(i * rows, rows)
                c = _conv_tap_sum(xpad, w_ref, base, rows, width)
                a, r, _ = _l2_silu_post(c, j, qk_heads, scale)
                dq = dy_refs[0][0, pl.ds(base, rows), :]
                dk = dy_refs[1][0, pl.ds(base, rows), :]
                dv = dy_refs[2][0, pl.ds(base, rows), :]
                dy = jnp.where(j < qk_heads, dq * scale, jnp.where(j < 2 * qk_heads, dk, dv))
                da_l2 = r * (dy - a * (r * r) * jnp.sum(a * dy, axis=-1, keepdims=True))
                da = jnp.where(j < 2 * qk_heads, da_l2, dy)
                dypad[pl.ds(base, rows), :] = da * _silu_grad(c)
                return carry

            lax.fori_loop(0, nblk, pre, 0)

        def step(i, carry):
            base = pl.multiple_of(i * rows, rows)
            dxa = jnp.zeros((rows, LANES), f32)
            for jj in range(width):
                dxa = dxa + w_ref[jj:jj + 1, :] * dypad[pl.ds(base + (width - 1) - jj, rows), :]
            dx_ref[0, pl.ds(base, rows), :] = dxa
            dyc = dypad[pl.ds(base, rows), :]
            db_ref[...] += dyc.reshape(rows // SUBLANES, SUBLANES, LANES).sum(axis=0)
            for jj in range(width):
                prod = dyc * xpad[pl.ds(base + CONV_PAD - (width - 1) + jj, rows), :]
                acc[jj * SUBLANES:(jj + 1) * SUBLANES, :] += prod.reshape(rows // SUBLANES, SUBLANES, LANES).sum(axis=0)
            return carry

        lax.fori_loop(0, nblk, step, 0)

        @pl.when(bi == bl - 1)
        def _():
            dw_ref[...] = jnp.zeros((wp, LANES), f32)
            for jj in range(width):
                dw_ref[jj:jj + 1, :] = jnp.sum(acc[jj * SUBLANES:(jj + 1) * SUBLANES, :], axis=0, keepdims=True)

    if qk_heads is None:
        dy_specs = [pl.BlockSpec((1, s, LANES), lambda j, bi: (bi, 0, j))]
    else:
        hh = qk_heads
        def dy_spec(part):
            def index(j, bi):
                mine = jnp.logical_and(j >= part * hh, j < (part + 1) * hh)
                return (jnp.where(mine, bi * hh + j - part * hh, 0), 0, 0)
            return pl.BlockSpec((1, s, LANES), index)

        dy_specs = [dy_spec(0), dy_spec(1), dy_spec(2)]
    return _pc(body, name=name, grid=(cn // LANES, bl),
               in_specs=[pl.BlockSpec((1, s, LANES), lambda j, bi: (bi, 0, j)),
                         pl.BlockSpec((width, LANES), lambda j, bi: (0, j))] + dy_specs,
               out_specs=[pl.BlockSpec((1, s, LANES), lambda j, bi: (bi, 0, j)),
                          pl.BlockSpec((wp, LANES), lambda j, bi: (0, j)),
                          pl.BlockSpec((SUBLANES, LANES), lambda j, bi: (0, j))],
               out_shape=[jax.ShapeDtypeStruct((bl, s, cn), f32), jax.ShapeDtypeStruct((wp, cn), f32),
                          jax.ShapeDtypeStruct((SUBLANES, cn), f32)],
               scratch=[pltpu.VMEM((s + CONV_PAD, LANES), f32), pltpu.VMEM((s + CONV_PAD, LANES), f32),
                        pltpu.VMEM((width * SUBLANES, LANES), f32)],
               sem=("parallel", "arbitrary"))(x, w, *dys)


def _group_rows(s):
    for rows in (256, 128):
        if s % rows == 0:
            return rows
    return CHUNK


def _group_masks(rows):
    r = lax.broadcasted_iota(jnp.int32, (rows, rows), 0)
    c = lax.broadcasted_iota(jnp.int32, (rows, rows), 1)
    return r, c, r >= c


def _decay(gc_col, gc_row, causal):
    return jnp.exp(jnp.where(causal, gc_col - gc_row, NEG))


def _inv_unit_lower_many(mats, r, c):
    n = r.shape[0]
    eye = (r == c).astype(f32)
    same16 = (r >> 4) == (c >> 4)
    ads = [jnp.where(same16, a, 0.0) for a in mats]
    aos = [a - ad for a, ad in zip(mats, ads)]
    xs = ads
    tds = [-x for x in xs]
    for _ in range(3):
        xs = [_dot(x, x) for x in xs]
        tds = [td + x + _dot(td, x) for td, x in zip(tds, xs)]
    bs = [ao + _dot(td, ao) for td, ao in zip(tds, aos)]
    ps = [-b for b in bs]
    span = 2
    while span < n // 16:
        bs = [_dot(b, b) for b in bs]
        ps = [p + b + _dot(p, b) for p, b in zip(ps, bs)]
        span *= 2
    return [p + td + _dot(p, td) for p, td in zip(ps, tds)]


def _lane_cumsum(x, y, reverse, name):
    rr, n = x.shape

    def body(x_ref, y_ref, o_ref):
        i = lax.broadcasted_iota(jnp.int32, (n, n), 0)
        j = lax.broadcasted_iota(jnp.int32, (n, n), 1)
        tri = ((i >= j) if reverse else (i <= j)).astype(f32)
        o_ref[...] = jnp.dot(x_ref[...] + y_ref[...], tri, precision=HI, preferred_element_type=f32)

    spec = pl.BlockSpec((rr, n), lambda: (0, 0))
    return pl.pallas_call(body, name=name, in_specs=[spec, spec], out_specs=spec,
                          out_shape=jax.ShapeDtypeStruct((rr, n), f32))(x, y)


def _gdn_specs(n_heads, nblk, rows, hp=1, rev=False):
    def blk(n):
        return nblk - 1 - n if rev else n

    def qkv(off):
        return pl.BlockSpec((rows, hp * LANES), lambda g, n: (
            lax.div(g * hp, n_heads) * nblk + blk(n), lax.div(off * n_heads + lax.rem(g * hp, n_heads), hp)))

    def per_head(last, block_rows=rows):
        return pl.BlockSpec((hp, block_rows, last), lambda g, n: (g, blk(n), 0))

    def row_vec():
        return pl.BlockSpec((hp, 1, 1, rows), lambda g, n: (g, blk(n), 0, 0))

    return qkv, per_head, row_vec


def _gdn_prep(qkv, cols, grow, bl, s, n_heads):
    rows = _group_rows(s)
    nblk = s // rows
    bh_n = bl * n_heads
    hp = PREP_HEADS_PER_STEP if n_heads % PREP_HEADS_PER_STEP == 0 else 1
    qkv_spec, ph, rv = _gdn_specs(n_heads, nblk, rows, hp)

    def body(k_ref, v_ref, cols_ref, grow_ref, u_ref, w_ref, t_ref):
        r, c, causal = _group_masks(rows)
        mats, rhs = [], []
        for h in range(hp):
            hs = slice(h * LANES, (h + 1) * LANES)
            k = k_ref[:, hs]
            gc = cols_ref[h, :, 0:1]
            beta = cols_ref[h, :, 1:2]
            kb = k * beta
            mats.append(jnp.where(r > c, _dot_nt(kb, k) * _decay(gc, grow_ref[h, 0], causal), 0.0))
            rhs.append((v_ref[:, hs] * beta, kb * jnp.exp(gc)))
        for h, tm in enumerate(_inv_unit_lower_many(mats, r, c)):
            tb = tm.astype(bf16)
            u_ref[h] = rhs[h][0] + jnp.dot(tb, rhs[h][0].astype(bf16), preferred_element_type=f32)
            w_ref[h] = (rhs[h][1] + jnp.dot(tb, rhs[h][1].astype(bf16), preferred_element_type=f32)).astype(bf16)
            t_ref[h] = tb

    return _pc(body, name="gdn_prep", grid=(bh_n // hp, nblk),
               in_specs=[qkv_spec(1), qkv_spec(2), ph(2), rv()],
               out_specs=[ph(LANES), ph(LANES), ph(rows)],
               out_shape=[jax.ShapeDtypeStruct((bh_n, s, LANES), f32), jax.ShapeDtypeStruct((bh_n, s, LANES), bf16),
                          jax.ShapeDtypeStruct((bh_n, s, rows), bf16)],
               sem=("parallel", "parallel"))(qkv, qkv, cols, grow)


def _gdn_scan(qkv, u, w, cols, grow, bl, s, n_heads):
    rows = _group_rows(s)
    nblk = s // rows
    bh_n = bl * n_heads
    d = n_heads * LANES
    hp = HEADS_PER_STEP if n_heads % HEADS_PER_STEP == 0 else 1
    qkv_spec, ph, rv = _gdn_specs(n_heads, nblk, rows, hp)

    def body(q_ref, k_ref, u_ref, w_ref, cols_ref, grow_ref, o_ref, vn_ref, ss_ref, s_scr):
        _acc_init(pl.program_id(1), s_scr)
        _, _, causal = _group_masks(rows)
        hh = range(hp)
        qs = [q_ref[:, h * LANES:(h + 1) * LANES] for h in hh]
        ks = [k_ref[:, h * LANES:(h + 1) * LANES] for h in hh]
        gcs = [cols_ref[h, :, 0:1] for h in hh]
        ps = [_dot_nt(qs[h], ks[h]) * _decay(gcs[h], grow_ref[h, 0], causal) for h in hh]
        sts = [s_scr[h] for h in hh]
        for h in hh:
            ss_ref[h] = sts[h].astype(bf16)
        vns = [u_ref[h] - _dot(w_ref[h], sts[h]) for h in hh]
        for h in hh:
            vn_ref[h] = vns[h].astype(bf16)
        o_state = [_dot(qs[h] * jnp.exp(gcs[h]), sts[h]) for h in hh]
        o_intra = [_dot(ps[h], vns[h]) for h in hh]
        for h in hh:
            o_ref[:, h * LANES:(h + 1) * LANES] = o_state[h] + o_intra[h]
        for h in hh:
            g_last = gcs[h][rows - 1:rows, :]
            s_scr[h] = jnp.exp(g_last) * sts[h] + _dot_tn(ks[h] * jnp.exp(g_last - gcs[h]), vns[h])

    return _pc(body, name="gdn_scan", grid=(bh_n // hp, nblk),
               in_specs=[qkv_spec(0), qkv_spec(1), ph(LANES), ph(LANES), ph(2), rv()],
               out_specs=[qkv_spec(0), ph(LANES), ph(LANES, block_rows=LANES)],
               out_shape=[jax.ShapeDtypeStruct((bl * s, d), f32), jax.ShapeDtypeStruct((bh_n, s, LANES), bf16),
                          jax.ShapeDtypeStruct((bh_n, nblk * LANES, LANES), bf16)],
               scratch=[pltpu.VMEM((hp, LANES, LANES), f32)],
               sem=("parallel", "arbitrary"))(qkv, qkv, u, w, cols, grow)


def _gdn_scan_bwd(do, qkv, w, vn, cols, grow, ss, bl, s, n_heads):
    rows = _group_rows(s)
    nblk = s // rows
    bh_n = bl * n_heads
    hp = HEADS_PER_STEP if n_heads % HEADS_PER_STEP == 0 else 1
    qkv_spec, ph, rv = _gdn_specs(n_heads, nblk, rows, hp, rev=True)

    def body(do_ref, q_ref, k_ref, w_ref, vn_ref, cols_ref, grow_ref, ss_ref,
             du_ref, dw_ref, dq_ref, dk_ref, dcol_ref, drow_ref, ds_scr):
        _acc_init(pl.program_id(1), ds_scr)
        _, _, causal = _group_masks(rows)
        last_row = lax.broadcasted_iota(jnp.int32, (rows, 1), 0) == rows - 1
        hh = range(hp)
        dos = [do_ref[:, h * LANES:(h + 1) * LANES] for h in hh]
        qs = [q_ref[:, h * LANES:(h + 1) * LANES] for h in hh]
        ks = [k_ref[:, h * LANES:(h + 1) * LANES] for h in hh]
        vns = [vn_ref[h] for h in hh]
        gcs = [cols_ref[h, :, 0:1] for h in hh]
        sts = [ss_ref[h] for h in hh]
        dss = [ds_scr[h] for h in hh]
        dmats = [_decay(gcs[h], grow_ref[h, 0], causal) for h in hh]
        gams = [jnp.exp(gc) for gc in gcs]
        qgs = [qs[h] * gams[h] for h in hh]
        g_lasts = [gc[rows - 1:rows, :] for gc in gcs]
        kd_scales = [jnp.exp(g_lasts[h] - gcs[h]) for h in hh]
        kdecs = [ks[h] * kd_scales[h] for h in hh]
        qks = [_dot_nt(qs[h], ks[h]) for h in hh]
        dpds = [_dot_nt(dos[h], vns[h]) * dmats[h] for h in hh]
        dvns = [_dot_tn(qks[h] * dmats[h], dos[h]) + _dot(kdecs[h], dss[h]) for h in hh]
        for h in hh:
            du_ref[h] = dvns[h]
        dkdecs = [_dot_nt(vns[h], dss[h]) for h in hh]
        for h in hh:
            dw_ref[h] = -_dot_nt(dvns[h], sts[h])
        dqgs = [_dot_nt(dos[h], sts[h]) for h in hh]
        dq_intra = [_dot(dpds[h], ks[h]) for h in hh]
        dk_intra = [_dot_tn(dpds[h], qs[h]) for h in hh]
        for h in hh:
            dq_ref[h] = dqgs[h] * gams[h] + dq_intra[h]
            dk_ref[h] = dk_intra[h] + dkdecs[h] * kd_scales[h]
            ep = dpds[h] * qks[h]
            drow_ref[h, 0] = -jnp.sum(ep, axis=0, keepdims=True)
            kd_rows = jnp.sum(dkdecs[h] * kdecs[h], axis=-1, keepdims=True)
            extra = jnp.sum(kd_rows) + jnp.exp(g_lasts[h]) * jnp.sum(sts[h] * dss[h])
            dcol_ref[h] = (jnp.sum(dqgs[h] * qgs[h], axis=-1, keepdims=True) + jnp.sum(ep, axis=-1, keepdims=True)
                           - kd_rows + jnp.where(last_row, extra, 0.0))
        ds_new = [jnp.exp(g_lasts[h]) * dss[h] + _dot_tn(qgs[h], dos[h]) - _dot_tn(w_ref[h], dvns[h]) for h in hh]
        for h in hh:
            ds_scr[h] = ds_new[h]

    return _pc(body, name="gdn_scan_bwd", grid=(bh_n // hp, nblk),
               in_specs=[qkv_spec(0), qkv_spec(0), qkv_spec(1), ph(LANES), ph(LANES), ph(2), rv(),
                         ph(LANES, block_rows=LANES)],
               out_specs=[ph(LANES), ph(LANES), ph(LANES), ph(LANES), ph(1), rv()],
               out_shape=[jax.ShapeDtypeStruct((bh_n, s, LANES), f32)] * 4
               + [jax.ShapeDtypeStruct((bh_n, s, 1), f32), jax.ShapeDtypeStruct((bh_n, nblk, 1, rows), f32)],
               scratch=[pltpu.VMEM((hp, LANES, LANES), f32)],
               sem=("parallel", "arbitrary"))(do, qkv, qkv, w, vn, cols, grow, ss)


def _gdn_prep_bwd(qkv, cols, grow, tmat, du, dw, dk_scan, dcol_scan, drow_scan, bl, s, n_heads):
    rows = _group_rows(s)
    nblk = s // rows
    bh_n = bl * n_heads
    hp = PREP_HEADS_PER_STEP if n_heads % PREP_HEADS_PER_STEP == 0 else 1
    qkv_spec, ph, rv = _gdn_specs(n_heads, nblk, rows, hp)

    def body(k_ref, v_ref, cols_ref, grow_ref, t_ref, du_ref, dw_ref, dks_ref, dcs_ref, drs_ref,
             dk_ref, dv_ref, dcols_ref, drow_ref):
        r, c, causal = _group_masks(rows)
        hh = range(hp)
        ks = [k_ref[:, h * LANES:(h + 1) * LANES] for h in hh]
        vs = [v_ref[:, h * LANES:(h + 1) * LANES] for h in hh]
        gcs = [cols_ref[h, :, 0:1] for h in hh]
        betas = [cols_ref[h, :, 1:2] for h in hh]
        tms = [t_ref[h] for h in hh]
        dus = [du_ref[h] for h in hh]
        dws = [dw_ref[h] for h in hh]
        gams = [jnp.exp(gc) for gc in gcs]
        kbs = [k * b for k, b in zip(ks, betas)]
        kbgs = [kb * g for kb, g in zip(kbs, gams)]
        dts = [_dot_nt(dus[h], vs[h] * betas[h]) + _dot_nt(dws[h], kbgs[h]) for h in hh]
        dvbs = [dus[h] + _dot_tn(tms[h], dus[h]) for h in hh]
        dkbgs = [dws[h] + _dot_tn(tms[h], dws[h]) for h in hh]
        kks = [_dot_nt(kbs[h], ks[h]) for h in hh]
        inner = [dts[h] + _dot_nt(dts[h], tms[h]) for h in hh]
        dads = [jnp.where(r > c, -(inner[h] + _dot_tn(tms[h], inner[h])), 0.0) * _decay(gcs[h], grow_ref[h, 0], causal)
                for h in hh]
        dkbs = [dkbgs[h] * gams[h] + _dot(dads[h], ks[h]) for h in hh]
        dk2 = [_dot_tn(dads[h], kbs[h]) for h in hh]
        for h in hh:
            dk_ref[h] = dks_ref[h] + dk2[h] + dkbs[h] * betas[h]
            dv_ref[h] = dvbs[h] * betas[h]
            ea = dads[h] * kks[h]
            dcols_ref[h, :, 0:1] = (dcs_ref[h] + jnp.sum(dkbgs[h] * kbgs[h], axis=-1, keepdims=True)
                                    + jnp.sum(ea, axis=-1, keepdims=True))
            dcols_ref[h, :, 1:2] = (jnp.sum(dvbs[h] * vs[h], axis=-1, keepdims=True)
                                    + jnp.sum(dkbs[h] * ks[h], axis=-1, keepdims=True))
            drow_ref[h, 0] = drs_ref[h, 0] - jnp.sum(ea, axis=0, keepdims=True)

    return _pc(body, name="gdn_prep_bwd", grid=(bh_n // hp, nblk),
               in_specs=[qkv_spec(1), qkv_spec(2), ph(2), rv(), ph(rows), ph(LANES), ph(LANES), ph(LANES), ph(1), rv()],
               out_specs=[ph(LANES), ph(LANES), ph(2), rv()],
               out_shape=[jax.ShapeDtypeStruct((bh_n, s, LANES), f32), jax.ShapeDtypeStruct((bh_n, s, LANES), f32),
                          jax.ShapeDtypeStruct((bh_n, s, 2), f32), jax.ShapeDtypeStruct((bh_n, nblk, 1, rows), f32)],
               sem=("parallel", "parallel"))(qkv, qkv, cols, grow, tmat, du, dw, dk_scan, dcol_scan, drow_scan)


def _row(v):
    return v.reshape(1, -1).astype(f32)


def _pad_lanes(v):
    v = v.reshape(1, -1).astype(f32)
    return jnp.pad(v, ((0, 0), (0, LANES - v.shape[1])))


def _local_step(x, tgt, p, need, emit):
    bl, s, d = x.shape
    t = bl * s
    n_heads = p["gdn_a_log"].shape[-1]
    assert d == n_heads * LANES and s % CHUNK == 0
    x2 = x.reshape(t, d)
    tgt2 = tgt.reshape(t, d)
    gr = {}

    n0, ub, gl = _pw1_glu(x2, _row(p["norm_mix_g"][0]), p["cv_w_pw1"], _row(p["cv_b_pw1"]))
    cv = need("conv", n0)
    dc = _dwconv_fwd(gl.reshape(bl, s, d), cv["cv_w_dw"], _row(p["cv_b_dw"]), "dwconv_fwd").reshape(t, d)
    sb, h1 = _ln_silu_mm_res(dc, _row(p["cv_ln_g"]), _row(p["cv_ln_b"]), cv["cv_w_pw2"], _row(p["cv_b_pw2"]), x2)
    m0 = need("mlp0", h1)
    n1, f0, r0, h2 = _mlp_fwd(h1, _row(p["norm_ffn_g"][0]), m0["w1"], m0["w2"], "mlp_fwd0")

    gd = need("gdn", h2)
    w_in = gd["w_in"]
    w_ab = jnp.pad(w_in[:, 4 * d:], ((0, 0), (0, LANES - 2 * n_heads)))
    a_log_pad = _pad_lanes(p["gdn_a_log"])
    dt_pad = _pad_lanes(p["gdn_dt_bias"])
    n2, qkv_pre, z, ab, gbeta = _gdn_in(h2, _row(p["norm_mix_g"][1]), w_in, w_ab, a_log_pad, dt_pad, n_heads)
    zero_bias = jnp.zeros((1, 3 * d), f32)
    qkv = _dwconv_fwd(qkv_pre.reshape(bl, s, 3 * d), cv["gdn_conv_w"], zero_bias, "sconv_fwd", qk_heads=n_heads).reshape(t, 3 * d)
    bh_n, rows = bl * n_heads, _group_rows(s)
    gates = gbeta.reshape(2, n_heads, bl, s).transpose(0, 2, 1, 3).reshape(2, bh_n * (s // rows), rows)
    gc_lanes = _lane_cumsum(gates[0], jnp.zeros_like(gates[0]), False, "gdn_gate_cumsum")
    grow = gc_lanes.reshape(bh_n, s // rows, 1, rows)
    cols = jnp.stack([gc_lanes.reshape(bh_n, s), gates[1].reshape(bh_n, s)], axis=-1)
    u, w, tmat = _gdn_prep(qkv, cols, grow, bl, s, n_heads)
    o, vn, ss = _gdn_scan(qkv, u, w, cols, grow, bl, s, n_heads)
    onb, h3 = _gated_norm_mm_res(o, z, _row(p["gdn_norm_g"]), gd["w_out"], h2, n_heads)
    m1 = need("mlp1", h3)
    n3, f1, r1, loss_acc, dh4, dgf = _mlp_fwd(h3, _row(p["norm_ffn_g"][1]), m1["w1"], m1["w2"], "mlp_fwd1_loss",
                                              loss=(_row(p["final_norm_g"]), tgt2))
    loss = loss_acc[0, 0]
    gr["loss"] = loss_acc[0, 0:1]
    gr["final_norm_g"] = dgf[0]

    df1, dh3, dg_ffn1, _ = _mlp_bwd(dh4, h3, _row(p["norm_ffn_g"][1]), f1, m1["w1"], m1["w2"], "mlp_bwd1")
    dw2_1 = _mm_tn(r1, dh4, "dw_mlp2_1", bf16)
    dw1_1 = _mm_tn_blocked(n3, df1, "dw_mlp1_1", bf16)

    dw_out = _mm_tn(onb, dh3, "dw_gdn_out", bf16)
    tie = emit("late", {"mlp_w2_1": dw2_1, "mlp_w1_1": dw1_1, "gdn_w_out": dw_out})
    do, dz, dng = _mm_nt([(dh3, gd["w_out"])], "dx_gdn_out",
                         _gated_norm_bwd_tail(o, z, _row(p["gdn_norm_g"]) + tie, n_heads))
    gr["gdn_norm_g"] = dng[0]
    du, dw_, dq, dk_scan, dcol_scan, drow_scan = _gdn_scan_bwd(do, qkv, w, vn, cols, grow, ss, bl, s, n_heads)
    dk, dv, dcols, drow = _gdn_prep_bwd(qkv, cols, grow, tmat, du, dw_, dk_scan, dcol_scan, drow_scan, bl, s, n_heads)
    dg_lanes = _lane_cumsum(dcols[..., 0].reshape(gc_lanes.shape), drow.reshape(gc_lanes.shape), True, "gdn_gate_cumsum_bwd")
    dqkv_pre, dconv_w, _ = _dwconv_bwd(qkv_pre.reshape(bl, s, 3 * d), [dq, dk, dv], cv["gdn_conv_w"], "sconv_bwd", qk_heads=n_heads)
    gr["gdn_conv_w"] = dconv_w[:cv["gdn_conv_w"].shape[0]]
    dgb_t = jnp.stack([dg_lanes.reshape(bl, n_heads, s), dcols[..., 1].reshape(bl, n_heads, s)])
    dgb_t = dgb_t.transpose(0, 2, 1, 3).reshape(2 * n_heads, t)
    dab, dal, ddt = _gates_bwd(dgb_t, ab, a_log_pad, dt_pad, n_heads)
    gr["gdn_a_log"] = dal[0, :n_heads]
    gr["gdn_dt_bias"] = ddt[0, :n_heads]
    dqkv2 = dqkv_pre.reshape(t, 3 * d)
    dw_in = jnp.concatenate(
        [_mm_tn(n2, dqkv2, "dw_gdn_in_qkv"), _mm_tn(n2, dz, "dw_gdn_in_z"), _mm_tn(n2, dab, "dw_gdn_in_ab")[:, :2 * n_heads]], axis=1)
    tie = emit("gdn_in", {"gdn_w_in": dw_in})
    dh2, dg_mix1 = _mm_nt([(dqkv2, (w_in, 0)), (dz, (w_in, 3)), (dab, w_ab)], "dx_gdn_in",
                          _rms_bwd_tail(h2, _row(p["norm_mix_g"][1]) + tie, dh3))

    df0, dh1, dg_ffn0, cs_h1 = _mlp_bwd(dh2, h1, _row(p["norm_ffn_g"][0]), f0, m0["w1"], m0["w2"], "mlp_bwd0")
    dw2_0 = _mm_tn(r0, dh2, "dw_mlp2_0", bf16)
    dw1_0 = _mm_tn_blocked(n1, df0, "dw_mlp1_0", bf16)
    dw_pw2 = _mm_tn(sb, dh1, "dw_pw2", bf16)
    tie = emit("mlp0", {"mlp_w2_0": dw2_0, "mlp_w1_0": dw1_0, "cv_w_pw2": dw_pw2})
    gr["norm_ffn_g"] = jnp.stack([dg_ffn0[0], dg_ffn1[0]])

    gr["cv_b_pw2"] = cs_h1[0]
    ddc, dlng, dlnb, cs_dc = _mm_nt([(dh1, cv["cv_w_pw2"])], "dx_pw2",
                                    _ln_silu_bwd_tail(dc, _row(p["cv_ln_g"]) + tie, _row(p["cv_ln_b"])))
    gr["cv_ln_g"] = dlng[0]
    gr["cv_ln_b"] = dlnb[0]
    gr["cv_b_dw"] = cs_dc[0]
    dgl, dw_dw, _ = _dwconv_bwd(gl.reshape(bl, s, d), [ddc.reshape(bl, s, d)], cv["cv_w_dw"], "dwconv_bwd")
    gr["cv_w_dw"] = dw_dw[:cv["cv_w_dw"].shape[0]]
    dub, cs_u = _glu_bwd(dgl.reshape(t, d), ub)
    gr["cv_b_pw1"] = cs_u[0]
    dw_pw1 = _mm_tn_blocked(n0, dub, "dw_pw1", bf16)
    tie = emit("last", {"cv_w_pw1": dw_pw1, "small": gr})
    dx, dg_mix0 = _mm_nt([(dub, ("blocks", p["cv_w_pw1"]))], "dx_pw1",
                         _rms_bwd_tail(x2, _row(p["norm_mix_g"][0]) + tie, dh1))
    return loss, dx.reshape(bl, s, d), {"norm_mix_g": jnp.stack([dg_mix0[0], dg_mix1[0]])}


ANY = pl.BlockSpec(memory_space=pl.ANY)
MESH = pl.DeviceIdType.MESH


def _flip(v, bit):
    return 1 - v if bit else v


def _all_gather_many(shards):
    na = len(shards)

    def body(*refs):
        x_refs, o_refs = refs[:na], refs[na:2 * na]
        send_sems, recv_sems, local_sems = refs[2 * na:]
        x, y, c = lax.axis_index("x"), lax.axis_index("y"), lax.axis_index("c")
        me, sibling = (x, y, c), (x, y, 1 - c)
        chips = [(1 - x, y), (x, 1 - y), (1 - x, 1 - y)]

        def copy(a, k, block, to, src=None):
            px, py, pc = block
            dst = o_refs[a].at[4 * px + 2 * py + pc]
            return pltpu.make_async_remote_copy(
                src_ref=dst if src is None else src, dst_ref=dst,
                send_sem=send_sems.at[a, k], recv_sem=recv_sems.at[a, k], device_id=to, device_id_type=MESH)

        mine = [pltpu.make_async_copy(x_refs[a], o_refs[a].at[4 * x + 2 * y + c], local_sems.at[a]) for a in range(na)]
        first = []
        for a in range(na):
            first.append(copy(a, 0, me, sibling, src=x_refs[a]))
            first += [copy(a, 1 + j, me, (*chip, c), src=x_refs[a]) for j, chip in enumerate(chips)]
        for cp in mine + first:
            cp.start()
        passed = []
        for j, chip in enumerate(chips):
            for a in range(na):
                copy(a, 1 + j, (*chip, c), me).wait_recv()
                fwd = copy(a, 4 + j, (*chip, c), sibling)
                fwd.start()
                passed.append(fwd)
        for a in range(na):
            copy(a, 0, sibling, me).wait_recv()
        for j, chip in enumerate(chips):
            for a in range(na):
                copy(a, 4 + j, (*chip, 1 - c), me).wait_recv()
        for cp in first + passed:
            cp.wait_send()
        for cp in mine:
            cp.wait()

    return pl.pallas_call(
        body, name="weights_all_gather",
        out_shape=[jax.ShapeDtypeStruct((N_DEV,) + a.shape, a.dtype) for a in shards],
        in_specs=[ANY] * na, out_specs=[ANY] * na,
        scratch_shapes=[pltpu.SemaphoreType.DMA((na, 7)), pltpu.SemaphoreType.DMA((na, 7)), pltpu.SemaphoreType.DMA((na,))],
        compiler_params=pltpu.CompilerParams(has_side_effects=True),
    )(*shards)


HBM = pl.BlockSpec(memory_space=pltpu.HBM)
SEM = pl.BlockSpec(memory_space=pltpu.SEMAPHORE)
EFFECT = pltpu.SideEffectType.DATAFLOW_SIDE_EFFECTING
N_PEERS = N_DEV - 1


def _exchange_copies(src_refs, land_refs, send_sems, recv_sems, scatter):
    x, y, c = lax.axis_index("x"), lax.axis_index("y"), lax.axis_index("c")
    me = 4 * x + 2 * y + c
    copies = []
    for a, (src, land) in enumerate(zip(src_refs, land_refs)):
        for k in range(1, N_DEV):
            px, py, pc = _flip(x, k & 4), _flip(y, k & 2), _flip(c, k & 1)
            i = a * N_PEERS + k - 1
            copies.append(pltpu.make_async_remote_copy(
                src_ref=src.at[4 * px + 2 * py + pc] if scatter[a] else src, dst_ref=land.at[me],
                send_sem=send_sems.at[i], recv_sem=recv_sems.at[i], device_id=(px, py, pc), device_id_type=MESH))
    return copies


def _exchange_start(srcs, scatter, name):
    na = len(srcs)
    lands = [lax.empty(s.shape if sc else (N_DEV,) + s.shape, s.dtype) for s, sc in zip(srcs, scatter)]

    def body(*refs):
        copies = _exchange_copies(refs[:na], refs[na:2 * na], refs[2 * na], refs[2 * na + 1], scatter)
        for cp in copies:
            cp.start()
        token = refs[-1]
        token[...] = jnp.zeros_like(token)

    outs = pl.pallas_call(
        body, name=name,
        out_shape=(pltpu.SemaphoreType.DMA((na * N_PEERS,)), pltpu.SemaphoreType.DMA((na * N_PEERS,)))
        + tuple(pltpu.HBM(a.shape, a.dtype) for a in srcs + lands) + (jax.ShapeDtypeStruct((SUBLANES, LANES), f32),),
        in_specs=[HBM] * (2 * na),
        out_specs=(SEM, SEM) + (HBM,) * (2 * na) + (pl.BlockSpec(memory_space=pltpu.VMEM),),
        input_output_aliases={i: 2 + i for i in range(2 * na)},
        compiler_params=pltpu.CompilerParams(has_side_effects=EFFECT),
    )(*[pltpu.with_memory_space_constraint(a, pltpu.HBM) for a in srcs + lands])
    return outs[0], outs[1], list(outs[2:2 + na]), list(outs[2 + na:2 + 2 * na]), outs[-1]


def _exchange_wait(started, after, scatter, name):
    send_sems, recv_sems, srcs, lands, _ = started
    na = len(srcs)

    def body(*refs):
        for cp in _exchange_copies(refs[:na], refs[na:2 * na], refs[2 * na], refs[2 * na + 1], scatter):
            cp.wait_send()
            cp.wait_recv()

    outs = pl.pallas_call(
        body, name=name,
        out_shape=tuple(pltpu.HBM(a.shape, a.dtype) for a in srcs + lands),
        in_specs=[HBM] * (2 * na) + [SEM, SEM, ANY], out_specs=(HBM,) * (2 * na),
        input_output_aliases={i: i for i in range(2 * na)},
        compiler_params=pltpu.CompilerParams(has_side_effects=EFFECT),
    )(*srcs, *lands, send_sems, recv_sems, after)
    return list(outs[:na]), list(outs[na:])


def _own_block(land, block, me):
    return lax.dynamic_update_index_in_dim(land, block, me, 0)


def _sum8_adamw(r2, w, m, v, layer, name):
    _, rr, cc = r2.shape
    tr = _tile(rr, 256)
    bc1 = 1.0 - ADAM_B1 ** ADAM_STEP
    bc2 = 1.0 - ADAM_B2 ** ADAM_STEP

    def body(r_ref, w_ref, m_ref, v_ref, g_ref, d_ref, nm_ref, nv_ref):
        gv = r_ref[0].astype(f32)
        for q in range(1, N_DEV):
            gv = gv + r_ref[q].astype(f32)
        g_ref[...] = gv
        nm = ADAM_B1 * m_ref[...] + (1.0 - ADAM_B1) * gv
        nv = ADAM_B2 * v_ref[...] + (1.0 - ADAM_B2) * (gv * gv)
        nm_ref[...] = nm
        nv_ref[...] = nv
        d_ref[...] = -ADAM_LR * ((nm / bc1) / (jnp.sqrt(nv / bc2) + ADAM_EPS) + ADAM_WD * w_ref[...])

    lspec = pl.BlockSpec((None, tr, cc), lambda i: (layer, i, 0))
    return _pc(body, name=name, grid=(rr // tr,),
               in_specs=[pl.BlockSpec((N_DEV, tr, cc), lambda i: (0, i, 0)), lspec, lspec, lspec],
               out_specs=[_rows(tr, cc)] * 4, out_shape=[jax.ShapeDtypeStruct((rr, cc), f32)] * 4,
               sem=("parallel",))(r2, w, m, v)


def _sum_devices(recv, name):
    _, rr, _ = recv.shape
    tr = _tile(rr, 512)

    def body(r_ref, o_ref):
        acc = r_ref[0]
        for i in range(1, N_DEV):
            acc = acc + r_ref[i]
        o_ref[...] = acc

    return _pc(body, name=name, grid=(rr // tr,),
               in_specs=[pl.BlockSpec((N_DEV, tr, LANES), lambda i: (0, i, 0))],
               out_specs=_rows(tr, LANES), out_shape=jax.ShapeDtypeStruct((rr, LANES), f32), sem=("parallel",))(recv)


def _adamw(w, g, m, v):
    rr = w.shape[0]
    tr = _tile(rr, 512)
    bc1 = 1.0 - ADAM_B1 ** ADAM_STEP
    bc2 = 1.0 - ADAM_B2 ** ADAM_STEP

    def body(w_ref, g_ref, m_ref, v_ref, d_ref, nm_ref, nv_ref):
        gv = g_ref[...]
        nm = ADAM_B1 * m_ref[...] + (1.0 - ADAM_B1) * gv
        nv = ADAM_B2 * v_ref[...] + (1.0 - ADAM_B2) * (gv * gv)
        nm_ref[...] = nm
        nv_ref[...] = nv
        d_ref[...] = -ADAM_LR * ((nm / bc1) / (jnp.sqrt(nv / bc2) + ADAM_EPS) + ADAM_WD * w_ref[...])

    spec = _rows(tr, LANES)
    return _pc(body, name="adamw", grid=(rr // tr,), in_specs=[spec] * 4, out_specs=[spec] * 3,
               out_shape=[jax.ShapeDtypeStruct((rr, LANES), f32)] * 3, sem=("parallel",))(w, g, m, v)


PACK_ROWS = 512
PART_ROWS = SUBLANES


def _pack(arrs):
    parts, sizes = [], []
    for a in arrs:
        flat = a.reshape(-1)
        n = flat.shape[0]
        rows = -(-n // (LANES * PART_ROWS)) * PART_ROWS
        if rows * LANES != n:
            flat = jnp.pad(flat, (0, rows * LANES - n))
        parts.append(flat.reshape(rows, LANES))
        sizes.append((rows, n))
    total = sum(r for r, _ in sizes)
    padded = -(-total // PACK_ROWS) * PACK_ROWS
    if padded > total:
        parts.append(jnp.zeros((padded - total, LANES), parts[0].dtype))
    return jnp.concatenate(parts, axis=0), sizes


def _unpack(packed, sizes, shapes):
    out, off = [], 0
    for (rows, n), shp in zip(sizes, shapes):
        piece = lax.slice_in_dim(packed, off, off + rows, axis=0)
        if rows * LANES != n:
            piece = lax.slice_in_dim(piece.reshape(-1), 0, n, axis=0)
        out.append(piece.reshape(tuple(shp)))
        off += rows
    return out


def _cols_to_blocks(a):
    n = a.shape[-1] // N_DEV
    a = a.reshape(a.shape[:-1] + (N_DEV, n))
    return jnp.moveaxis(a, -2, 0)


def _blocks_to_cols(a):
    a = jnp.moveaxis(a, 0, -2)
    return a.reshape(a.shape[:-2] + (a.shape[-2] * a.shape[-1],))


def _rows_to_blocks(a):
    k = a.shape[-2] // N_DEV
    a = a.reshape(a.shape[:-2] + (N_DEV, k, a.shape[-1]))
    return jnp.moveaxis(a, -3, 0)


def _blocks_to_rows(a):
    a = jnp.moveaxis(a, 0, -3)
    return a.reshape(a.shape[:-3] + (a.shape[-3] * a.shape[-2], a.shape[-1]))


COL_SHARDED = ("cv_w_pw1", "gdn_w_in", "mlp_w1")
ROW_SHARDED = ("cv_w_pw2", "gdn_w_out", "mlp_w2")
CONV_SHARDED = ("cv_w_dw", "gdn_conv_w")
REPLICATED = ("norm_mix_g", "norm_ffn_g", "final_norm_g", "cv_b_pw1", "cv_b_dw", "cv_ln_g", "cv_ln_b", "cv_b_pw2",
              "gdn_a_log", "gdn_dt_bias", "gdn_norm_g")
WEIGHTS = ("norm_mix_g", "norm_ffn_g", "final_norm_g", "cv_w_pw1", "cv_b_pw1", "cv_w_dw", "cv_b_dw", "cv_ln_g",
           "cv_ln_b", "cv_w_pw2", "cv_b_pw2", "gdn_w_in", "gdn_conv_w", "gdn_a_log", "gdn_dt_bias", "gdn_norm_g",
           "gdn_w_out", "mlp_w1", "mlp_w2")
MATMUL_SHARDED = COL_SHARDED + ROW_SHARDED


def _squeeze_layer(name, a):
    if name in ("norm_mix_g", "norm_ffn_g", "final_norm_g", "mlp_w1", "mlp_w2"):
        return a
    return a[0]


def _gather_weights(shards):
    me = 4 * lax.axis_index("x") + 2 * lax.axis_index("y") + lax.axis_index("c")
    pw1 = _all_gather_many([shards["cv_w_pw1"].astype(bf16)])[0]
    now = {"cv_w_pw1": pw1}

    def cast(a, tie):
        return (a + tie).astype(bf16)

    token, _ = lax.optimization_barrier((jnp.zeros((), f32), pw1))
    later, started = {}, {}
    for group in ("conv", "mlp0", "gdn", "mlp1"):
        if group == "conv":
            srcs = [cast(shards["cv_w_pw2"], token)] + [shards[n] + token for n in CONV_SHARDED]
        elif group == "gdn":
            srcs = [cast(shards["gdn_w_in"], token).reshape(-1, LANES), cast(shards["gdn_w_out"], token)]
        else:
            layer = int(group[-1])
            srcs = [cast(shards["mlp_w1"][layer], token), cast(shards["mlp_w2"][layer], token)]
        later[group] = srcs
        started[group] = _exchange_start(srcs, [False] * len(srcs), f"weights_{group}_start")
        token = started[group][4][0, 0]

    def need(group, after):
        srcs, lands = _exchange_wait(started[group], after, [False] * len(later[group]), f"weights_{group}_wait")
        lands = [_own_block(ld, own, me) for ld, own in zip(lands, srcs)]
        if group == "conv":
            out = {"cv_w_pw2": _blocks_to_rows(lands[0])}
            out.update({n: _blocks_to_cols(ld) for n, ld in zip(CONV_SHARDED, lands[1:])})
            return out
        if group == "gdn":
            w_in = _blocks_to_cols(lands[0].reshape((N_DEV,) + shards["gdn_w_in"].shape))
            return {"w_in": w_in, "w_out": _blocks_to_rows(lands[1])}
        return {"w1": lands[0], "w2": lands[1]}

    return now, need, token


def kernel(x, norm_mix_g, norm_ffn_g, final_norm_g, cv_w_pw1, cv_b_pw1, cv_w_dw, cv_b_dw, cv_ln_g, cv_ln_b, cv_w_pw2, cv_b_pw2, gdn_w_in, gdn_conv_w, gdn_a_log, gdn_dt_bias, gdn_norm_g, gdn_w_out, mlp_w1, mlp_w2, loss_target, m_norm_mix_g, m_norm_ffn_g, m_final_norm_g, m_cv_w_pw1, m_cv_b_pw1, m_cv_w_dw, m_cv_b_dw, m_cv_ln_g, m_cv_ln_b, m_cv_w_pw2, m_cv_b_pw2, m_gdn_w_in, m_gdn_conv_w, m_gdn_a_log, m_gdn_dt_bias, m_gdn_norm_g, m_gdn_w_out, m_mlp_w1, m_mlp_w2, v_norm_mix_g, v_norm_ffn_g, v_final_norm_g, v_cv_w_pw1, v_cv_b_pw1, v_cv_w_dw, v_cv_b_dw, v_cv_ln_g, v_cv_ln_b, v_cv_w_pw2, v_cv_b_pw2, v_gdn_w_in, v_gdn_conv_w, v_gdn_a_log, v_gdn_dt_bias, v_gdn_norm_g, v_gdn_w_out, v_mlp_w1, v_mlp_w2):
    w_in = dict(zip(WEIGHTS, (norm_mix_g, norm_ffn_g, final_norm_g, cv_w_pw1, cv_b_pw1, cv_w_dw, cv_b_dw, cv_ln_g, cv_ln_b, cv_w_pw2, cv_b_pw2, gdn_w_in, gdn_conv_w, gdn_a_log, gdn_dt_bias, gdn_norm_g, gdn_w_out, mlp_w1, mlp_w2)))
    m_in = dict(zip(WEIGHTS, (m_norm_mix_g, m_norm_ffn_g, m_final_norm_g, m_cv_w_pw1, m_cv_b_pw1, m_cv_w_dw, m_cv_b_dw, m_cv_ln_g, m_cv_ln_b, m_cv_w_pw2, m_cv_b_pw2, m_gdn_w_in, m_gdn_conv_w, m_gdn_a_log, m_gdn_dt_bias, m_gdn_norm_g, m_gdn_w_out, m_mlp_w1, m_mlp_w2)))
    v_in = dict(zip(WEIGHTS, (v_norm_mix_g, v_norm_ffn_g, v_final_norm_g, v_cv_w_pw1, v_cv_b_pw1, v_cv_w_dw, v_cv_b_dw, v_cv_ln_g, v_cv_ln_b, v_cv_w_pw2, v_cv_b_pw2, v_gdn_w_in, v_gdn_conv_w, v_gdn_a_log, v_gdn_dt_bias, v_gdn_norm_g, v_gdn_w_out, v_mlp_w1, v_mlp_w2)))
    me = 4 * lax.axis_index("x") + 2 * lax.axis_index("y") + lax.axis_index("c")

    shards = {n: _squeeze_layer(n, w_in[n]) for n in WEIGHTS}
    first, need, token = _gather_weights(shards)
    params = {n: shards[n] for n in REPLICATED}
    params.update(first)
    params["norm_mix_g"] = params["norm_mix_g"] + token

    def row_blocks(a):
        return a.reshape(N_DEV, a.shape[0] // N_DEV, a.shape[1])

    def flat_blocks(a):
        k, n8 = a.shape
        return _cols_to_blocks(a).reshape(N_DEV, k * (n8 // N_DEV) // LANES, LANES)

    def as_blocks(n, a):
        if n == "gdn_w_in":
            return flat_blocks(a).astype(bf16)
        return a if a.ndim == 3 else row_blocks(a)

    sent = []

    small = REPLICATED + CONV_SHARDED
    small_early = tuple(n for n in small if n != "norm_mix_g") + ("loss",)
    small_info = {}

    def emit(group, grads_out):
        names, blocks, scatter = list(grads_out), [], []
        for n in names:
            if n == "small":
                packed, small_info["sizes"] = _pack([grads_out[n][k] for k in small_early])
                small_info["shapes"] = [grads_out[n][k].shape for k in small_early]
                blocks.append(packed)
                scatter.append(False)
            else:
                blocks.append(as_blocks(n, grads_out[n]))
                scatter.append(True)
        sent.append((names, _exchange_start(blocks, scatter, f"grads_{group}_start"), scatter))
        return sent[-1][1][4][0, 0]

    _, grad_x, gr = _local_step(x, loss_target, params, need, emit)
    nm_shape = gr["norm_mix_g"].shape
    sent.append((["norm_mix_g"], _exchange_start([gr["norm_mix_g"].reshape(-1, LANES)], [False], "grads_norm_mix_start"), [False]))

    recv = {}

    def finish(entry, after):
        names, st, scatter = entry
        srcs, lands = _exchange_wait(st, after, scatter, f"grads_{names[0]}_wait")
        for n, ld, blk, sc in zip(names, lands, srcs, scatter):
            own = lax.dynamic_index_in_dim(blk, me, 0, keepdims=False) if sc else blk
            recv[n] = _own_block(ld, own, me)
        return lands[0]

    def as3d(n, a):
        if n == "gdn_w_in":
            return a.reshape(a.shape[0], -1, LANES)
        return a

    big = [("cv_w_pw2", 0, "cv_w_pw2"), ("gdn_w_in", 0, "gdn_w_in"), ("gdn_w_out", 0, "gdn_w_out"),
           ("mlp_w1", 0, "mlp_w1_0"), ("mlp_w1", 1, "mlp_w1_1"), ("mlp_w2", 0, "mlp_w2_0"), ("mlp_w2", 1, "mlp_w2_1"),
           ("cv_w_pw1", 0, "cv_w_pw1")]
    res = {n: {} for n in MATMUL_SHARDED}
    after = grad_x
    for entry in sent[:-2]:
        after = finish(entry, after)
    for n, layer, key in big[:-1]:
        res[n][layer] = _sum8_adamw(recv[key], as3d(n, w_in[n]), as3d(n, m_in[n]), as3d(n, v_in[n]), layer, f"adamw_{key}")
        after = res[n][layer][0]
    after = finish(sent[-2], after)
    finish(sent[-1], after)
    n, layer, key = big[-1]
    res[n][layer] = _sum8_adamw(recv[key], as3d(n, w_in[n]), as3d(n, m_in[n]), as3d(n, v_in[n]), layer, f"adamw_{key}")

    grads = dict(zip(small_early, _unpack(_sum_devices(recv["small"], "grads_small_sum"), small_info["sizes"], small_info["shapes"])))
    grads["norm_mix_g"] = _sum_devices(recv["norm_mix_g"], "grads_norm_mix_sum").reshape(nm_shape)
    loss = grads["loss"].reshape(())
    for n in CONV_SHARDED:
        cn = shards[n].shape[-1]
        grads[n] = lax.dynamic_slice_in_dim(grads[n], me * cn, cn, axis=1)

    out_groups = {n: [] for n in WEIGHTS}
    for n in MATMUL_SHARDED:
        layers = sorted(res[n])
        for k in range(4):
            pieces = [res[n][layer][k] for layer in layers]
            out_groups[n].append(jnp.stack(pieces).reshape(w_in[n].shape))

    sm_w = [shards[n] for n in small]
    sm_g = [grads[n].reshape(shards[n].shape) for n in small]
    sm_m = [_squeeze_layer(n, m_in[n]) for n in small]
    sm_v = [_squeeze_layer(n, v_in[n]) for n in small]
    wp, psz = _pack(sm_w)
    gp, _ = _pack(sm_g)
    mp, _ = _pack(sm_m)
    vp, _ = _pack(sm_v)
    dp, nmp, nvp = _adamw(wp, gp, mp, vp)
    shp = [a.shape for a in sm_w]
    for n, g, dl, nm, nv in zip(small, sm_g, _unpack(dp, psz, shp), _unpack(nmp, psz, shp), _unpack(nvp, psz, shp)):
        out_groups[n] = [a.reshape(w_in[n].shape) for a in (g, dl, nm, nv)]

    outs = [loss, grad_x]
    for k in range(4):
        outs += [out_groups[n][k] for n in WEIGHTS]
    return tuple(outs)
```

```python
import jax
import jax.numpy as jnp
from jax import lax
from jax.experimental import pallas as pl
from jax.experimental.pallas import tpu as pltpu

f32, bf16 = jnp.float32, jnp.bfloat16

NORM_EPS = 1e-6
L2_EPS = 1e-6
CHUNK = 64
LANES = 128
SUBLANES = 8
N_DEV = 8
VMEM_LIMIT = 56 * 1024 * 1024
ROW_TILE = 512
CONV_PAD = 32
HEADS_PER_STEP = 8
PREP_HEADS_PER_STEP = 8
NEG = -1e30

ADAM_LR, ADAM_B1, ADAM_B2, ADAM_EPS, ADAM_WD, ADAM_STEP = 0.001, 0.9, 0.999, 1e-08, 0.01, 10

NT = (((1,), (1,)), ((), ()))
TN = (((0,), (0,)), ((), ()))
HI = lax.Precision.HIGHEST


def _pc(body, *, name, grid, in_specs, out_specs, out_shape, scratch=(), sem=None):
    return pl.pallas_call(
        body, name=name, grid=grid, in_specs=in_specs, out_specs=out_specs, out_shape=out_shape,
        scratch_shapes=list(scratch),
        compiler_params=pltpu.CompilerParams(dimension_semantics=sem, vmem_limit_bytes=VMEM_LIMIT))


def _rows(tm, n):
    return pl.BlockSpec((tm, n), lambda i: (i, 0))


def _const(shape):
    return pl.BlockSpec(shape, lambda *_: (0,) * len(shape))


def _resident(shape):
    return pl.BlockSpec(shape, lambda *_: (0,) * len(shape), pipeline_mode=pl.Buffered(1))


def _tile(t, pref):
    return pref if t % pref == 0 else t


def _dot(a, b):
    return jnp.dot(a.astype(bf16), b.astype(bf16), preferred_element_type=f32)


def _dot_nt(a, b):
    return lax.dot_general(a.astype(bf16), b.astype(bf16), NT, preferred_element_type=f32)


def _dot_tn(a, b):
    return lax.dot_general(a.astype(bf16), b.astype(bf16), TN, preferred_element_type=f32)


def _sigmoid(x):
    return 1.0 / (1.0 + jnp.exp(-x))


def _silu_grad(x):
    s = _sigmoid(x)
    return s * (1.0 + x * (1.0 - s))


def _rms(x, g):
    rstd = lax.rsqrt(jnp.mean(x * x, axis=-1, keepdims=True) + NORM_EPS)
    xh = x * rstd
    return xh * g, xh, rstd


def _rms_bwd(dn, xh, rstd, g):
    dxh = dn * g
    return rstd * (dxh - xh * jnp.mean(dxh * xh, axis=-1, keepdims=True))


def _acc_init(step, *refs):
    @pl.when(step == 0)
    def _():
        for r in refs:
            r[...] = jnp.zeros(r.shape, r.dtype)


def _acc_rows(ref, val):
    ref[0:1, :] += jnp.sum(val, axis=0, keepdims=True)


def _pw1_glu(x, g, w, b):
    t, d = x.shape
    tm = _tile(t, ROW_TILE)

    nb_w = w.shape[0]

    def body(x_ref, g_ref, w_hbm, b_ref, n_ref, u_ref, gl_ref, w_ref, sems):
        _fetch_blocks(pl.program_id(0), w_hbm, w_ref, sems, True)
        n, _, _ = _rms(x_ref[...], g_ref[...])
        nb = n.astype(bf16)
        n_ref[...] = nb
        u = jnp.dot(nb, w_ref[...], preferred_element_type=f32) + b_ref[...]
        u_ref[...] = u.astype(bf16)
        gl_ref[...] = u[:, :d] * _sigmoid(u[:, d:])

    return _pc(body, name="pw1_glu", grid=(t // tm,),
               in_specs=[_rows(tm, d), _const((1, d)), ANY, _const((1, 2 * d))],
               out_specs=[_rows(tm, d), _rows(tm, 2 * d), _rows(tm, d)],
               out_shape=[jax.ShapeDtypeStruct((t, d), bf16), jax.ShapeDtypeStruct((t, 2 * d), bf16),
                          jax.ShapeDtypeStruct((t, d), f32)],
               scratch=[pltpu.VMEM((d, 2 * d), bf16), pltpu.SemaphoreType.DMA((nb_w,))],
               sem=("arbitrary",))(x, g, w, b)


def _ln_silu_mm_res(dc, ln_g, ln_b, w, b, res):
    t, d = dc.shape
    tm = _tile(t, ROW_TILE)

    def body(x_ref, g_ref, bb_ref, w_ref, b_ref, r_ref, s_ref, o_ref):
        x = x_ref[...]
        xc = x - jnp.mean(x, axis=-1, keepdims=True)
        rstd = lax.rsqrt(jnp.mean(xc * xc, axis=-1, keepdims=True) + NORM_EPS)
        ln = xc * rstd * g_ref[...] + bb_ref[...]
        sb = (ln * _sigmoid(ln)).astype(bf16)
        s_ref[...] = sb
        o_ref[...] = r_ref[...] + jnp.dot(sb, w_ref[...], preferred_element_type=f32) + b_ref[...]

    return _pc(body, name="ln_silu_pw2", grid=(t // tm,),
               in_specs=[_rows(tm, d), _const((1, d)), _const((1, d)), _resident((d, d)), _const((1, d)), _rows(tm, d)],
               out_specs=[_rows(tm, d), _rows(tm, d)],
               out_shape=[jax.ShapeDtypeStruct((t, d), bf16), jax.ShapeDtypeStruct((t, d), f32)],
               sem=("parallel",))(dc, ln_g, ln_b, w, b, res)


def _fetch_blocks(step, w_hbm, dst, sems, by_cols, layer=None):
    nb_w = w_hbm.shape[0]
    step_rows, step_cols = w_hbm.shape[-2], w_hbm.shape[-1]

    @pl.when(step == 0)
    def _():
        copies = []
        for j in range(nb_w):
            src = w_hbm.at[j] if layer is None else w_hbm.at[j, layer]
            if by_cols:
                part = dst.at[:, pl.ds(j * step_cols, step_cols)]
            else:
                part = dst.at[pl.ds(j * step_rows, step_rows), :]
            copies.append(pltpu.make_async_copy(src, part, sems.at[j]))
        for cp in copies:
            cp.start()
        for cp in copies:
            cp.wait()


def _mlp_fwd(h, g, w1g, w2g, name, loss=None):
    t, d = h.shape
    nb_w, _, bn = w1g.shape
    ff = nb_w * bn
    tm = _tile(t, 256)
    n_in = 4 if loss is None else 6
    n_out = 4 if loss is None else 6

    def body(*refs):
        h_ref, g_ref, w1_hbm, w2_hbm = refs[:4]
        n_ref, f_ref, r_ref = refs[n_in:n_in + 3]
        w1_ref, w2_ref, sem1, sem2 = refs[n_in + n_out:]
        _fetch_blocks(pl.program_id(0), w1_hbm, w1_ref, sem1, True)
        _fetch_blocks(pl.program_id(0), w2_hbm, w2_ref, sem2, False)
        hv = h_ref[...]
        n, _, _ = _rms(hv, g_ref[...])
        nb = n.astype(bf16)
        n_ref[...] = nb
        f = jnp.dot(nb, w1_ref[...], preferred_element_type=f32)
        f_ref[...] = f.astype(bf16)
        rb = jnp.square(jnp.maximum(f, 0.0)).astype(bf16)
        r_ref[...] = rb
        out = hv + jnp.dot(rb, w2_ref[...], preferred_element_type=f32)
        if loss is None:
            refs[n_in + 3][...] = out
        else:
            gf_ref, t_ref = refs[4:6]
            loss_ref, dh_ref, dg_ref = refs[n_in + 3:n_in + 6]
            _acc_init(pl.program_id(0), loss_ref, dg_ref)
            gv = gf_ref[...]
            y, xh, rstd = _rms(out, gv)
            e = y - t_ref[...]
            loss_ref[...] += 0.5 * jnp.sum(jnp.mean(e * e, axis=-1, keepdims=True))
            dy = e * (1.0 / d)
            _acc_rows(dg_ref, dy * xh)
            dh_ref[...] = _rms_bwd(dy, xh, rstd, gv)

    in_specs = [_rows(tm, d), _const((1, d)), ANY, ANY]
    out_specs = [_rows(tm, d), _rows(tm, ff), _rows(tm, ff)]
    out_shape = [jax.ShapeDtypeStruct((t, d), bf16), jax.ShapeDtypeStruct((t, ff), bf16), jax.ShapeDtypeStruct((t, ff), bf16)]
    args = [h, g, w1g, w2g]
    if loss is None:
        out_specs.append(_rows(tm, d))
        out_shape.append(jax.ShapeDtypeStruct((t, d), f32))
    else:
        in_specs += [_const((1, d)), _rows(tm, d)]
        args += list(loss)
        out_specs += [_const((SUBLANES, LANES)), _rows(tm, d), _const((SUBLANES, d))]
        out_shape += [jax.ShapeDtypeStruct((SUBLANES, LANES), f32), jax.ShapeDtypeStruct((t, d), f32),
                      jax.ShapeDtypeStruct((SUBLANES, d), f32)]
    return _pc(body, name=name, grid=(t // tm,), in_specs=in_specs, out_specs=out_specs, out_shape=out_shape,
               scratch=[pltpu.VMEM((d, ff), bf16), pltpu.VMEM((ff, d), bf16),
                        pltpu.SemaphoreType.DMA((nb_w,)), pltpu.SemaphoreType.DMA((nb_w,))],
               sem=("arbitrary",))(*args)


def _softplus(x):
    return jnp.maximum(x, 0.0) + jnp.log(1.0 + jnp.exp(-jnp.abs(x)))


def _gdn_in(h, g, w_main, w_ab, a_log_pad, dt_pad, n_heads):
    t, d = h.shape
    tm = _tile(t, ROW_TILE)
    gate_rows = 2 * n_heads

    def body(h_ref, g_ref, wm_ref, wab_ref, al_ref, dt_ref, n_ref, qkv_ref, z_ref, ab_ref, gb_ref):
        n, _, _ = _rms(h_ref[...], g_ref[...])
        nb = n.astype(bf16)
        n_ref[...] = nb
        p = jnp.dot(nb, wm_ref[...], preferred_element_type=f32)
        qkv_ref[...] = p[:, :3 * d]
        z_ref[...] = p[:, 3 * d:]
        ab = jnp.dot(nb, wab_ref[...], preferred_element_type=f32)
        ab_ref[...] = ab
        lane = lax.broadcasted_iota(jnp.int32, ab.shape, 1)
        decay = -jnp.exp(al_ref[...]) * _softplus(ab + dt_ref[...])
        gates = jnp.where(lane < n_heads, decay, jnp.where(lane < gate_rows, _sigmoid(ab), 0.0))
        gb_ref[...] = gates.T[:gate_rows, :]

    return _pc(body, name="gdn_in", grid=(t // tm,),
               in_specs=[_rows(tm, d), _const((1, d)), _resident((d, 4 * d)), _resident((d, LANES)),
                         _const((1, LANES)), _const((1, LANES))],
               out_specs=[_rows(tm, d), _rows(tm, 3 * d), _rows(tm, d), _rows(tm, LANES),
                          pl.BlockSpec((gate_rows, tm), lambda i: (0, i))],
               out_shape=[jax.ShapeDtypeStruct((t, d), bf16), jax.ShapeDtypeStruct((t, 3 * d), f32),
                          jax.ShapeDtypeStruct((t, d), f32), jax.ShapeDtypeStruct((t, LANES), f32),
                          jax.ShapeDtypeStruct((gate_rows, t), f32)],
               sem=("parallel",))(h, g, w_main, w_ab, a_log_pad, dt_pad)


def _gated_norm_mm_res(o, z, ng, w, res, n_heads):
    t, d = o.shape
    tm = _tile(t, ROW_TILE)

    def body(o_ref, z_ref, ng_ref, w_ref, r_ref, on_ref, out_ref):
        for hd in range(n_heads):
            sl = slice(hd * LANES, (hd + 1) * LANES)
            rn, _, _ = _rms(o_ref[:, sl], ng_ref[...])
            zz = z_ref[:, sl]
            on_ref[:, sl] = (rn * (zz * _sigmoid(zz))).astype(bf16)
        out_ref[...] = r_ref[...] + jnp.dot(on_ref[...], w_ref[...], preferred_element_type=f32)

    return _pc(body, name="gated_norm_wout", grid=(t // tm,),
               in_specs=[_rows(tm, d), _rows(tm, d), _const((1, LANES)), _resident((d, d)), _rows(tm, d)],
               out_specs=[_rows(tm, d), _rows(tm, d)],
               out_shape=[jax.ShapeDtypeStruct((t, d), bf16), jax.ShapeDtypeStruct((t, d), f32)],
               sem=("parallel",))(o, z, ng, w, res)


def _mlp_bwd(dho, h, g, fb, w1g, w2g, name):
    t, d = h.shape
    nb_w, _, bn = w1g.shape
    ff = nb_w * bn
    tm = _tile(t, 256)

    def body(do_ref, h_ref, g_ref, f_ref, w1_hbm, w2_hbm, df_ref, dh_ref, dg_ref, cs_ref, w1_ref, w2_ref, sem1, sem2):
        _fetch_blocks(pl.program_id(0), w1_hbm, w1_ref, sem1, True)
        _fetch_blocks(pl.program_id(0), w2_hbm, w2_ref, sem2, False)
        _acc_init(pl.program_id(0), dg_ref, cs_ref)
        do = do_ref[...]
        dr = lax.dot_general(do.astype(bf16), w2_ref[...], NT, preferred_element_type=f32)
        dfb = (dr * (2.0 * jnp.maximum(f_ref[...].astype(f32), 0.0))).astype(bf16)
        df_ref[...] = dfb
        dn = lax.dot_general(dfb, w1_ref[...], NT, preferred_element_type=f32)
        gv = g_ref[...]
        _, xh, rstd = _rms(h_ref[...], gv)
        _acc_rows(dg_ref, dn * xh)
        dh = do + _rms_bwd(dn, xh, rstd, gv)
        dh_ref[...] = dh
        _acc_rows(cs_ref, dh)

    return _pc(body, name=name, grid=(t // tm,),
               in_specs=[_rows(tm, d), _rows(tm, d), _const((1, d)), _rows(tm, ff), ANY, ANY],
               out_specs=[_rows(tm, ff), _rows(tm, d), _const((SUBLANES, d)), _const((SUBLANES, d))],
               out_shape=[jax.ShapeDtypeStruct((t, ff), bf16), jax.ShapeDtypeStruct((t, d), f32),
                          jax.ShapeDtypeStruct((SUBLANES, d), f32), jax.ShapeDtypeStruct((SUBLANES, d), f32)],
               scratch=[pltpu.VMEM((d, ff), bf16), pltpu.VMEM((ff, d), bf16),
                        pltpu.SemaphoreType.DMA((nb_w,)), pltpu.SemaphoreType.DMA((nb_w,))],
               sem=("arbitrary",))(dho, h, g, fb, w1g, w2g)


class _Tail:
    def __init__(self, fn, ins, outs):
        self.fn, self.ins, self.outs = fn, ins, outs


def _mm_nt(pairs, name, tail=None):
    t = pairs[0][0].shape[0]
    tm = _tile(t, ROW_TILE)
    npair = len(pairs)
    in_specs, args, scratch, blocked = [], [], [], []
    k = None
    for dy, w in pairs:
        nn = dy.shape[1]
        if isinstance(w, tuple) and isinstance(w[0], str):
            w = w[1]
            k = w.shape[1]
            wspec = ANY
            blocked.append(True)
            scratch += [pltpu.VMEM((k, nn), bf16), pltpu.SemaphoreType.DMA((w.shape[0],))]
        elif isinstance(w, tuple):
            w, idx = w
            k = w.shape[0]
            wspec = pl.BlockSpec((k, nn), lambda *_, idx=idx: (0, idx), pipeline_mode=pl.Buffered(1))
            blocked.append(False)
        else:
            k = w.shape[0]
            wspec = _resident(w.shape)
            blocked.append(False)
        in_specs += [_rows(tm, nn), wspec]
        args += [dy, w]
    n_tin = len(tail.ins) if tail else 0
    n_out = len(tail.outs) if tail else 1
    if tail:
        for arr, kind in tail.ins:
            in_specs.append(_rows(tm, arr.shape[1]) if kind == "rows" else _const(arr.shape))
            args.append(arr)
        out_specs = [_rows(tm, c) if kind == "rows" else _const((SUBLANES, c)) for c, kind in tail.outs]
        out_shape = [jax.ShapeDtypeStruct((t, c) if kind == "rows" else (SUBLANES, c), f32) for c, kind in tail.outs]
    else:
        out_specs = _rows(tm, k)
        out_shape = jax.ShapeDtypeStruct((t, k), f32)

    def body(*refs):
        step = pl.program_id(0)
        tin = refs[2 * npair:2 * npair + n_tin]
        outs = refs[2 * npair + n_tin:2 * npair + n_tin + n_out]
        scr = list(refs[2 * npair + n_tin + n_out:])
        acc = None
        for p in range(npair):
            w_ref = refs[2 * p + 1]
            if blocked[p]:
                w_vmem, sems = scr.pop(0), scr.pop(0)
                _fetch_blocks(step, w_ref, w_vmem, sems, True)
                w_ref = w_vmem
            part = lax.dot_general(refs[2 * p][...].astype(bf16), w_ref[...], NT, preferred_element_type=f32)
            acc = part if acc is None else acc + part
        if tail is None:
            outs[0][...] = acc
        else:
            _acc_init(step, *[o for o, (_, kind) in zip(outs, tail.outs) if kind == "acc"])
            tail.fn(acc, tin, outs)

    sequential = tail is not None or any(blocked)
    return _pc(body, name=name, grid=(t // tm,), in_specs=in_specs, out_specs=out_specs, out_shape=out_shape,
               scratch=scratch, sem=("arbitrary",) if sequential else ("parallel",))(*args)


def _rms_bwd_tail(h, g, dres):
    def fn(dn, ins, outs):
        h_ref, g_ref, dr_ref = ins
        dh_ref, dg_ref = outs
        gv = g_ref[...]
        _, xh, rstd = _rms(h_ref[...], gv)
        _acc_rows(dg_ref, dn * xh)
        dh_ref[...] = dr_ref[...] + _rms_bwd(dn, xh, rstd, gv)

    d = h.shape[1]
    return _Tail(fn, [(h, "rows"), (g, "const"), (dres, "rows")], [(d, "rows"), (d, "acc")])


def _ln_silu_bwd_tail(dc, ln_g, ln_b):
    def fn(ds, ins, outs):
        x_ref, g_ref, b_ref = ins
        dx_ref, dg_ref, db_ref, cs_ref = outs
        x = x_ref[...]
        gv = g_ref[...]
        xc = x - jnp.mean(x, axis=-1, keepdims=True)
        rstd = lax.rsqrt(jnp.mean(xc * xc, axis=-1, keepdims=True) + NORM_EPS)
        xh = xc * rstd
        dln = ds * _silu_grad(xh * gv + b_ref[...])
        _acc_rows(dg_ref, dln * xh)
        _acc_rows(db_ref, dln)
        dxh = dln * gv
        dx = rstd * (dxh - jnp.mean(dxh, axis=-1, keepdims=True) - xh * jnp.mean(dxh * xh, axis=-1, keepdims=True))
        dx_ref[...] = dx
        _acc_rows(cs_ref, dx)

    d = dc.shape[1]
    return _Tail(fn, [(dc, "rows"), (ln_g, "const"), (ln_b, "const")], [(d, "rows"), (d, "acc"), (d, "acc"), (d, "acc")])


def _gated_norm_bwd_tail(o, z, ng, n_heads):
    def fn(don_all, ins, outs):
        o_ref, z_ref, ng_ref = ins
        do_ref, dz_ref, dng_ref = outs
        gv = ng_ref[...]
        for hd in range(n_heads):
            sl = slice(hd * LANES, (hd + 1) * LANES)
            rn, xh, rstd = _rms(o_ref[:, sl], gv)
            zz = z_ref[:, sl]
            don = don_all[:, sl]
            dz_ref[:, sl] = don * rn * _silu_grad(zz)
            drn = don * (zz * _sigmoid(zz))
            _acc_rows(dng_ref, drn * xh)
            do_ref[:, sl] = _rms_bwd(drn, xh, rstd, gv)

    d = o.shape[1]
    return _Tail(fn, [(o, "rows"), (z, "rows"), (ng, "const")], [(d, "rows"), (d, "rows"), (LANES, "acc")])


def _mm_tn_blocked(x, dy, name, out_dtype=f32):
    t, k = x.shape
    bn = dy.shape[1] // N_DEV
    tm = _tile(t, 2048 if k <= 1024 else 1024)
    jb = N_DEV
    while jb > 1 and k * jb * bn * 4 > 8 * 1024 * 1024:
        jb //= 2
    nt = t // tm

    def body(x_ref, dy_ref, o_ref, *acc):
        acc_ref = acc[0] if acc else o_ref
        _acc_init(pl.program_id(1), acc_ref)
        xt = x_ref[...].astype(bf16).T
        for jj in range(jb):
            acc_ref[jj] += jnp.dot(xt, dy_ref[:, jj * bn:(jj + 1) * bn].astype(bf16), preferred_element_type=f32)
        if acc:
            @pl.when(pl.program_id(1) == nt - 1)
            def _():
                o_ref[...] = acc_ref[...].astype(out_dtype)

    return _pc(body, name=name, grid=(N_DEV // jb, nt),
               in_specs=[pl.BlockSpec((tm, k), lambda j, i: (i, 0)), pl.BlockSpec((tm, jb * bn), lambda j, i: (i, j))],
               out_specs=pl.BlockSpec((jb, k, bn), lambda j, i: (j, 0, 0)),
               out_shape=jax.ShapeDtypeStruct((N_DEV, k, bn), out_dtype),
               scratch=[] if out_dtype == f32 else [pltpu.VMEM((jb, k, bn), f32)],
               sem=("parallel", "arbitrary"))(x, dy)


def _mm_tn(x, dy, name, out_dtype=f32):
    t, k = x.shape
    n = dy.shape[1]
    tm = _tile(t, 2048 if k <= 1024 else 1024)
    cap = max(LANES, (2 * 1024 * 1024) // k)
    tn = n
    if n > cap:
        tn = max(c for c in range(LANES, cap + 1, LANES) if n % c == 0)
    nt = t // tm

    def body(x_ref, dy_ref, o_ref, *acc):
        acc_ref = acc[0] if acc else o_ref
        _acc_init(pl.program_id(1), acc_ref)
        acc_ref[...] += lax.dot_general(x_ref[...].astype(bf16), dy_ref[...].astype(bf16), TN, preferred_element_type=f32)
        if acc:
            @pl.when(pl.program_id(1) == nt - 1)
            def _():
                o_ref[...] = acc_ref[...].astype(out_dtype)

    return _pc(body, name=name, grid=(n // tn, nt),
               in_specs=[pl.BlockSpec((tm, k), lambda j, i: (i, 0)), pl.BlockSpec((tm, tn), lambda j, i: (i, j))],
               out_specs=pl.BlockSpec((k, tn), lambda j, i: (0, j)),
               out_shape=jax.ShapeDtypeStruct((k, n), out_dtype),
               scratch=[] if out_dtype == f32 else [pltpu.VMEM((k, tn), f32)],
               sem=("parallel", "arbitrary"))(x, dy)


def _gates_bwd(dgb_t, ab, a_log_pad, dt_pad, n_heads):
    t = ab.shape[0]
    tm = _tile(t, ROW_TILE)
    gate_rows = 2 * n_heads

    def body(dgb_ref, ab_ref, al_ref, dt_ref, dab_ref, dal_ref, ddt_ref):
        _acc_init(pl.program_id(0), dal_ref, ddt_ref)
        ab = ab_ref[...]
        dgb = jnp.concatenate([dgb_ref[...], jnp.zeros((LANES - gate_rows, tm), f32)], axis=0).T
        lane = lax.broadcasted_iota(jnp.int32, ab.shape, 1)
        is_a = lane < n_heads
        is_b = jnp.logical_and(lane >= n_heads, lane < 2 * n_heads)
        xa = ab + dt_ref[...]
        neg_a = -jnp.exp(al_ref[...])
        dg_da = neg_a * _sigmoid(xa)
        beta = _sigmoid(ab)
        da = jnp.where(is_a, dgb * dg_da, 0.0)
        dab_ref[...] = da + jnp.where(is_b, dgb * beta * (1.0 - beta), 0.0)
        _acc_rows(dal_ref, jnp.where(is_a, dgb * neg_a * _softplus(xa), 0.0))
        _acc_rows(ddt_ref, da)

    return _pc(body, name="gates_bwd", grid=(t // tm,),
               in_specs=[pl.BlockSpec((gate_rows, tm), lambda i: (0, i)), _rows(tm, LANES), _const((1, LANES)), _const((1, LANES))],
               out_specs=[_rows(tm, LANES), _const((SUBLANES, LANES)), _const((SUBLANES, LANES))],
               out_shape=[jax.ShapeDtypeStruct((t, LANES), f32), jax.ShapeDtypeStruct((SUBLANES, LANES), f32),
                          jax.ShapeDtypeStruct((SUBLANES, LANES), f32)],
               sem=("arbitrary",))(dgb_t, ab, a_log_pad, dt_pad)


def _glu_bwd(dgl, ub):
    t, d = dgl.shape
    tm = _tile(t, ROW_TILE)

    def body(dgl_ref, u_ref, du_ref, cs_ref):
        _acc_init(pl.program_id(0), cs_ref)
        dgl = dgl_ref[...]
        a = u_ref[:, :d].astype(f32)
        sb = _sigmoid(u_ref[:, d:].astype(f32))
        da = dgl * sb
        db = dgl * a * sb * (1.0 - sb)
        du_ref[:, :d] = da.astype(bf16)
        du_ref[:, d:] = db.astype(bf16)
        cs_ref[0:1, :d] += jnp.sum(da, axis=0, keepdims=True)
        cs_ref[0:1, d:] += jnp.sum(db, axis=0, keepdims=True)

    return _pc(body, name="glu_bwd", grid=(t // tm,),
               in_specs=[_rows(tm, d), _rows(tm, 2 * d)],
               out_specs=[_rows(tm, 2 * d), _const((SUBLANES, 2 * d))],
               out_shape=[jax.ShapeDtypeStruct((t, 2 * d), bf16), jax.ShapeDtypeStruct((SUBLANES, 2 * d), f32)],
               sem=("arbitrary",))(dgl, ub)


def _conv_rows(s):
    return 256 if s % 256 == 0 else s


def _conv_tap_sum(pad_ref, w_ref, base, rows, width):
    acc = jnp.zeros((rows, LANES), f32)
    for j in range(width):
        acc = acc + w_ref[j:j + 1, :] * pad_ref[pl.ds(base + CONV_PAD - (width - 1) + j, rows), :]
    return acc


def _l2_silu_post(c, j, n_heads, scale):
    a = c * _sigmoid(c)
    r = lax.rsqrt(jnp.sum(a * a, axis=-1, keepdims=True) + L2_EPS)
    mult = jnp.where(j < n_heads, r * scale, jnp.where(j < 2 * n_heads, r, 1.0))
    return a, r, a * mult


def _dwconv_fwd(x, w, b, name, qk_heads=None):
    bl, s, cn = x.shape
    width = w.shape[0]
    rows = _conv_rows(s)
    scale = float(LANES) ** -0.5

    def body(x_ref, w_ref, b_ref, o_ref, pad_ref):
        j = pl.program_id(1)
        pad_ref[0:CONV_PAD, :] = jnp.zeros((CONV_PAD, LANES), f32)
        pad_ref[CONV_PAD:, :] = x_ref[0]

        def step(i, carry):
            base = pl.multiple_of(i * rows, rows)
            acc = _conv_tap_sum(pad_ref, w_ref, base, rows, width)
            if qk_heads is None:
                acc = acc + b_ref[...]
            else:
                _, _, acc = _l2_silu_post(acc, j, qk_heads, scale)
            o_ref[0, pl.ds(base, rows), :] = acc
            return carry

        lax.fori_loop(0, s // rows, step, 0)

    return _pc(body, name=name, grid=(bl, cn // LANES),
               in_specs=[pl.BlockSpec((1, s, LANES), lambda bi, j: (bi, 0, j)),
                         pl.BlockSpec((width, LANES), lambda bi, j: (0, j)),
                         pl.BlockSpec((1, LANES), lambda bi, j: (0, j))],
               out_specs=pl.BlockSpec((1, s, LANES), lambda bi, j: (bi, 0, j)),
               out_shape=jax.ShapeDtypeStruct((bl, s, cn), f32),
               scratch=[pltpu.VMEM((s + CONV_PAD, LANES), f32)],
               sem=("parallel", "parallel"))(x, w, b)


def _dwconv_bwd(x, dys, w, name, qk_heads=None):
    bl, s, cn = x.shape
    width = w.shape[0]
    wp = -(-width // SUBLANES) * SUBLANES
    rows = _conv_rows(s)
    scale = float(LANES) ** -0.5
    nblk = s // rows
    ndy = len(dys)

    def body(*refs):
        x_ref, w_ref = refs[0], refs[1]
        dy_refs = refs[2:2 + ndy]
        dx_ref, dw_ref, db_ref, xpad, dypad, acc = refs[2 + ndy:]
        j = pl.program_id(0)
        bi = pl.program_id(1)
        _acc_init(bi, acc, db_ref)
        xpad[0:CONV_PAD, :] = jnp.zeros((CONV_PAD, LANES), f32)
        xpad[CONV_PAD:, :] = x_ref[0]
        dypad[s:, :] = jnp.zeros((CONV_PAD, LANES), f32)
        if qk_heads is None:
            dypad[0:s, :] = dy_refs[0][0]
        else:
            def pre(i, carry):
                base = pl.multiple_of(i * rows, rows)
                c = _conv_tap_sum(xpad, w_ref, base, rows, width)
                a, r, _ = _l2_silu_post(c, j, qk_heads, scale)
                dq = dy_refs[0][0, pl.ds(base, rows), :]
                dk = dy_refs[1][0, pl.ds(base, rows), :]
                dv = dy_refs[2][0, pl.ds(base, rows), :]
                dy = jnp.where(j < qk_heads, dq * scale, jnp.where(j < 2 * qk_heads, dk, dv))
                da_l2 = r * (dy - a * (r * r) * jnp.sum(a * dy, axis=-1, keepdims=True))
                da = jnp.where(j < 2 * qk_heads, da_l2, dy)
                dypad[pl.ds(base, rows), :] = da * _silu_grad(c)
                return carry

            lax.fori_loop(0, nblk, pre, 0)

        def step(i, carry):
            base = pl.multiple_of(i * rows, rows)
            dxa = jnp.zeros((rows, LANES), f32)
            for jj in range(width):
                dxa = dxa + w_ref[jj:jj + 1, :] * dypad[pl.ds(base + (width - 1) - jj, rows), :]
            dx_ref[0, pl.ds(base, rows), :] = dxa
            dyc = dypad[pl.ds(base, rows), :]
            db_ref[...] += dyc.reshape(rows // SUBLANES, SUBLANES, LANES).sum(axis=0)
            for jj in range(width):
                prod = dyc * xpad[pl.ds(base + CONV_PAD - (width - 1) + jj, rows), :]
                acc[jj * SUBLANES:(jj + 1) * SUBLANES, :] += prod.reshape(rows // SUBLANES, SUBLANES, LANES).sum(axis=0)
            return carry

        lax.fori_loop(0, nblk, step, 0)

        @pl.when(bi == bl - 1)
        def _():
            dw_ref[...] = jnp.zeros((wp, LANES), f32)
            for jj in range(width):
                dw_ref[jj:jj + 1, :] = jnp.sum(acc[jj * SUBLANES:(jj + 1) * SUBLANES, :], axis=0, keepdims=True)

    if qk_heads is None:
        dy_specs = [pl.BlockSpec((1, s, LANES), lambda j, bi: (bi, 0, j))]
    else:
        hh = qk_heads
        def dy_spec(part):
            def index(j, bi):
                mine = jnp.logical_and(j >= part * hh, j < (part + 1) * hh)
                return (jnp.where(mine, bi * hh + j - part * hh, 0), 0, 0)
            return pl.BlockSpec((1, s, LANES), index)

        dy_specs = [dy_spec(0), dy_spec(1), dy_spec(2)]
    return _pc(body, name=name, grid=(cn // LANES, bl),
               in_specs=[pl.BlockSpec((1, s, LANES), lambda j, bi: (bi, 0, j)),
                         pl.BlockSpec((width, LANES), lambda j, bi: (0, j))] + dy_specs,
               out_specs=[pl.BlockSpec((1, s, LANES), lambda j, bi: (bi, 0, j)),
                          pl.BlockSpec((wp, LANES), lambda j, bi: (0, j)),
                          pl.BlockSpec((SUBLANES, LANES), lambda j, bi: (0, j))],
               out_shape=[jax.ShapeDtypeStruct((bl, s, cn), f32), jax.ShapeDtypeStruct((wp, cn), f32),
                          jax.ShapeDtypeStruct((SUBLANES, cn), f32)],
               scratch=[pltpu.VMEM((s + CONV_PAD, LANES), f32), pltpu.VMEM((s + CONV_PAD, LANES), f32),
                        pltpu.VMEM((width * SUBLANES, LANES), f32)],
               sem=("parallel", "arbitrary"))(x, w, *dys)


def _group_rows(s):
    for rows in (256, 128):
        if s % rows == 0:
            return rows
    return CHUNK


def _group_masks(rows):
    r = lax.broadcasted_iota(jnp.int32, (rows, rows), 0)
    c = lax.broadcasted_iota(jnp.int32, (rows, rows), 1)
    return r, c, r >= c


def _decay(gc_col, gc_row, causal):
    return jnp.exp(jnp.where(causal, gc_col - gc_row, NEG))


def _inv_unit_lower_many(mats, r, c):
    n = r.shape[0]
    eye = (r == c).astype(f32)
    same16 = (r >> 4) == (c >> 4)
    ads = [jnp.where(same16, a, 0.0) for a in mats]
    aos = [a - ad for a, ad in zip(mats, ads)]
    xs = ads
    tds = [-x for x in xs]
    for _ in range(3):
        xs = [_dot(x, x) for x in xs]
        tds = [td + x + _dot(td, x) for td, x in zip(tds, xs)]
    bs = [ao + _dot(td, ao) for td, ao in zip(tds, aos)]
    ps = [-b for b in bs]
    span = 2
    while span < n // 16:
        bs = [_dot(b, b) for b in bs]
        ps = [p + b + _dot(p, b) for p, b in zip(ps, bs)]
        span *= 2
    return [p + td + _dot(p, td) for p, td in zip(ps, tds)]


def _lane_cumsum(x, y, reverse, name):
    rr, n = x.shape

    def body(x_ref, y_ref, o_ref):
        i = lax.broadcasted_iota(jnp.int32, (n, n), 0)
        j = lax.broadcasted_iota(jnp.int32, (n, n), 1)
        tri = ((i >= j) if reverse else (i <= j)).astype(f32)
        o_ref[...] = jnp.dot(x_ref[...] + y_ref[...], tri, precision=HI, preferred_element_type=f32)

    spec = pl.BlockSpec((rr, n), lambda: (0, 0))
    return pl.pallas_call(body, name=name, in_specs=[spec, spec], out_specs=spec,
                          out_shape=jax.ShapeDtypeStruct((rr, n), f32))(x, y)


def _gdn_specs(n_heads, nblk, rows, hp=1, rev=False):
    def blk(n):
        return nblk - 1 - n if rev else n

    def qkv(off):
        return pl.BlockSpec((rows, hp * LANES), lambda g, n: (
            lax.div(g * hp, n_heads) * nblk + blk(n), lax.div(off * n_heads + lax.rem(g * hp, n_heads), hp)))

    def per_head(last, block_rows=rows):
        return pl.BlockSpec((hp, block_rows, last), lambda g, n: (g, blk(n), 0))

    def row_vec():
        return pl.BlockSpec((hp, 1, 1, rows), lambda g, n: (g, blk(n), 0, 0))

    return qkv, per_head, row_vec


def _gdn_prep(qkv, cols, grow, bl, s, n_heads):
    rows = _group_rows(s)
    nblk = s // rows
    bh_n = bl * n_heads
    hp = PREP_HEADS_PER_STEP if n_heads % PREP_HEADS_PER_STEP == 0 else 1
    qkv_spec, ph, rv = _gdn_specs(n_heads, nblk, rows, hp)

    def body(k_ref, v_ref, cols_ref, grow_ref, u_ref, w_ref, t_ref):
        r, c, causal = _group_masks(rows)
        mats, rhs = [], []
        for h in range(hp):
            hs = slice(h * LANES, (h + 1) * LANES)
            k = k_ref[:, hs]
            gc = cols_ref[h, :, 0:1]
            beta = cols_ref[h, :, 1:2]
            kb = k * beta
            mats.append(jnp.where(r > c, _dot_nt(kb, k) * _decay(gc, grow_ref[h, 0], causal), 0.0))
            rhs.append((v_ref[:, hs] * beta, kb * jnp.exp(gc)))
        for h, tm in enumerate(_inv_unit_lower_many(mats, r, c)):
            tb = tm.astype(bf16)
            u_ref[h] = rhs[h][0] + jnp.dot(tb, rhs[h][0].astype(bf16), preferred_element_type=f32)
            w_ref[h] = (rhs[h][1] + jnp.dot(tb, rhs[h][1].astype(bf16), preferred_element_type=f32)).astype(bf16)
            t_ref[h] = tb

    return _pc(body, name="gdn_prep", grid=(bh_n // hp, nblk),
               in_specs=[qkv_spec(1), qkv_spec(2), ph(2), rv()],
               out_specs=[ph(LANES), ph(LANES), ph(rows)],
               out_shape=[jax.ShapeDtypeStruct((bh_n, s, LANES), f32), jax.ShapeDtypeStruct((bh_n, s, LANES), bf16),
                          jax.ShapeDtypeStruct((bh_n, s, rows), bf16)],
               sem=("parallel", "parallel"))(qkv, qkv, cols, grow)


def _gdn_scan(qkv, u, w, cols, grow, bl, s, n_heads):
    rows = _group_rows(s)
    nblk = s // rows
    bh_n = bl * n_heads
    d = n_heads * LANES
    hp = HEADS_PER_STEP if n_heads % HEADS_PER_STEP == 0 else 1
    qkv_spec, ph, rv = _gdn_specs(n_heads, nblk, rows, hp)

    def body(q_ref, k_ref, u_ref, w_ref, cols_ref, grow_ref, o_ref, vn_ref, ss_ref, s_scr):
        _acc_init(pl.program_id(1), s_scr)
        _, _, causal = _group_masks(rows)
        hh = range(hp)
        qs = [q_ref[:, h * LANES:(h + 1) * LANES] for h in hh]
        ks = [k_ref[:, h * LANES:(h + 1) * LANES] for h in hh]
        gcs = [cols_ref[h, :, 0:1] for h in hh]
        ps = [_dot_nt(qs[h], ks[h]) * _decay(gcs[h], grow_ref[h, 0], causal) for h in hh]
        sts = [s_scr[h] for h in hh]
        for h in hh:
            ss_ref[h] = sts[h].astype(bf16)
        vns = [u_ref[h] - _dot(w_ref[h], sts[h]) for h in hh]
        for h in hh:
            vn_ref[h] = vns[h].astype(bf16)
        o_state = [_dot(qs[h] * jnp.exp(gcs[h]), sts[h]) for h in hh]
        o_intra = [_dot(ps[h], vns[h]) for h in hh]
        for h in hh:
            o_ref[:, h * LANES:(h + 1) * LANES] = o_state[h] + o_intra[h]
        for h in hh:
            g_last = gcs[h][rows - 1:rows, :]
            s_scr[h] = jnp.exp(g_last) * sts[h] + _dot_tn(ks[h] * jnp.exp(g_last - gcs[h]), vns[h])

    return _pc(body, name="gdn_scan", grid=(bh_n // hp, nblk),
               in_specs=[qkv_spec(0), qkv_spec(1), ph(LANES), ph(LANES), ph(2), rv()],
               out_specs=[qkv_spec(0), ph(LANES), ph(LANES, block_rows=LANES)],
               out_shape=[jax.ShapeDtypeStruct((bl * s, d), f32), jax.ShapeDtypeStruct((bh_n, s, LANES), bf16),
                          jax.ShapeDtypeStruct((bh_n, nblk * LANES, LANES), bf16)],
               scratch=[pltpu.VMEM((hp, LANES, LANES), f32)],
               sem=("parallel", "arbitrary"))(qkv, qkv, u, w, cols, grow)


def _gdn_scan_bwd(do, qkv, w, vn, cols, grow, ss, bl, s, n_heads):
    rows = _group_rows(s)
    nblk = s // rows
    bh_n = bl * n_heads
    hp = HEADS_PER_STEP if n_heads % HEADS_PER_STEP == 0 else 1
    qkv_spec, ph, rv = _gdn_specs(n_heads, nblk, rows, hp, rev=True)

    def body(do_ref, q_ref, k_ref, w_ref, vn_ref, cols_ref, grow_ref, ss_ref,
             du_ref, dw_ref, dq_ref, dk_ref, dcol_ref, drow_ref, ds_scr):
        _acc_init(pl.program_id(1), ds_scr)
        _, _, causal = _group_masks(rows)
        last_row = lax.broadcasted_iota(jnp.int32, (rows, 1), 0) == rows - 1
        hh = range(hp)
        dos = [do_ref[:, h * LANES:(h + 1) * LANES] for h in hh]
        qs = [q_ref[:, h * LANES:(h + 1) * LANES] for h in hh]
        ks = [k_ref[:, h * LANES:(h + 1) * LANES] for h in hh]
        vns = [vn_ref[h] for h in hh]
        gcs = [cols_ref[h, :, 0:1] for h in hh]
        sts = [ss_ref[h] for h in hh]
        dss = [ds_scr[h] for h in hh]
        dmats = [_decay(gcs[h], grow_ref[h, 0], causal) for h in hh]
        gams = [jnp.exp(gc) for gc in gcs]
        qgs = [qs[h] * gams[h] for h in hh]
        g_lasts = [gc[rows - 1:rows, :] for gc in gcs]
        kd_scales = [jnp.exp(g_lasts[h] - gcs[h]) for h in hh]
        kdecs = [ks[h] * kd_scales[h] for h in hh]
        qks = [_dot_nt(qs[h], ks[h]) for h in hh]
        dpds = [_dot_nt(dos[h], vns[h]) * dmats[h] for h in hh]
        dvns = [_dot_tn(qks[h] * dmats[h], dos[h]) + _dot(kdecs[h], dss[h]) for h in hh]
        for h in hh:
            du_ref[h] = dvns[h]
        dkdecs = [_dot_nt(vns[h], dss[h]) for h in hh]
        for h in hh:
            dw_ref[h] = -_dot_nt(dvns[h], sts[h])
        dqgs = [_dot_nt(dos[h], sts[h]) for h in hh]
        dq_intra = [_dot(dpds[h], ks[h]) for h in hh]
        dk_intra = [_dot_tn(dpds[h], qs[h]) for h in hh]
        for h in hh:
            dq_ref[h] = dqgs[h] * gams[h] + dq_intra[h]
            dk_ref[h] = dk_intra[h] + dkdecs[h] * kd_scales[h]
            ep = dpds[h] * qks[h]
            drow_ref[h, 0] = -jnp.sum(ep, axis=0, keepdims=True)
            kd_rows = jnp.sum(dkdecs[h] * kdecs[h], axis=-1, keepdims=True)
            extra = jnp.sum(kd_rows) + jnp.exp(g_lasts[h]) * jnp.sum(sts[h] * dss[h])
            dcol_ref[h] = (jnp.sum(dqgs[h] * qgs[h], axis=-1, keepdims=True) + jnp.sum(ep, axis=-1, keepdims=True)
                           - kd_rows + jnp.where(last_row, extra, 0.0))
        ds_new = [jnp.exp(g_lasts[h]) * dss[h] + _dot_tn(qgs[h], dos[h]) - _dot_tn(w_ref[h], dvns[h]) for h in hh]
        for h in hh:
            ds_scr[h] = ds_new[h]

    return _pc(body, name="gdn_scan_bwd", grid=(bh_n // hp, nblk),
               in_specs=[qkv_spec(0), qkv_spec(0), qkv_spec(1), ph(LANES), ph(LANES), ph(2), rv(),
                         ph(LANES, block_rows=LANES)],
               out_specs=[ph(LANES), ph(LANES), ph(LANES), ph(LANES), ph(1), rv()],
               out_shape=[jax.ShapeDtypeStruct((bh_n, s, LANES), f32)] * 4
               + [jax.ShapeDtypeStruct((bh_n, s, 1), f32), jax.ShapeDtypeStruct((bh_n, nblk, 1, rows), f32)],
               scratch=[pltpu.VMEM((hp, LANES, LANES), f32)],
               sem=("parallel", "arbitrary"))(do, qkv, qkv, w, vn, cols, grow, ss)


def _gdn_prep_bwd(qkv, cols, grow, tmat, du, dw, dk_scan, dcol_scan, drow_scan, bl, s, n_heads):
    rows = _group_rows(s)
    nblk = s // rows
    bh_n = bl * n_heads
    hp = PREP_HEADS_PER_STEP if n_heads % PREP_HEADS_PER_STEP == 0 else 1
    qkv_spec, ph, rv = _gdn_specs(n_heads, nblk, rows, hp)

    def body(k_ref, v_ref, cols_ref, grow_ref, t_ref, du_ref, dw_ref, dks_ref, dcs_ref, drs_ref,
             dk_ref, dv_ref, dcols_ref, drow_ref):
        r, c, causal = _group_masks(rows)
        hh = range(hp)
        ks = [k_ref[:, h * LANES:(h + 1) * LANES] for h in hh]
        vs = [v_ref[:, h * LANES:(h + 1) * LANES] for h in hh]
        gcs = [cols_ref[h, :, 0:1] for h in hh]
        betas = [cols_ref[h, :, 1:2] for h in hh]
        tms = [t_ref[h] for h in hh]
        dus = [du_ref[h] for h in hh]
        dws = [dw_ref[h] for h in hh]
        gams = [jnp.exp(gc) for gc in gcs]
        kbs = [k * b for k, b in zip(ks, betas)]
        kbgs = [kb * g for kb, g in zip(kbs, gams)]
        dts = [_dot_nt(dus[h], vs[h] * betas[h]) + _dot_nt(dws[h], kbgs[h]) for h in hh]
        dvbs = [dus[h] + _dot_tn(tms[h], dus[h]) for h in hh]
        dkbgs = [dws[h] + _dot_tn(tms[h], dws[h]) for h in hh]
        kks = [_dot_nt(kbs[h], ks[h]) for h in hh]
        inner = [dts[h] + _dot_nt(dts[h], tms[h]) for h in hh]
        dads = [jnp.where(r > c, -(inner[h] + _dot_tn(tms[h], inner[h])), 0.0) * _decay(gcs[h], grow_ref[h, 0], causal)
                for h in hh]
        dkbs = [dkbgs[h] * gams[h] + _dot(dads[h], ks[h]) for h in hh]
        dk2 = [_dot_tn(dads[h], kbs[h]) for h in hh]
        for h in hh:
            dk_ref[h] = dks_ref[h] + dk2[h] + dkbs[h] * betas[h]
            dv_ref[h] = dvbs[h] * betas[h]
            ea = dads[h] * kks[h]
            dcols_ref[h, :, 0:1] = (dcs_ref[h] + jnp.sum(dkbgs[h] * kbgs[h], axis=-1, keepdims=True)
                                    + jnp.sum(ea, axis=-1, keepdims=True))
            dcols_ref[h, :, 1:2] = (jnp.sum(dvbs[h] * vs[h], axis=-1, keepdims=True)
                                    + jnp.sum(dkbs[h] * ks[h], axis=-1, keepdims=True))
            drow_ref[h, 0] = drs_ref[h, 0] - jnp.sum(ea, axis=0, keepdims=True)

    return _pc(body, name="gdn_prep_bwd", grid=(bh_n // hp, nblk),
               in_specs=[qkv_spec(1), qkv_spec(2), ph(2), rv(), ph(rows), ph(LANES), ph(LANES), ph(LANES), ph(1), rv()],
               out_specs=[ph(LANES), ph(LANES), ph(2), rv()],
               out_shape=[jax.ShapeDtypeStruct((bh_n, s, LANES), f32), jax.ShapeDtypeStruct((bh_n, s, LANES), f32),
                          jax.ShapeDtypeStruct((bh_n, s, 2), f32), jax.ShapeDtypeStruct((bh_n, nblk, 1, rows), f32)],
               sem=("parallel", "parallel"))(qkv, qkv, cols, grow, tmat, du, dw, dk_scan, dcol_scan, drow_scan)


def _row(v):
    return v.reshape(1, -1).astype(f32)


def _pad_lanes(v):
    v = v.reshape(1, -1).astype(f32)
    return jnp.pad(v, ((0, 0), (0, LANES - v.shape[1])))


def _local_step(x, tgt, p, need, emit):
    bl, s, d = x.shape
    t = bl * s
    n_heads = p["gdn_a_log"].shape[-1]
    assert d == n_heads * LANES and s % CHUNK == 0
    x2 = x.reshape(t, d)
    tgt2 = tgt.reshape(t, d)
    gr = {}

    n0, ub, gl = _pw1_glu(x2, _row(p["norm_mix_g"][0]), p["cv_w_pw1"], _row(p["cv_b_pw1"]))
    cv = need("conv", n0)
    dc = _dwconv_fwd(gl.reshape(bl, s, d), cv["cv_w_dw"], _row(p["cv_b_dw"]), "dwconv_fwd").reshape(t, d)
    sb, h1 = _ln_silu_mm_res(dc, _row(p["cv_ln_g"]), _row(p["cv_ln_b"]), cv["cv_w_pw2"], _row(p["cv_b_pw2"]), x2)
    m0 = need("mlp0", h1)
    n1, f0, r0, h2 = _mlp_fwd(h1, _row(p["norm_ffn_g"][0]), m0["w1"], m0["w2"], "mlp_fwd0")

    gd = need("gdn", h2)
    w_in = gd["w_in"]
    w_ab = jnp.pad(w_in[:, 4 * d:], ((0, 0), (0, LANES - 2 * n_heads)))
    a_log_pad = _pad_lanes(p["gdn_a_log"])
    dt_pad = _pad_lanes(p["gdn_dt_bias"])
    n2, qkv_pre, z, ab, gbeta = _gdn_in(h2, _row(p["norm_mix_g"][1]), w_in, w_ab, a_log_pad, dt_pad, n_heads)
    zero_bias = jnp.zeros((1, 3 * d), f32)
    qkv = _dwconv_fwd(qkv_pre.reshape(bl, s, 3 * d), cv["gdn_conv_w"], zero_bias, "sconv_fwd", qk_heads=n_heads).reshape(t, 3 * d)
    bh_n, rows = bl * n_heads, _group_rows(s)
    gates = gbeta.reshape(2, n_heads, bl, s).transpose(0, 2, 1, 3).reshape(2, bh_n * (s // rows), rows)
    gc_lanes = _lane_cumsum(gates[0], jnp.zeros_like(gates[0]), False, "gdn_gate_cumsum")
    grow = gc_lanes.reshape(bh_n, s // rows, 1, rows)
    cols = jnp.stack([gc_lanes.reshape(bh_n, s), gates[1].reshape(bh_n, s)], axis=-1)
    u, w, tmat = _gdn_prep(qkv, cols, grow, bl, s, n_heads)
    o, vn, ss = _gdn_scan(qkv, u, w, cols, grow, bl, s, n_heads)
    onb, h3 = _gated_norm_mm_res(o, z, _row(p["gdn_norm_g"]), gd["w_out"], h2, n_heads)
    m1 = need("mlp1", h3)
    n3, f1, r1, loss_acc, dh4, dgf = _mlp_fwd(h3, _row(p["norm_ffn_g"][1]), m1["w1"], m1["w2"], "mlp_fwd1_loss",
                                              loss=(_row(p["final_norm_g"]), tgt2))
    loss = loss_acc[0, 0]
    gr["loss"] = loss_acc[0, 0:1]
    gr["final_norm_g"] = dgf[0]

    df1, dh3, dg_ffn1, _ = _mlp_bwd(dh4, h3, _row(p["norm_ffn_g"][1]), f1, m1["w1"], m1["w2"], "mlp_bwd1")
    dw2_1 = _mm_tn(r1, dh4, "dw_mlp2_1", bf16)
    dw1_1 = _mm_tn_blocked(n3, df1, "dw_mlp1_1", bf16)

    dw_out = _mm_tn(onb, dh3, "dw_gdn_out", bf16)
    tie = emit("late", {"mlp_w2_1": dw2_1, "mlp_w1_1": dw1_1, "gdn_w_out": dw_out})
    do, dz, dng = _mm_nt([(dh3, gd["w_out"])], "dx_gdn_out",
                         _gated_norm_bwd_tail(o, z, _row(p["gdn_norm_g"]) + tie, n_heads))
    gr["gdn_norm_g"] = dng[0]
    du, dw_, dq, dk_scan, dcol_scan, drow_scan = _gdn_scan_bwd(do, qkv, w, vn, cols, grow, ss, bl, s, n_heads)
    dk, dv, dcols, drow = _gdn_prep_bwd(qkv, cols, grow, tmat, du, dw_, dk_scan, dcol_scan, drow_scan, bl, s, n_heads)
    dg_lanes = _lane_cumsum(dcols[..., 0].reshape(gc_lanes.shape), drow.reshape(gc_lanes.shape), True, "gdn_gate_cumsum_bwd")
    dqkv_pre, dconv_w, _ = _dwconv_bwd(qkv_pre.reshape(bl, s, 3 * d), [dq, dk, dv], cv["gdn_conv_w"], "sconv_bwd", qk_heads=n_heads)
    gr["gdn_conv_w"] = dconv_w[:cv["gdn_conv_w"].shape[0]]
    dgb_t = jnp.stack([dg_lanes.reshape(bl, n_heads, s), dcols[..., 1].reshape(bl, n_heads, s)])
    dgb_t = dgb_t.transpose(0, 2, 1, 3).reshape(2 * n_heads, t)
    dab, dal, ddt = _gates_bwd(dgb_t, ab, a_log_pad, dt_pad, n_heads)
    gr["gdn_a_log"] = dal[0, :n_heads]
    gr["gdn_dt_bias"] = ddt[0, :n_heads]
    dqkv2 = dqkv_pre.reshape(t, 3 * d)
    dw_in = jnp.concatenate(
        [_mm_tn(n2, dqkv2, "dw_gdn_in_qkv"), _mm_tn(n2, dz, "dw_gdn_in_z"), _mm_tn(n2, dab, "dw_gdn_in_ab")[:, :2 * n_heads]], axis=1)
    tie = emit("gdn_in", {"gdn_w_in": dw_in})
    dh2, dg_mix1 = _mm_nt([(dqkv2, (w_in, 0)), (dz, (w_in, 3)), (dab, w_ab)], "dx_gdn_in",
                          _rms_bwd_tail(h2, _row(p["norm_mix_g"][1]) + tie, dh3))

    df0, dh1, dg_ffn0, cs_h1 = _mlp_bwd(dh2, h1, _row(p["norm_ffn_g"][0]), f0, m0["w1"], m0["w2"], "mlp_bwd0")
    dw2_0 = _mm_tn(r0, dh2, "dw_mlp2_0", bf16)
    dw1_0 = _mm_tn_blocked(n1, df0, "dw_mlp1_0", bf16)
    dw_pw2 = _mm_tn(sb, dh1, "dw_pw2", bf16)
    tie = emit("mlp0", {"mlp_w2_0": dw2_0, "mlp_w1_0": dw1_0, "cv_w_pw2": dw_pw2})
    gr["norm_ffn_g"] = jnp.stack([dg_ffn0[0], dg_ffn1[0]])

    gr["cv_b_pw2"] = cs_h1[0]
    ddc, dlng, dlnb, cs_dc = _mm_nt([(dh1, cv["cv_w_pw2"])], "dx_pw2",
                                    _ln_silu_bwd_tail(dc, _row(p["cv_ln_g"]) + tie, _row(p["cv_ln_b"])))
    gr["cv_ln_g"] = dlng[0]
    gr["cv_ln_b"] = dlnb[0]
    gr["cv_b_dw"] = cs_dc[0]
    dgl, dw_dw, _ = _dwconv_bwd(gl.reshape(bl, s, d), [ddc.reshape(bl, s, d)], cv["cv_w_dw"], "dwconv_bwd")
    gr["cv_w_dw"] = dw_dw[:cv["cv_w_dw"].shape[0]]
    dub, cs_u = _glu_bwd(dgl.reshape(t, d), ub)
    gr["cv_b_pw1"] = cs_u[0]
    dw_pw1 = _mm_tn_blocked(n0, dub, "dw_pw1", bf16)
    tie = emit("last", {"cv_w_pw1": dw_pw1, "small": gr})
    dx, dg_mix0 = _mm_nt([(dub, ("blocks", p["cv_w_pw1"]))], "dx_pw1",
                         _rms_bwd_tail(x2, _row(p["norm_mix_g"][0]) + tie, dh1))
    return loss, dx.reshape(bl, s, d), {"norm_mix_g": jnp.stack([dg_mix0[0], dg_mix1[0]])}


ANY = pl.BlockSpec(memory_space=pl.ANY)
MESH = pl.DeviceIdType.MESH


def _flip(v, bit):
    return 1 - v if bit else v


def _all_gather_many(shards):
    na = len(shards)

    def body(*refs):
        x_refs, o_refs = refs[:na], refs[na:2 * na]
        send_sems, recv_sems, local_sems = refs[2 * na:]
        x, y, c = lax.axis_index("x"), lax.axis_index("y"), lax.axis_index("c")
        me, sibling = (x, y, c), (x, y, 1 - c)
        chips = [(1 - x, y), (x, 1 - y), (1 - x, 1 - y)]

        def copy(a, k, block, to, src=None):
            px, py, pc = block
            dst = o_refs[a].at[4 * px + 2 * py + pc]
            return pltpu.make_async_remote_copy(
                src_ref=dst if src is None else src, dst_ref=dst,
                send_sem=send_sems.at[a, k], recv_sem=recv_sems.at[a, k], device_id=to, device_id_type=MESH)

        mine = [pltpu.make_async_copy(x_refs[a], o_refs[a].at[4 * x + 2 * y + c], local_sems.at[a]) for a in range(na)]
        first = []
        for a in range(na):
            first.append(copy(a, 0, me, sibling, src=x_refs[a]))
            first += [copy(a, 1 + j, me, (*chip, c), src=x_refs[a]) for j, chip in enumerate(chips)]
        for cp in mine + first:
            cp.start()
        passed = []
        for j, chip in enumerate(chips):
            for a in range(na):
                copy(a, 1 + j, (*chip, c), me).wait_recv()
                fwd = copy(a, 4 + j, (*chip, c), sibling)
                fwd.start()
                passed.append(fwd)
        for a in range(na):
            copy(a, 0, sibling, me).wait_recv()
        for j, chip in enumerate(chips):
            for a in range(na):
                copy(a, 4 + j, (*chip, 1 - c), me).wait_recv()
        for cp in first + passed:
            cp.wait_send()
        for cp in mine:
            cp.wait()

    return pl.pallas_call(
        body, name="weights_all_gather",
        out_shape=[jax.ShapeDtypeStruct((N_DEV,) + a.shape, a.dtype) for a in shards],
        in_specs=[ANY] * na, out_specs=[ANY] * na,
        scratch_shapes=[pltpu.SemaphoreType.DMA((na, 7)), pltpu.SemaphoreType.DMA((na, 7)), pltpu.SemaphoreType.DMA((na,))],
        compiler_params=pltpu.CompilerParams(has_side_effects=True),
    )(*shards)


HBM = pl.BlockSpec(memory_space=pltpu.HBM)
SEM = pl.BlockSpec(memory_space=pltpu.SEMAPHORE)
EFFECT = pltpu.SideEffectType.DATAFLOW_SIDE_EFFECTING
N_PEERS = N_DEV - 1


def _exchange_copies(src_refs, land_refs, send_sems, recv_sems, scatter):
    x, y, c = lax.axis_index("x"), lax.axis_index("y"), lax.axis_index("c")
    me = 4 * x + 2 * y + c
    copies = []
    for a, (src, land) in enumerate(zip(src_refs, land_refs)):
        for k in range(1, N_DEV):
            px, py, pc = _flip(x, k & 4), _flip(y, k & 2), _flip(c, k & 1)
            i = a * N_PEERS + k - 1
            copies.append(pltpu.make_async_remote_copy(
                src_ref=src.at[4 * px + 2 * py + pc] if scatter[a] else src, dst_ref=land.at[me],
                send_sem=send_sems.at[i], recv_sem=recv_sems.at[i], device_id=(px, py, pc), device_id_type=MESH))
    return copies


def _exchange_start(srcs, scatter, name):
    na = len(srcs)
    lands = [lax.empty(s.shape if sc else (N_DEV,) + s.shape, s.dtype) for s, sc in zip(srcs, scatter)]

    def body(*refs):
        copies = _exchange_copies(refs[:na], refs[na:2 * na], refs[2 * na], refs[2 * na + 1], scatter)
        for cp in copies:
            cp.start()
        token = refs[-1]
        token[...] = jnp.zeros_like(token)

    outs = pl.pallas_call(
        body, name=name,
        out_shape=(pltpu.SemaphoreType.DMA((na * N_PEERS,)), pltpu.SemaphoreType.DMA((na * N_PEERS,)))
        + tuple(pltpu.HBM(a.shape, a.dtype) for a in srcs + lands) + (jax.ShapeDtypeStruct((SUBLANES, LANES), f32),),
        in_specs=[HBM] * (2 * na),
        out_specs=(SEM, SEM) + (HBM,) * (2 * na) + (pl.BlockSpec(memory_space=pltpu.VMEM),),
        input_output_aliases={i: 2 + i for i in range(2 * na)},
        compiler_params=pltpu.CompilerParams(has_side_effects=EFFECT),
    )(*[pltpu.with_memory_space_constraint(a, pltpu.HBM) for a in srcs + lands])
    return outs[0], outs[1], list(outs[2:2 + na]), list(outs[2 + na:2 + 2 * na]), outs[-1]


def _exchange_wait(started, after, scatter, name):
    send_sems, recv_sems, srcs, lands, _ = started
    na = len(srcs)

    def body(*refs):
        for cp in _exchange_copies(refs[:na], refs[na:2 * na], refs[2 * na], refs[2 * na + 1], scatter):
            cp.wait_send()
            cp.wait_recv()

    outs = pl.pallas_call(
        body, name=name,
        out_shape=tuple(pltpu.HBM(a.shape, a.dtype) for a in srcs + lands),
        in_specs=[HBM] * (2 * na) + [SEM, SEM, ANY], out_specs=(HBM,) * (2 * na),
        input_output_aliases={i: i for i in range(2 * na)},
        compiler_params=pltpu.CompilerParams(has_side_effects=EFFECT),
    )(*srcs, *lands, send_sems, recv_sems, after)
    return list(outs[:na]), list(outs[na:])


def _own_block(land, block, me):
    return lax.dynamic_update_index_in_dim(land, block, me, 0)


def _sum8_adamw(r2, w, m, v, layer, name):
    _, rr, cc = r2.shape
    tr = _tile(rr, 256)
    bc1 = 1.0 - ADAM_B1 ** ADAM_STEP
    bc2 = 1.0 - ADAM_B2 ** ADAM_STEP

    def body(r_ref, w_ref, m_ref, v_ref, g_ref, d_ref, nm_ref, nv_ref):
        gv = r_ref[0].astype(f32)
        for q in range(1, N_DEV):
            gv = gv + r_ref[q].astype(f32)
        g_ref[...] = gv
        nm = ADAM_B1 * m_ref[...] + (1.0 - ADAM_B1) * gv
        nv = ADAM_B2 * v_ref[...] + (1.0 - ADAM_B2) * (gv * gv)
        nm_ref[...] = nm
        nv_ref[...] = nv
        d_ref[...] = -ADAM_LR * ((nm / bc1) / (jnp.sqrt(nv / bc2) + ADAM_EPS) + ADAM_WD * w_ref[...])

    lspec = pl.BlockSpec((None, tr, cc), lambda i: (layer, i, 0))
    return _pc(body, name=name, grid=(rr // tr,),
               in_specs=[pl.BlockSpec((N_DEV, tr, cc), lambda i: (0, i, 0)), lspec, lspec, lspec],
               out_specs=[_rows(tr, cc)] * 4, out_shape=[jax.ShapeDtypeStruct((rr, cc), f32)] * 4,
               sem=("parallel",))(r2, w, m, v)


def _sum_devices(recv, name):
    _, rr, _ = recv.shape
    tr = _tile(rr, 512)

    def body(r_ref, o_ref):
        acc = r_ref[0]
        for i in range(1, N_DEV):
            acc = acc + r_ref[i]
        o_ref[...] = acc

    return _pc(body, name=name, grid=(rr // tr,),
               in_specs=[pl.BlockSpec((N_DEV, tr, LANES), lambda i: (0, i, 0))],
               out_specs=_rows(tr, LANES), out_shape=jax.ShapeDtypeStruct((rr, LANES), f32), sem=("parallel",))(recv)


def _adamw(w, g, m, v):
    rr = w.shape[0]
    tr = _tile(rr, 512)
    bc1 = 1.0 - ADAM_B1 ** ADAM_STEP
    bc2 = 1.0 - ADAM_B2 ** ADAM_STEP

    def body(w_ref, g_ref, m_ref, v_ref, d_ref, nm_ref, nv_ref):
        gv = g_ref[...]
        nm = ADAM_B1 * m_ref[...] + (1.0 - ADAM_B1) * gv
        nv = ADAM_B2 * v_ref[...] + (1.0 - ADAM_B2) * (gv * gv)
        nm_ref[...] = nm
        nv_ref[...] = nv
        d_ref[...] = -ADAM_LR * ((nm / bc1) / (jnp.sqrt(nv / bc2) + ADAM_EPS) + ADAM_WD * w_ref[...])

    spec = _rows(tr, LANES)
    return _pc(body, name="adamw", grid=(rr // tr,), in_specs=[spec] * 4, out_specs=[spec] * 3,
               out_shape=[jax.ShapeDtypeStruct((rr, LANES), f32)] * 3, sem=("parallel",))(w, g, m, v)


PACK_ROWS = 512
PART_ROWS = SUBLANES


def _pack(arrs):
    parts, sizes = [], []
    for a in arrs:
        flat = a.reshape(-1)
        n = flat.shape[0]
        rows = -(-n // (LANES * PART_ROWS)) * PART_ROWS
        if rows * LANES != n:
            flat = jnp.pad(flat, (0, rows * LANES - n))
        parts.append(flat.reshape(rows, LANES))
        sizes.append((rows, n))
    total = sum(r for r, _ in sizes)
    padded = -(-total // PACK_ROWS) * PACK_ROWS
    if padded > total:
        parts.append(jnp.zeros((padded - total, LANES), parts[0].dtype))
    return jnp.concatenate(parts, axis=0), sizes


def _unpack(packed, sizes, shapes):
    out, off = [], 0
    for (rows, n), shp in zip(sizes, shapes):
        piece = lax.slice_in_dim(packed, off, off + rows, axis=0)
        if rows * LANES != n:
            piece = lax.slice_in_dim(piece.reshape(-1), 0, n, axis=0)
        out.append(piece.reshape(tuple(shp)))
        off += rows
    return out


def _cols_to_blocks(a):
    n = a.shape[-1] // N_DEV
    a = a.reshape(a.shape[:-1] + (N_DEV, n))
    return jnp.moveaxis(a, -2, 0)


def _blocks_to_cols(a):
    a = jnp.moveaxis(a, 0, -2)
    return a.reshape(a.shape[:-2] + (a.shape[-2] * a.shape[-1],))


def _rows_to_blocks(a):
    k = a.shape[-2] // N_DEV
    a = a.reshape(a.shape[:-2] + (N_DEV, k, a.shape[-1]))
    return jnp.moveaxis(a, -3, 0)


def _blocks_to_rows(a):
    a = jnp.moveaxis(a, 0, -3)
    return a.reshape(a.shape[:-3] + (a.shape[-3] * a.shape[-2], a.shape[-1]))


COL_SHARDED = ("cv_w_pw1", "gdn_w_in", "mlp_w1")
ROW_SHARDED = ("cv_w_pw2", "gdn_w_out", "mlp_w2")
CONV_SHARDED = ("cv_w_dw", "gdn_conv_w")
REPLICATED = ("norm_mix_g", "norm_ffn_g", "final_norm_g", "cv_b_pw1", "cv_b_dw", "cv_ln_g", "cv_ln_b", "cv_b_pw2",
              "gdn_a_log", "gdn_dt_bias", "gdn_norm_g")
WEIGHTS = ("norm_mix_g", "norm_ffn_g", "final_norm_g", "cv_w_pw1", "cv_b_pw1", "cv_w_dw", "cv_b_dw", "cv_ln_g",
           "cv_ln_b", "cv_w_pw2", "cv_b_pw2", "gdn_w_in", "gdn_conv_w", "gdn_a_log", "gdn_dt_bias", "gdn_norm_g",
           "gdn_w_out", "mlp_w1", "mlp_w2")
MATMUL_SHARDED = COL_SHARDED + ROW_SHARDED


def _squeeze_layer(name, a):
    if name in ("norm_mix_g", "norm_ffn_g", "final_norm_g", "mlp_w1", "mlp_w2"):
        return a
    return a[0]


def _gather_weights(shards):
    me = 4 * lax.axis_index("x") + 2 * lax.axis_index("y") + lax.axis_index("c")
    pw1 = _all_gather_many([shards["cv_w_pw1"].astype(bf16)])[0]
    now = {"cv_w_pw1": pw1}

    def cast(a, tie):
        return (a + tie).astype(bf16)

    token, _ = lax.optimization_barrier((jnp.zeros((), f32), pw1))
    later, started = {}, {}
    for group in ("conv", "mlp0", "gdn", "mlp1"):
        if group == "conv":
            srcs = [cast(shards["cv_w_pw2"], token)] + [shards[n] + token for n in CONV_SHARDED]
        elif group == "gdn":
            srcs = [cast(shards["gdn_w_in"], token).reshape(-1, LANES), cast(shards["gdn_w_out"], token)]
        else:
            layer = int(group[-1])
            srcs = [cast(shards["mlp_w1"][layer], token), cast(shards["mlp_w2"][layer], token)]
        later[group] = srcs
        started[group] = _exchange_start(srcs, [False] * len(srcs), f"weights_{group}_start")
        token = started[group][4][0, 0]

    def need(group, after):
        srcs, lands = _exchange_wait(started[group], after, [False] * len(later[group]), f"weights_{group}_wait")
        lands = [_own_block(ld, own, me) for ld, own in zip(lands, srcs)]
        if group == "conv":
            out = {"cv_w_pw2": _blocks_to_rows(lands[0])}
            out.update({n: _blocks_to_cols(ld) for n, ld in zip(CONV_SHARDED, lands[1:])})
            return out
        if group == "gdn":
            w_in = _blocks_to_cols(lands[0].reshape((N_DEV,) + shards["gdn_w_in"].shape))
            return {"w_in": w_in, "w_out": _blocks_to_rows(lands[1])}
        return {"w1": lands[0], "w2": lands[1]}

    return now, need, token


def kernel(x, norm_mix_g, norm_ffn_g, final_norm_g, cv_w_pw1, cv_b_pw1, cv_w_dw, cv_b_dw, cv_ln_g, cv_ln_b, cv_w_pw2, cv_b_pw2, gdn_w_in, gdn_conv_w, gdn_a_log, gdn_dt_bias, gdn_norm_g, gdn_w_out, mlp_w1, mlp_w2, loss_target, m_norm_mix_g, m_norm_ffn_g, m_final_norm_g, m_cv_w_pw1, m_cv_b_pw1, m_cv_w_dw, m_cv_b_dw, m_cv_ln_g, m_cv_ln_b, m_cv_w_pw2, m_cv_b_pw2, m_gdn_w_in, m_gdn_conv_w, m_gdn_a_log, m_gdn_dt_bias, m_gdn_norm_g, m_gdn_w_out, m_mlp_w1, m_mlp_w2, v_norm_mix_g, v_norm_ffn_g, v_final_norm_g, v_cv_w_pw1, v_cv_b_pw1, v_cv_w_dw, v_cv_b_dw, v_cv_ln_g, v_cv_ln_b, v_cv_w_pw2, v_cv_b_pw2, v_gdn_w_in, v_gdn_conv_w, v_gdn_a_log, v_gdn_dt_bias, v_gdn_norm_g, v_gdn_w_out, v_mlp_w1, v_mlp_w2):
    w_in = dict(zip(WEIGHTS, (norm_mix_g, norm_ffn_g, final_norm_g, cv_w_pw1, cv_b_pw1, cv_w_dw, cv_b_dw, cv_ln_g, cv_ln_b, cv_w_pw2, cv_b_pw2, gdn_w_in, gdn_conv_w, gdn_a_log, gdn_dt_bias, gdn_norm_g, gdn_w_out, mlp_w1, mlp_w2)))
    m_in = dict(zip(WEIGHTS, (m_norm_mix_g, m_norm_ffn_g, m_final_norm_g, m_cv_w_pw1, m_cv_b_pw1, m_cv_w_dw, m_cv_b_dw, m_cv_ln_g, m_cv_ln_b, m_cv_w_pw2, m_cv_b_pw2, m_gdn_w_in, m_gdn_conv_w, m_gdn_a_log, m_gdn_dt_bias, m_gdn_norm_g, m_gdn_w_out, m_mlp_w1, m_mlp_w2)))
    v_in = dict(zip(WEIGHTS, (v_norm_mix_g, v_norm_ffn_g, v_final_norm_g, v_cv_w_pw1, v_cv_b_pw1, v_cv_w_dw, v_cv_b_dw, v_cv_ln_g, v_cv_ln_b, v_cv_w_pw2, v_cv_b_pw2, v_gdn_w_in, v_gdn_conv_w, v_gdn_a_log, v_gdn_dt_bias, v_gdn_norm_g, v_gdn_w_out, v_mlp_w1, v_mlp_w2)))
    me = 4 * lax.axis_index("x") + 2 * lax.axis_index("y") + lax.axis_index("c")

    shards = {n: _squeeze_layer(n, w_in[n]) for n in WEIGHTS}
    first, need, token = _gather_weights(shards)
    params = {n: shards[n] for n in REPLICATED}
    params.update(first)
    params["norm_mix_g"] = params["norm_mix_g"] + token

    def row_blocks(a):
        return a.reshape(N_DEV, a.shape[0] // N_DEV, a.shape[1])

    def flat_blocks(a):
        k, n8 = a.shape
        return _cols_to_blocks(a).reshape(N_DEV, k * (n8 // N_DEV) // LANES, LANES)

    def as_blocks(n, a):
        if n == "gdn_w_in":
            return flat_blocks(a).astype(bf16)
        return a if a.ndim == 3 else row_blocks(a)

    sent = []

    small = REPLICATED + CONV_SHARDED
    small_early = tuple(n for n in small if n != "norm_mix_g") + ("loss",)
    small_info = {}

    def emit(group, grads_out):
        names, blocks, scatter = list(grads_out), [], []
        for n in names:
            if n == "small":
                packed, small_info["sizes"] = _pack([grads_out[n][k] for k in small_early])
                small_info["shapes"] = [grads_out[n][k].shape for k in small_early]
                blocks.append(packed)
                scatter.append(False)
            else:
                blocks.append(as_blocks(n, grads_out[n]))
                scatter.append(True)
        sent.append((names, _exchange_start(blocks, scatter, f"grads_{group}_start"), scatter))
        return sent[-1][1][4][0, 0]

    _, grad_x, gr = _local_step(x, loss_target, params, need, emit)
    nm_shape = gr["norm_mix_g"].shape
    sent.append((["norm_mix_g"], _exchange_start([gr["norm_mix_g"].reshape(-1, LANES)], [False], "grads_norm_mix_start"), [False]))

    recv = {}

    def finish(entry, after):
        names, st, scatter = entry
        srcs, lands = _exchange_wait(st, after, scatter, f"grads_{names[0]}_wait")
        for n, ld, blk, sc in zip(names, lands, srcs, scatter):
            own = lax.dynamic_index_in_dim(blk, me, 0, keepdims=False) if sc else blk
            recv[n] = _own_block(ld, own, me)
        return lands[0]

    def as3d(n, a):
        if n == "gdn_w_in":
            return a.reshape(a.shape[0], -1, LANES)
        return a

    big = [("cv_w_pw2", 0, "cv_w_pw2"), ("gdn_w_in", 0, "gdn_w_in"), ("gdn_w_out", 0, "gdn_w_out"),
           ("mlp_w1", 0, "mlp_w1_0"), ("mlp_w1", 1, "mlp_w1_1"), ("mlp_w2", 0, "mlp_w2_0"), ("mlp_w2", 1, "mlp_w2_1"),
           ("cv_w_pw1", 0, "cv_w_pw1")]
    res = {n: {} for n in MATMUL_SHARDED}
    after = grad_x
    for entry in sent[:-2]:
        after = finish(entry, after)
    for n, layer, key in big[:-1]:
        res[n][layer] = _sum8_adamw(recv[key], as3d(n, w_in[n]), as3d(n, m_in[n]), as3d(n, v_in[n]), layer, f"adamw_{key}")
        after = res[n][layer][0]
    after = finish(sent[-2], after)
    finish(sent[-1], after)
    n, layer, key = big[-1]
    res[n][layer] = _sum8_adamw(recv[key], as3d(n, w_in[n]), as3d(n, m_in[n]), as3d(n, v_in[n]), layer, f"adamw_{key}")

    grads = dict(zip(small_early, _unpack(_sum_devices(recv["small"], "grads_small_sum"), small_info["sizes"], small_info["shapes"])))
    grads["norm_mix_g"] = _sum_devices(recv["norm_mix_g"], "grads_norm_mix_sum").reshape(nm_shape)
    loss = grads["loss"].reshape(())
    for n in CONV_SHARDED:
        cn = shards[n].shape[-1]
        grads[n] = lax.dynamic_slice_in_dim(grads[n], me * cn, cn, axis=1)

    out_groups = {n: [] for n in WEIGHTS}
    for n in MATMUL_SHARDED:
        layers = sorted(res[n])
        for k in range(4):
            pieces = [res[n][layer][k] for layer in layers]
            out_groups[n].append(jnp.stack(pieces).reshape(w_in[n].shape))

    sm_w = [shards[n] for n in small]
    sm_g = [grads[n].reshape(shards[n].shape) for n in small]
    sm_m = [_squeeze_layer(n, m_in[n]) for n in small]
    sm_v = [_squeeze_layer(n, v_in[n]) for n in small]
    wp, psz = _pack(sm_w)
    gp, _ = _pack(sm_g)
    mp, _ = _pack(sm_m)
    vp, _ = _pack(sm_v)
    dp, nmp, nvp = _adamw(wp, gp, mp, vp)
    shp = [a.shape for a in sm_w]
    for n, g, dl, nm, nv in zip(small, sm_g, _unpack(dp, psz, shp), _unpack(nmp, psz, shp), _unpack(nvp, psz, shp)):
        out_groups[n] = [a.reshape(w_in[n].shape) for a in (g, dl, nm, nv)]

    outs = [loss, grad_x]
    for k in range(4):
        outs += [out_groups[n][k] for n in WEIGHTS]
    return tuple(outs)
```

```python
import jax
import jax.numpy as jnp
from jax import lax
from jax.experimental import pallas as pl
from jax.experimental.pallas import tpu as pltpu

f32, bf16 = jnp.float32, jnp.bfloat16

NORM_EPS = 1e-6
L2_EPS = 1e-6
CHUNK = 64
LANES = 128
SUBLANES = 8
N_DEV = 8
VMEM_LIMIT = 56 * 1024 * 1024
ROW_TILE = 512
CONV_PAD = 32
HEADS_PER_STEP = 8
PREP_HEADS_PER_STEP = 8
NEG = -1e30

ADAM_LR, ADAM_B1, ADAM_B2, ADAM_EPS, ADAM_WD, ADAM_STEP = 0.001, 0.9, 0.999, 1e-08, 0.01, 10

NT = (((1,), (1,)), ((), ()))
TN = (((0,), (0,)), ((), ()))
HI = lax.Precision.HIGHEST


def _pc(body, *, name, grid, in_specs, out_specs, out_shape, scratch=(), sem=None):
    return pl.pallas_call(
        body, name=name, grid=grid, in_specs=in_specs, out_specs=out_specs, out_shape=out_shape,
        scratch_shapes=list(scratch),
        compiler_params=pltpu.CompilerParams(dimension_semantics=sem, vmem_limit_bytes=VMEM_LIMIT))


def _rows(tm, n):
    return pl.BlockSpec((tm, n), lambda i: (i, 0))


def _const(shape):
    return pl.BlockSpec(shape, lambda *_: (0,) * len(shape))


def _resident(shape):
    return pl.BlockSpec(shape, lambda *_: (0,) * len(shape), pipeline_mode=pl.Buffered(1))


def _tile(t, pref):
    return pref if t % pref == 0 else t


def _dot(a, b):
    return jnp.dot(a.astype(bf16), b.astype(bf16), preferred_element_type=f32)


def _dot_nt(a, b):
    return lax.dot_general(a.astype(bf16), b.astype(bf16), NT, preferred_element_type=f32)


def _dot_tn(a, b):
    return lax.dot_general(a.astype(bf16), b.astype(bf16), TN, preferred_element_type=f32)


def _sigmoid(x):
    return 1.0 / (1.0 + jnp.exp(-x))


def _silu_grad(x):
    s = _sigmoid(x)
    return s * (1.0 + x * (1.0 - s))


def _rms(x, g):
    rstd = lax.rsqrt(jnp.mean(x * x, axis=-1, keepdims=True) + NORM_EPS)
    xh = x * rstd
    return xh * g, xh, rstd


def _rms_bwd(dn, xh, rstd, g):
    dxh = dn * g
    return rstd * (dxh - xh * jnp.mean(dxh * xh, axis=-1, keepdims=True))


def _acc_init(step, *refs):
    @pl.when(step == 0)
    def _():
        for r in refs:
            r[...] = jnp.zeros(r.shape, r.dtype)


def _acc_rows(ref, val):
    ref[0:1, :] += jnp.sum(val, axis=0, keepdims=True)


def _pw1_glu(x, g, w, b):
    t, d = x.shape
    tm = _tile(t, ROW_TILE)

    nb_w = w.shape[0]

    def body(x_ref, g_ref, w_hbm, b_ref, n_ref, u_ref, gl_ref, w_ref, sems):
        _fetch_blocks(pl.program_id(0), w_hbm, w_ref, sems, True)
        n, _, _ = _rms(x_ref[...], g_ref[...])
        nb = n.astype(bf16)
        n_ref[...] = nb
        u = jnp.dot(nb, w_ref[...], preferred_element_type=f32) + b_ref[...]
        u_ref[...] = u.astype(bf16)
        gl_ref[...] = u[:, :d] * _sigmoid(u[:, d:])

    return _pc(body, name="pw1_glu", grid=(t // tm,),
               in_specs=[_rows(tm, d), _const((1, d)), ANY, _const((1, 2 * d))],
               out_specs=[_rows(tm, d), _rows(tm, 2 * d), _rows(tm, d)],
               out_shape=[jax.ShapeDtypeStruct((t, d), bf16), jax.ShapeDtypeStruct((t, 2 * d), bf16),
                          jax.ShapeDtypeStruct((t, d), f32)],
               scratch=[pltpu.VMEM((d, 2 * d), bf16), pltpu.SemaphoreType.DMA((nb_w,))],
               sem=("arbitrary",))(x, g, w, b)


def _ln_silu_mm_res(dc, ln_g, ln_b, w, b, res):
    t, d = dc.shape
    tm = _tile(t, ROW_TILE)

    def body(x_ref, g_ref, bb_ref, w_ref, b_ref, r_ref, s_ref, o_ref):
        x = x_ref[...]
        xc = x - jnp.mean(x, axis=-1, keepdims=True)
        rstd = lax.rsqrt(jnp.mean(xc * xc, axis=-1, keepdims=True) + NORM_EPS)
        ln = xc * rstd * g_ref[...] + bb_ref[...]
        sb = (ln * _sigmoid(ln)).astype(bf16)
        s_ref[...] = sb
        o_ref[...] = r_ref[...] + jnp.dot(sb, w_ref[...], preferred_element_type=f32) + b_ref[...]

    return _pc(body, name="ln_silu_pw2", grid=(t // tm,),
               in_specs=[_rows(tm, d), _const((1, d)), _const((1, d)), _resident((d, d)), _const((1, d)), _rows(tm, d)],
               out_specs=[_rows(tm, d), _rows(tm, d)],
               out_shape=[jax.ShapeDtypeStruct((t, d), bf16), jax.ShapeDtypeStruct((t, d), f32)],
               sem=("parallel",))(dc, ln_g, ln_b, w, b, res)


def _fetch_blocks(step, w_hbm, dst, sems, by_cols, layer=None):
    nb_w = w_hbm.shape[0]
    step_rows, step_cols = w_hbm.shape[-2], w_hbm.shape[-1]

    @pl.when(step == 0)
    def _():
        copies = []
        for j in range(nb_w):
            src = w_hbm.at[j] if layer is None else w_hbm.at[j, layer]
            if by_cols:
                part = dst.at[:, pl.ds(j * step_cols, step_cols)]
            else:
                part = dst.at[pl.ds(j * step_rows, step_rows), :]
            copies.append(pltpu.make_async_copy(src, part, sems.at[j]))
        for cp in copies:
            cp.start()
        for cp in copies:
            cp.wait()


def _mlp_fwd(h, g, w1g, w2g, name, loss=None):
    t, d = h.shape
    nb_w, _, bn = w1g.shape
    ff = nb_w * bn
    tm = _tile(t, 256)
    n_in = 4 if loss is None else 6
    n_out = 4 if loss is None else 6

    def body(*refs):
        h_ref, g_ref, w1_hbm, w2_hbm = refs[:4]
        n_ref, f_ref, r_ref = refs[n_in:n_in + 3]
        w1_ref, w2_ref, sem1, sem2 = refs[n_in + n_out:]
        _fetch_blocks(pl.program_id(0), w1_hbm, w1_ref, sem1, True)
        _fetch_blocks(pl.program_id(0), w2_hbm, w2_ref, sem2, False)
        hv = h_ref[...]
        n, _, _ = _rms(hv, g_ref[...])
        nb = n.astype(bf16)
        n_ref[...] = nb
        f = jnp.dot(nb, w1_ref[...], preferred_element_type=f32)
        f_ref[...] = f.astype(bf16)
        rb = jnp.square(jnp.maximum(f, 0.0)).astype(bf16)
        r_ref[...] = rb
        out = hv + jnp.dot(rb, w2_ref[...], preferred_element_type=f32)
        if loss is None:
            refs[n_in + 3][...] = out
        else:
            gf_ref, t_ref = refs[4:6]
            loss_ref, dh_ref, dg_ref = refs[n_in + 3:n_in + 6]
            _acc_init(pl.program_id(0), loss_ref, dg_ref)
            gv = gf_ref[...]
            y, xh, rstd = _rms(out, gv)
            e = y - t_ref[...]
            loss_ref[...] += 0.5 * jnp.sum(jnp.mean(e * e, axis=-1, keepdims=True))
            dy = e * (1.0 / d)
            _acc_rows(dg_ref, dy * xh)
            dh_ref[...] = _rms_bwd(dy, xh, rstd, gv)

    in_specs = [_rows(tm, d), _const((1, d)), ANY, ANY]
    out_specs = [_rows(tm, d), _rows(tm, ff), _rows(tm, ff)]
    out_shape = [jax.ShapeDtypeStruct((t, d), bf16), jax.ShapeDtypeStruct((t, ff), bf16), jax.ShapeDtypeStruct((t, ff), bf16)]
    args = [h, g, w1g, w2g]
    if loss is None:
        out_specs.append(_rows(tm, d))
        out_shape.append(jax.ShapeDtypeStruct((t, d), f32))
    else:
        in_specs += [_const((1, d)), _rows(tm, d)]
        args += list(loss)
        out_specs += [_const((SUBLANES, LANES)), _rows(tm, d), _const((SUBLANES, d))]
        out_shape += [jax.ShapeDtypeStruct((SUBLANES, LANES), f32), jax.ShapeDtypeStruct((t, d), f32),
                      jax.ShapeDtypeStruct((SUBLANES, d), f32)]
    return _pc(body, name=name, grid=(t // tm,), in_specs=in_specs, out_specs=out_specs, out_shape=out_shape,
               scratch=[pltpu.VMEM((d, ff), bf16), pltpu.VMEM((ff, d), bf16),
                        pltpu.SemaphoreType.DMA((nb_w,)), pltpu.SemaphoreType.DMA((nb_w,))],
               sem=("arbitrary",))(*args)


def _softplus(x):
    return jnp.maximum(x, 0.0) + jnp.log(1.0 + jnp.exp(-jnp.abs(x)))


def _gdn_in(h, g, w_main, w_ab, a_log_pad, dt_pad, n_heads):
    t, d = h.shape
    tm = _tile(t, ROW_TILE)
    gate_rows = 2 * n_heads

    def body(h_ref, g_ref, wm_ref, wab_ref, al_ref, dt_ref, n_ref, qkv_ref, z_ref, ab_ref, gb_ref):
        n, _, _ = _rms(h_ref[...], g_ref[...])
        nb = n.astype(bf16)
        n_ref[...] = nb
        p = jnp.dot(nb, wm_ref[...], preferred_element_type=f32)
        qkv_ref[...] = p[:, :3 * d]
        z_ref[...] = p[:, 3 * d:]
        ab = jnp.dot(nb, wab_ref[...], preferred_element_type=f32)
        ab_ref[...] = ab
        lane = lax.broadcasted_iota(jnp.int32, ab.shape, 1)
        decay = -jnp.exp(al_ref[...]) * _softplus(ab + dt_ref[...])
        gates = jnp.where(lane < n_heads, decay, jnp.where(lane < gate_rows, _sigmoid(ab), 0.0))
        gb_ref[...] = gates.T[:gate_rows, :]

    return _pc(body, name="gdn_in", grid=(t // tm,),
               in_specs=[_rows(tm, d), _const((1, d)), _resident((d, 4 * d)), _resident((d, LANES)),
                         _const((1, LANES)), _const((1, LANES))],
               out_specs=[_rows(tm, d), _rows(tm, 3 * d), _rows(tm, d), _rows(tm, LANES),
                          pl.BlockSpec((gate_rows, tm), lambda i: (0, i))],
               out_shape=[jax.ShapeDtypeStruct((t, d), bf16), jax.ShapeDtypeStruct((t, 3 * d), f32),
                          jax.ShapeDtypeStruct((t, d), f32), jax.ShapeDtypeStruct((t, LANES), f32),
                          jax.ShapeDtypeStruct((gate_rows, t), f32)],
               sem=("parallel",))(h, g, w_main, w_ab, a_log_pad, dt_pad)


def _gated_norm_mm_res(o, z, ng, w, res, n_heads):
    t, d = o.shape
    tm = _tile(t, ROW_TILE)

    def body(o_ref, z_ref, ng_ref, w_ref, r_ref, on_ref, out_ref):
        for hd in range(n_heads):
            sl = slice(hd * LANES, (hd + 1) * LANES)
            rn, _, _ = _rms(o_ref[:, sl], ng_ref[...])
            zz = z_ref[:, sl]
            on_ref[:, sl] = (rn * (zz * _sigmoid(zz))).astype(bf16)
        out_ref[...] = r_ref[...] + jnp.dot(on_ref[...], w_ref[...], preferred_element_type=f32)

    return _pc(body, name="gated_norm_wout", grid=(t // tm,),
               in_specs=[_rows(tm, d), _rows(tm, d), _const((1, LANES)), _resident((d, d)), _rows(tm, d)],
               out_specs=[_rows(tm, d), _rows(tm, d)],
               out_shape=[jax.ShapeDtypeStruct((t, d), bf16), jax.ShapeDtypeStruct((t, d), f32)],
               sem=("parallel",))(o, z, ng, w, res)


def _mlp_bwd(dho, h, g, fb, w1g, w2g, name):
    t, d = h.shape
    nb_w, _, bn = w1g.shape
    ff = nb_w * bn
    tm = _tile(t, 256)

    def body(do_ref, h_ref, g_ref, f_ref, w1_hbm, w2_hbm, df_ref, dh_ref, dg_ref, cs_ref, w1_ref, w2_ref, sem1, sem2):
        _fetch_blocks(pl.program_id(0), w1_hbm, w1_ref, sem1, True)
        _fetch_blocks(pl.program_id(0), w2_hbm, w2_ref, sem2, False)
        _acc_init(pl.program_id(0), dg_ref, cs_ref)
        do = do_ref[...]
        dr = lax.dot_general(do.astype(bf16), w2_ref[...], NT, preferred_element_type=f32)
        dfb = (dr * (2.0 * jnp.maximum(f_ref[...].astype(f32), 0.0))).astype(bf16)
        df_ref[...] = dfb
        dn = lax.dot_general(dfb, w1_ref[...], NT, preferred_element_type=f32)
        gv = g_ref[...]
        _, xh, rstd = _rms(h_ref[...], gv)
        _acc_rows(dg_ref, dn * xh)
        dh = do + _rms_bwd(dn, xh, rstd, gv)
        dh_ref[...] = dh
        _acc_rows(cs_ref, dh)

    return _pc(body, name=name, grid=(t // tm,),
               in_specs=[_rows(tm, d), _rows(tm, d), _const((1, d)), _rows(tm, ff), ANY, ANY],
               out_specs=[_rows(tm, ff), _rows(tm, d), _const((SUBLANES, d)), _const((SUBLANES, d))],
               out_shape=[jax.ShapeDtypeStruct((t, ff), bf16), jax.ShapeDtypeStruct((t, d), f32),
                          jax.ShapeDtypeStruct((SUBLANES, d), f32), jax.ShapeDtypeStruct((SUBLANES, d), f32)],
               scratch=[pltpu.VMEM((d, ff), bf16), pltpu.VMEM((ff, d), bf16),
                        pltpu.SemaphoreType.DMA((nb_w,)), pltpu.SemaphoreType.DMA((nb_w,))],
               sem=("arbitrary",))(dho, h, g, fb, w1g, w2g)


class _Tail:
    def __init__(self, fn, ins, outs):
        self.fn, self.ins, self.outs = fn, ins, outs


def _mm_nt(pairs, name, tail=None):
    t = pairs[0][0].shape[0]
    tm = _tile(t, ROW_TILE)
    npair = len(pairs)
    in_specs, args, scratch, blocked = [], [], [], []
    k = None
    for dy, w in pairs:
        nn = dy.shape[1]
        if isinstance(w, tuple) and isinstance(w[0], str):
            w = w[1]
            k = w.shape[1]
            wspec = ANY
            blocked.append(True)
            scratch += [pltpu.VMEM((k, nn), bf16), pltpu.SemaphoreType.DMA((w.shape[0],))]
        elif isinstance(w, tuple):
            w, idx = w
            k = w.shape[0]
            wspec = pl.BlockSpec((k, nn), lambda *_, idx=idx: (0, idx), pipeline_mode=pl.Buffered(1))
            blocked.append(False)
        else:
            k = w.shape[0]
            wspec = _resident(w.shape)
            blocked.append(False)
        in_specs += [_rows(tm, nn), wspec]
        args += [dy, w]
    n_tin = len(tail.ins) if tail else 0
    n_out = len(tail.outs) if tail else 1
    if tail:
        for arr, kind in tail.ins:
            in_specs.append(_rows(tm, arr.shape[1]) if kind == "rows" else _const(arr.shape))
            args.append(arr)
        out_specs = [_rows(tm, c) if kind == "rows" else _const((SUBLANES, c)) for c, kind in tail.outs]
        out_shape = [jax.ShapeDtypeStruct((t, c) if kind == "rows" else (SUBLANES, c), f32) for c, kind in tail.outs]
    else:
        out_specs = _rows(tm, k)
        out_shape = jax.ShapeDtypeStruct((t, k), f32)

    def body(*refs):
        step = pl.program_id(0)
        tin = refs[2 * npair:2 * npair + n_tin]
        outs = refs[2 * npair + n_tin:2 * npair + n_tin + n_out]
        scr = list(refs[2 * npair + n_tin + n_out:])
        acc = None
        for p in range(npair):
            w_ref = refs[2 * p + 1]
            if blocked[p]:
                w_vmem, sems = scr.pop(0), scr.pop(0)
                _fetch_blocks(step, w_ref, w_vmem, sems, True)
                w_ref = w_vmem
            part = lax.dot_general(refs[2 * p][...].astype(bf16), w_ref[...], NT, preferred_element_type=f32)
            acc = part if acc is None else acc + part
        if tail is None:
            outs[0][...] = acc
        else:
            _acc_init(step, *[o for o, (_, kind) in zip(outs, tail.outs) if kind == "acc"])
            tail.fn(acc, tin, outs)

    sequential = tail is not None or any(blocked)
    return _pc(body, name=name, grid=(t // tm,), in_specs=in_specs, out_specs=out_specs, out_shape=out_shape,
               scratch=scratch, sem=("arbitrary",) if sequential else ("parallel",))(*args)


def _rms_bwd_tail(h, g, dres):
    def fn(dn, ins, outs):
        h_ref, g_ref, dr_ref = ins
        dh_ref, dg_ref = outs
        gv = g_ref[...]
        _, xh, rstd = _rms(h_ref[...], gv)
        _acc_rows(dg_ref, dn * xh)
        dh_ref[...] = dr_ref[...] + _rms_bwd(dn, xh, rstd, gv)

    d = h.shape[1]
    return _Tail(fn, [(h, "rows"), (g, "const"), (dres, "rows")], [(d, "rows"), (d, "acc")])


def _ln_silu_bwd_tail(dc, ln_g, ln_b):
    def fn(ds, ins, outs):
        x_ref, g_ref, b_ref = ins
        dx_ref, dg_ref, db_ref, cs_ref = outs
        x = x_ref[...]
        gv = g_ref[...]
        xc = x - jnp.mean(x, axis=-1, keepdims=True)
        rstd = lax.rsqrt(jnp.mean(xc * xc, axis=-1, keepdims=True) + NORM_EPS)
        xh = xc * rstd
        dln = ds * _silu_grad(xh * gv + b_ref[...])
        _acc_rows(dg_ref, dln * xh)
        _acc_rows(db_ref, dln)
        dxh = dln * gv
        dx = rstd * (dxh - jnp.mean(dxh, axis=-1, keepdims=True) - xh * jnp.mean(dxh * xh, axis=-1, keepdims=True))
        dx_ref[...] = dx
        _acc_rows(cs_ref, dx)

    d = dc.shape[1]
    return _Tail(fn, [(dc, "rows"), (ln_g, "const"), (ln_b, "const")], [(d, "rows"), (d, "acc"), (d, "acc"), (d, "acc")])


def _gated_norm_bwd_tail(o, z, ng, n_heads):
    def fn(don_all, ins, outs):
        o_ref, z_ref, ng_ref = ins
        do_ref, dz_ref, dng_ref = outs
        gv = ng_ref[...]
        for hd in range(n_heads):
            sl = slice(hd * LANES, (hd + 1) * LANES)
            rn, xh, rstd = _rms(o_ref[:, sl], gv)
            zz = z_ref[:, sl]
            don = don_all[:, sl]
            dz_ref[:, sl] = don * rn * _silu_grad(zz)
            drn = don * (zz * _sigmoid(zz))
            _acc_rows(dng_ref, drn * xh)
            do_ref[:, sl] = _rms_bwd(drn, xh, rstd, gv)

    d = o.shape[1]
    return _Tail(fn, [(o, "rows"), (z, "rows"), (ng, "const")], [(d, "rows"), (d, "rows"), (LANES, "acc")])


def _mm_tn_blocked(x, dy, name, out_dtype=f32):
    t, k = x.shape
    bn = dy.shape[1] // N_DEV
    tm = _tile(t, 2048 if k <= 1024 else 1024)
    jb = N_DEV
    while jb > 1 and k * jb * bn * 4 > 8 * 1024 * 1024:
        jb //= 2
    nt = t // tm

    def body(x_ref, dy_ref, o_ref, *acc):
        acc_ref = acc[0] if acc else o_ref
        _acc_init(pl.program_id(1), acc_ref)
        xt = x_ref[...].astype(bf16).T
        for jj in range(jb):
            acc_ref[jj] += jnp.dot(xt, dy_ref[:, jj * bn:(jj + 1) * bn].astype(bf16), preferred_element_type=f32)
        if acc:
            @pl.when(pl.program_id(1) == nt - 1)
            def _():
                o_ref[...] = acc_ref[...].astype(out_dtype)

    return _pc(body, name=name, grid=(N_DEV // jb, nt),
               in_specs=[pl.BlockSpec((tm, k), lambda j, i: (i, 0)), pl.BlockSpec((tm, jb * bn), lambda j, i: (i, j))],
               out_specs=pl.BlockSpec((jb, k, bn), lambda j, i: (j, 0, 0)),
               out_shape=jax.ShapeDtypeStruct((N_DEV, k, bn), out_dtype),
               scratch=[] if out_dtype == f32 else [pltpu.VMEM((jb, k, bn), f32)],
               sem=("parallel", "arbitrary"))(x, dy)


def _mm_tn(x, dy, name, out_dtype=f32):
    t, k = x.shape
    n = dy.shape[1]
    tm = _tile(t, 2048 if k <= 1024 else 1024)
    cap = max(LANES, (2 * 1024 * 1024) // k)
    tn = n
    if n > cap:
        tn = max(c for c in range(LANES, cap + 1, LANES) if n % c == 0)
    nt = t // tm

    def body(x_ref, dy_ref, o_ref, *acc):
        acc_ref = acc[0] if acc else o_ref
        _acc_init(pl.program_id(1), acc_ref)
        acc_ref[...] += lax.dot_general(x_ref[...].astype(bf16), dy_ref[...].astype(bf16), TN, preferred_element_type=f32)
        if acc:
            @pl.when(pl.program_id(1) == nt - 1)
            def _():
                o_ref[...] = acc_ref[...].astype(out_dtype)

    return _pc(body, name=name, grid=(n // tn, nt),
               in_specs=[pl.BlockSpec((tm, k), lambda j, i: (i, 0)), pl.BlockSpec((tm, tn), lambda j, i: (i, j))],
               out_specs=pl.BlockSpec((k, tn), lambda j, i: (0, j)),
               out_shape=jax.ShapeDtypeStruct((k, n), out_dtype),
               scratch=[] if out_dtype == f32 else [pltpu.VMEM((k, tn), f32)],
               sem=("parallel", "arbitrary"))(x, dy)


def _gates_bwd(dgb_t, ab, a_log_pad, dt_pad, n_heads):
    t = ab.shape[0]
    tm = _tile(t, ROW_TILE)
    gate_rows = 2 * n_heads

    def body(dgb_ref, ab_ref, al_ref, dt_ref, dab_ref, dal_ref, ddt_ref):
        _acc_init(pl.program_id(0), dal_ref, ddt_ref)
        ab = ab_ref[...]
        dgb = jnp.concatenate([dgb_ref[...], jnp.zeros((LANES - gate_rows, tm), f32)], axis=0).T
        lane = lax.broadcasted_iota(jnp.int32, ab.shape, 1)
        is_a = lane < n_heads
        is_b = jnp.logical_and(lane >= n_heads, lane < 2 * n_heads)
        xa = ab + dt_ref[...]
        neg_a = -jnp.exp(al_ref[...])
        dg_da = neg_a * _sigmoid(xa)
        beta = _sigmoid(ab)
        da = jnp.where(is_a, dgb * dg_da, 0.0)
        dab_ref[...] = da + jnp.where(is_b, dgb * beta * (1.0 - beta), 0.0)
        _acc_rows(dal_ref, jnp.where(is_a, dgb * neg_a * _softplus(xa), 0.0))
        _acc_rows(ddt_ref, da)

    return _pc(body, name="gates_bwd", grid=(t // tm,),
               in_specs=[pl.BlockSpec((gate_rows, tm), lambda i: (0, i)), _rows(tm, LANES), _const((1, LANES)), _const((1, LANES))],
               out_specs=[_rows(tm, LANES), _const((SUBLANES, LANES)), _const((SUBLANES, LANES))],
               out_shape=[jax.ShapeDtypeStruct((t, LANES), f32), jax.ShapeDtypeStruct((SUBLANES, LANES), f32),
                          jax.ShapeDtypeStruct((SUBLANES, LANES), f32)],
               sem=("arbitrary",))(dgb_t, ab, a_log_pad, dt_pad)


def _glu_bwd(dgl, ub):
    t, d = dgl.shape
    tm = _tile(t, ROW_TILE)

    def body(dgl_ref, u_ref, du_ref, cs_ref):
        _acc_init(pl.program_id(0), cs_ref)
        dgl = dgl_ref[...]
        a = u_ref[:, :d].astype(f32)
        sb = _sigmoid(u_ref[:, d:].astype(f32))
        da = dgl * sb
        db = dgl * a * sb * (1.0 - sb)
        du_ref[:, :d] = da.astype(bf16)
        du_ref[:, d:] = db.astype(bf16)
        cs_ref[0:1, :d] += jnp.sum(da, axis=0, keepdims=True)
        cs_ref[0:1, d:] += jnp.sum(db, axis=0, keepdims=True)

    return _pc(body, name="glu_bwd", grid=(t // tm,),
               in_specs=[_rows(tm, d), _rows(tm, 2 * d)],
               out_specs=[_rows(tm, 2 * d), _const((SUBLANES, 2 * d))],
               out_shape=[jax.ShapeDtypeStruct((t, 2 * d), bf16), jax.ShapeDtypeStruct((SUBLANES, 2 * d), f32)],
               sem=("arbitrary",))(dgl, ub)


def _conv_rows(s):
    return 256 if s % 256 == 0 else s


def _conv_tap_sum(pad_ref, w_ref, base, rows, width):
    acc = jnp.zeros((rows, LANES), f32)
    for j in range(width):
        acc = acc + w_ref[j:j + 1, :] * pad_ref[pl.ds(base + CONV_PAD - (width - 1) + j, rows), :]
    return acc


def _qkv_kinds(j, n_heads, run):
    pl.when(j < n_heads)(lambda: run("q", float(LANES) ** -0.5))
    pl.when(jnp.logical_and(j >= n_heads, j < 2 * n_heads))(lambda: run("k", 1.0))
    pl.when(j >= 2 * n_heads)(lambda: run("v", 1.0))


def _l2_silu_post(c, kind, scale):
    a = c * _sigmoid(c)
    if kind == "v":
        return a, None, a
    r = lax.rsqrt(jnp.sum(a * a, axis=-1, keepdims=True) + L2_EPS)
    return a, r, a * (r * scale)


def _dwconv_fwd(x, w, b, name, qk_heads=None):
    bl, s, cn = x.shape
    width = w.shape[0]
    rows = _conv_rows(s)

    def body(x_ref, w_ref, b_ref, o_ref, pad_ref):
        pad_ref[0:CONV_PAD, :] = jnp.zeros((CONV_PAD, LANES), f32)
        pad_ref[CONV_PAD:, :] = x_ref[0]

        def run(kind, scale):
            def step(i, carry):
                base = pl.multiple_of(i * rows, rows)
                acc = _conv_tap_sum(pad_ref, w_ref, base, rows, width)
                if kind is None:
                    acc = acc + b_ref[...]
                else:
                    _, _, acc = _l2_silu_post(acc, kind, scale)
                o_ref[0, pl.ds(base, rows), :] = acc
                return carry

            lax.fori_loop(0, s // rows, step, 0)

        if qk_heads is None:
            run(None, 1.0)
        else:
            _qkv_kinds(pl.program_id(1), qk_heads, run)

    return _pc(body, name=name, grid=(bl, cn // LANES),
               in_specs=[pl.BlockSpec((1, s, LANES), lambda bi, j: (bi, 0, j)),
                         pl.BlockSpec((width, LANES), lambda bi, j: (0, j)),
                         pl.BlockSpec((1, LANES), lambda bi, j: (0, j))],
               out_specs=pl.BlockSpec((1, s, LANES), lambda bi, j: (bi, 0, j)),
               out_shape=jax.ShapeDtypeStruct((bl, s, cn), f32),
               scratch=[pltpu.VMEM((s + CONV_PAD, LANES), f32)],
               sem=("parallel", "parallel"))(x, w, b)


def _dwconv_bwd(x, dys, w, name, qk_heads=None):
    bl, s, cn = x.shape
    width = w.shape[0]
    wp = -(-width // SUBLANES) * SUBLANES
    rows = _conv_rows(s)
    nblk = s // rows
    ndy = len(dys)

    def body(*refs):
        x_ref, w_ref = refs[0], refs[1]
        dy_refs = refs[2:2 + ndy]
        dx_ref, dw_ref, db_ref, xpad, dypad, acc = refs[2 + ndy:]
        j = pl.program_id(0)
        bi = pl.program_id(1)
        _acc_init(bi, acc, db_ref)
        xpad[0:CONV_PAD, :] = jnp.zeros((CONV_PAD, LANES), f32)
        xpad[CONV_PAD:, :] = x_ref[0]
        dypad[s:, :] = jnp.zeros((CONV_PAD, LANES), f32)
        if qk_heads is None:
            dypad[0:s, :] = dy_refs[0][0]
        else:
            def run(kind, scale):
                dy_ref = dy_refs["qkv".index(kind)]

                def pre(i, carry):
                    base = pl.multiple_of(i * rows, rows)
                    c = _conv_tap_sum(xpad, w_ref, base, rows, width)
                    a, r, _ = _l2_silu_post(c, kind, scale)
                    da = dy_ref[0, pl.ds(base, rows), :]
                    if kind != "v":
                        dy = da * scale
                        da = r * (dy - a * (r * r) * jnp.sum(a * dy, axis=-1, keepdims=True))
                    dypad[pl.ds(base, rows), :] = da * _silu_grad(c)
                    return carry

                lax.fori_loop(0, nblk, pre, 0)

            _qkv_kinds(j, qk_heads, run)

        def step(i, carry):
            base = pl.multiple_of(i * rows, rows)
            dxa = jnp.zeros((rows, LANES), f32)
            for jj in range(width):
                dxa = dxa + w_ref[jj:jj + 1, :] * dypad[pl.ds(base + (width - 1) - jj, rows), :]
            dx_ref[0, pl.ds(base, rows), :] = dxa
            dyc = dypad[pl.ds(base, rows), :]
            db_ref[...] += dyc.reshape(rows // SUBLANES, SUBLANES, LANES).sum(axis=0)
            for jj in range(width):
                prod = dyc * xpad[pl.ds(base + CONV_PAD - (width - 1) + jj, rows), :]
                acc[jj * SUBLANES:(jj + 1) * SUBLANES, :] += prod.reshape(rows // SUBLANES, SUBLANES, LANES).sum(axis=0)
            return carry

        lax.fori_loop(0, nblk, step, 0)

        @pl.when(bi == bl - 1)
        def _():
            dw_ref[...] = jnp.zeros((wp, LANES), f32)
            for jj in range(width):
                dw_ref[jj:jj + 1, :] = jnp.sum(acc[jj * SUBLANES:(jj + 1) * SUBLANES, :], axis=0, keepdims=True)

    if qk_heads is None:
        dy_specs = [pl.BlockSpec((1, s, LANES), lambda j, bi: (bi, 0, j))]
    else:
        hh = qk_heads
        def dy_spec(part):
            def index(j, bi):
                mine = jnp.logical_and(j >= part * hh, j < (part + 1) * hh)
                return (jnp.where(mine, bi * hh + j - part * hh, 0), 0, 0)
            return pl.BlockSpec((1, s, LANES), index)

        dy_specs = [dy_spec(0), dy_spec(1), dy_spec(2)]
    return _pc(body, name=name, grid=(cn // LANES, bl),
               in_specs=[pl.BlockSpec((1, s, LANES), lambda j, bi: (bi, 0, j)),
                         pl.BlockSpec((width, LANES), lambda j, bi: (0, j))] + dy_specs,
               out_specs=[pl.BlockSpec((1, s, LANES), lambda j, bi: (bi, 0, j)),
                          pl.BlockSpec((wp, LANES), lambda j, bi: (0, j)),
                          pl.BlockSpec((SUBLANES, LANES), lambda j, bi: (0, j))],
               out_shape=[jax.ShapeDtypeStruct((bl, s, cn), f32), jax.ShapeDtypeStruct((wp, cn), f32),
                          jax.ShapeDtypeStruct((SUBLANES, cn), f32)],
               scratch=[pltpu.VMEM((s + CONV_PAD, LANES), f32), pltpu.VMEM((s + CONV_PAD, LANES), f32),
                        pltpu.VMEM((width * SUBLANES, LANES), f32)],
               sem=("parallel", "arbitrary"))(x, w, *dys)


def _group_rows(s):
    for rows in (256, 128):
        if s % rows == 0:
            return rows
    return CHUNK


def _group_masks(rows):
    r = lax.broadcasted_iota(jnp.int32, (rows, rows), 0)
    c = lax.broadcasted_iota(jnp.int32, (rows, rows), 1)
    return r, c, r >= c


def _decay(gc_col, gc_row, causal):
    return jnp.exp(jnp.where(causal, gc_col - gc_row, NEG))


def _inv_unit_lower_many(mats, r, c):
    n = r.shape[0]
    eye = (r == c).astype(f32)
    same16 = (r >> 4) == (c >> 4)
    ads = [jnp.where(same16, a, 0.0) for a in mats]
    aos = [a - ad for a, ad in zip(mats, ads)]
    xs = ads
    tds = [-x for x in xs]
    for _ in range(3):
        xs = [_dot(x, x) for x in xs]
        tds = [td + x + _dot(td, x) for td, x in zip(tds, xs)]
    bs = [ao + _dot(td, ao) for td, ao in zip(tds, aos)]
    ps = [-b for b in bs]
    span = 2
    while span < n // 16:
        bs = [_dot(b, b) for b in bs]
        ps = [p + b + _dot(p, b) for p, b in zip(ps, bs)]
        span *= 2
    return [p + td + _dot(p, td) for p, td in zip(ps, tds)]


def _lane_cumsum(x, y, reverse, name):
    rr, n = x.shape

    def body(x_ref, y_ref, o_ref):
        i = lax.broadcasted_iota(jnp.int32, (n, n), 0)
        j = lax.broadcasted_iota(jnp.int32, (n, n), 1)
        tri = ((i >= j) if reverse else (i <= j)).astype(f32)
        o_ref[...] = jnp.dot(x_ref[...] + y_ref[...], tri, precision=HI, preferred_element_type=f32)

    spec = pl.BlockSpec((rr, n), lambda: (0, 0))
    return pl.pallas_call(body, name=name, in_specs=[spec, spec], out_specs=spec,
                          out_shape=jax.ShapeDtypeStruct((rr, n), f32))(x, y)


def _gdn_specs(n_heads, nblk, rows, hp=1, rev=False):
    def blk(n):
        return nblk - 1 - n if rev else n

    def qkv(off):
        return pl.BlockSpec((rows, hp * LANES), lambda g, n: (
            lax.div(g * hp, n_heads) * nblk + blk(n), lax.div(off * n_heads + lax.rem(g * hp, n_heads), hp)))

    def per_head(last, block_rows=rows):
        return pl.BlockSpec((hp, block_rows, last), lambda g, n: (g, blk(n), 0))

    def row_vec():
        return pl.BlockSpec((hp, 1, 1, rows), lambda g, n: (g, blk(n), 0, 0))

    return qkv, per_head, row_vec


def _gdn_prep(qkv, cols, grow, bl, s, n_heads):
    rows = _group_rows(s)
    nblk = s // rows
    bh_n = bl * n_heads
    hp = PREP_HEADS_PER_STEP if n_heads % PREP_HEADS_PER_STEP == 0 else 1
    qkv_spec, ph, rv = _gdn_specs(n_heads, nblk, rows, hp)

    def body(k_ref, v_ref, cols_ref, grow_ref, u_ref, w_ref, t_ref):
        r, c, causal = _group_masks(rows)
        mats, rhs = [], []
        for h in range(hp):
            hs = slice(h * LANES, (h + 1) * LANES)
            k = k_ref[:, hs]
            gc = cols_ref[h, :, 0:1]
            beta = cols_ref[h, :, 1:2]
            kb = k * beta
            mats.append(jnp.where(r > c, _dot_nt(kb, k) * _decay(gc, grow_ref[h, 0], causal), 0.0))
            rhs.append((v_ref[:, hs] * beta, kb * jnp.exp(gc)))
        for h, tm in enumerate(_inv_unit_lower_many(mats, r, c)):
            tb = tm.astype(bf16)
            u_ref[h] = rhs[h][0] + jnp.dot(tb, rhs[h][0].astype(bf16), preferred_element_type=f32)
            w_ref[h] = (rhs[h][1] + jnp.dot(tb, rhs[h][1].astype(bf16), preferred_element_type=f32)).astype(bf16)
            t_ref[h] = tb

    return _pc(body, name="gdn_prep", grid=(bh_n // hp, nblk),
               in_specs=[qkv_spec(1), qkv_spec(2), ph(2), rv()],
               out_specs=[ph(LANES), ph(LANES), ph(rows)],
               out_shape=[jax.ShapeDtypeStruct((bh_n, s, LANES), f32), jax.ShapeDtypeStruct((bh_n, s, LANES), bf16),
                          jax.ShapeDtypeStruct((bh_n, s, rows), bf16)],
               sem=("parallel", "parallel"))(qkv, qkv, cols, grow)


def _gdn_scan(qkv, u, w, cols, grow, bl, s, n_heads):
    rows = _group_rows(s)
    nblk = s // rows
    bh_n = bl * n_heads
    d = n_heads * LANES
    hp = HEADS_PER_STEP if n_heads % HEADS_PER_STEP == 0 else 1
    qkv_spec, ph, rv = _gdn_specs(n_heads, nblk, rows, hp)

    def body(q_ref, k_ref, u_ref, w_ref, cols_ref, grow_ref, o_ref, vn_ref, ss_ref, s_scr):
        _acc_init(pl.program_id(1), s_scr)
        _, _, causal = _group_masks(rows)
        hh = range(hp)
        qs = [q_ref[:, h * LANES:(h + 1) * LANES] for h in hh]
        ks = [k_ref[:, h * LANES:(h + 1) * LANES] for h in hh]
        gcs = [cols_ref[h, :, 0:1] for h in hh]
        ps = [_dot_nt(qs[h], ks[h]) * _decay(gcs[h], grow_ref[h, 0], causal) for h in hh]
        sts = [s_scr[h] for h in hh]
        for h in hh:
            ss_ref[h] = sts[h].astype(bf16)
        vns = [u_ref[h] - _dot(w_ref[h], sts[h]) for h in hh]
        for h in hh:
            vn_ref[h] = vns[h].astype(bf16)
        o_state = [_dot(qs[h] * jnp.exp(gcs[h]), sts[h]) for h in hh]
        o_intra = [_dot(ps[h], vns[h]) for h in hh]
        for h in hh:
            o_ref[:, h * LANES:(h + 1) * LANES] = o_state[h] + o_intra[h]
        for h in hh:
            g_last = gcs[h][rows - 1:rows, :]
            s_scr[h] = jnp.exp(g_last) * sts[h] + _dot_tn(ks[h] * jnp.exp(g_last - gcs[h]), vns[h])

    return _pc(body, name="gdn_scan", grid=(bh_n // hp, nblk),
               in_specs=[qkv_spec(0), qkv_spec(1), ph(LANES), ph(LANES), ph(2), rv()],
               out_specs=[qkv_spec(0), ph(LANES), ph(LANES, block_rows=LANES)],
               out_shape=[jax.ShapeDtypeStruct((bl * s, d), f32), jax.ShapeDtypeStruct((bh_n, s, LANES), bf16),
                          jax.ShapeDtypeStruct((bh_n, nblk * LANES, LANES), bf16)],
               scratch=[pltpu.VMEM((hp, LANES, LANES), f32)],
               sem=("parallel", "arbitrary"))(qkv, qkv, u, w, cols, grow)


def _gdn_scan_bwd(do, qkv, w, vn, cols, grow, ss, bl, s, n_heads):
    rows = _group_rows(s)
    nblk = s // rows
    bh_n = bl * n_heads
    hp = HEADS_PER_STEP if n_heads % HEADS_PER_STEP == 0 else 1
    qkv_spec, ph, rv = _gdn_specs(n_heads, nblk, rows, hp, rev=True)

    def body(do_ref, q_ref, k_ref, w_ref, vn_ref, cols_ref, grow_ref, ss_ref,
             du_ref, dw_ref, dq_ref, dk_ref, dcol_ref, drow_ref, ds_scr):
        _acc_init(pl.program_id(1), ds_scr)
        _, _, causal = _group_masks(rows)
        last_row = lax.broadcasted_iota(jnp.int32, (rows, 1), 0) == rows - 1
        hh = range(hp)
        dos = [do_ref[:, h * LANES:(h + 1) * LANES] for h in hh]
        qs = [q_ref[:, h * LANES:(h + 1) * LANES] for h in hh]
        ks = [k_ref[:, h * LANES:(h + 1) * LANES] for h in hh]
        vns = [vn_ref[h] for h in hh]
        gcs = [cols_ref[h, :, 0:1] for h in hh]
        sts = [ss_ref[h] for h in hh]
        dss = [ds_scr[h] for h in hh]
        dmats = [_decay(gcs[h], grow_ref[h, 0], causal) for h in hh]
        gams = [jnp.exp(gc) for gc in gcs]
        qgs = [qs[h] * gams[h] for h in hh]
        g_lasts = [gc[rows - 1:rows, :] for gc in gcs]
        kd_scales = [jnp.exp(g_lasts[h] - gcs[h]) for h in hh]
        kdecs = [ks[h] * kd_scales[h] for h in hh]
        qks = [_dot_nt(qs[h], ks[h]) for h in hh]
        dpds = [_dot_nt(dos[h], vns[h]) * dmats[h] for h in hh]
        dvns = [_dot_tn(qks[h] * dmats[h], dos[h]) + _dot(kdecs[h], dss[h]) for h in hh]
        for h in hh:
            du_ref[h] = dvns[h]
        dkdecs = [_dot_nt(vns[h], dss[h]) for h in hh]
        for h in hh:
            dw_ref[h] = -_dot_nt(dvns[h], sts[h])
        dqgs = [_dot_nt(dos[h], sts[h]) for h in hh]
        dq_intra = [_dot(dpds[h], ks[h]) for h in hh]
        dk_intra = [_dot_tn(dpds[h], qs[h]) for h in hh]
        for h in hh:
            dq_ref[h] = dqgs[h] * gams[h] + dq_intra[h]
            dk_ref[h] = dk_intra[h] + dkdecs[h] * kd_scales[h]
            ep = dpds[h] * qks[h]
            drow_ref[h, 0] = -jnp.sum(ep, axis=0, keepdims=True)
            kd_rows = jnp.sum(dkdecs[h] * kdecs[h], axis=-1, keepdims=True)
            extra = jnp.sum(kd_rows) + jnp.exp(g_lasts[h]) * jnp.sum(sts[h] * dss[h])
            dcol_ref[h] = (jnp.sum(dqgs[h] * qgs[h], axis=-1, keepdims=True) + jnp.sum(ep, axis=-1, keepdims=True)
                           - kd_rows + jnp.where(last_row, extra, 0.0))
        ds_new = [jnp.exp(g_lasts[h]) * dss[h] + _dot_tn(qgs[h], dos[h]) - _dot_tn(w_ref[h], dvns[h]) for h in hh]
        for h in hh:
            ds_scr[h] = ds_new[h]

    return _pc(body, name="gdn_scan_bwd", grid=(bh_n // hp, nblk),
               in_specs=[qkv_spec(0), qkv_spec(0), qkv_spec(1), ph(LANES), ph(LANES), ph(2), rv(),
                         ph(LANES, block_rows=LANES)],
               out_specs=[ph(LANES), ph(LANES), ph(LANES), ph(LANES), ph(1), rv()],
               out_shape=[jax.ShapeDtypeStruct((bh_n, s, LANES), f32)] * 4
               + [jax.ShapeDtypeStruct((bh_n, s, 1), f32), jax.ShapeDtypeStruct((bh_n, nblk, 1, rows), f32)],
               scratch=[pltpu.VMEM((hp, LANES, LANES), f32)],
               sem=("parallel", "arbitrary"))(do, qkv, qkv, w, vn, cols, grow, ss)


def _gdn_prep_bwd(qkv, cols, grow, tmat, du, dw, dk_scan, dcol_scan, drow_scan, bl, s, n_heads):
    rows = _group_rows(s)
    nblk = s // rows
    bh_n = bl * n_heads
    hp = PREP_HEADS_PER_STEP if n_heads % PREP_HEADS_PER_STEP == 0 else 1
    qkv_spec, ph, rv = _gdn_specs(n_heads, nblk, rows, hp)

    def body(k_ref, v_ref, cols_ref, grow_ref, t_ref, du_ref, dw_ref, dks_ref, dcs_ref, drs_ref,
             dk_ref, dv_ref, dcols_ref, drow_ref):
        r, c, causal = _group_masks(rows)
        hh = range(hp)
        ks = [k_ref[:, h * LANES:(h + 1) * LANES] for h in hh]
        vs = [v_ref[:, h * LANES:(h + 1) * LANES] for h in hh]
        gcs = [cols_ref[h, :, 0:1] for h in hh]
        betas = [cols_ref[h, :, 1:2] for h in hh]
        tms = [t_ref[h] for h in hh]
        dus = [du_ref[h] for h in hh]
        dws = [dw_ref[h] for h in hh]
        gams = [jnp.exp(gc) for gc in gcs]
        kbs = [k * b for k, b in zip(ks, betas)]
        kbgs = [kb * g for kb, g in zip(kbs, gams)]
        dts = [_dot_nt(dus[h], vs[h] * betas[h]) + _dot_nt(dws[h], kbgs[h]) for h in hh]
        dvbs = [dus[h] + _dot_tn(tms[h], dus[h]) for h in hh]
        dkbgs = [dws[h] + _dot_tn(tms[h], dws[h]) for h in hh]
        kks = [_dot_nt(kbs[h], ks[h]) for h in hh]
        inner = [dts[h] + _dot_nt(dts[h], tms[h]) for h in hh]
        dads = [jnp.where(r > c, -(inner[h] + _dot_tn(tms[h], inner[h])), 0.0) * _decay(gcs[h], grow_ref[h, 0], causal)
                for h in hh]
        dkbs = [dkbgs[h] * gams[h] + _dot(dads[h], ks[h]) for h in hh]
        dk2 = [_dot_tn(dads[h], kbs[h]) for h in hh]
        for h in hh:
            dk_ref[h] = dks_ref[h] + dk2[h] + dkbs[h] * betas[h]
            dv_ref[h] = dvbs[h] * betas[h]
            ea = dads[h] * kks[h]
            dcols_ref[h, :, 0:1] = (dcs_ref[h] + jnp.sum(dkbgs[h] * kbgs[h], axis=-1, keepdims=True)
                                    + jnp.sum(ea, axis=-1, keepdims=True))
            dcols_ref[h, :, 1:2] = (jnp.sum(dvbs[h] * vs[h], axis=-1, keepdims=True)
                                    + jnp.sum(dkbs[h] * ks[h], axis=-1, keepdims=True))
            drow_ref[h, 0] = drs_ref[h, 0] - jnp.sum(ea, axis=0, keepdims=True)

    return _pc(body, name="gdn_prep_bwd", grid=(bh_n // hp, nblk),
               in_specs=[qkv_spec(1), qkv_spec(2), ph(2), rv(), ph(rows), ph(LANES), ph(LANES), ph(LANES), ph(1), rv()],
               out_specs=[ph(LANES), ph(LANES), ph(2), rv()],
               out_shape=[jax.ShapeDtypeStruct((bh_n, s, LANES), f32), jax.ShapeDtypeStruct((bh_n, s, LANES), f32),
                          jax.ShapeDtypeStruct((bh_n, s, 2), f32), jax.ShapeDtypeStruct((bh_n, nblk, 1, rows), f32)],
               sem=("parallel", "parallel"))(qkv, qkv, cols, grow, tmat, du, dw, dk_scan, dcol_scan, drow_scan)


def _row(v):
    return v.reshape(1, -1).astype(f32)


def _pad_lanes(v):
    v = v.reshape(1, -1).astype(f32)
    return jnp.pad(v, ((0, 0), (0, LANES - v.shape[1])))


def _local_step(x, tgt, p, need, emit):
    bl, s, d = x.shape
    t = bl * s
    n_heads = p["gdn_a_log"].shape[-1]
    assert d == n_heads * LANES and s % CHUNK == 0
    x2 = x.reshape(t, d)
    tgt2 = tgt.reshape(t, d)
    gr = {}

    n0, ub, gl = _pw1_glu(x2, _row(p["norm_mix_g"][0]), p["cv_w_pw1"], _row(p["cv_b_pw1"]))
    cv = need("conv", n0)
    dc = _dwconv_fwd(gl.reshape(bl, s, d), cv["cv_w_dw"], _row(p["cv_b_dw"]), "dwconv_fwd").reshape(t, d)
    sb, h1 = _ln_silu_mm_res(dc, _row(p["cv_ln_g"]), _row(p["cv_ln_b"]), cv["cv_w_pw2"], _row(p["cv_b_pw2"]), x2)
    m0 = need("mlp0", h1)
    n1, f0, r0, h2 = _mlp_fwd(h1, _row(p["norm_ffn_g"][0]), m0["w1"], m0["w2"], "mlp_fwd0")

    gd = need("gdn", h2)
    w_in = gd["w_in"]
    w_ab = jnp.pad(w_in[:, 4 * d:], ((0, 0), (0, LANES - 2 * n_heads)))
    a_log_pad = _pad_lanes(p["gdn_a_log"])
    dt_pad = _pad_lanes(p["gdn_dt_bias"])
    n2, qkv_pre, z, ab, gbeta = _gdn_in(h2, _row(p["norm_mix_g"][1]), w_in, w_ab, a_log_pad, dt_pad, n_heads)
    zero_bias = jnp.zeros((1, 3 * d), f32)
    qkv = _dwconv_fwd(qkv_pre.reshape(bl, s, 3 * d), cv["gdn_conv_w"], zero_bias, "sconv_fwd", qk_heads=n_heads).reshape(t, 3 * d)
    bh_n, rows = bl * n_heads, _group_rows(s)
    gates = gbeta.reshape(2, n_heads, bl, s).transpose(0, 2, 1, 3).reshape(2, bh_n * (s // rows), rows)
    gc_lanes = _lane_cumsum(gates[0], jnp.zeros_like(gates[0]), False, "gdn_gate_cumsum")
    grow = gc_lanes.reshape(bh_n, s // rows, 1, rows)
    cols = jnp.stack([gc_lanes.reshape(bh_n, s), gates[1].reshape(bh_n, s)], axis=-1)
    u, w, tmat = _gdn_prep(qkv, cols, grow, bl, s, n_heads)
    o, vn, ss = _gdn_scan(qkv, u, w, cols, grow, bl, s, n_heads)
    onb, h3 = _gated_norm_mm_res(o, z, _row(p["gdn_norm_g"]), gd["w_out"], h2, n_heads)
    m1 = need("mlp1", h3)
    n3, f1, r1, loss_acc, dh4, dgf = _mlp_fwd(h3, _row(p["norm_ffn_g"][1]), m1["w1"], m1["w2"], "mlp_fwd1_loss",
                                              loss=(_row(p["final_norm_g"]), tgt2))
    loss = loss_acc[0, 0]
    gr["loss"] = loss_acc[0, 0:1]
    gr["final_norm_g"] = dgf[0]

    df1, dh3, dg_ffn1, _ = _mlp_bwd(dh4, h3, _row(p["norm_ffn_g"][1]), f1, m1["w1"], m1["w2"], "mlp_bwd1")
    dw2_1 = _mm_tn(r1, dh4, "dw_mlp2_1", bf16)
    dw1_1 = _mm_tn_blocked(n3, df1, "dw_mlp1_1", bf16)

    dw_out = _mm_tn(onb, dh3, "dw_gdn_out", bf16)
    tie = emit("late", {"mlp_w2_1": dw2_1, "mlp_w1_1": dw1_1, "gdn_w_out": dw_out})
    do, dz, dng = _mm_nt([(dh3, gd["w_out"])], "dx_gdn_out",
                         _gated_norm_bwd_tail(o, z, _row(p["gdn_norm_g"]) + tie, n_heads))
    gr["gdn_norm_g"] = dng[0]
    du, dw_, dq, dk_scan, dcol_scan, drow_scan = _gdn_scan_bwd(do, qkv, w, vn, cols, grow, ss, bl, s, n_heads)
    dk, dv, dcols, drow = _gdn_prep_bwd(qkv, cols, grow, tmat, du, dw_, dk_scan, dcol_scan, drow_scan, bl, s, n_heads)
    dg_lanes = _lane_cumsum(dcols[..., 0].reshape(gc_lanes.shape), drow.reshape(gc_lanes.shape), True, "gdn_gate_cumsum_bwd")
    dqkv_pre, dconv_w, _ = _dwconv_bwd(qkv_pre.reshape(bl, s, 3 * d), [dq, dk, dv], cv["gdn_conv_w"], "sconv_bwd", qk_heads=n_heads)
    gr["gdn_conv_w"] = dconv_w[:cv["gdn_conv_w"].shape[0]]
    dgb_t = jnp.stack([dg_lanes.reshape(bl, n_heads, s), dcols[..., 1].reshape(bl, n_heads, s)])
    dgb_t = dgb_t.transpose(0, 2, 1, 3).reshape(2 * n_heads, t)
    dab, dal, ddt = _gates_bwd(dgb_t, ab, a_log_pad, dt_pad, n_heads)
    gr["gdn_a_log"] = dal[0, :n_heads]
    gr["gdn_dt_bias"] = ddt[0, :n_heads]
    dqkv2 = dqkv_pre.reshape(t, 3 * d)
    dw_in = jnp.concatenate(
        [_mm_tn(n2, dqkv2, "dw_gdn_in_qkv"), _mm_tn(n2, dz, "dw_gdn_in_z"), _mm_tn(n2, dab, "dw_gdn_in_ab")[:, :2 * n_heads]], axis=1)
    tie = emit("gdn_in", {"gdn_w_in": dw_in})
    dh2, dg_mix1 = _mm_nt([(dqkv2, (w_in, 0)), (dz, (w_in, 3)), (dab, w_ab)], "dx_gdn_in",
                          _rms_bwd_tail(h2, _row(p["norm_mix_g"][1]) + tie, dh3))

    df0, dh1, dg_ffn0, cs_h1 = _mlp_bwd(dh2, h1, _row(p["norm_ffn_g"][0]), f0, m0["w1"], m0["w2"], "mlp_bwd0")
    dw2_0 = _mm_tn(r0, dh2, "dw_mlp2_0", bf16)
    dw1_0 = _mm_tn_blocked(n1, df0, "dw_mlp1_0", bf16)
    dw_pw2 = _mm_tn(sb, dh1, "dw_pw2", bf16)
    tie = emit("mlp0", {"mlp_w2_0": dw2_0, "mlp_w1_0": dw1_0, "cv_w_pw2": dw_pw2})
    gr["norm_ffn_g"] = jnp.stack([dg_ffn0[0], dg_ffn1[0]])

    gr["cv_b_pw2"] = cs_h1[0]
    ddc, dlng, dlnb, cs_dc = _mm_nt([(dh1, cv["cv_w_pw2"])], "dx_pw2",
                                    _ln_silu_bwd_tail(dc, _row(p["cv_ln_g"]) + tie, _row(p["cv_ln_b"])))
    gr["cv_ln_g"] = dlng[0]
    gr["cv_ln_b"] = dlnb[0]
    gr["cv_b_dw"] = cs_dc[0]
    dgl, dw_dw, _ = _dwconv_bwd(gl.reshape(bl, s, d), [ddc.reshape(bl, s, d)], cv["cv_w_dw"], "dwconv_bwd")
    gr["cv_w_dw"] = dw_dw[:cv["cv_w_dw"].shape[0]]
    dub, cs_u = _glu_bwd(dgl.reshape(t, d), ub)
    gr["cv_b_pw1"] = cs_u[0]
    dw_pw1 = _mm_tn_blocked(n0, dub, "dw_pw1", bf16)
    tie = emit("last", {"cv_w_pw1": dw_pw1, "small": gr})
    dx, dg_mix0 = _mm_nt([(dub, ("blocks", p["cv_w_pw1"]))], "dx_pw1",
                         _rms_bwd_tail(x2, _row(p["norm_mix_g"][0]) + tie, dh1))
    return loss, dx.reshape(bl, s, d), {"norm_mix_g": jnp.stack([dg_mix0[0], dg_mix1[0]])}


ANY = pl.BlockSpec(memory_space=pl.ANY)
MESH = pl.DeviceIdType.MESH


def _flip(v, bit):
    return 1 - v if bit else v


def _all_gather_many(shards):
    na = len(shards)

    def body(*refs):
        x_refs, o_refs = refs[:na], refs[na:2 * na]
        send_sems, recv_sems, local_sems = refs[2 * na:]
        x, y, c = lax.axis_index("x"), lax.axis_index("y"), lax.axis_index("c")
        me, sibling = (x, y, c), (x, y, 1 - c)
        chips = [(1 - x, y), (x, 1 - y), (1 - x, 1 - y)]

        def copy(a, k, block, to, src=None):
            px, py, pc = block
            dst = o_refs[a].at[4 * px + 2 * py + pc]
            return pltpu.make_async_remote_copy(
                src_ref=dst if src is None else src, dst_ref=dst,
                send_sem=send_sems.at[a, k], recv_sem=recv_sems.at[a, k], device_id=to, device_id_type=MESH)

        mine = [pltpu.make_async_copy(x_refs[a], o_refs[a].at[4 * x + 2 * y + c], local_sems.at[a]) for a in range(na)]
        first = []
        for a in range(na):
            first.append(copy(a, 0, me, sibling, src=x_refs[a]))
            first += [copy(a, 1 + j, me, (*chip, c), src=x_refs[a]) for j, chip in enumerate(chips)]
        for cp in mine + first:
            cp.start()
        passed = []
        for j, chip in enumerate(chips):
            for a in range(na):
                copy(a, 1 + j, (*chip, c), me).wait_recv()
                fwd = copy(a, 4 + j, (*chip, c), sibling)
                fwd.start()
                passed.append(fwd)
        for a in range(na):
            copy(a, 0, sibling, me).wait_recv()
        for j, chip in enumerate(chips):
            for a in range(na):
                copy(a, 4 + j, (*chip, 1 - c), me).wait_recv()
        for cp in first + passed:
            cp.wait_send()
        for cp in mine:
            cp.wait()

    return pl.pallas_call(
        body, name="weights_all_gather",
        out_shape=[jax.ShapeDtypeStruct((N_DEV,) + a.shape, a.dtype) for a in shards],
        in_specs=[ANY] * na, out_specs=[ANY] * na,
        scratch_shapes=[pltpu.SemaphoreType.DMA((na, 7)), pltpu.SemaphoreType.DMA((na, 7)), pltpu.SemaphoreType.DMA((na,))],
        compiler_params=pltpu.CompilerParams(has_side_effects=True),
    )(*shards)


HBM = pl.BlockSpec(memory_space=pltpu.HBM)
SEM = pl.BlockSpec(memory_space=pltpu.SEMAPHORE)
EFFECT = pltpu.SideEffectType.DATAFLOW_SIDE_EFFECTING
N_PEERS = N_DEV - 1


def _exchange_copies(src_refs, land_refs, send_sems, recv_sems, scatter):
    x, y, c = lax.axis_index("x"), lax.axis_index("y"), lax.axis_index("c")
    me = 4 * x + 2 * y + c
    copies = []
    for a, (src, land) in enumerate(zip(src_refs, land_refs)):
        for k in range(1, N_DEV):
            px, py, pc = _flip(x, k & 4), _flip(y, k & 2), _flip(c, k & 1)
            i = a * N_PEERS + k - 1
            copies.append(pltpu.make_async_remote_copy(
                src_ref=src.at[4 * px + 2 * py + pc] if scatter[a] else src, dst_ref=land.at[me],
                send_sem=send_sems.at[i], recv_sem=recv_sems.at[i], device_id=(px, py, pc), device_id_type=MESH))
    return copies


def _exchange_start(srcs, scatter, name):
    na = len(srcs)
    lands = [lax.empty(s.shape if sc else (N_DEV,) + s.shape, s.dtype) for s, sc in zip(srcs, scatter)]

    def body(*refs):
        copies = _exchange_copies(refs[:na], refs[na:2 * na], refs[2 * na], refs[2 * na + 1], scatter)
        for cp in copies:
            cp.start()
        token = refs[-1]
        token[...] = jnp.zeros_like(token)

    outs = pl.pallas_call(
        body, name=name,
        out_shape=(pltpu.SemaphoreType.DMA((na * N_PEERS,)), pltpu.SemaphoreType.DMA((na * N_PEERS,)))
        + tuple(pltpu.HBM(a.shape, a.dtype) for a in srcs + lands) + (jax.ShapeDtypeStruct((SUBLANES, LANES), f32),),
        in_specs=[HBM] * (2 * na),
        out_specs=(SEM, SEM) + (HBM,) * (2 * na) + (pl.BlockSpec(memory_space=pltpu.VMEM),),
        input_output_aliases={i: 2 + i for i in range(2 * na)},
        compiler_params=pltpu.CompilerParams(has_side_effects=EFFECT),
    )(*[pltpu.with_memory_space_constraint(a, pltpu.HBM) for a in srcs + lands])
    return outs[0], outs[1], list(outs[2:2 + na]), list(outs[2 + na:2 + 2 * na]), outs[-1]


def _exchange_wait(started, after, scatter, name):
    send_sems, recv_sems, srcs, lands, _ = started
    na = len(srcs)

    def body(*refs):
        for cp in _exchange_copies(refs[:na], refs[na:2 * na], refs[2 * na], refs[2 * na + 1], scatter):
            cp.wait_send()
            cp.wait_recv()

    outs = pl.pallas_call(
        body, name=name,
        out_shape=tuple(pltpu.HBM(a.shape, a.dtype) for a in srcs + lands),
        in_specs=[HBM] * (2 * na) + [SEM, SEM, ANY], out_specs=(HBM,) * (2 * na),
        input_output_aliases={i: i for i in range(2 * na)},
        compiler_params=pltpu.CompilerParams(has_side_effects=EFFECT),
    )(*srcs, *lands, send_sems, recv_sems, after)
    return list(outs[:na]), list(outs[na:])


def _own_block(land, block, me):
    return lax.dynamic_update_index_in_dim(land, block, me, 0)


def _sum8_adamw(r2, w, m, v, layer, name):
    _, rr, cc = r2.shape
    tr = _tile(rr, 256)
    bc1 = 1.0 - ADAM_B1 ** ADAM_STEP
    bc2 = 1.0 - ADAM_B2 ** ADAM_STEP

    def body(r_ref, w_ref, m_ref, v_ref, g_ref, d_ref, nm_ref, nv_ref):
        gv = r_ref[0].astype(f32)
        for q in range(1, N_DEV):
            gv = gv + r_ref[q].astype(f32)
        g_ref[...] = gv
        nm = ADAM_B1 * m_ref[...] + (1.0 - ADAM_B1) * gv
        nv = ADAM_B2 * v_ref[...] + (1.0 - ADAM_B2) * (gv * gv)
        nm_ref[...] = nm
        nv_ref[...] = nv
        d_ref[...] = -ADAM_LR * ((nm / bc1) / (jnp.sqrt(nv / bc2) + ADAM_EPS) + ADAM_WD * w_ref[...])

    lspec = pl.BlockSpec((None, tr, cc), lambda i: (layer, i, 0))
    return _pc(body, name=name, grid=(rr // tr,),
               in_specs=[pl.BlockSpec((N_DEV, tr, cc), lambda i: (0, i, 0)), lspec, lspec, lspec],
               out_specs=[_rows(tr, cc)] * 4, out_shape=[jax.ShapeDtypeStruct((rr, cc), f32)] * 4,
               sem=("parallel",))(r2, w, m, v)


def _sum_devices(recv, name):
    _, rr, _ = recv.shape
    tr = _tile(rr, 512)

    def body(r_ref, o_ref):
        acc = r_ref[0]
        for i in range(1, N_DEV):
            acc = acc + r_ref[i]
        o_ref[...] = acc

    return _pc(body, name=name, grid=(rr // tr,),
               in_specs=[pl.BlockSpec((N_DEV, tr, LANES), lambda i: (0, i, 0))],
               out_specs=_rows(tr, LANES), out_shape=jax.ShapeDtypeStruct((rr, LANES), f32), sem=("parallel",))(recv)


def _adamw(w, g, m, v):
    rr = w.shape[0]
    tr = _tile(rr, 512)
    bc1 = 1.0 - ADAM_B1 ** ADAM_STEP
    bc2 = 1.0 - ADAM_B2 ** ADAM_STEP

    def body(w_ref, g_ref, m_ref, v_ref, d_ref, nm_ref, nv_ref):
        gv = g_ref[...]
        nm = ADAM_B1 * m_ref[...] + (1.0 - ADAM_B1) * gv
        nv = ADAM_B2 * v_ref[...] + (1.0 - ADAM_B2) * (gv * gv)
        nm_ref[...] = nm
        nv_ref[...] = nv
        d_ref[...] = -ADAM_LR * ((nm / bc1) / (jnp.sqrt(nv / bc2) + ADAM_EPS) + ADAM_WD * w_ref[...])

    spec = _rows(tr, LANES)
    return _pc(body, name="adamw", grid=(rr // tr,), in_specs=[spec] * 4, out_specs=[spec] * 3,
               out_shape=[jax.ShapeDtypeStruct((rr, LANES), f32)] * 3, sem=("parallel",))(w, g, m, v)


PACK_ROWS = 512
PART_ROWS = SUBLANES


def _pack(arrs):
    parts, sizes = [], []
    for a in arrs:
        flat = a.reshape(-1)
        n = flat.shape[0]
        rows = -(-n // (LANES * PART_ROWS)) * PART_ROWS
        if rows * LANES != n:
            flat = jnp.pad(flat, (0, rows * LANES - n))
        parts.append(flat.reshape(rows, LANES))
        sizes.append((rows, n))
    total = sum(r for r, _ in sizes)
    padded = -(-total // PACK_ROWS) * PACK_ROWS
    if padded > total:
        parts.append(jnp.zeros((padded - total, LANES), parts[0].dtype))
    return jnp.concatenate(parts, axis=0), sizes


def _unpack(packed, sizes, shapes):
    out, off = [], 0
    for (rows, n), shp in zip(sizes, shapes):
        piece = lax.slice_in_dim(packed, off, off + rows, axis=0)
        if rows * LANES != n:
            piece = lax.slice_in_dim(piece.reshape(-1), 0, n, axis=0)
        out.append(piece.reshape(tuple(shp)))
        off += rows
    return out


def _cols_to_blocks(a):
    n = a.shape[-1] // N_DEV
    a = a.reshape(a.shape[:-1] + (N_DEV, n))
    return jnp.moveaxis(a, -2, 0)


def _blocks_to_cols(a):
    a = jnp.moveaxis(a, 0, -2)
    return a.reshape(a.shape[:-2] + (a.shape[-2] * a.shape[-1],))


def _rows_to_blocks(a):
    k = a.shape[-2] // N_DEV
    a = a.reshape(a.shape[:-2] + (N_DEV, k, a.shape[-1]))
    return jnp.moveaxis(a, -3, 0)


def _blocks_to_rows(a):
    a = jnp.moveaxis(a, 0, -3)
    return a.reshape(a.shape[:-3] + (a.shape[-3] * a.shape[-2], a.shape[-1]))


COL_SHARDED = ("cv_w_pw1", "gdn_w_in", "mlp_w1")
ROW_SHARDED = ("cv_w_pw2", "gdn_w_out", "mlp_w2")
CONV_SHARDED = ("cv_w_dw", "gdn_conv_w")
REPLICATED = ("norm_mix_g", "norm_ffn_g", "final_norm_g", "cv_b_pw1", "cv_b_dw", "cv_ln_g", "cv_ln_b", "cv_b_pw2",
              "gdn_a_log", "gdn_dt_bias", "gdn_norm_g")
WEIGHTS = ("norm_mix_g", "norm_ffn_g", "final_norm_g", "cv_w_pw1", "cv_b_pw1", "cv_w_dw", "cv_b_dw", "cv_ln_g",
           "cv_ln_b", "cv_w_pw2", "cv_b_pw2", "gdn_w_in", "gdn_conv_w", "gdn_a_log", "gdn_dt_bias", "gdn_norm_g",
           "gdn_w_out", "mlp_w1", "mlp_w2")
MATMUL_SHARDED = COL_SHARDED + ROW_SHARDED


def _squeeze_layer(name, a):
    if name in ("norm_mix_g", "norm_ffn_g", "final_norm_g", "mlp_w1", "mlp_w2"):
        return a
    return a[0]


def _gather_weights(shards):
    me = 4 * lax.axis_index("x") + 2 * lax.axis_index("y") + lax.axis_index("c")
    pw1 = _all_gather_many([shards["cv_w_pw1"].astype(bf16)])[0]
    now = {"cv_w_pw1": pw1}

    def cast(a, tie):
        return (a + tie).astype(bf16)

    token, _ = lax.optimization_barrier((jnp.zeros((), f32), pw1))
    later, started = {}, {}
    for group in ("conv", "mlp0", "gdn", "mlp1"):
        if group == "conv":
            srcs = [cast(shards["cv_w_pw2"], token)] + [shards[n] + token for n in CONV_SHARDED]
        elif group == "gdn":
            srcs = [cast(shards["gdn_w_in"], token).reshape(-1, LANES), cast(shards["gdn_w_out"], token)]
        else:
            layer = int(group[-1])
            srcs = [cast(shards["mlp_w1"][layer], token), cast(shards["mlp_w2"][layer], token)]
        later[group] = srcs
        started[group] = _exchange_start(srcs, [False] * len(srcs), f"weights_{group}_start")
        token = started[group][4][0, 0]

    def need(group, after):
        srcs, lands = _exchange_wait(started[group], after, [False] * len(later[group]), f"weights_{group}_wait")
        lands = [_own_block(ld, own, me) for ld, own in zip(lands, srcs)]
        if group == "conv":
            out = {"cv_w_pw2": _blocks_to_rows(lands[0])}
            out.update({n: _blocks_to_cols(ld) for n, ld in zip(CONV_SHARDED, lands[1:])})
            return out
        if group == "gdn":
            w_in = _blocks_to_cols(lands[0].reshape((N_DEV,) + shards["gdn_w_in"].shape))
            return {"w_in": w_in, "w_out": _blocks_to_rows(lands[1])}
        return {"w1": lands[0], "w2": lands[1]}

    return now, need, token


def kernel(x, norm_mix_g, norm_ffn_g, final_norm_g, cv_w_pw1, cv_b_pw1, cv_w_dw, cv_b_dw, cv_ln_g, cv_ln_b, cv_w_pw2, cv_b_pw2, gdn_w_in, gdn_conv_w, gdn_a_log, gdn_dt_bias, gdn_norm_g, gdn_w_out, mlp_w1, mlp_w2, loss_target, m_norm_mix_g, m_norm_ffn_g, m_final_norm_g, m_cv_w_pw1, m_cv_b_pw1, m_cv_w_dw, m_cv_b_dw, m_cv_ln_g, m_cv_ln_b, m_cv_w_pw2, m_cv_b_pw2, m_gdn_w_in, m_gdn_conv_w, m_gdn_a_log, m_gdn_dt_bias, m_gdn_norm_g, m_gdn_w_out, m_mlp_w1, m_mlp_w2, v_norm_mix_g, v_norm_ffn_g, v_final_norm_g, v_cv_w_pw1, v_cv_b_pw1, v_cv_w_dw, v_cv_b_dw, v_cv_ln_g, v_cv_ln_b, v_cv_w_pw2, v_cv_b_pw2, v_gdn_w_in, v_gdn_conv_w, v_gdn_a_log, v_gdn_dt_bias, v_gdn_norm_g, v_gdn_w_out, v_mlp_w1, v_mlp_w2):
    w_in = dict(zip(WEIGHTS, (norm_mix_g, norm_ffn_g, final_norm_g, cv_w_pw1, cv_b_pw1, cv_w_dw, cv_b_dw, cv_ln_g, cv_ln_b, cv_w_pw2, cv_b_pw2, gdn_w_in, gdn_conv_w, gdn_a_log, gdn_dt_bias, gdn_norm_g, gdn_w_out, mlp_w1, mlp_w2)))
    m_in = dict(zip(WEIGHTS, (m_norm_mix_g, m_norm_ffn_g, m_final_norm_g, m_cv_w_pw1, m_cv_b_pw1, m_cv_w_dw, m_cv_b_dw, m_cv_ln_g, m_cv_ln_b, m_cv_w_pw2, m_cv_b_pw2, m_gdn_w_in, m_gdn_conv_w, m_gdn_a_log, m_gdn_dt_bias, m_gdn_norm_g, m_gdn_w_out, m_mlp_w1, m_mlp_w2)))
    v_in = dict(zip(WEIGHTS, (v_norm_mix_g, v_norm_ffn_g, v_final_norm_g, v_cv_w_pw1, v_cv_b_pw1, v_cv_w_dw, v_cv_b_dw, v_cv_ln_g, v_cv_ln_b, v_cv_w_pw2, v_cv_b_pw2, v_gdn_w_in, v_gdn_conv_w, v_gdn_a_log, v_gdn_dt_bias, v_gdn_norm_g, v_gdn_w_out, v_mlp_w1, v_mlp_w2)))
    me = 4 * lax.axis_index("x") + 2 * lax.axis_index("y") + lax.axis_index("c")

    shards = {n: _squeeze_layer(n, w_in[n]) for n in WEIGHTS}
    first, need, token = _gather_weights(shards)
    params = {n: shards[n] for n in REPLICATED}
    params.update(first)
    params["norm_mix_g"] = params["norm_mix_g"] + token

    def row_blocks(a):
        return a.reshape(N_DEV, a.shape[0] // N_DEV, a.shape[1])

    def flat_blocks(a):
        k, n8 = a.shape
        return _cols_to_blocks(a).reshape(N_DEV, k * (n8 // N_DEV) // LANES, LANES)

    def as_blocks(n, a):
        if n == "gdn_w_in":
            return flat_blocks(a).astype(bf16)
        return a if a.ndim == 3 else row_blocks(a)

    sent = []

    small = REPLICATED + CONV_SHARDED
    small_early = tuple(n for n in small if n != "norm_mix_g") + ("loss",)
    small_info = {}

    def emit(group, grads_out):
        names, blocks, scatter = list(grads_out), [], []
        for n in names:
            if n == "small":
                packed, small_info["sizes"] = _pack([grads_out[n][k] for k in small_early])
                small_info["shapes"] = [grads_out[n][k].shape for k in small_early]
                blocks.append(packed)
                scatter.append(False)
            else:
                blocks.append(as_blocks(n, grads_out[n]))
                scatter.append(True)
        sent.append((names, _exchange_start(blocks, scatter, f"grads_{group}_start"), scatter))
        return sent[-1][1][4][0, 0]

    _, grad_x, gr = _local_step(x, loss_target, params, need, emit)
    nm_shape = gr["norm_mix_g"].shape
    sent.append((["norm_mix_g"], _exchange_start([gr["norm_mix_g"].reshape(-1, LANES)], [False], "grads_norm_mix_start"), [False]))

    recv = {}

    def finish(entry, after):
        names, st, scatter = entry
        srcs, lands = _exchange_wait(st, after, scatter, f"grads_{names[0]}_wait")
        for n, ld, blk, sc in zip(names, lands, srcs, scatter):
            own = lax.dynamic_index_in_dim(blk, me, 0, keepdims=False) if sc else blk
            recv[n] = _own_block(ld, own, me)
        return lands[0]

    def as3d(n, a):
        if n == "gdn_w_in":
            return a.reshape(a.shape[0], -1, LANES)
        return a

    big = [("cv_w_pw2", 0, "cv_w_pw2"), ("gdn_w_in", 0, "gdn_w_in"), ("gdn_w_out", 0, "gdn_w_out"),
           ("mlp_w1", 0, "mlp_w1_0"), ("mlp_w1", 1, "mlp_w1_1"), ("mlp_w2", 0, "mlp_w2_0"), ("mlp_w2", 1, "mlp_w2_1"),
           ("cv_w_pw1", 0, "cv_w_pw1")]
    res = {n: {} for n in MATMUL_SHARDED}
    after = grad_x
    for entry in sent[:-2]:
        after = finish(entry, after)
    for n, layer, key in big[:-1]:
        res[n][layer] = _sum8_adamw(recv[key], as3d(n, w_in[n]), as3d(n, m_in[n]), as3d(n, v_in[n]), layer, f"adamw_{key}")
        after = res[n][layer][0]
    after = finish(sent[-2], after)
    finish(sent[-1], after)
    n, layer, key = big[-1]
    res[n][layer] = _sum8_adamw(recv[key], as3d(n, w_in[n]), as3d(n, m_in[n]), as3d(n, v_in[n]), layer, f"adamw_{key}")

    grads = dict(zip(small_early, _unpack(_sum_devices(recv["small"], "grads_small_sum"), small_info["sizes"], small_info["shapes"])))
    grads["norm_mix_g"] = _sum_devices(recv["norm_mix_g"], "grads_norm_mix_sum").reshape(nm_shape)
    loss = grads["loss"].reshape(())
    for n in CONV_SHARDED:
        cn = shards[n].shape[-1]
        grads[n] = lax.dynamic_slice_in_dim(grads[n], me * cn, cn, axis=1)

    out_groups = {n: [] for n in WEIGHTS}
    for n in MATMUL_SHARDED:
        layers = sorted(res[n])
        for k in range(4):
            pieces = [res[n][layer][k] for layer in layers]
            out_groups[n].append(jnp.stack(pieces).reshape(w_in[n].shape))

    sm_w = [shards[n] for n in small]
    sm_g = [grads[n].reshape(shards[n].shape) for n in small]
    sm_m = [_squeeze_layer(n, m_in[n]) for n in small]
    sm_v = [_squeeze_layer(n, v_in[n]) for n in small]
    wp, psz = _pack(sm_w)
    gp, _ = _pack(sm_g)
    mp, _ = _pack(sm_m)
    vp, _ = _pack(sm_v)
    dp, nmp, nvp = _adamw(wp, gp, mp, vp)
    shp = [a.shape for a in sm_w]
    for n, g, dl, nm, nv in zip(small, sm_g, _unpack(dp, psz, shp), _unpack(nmp, psz, shp), _unpack(nvp, psz, shp)):
        out_groups[n] = [a.reshape(w_in[n].shape) for a in (g, dl, nm, nv)]

    outs = [loss, grad_x]
    for k in range(4):
        outs += [out_groups[n][k] for n in WEIGHTS]
    return tuple(outs)
```

```python
import jax
import jax.numpy as jnp
from jax import lax
from jax.experimental import pallas as pl
from jax.experimental.pallas import tpu as pltpu

f32, bf16 = jnp.float32, jnp.bfloat16

NORM_EPS = 1e-6
L2_EPS = 1e-6
CHUNK = 64
LANES = 128
SUBLANES = 8
N_DEV = 8
VMEM_LIMIT = 56 * 1024 * 1024
ROW_TILE = 512
CONV_PAD = 32
NEG = -1e30

ADAM_LR, ADAM_B1, ADAM_B2, ADAM_EPS, ADAM_WD, ADAM_STEP = 0.001, 0.9, 0.999, 1e-08, 0.01, 10

NT = (((1,), (1,)), ((), ()))
TN = (((0,), (0,)), ((), ()))
HI = lax.Precision.HIGHEST


def _pc(body, *, name, grid, in_specs, out_specs, out_shape, scratch=(), sem=None):
    return pl.pallas_call(
        body, name=name, grid=grid, in_specs=in_specs, out_specs=out_specs, out_shape=out_shape,
        scratch_shapes=list(scratch),
        compiler_params=pltpu.CompilerParams(dimension_semantics=sem, vmem_limit_bytes=VMEM_LIMIT))


def _rows(tm, n):
    return pl.BlockSpec((tm, n), lambda i: (i, 0))


def _const(shape):
    return pl.BlockSpec(shape, lambda *_: (0,) * len(shape))


def _resident(shape):
    return pl.BlockSpec(shape, lambda *_: (0,) * len(shape), pipeline_mode=pl.Buffered(1))


def _tile(t, pref):
    return pref if t % pref == 0 else t


def _dot(a, b):
    return jnp.dot(a.astype(bf16), b.astype(bf16), preferred_element_type=f32)


def _dot_nt(a, b):
    return lax.dot_general(a.astype(bf16), b.astype(bf16), NT, preferred_element_type=f32)


def _dot_tn(a, b):
    return lax.dot_general(a.astype(bf16), b.astype(bf16), TN, preferred_element_type=f32)


def _sigmoid(x):
    return 1.0 / (1.0 + jnp.exp(-x))


def _silu_grad(x):
    s = _sigmoid(x)
    return s * (1.0 + x * (1.0 - s))


def _rms(x, g):
    rstd = lax.rsqrt(jnp.mean(x * x, axis=-1, keepdims=True) + NORM_EPS)
    xh = x * rstd
    return xh * g, xh, rstd


def _rms_bwd(dn, xh, rstd, g):
    dxh = dn * g
    return rstd * (dxh - xh * jnp.mean(dxh * xh, axis=-1, keepdims=True))


def _acc_init(step, *refs):
    @pl.when(step == 0)
    def _():
        for r in refs:
            r[...] = jnp.zeros(r.shape, r.dtype)


def _acc_rows(ref, val):
    ref[0:1, :] += jnp.sum(val, axis=0, keepdims=True)


def _pw1_glu(x, g, w, b):
    t, d = x.shape
    tm = _tile(t, ROW_TILE)

    nb_w = w.shape[0]

    def body(x_ref, g_ref, w_hbm, b_ref, n_ref, u_ref, gl_ref, w_ref, sems):
        _fetch_blocks(pl.program_id(0), w_hbm, w_ref, sems, True)
        n, _, _ = _rms(x_ref[...], g_ref[...])
        nb = n.astype(bf16)
        n_ref[...] = nb
        u = jnp.dot(nb, w_ref[...], preferred_element_type=f32) + b_ref[...]
        u_ref[...] = u.astype(bf16)
        gl_ref[...] = u[:, :d] * _sigmoid(u[:, d:])

    return _pc(body, name="pw1_glu", grid=(t // tm,),
               in_specs=[_rows(tm, d), _const((1, d)), ANY, _const((1, 2 * d))],
               out_specs=[_rows(tm, d), _rows(tm, 2 * d), _rows(tm, d)],
               out_shape=[jax.ShapeDtypeStruct((t, d), bf16), jax.ShapeDtypeStruct((t, 2 * d), bf16),
                          jax.ShapeDtypeStruct((t, d), f32)],
               scratch=[pltpu.VMEM((d, 2 * d), bf16), pltpu.SemaphoreType.DMA((nb_w,))],
               sem=("arbitrary",))(x, g, w, b)


def _ln_silu_mm_res(dc, ln_g, ln_b, w, b, res):
    t, d = dc.shape
    tm = _tile(t, ROW_TILE)

    def body(x_ref, g_ref, bb_ref, w_ref, b_ref, r_ref, s_ref, o_ref):
        x = x_ref[...]
        xc = x - jnp.mean(x, axis=-1, keepdims=True)
        rstd = lax.rsqrt(jnp.mean(xc * xc, axis=-1, keepdims=True) + NORM_EPS)
        ln = xc * rstd * g_ref[...] + bb_ref[...]
        sb = (ln * _sigmoid(ln)).astype(bf16)
        s_ref[...] = sb
        o_ref[...] = r_ref[...] + jnp.dot(sb, w_ref[...], preferred_element_type=f32) + b_ref[...]

    return _pc(body, name="ln_silu_pw2", grid=(t // tm,),
               in_specs=[_rows(tm, d), _const((1, d)), _const((1, d)), _resident((d, d)), _const((1, d)), _rows(tm, d)],
               out_specs=[_rows(tm, d), _rows(tm, d)],
               out_shape=[jax.ShapeDtypeStruct((t, d), bf16), jax.ShapeDtypeStruct((t, d), f32)],
               sem=("parallel",))(dc, ln_g, ln_b, w, b, res)


def _fetch_blocks(step, w_hbm, dst, sems, by_cols, layer=None):
    nb_w = w_hbm.shape[0]
    step_rows, step_cols = w_hbm.shape[-2], w_hbm.shape[-1]

    @pl.when(step == 0)
    def _():
        copies = []
        for j in range(nb_w):
            src = w_hbm.at[j] if layer is None else w_hbm.at[j, layer]
            if by_cols:
                part = dst.at[:, pl.ds(j * step_cols, step_cols)]
            else:
                part = dst.at[pl.ds(j * step_rows, step_rows), :]
            copies.append(pltpu.make_async_copy(src, part, sems.at[j]))
        for cp in copies:
            cp.start()
        for cp in copies:
            cp.wait()


def _mlp_fwd(h, g, w1g, w2g, name, loss=None):
    t, d = h.shape
    nb_w, _, bn = w1g.shape
    ff = nb_w * bn
    tm = _tile(t, 256)
    n_in = 4 if loss is None else 6
    n_out = 4 if loss is None else 6

    def body(*refs):
        h_ref, g_ref, w1_hbm, w2_hbm = refs[:4]
        n_ref, f_ref, r_ref = refs[n_in:n_in + 3]
        w1_ref, w2_ref, sem1, sem2 = refs[n_in + n_out:]
        _fetch_blocks(pl.program_id(0), w1_hbm, w1_ref, sem1, True)
        _fetch_blocks(pl.program_id(0), w2_hbm, w2_ref, sem2, False)
        hv = h_ref[...]
        n, _, _ = _rms(hv, g_ref[...])
        nb = n.astype(bf16)
        n_ref[...] = nb
        f = jnp.dot(nb, w1_ref[...], preferred_element_type=f32)
        f_ref[...] = f.astype(bf16)
        rb = jnp.square(jnp.maximum(f, 0.0)).astype(bf16)
        r_ref[...] = rb
        out = hv + jnp.dot(rb, w2_ref[...], preferred_element_type=f32)
        if loss is None:
            refs[n_in + 3][...] = out
        else:
            gf_ref, t_ref = refs[4:6]
            loss_ref, dh_ref, dg_ref = refs[n_in + 3:n_in + 6]
            _acc_init(pl.program_id(0), loss_ref, dg_ref)
            gv = gf_ref[...]
            y, xh, rstd = _rms(out, gv)
            e = y - t_ref[...]
            loss_ref[...] += 0.5 * jnp.sum(jnp.mean(e * e, axis=-1, keepdims=True))
            dy = e * (1.0 / d)
            _acc_rows(dg_ref, dy * xh)
            dh_ref[...] = _rms_bwd(dy, xh, rstd, gv)

    in_specs = [_rows(tm, d), _const((1, d)), ANY, ANY]
    out_specs = [_rows(tm, d), _rows(tm, ff), _rows(tm, ff)]
    out_shape = [jax.ShapeDtypeStruct((t, d), bf16), jax.ShapeDtypeStruct((t, ff), bf16), jax.ShapeDtypeStruct((t, ff), bf16)]
    args = [h, g, w1g, w2g]
    if loss is None:
        out_specs.append(_rows(tm, d))
        out_shape.append(jax.ShapeDtypeStruct((t, d), f32))
    else:
        in_specs += [_const((1, d)), _rows(tm, d)]
        args += list(loss)
        out_specs += [_const((SUBLANES, LANES)), _rows(tm, d), _const((SUBLANES, d))]
        out_shape += [jax.ShapeDtypeStruct((SUBLANES, LANES), f32), jax.ShapeDtypeStruct((t, d), f32),
                      jax.ShapeDtypeStruct((SUBLANES, d), f32)]
    return _pc(body, name=name, grid=(t // tm,), in_specs=in_specs, out_specs=out_specs, out_shape=out_shape,
               scratch=[pltpu.VMEM((d, ff), bf16), pltpu.VMEM((ff, d), bf16),
                        pltpu.SemaphoreType.DMA((nb_w,)), pltpu.SemaphoreType.DMA((nb_w,))],
               sem=("arbitrary",))(*args)


def _softplus(x):
    return jnp.maximum(x, 0.0) + jnp.log(1.0 + jnp.exp(-jnp.abs(x)))


def _gdn_in(h, g, w_main, w_ab, a_log_pad, dt_pad, n_heads):
    t, d = h.shape
    tm = _tile(t, ROW_TILE)
    gate_rows = 2 * n_heads

    def body(h_ref, g_ref, wm_ref, wab_ref, al_ref, dt_ref, n_ref, qkv_ref, z_ref, ab_ref, gb_ref):
        n, _, _ = _rms(h_ref[...], g_ref[...])
        nb = n.astype(bf16)
        n_ref[...] = nb
        p = jnp.dot(nb, wm_ref[...], preferred_element_type=f32)
        qkv_ref[...] = p[:, :3 * d]
        z_ref[...] = p[:, 3 * d:]
        ab = jnp.dot(nb, wab_ref[...], preferred_element_type=f32)
        ab_ref[...] = ab
        lane = lax.broadcasted_iota(jnp.int32, ab.shape, 1)
        decay = -jnp.exp(al_ref[...]) * _softplus(ab + dt_ref[...])
        gb_ref[...] = jnp.where(lane < n_heads, decay, jnp.where(lane < gate_rows, _sigmoid(ab), 0.0))

    return _pc(body, name="gdn_in", grid=(t // tm,),
               in_specs=[_rows(tm, d), _const((1, d)), _resident((d, 4 * d)), _resident((d, LANES)),
                         _const((1, LANES)), _const((1, LANES))],
               out_specs=[_rows(tm, d), _rows(tm, 3 * d), _rows(tm, d), _rows(tm, LANES), _rows(tm, LANES)],
               out_shape=[jax.ShapeDtypeStruct((t, d), bf16), jax.ShapeDtypeStruct((t, 3 * d), f32),
                          jax.ShapeDtypeStruct((t, d), f32), jax.ShapeDtypeStruct((t, LANES), f32),
                          jax.ShapeDtypeStruct((t, LANES), f32)],
               sem=("parallel",))(h, g, w_main, w_ab, a_log_pad, dt_pad)


def _gated_norm_mm_res(o, z, ng, w, res, n_heads):
    t, d = o.shape
    tm = _tile(t, ROW_TILE)

    def body(o_ref, z_ref, ng_ref, w_ref, r_ref, on_ref, out_ref):
        for hd in range(n_heads):
            sl = slice(hd * LANES, (hd + 1) * LANES)
            rn, _, _ = _rms(o_ref[:, sl], ng_ref[...])
            zz = z_ref[:, sl]
            on_ref[:, sl] = (rn * (zz * _sigmoid(zz))).astype(bf16)
        out_ref[...] = r_ref[...] + jnp.dot(on_ref[...], w_ref[...], preferred_element_type=f32)

    return _pc(body, name="gated_norm_wout", grid=(t // tm,),
               in_specs=[_rows(tm, d), _rows(tm, d), _const((1, LANES)), _resident((d, d)), _rows(tm, d)],
               out_specs=[_rows(tm, d), _rows(tm, d)],
               out_shape=[jax.ShapeDtypeStruct((t, d), bf16), jax.ShapeDtypeStruct((t, d), f32)],
               sem=("parallel",))(o, z, ng, w, res)


def _mlp_bwd(dho, h, g, fb, w1g, w2g, name):
    t, d = h.shape
    nb_w, _, bn = w1g.shape
    ff = nb_w * bn
    tm = _tile(t, 256)

    def body(do_ref, h_ref, g_ref, f_ref, w1_hbm, w2_hbm, df_ref, dh_ref, dg_ref, cs_ref, w1_ref, w2_ref, sem1, sem2):
        _fetch_blocks(pl.program_id(0), w1_hbm, w1_ref, sem1, True)
        _fetch_blocks(pl.program_id(0), w2_hbm, w2_ref, sem2, False)
        _acc_init(pl.program_id(0), dg_ref, cs_ref)
        do = do_ref[...]
        dr = lax.dot_general(do.astype(bf16), w2_ref[...], NT, preferred_element_type=f32)
        dfb = (dr * (2.0 * jnp.maximum(f_ref[...].astype(f32), 0.0))).astype(bf16)
        df_ref[...] = dfb
        dn = lax.dot_general(dfb, w1_ref[...], NT, preferred_element_type=f32)
        gv = g_ref[...]
        _, xh, rstd = _rms(h_ref[...], gv)
        _acc_rows(dg_ref, dn * xh)
        dh = do + _rms_bwd(dn, xh, rstd, gv)
        dh_ref[...] = dh
        _acc_rows(cs_ref, dh)

    return _pc(body, name=name, grid=(t // tm,),
               in_specs=[_rows(tm, d), _rows(tm, d), _const((1, d)), _rows(tm, ff), ANY, ANY],
               out_specs=[_rows(tm, ff), _rows(tm, d), _const((SUBLANES, d)), _const((SUBLANES, d))],
               out_shape=[jax.ShapeDtypeStruct((t, ff), bf16), jax.ShapeDtypeStruct((t, d), f32),
                          jax.ShapeDtypeStruct((SUBLANES, d), f32), jax.ShapeDtypeStruct((SUBLANES, d), f32)],
               scratch=[pltpu.VMEM((d, ff), bf16), pltpu.VMEM((ff, d), bf16),
                        pltpu.SemaphoreType.DMA((nb_w,)), pltpu.SemaphoreType.DMA((nb_w,))],
               sem=("arbitrary",))(dho, h, g, fb, w1g, w2g)


class _Tail:
    def __init__(self, fn, ins, outs):
        self.fn, self.ins, self.outs = fn, ins, outs


def _mm_nt(pairs, name, tail=None):
    t = pairs[0][0].shape[0]
    tm = _tile(t, ROW_TILE)
    npair = len(pairs)
    in_specs, args, scratch, blocked = [], [], [], []
    k = None
    for dy, w in pairs:
        nn = dy.shape[1]
        if isinstance(w, tuple) and isinstance(w[0], str):
            w = w[1]
            k = w.shape[1]
            wspec = ANY
            blocked.append(True)
            scratch += [pltpu.VMEM((k, nn), bf16), pltpu.SemaphoreType.DMA((w.shape[0],))]
        elif isinstance(w, tuple):
            w, idx = w
            k = w.shape[0]
            wspec = pl.BlockSpec((k, nn), lambda *_, idx=idx: (0, idx), pipeline_mode=pl.Buffered(1))
            blocked.append(False)
        else:
            k = w.shape[0]
            wspec = _resident(w.shape)
            blocked.append(False)
        in_specs += [_rows(tm, nn), wspec]
        args += [dy, w]
    n_tin = len(tail.ins) if tail else 0
    n_out = len(tail.outs) if tail else 1
    if tail:
        for arr, kind in tail.ins:
            in_specs.append(_rows(tm, arr.shape[1]) if kind == "rows" else _const(arr.shape))
            args.append(arr)
        out_specs = [_rows(tm, c) if kind == "rows" else _const((SUBLANES, c)) for c, kind in tail.outs]
        out_shape = [jax.ShapeDtypeStruct((t, c) if kind == "rows" else (SUBLANES, c), f32) for c, kind in tail.outs]
    else:
        out_specs = _rows(tm, k)
        out_shape = jax.ShapeDtypeStruct((t, k), f32)

    def body(*refs):
        step = pl.program_id(0)
        tin = refs[2 * npair:2 * npair + n_tin]
        outs = refs[2 * npair + n_tin:2 * npair + n_tin + n_out]
        scr = list(refs[2 * npair + n_tin + n_out:])
        acc = None
        for p in range(npair):
            w_ref = refs[2 * p + 1]
            if blocked[p]:
                w_vmem, sems = scr.pop(0), scr.pop(0)
                _fetch_blocks(step, w_ref, w_vmem, sems, True)
                w_ref = w_vmem
            part = lax.dot_general(refs[2 * p][...].astype(bf16), w_ref[...], NT, preferred_element_type=f32)
            acc = part if acc is None else acc + part
        if tail is None:
            outs[0][...] = acc
        else:
            _acc_init(step, *[o for o, (_, kind) in zip(outs, tail.outs) if kind == "acc"])
            tail.fn(acc, tin, outs)

    sequential = tail is not None or any(blocked)
    return _pc(body, name=name, grid=(t // tm,), in_specs=in_specs, out_specs=out_specs, out_shape=out_shape,
               scratch=scratch, sem=("arbitrary",) if sequential else ("parallel",))(*args)


def _rms_bwd_tail(h, g, dres):
    def fn(dn, ins, outs):
        h_ref, g_ref, dr_ref = ins
        dh_ref, dg_ref = outs
        gv = g_ref[...]
        _, xh, rstd = _rms(h_ref[...], gv)
        _acc_rows(dg_ref, dn * xh)
        dh_ref[...] = dr_ref[...] + _rms_bwd(dn, xh, rstd, gv)

    d = h.shape[1]
    return _Tail(fn, [(h, "rows"), (g, "const"), (dres, "rows")], [(d, "rows"), (d, "acc")])


def _ln_silu_bwd_tail(dc, ln_g, ln_b):
    def fn(ds, ins, outs):
        x_ref, g_ref, b_ref = ins
        dx_ref, dg_ref, db_ref, cs_ref = outs
        x = x_ref[...]
        gv = g_ref[...]
        xc = x - jnp.mean(x, axis=-1, keepdims=True)
        rstd = lax.rsqrt(jnp.mean(xc * xc, axis=-1, keepdims=True) + NORM_EPS)
        xh = xc * rstd
        dln = ds * _silu_grad(xh * gv + b_ref[...])
        _acc_rows(dg_ref, dln * xh)
        _acc_rows(db_ref, dln)
        dxh = dln * gv
        dx = rstd * (dxh - jnp.mean(dxh, axis=-1, keepdims=True) - xh * jnp.mean(dxh * xh, axis=-1, keepdims=True))
        dx_ref[...] = dx
        _acc_rows(cs_ref, dx)

    d = dc.shape[1]
    return _Tail(fn, [(dc, "rows"), (ln_g, "const"), (ln_b, "const")], [(d, "rows"), (d, "acc"), (d, "acc"), (d, "acc")])


def _gated_norm_bwd_tail(o, z, ng, n_heads):
    def fn(don_all, ins, outs):
        o_ref, z_ref, ng_ref = ins
        do_ref, dz_ref, dng_ref = outs
        gv = ng_ref[...]
        for hd in range(n_heads):
            sl = slice(hd * LANES, (hd + 1) * LANES)
            rn, xh, rstd = _rms(o_ref[:, sl], gv)
            zz = z_ref[:, sl]
            don = don_all[:, sl]
            dz_ref[:, sl] = don * rn * _silu_grad(zz)
            drn = don * (zz * _sigmoid(zz))
            _acc_rows(dng_ref, drn * xh)
            do_ref[:, sl] = _rms_bwd(drn, xh, rstd, gv)

    d = o.shape[1]
    return _Tail(fn, [(o, "rows"), (z, "rows"), (ng, "const")], [(d, "rows"), (d, "rows"), (LANES, "acc")])


def _mm_tn_blocked(x, dy, name, out_dtype=f32):
    t, k = x.shape
    bn = dy.shape[1] // N_DEV
    tm = _tile(t, 2048 if k <= 1024 else 1024)
    jb = N_DEV
    while jb > 1 and k * jb * bn * 4 > 8 * 1024 * 1024:
        jb //= 2
    nt = t // tm

    def body(x_ref, dy_ref, o_ref, *acc):
        acc_ref = acc[0] if acc else o_ref
        _acc_init(pl.program_id(1), acc_ref)
        xt = x_ref[...].astype(bf16).T
        for jj in range(jb):
            acc_ref[jj] += jnp.dot(xt, dy_ref[:, jj * bn:(jj + 1) * bn].astype(bf16), preferred_element_type=f32)
        if acc:
            @pl.when(pl.program_id(1) == nt - 1)
            def _():
                o_ref[...] = acc_ref[...].astype(out_dtype)

    return _pc(body, name=name, grid=(N_DEV // jb, nt),
               in_specs=[pl.BlockSpec((tm, k), lambda j, i: (i, 0)), pl.BlockSpec((tm, jb * bn), lambda j, i: (i, j))],
               out_specs=pl.BlockSpec((jb, k, bn), lambda j, i: (j, 0, 0)),
               out_shape=jax.ShapeDtypeStruct((N_DEV, k, bn), out_dtype),
               scratch=[] if out_dtype == f32 else [pltpu.VMEM((jb, k, bn), f32)],
               sem=("parallel", "arbitrary"))(x, dy)


def _mm_tn(x, dy, name, out_dtype=f32):
    t, k = x.shape
    n = dy.shape[1]
    tm = _tile(t, 2048 if k <= 1024 else 1024)
    cap = max(LANES, (2 * 1024 * 1024) // k)
    tn = n
    if n > cap:
        tn = max(c for c in range(LANES, cap + 1, LANES) if n % c == 0)
    nt = t // tm

    def body(x_ref, dy_ref, o_ref, *acc):
        acc_ref = acc[0] if acc else o_ref
        _acc_init(pl.program_id(1), acc_ref)
        acc_ref[...] += lax.dot_general(x_ref[...].astype(bf16), dy_ref[...].astype(bf16), TN, preferred_element_type=f32)
        if acc:
            @pl.when(pl.program_id(1) == nt - 1)
            def _():
                o_ref[...] = acc_ref[...].astype(out_dtype)

    return _pc(body, name=name, grid=(n // tn, nt),
               in_specs=[pl.BlockSpec((tm, k), lambda j, i: (i, 0)), pl.BlockSpec((tm, tn), lambda j, i: (i, j))],
               out_specs=pl.BlockSpec((k, tn), lambda j, i: (0, j)),
               out_shape=jax.ShapeDtypeStruct((k, n), out_dtype),
               scratch=[] if out_dtype == f32 else [pltpu.VMEM((k, tn), f32)],
               sem=("parallel", "arbitrary"))(x, dy)


def _gates_bwd(dgb_t, ab, a_log_pad, dt_pad, n_heads):
    t = ab.shape[0]
    tm = _tile(t, ROW_TILE)
    gate_rows = 2 * n_heads

    def body(dgb_ref, ab_ref, al_ref, dt_ref, dab_ref, dal_ref, ddt_ref):
        _acc_init(pl.program_id(0), dal_ref, ddt_ref)
        ab = ab_ref[...]
        dgb = jnp.concatenate([dgb_ref[...], jnp.zeros((LANES - gate_rows, tm), f32)], axis=0).T
        lane = lax.broadcasted_iota(jnp.int32, ab.shape, 1)
        is_a = lane < n_heads
        is_b = jnp.logical_and(lane >= n_heads, lane < 2 * n_heads)
        xa = ab + dt_ref[...]
        neg_a = -jnp.exp(al_ref[...])
        dg_da = neg_a * _sigmoid(xa)
        beta = _sigmoid(ab)
        da = jnp.where(is_a, dgb * dg_da, 0.0)
        dab_ref[...] = da + jnp.where(is_b, dgb * beta * (1.0 - beta), 0.0)
        _acc_rows(dal_ref, jnp.where(is_a, dgb * neg_a * _softplus(xa), 0.0))
        _acc_rows(ddt_ref, da)

    return _pc(body, name="gates_bwd", grid=(t // tm,),
               in_specs=[pl.BlockSpec((gate_rows, tm), lambda i: (0, i)), _rows(tm, LANES), _const((1, LANES)), _const((1, LANES))],
               out_specs=[_rows(tm, LANES), _const((SUBLANES, LANES)), _const((SUBLANES, LANES))],
               out_shape=[jax.ShapeDtypeStruct((t, LANES), f32), jax.ShapeDtypeStruct((SUBLANES, LANES), f32),
                          jax.ShapeDtypeStruct((SUBLANES, LANES), f32)],
               sem=("arbitrary",))(dgb_t, ab, a_log_pad, dt_pad)


def _glu_bwd(dgl, ub):
    t, d = dgl.shape
    tm = _tile(t, ROW_TILE)

    def body(dgl_ref, u_ref, du_ref, cs_ref):
        _acc_init(pl.program_id(0), cs_ref)
        dgl = dgl_ref[...]
        a = u_ref[:, :d].astype(f32)
        sb = _sigmoid(u_ref[:, d:].astype(f32))
        da = dgl * sb
        db = dgl * a * sb * (1.0 - sb)
        du_ref[:, :d] = da.astype(bf16)
        du_ref[:, d:] = db.astype(bf16)
        cs_ref[0:1, :d] += jnp.sum(da, axis=0, keepdims=True)
        cs_ref[0:1, d:] += jnp.sum(db, axis=0, keepdims=True)

    return _pc(body, name="glu_bwd", grid=(t // tm,),
               in_specs=[_rows(tm, d), _rows(tm, 2 * d)],
               out_specs=[_rows(tm, 2 * d), _const((SUBLANES, 2 * d))],
               out_shape=[jax.ShapeDtypeStruct((t, 2 * d), bf16), jax.ShapeDtypeStruct((SUBLANES, 2 * d), f32)],
               sem=("arbitrary",))(dgl, ub)


def _conv_rows(s):
    return 256 if s % 256 == 0 else s


def _conv_tap_sum(pad_ref, w_ref, base, rows, width):
    acc = jnp.zeros((rows, LANES), f32)
    for j in range(width):
        acc = acc + w_ref[j:j + 1, :] * pad_ref[pl.ds(base + CONV_PAD - (width - 1) + j, rows), :]
    return acc


def _qkv_kinds(j, n_heads, run):
    pl.when(j < n_heads)(lambda: run("q", float(LANES) ** -0.5))
    pl.when(jnp.logical_and(j >= n_heads, j < 2 * n_heads))(lambda: run("k", 1.0))
    pl.when(j >= 2 * n_heads)(lambda: run("v", 1.0))


def _l2_silu_post(c, kind, scale):
    a = c * _sigmoid(c)
    if kind == "v":
        return a, None, a
    r = lax.rsqrt(jnp.sum(a * a, axis=-1, keepdims=True) + L2_EPS)
    return a, r, a * (r * scale)


def _dwconv_fwd(x, w, b, name, qk_heads=None):
    bl, s, cn = x.shape
    width = w.shape[0]
    rows = _conv_rows(s)

    def body(x_ref, w_ref, b_ref, o_ref, pad_ref):
        pad_ref[0:CONV_PAD, :] = jnp.zeros((CONV_PAD, LANES), f32)
        pad_ref[CONV_PAD:, :] = x_ref[0]

        def run(kind, scale):
            def step(i, carry):
                base = pl.multiple_of(i * rows, rows)
                acc = _conv_tap_sum(pad_ref, w_ref, base, rows, width)
                if kind is None:
                    acc = acc + b_ref[...]
                else:
                    _, _, acc = _l2_silu_post(acc, kind, scale)
                o_ref[0, pl.ds(base, rows), :] = acc
                return carry

            lax.fori_loop(0, s // rows, step, 0)

        if qk_heads is None:
            run(None, 1.0)
        else:
            _qkv_kinds(pl.program_id(1), qk_heads, run)

    return _pc(body, name=name, grid=(bl, cn // LANES),
               in_specs=[pl.BlockSpec((1, s, LANES), lambda bi, j: (bi, 0, j)),
                         pl.BlockSpec((width, LANES), lambda bi, j: (0, j)),
                         pl.BlockSpec((1, LANES), lambda bi, j: (0, j))],
               out_specs=pl.BlockSpec((1, s, LANES), lambda bi, j: (bi, 0, j)),
               out_shape=jax.ShapeDtypeStruct((bl, s, cn), f32),
               scratch=[pltpu.VMEM((s + CONV_PAD, LANES), f32)],
               sem=("parallel", "parallel"))(x, w, b)


def _dwconv_bwd(x, dys, w, name, qk_heads=None):
    bl, s, cn = x.shape
    width = w.shape[0]
    wp = -(-width // SUBLANES) * SUBLANES
    rows = _conv_rows(s)
    nblk = s // rows
    ndy = len(dys)

    def body(*refs):
        x_ref, w_ref = refs[0], refs[1]
        dy_refs = refs[2:2 + ndy]
        dx_ref, dw_ref, db_ref, xpad, dypad, acc = refs[2 + ndy:]
        j = pl.program_id(0)
        bi = pl.program_id(1)
        _acc_init(bi, acc, db_ref)
        xpad[0:CONV_PAD, :] = jnp.zeros((CONV_PAD, LANES), f32)
        xpad[CONV_PAD:, :] = x_ref[0]
        dypad[s:, :] = jnp.zeros((CONV_PAD, LANES), f32)
        if qk_heads is None:
            dypad[0:s, :] = dy_refs[0][0]
        else:
            def run(kind, scale):
                dy_ref = dy_refs["qkv".index(kind)]

                def pre(i, carry):
                    base = pl.multiple_of(i * rows, rows)
                    c = _conv_tap_sum(xpad, w_ref, base, rows, width)
                    a, r, _ = _l2_silu_post(c, kind, scale)
                    da = dy_ref[0, pl.ds(base, rows), :]
                    if kind != "v":
                        dy = da * scale
                        da = r * (dy - a * (r * r) * jnp.sum(a * dy, axis=-1, keepdims=True))
                    dypad[pl.ds(base, rows), :] = da * _silu_grad(c)
                    return carry

                lax.fori_loop(0, nblk, pre, 0)

            _qkv_kinds(j, qk_heads, run)

        def step(i, carry):
            base = pl.multiple_of(i * rows, rows)
            dxa = jnp.zeros((rows, LANES), f32)
            for jj in range(width):
                dxa = dxa + w_ref[jj:jj + 1, :] * dypad[pl.ds(base + (width - 1) - jj, rows), :]
            dx_ref[0, pl.ds(base, rows), :] = dxa
            dyc = dypad[pl.ds(base, rows), :]
            db_ref[...] += dyc.reshape(rows // SUBLANES, SUBLANES, LANES).sum(axis=0)
            for jj in range(width):
                prod = dyc * xpad[pl.ds(base + CONV_PAD - (width - 1) + jj, rows), :]
                acc[jj * SUBLANES:(jj + 1) * SUBLANES, :] += prod.reshape(rows // SUBLANES, SUBLANES, LANES).sum(axis=0)
            return carry

        lax.fori_loop(0, nblk, step, 0)

        @pl.when(bi == bl - 1)
        def _():
            dw_ref[...] = jnp.zeros((wp, LANES), f32)
            for jj in range(width):
                dw_ref[jj:jj + 1, :] = jnp.sum(acc[jj * SUBLANES:(jj + 1) * SUBLANES, :], axis=0, keepdims=True)

    if qk_heads is None:
        dy_specs = [pl.BlockSpec((1, s, LANES), lambda j, bi: (bi, 0, j))]
    else:
        hh = qk_heads
        def dy_spec(part):
            def index(j, bi):
                mine = jnp.logical_and(j >= part * hh, j < (part + 1) * hh)
                return (jnp.where(mine, bi * hh + j - part * hh, 0), 0, 0)
            return pl.BlockSpec((1, s, LANES), index)

        dy_specs = [dy_spec(0), dy_spec(1), dy_spec(2)]
    return _pc(body, name=name, grid=(cn // LANES, bl),
               in_specs=[pl.BlockSpec((1, s, LANES), lambda j, bi: (bi, 0, j)),
                         pl.BlockSpec((width, LANES), lambda j, bi: (0, j))] + dy_specs,
               out_specs=[pl.BlockSpec((1, s, LANES), lambda j, bi: (bi, 0, j)),
                          pl.BlockSpec((wp, LANES), lambda j, bi: (0, j)),
                          pl.BlockSpec((SUBLANES, LANES), lambda j, bi: (0, j))],
               out_shape=[jax.ShapeDtypeStruct((bl, s, cn), f32), jax.ShapeDtypeStruct((wp, cn), f32),
                          jax.ShapeDtypeStruct((SUBLANES, cn), f32)],
               scratch=[pltpu.VMEM((s + CONV_PAD, LANES), f32), pltpu.VMEM((s + CONV_PAD, LANES), f32),
                        pltpu.VMEM((width * SUBLANES, LANES), f32)],
               sem=("parallel", "arbitrary"))(x, w, *dys)


def _group_rows(s):
    for rows in (256, 128):
        if s % rows == 0:
            return rows
    return CHUNK


def _group_masks(rows):
    r = lax.broadcasted_iota(jnp.int32, (rows, rows), 0)
    c = lax.broadcasted_iota(jnp.int32, (rows, rows), 1)
    return r, c, r >= c


def _decay(gc_col, gc_row, causal):
    return jnp.exp(jnp.where(causal, gc_col - gc_row, NEG))


def _inv_unit_lower_many(mats, r, c):
    n = r.shape[0]
    eye = (r == c).astype(f32)
    same16 = (r >> 4) == (c >> 4)
    ads = [jnp.where(same16, a, 0.0) for a in mats]
    aos = [a - ad for a, ad in zip(mats, ads)]
    xs = ads
    tds = [-x for x in xs]
    for _ in range(3):
        xs = [_dot(x, x) for x in xs]
        tds = [td + x + _dot(td, x) for td, x in zip(tds, xs)]
    bs = [ao + _dot(td, ao) for td, ao in zip(tds, aos)]
    ps = [-b for b in bs]
    span = 2
    while span < n // 16:
        bs = [_dot(b, b) for b in bs]
        ps = [p + b + _dot(p, b) for p, b in zip(ps, bs)]
        span *= 2
    return [p + td + _dot(p, td) for p, td in zip(ps, tds)]


def _lane_cumsum(x, y, reverse, name):
    rr, n = x.shape

    def body(x_ref, y_ref, o_ref):
        i = lax.broadcasted_iota(jnp.int32, (n, n), 0)
        j = lax.broadcasted_iota(jnp.int32, (n, n), 1)
        tri = ((i >= j) if reverse else (i <= j)).astype(f32)
        o_ref[...] = jnp.dot(x_ref[...] + y_ref[...], tri, precision=HI, preferred_element_type=f32)

    spec = pl.BlockSpec((rr, n), lambda: (0, 0))
    return pl.pallas_call(body, name=name, in_specs=[spec, spec], out_specs=spec,
                          out_shape=jax.ShapeDtypeStruct((rr, n), f32))(x, y)


def _gdn_specs(n_heads, nblk, rows, hp=1, rev=False):
    def blk(n):
        return nblk - 1 - n if rev else n

    def qkv(off):
        return pl.BlockSpec((rows, hp * LANES), lambda g, n: (
            lax.div(g * hp, n_heads) * nblk + blk(n), lax.div(off * n_heads + lax.rem(g * hp, n_heads), hp)))

    def per_head(last, block_rows=rows):
        return pl.BlockSpec((hp, block_rows, last), lambda g, n: (g, blk(n), 0))

    def row_vec():
        return pl.BlockSpec((hp, 1, 1, rows), lambda g, n: (g, blk(n), 0, 0))

    return qkv, per_head, row_vec


def _gate_cumsum(gbeta, bl, s, n_heads):
    rows = _group_rows(s)
    nblk = s // rows

    def body(g_ref, cols_ref, gct_ref):
        i = lax.broadcasted_iota(jnp.int32, (rows, rows), 0)
        j = lax.broadcasted_iota(jnp.int32, (rows, rows), 1)
        blk = g_ref[...]
        summed = jnp.dot((i >= j).astype(f32), blk, precision=HI, preferred_element_type=f32)
        lane = lax.broadcasted_iota(jnp.int32, blk.shape, 1)
        cols = jnp.where(lane < n_heads, summed, blk)
        cols_ref[...] = cols
        gct_ref[...] = cols.T[:n_heads, :]

    return _pc(body, name="gdn_gate_cumsum", grid=(bl, nblk),
               in_specs=[pl.BlockSpec((rows, LANES), lambda b, n: (b * nblk + n, 0))],
               out_specs=[pl.BlockSpec((rows, LANES), lambda b, n: (b * nblk + n, 0)),
                          pl.BlockSpec((None, n_heads, rows), lambda b, n: (b, 0, n))],
               out_shape=[jax.ShapeDtypeStruct(gbeta.shape, f32), jax.ShapeDtypeStruct((bl, n_heads, s), f32)],
               sem=("parallel", "parallel"))(gbeta)


def _gate_cols_spec(nblk, rows, rev=False):
    return pl.BlockSpec((rows, LANES), lambda g, n: (g * nblk + (nblk - 1 - n if rev else n), 0))


def _gdn_prep(qkv, cols, grow, bl, s, n_heads):
    rows = _group_rows(s)
    nblk = s // rows
    bh_n = bl * n_heads
    hp = n_heads
    qkv_spec, ph, rv = _gdn_specs(n_heads, nblk, rows, hp)

    def body(k_ref, v_ref, cols_ref, grow_ref, u_ref, w_ref, t_ref):
        r, c, causal = _group_masks(rows)
        mats, rhs = [], []
        for h in range(hp):
            hs = slice(h * LANES, (h + 1) * LANES)
            k = k_ref[:, hs]
            gc = cols_ref[:, h:h + 1]
            beta = cols_ref[:, hp + h:hp + h + 1]
            kb = k * beta
            mats.append(jnp.where(r > c, _dot_nt(kb, k) * _decay(gc, grow_ref[h, 0], causal), 0.0))
            rhs.append((v_ref[:, hs] * beta, kb * jnp.exp(gc)))
        for h, tm in enumerate(_inv_unit_lower_many(mats, r, c)):
            tb = tm.astype(bf16)
            u_ref[h] = rhs[h][0] + jnp.dot(tb, rhs[h][0].astype(bf16), preferred_element_type=f32)
            w_ref[h] = (rhs[h][1] + jnp.dot(tb, rhs[h][1].astype(bf16), preferred_element_type=f32)).astype(bf16)
            t_ref[h] = tb

    return _pc(body, name="gdn_prep", grid=(bh_n // hp, nblk),
               in_specs=[qkv_spec(1), qkv_spec(2), _gate_cols_spec(nblk, rows), rv()],
               out_specs=[ph(LANES), ph(LANES), ph(rows)],
               out_shape=[jax.ShapeDtypeStruct((bh_n, s, LANES), f32), jax.ShapeDtypeStruct((bh_n, s, LANES), bf16),
                          jax.ShapeDtypeStruct((bh_n, s, rows), bf16)],
               sem=("parallel", "parallel"))(qkv, qkv, cols, grow)


def _gdn_scan(qkv, u, w, cols, grow, bl, s, n_heads):
    rows = _group_rows(s)
    nblk = s // rows
    bh_n = bl * n_heads
    d = n_heads * LANES
    hp = n_heads
    qkv_spec, ph, rv = _gdn_specs(n_heads, nblk, rows, hp)

    def body(q_ref, k_ref, u_ref, w_ref, cols_ref, grow_ref, o_ref, vn_ref, ss_ref, s_scr):
        _acc_init(pl.program_id(1), s_scr)
        _, _, causal = _group_masks(rows)
        hh = range(hp)
        qs = [q_ref[:, h * LANES:(h + 1) * LANES] for h in hh]
        ks = [k_ref[:, h * LANES:(h + 1) * LANES] for h in hh]
        gcs = [cols_ref[:, h:h + 1] for h in hh]
        ps = [_dot_nt(qs[h], ks[h]) * _decay(gcs[h], grow_ref[h, 0], causal) for h in hh]
        sts = [s_scr[h] for h in hh]
        for h in hh:
            ss_ref[h] = sts[h].astype(bf16)
        vns = [u_ref[h] - _dot(w_ref[h], sts[h]) for h in hh]
        for h in hh:
            vn_ref[h] = vns[h].astype(bf16)
        o_state = [_dot(qs[h] * jnp.exp(gcs[h]), sts[h]) for h in hh]
        o_intra = [_dot(ps[h], vns[h]) for h in hh]
        for h in hh:
            o_ref[:, h * LANES:(h + 1) * LANES] = o_state[h] + o_intra[h]
        for h in hh:
            g_last = gcs[h][rows - 1:rows, :]
            s_scr[h] = jnp.exp(g_last) * sts[h] + _dot_tn(ks[h] * jnp.exp(g_last - gcs[h]), vns[h])

    return _pc(body, name="gdn_scan", grid=(bh_n // hp, nblk),
               in_specs=[qkv_spec(0), qkv_spec(1), ph(LANES), ph(LANES), _gate_cols_spec(nblk, rows), rv()],
               out_specs=[qkv_spec(0), ph(LANES), ph(LANES, block_rows=LANES)],
               out_shape=[jax.ShapeDtypeStruct((bl * s, d), f32), jax.ShapeDtypeStruct((bh_n, s, LANES), bf16),
                          jax.ShapeDtypeStruct((bh_n, nblk * LANES, LANES), bf16)],
               scratch=[pltpu.VMEM((hp, LANES, LANES), f32)],
               sem=("parallel", "arbitrary"))(qkv, qkv, u, w, cols, grow)


def _gdn_scan_bwd(do, qkv, w, vn, cols, grow, ss, bl, s, n_heads):
    rows = _group_rows(s)
    nblk = s // rows
    bh_n = bl * n_heads
    hp = n_heads
    qkv_spec, ph, rv = _gdn_specs(n_heads, nblk, rows, hp, rev=True)

    def body(do_ref, q_ref, k_ref, w_ref, vn_ref, cols_ref, grow_ref, ss_ref,
             du_ref, dw_ref, dq_ref, dk_ref, dcol_ref, drow_ref, ds_scr):
        _acc_init(pl.program_id(1), ds_scr)
        _, _, causal = _group_masks(rows)
        last_row = lax.broadcasted_iota(jnp.int32, (rows, 1), 0) == rows - 1
        hh = range(hp)
        dos = [do_ref[:, h * LANES:(h + 1) * LANES] for h in hh]
        qs = [q_ref[:, h * LANES:(h + 1) * LANES] for h in hh]
        ks = [k_ref[:, h * LANES:(h + 1) * LANES] for h in hh]
        vns = [vn_ref[h] for h in hh]
        gcs = [cols_ref[:, h:h + 1] for h in hh]
        sts = [ss_ref[h] for h in hh]
        dss = [ds_scr[h] for h in hh]
        dmats = [_decay(gcs[h], grow_ref[h, 0], causal) for h in hh]
        gams = [jnp.exp(gc) for gc in gcs]
        qgs = [qs[h] * gams[h] for h in hh]
        g_lasts = [gc[rows - 1:rows, :] for gc in gcs]
        kd_scales = [jnp.exp(g_lasts[h] - gcs[h]) for h in hh]
        kdecs = [ks[h] * kd_scales[h] for h in hh]
        qks = [_dot_nt(qs[h], ks[h]) for h in hh]
        dpds = [_dot_nt(dos[h], vns[h]) * dmats[h] for h in hh]
        dvns = [_dot_tn(qks[h] * dmats[h], dos[h]) + _dot(kdecs[h], dss[h]) for h in hh]
        for h in hh:
            du_ref[h] = dvns[h]
        dkdecs = [_dot_nt(vns[h], dss[h]) for h in hh]
        for h in hh:
            dw_ref[h] = -_dot_nt(dvns[h], sts[h])
        dqgs = [_dot_nt(dos[h], sts[h]) for h in hh]
        dq_intra = [_dot(dpds[h], ks[h]) for h in hh]
        dk_intra = [_dot_tn(dpds[h], qs[h]) for h in hh]
        for h in hh:
            dq_ref[h] = dqgs[h] * gams[h] + dq_intra[h]
            dk_ref[h] = dk_intra[h] + dkdecs[h] * kd_scales[h]
            ep = dpds[h] * qks[h]
            drow_ref[h, 0] = -jnp.sum(ep, axis=0, keepdims=True)
            kd_rows = jnp.sum(dkdecs[h] * kdecs[h], axis=-1, keepdims=True)
            extra = jnp.sum(kd_rows) + jnp.exp(g_lasts[h]) * jnp.sum(sts[h] * dss[h])
            dcol_ref[h] = (jnp.sum(dqgs[h] * qgs[h], axis=-1, keepdims=True) + jnp.sum(ep, axis=-1, keepdims=True)
                           - kd_rows + jnp.where(last_row, extra, 0.0))
        ds_new = [jnp.exp(g_lasts[h]) * dss[h] + _dot_tn(qgs[h], dos[h]) - _dot_tn(w_ref[h], dvns[h]) for h in hh]
        for h in hh:
            ds_scr[h] = ds_new[h]

    return _pc(body, name="gdn_scan_bwd", grid=(bh_n // hp, nblk),
               in_specs=[qkv_spec(0), qkv_spec(0), qkv_spec(1), ph(LANES), ph(LANES), _gate_cols_spec(nblk, rows, rev=True), rv(),
                         ph(LANES, block_rows=LANES)],
               out_specs=[ph(LANES), ph(LANES), ph(LANES), ph(LANES), ph(1), rv()],
               out_shape=[jax.ShapeDtypeStruct((bh_n, s, LANES), f32)] * 4
               + [jax.ShapeDtypeStruct((bh_n, s, 1), f32), jax.ShapeDtypeStruct((bh_n, nblk, 1, rows), f32)],
               scratch=[pltpu.VMEM((hp, LANES, LANES), f32)],
               sem=("parallel", "arbitrary"))(do, qkv, qkv, w, vn, cols, grow, ss)


def _gdn_prep_bwd(qkv, cols, grow, tmat, du, dw, dk_scan, dcol_scan, drow_scan, bl, s, n_heads):
    rows = _group_rows(s)
    nblk = s // rows
    bh_n = bl * n_heads
    hp = n_heads
    qkv_spec, ph, rv = _gdn_specs(n_heads, nblk, rows, hp)

    def body(k_ref, v_ref, cols_ref, grow_ref, t_ref, du_ref, dw_ref, dks_ref, dcs_ref, drs_ref,
             dk_ref, dv_ref, dcols_ref, drow_ref):
        r, c, causal = _group_masks(rows)
        hh = range(hp)
        ks = [k_ref[:, h * LANES:(h + 1) * LANES] for h in hh]
        vs = [v_ref[:, h * LANES:(h + 1) * LANES] for h in hh]
        gcs = [cols_ref[:, h:h + 1] for h in hh]
        betas = [cols_ref[:, hp + h:hp + h + 1] for h in hh]
        tms = [t_ref[h] for h in hh]
        dus = [du_ref[h] for h in hh]
        dws = [dw_ref[h] for h in hh]
        gams = [jnp.exp(gc) for gc in gcs]
        kbs = [k * b for k, b in zip(ks, betas)]
        kbgs = [kb * g for kb, g in zip(kbs, gams)]
        dts = [_dot_nt(dus[h], vs[h] * betas[h]) + _dot_nt(dws[h], kbgs[h]) for h in hh]
        dvbs = [dus[h] + _dot_tn(tms[h], dus[h]) for h in hh]
        dkbgs = [dws[h] + _dot_tn(tms[h], dws[h]) for h in hh]
        kks = [_dot_nt(kbs[h], ks[h]) for h in hh]
        inner = [dts[h] + _dot_nt(dts[h], tms[h]) for h in hh]
        dads = [jnp.where(r > c, -(inner[h] + _dot_tn(tms[h], inner[h])), 0.0) * _decay(gcs[h], grow_ref[h, 0], causal)
                for h in hh]
        dkbs = [dkbgs[h] * gams[h] + _dot(dads[h], ks[h]) for h in hh]
        dk2 = [_dot_tn(dads[h], kbs[h]) for h in hh]
        for h in hh:
            dk_ref[h] = dks_ref[h] + dk2[h] + dkbs[h] * betas[h]
            dv_ref[h] = dvbs[h] * betas[h]
            ea = dads[h] * kks[h]
            dcols_ref[h, :, 0:1] = (dcs_ref[h] + jnp.sum(dkbgs[h] * kbgs[h], axis=-1, keepdims=True)
                                    + jnp.sum(ea, axis=-1, keepdims=True))
            dcols_ref[h, :, 1:2] = (jnp.sum(dvbs[h] * vs[h], axis=-1, keepdims=True)
                                    + jnp.sum(dkbs[h] * ks[h], axis=-1, keepdims=True))
            drow_ref[h, 0] = drs_ref[h, 0] - jnp.sum(ea, axis=0, keepdims=True)

    return _pc(body, name="gdn_prep_bwd", grid=(bh_n // hp, nblk),
               in_specs=[qkv_spec(1), qkv_spec(2), _gate_cols_spec(nblk, rows), rv(), ph(rows), ph(LANES), ph(LANES), ph(LANES), ph(1), rv()],
               out_specs=[ph(LANES), ph(LANES), ph(2), rv()],
               out_shape=[jax.ShapeDtypeStruct((bh_n, s, LANES), f32), jax.ShapeDtypeStruct((bh_n, s, LANES), f32),
                          jax.ShapeDtypeStruct((bh_n, s, 2), f32), jax.ShapeDtypeStruct((bh_n, nblk, 1, rows), f32)],
               sem=("parallel", "parallel"))(qkv, qkv, cols, grow, tmat, du, dw, dk_scan, dcol_scan, drow_scan)


def _row(v):
    return v.reshape(1, -1).astype(f32)


def _pad_lanes(v):
    v = v.reshape(1, -1).astype(f32)
    return jnp.pad(v, ((0, 0), (0, LANES - v.shape[1])))


def _local_step(x, tgt, p, need, emit):
    bl, s, d = x.shape
    t = bl * s
    n_heads = p["gdn_a_log"].shape[-1]
    assert d == n_heads * LANES and s % CHUNK == 0
    x2 = x.reshape(t, d)
    tgt2 = tgt.reshape(t, d)
    gr = {}

    n0, ub, gl = _pw1_glu(x2, _row(p["norm_mix_g"][0]), p["cv_w_pw1"], _row(p["cv_b_pw1"]))
    cv = need("conv", n0)
    dc = _dwconv_fwd(gl.reshape(bl, s, d), cv["cv_w_dw"], _row(p["cv_b_dw"]), "dwconv_fwd").reshape(t, d)
    sb, h1 = _ln_silu_mm_res(dc, _row(p["cv_ln_g"]), _row(p["cv_ln_b"]), cv["cv_w_pw2"], _row(p["cv_b_pw2"]), x2)
    m0 = need("mlp0", h1)
    n1, f0, r0, h2 = _mlp_fwd(h1, _row(p["norm_ffn_g"][0]), m0["w1"], m0["w2"], "mlp_fwd0")

    gd = need("gdn", h2)
    w_in = gd["w_in"]
    w_ab = jnp.pad(w_in[:, 4 * d:], ((0, 0), (0, LANES - 2 * n_heads)))
    a_log_pad = _pad_lanes(p["gdn_a_log"])
    dt_pad = _pad_lanes(p["gdn_dt_bias"])
    n2, qkv_pre, z, ab, gbeta = _gdn_in(h2, _row(p["norm_mix_g"][1]), w_in, w_ab, a_log_pad, dt_pad, n_heads)
    zero_bias = jnp.zeros((1, 3 * d), f32)
    qkv = _dwconv_fwd(qkv_pre.reshape(bl, s, 3 * d), cv["gdn_conv_w"], zero_bias, "sconv_fwd", qk_heads=n_heads).reshape(t, 3 * d)
    bh_n, rows = bl * n_heads, _group_rows(s)
    cols, gc_t = _gate_cumsum(gbeta, bl, s, n_heads)
    grow = gc_t.reshape(bh_n, s // rows, 1, rows)
    u, w, tmat = _gdn_prep(qkv, cols, grow, bl, s, n_heads)
    o, vn, ss = _gdn_scan(qkv, u, w, cols, grow, bl, s, n_heads)
    onb, h3 = _gated_norm_mm_res(o, z, _row(p["gdn_norm_g"]), gd["w_out"], h2, n_heads)
    m1 = need("mlp1", h3)
    n3, f1, r1, loss_acc, dh4, dgf = _mlp_fwd(h3, _row(p["norm_ffn_g"][1]), m1["w1"], m1["w2"], "mlp_fwd1_loss",
                                              loss=(_row(p["final_norm_g"]), tgt2))
    loss = loss_acc[0, 0]
    gr["loss"] = loss_acc[0, 0:1]
    gr["final_norm_g"] = dgf[0]

    df1, dh3, dg_ffn1, _ = _mlp_bwd(dh4, h3, _row(p["norm_ffn_g"][1]), f1, m1["w1"], m1["w2"], "mlp_bwd1")
    dw2_1 = _mm_tn(r1, dh4, "dw_mlp2_1", bf16)
    dw1_1 = _mm_tn_blocked(n3, df1, "dw_mlp1_1", bf16)

    dw_out = _mm_tn(onb, dh3, "dw_gdn_out", bf16)
    tie = emit("late", {"mlp_w2_1": dw2_1, "mlp_w1_1": dw1_1, "gdn_w_out": dw_out})
    do, dz, dng = _mm_nt([(dh3, gd["w_out"])], "dx_gdn_out",
                         _gated_norm_bwd_tail(o, z, _row(p["gdn_norm_g"]) + tie, n_heads))
    gr["gdn_norm_g"] = dng[0]
    du, dw_, dq, dk_scan, dcol_scan, drow_scan = _gdn_scan_bwd(do, qkv, w, vn, cols, grow, ss, bl, s, n_heads)
    dk, dv, dcols, drow = _gdn_prep_bwd(qkv, cols, grow, tmat, du, dw_, dk_scan, dcol_scan, drow_scan, bl, s, n_heads)
    lanes_shape = (bh_n * (s // rows), rows)
    dg_lanes = _lane_cumsum(dcols[..., 0].reshape(lanes_shape), drow.reshape(lanes_shape), True, "gdn_gate_cumsum_bwd")
    dqkv_pre, dconv_w, _ = _dwconv_bwd(qkv_pre.reshape(bl, s, 3 * d), [dq, dk, dv], cv["gdn_conv_w"], "sconv_bwd", qk_heads=n_heads)
    gr["gdn_conv_w"] = dconv_w[:cv["gdn_conv_w"].shape[0]]
    dgb_t = jnp.stack([dg_lanes.reshape(bl, n_heads, s), dcols[..., 1].reshape(bl, n_heads, s)])
    dgb_t = dgb_t.transpose(0, 2, 1, 3).reshape(2 * n_heads, t)
    dab, dal, ddt = _gates_bwd(dgb_t, ab, a_log_pad, dt_pad, n_heads)
    gr["gdn_a_log"] = dal[0, :n_heads]
    gr["gdn_dt_bias"] = ddt[0, :n_heads]
    dqkv2 = dqkv_pre.reshape(t, 3 * d)
    dw_in = jnp.concatenate(
        [_mm_tn(n2, dqkv2, "dw_gdn_in_qkv"), _mm_tn(n2, dz, "dw_gdn_in_z"), _mm_tn(n2, dab, "dw_gdn_in_ab")[:, :2 * n_heads]], axis=1)
    tie = emit("gdn_in", {"gdn_w_in": dw_in})
    dh2, dg_mix1 = _mm_nt([(dqkv2, (w_in, 0)), (dz, (w_in, 3)), (dab, w_ab)], "dx_gdn_in",
                          _rms_bwd_tail(h2, _row(p["norm_mix_g"][1]) + tie, dh3))

    df0, dh1, dg_ffn0, cs_h1 = _mlp_bwd(dh2, h1, _row(p["norm_ffn_g"][0]), f0, m0["w1"], m0["w2"], "mlp_bwd0")
    dw2_0 = _mm_tn(r0, dh2, "dw_mlp2_0", bf16)
    dw1_0 = _mm_tn_blocked(n1, df0, "dw_mlp1_0", bf16)
    dw_pw2 = _mm_tn(sb, dh1, "dw_pw2", bf16)
    tie = emit("mlp0", {"mlp_w2_0": dw2_0, "mlp_w1_0": dw1_0, "cv_w_pw2": dw_pw2})
    gr["norm_ffn_g"] = jnp.stack([dg_ffn0[0], dg_ffn1[0]])

    gr["cv_b_pw2"] = cs_h1[0]
    ddc, dlng, dlnb, cs_dc = _mm_nt([(dh1, cv["cv_w_pw2"])], "dx_pw2",
                                    _ln_silu_bwd_tail(dc, _row(p["cv_ln_g"]) + tie, _row(p["cv_ln_b"])))
    gr["cv_ln_g"] = dlng[0]
    gr["cv_ln_b"] = dlnb[0]
    gr["cv_b_dw"] = cs_dc[0]
    dgl, dw_dw, _ = _dwconv_bwd(gl.reshape(bl, s, d), [ddc.reshape(bl, s, d)], cv["cv_w_dw"], "dwconv_bwd")
    gr["cv_w_dw"] = dw_dw[:cv["cv_w_dw"].shape[0]]
    dub, cs_u = _glu_bwd(dgl.reshape(t, d), ub)
    gr["cv_b_pw1"] = cs_u[0]
    dw_pw1 = _mm_tn_blocked(n0, dub, "dw_pw1", bf16)
    tie = emit("last", {"cv_w_pw1": dw_pw1, "small": gr})
    dx, dg_mix0 = _mm_nt([(dub, ("blocks", p["cv_w_pw1"]))], "dx_pw1",
                         _rms_bwd_tail(x2, _row(p["norm_mix_g"][0]) + tie, dh1))
    return loss, dx.reshape(bl, s, d), {"norm_mix_g": jnp.stack([dg_mix0[0], dg_mix1[0]])}


ANY = pl.BlockSpec(memory_space=pl.ANY)
MESH = pl.DeviceIdType.MESH


def _flip(v, bit):
    return 1 - v if bit else v


def _all_gather_many(shards):
    na = len(shards)

    def body(*refs):
        x_refs, o_refs = refs[:na], refs[na:2 * na]
        send_sems, recv_sems, local_sems = refs[2 * na:]
        x, y, c = lax.axis_index("x"), lax.axis_index("y"), lax.axis_index("c")
        me, sibling = (x, y, c), (x, y, 1 - c)
        chips = [(1 - x, y), (x, 1 - y), (1 - x, 1 - y)]

        def copy(a, k, block, to, src=None):
            px, py, pc = block
            dst = o_refs[a].at[4 * px + 2 * py + pc]
            return pltpu.make_async_remote_copy(
                src_ref=dst if src is None else src, dst_ref=dst,
                send_sem=send_sems.at[a, k], recv_sem=recv_sems.at[a, k], device_id=to, device_id_type=MESH)

        mine = [pltpu.make_async_copy(x_refs[a], o_refs[a].at[4 * x + 2 * y + c], local_sems.at[a]) for a in range(na)]
        first = []
        for a in range(na):
            first.append(copy(a, 0, me, sibling, src=x_refs[a]))
            first += [copy(a, 1 + j, me, (*chip, c), src=x_refs[a]) for j, chip in enumerate(chips)]
        for cp in mine + first:
            cp.start()
        passed = []
        for j, chip in enumerate(chips):
            for a in range(na):
                copy(a, 1 + j, (*chip, c), me).wait_recv()
                fwd = copy(a, 4 + j, (*chip, c), sibling)
                fwd.start()
                passed.append(fwd)
        for a in range(na):
            copy(a, 0, sibling, me).wait_recv()
        for j, chip in enumerate(chips):
            for a in range(na):
                copy(a, 4 + j, (*chip, 1 - c), me).wait_recv()
        for cp in first + passed:
            cp.wait_send()
        for cp in mine:
            cp.wait()

    return pl.pallas_call(
        body, name="weights_all_gather",
        out_shape=[jax.ShapeDtypeStruct((N_DEV,) + a.shape, a.dtype) for a in shards],
        in_specs=[ANY] * na, out_specs=[ANY] * na,
        scratch_shapes=[pltpu.SemaphoreType.DMA((na, 7)), pltpu.SemaphoreType.DMA((na, 7)), pltpu.SemaphoreType.DMA((na,))],
        compiler_params=pltpu.CompilerParams(has_side_effects=True),
    )(*shards)


HBM = pl.BlockSpec(memory_space=pltpu.HBM)
SEM = pl.BlockSpec(memory_space=pltpu.SEMAPHORE)
EFFECT = pltpu.SideEffectType.DATAFLOW_SIDE_EFFECTING
N_PEERS = N_DEV - 1


def _exchange_copies(src_refs, land_refs, send_sems, recv_sems, scatter):
    x, y, c = lax.axis_index("x"), lax.axis_index("y"), lax.axis_index("c")
    me = 4 * x + 2 * y + c
    copies = []
    for a, (src, land) in enumerate(zip(src_refs, land_refs)):
        for k in range(1, N_DEV):
            px, py, pc = _flip(x, k & 4), _flip(y, k & 2), _flip(c, k & 1)
            i = a * N_PEERS + k - 1
            copies.append(pltpu.make_async_remote_copy(
                src_ref=src.at[4 * px + 2 * py + pc] if scatter[a] else src, dst_ref=land.at[me],
                send_sem=send_sems.at[i], recv_sem=recv_sems.at[i], device_id=(px, py, pc), device_id_type=MESH))
    return copies


def _exchange_start(srcs, scatter, name):
    na = len(srcs)
    lands = [lax.empty(s.shape if sc else (N_DEV,) + s.shape, s.dtype) for s, sc in zip(srcs, scatter)]

    def body(*refs):
        copies = _exchange_copies(refs[:na], refs[na:2 * na], refs[2 * na], refs[2 * na + 1], scatter)
        for cp in copies:
            cp.start()
        token = refs[-1]
        token[...] = jnp.zeros_like(token)

    outs = pl.pallas_call(
        body, name=name,
        out_shape=(pltpu.SemaphoreType.DMA((na * N_PEERS,)), pltpu.SemaphoreType.DMA((na * N_PEERS,)))
        + tuple(pltpu.HBM(a.shape, a.dtype) for a in srcs + lands) + (jax.ShapeDtypeStruct((SUBLANES, LANES), f32),),
        in_specs=[HBM] * (2 * na),
        out_specs=(SEM, SEM) + (HBM,) * (2 * na) + (pl.BlockSpec(memory_space=pltpu.VMEM),),
        input_output_aliases={i: 2 + i for i in range(2 * na)},
        compiler_params=pltpu.CompilerParams(has_side_effects=EFFECT),
    )(*[pltpu.with_memory_space_constraint(a, pltpu.HBM) for a in srcs + lands])
    return outs[0], outs[1], list(outs[2:2 + na]), list(outs[2 + na:2 + 2 * na]), outs[-1]


def _exchange_wait(started, after, scatter, name):
    send_sems, recv_sems, srcs, lands, _ = started
    na = len(srcs)

    def body(*refs):
        for cp in _exchange_copies(refs[:na], refs[na:2 * na], refs[2 * na], refs[2 * na + 1], scatter):
            cp.wait_send()
            cp.wait_recv()

    outs = pl.pallas_call(
        body, name=name,
        out_shape=tuple(pltpu.HBM(a.shape, a.dtype) for a in srcs + lands),
        in_specs=[HBM] * (2 * na) + [SEM, SEM, ANY], out_specs=(HBM,) * (2 * na),
        input_output_aliases={i: i for i in range(2 * na)},
        compiler_params=pltpu.CompilerParams(has_side_effects=EFFECT),
    )(*srcs, *lands, send_sems, recv_sems, after)
    return list(outs[:na]), list(outs[na:])


def _own_block(land, block, me):
    return lax.dynamic_update_index_in_dim(land, block, me, 0)


def _sum8_adamw(r2, w, m, v, layer, name):
    _, rr, cc = r2.shape
    tr = _tile(rr, 256)
    bc1 = 1.0 - ADAM_B1 ** ADAM_STEP
    bc2 = 1.0 - ADAM_B2 ** ADAM_STEP

    def body(r_ref, w_ref, m_ref, v_ref, g_ref, d_ref, nm_ref, nv_ref):
        gv = r_ref[0].astype(f32)
        for q in range(1, N_DEV):
            gv = gv + r_ref[q].astype(f32)
        g_ref[...] = gv
        nm = ADAM_B1 * m_ref[...] + (1.0 - ADAM_B1) * gv
        nv = ADAM_B2 * v_ref[...] + (1.0 - ADAM_B2) * (gv * gv)
        nm_ref[...] = nm
        nv_ref[...] = nv
        d_ref[...] = -ADAM_LR * ((nm / bc1) / (jnp.sqrt(nv / bc2) + ADAM_EPS) + ADAM_WD * w_ref[...])

    lspec = pl.BlockSpec((None, tr, cc), lambda i: (layer, i, 0))
    return _pc(body, name=name, grid=(rr // tr,),
               in_specs=[pl.BlockSpec((N_DEV, tr, cc), lambda i: (0, i, 0)), lspec, lspec, lspec],
               out_specs=[_rows(tr, cc)] * 4, out_shape=[jax.ShapeDtypeStruct((rr, cc), f32)] * 4,
               sem=("parallel",))(r2, w, m, v)


def _sum_devices(recv, name):
    _, rr, _ = recv.shape
    tr = _tile(rr, 512)

    def body(r_ref, o_ref):
        acc = r_ref[0]
        for i in range(1, N_DEV):
            acc = acc + r_ref[i]
        o_ref[...] = acc

    return _pc(body, name=name, grid=(rr // tr,),
               in_specs=[pl.BlockSpec((N_DEV, tr, LANES), lambda i: (0, i, 0))],
               out_specs=_rows(tr, LANES), out_shape=jax.ShapeDtypeStruct((rr, LANES), f32), sem=("parallel",))(recv)


def _adamw(w, g, m, v):
    rr = w.shape[0]
    tr = _tile(rr, 512)
    bc1 = 1.0 - ADAM_B1 ** ADAM_STEP
    bc2 = 1.0 - ADAM_B2 ** ADAM_STEP

    def body(w_ref, g_ref, m_ref, v_ref, d_ref, nm_ref, nv_ref):
        gv = g_ref[...]
        nm = ADAM_B1 * m_ref[...] + (1.0 - ADAM_B1) * gv
        nv = ADAM_B2 * v_ref[...] + (1.0 - ADAM_B2) * (gv * gv)
        nm_ref[...] = nm
        nv_ref[...] = nv
        d_ref[...] = -ADAM_LR * ((nm / bc1) / (jnp.sqrt(nv / bc2) + ADAM_EPS) + ADAM_WD * w_ref[...])

    spec = _rows(tr, LANES)
    return _pc(body, name="adamw", grid=(rr // tr,), in_specs=[spec] * 4, out_specs=[spec] * 3,
               out_shape=[jax.ShapeDtypeStruct((rr, LANES), f32)] * 3, sem=("parallel",))(w, g, m, v)


PACK_ROWS = 512
PART_ROWS = SUBLANES


def _pack(arrs):
    parts, sizes = [], []
    for a in arrs:
        flat = a.reshape(-1)
        n = flat.shape[0]
        rows = -(-n // (LANES * PART_ROWS)) * PART_ROWS
        if rows * LANES != n:
            flat = jnp.pad(flat, (0, rows * LANES - n))
        parts.append(flat.reshape(rows, LANES))
        sizes.append((rows, n))
    total = sum(r for r, _ in sizes)
    padded = -(-total // PACK_ROWS) * PACK_ROWS
    if padded > total:
        parts.append(jnp.zeros((padded - total, LANES), parts[0].dtype))
    return jnp.concatenate(parts, axis=0), sizes


def _unpack(packed, sizes, shapes):
    out, off = [], 0
    for (rows, n), shp in zip(sizes, shapes):
        piece = lax.slice_in_dim(packed, off, off + rows, axis=0)
        if rows * LANES != n:
            piece = lax.slice_in_dim(piece.reshape(-1), 0, n, axis=0)
        out.append(piece.reshape(tuple(shp)))
        off += rows
    return out


def _cols_to_blocks(a):
    n = a.shape[-1] // N_DEV
    a = a.reshape(a.shape[:-1] + (N_DEV, n))
    return jnp.moveaxis(a, -2, 0)


def _blocks_to_cols(a):
    a = jnp.moveaxis(a, 0, -2)
    return a.reshape(a.shape[:-2] + (a.shape[-2] * a.shape[-1],))


def _rows_to_blocks(a):
    k = a.shape[-2] // N_DEV
    a = a.reshape(a.shape[:-2] + (N_DEV, k, a.shape[-1]))
    return jnp.moveaxis(a, -3, 0)


def _blocks_to_rows(a):
    a = jnp.moveaxis(a, 0, -3)
    return a.reshape(a.shape[:-3] + (a.shape[-3] * a.shape[-2], a.shape[-1]))


COL_SHARDED = ("cv_w_pw1", "gdn_w_in", "mlp_w1")
ROW_SHARDED = ("cv_w_pw2", "gdn_w_out", "mlp_w2")
CONV_SHARDED = ("cv_w_dw", "gdn_conv_w")
REPLICATED = ("norm_mix_g", "norm_ffn_g", "final_norm_g", "cv_b_pw1", "cv_b_dw", "cv_ln_g", "cv_ln_b", "cv_b_pw2",
              "gdn_a_log", "gdn_dt_bias", "gdn_norm_g")
WEIGHTS = ("norm_mix_g", "norm_ffn_g", "final_norm_g", "cv_w_pw1", "cv_b_pw1", "cv_w_dw", "cv_b_dw", "cv_ln_g",
           "cv_ln_b", "cv_w_pw2", "cv_b_pw2", "gdn_w_in", "gdn_conv_w", "gdn_a_log", "gdn_dt_bias", "gdn_norm_g",
           "gdn_w_out", "mlp_w1", "mlp_w2")
MATMUL_SHARDED = COL_SHARDED + ROW_SHARDED


def _squeeze_layer(name, a):
    if name in ("norm_mix_g", "norm_ffn_g", "final_norm_g", "mlp_w1", "mlp_w2"):
        return a
    return a[0]


def _gather_weights(shards):
    me = 4 * lax.axis_index("x") + 2 * lax.axis_index("y") + lax.axis_index("c")
    pw1 = _all_gather_many([shards["cv_w_pw1"].astype(bf16)])[0]
    now = {"cv_w_pw1": pw1}

    def cast(a, tie):
        return (a + tie).astype(bf16)

    token, _ = lax.optimization_barrier((jnp.zeros((), f32), pw1))
    later, started = {}, {}
    for group in ("conv", "mlp0", "gdn", "mlp1"):
        if group == "conv":
            srcs = [cast(shards["cv_w_pw2"], token)] + [shards[n] + token for n in CONV_SHARDED]
        elif group == "gdn":
            srcs = [cast(shards["gdn_w_in"], token).reshape(-1, LANES), cast(shards["gdn_w_out"], token)]
        else:
            layer = int(group[-1])
            srcs = [cast(shards["mlp_w1"][layer], token), cast(shards["mlp_w2"][layer], token)]
        later[group] = srcs
        started[group] = _exchange_start(srcs, [False] * len(srcs), f"weights_{group}_start")
        token = started[group][4][0, 0]

    def need(group, after):
        srcs, lands = _exchange_wait(started[group], after, [False] * len(later[group]), f"weights_{group}_wait")
        lands = [_own_block(ld, own, me) for ld, own in zip(lands, srcs)]
        if group == "conv":
            out = {"cv_w_pw2": _blocks_to_rows(lands[0])}
            out.update({n: _blocks_to_cols(ld) for n, ld in zip(CONV_SHARDED, lands[1:])})
            return out
        if group == "gdn":
            w_in = _blocks_to_cols(lands[0].reshape((N_DEV,) + shards["gdn_w_in"].shape))
            return {"w_in": w_in, "w_out": _blocks_to_rows(lands[1])}
        return {"w1": lands[0], "w2": lands[1]}

    return now, need, token


def kernel(x, norm_mix_g, norm_ffn_g, final_norm_g, cv_w_pw1, cv_b_pw1, cv_w_dw, cv_b_dw, cv_ln_g, cv_ln_b, cv_w_pw2, cv_b_pw2, gdn_w_in, gdn_conv_w, gdn_a_log, gdn_dt_bias, gdn_norm_g, gdn_w_out, mlp_w1, mlp_w2, loss_target, m_norm_mix_g, m_norm_ffn_g, m_final_norm_g, m_cv_w_pw1, m_cv_b_pw1, m_cv_w_dw, m_cv_b_dw, m_cv_ln_g, m_cv_ln_b, m_cv_w_pw2, m_cv_b_pw2, m_gdn_w_in, m_gdn_conv_w, m_gdn_a_log, m_gdn_dt_bias, m_gdn_norm_g, m_gdn_w_out, m_mlp_w1, m_mlp_w2, v_norm_mix_g, v_norm_ffn_g, v_final_norm_g, v_cv_w_pw1, v_cv_b_pw1, v_cv_w_dw, v_cv_b_dw, v_cv_ln_g, v_cv_ln_b, v_cv_w_pw2, v_cv_b_pw2, v_gdn_w_in, v_gdn_conv_w, v_gdn_a_log, v_gdn_dt_bias, v_gdn_norm_g, v_gdn_w_out, v_mlp_w1, v_mlp_w2):
    w_in = dict(zip(WEIGHTS, (norm_mix_g, norm_ffn_g, final_norm_g, cv_w_pw1, cv_b_pw1, cv_w_dw, cv_b_dw, cv_ln_g, cv_ln_b, cv_w_pw2, cv_b_pw2, gdn_w_in, gdn_conv_w, gdn_a_log, gdn_dt_bias, gdn_norm_g, gdn_w_out, mlp_w1, mlp_w2)))
    m_in = dict(zip(WEIGHTS, (m_norm_mix_g, m_norm_ffn_g, m_final_norm_g, m_cv_w_pw1, m_cv_b_pw1, m_cv_w_dw, m_cv_b_dw, m_cv_ln_g, m_cv_ln_b, m_cv_w_pw2, m_cv_b_pw2, m_gdn_w_in, m_gdn_conv_w, m_gdn_a_log, m_gdn_dt_bias, m_gdn_norm_g, m_gdn_w_out, m_mlp_w1, m_mlp_w2)))
    v_in = dict(zip(WEIGHTS, (v_norm_mix_g, v_norm_ffn_g, v_final_norm_g, v_cv_w_pw1, v_cv_b_pw1, v_cv_w_dw, v_cv_b_dw, v_cv_ln_g, v_cv_ln_b, v_cv_w_pw2, v_cv_b_pw2, v_gdn_w_in, v_gdn_conv_w, v_gdn_a_log, v_gdn_dt_bias, v_gdn_norm_g, v_gdn_w_out, v_mlp_w1, v_mlp_w2)))
    me = 4 * lax.axis_index("x") + 2 * lax.axis_index("y") + lax.axis_index("c")

    shards = {n: _squeeze_layer(n, w_in[n]) for n in WEIGHTS}
    first, need, token = _gather_weights(shards)
    params = {n: shards[n] for n in REPLICATED}
    params.update(first)
    params["norm_mix_g"] = params["norm_mix_g"] + token

    def row_blocks(a):
        return a.reshape(N_DEV, a.shape[0] // N_DEV, a.shape[1])

    def flat_blocks(a):
        k, n8 = a.shape
        return _cols_to_blocks(a).reshape(N_DEV, k * (n8 // N_DEV) // LANES, LANES)

    def as_blocks(n, a):
        if n == "gdn_w_in":
            return flat_blocks(a).astype(bf16)
        return a if a.ndim == 3 else row_blocks(a)

    sent = []

    small = REPLICATED + CONV_SHARDED
    small_early = tuple(n for n in small if n != "norm_mix_g") + ("loss",)
    small_info = {}

    def emit(group, grads_out):
        names, blocks, scatter = list(grads_out), [], []
        for n in names:
            if n == "small":
                packed, small_info["sizes"] = _pack([grads_out[n][k] for k in small_early])
                small_info["shapes"] = [grads_out[n][k].shape for k in small_early]
                blocks.append(packed)
                scatter.append(False)
            else:
                blocks.append(as_blocks(n, grads_out[n]))
                scatter.append(True)
        sent.append((names, _exchange_start(blocks, scatter, f"grads_{group}_start"), scatter))
        return sent[-1][1][4][0, 0]

    _, grad_x, gr = _local_step(x, loss_target, params, need, emit)
    nm_shape = gr["norm_mix_g"].shape
    sent.append((["norm_mix_g"], _exchange_start([gr["norm_mix_g"].reshape(-1, LANES)], [False], "grads_norm_mix_start"), [False]))

    recv = {}

    def finish(entry, after):
        names, st, scatter = entry
        srcs, lands = _exchange_wait(st, after, scatter, f"grads_{names[0]}_wait")
        for n, ld, blk, sc in zip(names, lands, srcs, scatter):
            own = lax.dynamic_index_in_dim(blk, me, 0, keepdims=False) if sc else blk
            recv[n] = _own_block(ld, own, me)
        return lands[0]

    def as3d(n, a):
        if n == "gdn_w_in":
            return a.reshape(a.shape[0], -1, LANES)
        return a

    big = [("cv_w_pw2", 0, "cv_w_pw2"), ("gdn_w_in", 0, "gdn_w_in"), ("gdn_w_out", 0, "gdn_w_out"),
           ("mlp_w1", 0, "mlp_w1_0"), ("mlp_w1", 1, "mlp_w1_1"), ("mlp_w2", 0, "mlp_w2_0"), ("mlp_w2", 1, "mlp_w2_1"),
           ("cv_w_pw1", 0, "cv_w_pw1")]
    res = {n: {} for n in MATMUL_SHARDED}
    after = grad_x
    for entry in sent[:-2]:
        after = finish(entry, after)
    for n, layer, key in big[:-1]:
        res[n][layer] = _sum8_adamw(recv[key], as3d(n, w_in[n]), as3d(n, m_in[n]), as3d(n, v_in[n]), layer, f"adamw_{key}")
        after = res[n][layer][0]
    after = finish(sent[-2], after)
    finish(sent[-1], after)
    n, layer, key = big[-1]
    res[n][layer] = _sum8_adamw(recv[key], as3d(n, w_in[n]), as3d(n, m_in[n]), as3d(n, v_in[n]), layer, f"adamw_{key}")

    grads = dict(zip(small_early, _unpack(_sum_devices(recv["small"], "grads_small_sum"), small_info["sizes"], small_info["shapes"])))
    grads["norm_mix_g"] = _sum_devices(recv["norm_mix_g"], "grads_norm_mix_sum").reshape(nm_shape)
    loss = grads["loss"].reshape(())
    for n in CONV_SHARDED:
        cn = shards[n].shape[-1]
        grads[n] = lax.dynamic_slice_in_dim(grads[n], me * cn, cn, axis=1)

    out_groups = {n: [] for n in WEIGHTS}
    for n in MATMUL_SHARDED:
        layers = sorted(res[n])
        for k in range(4):
            pieces = [res[n][layer][k] for layer in layers]
            out_groups[n].append(jnp.stack(pieces).reshape(w_in[n].shape))

    sm_w = [shards[n] for n in small]
    sm_g = [grads[n].reshape(shards[n].shape) for n in small]
    sm_m = [_squeeze_layer(n, m_in[n]) for n in small]
    sm_v = [_squeeze_layer(n, v_in[n]) for n in small]
    wp, psz = _pack(sm_w)
    gp, _ = _pack(sm_g)
    mp, _ = _pack(sm_m)
    vp, _ = _pack(sm_v)
    dp, nmp, nvp = _adamw(wp, gp, mp, vp)
    shp = [a.shape for a in sm_w]
    for n, g, dl, nm, nv in zip(small, sm_g, _unpack(dp, psz, shp), _unpack(nmp, psz, shp), _unpack(nvp, psz, shp)):
        out_groups[n] = [a.reshape(w_in[n].shape) for a in (g, dl, nm, nv)]

    outs = [loss, grad_x]
    for k in range(4):
        outs += [out_groups[n][k] for n in WEIGHTS]
    return tuple(outs)
```

```python
import jax
import jax.numpy as jnp
from jax import lax
from jax.experimental import pallas as pl
from jax.experimental.pallas import tpu as pltpu

f32, bf16 = jnp.float32, jnp.bfloat16

NORM_EPS = 1e-6
L2_EPS = 1e-6
CHUNK = 64
LANES = 128
SUBLANES = 8
N_DEV = 8
VMEM_LIMIT = 56 * 1024 * 1024
ROW_TILE = 512
CONV_PAD = 32
NEG = -1e30

ADAM_LR, ADAM_B1, ADAM_B2, ADAM_EPS, ADAM_WD, ADAM_STEP = 0.001, 0.9, 0.999, 1e-08, 0.01, 10

NT = (((1,), (1,)), ((), ()))
TN = (((0,), (0,)), ((), ()))
HI = lax.Precision.HIGHEST


def _pc(body, *, name, grid, in_specs, out_specs, out_shape, scratch=(), sem=None):
    return pl.pallas_call(
        body, name=name, grid=grid, in_specs=in_specs, out_specs=out_specs, out_shape=out_shape,
        scratch_shapes=list(scratch),
        compiler_params=pltpu.CompilerParams(dimension_semantics=sem, vmem_limit_bytes=VMEM_LIMIT))


def _rows(tm, n):
    return pl.BlockSpec((tm, n), lambda i: (i, 0))


def _const(shape):
    return pl.BlockSpec(shape, lambda *_: (0,) * len(shape))


def _resident(shape):
    return pl.BlockSpec(shape, lambda *_: (0,) * len(shape), pipeline_mode=pl.Buffered(1))


def _tile(t, pref):
    return pref if t % pref == 0 else t


def _dot(a, b):
    return jnp.dot(a.astype(bf16), b.astype(bf16), preferred_element_type=f32)


def _dot_nt(a, b):
    return lax.dot_general(a.astype(bf16), b.astype(bf16), NT, preferred_element_type=f32)


def _dot_tn(a, b):
    return lax.dot_general(a.astype(bf16), b.astype(bf16), TN, preferred_element_type=f32)


def _sigmoid(x):
    return 1.0 / (1.0 + jnp.exp(-x))


def _silu_grad(x):
    s = _sigmoid(x)
    return s * (1.0 + x * (1.0 - s))


def _rms(x, g):
    rstd = lax.rsqrt(jnp.mean(x * x, axis=-1, keepdims=True) + NORM_EPS)
    xh = x * rstd
    return xh * g, xh, rstd


def _rms_bwd(dn, xh, rstd, g):
    dxh = dn * g
    return rstd * (dxh - xh * jnp.mean(dxh * xh, axis=-1, keepdims=True))


def _acc_init(step, *refs):
    @pl.when(step == 0)
    def _():
        for r in refs:
            r[...] = jnp.zeros(r.shape, r.dtype)


def _acc_rows(ref, val):
    ref[0:1, :] += jnp.sum(val, axis=0, keepdims=True)


def _pw1_glu(x, g, w, b):
    t, d = x.shape
    tm = _tile(t, ROW_TILE)

    nb_w = w.shape[0]

    def body(x_ref, g_ref, w_hbm, b_ref, n_ref, u_ref, gl_ref, w_ref, sems):
        _fetch_blocks(pl.program_id(0), w_hbm, w_ref, sems, True)
        n, _, _ = _rms(x_ref[...], g_ref[...])
        nb = n.astype(bf16)
        n_ref[...] = nb
        u = jnp.dot(nb, w_ref[...], preferred_element_type=f32) + b_ref[...]
        u_ref[...] = u.astype(bf16)
        gl_ref[...] = u[:, :d] * _sigmoid(u[:, d:])

    return _pc(body, name="pw1_glu", grid=(t // tm,),
               in_specs=[_rows(tm, d), _const((1, d)), ANY, _const((1, 2 * d))],
               out_specs=[_rows(tm, d), _rows(tm, 2 * d), _rows(tm, d)],
               out_shape=[jax.ShapeDtypeStruct((t, d), bf16), jax.ShapeDtypeStruct((t, 2 * d), bf16),
                          jax.ShapeDtypeStruct((t, d), f32)],
               scratch=[pltpu.VMEM((d, 2 * d), bf16), pltpu.SemaphoreType.DMA((nb_w,))],
               sem=("arbitrary",))(x, g, w, b)


def _ln_silu_mm_res(dc, ln_g, ln_b, w, b, res):
    t, d = dc.shape
    tm = _tile(t, ROW_TILE)

    def body(x_ref, g_ref, bb_ref, w_ref, b_ref, r_ref, s_ref, o_ref):
        x = x_ref[...]
        xc = x - jnp.mean(x, axis=-1, keepdims=True)
        rstd = lax.rsqrt(jnp.mean(xc * xc, axis=-1, keepdims=True) + NORM_EPS)
        ln = xc * rstd * g_ref[...] + bb_ref[...]
        sb = (ln * _sigmoid(ln)).astype(bf16)
        s_ref[...] = sb
        o_ref[...] = r_ref[...] + jnp.dot(sb, w_ref[...], preferred_element_type=f32) + b_ref[...]

    return _pc(body, name="ln_silu_pw2", grid=(t // tm,),
               in_specs=[_rows(tm, d), _const((1, d)), _const((1, d)), _resident((d, d)), _const((1, d)), _rows(tm, d)],
               out_specs=[_rows(tm, d), _rows(tm, d)],
               out_shape=[jax.ShapeDtypeStruct((t, d), bf16), jax.ShapeDtypeStruct((t, d), f32)],
               sem=("parallel",))(dc, ln_g, ln_b, w, b, res)


def _fetch_blocks(step, w_hbm, dst, sems, by_cols, layer=None):
    nb_w = w_hbm.shape[0]
    step_rows, step_cols = w_hbm.shape[-2], w_hbm.shape[-1]

    @pl.when(step == 0)
    def _():
        copies = []
        for j in range(nb_w):
            src = w_hbm.at[j] if layer is None else w_hbm.at[j, layer]
            if by_cols:
                part = dst.at[:, pl.ds(j * step_cols, step_cols)]
            else:
                part = dst.at[pl.ds(j * step_rows, step_rows), :]
            copies.append(pltpu.make_async_copy(src, part, sems.at[j]))
        for cp in copies:
            cp.start()
        for cp in copies:
            cp.wait()


def _mlp_fwd(h, g, w1g, w2g, name, loss=None):
    t, d = h.shape
    nb_w, _, bn = w1g.shape
    ff = nb_w * bn
    tm = _tile(t, 256)
    n_in = 4 if loss is None else 6
    n_out = 4 if loss is None else 6

    def body(*refs):
        h_ref, g_ref, w1_hbm, w2_hbm = refs[:4]
        n_ref, f_ref, r_ref = refs[n_in:n_in + 3]
        w1_ref, w2_ref, sem1, sem2 = refs[n_in + n_out:]
        _fetch_blocks(pl.program_id(0), w1_hbm, w1_ref, sem1, True)
        _fetch_blocks(pl.program_id(0), w2_hbm, w2_ref, sem2, False)
        hv = h_ref[...]
        n, _, _ = _rms(hv, g_ref[...])
        nb = n.astype(bf16)
        n_ref[...] = nb
        f = jnp.dot(nb, w1_ref[...], preferred_element_type=f32)
        f_ref[...] = f.astype(bf16)
        rb = jnp.square(jnp.maximum(f, 0.0)).astype(bf16)
        r_ref[...] = rb
        out = hv + jnp.dot(rb, w2_ref[...], preferred_element_type=f32)
        if loss is None:
            refs[n_in + 3][...] = out
        else:
            gf_ref, t_ref = refs[4:6]
            loss_ref, dh_ref, dg_ref = refs[n_in + 3:n_in + 6]
            _acc_init(pl.program_id(0), loss_ref, dg_ref)
            gv = gf_ref[...]
            y, xh, rstd = _rms(out, gv)
            e = y - t_ref[...]
            loss_ref[...] += 0.5 * jnp.sum(jnp.mean(e * e, axis=-1, keepdims=True))
            dy = e * (1.0 / d)
            _acc_rows(dg_ref, dy * xh)
            dh_ref[...] = _rms_bwd(dy, xh, rstd, gv)

    in_specs = [_rows(tm, d), _const((1, d)), ANY, ANY]
    out_specs = [_rows(tm, d), _rows(tm, ff), _rows(tm, ff)]
    out_shape = [jax.ShapeDtypeStruct((t, d), bf16), jax.ShapeDtypeStruct((t, ff), bf16), jax.ShapeDtypeStruct((t, ff), bf16)]
    args = [h, g, w1g, w2g]
    if loss is None:
        out_specs.append(_rows(tm, d))
        out_shape.append(jax.ShapeDtypeStruct((t, d), f32))
    else:
        in_specs += [_const((1, d)), _rows(tm, d)]
        args += list(loss)
        out_specs += [_const((SUBLANES, LANES)), _rows(tm, d), _const((SUBLANES, d))]
        out_shape += [jax.ShapeDtypeStruct((SUBLANES, LANES), f32), jax.ShapeDtypeStruct((t, d), f32),
                      jax.ShapeDtypeStruct((SUBLANES, d), f32)]
    return _pc(body, name=name, grid=(t // tm,), in_specs=in_specs, out_specs=out_specs, out_shape=out_shape,
               scratch=[pltpu.VMEM((d, ff), bf16), pltpu.VMEM((ff, d), bf16),
                        pltpu.SemaphoreType.DMA((nb_w,)), pltpu.SemaphoreType.DMA((nb_w,))],
               sem=("arbitrary",))(*args)


def _softplus(x):
    return jnp.maximum(x, 0.0) + jnp.log(1.0 + jnp.exp(-jnp.abs(x)))


def _gdn_in(h, g, w_main, w_ab, a_log_pad, dt_pad, n_heads):
    t, d = h.shape
    tm = _tile(t, ROW_TILE)
    gate_rows = 2 * n_heads

    def body(h_ref, g_ref, wm_ref, wab_ref, al_ref, dt_ref, n_ref, qkv_ref, z_ref, ab_ref, gb_ref):
        n, _, _ = _rms(h_ref[...], g_ref[...])
        nb = n.astype(bf16)
        n_ref[...] = nb
        p = jnp.dot(nb, wm_ref[...], preferred_element_type=f32)
        qkv_ref[...] = p[:, :3 * d]
        z_ref[...] = p[:, 3 * d:]
        ab = jnp.dot(nb, wab_ref[...], preferred_element_type=f32)
        ab_ref[...] = ab
        lane = lax.broadcasted_iota(jnp.int32, ab.shape, 1)
        decay = -jnp.exp(al_ref[...]) * _softplus(ab + dt_ref[...])
        gb_ref[...] = jnp.where(lane < n_heads, decay, jnp.where(lane < gate_rows, _sigmoid(ab), 0.0))

    return _pc(body, name="gdn_in", grid=(t // tm,),
               in_specs=[_rows(tm, d), _const((1, d)), _resident((d, 4 * d)), _resident((d, LANES)),
                         _const((1, LANES)), _const((1, LANES))],
               out_specs=[_rows(tm, d), _rows(tm, 3 * d), _rows(tm, d), _rows(tm, LANES), _rows(tm, LANES)],
               out_shape=[jax.ShapeDtypeStruct((t, d), bf16), jax.ShapeDtypeStruct((t, 3 * d), f32),
                          jax.ShapeDtypeStruct((t, d), f32), jax.ShapeDtypeStruct((t, LANES), f32),
                          jax.ShapeDtypeStruct((t, LANES), f32)],
               sem=("parallel",))(h, g, w_main, w_ab, a_log_pad, dt_pad)


def _gated_norm_mm_res(o, z, ng, w, res, n_heads):
    t, d = o.shape
    tm = _tile(t, ROW_TILE)

    def body(o_ref, z_ref, ng_ref, w_ref, r_ref, on_ref, out_ref):
        for hd in range(n_heads):
            sl = slice(hd * LANES, (hd + 1) * LANES)
            rn, _, _ = _rms(o_ref[:, sl], ng_ref[...])
            zz = z_ref[:, sl]
            on_ref[:, sl] = (rn * (zz * _sigmoid(zz))).astype(bf16)
        out_ref[...] = r_ref[...] + jnp.dot(on_ref[...], w_ref[...], preferred_element_type=f32)

    return _pc(body, name="gated_norm_wout", grid=(t // tm,),
               in_specs=[_rows(tm, d), _rows(tm, d), _const((1, LANES)), _resident((d, d)), _rows(tm, d)],
               out_specs=[_rows(tm, d), _rows(tm, d)],
               out_shape=[jax.ShapeDtypeStruct((t, d), bf16), jax.ShapeDtypeStruct((t, d), f32)],
               sem=("parallel",))(o, z, ng, w, res)


def _mlp_bwd(dho, h, g, fb, w1g, w2g, name):
    t, d = h.shape
    nb_w, _, bn = w1g.shape
    ff = nb_w * bn
    tm = _tile(t, 256)

    def body(do_ref, h_ref, g_ref, f_ref, w1_hbm, w2_hbm, df_ref, dh_ref, dg_ref, cs_ref, w1_ref, w2_ref, sem1, sem2):
        _fetch_blocks(pl.program_id(0), w1_hbm, w1_ref, sem1, True)
        _fetch_blocks(pl.program_id(0), w2_hbm, w2_ref, sem2, False)
        _acc_init(pl.program_id(0), dg_ref, cs_ref)
        do = do_ref[...]
        dr = lax.dot_general(do.astype(bf16), w2_ref[...], NT, preferred_element_type=f32)
        dfb = (dr * (2.0 * jnp.maximum(f_ref[...].astype(f32), 0.0))).astype(bf16)
        df_ref[...] = dfb
        dn = lax.dot_general(dfb, w1_ref[...], NT, preferred_element_type=f32)
        gv = g_ref[...]
        _, xh, rstd = _rms(h_ref[...], gv)
        _acc_rows(dg_ref, dn * xh)
        dh = do + _rms_bwd(dn, xh, rstd, gv)
        dh_ref[...] = dh
        _acc_rows(cs_ref, dh)

    return _pc(body, name=name, grid=(t // tm,),
               in_specs=[_rows(tm, d), _rows(tm, d), _const((1, d)), _rows(tm, ff), ANY, ANY],
               out_specs=[_rows(tm, ff), _rows(tm, d), _const((SUBLANES, d)), _const((SUBLANES, d))],
               out_shape=[jax.ShapeDtypeStruct((t, ff), bf16), jax.ShapeDtypeStruct((t, d), f32),
                          jax.ShapeDtypeStruct((SUBLANES, d), f32), jax.ShapeDtypeStruct((SUBLANES, d), f32)],
               scratch=[pltpu.VMEM((d, ff), bf16), pltpu.VMEM((ff, d), bf16),
                        pltpu.SemaphoreType.DMA((nb_w,)), pltpu.SemaphoreType.DMA((nb_w,))],
               sem=("arbitrary",))(dho, h, g, fb, w1g, w2g)


class _Tail:
    def __init__(self, fn, ins, outs):
        self.fn, self.ins, self.outs = fn, ins, outs


def _mm_nt(pairs, name, tail=None):
    t = pairs[0][0].shape[0]
    tm = _tile(t, ROW_TILE)
    npair = len(pairs)
    in_specs, args, scratch, blocked = [], [], [], []
    k = None
    for dy, w in pairs:
        nn = dy.shape[1]
        if isinstance(w, tuple) and isinstance(w[0], str):
            w = w[1]
            k = w.shape[1]
            wspec = ANY
            blocked.append(True)
            scratch += [pltpu.VMEM((k, nn), bf16), pltpu.SemaphoreType.DMA((w.shape[0],))]
        elif isinstance(w, tuple):
            w, idx = w
            k = w.shape[0]
            wspec = pl.BlockSpec((k, nn), lambda *_, idx=idx: (0, idx), pipeline_mode=pl.Buffered(1))
            blocked.append(False)
        else:
            k = w.shape[0]
            wspec = _resident(w.shape)
            blocked.append(False)
        in_specs += [_rows(tm, nn), wspec]
        args += [dy, w]
    n_tin = len(tail.ins) if tail else 0
    n_out = len(tail.outs) if tail else 1
    if tail:
        for arr, kind in tail.ins:
            in_specs.append(_rows(tm, arr.shape[1]) if kind == "rows" else _const(arr.shape))
            args.append(arr)
        out_specs = [_rows(tm, c) if kind == "rows" else _const((SUBLANES, c)) for c, kind in tail.outs]
        out_shape = [jax.ShapeDtypeStruct((t, c) if kind == "rows" else (SUBLANES, c), f32) for c, kind in tail.outs]
    else:
        out_specs = _rows(tm, k)
        out_shape = jax.ShapeDtypeStruct((t, k), f32)

    def body(*refs):
        step = pl.program_id(0)
        tin = refs[2 * npair:2 * npair + n_tin]
        outs = refs[2 * npair + n_tin:2 * npair + n_tin + n_out]
        scr = list(refs[2 * npair + n_tin + n_out:])
        acc = None
        for p in range(npair):
            w_ref = refs[2 * p + 1]
            if blocked[p]:
                w_vmem, sems = scr.pop(0), scr.pop(0)
                _fetch_blocks(step, w_ref, w_vmem, sems, True)
                w_ref = w_vmem
            part = lax.dot_general(refs[2 * p][...].astype(bf16), w_ref[...], NT, preferred_element_type=f32)
            acc = part if acc is None else acc + part
        if tail is None:
            outs[0][...] = acc
        else:
            _acc_init(step, *[o for o, (_, kind) in zip(outs, tail.outs) if kind == "acc"])
            tail.fn(acc, tin, outs)

    sequential = tail is not None or any(blocked)
    return _pc(body, name=name, grid=(t // tm,), in_specs=in_specs, out_specs=out_specs, out_shape=out_shape,
               scratch=scratch, sem=("arbitrary",) if sequential else ("parallel",))(*args)


def _rms_bwd_tail(h, g, dres):
    def fn(dn, ins, outs):
        h_ref, g_ref, dr_ref = ins
        dh_ref, dg_ref = outs
        gv = g_ref[...]
        _, xh, rstd = _rms(h_ref[...], gv)
        _acc_rows(dg_ref, dn * xh)
        dh_ref[...] = dr_ref[...] + _rms_bwd(dn, xh, rstd, gv)

    d = h.shape[1]
    return _Tail(fn, [(h, "rows"), (g, "const"), (dres, "rows")], [(d, "rows"), (d, "acc")])


def _ln_silu_bwd_tail(dc, ln_g, ln_b):
    def fn(ds, ins, outs):
        x_ref, g_ref, b_ref = ins
        dx_ref, dg_ref, db_ref, cs_ref = outs
        x = x_ref[...]
        gv = g_ref[...]
        xc = x - jnp.mean(x, axis=-1, keepdims=True)
        rstd = lax.rsqrt(jnp.mean(xc * xc, axis=-1, keepdims=True) + NORM_EPS)
        xh = xc * rstd
        dln = ds * _silu_grad(xh * gv + b_ref[...])
        _acc_rows(dg_ref, dln * xh)
        _acc_rows(db_ref, dln)
        dxh = dln * gv
        dx = rstd * (dxh - jnp.mean(dxh, axis=-1, keepdims=True) - xh * jnp.mean(dxh * xh, axis=-1, keepdims=True))
        dx_ref[...] = dx
        _acc_rows(cs_ref, dx)

    d = dc.shape[1]
    return _Tail(fn, [(dc, "rows"), (ln_g, "const"), (ln_b, "const")], [(d, "rows"), (d, "acc"), (d, "acc"), (d, "acc")])


def _gated_norm_bwd_tail(o, z, ng, n_heads):
    def fn(don_all, ins, outs):
        o_ref, z_ref, ng_ref = ins
        do_ref, dz_ref, dng_ref = outs
        gv = ng_ref[...]
        for hd in range(n_heads):
            sl = slice(hd * LANES, (hd + 1) * LANES)
            rn, xh, rstd = _rms(o_ref[:, sl], gv)
            zz = z_ref[:, sl]
            don = don_all[:, sl]
            dz_ref[:, sl] = don * rn * _silu_grad(zz)
            drn = don * (zz * _sigmoid(zz))
            _acc_rows(dng_ref, drn * xh)
            do_ref[:, sl] = _rms_bwd(drn, xh, rstd, gv)

    d = o.shape[1]
    return _Tail(fn, [(o, "rows"), (z, "rows"), (ng, "const")], [(d, "rows"), (d, "rows"), (LANES, "acc")])


def _mm_tn_blocked(x, dy, name, out_dtype=f32):
    t, k = x.shape
    bn = dy.shape[1] // N_DEV
    tm = _tile(t, 2048 if k <= 1024 else 1024)
    jb = N_DEV
    while jb > 1 and k * jb * bn * 4 > 8 * 1024 * 1024:
        jb //= 2
    nt = t // tm

    def body(x_ref, dy_ref, o_ref, *acc):
        acc_ref = acc[0] if acc else o_ref
        _acc_init(pl.program_id(1), acc_ref)
        xt = x_ref[...].astype(bf16).T
        for jj in range(jb):
            acc_ref[jj] += jnp.dot(xt, dy_ref[:, jj * bn:(jj + 1) * bn].astype(bf16), preferred_element_type=f32)
        if acc:
            @pl.when(pl.program_id(1) == nt - 1)
            def _():
                o_ref[...] = acc_ref[...].astype(out_dtype)

    return _pc(body, name=name, grid=(N_DEV // jb, nt),
               in_specs=[pl.BlockSpec((tm, k), lambda j, i: (i, 0)), pl.BlockSpec((tm, jb * bn), lambda j, i: (i, j))],
               out_specs=pl.BlockSpec((jb, k, bn), lambda j, i: (j, 0, 0)),
               out_shape=jax.ShapeDtypeStruct((N_DEV, k, bn), out_dtype),
               scratch=[] if out_dtype == f32 else [pltpu.VMEM((jb, k, bn), f32)],
               sem=("parallel", "arbitrary"))(x, dy)


def _mm_tn(x, dy, name, out_dtype=f32):
    t, k = x.shape
    n = dy.shape[1]
    tm = _tile(t, 2048 if k <= 1024 else 1024)
    cap = max(LANES, (2 * 1024 * 1024) // k)
    tn = n
    if n > cap:
        tn = max(c for c in range(LANES, cap + 1, LANES) if n % c == 0)
    nt = t // tm

    def body(x_ref, dy_ref, o_ref, *acc):
        acc_ref = acc[0] if acc else o_ref
        _acc_init(pl.program_id(1), acc_ref)
        acc_ref[...] += lax.dot_general(x_ref[...].astype(bf16), dy_ref[...].astype(bf16), TN, preferred_element_type=f32)
        if acc:
            @pl.when(pl.program_id(1) == nt - 1)
            def _():
                o_ref[...] = acc_ref[...].astype(out_dtype)

    return _pc(body, name=name, grid=(n // tn, nt),
               in_specs=[pl.BlockSpec((tm, k), lambda j, i: (i, 0)), pl.BlockSpec((tm, tn), lambda j, i: (i, j))],
               out_specs=pl.BlockSpec((k, tn), lambda j, i: (0, j)),
               out_shape=jax.ShapeDtypeStruct((k, n), out_dtype),
               scratch=[] if out_dtype == f32 else [pltpu.VMEM((k, tn), f32)],
               sem=("parallel", "arbitrary"))(x, dy)


def _gates_bwd(dgb_t, ab, a_log_pad, dt_pad, n_heads):
    t = ab.shape[0]
    tm = _tile(t, ROW_TILE)
    gate_rows = 2 * n_heads

    def body(dgb_ref, ab_ref, al_ref, dt_ref, dab_ref, dal_ref, ddt_ref):
        _acc_init(pl.program_id(0), dal_ref, ddt_ref)
        ab = ab_ref[...]
        dgb = jnp.concatenate([dgb_ref[...], jnp.zeros((LANES - gate_rows, tm), f32)], axis=0).T
        lane = lax.broadcasted_iota(jnp.int32, ab.shape, 1)
        is_a = lane < n_heads
        is_b = jnp.logical_and(lane >= n_heads, lane < 2 * n_heads)
        xa = ab + dt_ref[...]
        neg_a = -jnp.exp(al_ref[...])
        dg_da = neg_a * _sigmoid(xa)
        beta = _sigmoid(ab)
        da = jnp.where(is_a, dgb * dg_da, 0.0)
        dab_ref[...] = da + jnp.where(is_b, dgb * beta * (1.0 - beta), 0.0)
        _acc_rows(dal_ref, jnp.where(is_a, dgb * neg_a * _softplus(xa), 0.0))
        _acc_rows(ddt_ref, da)

    return _pc(body, name="gates_bwd", grid=(t // tm,),
               in_specs=[pl.BlockSpec((gate_rows, tm), lambda i: (0, i)), _rows(tm, LANES), _const((1, LANES)), _const((1, LANES))],
               out_specs=[_rows(tm, LANES), _const((SUBLANES, LANES)), _const((SUBLANES, LANES))],
               out_shape=[jax.ShapeDtypeStruct((t, LANES), f32), jax.ShapeDtypeStruct((SUBLANES, LANES), f32),
                          jax.ShapeDtypeStruct((SUBLANES, LANES), f32)],
               sem=("arbitrary",))(dgb_t, ab, a_log_pad, dt_pad)


def _glu_bwd(dgl, ub):
    t, d = dgl.shape
    tm = _tile(t, ROW_TILE)

    def body(dgl_ref, u_ref, du_ref, cs_ref):
        _acc_init(pl.program_id(0), cs_ref)
        dgl = dgl_ref[...]
        a = u_ref[:, :d].astype(f32)
        sb = _sigmoid(u_ref[:, d:].astype(f32))
        da = dgl * sb
        db = dgl * a * sb * (1.0 - sb)
        du_ref[:, :d] = da.astype(bf16)
        du_ref[:, d:] = db.astype(bf16)
        cs_ref[0:1, :d] += jnp.sum(da, axis=0, keepdims=True)
        cs_ref[0:1, d:] += jnp.sum(db, axis=0, keepdims=True)

    return _pc(body, name="glu_bwd", grid=(t // tm,),
               in_specs=[_rows(tm, d), _rows(tm, 2 * d)],
               out_specs=[_rows(tm, 2 * d), _const((SUBLANES, 2 * d))],
               out_shape=[jax.ShapeDtypeStruct((t, 2 * d), bf16), jax.ShapeDtypeStruct((SUBLANES, 2 * d), f32)],
               sem=("arbitrary",))(dgl, ub)


def _conv_rows(s):
    return 256 if s % 256 == 0 else s


def _conv_tap_sum(pad_ref, w_ref, base, rows, width):
    acc = jnp.zeros((rows, LANES), f32)
    for j in range(width):
        acc = acc + w_ref[j:j + 1, :] * pad_ref[pl.ds(base + CONV_PAD - (width - 1) + j, rows), :]
    return acc


def _qkv_kinds(j, n_heads, run):
    pl.when(j < n_heads)(lambda: run("q", float(LANES) ** -0.5))
    pl.when(jnp.logical_and(j >= n_heads, j < 2 * n_heads))(lambda: run("k", 1.0))
    pl.when(j >= 2 * n_heads)(lambda: run("v", 1.0))


def _l2_silu_post(c, kind, scale):
    a = c * _sigmoid(c)
    if kind == "v":
        return a, None, a
    r = lax.rsqrt(jnp.sum(a * a, axis=-1, keepdims=True) + L2_EPS)
    return a, r, a * (r * scale)


def _dwconv_fwd(x, w, b, name, qk_heads=None):
    bl, s, cn = x.shape
    width = w.shape[0]
    rows = _conv_rows(s)

    def body(x_ref, w_ref, b_ref, o_ref, pad_ref):
        pad_ref[0:CONV_PAD, :] = jnp.zeros((CONV_PAD, LANES), f32)
        pad_ref[CONV_PAD:, :] = x_ref[0]

        def run(kind, scale):
            def step(i, carry):
                base = pl.multiple_of(i * rows, rows)
                acc = _conv_tap_sum(pad_ref, w_ref, base, rows, width)
                if kind is None:
                    acc = acc + b_ref[...]
                else:
                    _, _, acc = _l2_silu_post(acc, kind, scale)
                o_ref[0, pl.ds(base, rows), :] = acc
                return carry

            lax.fori_loop(0, s // rows, step, 0)

        if qk_heads is None:
            run(None, 1.0)
        else:
            _qkv_kinds(pl.program_id(1), qk_heads, run)

    return _pc(body, name=name, grid=(bl, cn // LANES),
               in_specs=[pl.BlockSpec((1, s, LANES), lambda bi, j: (bi, 0, j)),
                         pl.BlockSpec((width, LANES), lambda bi, j: (0, j)),
                         pl.BlockSpec((1, LANES), lambda bi, j: (0, j))],
               out_specs=pl.BlockSpec((1, s, LANES), lambda bi, j: (bi, 0, j)),
               out_shape=jax.ShapeDtypeStruct((bl, s, cn), f32),
               scratch=[pltpu.VMEM((s + CONV_PAD, LANES), f32)],
               sem=("parallel", "parallel"))(x, w, b)


def _dwconv_bwd(x, dys, w, name, qk_heads=None):
    bl, s, cn = x.shape
    width = w.shape[0]
    wp = -(-width // SUBLANES) * SUBLANES
    rows = _conv_rows(s)
    nblk = s // rows
    ndy = len(dys)

    def body(*refs):
        x_ref, w_ref = refs[0], refs[1]
        dy_refs = refs[2:2 + ndy]
        dx_ref, dw_ref, db_ref, xpad, dypad, acc = refs[2 + ndy:]
        j = pl.program_id(0)
        bi = pl.program_id(1)
        _acc_init(bi, acc, db_ref)
        xpad[0:CONV_PAD, :] = jnp.zeros((CONV_PAD, LANES), f32)
        xpad[CONV_PAD:, :] = x_ref[0]
        dypad[s:, :] = jnp.zeros((CONV_PAD, LANES), f32)
        if qk_heads is None:
            dypad[0:s, :] = dy_refs[0][0]
        else:
            def run(kind, scale):
                dy_ref = dy_refs["qkv".index(kind)]

                def pre(i, carry):
                    base = pl.multiple_of(i * rows, rows)
                    c = _conv_tap_sum(xpad, w_ref, base, rows, width)
                    a, r, _ = _l2_silu_post(c, kind, scale)
                    da = dy_ref[0, pl.ds(base, rows), :]
                    if kind != "v":
                        dy = da * scale
                        da = r * (dy - a * (r * r) * jnp.sum(a * dy, axis=-1, keepdims=True))
                    dypad[pl.ds(base, rows), :] = da * _silu_grad(c)
                    return carry

                lax.fori_loop(0, nblk, pre, 0)

            _qkv_kinds(j, qk_heads, run)

        def step(i, carry):
            base = pl.multiple_of(i * rows, rows)
            dxa = jnp.zeros((rows, LANES), f32)
            for jj in range(width):
                dxa = dxa + w_ref[jj:jj + 1, :] * dypad[pl.ds(base + (width - 1) - jj, rows), :]
            dx_ref[0, pl.ds(base, rows), :] = dxa
            dyc = dypad[pl.ds(base, rows), :]
            db_ref[...] += dyc.reshape(rows // SUBLANES, SUBLANES, LANES).sum(axis=0)
            for jj in range(width):
                prod = dyc * xpad[pl.ds(base + CONV_PAD - (width - 1) + jj, rows), :]
                acc[jj * SUBLANES:(jj + 1) * SUBLANES, :] += prod.reshape(rows // SUBLANES, SUBLANES, LANES).sum(axis=0)
            return carry

        lax.fori_loop(0, nblk, step, 0)

        @pl.when(bi == bl - 1)
        def _():
            dw_ref[...] = jnp.zeros((wp, LANES), f32)
            for jj in range(width):
                dw_ref[jj:jj + 1, :] = jnp.sum(acc[jj * SUBLANES:(jj + 1) * SUBLANES, :], axis=0, keepdims=True)

    if qk_heads is None:
        dy_specs = [pl.BlockSpec((1, s, LANES), lambda j, bi: (bi, 0, j))]
    else:
        hh = qk_heads
        def dy_spec(part):
            def index(j, bi):
                mine = jnp.logical_and(j >= part * hh, j < (part + 1) * hh)
                return (jnp.where(mine, bi * hh + j - part * hh, 0), 0, 0)
            return pl.BlockSpec((1, s, LANES), index)

        dy_specs = [dy_spec(0), dy_spec(1), dy_spec(2)]
    return _pc(body, name=name, grid=(cn // LANES, bl),
               in_specs=[pl.BlockSpec((1, s, LANES), lambda j, bi: (bi, 0, j)),
                         pl.BlockSpec((width, LANES), lambda j, bi: (0, j))] + dy_specs,
               out_specs=[pl.BlockSpec((1, s, LANES), lambda j, bi: (bi, 0, j)),
                          pl.BlockSpec((wp, LANES), lambda j, bi: (0, j)),
                          pl.BlockSpec((SUBLANES, LANES), lambda j, bi: (0, j))],
               out_shape=[jax.ShapeDtypeStruct((bl, s, cn), f32), jax.ShapeDtypeStruct((wp, cn), f32),
                          jax.ShapeDtypeStruct((SUBLANES, cn), f32)],
               scratch=[pltpu.VMEM((s + CONV_PAD, LANES), f32), pltpu.VMEM((s + CONV_PAD, LANES), f32),
                        pltpu.VMEM((width * SUBLANES, LANES), f32)],
               sem=("parallel", "arbitrary"))(x, w, *dys)


def _group_rows(s):
    for rows in (256, 128):
        if s % rows == 0:
            return rows
    return CHUNK


def _group_masks(rows):
    r = lax.broadcasted_iota(jnp.int32, (rows, rows), 0)
    c = lax.broadcasted_iota(jnp.int32, (rows, rows), 1)
    return r, c, r >= c


def _decay(gc_col, gc_row, causal):
    return jnp.exp(jnp.where(causal, gc_col - gc_row, NEG))


def _inv_unit_lower_many(mats, r, c):
    n = r.shape[0]
    eye = (r == c).astype(f32)
    same16 = (r >> 4) == (c >> 4)
    ads = [jnp.where(same16, a, 0.0) for a in mats]
    aos = [a - ad for a, ad in zip(mats, ads)]
    xs = ads
    tds = [-x for x in xs]
    for _ in range(3):
        xs = [_dot(x, x) for x in xs]
        tds = [td + x + _dot(td, x) for td, x in zip(tds, xs)]
    bs = [ao + _dot(td, ao) for td, ao in zip(tds, aos)]
    ps = [-b for b in bs]
    span = 2
    while span < n // 16:
        bs = [_dot(b, b) for b in bs]
        ps = [p + b + _dot(p, b) for p, b in zip(ps, bs)]
        span *= 2
    return [p + td + _dot(p, td) for p, td in zip(ps, tds)]


def _lane_cumsum(x, y, reverse, name):
    rr, n = x.shape

    def body(x_ref, y_ref, o_ref):
        i = lax.broadcasted_iota(jnp.int32, (n, n), 0)
        j = lax.broadcasted_iota(jnp.int32, (n, n), 1)
        tri = ((i >= j) if reverse else (i <= j)).astype(f32)
        o_ref[...] = jnp.dot(x_ref[...] + y_ref[...], tri, precision=HI, preferred_element_type=f32)

    spec = pl.BlockSpec((rr, n), lambda: (0, 0))
    return pl.pallas_call(body, name=name, in_specs=[spec, spec], out_specs=spec,
                          out_shape=jax.ShapeDtypeStruct((rr, n), f32))(x, y)


def _gdn_specs(n_heads, nblk, rows, hp=1, rev=False):
    def blk(n):
        return nblk - 1 - n if rev else n

    def qkv(off):
        return pl.BlockSpec((rows, hp * LANES), lambda g, n: (
            lax.div(g * hp, n_heads) * nblk + blk(n), lax.div(off * n_heads + lax.rem(g * hp, n_heads), hp)))

    def per_head(last, block_rows=rows):
        return pl.BlockSpec((hp, block_rows, last), lambda g, n: (g, blk(n), 0))

    def row_vec():
        return pl.BlockSpec((hp, 1, 1, rows), lambda g, n: (g, blk(n), 0, 0))

    return qkv, per_head, row_vec


def _gate_cumsum(gbeta, bl, s, n_heads):
    rows = _group_rows(s)
    nblk = s // rows

    def body(g_ref, cols_ref, gct_ref):
        i = lax.broadcasted_iota(jnp.int32, (rows, rows), 0)
        j = lax.broadcasted_iota(jnp.int32, (rows, rows), 1)
        blk = g_ref[...]
        summed = jnp.dot((i >= j).astype(f32), blk, precision=HI, preferred_element_type=f32)
        for h in range(n_heads):
            cols_ref[h, :, 0:1] = summed[:, h:h + 1]
            cols_ref[h, :, 1:2] = blk[:, n_heads + h:n_heads + h + 1]
        gct_ref[...] = summed.T[:n_heads, :]

    return _pc(body, name="gdn_gate_cumsum", grid=(bl, nblk),
               in_specs=[pl.BlockSpec((rows, LANES), lambda b, n: (b * nblk + n, 0))],
               out_specs=[pl.BlockSpec((n_heads, rows, 2), lambda b, n: (b, n, 0)),
                          pl.BlockSpec((None, n_heads, rows), lambda b, n: (b, 0, n))],
               out_shape=[jax.ShapeDtypeStruct((bl * n_heads, s, 2), f32), jax.ShapeDtypeStruct((bl, n_heads, s), f32)],
               sem=("parallel", "parallel"))(gbeta)


def _gate_cols_spec(nblk, rows, rev=False, n_heads=None):
    return pl.BlockSpec((n_heads, rows, 2), lambda g, n: (g, nblk - 1 - n if rev else n, 0))


def _gdn_prep(qkv, cols, grow, bl, s, n_heads):
    rows = _group_rows(s)
    nblk = s // rows
    bh_n = bl * n_heads
    hp = n_heads
    qkv_spec, ph, rv = _gdn_specs(n_heads, nblk, rows, hp)

    def body(k_ref, v_ref, cols_ref, grow_ref, u_ref, w_ref, t_ref):
        r, c, causal = _group_masks(rows)
        mats, rhs = [], []
        for h in range(hp):
            hs = slice(h * LANES, (h + 1) * LANES)
            k = k_ref[:, hs]
            gc = cols_ref[h, :, 0:1]
            beta = cols_ref[h, :, 1:2]
            kb = k * beta
            mats.append(jnp.where(r > c, _dot_nt(kb, k) * _decay(gc, grow_ref[h, 0], causal), 0.0))
            rhs.append((v_ref[:, hs] * beta, kb * jnp.exp(gc)))
        for h, tm in enumerate(_inv_unit_lower_many(mats, r, c)):
            tb = tm.astype(bf16)
            u_ref[h] = rhs[h][0] + jnp.dot(tb, rhs[h][0].astype(bf16), preferred_element_type=f32)
            w_ref[h] = (rhs[h][1] + jnp.dot(tb, rhs[h][1].astype(bf16), preferred_element_type=f32)).astype(bf16)
            t_ref[h] = tb

    return _pc(body, name="gdn_prep", grid=(bh_n // hp, nblk),
               in_specs=[qkv_spec(1), qkv_spec(2), _gate_cols_spec(nblk, rows, n_heads=hp), rv()],
               out_specs=[ph(LANES), ph(LANES), ph(rows)],
               out_shape=[jax.ShapeDtypeStruct((bh_n, s, LANES), f32), jax.ShapeDtypeStruct((bh_n, s, LANES), bf16),
                          jax.ShapeDtypeStruct((bh_n, s, rows), bf16)],
               sem=("parallel", "parallel"))(qkv, qkv, cols, grow)


def _gdn_scan(qkv, u, w, cols, grow, bl, s, n_heads):
    rows = _group_rows(s)
    nblk = s // rows
    bh_n = bl * n_heads
    d = n_heads * LANES
    hp = n_heads
    qkv_spec, ph, rv = _gdn_specs(n_heads, nblk, rows, hp)

    def body(q_ref, k_ref, u_ref, w_ref, cols_ref, grow_ref, o_ref, vn_ref, ss_ref, s_scr):
        _acc_init(pl.program_id(1), s_scr)
        _, _, causal = _group_masks(rows)
        hh = range(hp)
        qs = [q_ref[:, h * LANES:(h + 1) * LANES] for h in hh]
        ks = [k_ref[:, h * LANES:(h + 1) * LANES] for h in hh]
        gcs = [cols_ref[h, :, 0:1] for h in hh]
        ps = [_dot_nt(qs[h], ks[h]) * _decay(gcs[h], grow_ref[h, 0], causal) for h in hh]
        sts = [s_scr[h] for h in hh]
        for h in hh:
            ss_ref[h] = sts[h].astype(bf16)
        vns = [u_ref[h] - _dot(w_ref[h], sts[h]) for h in hh]
        for h in hh:
            vn_ref[h] = vns[h].astype(bf16)
        o_state = [_dot(qs[h] * jnp.exp(gcs[h]), sts[h]) for h in hh]
        o_intra = [_dot(ps[h], vns[h]) for h in hh]
        for h in hh:
            o_ref[:, h * LANES:(h + 1) * LANES] = o_state[h] + o_intra[h]
        for h in hh:
            g_last = gcs[h][rows - 1:rows, :]
            s_scr[h] = jnp.exp(g_last) * sts[h] + _dot_tn(ks[h] * jnp.exp(g_last - gcs[h]), vns[h])

    return _pc(body, name="gdn_scan", grid=(bh_n // hp, nblk),
               in_specs=[qkv_spec(0), qkv_spec(1), ph(LANES), ph(LANES), _gate_cols_spec(nblk, rows, n_heads=hp), rv()],
               out_specs=[qkv_spec(0), ph(LANES), ph(LANES, block_rows=LANES)],
               out_shape=[jax.ShapeDtypeStruct((bl * s, d), f32), jax.ShapeDtypeStruct((bh_n, s, LANES), bf16),
                          jax.ShapeDtypeStruct((bh_n, nblk * LANES, LANES), bf16)],
               scratch=[pltpu.VMEM((hp, LANES, LANES), f32)],
               sem=("parallel", "arbitrary"))(qkv, qkv, u, w, cols, grow)


def _gdn_scan_bwd(do, qkv, w, vn, cols, grow, ss, bl, s, n_heads):
    rows = _group_rows(s)
    nblk = s // rows
    bh_n = bl * n_heads
    hp = n_heads
    qkv_spec, ph, rv = _gdn_specs(n_heads, nblk, rows, hp, rev=True)

    def body(do_ref, q_ref, k_ref, w_ref, vn_ref, cols_ref, grow_ref, ss_ref,
             du_ref, dw_ref, dq_ref, dk_ref, dcol_ref, drow_ref, ds_scr):
        _acc_init(pl.program_id(1), ds_scr)
        _, _, causal = _group_masks(rows)
        last_row = lax.broadcasted_iota(jnp.int32, (rows, 1), 0) == rows - 1
        hh = range(hp)
        dos = [do_ref[:, h * LANES:(h + 1) * LANES] for h in hh]
        qs = [q_ref[:, h * LANES:(h + 1) * LANES] for h in hh]
        ks = [k_ref[:, h * LANES:(h + 1) * LANES] for h in hh]
        vns = [vn_ref[h] for h in hh]
        gcs = [cols_ref[h, :, 0:1] for h in hh]
        sts = [ss_ref[h] for h in hh]
        dss = [ds_scr[h] for h in hh]
        dmats = [_decay(gcs[h], grow_ref[h, 0], causal) for h in hh]
        gams = [jnp.exp(gc) for gc in gcs]
        qgs = [qs[h] * gams[h] for h in hh]
        g_lasts = [gc[rows - 1:rows, :] for gc in gcs]
        kd_scales = [jnp.exp(g_lasts[h] - gcs[h]) for h in hh]
        kdecs = [ks[h] * kd_scales[h] for h in hh]
        qks = [_dot_nt(qs[h], ks[h]) for h in hh]
        dpds = [_dot_nt(dos[h], vns[h]) * dmats[h] for h in hh]
        dvns = [_dot_tn(qks[h] * dmats[h], dos[h]) + _dot(kdecs[h], dss[h]) for h in hh]
        for h in hh:
            du_ref[h] = dvns[h]
        dkdecs = [_dot_nt(vns[h], dss[h]) for h in hh]
        for h in hh:
            dw_ref[h] = -_dot_nt(dvns[h], sts[h])
        dqgs = [_dot_nt(dos[h], sts[h]) for h in hh]
        dq_intra = [_dot(dpds[h], ks[h]) for h in hh]
        dk_intra = [_dot_tn(dpds[h], qs[h]) for h in hh]
        for h in hh:
            dq_ref[h] = dqgs[h] * gams[h] + dq_intra[h]
            dk_ref[h] = dk_intra[h] + dkdecs[h] * kd_scales[h]
            ep = dpds[h] * qks[h]
            drow_ref[h, 0] = -jnp.sum(ep, axis=0, keepdims=True)
            kd_rows = jnp.sum(dkdecs[h] * kdecs[h], axis=-1, keepdims=True)
            extra = jnp.sum(kd_rows) + jnp.exp(g_lasts[h]) * jnp.sum(sts[h] * dss[h])
            dcol_ref[h] = (jnp.sum(dqgs[h] * qgs[h], axis=-1, keepdims=True) + jnp.sum(ep, axis=-1, keepdims=True)
                           - kd_rows + jnp.where(last_row, extra, 0.0))
        ds_new = [jnp.exp(g_lasts[h]) * dss[h] + _dot_tn(qgs[h], dos[h]) - _dot_tn(w_ref[h], dvns[h]) for h in hh]
        for h in hh:
            ds_scr[h] = ds_new[h]

    return _pc(body, name="gdn_scan_bwd", grid=(bh_n // hp, nblk),
               in_specs=[qkv_spec(0), qkv_spec(0), qkv_spec(1), ph(LANES), ph(LANES), _gate_cols_spec(nblk, rows, rev=True, n_heads=hp), rv(),
                         ph(LANES, block_rows=LANES)],
               out_specs=[ph(LANES), ph(LANES), ph(LANES), ph(LANES), ph(1), rv()],
               out_shape=[jax.ShapeDtypeStruct((bh_n, s, LANES), f32)] * 4
               + [jax.ShapeDtypeStruct((bh_n, s, 1), f32), jax.ShapeDtypeStruct((bh_n, nblk, 1, rows), f32)],
               scratch=[pltpu.VMEM((hp, LANES, LANES), f32)],
               sem=("parallel", "arbitrary"))(do, qkv, qkv, w, vn, cols, grow, ss)


def _gdn_prep_bwd(qkv, cols, grow, tmat, du, dw, dk_scan, dcol_scan, drow_scan, bl, s, n_heads):
    rows = _group_rows(s)
    nblk = s // rows
    bh_n = bl * n_heads
    hp = n_heads
    qkv_spec, ph, rv = _gdn_specs(n_heads, nblk, rows, hp)

    def body(k_ref, v_ref, cols_ref, grow_ref, t_ref, du_ref, dw_ref, dks_ref, dcs_ref, drs_ref,
             dk_ref, dv_ref, dcols_ref, drow_ref):
        r, c, causal = _group_masks(rows)
        hh = range(hp)
        ks = [k_ref[:, h * LANES:(h + 1) * LANES] for h in hh]
        vs = [v_ref[:, h * LANES:(h + 1) * LANES] for h in hh]
        gcs = [cols_ref[h, :, 0:1] for h in hh]
        betas = [cols_ref[h, :, 1:2] for h in hh]
        tms = [t_ref[h] for h in hh]
        dus = [du_ref[h] for h in hh]
        dws = [dw_ref[h] for h in hh]
        gams = [jnp.exp(gc) for gc in gcs]
        kbs = [k * b for k, b in zip(ks, betas)]
        kbgs = [kb * g for kb, g in zip(kbs, gams)]
        dts = [_dot_nt(dus[h], vs[h] * betas[h]) + _dot_nt(dws[h], kbgs[h]) for h in hh]
        dvbs = [dus[h] + _dot_tn(tms[h], dus[h]) for h in hh]
        dkbgs = [dws[h] + _dot_tn(tms[h], dws[h]) for h in hh]
        kks = [_dot_nt(kbs[h], ks[h]) for h in hh]
        inner = [dts[h] + _dot_nt(dts[h], tms[h]) for h in hh]
        dads = [jnp.where(r > c, -(inner[h] + _dot_tn(tms[h], inner[h])), 0.0) * _decay(gcs[h], grow_ref[h, 0], causal)
                for h in hh]
        dkbs = [dkbgs[h] * gams[h] + _dot(dads[h], ks[h]) for h in hh]
        dk2 = [_dot_tn(dads[h], kbs[h]) for h in hh]
        for h in hh:
            dk_ref[h] = dks_ref[h] + dk2[h] + dkbs[h] * betas[h]
            dv_ref[h] = dvbs[h] * betas[h]
            ea = dads[h] * kks[h]
            dcols_ref[h, :, 0:1] = (dcs_ref[h] + jnp.sum(dkbgs[h] * kbgs[h], axis=-1, keepdims=True)
                                    + jnp.sum(ea, axis=-1, keepdims=True))
            dcols_ref[h, :, 1:2] = (jnp.sum(dvbs[h] * vs[h], axis=-1, keepdims=True)
                                    + jnp.sum(dkbs[h] * ks[h], axis=-1, keepdims=True))
            drow_ref[h, 0] = drs_ref[h, 0] - jnp.sum(ea, axis=0, keepdims=True)

    return _pc(body, name="gdn_prep_bwd", grid=(bh_n // hp, nblk),
               in_specs=[qkv_spec(1), qkv_spec(2), _gate_cols_spec(nblk, rows, n_heads=hp), rv(), ph(rows), ph(LANES), ph(LANES), ph(LANES), ph(1), rv()],
               out_specs=[ph(LANES), ph(LANES), ph(2), rv()],
               out_shape=[jax.ShapeDtypeStruct((bh_n, s, LANES), f32), jax.ShapeDtypeStruct((bh_n, s, LANES), f32),
                          jax.ShapeDtypeStruct((bh_n, s, 2), f32), jax.ShapeDtypeStruct((bh_n, nblk, 1, rows), f32)],
               sem=("parallel", "parallel"))(qkv, qkv, cols, grow, tmat, du, dw, dk_scan, dcol_scan, drow_scan)


def _row(v):
    return v.reshape(1, -1).astype(f32)


def _pad_lanes(v):
    v = v.reshape(1, -1).astype(f32)
    return jnp.pad(v, ((0, 0), (0, LANES - v.shape[1])))


def _local_step(x, tgt, p, need, emit):
    bl, s, d = x.shape
    t = bl * s
    n_heads = p["gdn_a_log"].shape[-1]
    assert d == n_heads * LANES and s % CHUNK == 0
    x2 = x.reshape(t, d)
    tgt2 = tgt.reshape(t, d)
    gr = {}

    n0, ub, gl = _pw1_glu(x2, _row(p["norm_mix_g"][0]), p["cv_w_pw1"], _row(p["cv_b_pw1"]))
    cv = need("conv", n0)
    dc = _dwconv_fwd(gl.reshape(bl, s, d), cv["cv_w_dw"], _row(p["cv_b_dw"]), "dwconv_fwd").reshape(t, d)
    sb, h1 = _ln_silu_mm_res(dc, _row(p["cv_ln_g"]), _row(p["cv_ln_b"]), cv["cv_w_pw2"], _row(p["cv_b_pw2"]), x2)
    m0 = need("mlp0", h1)
    n1, f0, r0, h2 = _mlp_fwd(h1, _row(p["norm_ffn_g"][0]), m0["w1"], m0["w2"], "mlp_fwd0")

    gd = need("gdn", h2)
    w_in = gd["w_in"]
    w_ab = jnp.pad(w_in[:, 4 * d:], ((0, 0), (0, LANES - 2 * n_heads)))
    a_log_pad = _pad_lanes(p["gdn_a_log"])
    dt_pad = _pad_lanes(p["gdn_dt_bias"])
    n2, qkv_pre, z, ab, gbeta = _gdn_in(h2, _row(p["norm_mix_g"][1]), w_in, w_ab, a_log_pad, dt_pad, n_heads)
    zero_bias = jnp.zeros((1, 3 * d), f32)
    qkv = _dwconv_fwd(qkv_pre.reshape(bl, s, 3 * d), cv["gdn_conv_w"], zero_bias, "sconv_fwd", qk_heads=n_heads).reshape(t, 3 * d)
    bh_n, rows = bl * n_heads, _group_rows(s)
    cols, gc_t = _gate_cumsum(gbeta, bl, s, n_heads)
    grow = gc_t.reshape(bh_n, s // rows, 1, rows)
    u, w, tmat = _gdn_prep(qkv, cols, grow, bl, s, n_heads)
    o, vn, ss = _gdn_scan(qkv, u, w, cols, grow, bl, s, n_heads)
    onb, h3 = _gated_norm_mm_res(o, z, _row(p["gdn_norm_g"]), gd["w_out"], h2, n_heads)
    m1 = need("mlp1", h3)
    n3, f1, r1, loss_acc, dh4, dgf = _mlp_fwd(h3, _row(p["norm_ffn_g"][1]), m1["w1"], m1["w2"], "mlp_fwd1_loss",
                                              loss=(_row(p["final_norm_g"]), tgt2))
    loss = loss_acc[0, 0]
    gr["loss"] = loss_acc[0, 0:1]
    gr["final_norm_g"] = dgf[0]

    df1, dh3, dg_ffn1, _ = _mlp_bwd(dh4, h3, _row(p["norm_ffn_g"][1]), f1, m1["w1"], m1["w2"], "mlp_bwd1")
    dw2_1 = _mm_tn(r1, dh4, "dw_mlp2_1", bf16)
    dw1_1 = _mm_tn_blocked(n3, df1, "dw_mlp1_1", bf16)

    dw_out = _mm_tn(onb, dh3, "dw_gdn_out", bf16)
    tie = emit("late", {"mlp_w2_1": dw2_1, "mlp_w1_1": dw1_1, "gdn_w_out": dw_out})
    do, dz, dng = _mm_nt([(dh3, gd["w_out"])], "dx_gdn_out",
                         _gated_norm_bwd_tail(o, z, _row(p["gdn_norm_g"]) + tie, n_heads))
    gr["gdn_norm_g"] = dng[0]
    du, dw_, dq, dk_scan, dcol_scan, drow_scan = _gdn_scan_bwd(do, qkv, w, vn, cols, grow, ss, bl, s, n_heads)
    dk, dv, dcols, drow = _gdn_prep_bwd(qkv, cols, grow, tmat, du, dw_, dk_scan, dcol_scan, drow_scan, bl, s, n_heads)
    lanes_shape = (bh_n * (s // rows), rows)
    dg_lanes = _lane_cumsum(dcols[..., 0].reshape(lanes_shape), drow.reshape(lanes_shape), True, "gdn_gate_cumsum_bwd")
    dqkv_pre, dconv_w, _ = _dwconv_bwd(qkv_pre.reshape(bl, s, 3 * d), [dq, dk, dv], cv["gdn_conv_w"], "sconv_bwd", qk_heads=n_heads)
    gr["gdn_conv_w"] = dconv_w[:cv["gdn_conv_w"].shape[0]]
    dgb_t = jnp.stack([dg_lanes.reshape(bl, n_heads, s), dcols[..., 1].reshape(bl, n_heads, s)])
    dgb_t = dgb_t.transpose(0, 2, 1, 3).reshape(2 * n_heads, t)
    dab, dal, ddt = _gates_bwd(dgb_t, ab, a_log_pad, dt_pad, n_heads)
    gr["gdn_a_log"] = dal[0, :n_heads]
    gr["gdn_dt_bias"] = ddt[0, :n_heads]
    dqkv2 = dqkv_pre.reshape(t, 3 * d)
    dw_in = jnp.concatenate(
        [_mm_tn(n2, dqkv2, "dw_gdn_in_qkv"), _mm_tn(n2, dz, "dw_gdn_in_z"), _mm_tn(n2, dab, "dw_gdn_in_ab")[:, :2 * n_heads]], axis=1)
    tie = emit("gdn_in", {"gdn_w_in": dw_in})
    dh2, dg_mix1 = _mm_nt([(dqkv2, (w_in, 0)), (dz, (w_in, 3)), (dab, w_ab)], "dx_gdn_in",
                          _rms_bwd_tail(h2, _row(p["norm_mix_g"][1]) + tie, dh3))

    df0, dh1, dg_ffn0, cs_h1 = _mlp_bwd(dh2, h1, _row(p["norm_ffn_g"][0]), f0, m0["w1"], m0["w2"], "mlp_bwd0")
    dw2_0 = _mm_tn(r0, dh2, "dw_mlp2_0", bf16)
    dw1_0 = _mm_tn_blocked(n1, df0, "dw_mlp1_0", bf16)
    dw_pw2 = _mm_tn(sb, dh1, "dw_pw2", bf16)
    tie = emit("mlp0", {"mlp_w2_0": dw2_0, "mlp_w1_0": dw1_0, "cv_w_pw2": dw_pw2})
    gr["norm_ffn_g"] = jnp.stack([dg_ffn0[0], dg_ffn1[0]])

    gr["cv_b_pw2"] = cs_h1[0]
    ddc, dlng, dlnb, cs_dc = _mm_nt([(dh1, cv["cv_w_pw2"])], "dx_pw2",
                                    _ln_silu_bwd_tail(dc, _row(p["cv_ln_g"]) + tie, _row(p["cv_ln_b"])))
    gr["cv_ln_g"] = dlng[0]
    gr["cv_ln_b"] = dlnb[0]
    gr["cv_b_dw"] = cs_dc[0]
    dgl, dw_dw, _ = _dwconv_bwd(gl.reshape(bl, s, d), [ddc.reshape(bl, s, d)], cv["cv_w_dw"], "dwconv_bwd")
    gr["cv_w_dw"] = dw_dw[:cv["cv_w_dw"].shape[0]]
    dub, cs_u = _glu_bwd(dgl.reshape(t, d), ub)
    gr["cv_b_pw1"] = cs_u[0]
    dw_pw1 = _mm_tn_blocked(n0, dub, "dw_pw1", bf16)
    tie = emit("last", {"cv_w_pw1": dw_pw1, "small": gr})
    dx, dg_mix0 = _mm_nt([(dub, ("blocks", p["cv_w_pw1"]))], "dx_pw1",
                         _rms_bwd_tail(x2, _row(p["norm_mix_g"][0]) + tie, dh1))
    return loss, dx.reshape(bl, s, d), {"norm_mix_g": jnp.stack([dg_mix0[0], dg_mix1[0]])}


ANY = pl.BlockSpec(memory_space=pl.ANY)
MESH = pl.DeviceIdType.MESH


def _flip(v, bit):
    return 1 - v if bit else v


def _all_gather_many(shards):
    na = len(shards)

    def body(*refs):
        x_refs, o_refs = refs[:na], refs[na:2 * na]
        send_sems, recv_sems, local_sems = refs[2 * na:]
        x, y, c = lax.axis_index("x"), lax.axis_index("y"), lax.axis_index("c")
        me, sibling = (x, y, c), (x, y, 1 - c)
        chips = [(1 - x, y), (x, 1 - y), (1 - x, 1 - y)]

        def copy(a, k, block, to, src=None):
            px, py, pc = block
            dst = o_refs[a].at[4 * px + 2 * py + pc]
            return pltpu.make_async_remote_copy(
                src_ref=dst if src is None else src, dst_ref=dst,
                send_sem=send_sems.at[a, k], recv_sem=recv_sems.at[a, k], device_id=to, device_id_type=MESH)

        mine = [pltpu.make_async_copy(x_refs[a], o_refs[a].at[4 * x + 2 * y + c], local_sems.at[a]) for a in range(na)]
        first = []
        for a in range(na):
            first.append(copy(a, 0, me, sibling, src=x_refs[a]))
            first += [copy(a, 1 + j, me, (*chip, c), src=x_refs[a]) for j, chip in enumerate(chips)]
        for cp in mine + first:
            cp.start()
        passed = []
        for j, chip in enumerate(chips):
            for a in range(na):
                copy(a, 1 + j, (*chip, c), me).wait_recv()
                fwd = copy(a, 4 + j, (*chip, c), sibling)
                fwd.start()
                passed.append(fwd)
        for a in range(na):
            copy(a, 0, sibling, me).wait_recv()
        for j, chip in enumerate(chips):
            for a in range(na):
                copy(a, 4 + j, (*chip, 1 - c), me).wait_recv()
        for cp in first + passed:
            cp.wait_send()
        for cp in mine:
            cp.wait()

    return pl.pallas_call(
        body, name="weights_all_gather",
        out_shape=[jax.ShapeDtypeStruct((N_DEV,) + a.shape, a.dtype) for a in shards],
        in_specs=[ANY] * na, out_specs=[ANY] * na,
        scratch_shapes=[pltpu.SemaphoreType.DMA((na, 7)), pltpu.SemaphoreType.DMA((na, 7)), pltpu.SemaphoreType.DMA((na,))],
        compiler_params=pltpu.CompilerParams(has_side_effects=True),
    )(*shards)


HBM = pl.BlockSpec(memory_space=pltpu.HBM)
SEM = pl.BlockSpec(memory_space=pltpu.SEMAPHORE)
EFFECT = pltpu.SideEffectType.DATAFLOW_SIDE_EFFECTING
N_PEERS = N_DEV - 1


def _exchange_copies(src_refs, land_refs, send_sems, recv_sems, scatter):
    x, y, c = lax.axis_index("x"), lax.axis_index("y"), lax.axis_index("c")
    me = 4 * x + 2 * y + c
    copies = []
    for a, (src, land) in enumerate(zip(src_refs, land_refs)):
        for k in range(1, N_DEV):
            px, py, pc = _flip(x, k & 4), _flip(y, k & 2), _flip(c, k & 1)
            i = a * N_PEERS + k - 1
            copies.append(pltpu.make_async_remote_copy(
                src_ref=src.at[4 * px + 2 * py + pc] if scatter[a] else src, dst_ref=land.at[me],
                send_sem=send_sems.at[i], recv_sem=recv_sems.at[i], device_id=(px, py, pc), device_id_type=MESH))
    return copies


def _exchange_start(srcs, scatter, name):
    na = len(srcs)
    lands = [lax.empty(s.shape if sc else (N_DEV,) + s.shape, s.dtype) for s, sc in zip(srcs, scatter)]

    def body(*refs):
        copies = _exchange_copies(refs[:na], refs[na:2 * na], refs[2 * na], refs[2 * na + 1], scatter)
        for cp in copies:
            cp.start()
        token = refs[-1]
        token[...] = jnp.zeros_like(token)

    outs = pl.pallas_call(
        body, name=name,
        out_shape=(pltpu.SemaphoreType.DMA((na * N_PEERS,)), pltpu.SemaphoreType.DMA((na * N_PEERS,)))
        + tuple(pltpu.HBM(a.shape, a.dtype) for a in srcs + lands) + (jax.ShapeDtypeStruct((SUBLANES, LANES), f32),),
        in_specs=[HBM] * (2 * na),
        out_specs=(SEM, SEM) + (HBM,) * (2 * na) + (pl.BlockSpec(memory_space=pltpu.VMEM),),
        input_output_aliases={i: 2 + i for i in range(2 * na)},
        compiler_params=pltpu.CompilerParams(has_side_effects=EFFECT),
    )(*[pltpu.with_memory_space_constraint(a, pltpu.HBM) for a in srcs + lands])
    return outs[0], outs[1], list(outs[2:2 + na]), list(outs[2 + na:2 + 2 * na]), outs[-1]


def _exchange_wait(started, after, scatter, name):
    send_sems, recv_sems, srcs, lands, _ = started
    na = len(srcs)

    def body(*refs):
        for cp in _exchange_copies(refs[:na], refs[na:2 * na], refs[2 * na], refs[2 * na + 1], scatter):
            cp.wait_send()
            cp.wait_recv()

    outs = pl.pallas_call(
        body, name=name,
        out_shape=tuple(pltpu.HBM(a.shape, a.dtype) for a in srcs + lands),
        in_specs=[HBM] * (2 * na) + [SEM, SEM, ANY], out_specs=(HBM,) * (2 * na),
        input_output_aliases={i: i for i in range(2 * na)},
        compiler_params=pltpu.CompilerParams(has_side_effects=EFFECT),
    )(*srcs, *lands, send_sems, recv_sems, after)
    return list(outs[:na]), list(outs[na:])


def _own_block(land, block, me):
    return lax.dynamic_update_index_in_dim(land, block, me, 0)


def _sum8_adamw(r2, w, m, v, layer, name):
    _, rr, cc = r2.shape
    tr = _tile(rr, 256)
    bc1 = 1.0 - ADAM_B1 ** ADAM_STEP
    bc2 = 1.0 - ADAM_B2 ** ADAM_STEP

    def body(r_ref, w_ref, m_ref, v_ref, g_ref, d_ref, nm_ref, nv_ref):
        gv = r_ref[0].astype(f32)
        for q in range(1, N_DEV):
            gv = gv + r_ref[q].astype(f32)
        g_ref[...] = gv
        nm = ADAM_B1 * m_ref[...] + (1.0 - ADAM_B1) * gv
        nv = ADAM_B2 * v_ref[...] + (1.0 - ADAM_B2) * (gv * gv)
        nm_ref[...] = nm
        nv_ref[...] = nv
        d_ref[...] = -ADAM_LR * ((nm / bc1) / (jnp.sqrt(nv / bc2) + ADAM_EPS) + ADAM_WD * w_ref[...])

    lspec = pl.BlockSpec((None, tr, cc), lambda i: (layer, i, 0))
    return _pc(body, name=name, grid=(rr // tr,),
               in_specs=[pl.BlockSpec((N_DEV, tr, cc), lambda i: (0, i, 0)), lspec, lspec, lspec],
               out_specs=[_rows(tr, cc)] * 4, out_shape=[jax.ShapeDtypeStruct((rr, cc), f32)] * 4,
               sem=("parallel",))(r2, w, m, v)


def _sum_devices(recv, name):
    _, rr, _ = recv.shape
    tr = _tile(rr, 512)

    def body(r_ref, o_ref):
        acc = r_ref[0]
        for i in range(1, N_DEV):
            acc = acc + r_ref[i]
        o_ref[...] = acc

    return _pc(body, name=name, grid=(rr // tr,),
               in_specs=[pl.BlockSpec((N_DEV, tr, LANES), lambda i: (0, i, 0))],
               out_specs=_rows(tr, LANES), out_shape=jax.ShapeDtypeStruct((rr, LANES), f32), sem=("parallel",))(recv)


def _adamw(w, g, m, v):
    rr = w.shape[0]
    tr = _tile(rr, 512)
    bc1 = 1.0 - ADAM_B1 ** ADAM_STEP
    bc2 = 1.0 - ADAM_B2 ** ADAM_STEP

    def body(w_ref, g_ref, m_ref, v_ref, d_ref, nm_ref, nv_ref):
        gv = g_ref[...]
        nm = ADAM_B1 * m_ref[...] + (1.0 - ADAM_B1) * gv
        nv = ADAM_B2 * v_ref[...] + (1.0 - ADAM_B2) * (gv * gv)
        nm_ref[...] = nm
        nv_ref[...] = nv
        d_ref[...] = -ADAM_LR * ((nm / bc1) / (jnp.sqrt(nv / bc2) + ADAM_EPS) + ADAM_WD * w_ref[...])

    spec = _rows(tr, LANES)
    return _pc(body, name="adamw", grid=(rr // tr,), in_specs=[spec] * 4, out_specs=[spec] * 3,
               out_shape=[jax.ShapeDtypeStruct((rr, LANES), f32)] * 3, sem=("parallel",))(w, g, m, v)


PACK_ROWS = 512
PART_ROWS = SUBLANES


def _pack(arrs):
    parts, sizes = [], []
    for a in arrs:
        flat = a.reshape(-1)
        n = flat.shape[0]
        rows = -(-n // (LANES * PART_ROWS)) * PART_ROWS
        if rows * LANES != n:
            flat = jnp.pad(flat, (0, rows * LANES - n))
        parts.append(flat.reshape(rows, LANES))
        sizes.append((rows, n))
    total = sum(r for r, _ in sizes)
    padded = -(-total // PACK_ROWS) * PACK_ROWS
    if padded > total:
        parts.append(jnp.zeros((padded - total, LANES), parts[0].dtype))
    return jnp.concatenate(parts, axis=0), sizes


def _unpack(packed, sizes, shapes):
    out, off = [], 0
    for (rows, n), shp in zip(sizes, shapes):
        piece = lax.slice_in_dim(packed, off, off + rows, axis=0)
        if rows * LANES != n:
            piece = lax.slice_in_dim(piece.reshape(-1), 0, n, axis=0)
        out.append(piece.reshape(tuple(shp)))
        off += rows
    return out


def _cols_to_blocks(a):
    n = a.shape[-1] // N_DEV
    a = a.reshape(a.shape[:-1] + (N_DEV, n))
    return jnp.moveaxis(a, -2, 0)


def _blocks_to_cols(a):
    a = jnp.moveaxis(a, 0, -2)
    return a.reshape(a.shape[:-2] + (a.shape[-2] * a.shape[-1],))


def _rows_to_blocks(a):
    k = a.shape[-2] // N_DEV
    a = a.reshape(a.shape[:-2] + (N_DEV, k, a.shape[-1]))
    return jnp.moveaxis(a, -3, 0)


def _blocks_to_rows(a):
    a = jnp.moveaxis(a, 0, -3)
    return a.reshape(a.shape[:-3] + (a.shape[-3] * a.shape[-2], a.shape[-1]))


COL_SHARDED = ("cv_w_pw1", "gdn_w_in", "mlp_w1")
ROW_SHARDED = ("cv_w_pw2", "gdn_w_out", "mlp_w2")
CONV_SHARDED = ("cv_w_dw", "gdn_conv_w")
REPLICATED = ("norm_mix_g", "norm_ffn_g", "final_norm_g", "cv_b_pw1", "cv_b_dw", "cv_ln_g", "cv_ln_b", "cv_b_pw2",
              "gdn_a_log", "gdn_dt_bias", "gdn_norm_g")
WEIGHTS = ("norm_mix_g", "norm_ffn_g", "final_norm_g", "cv_w_pw1", "cv_b_pw1", "cv_w_dw", "cv_b_dw", "cv_ln_g",
           "cv_ln_b", "cv_w_pw2", "cv_b_pw2", "gdn_w_in", "gdn_conv_w", "gdn_a_log", "gdn_dt_bias", "gdn_norm_g",
           "gdn_w_out", "mlp_w1", "mlp_w2")
MATMUL_SHARDED = COL_SHARDED + ROW_SHARDED


def _squeeze_layer(name, a):
    if name in ("norm_mix_g", "norm_ffn_g", "final_norm_g", "mlp_w1", "mlp_w2"):
        return a
    return a[0]


def _gather_weights(shards):
    me = 4 * lax.axis_index("x") + 2 * lax.axis_index("y") + lax.axis_index("c")
    pw1 = _all_gather_many([shards["cv_w_pw1"].astype(bf16)])[0]
    now = {"cv_w_pw1": pw1}

    def cast(a, tie):
        return (a + tie).astype(bf16)

    token, _ = lax.optimization_barrier((jnp.zeros((), f32), pw1))
    later, started = {}, {}
    for group in ("conv", "mlp0", "gdn", "mlp1"):
        if group == "conv":
            srcs = [cast(shards["cv_w_pw2"], token)] + [shards[n] + token for n in CONV_SHARDED]
        elif group == "gdn":
            srcs = [cast(shards["gdn_w_in"], token).reshape(-1, LANES), cast(shards["gdn_w_out"], token)]
        else:
            layer = int(group[-1])
            srcs = [cast(shards["mlp_w1"][layer], token), cast(shards["mlp_w2"][layer], token)]
        later[group] = srcs
        started[group] = _exchange_start(srcs, [False] * len(srcs), f"weights_{group}_start")
        token = started[group][4][0, 0]

    def need(group, after):
        srcs, lands = _exchange_wait(started[group], after, [False] * len(later[group]), f"weights_{group}_wait")
        lands = [_own_block(ld, own, me) for ld, own in zip(lands, srcs)]
        if group == "conv":
            out = {"cv_w_pw2": _blocks_to_rows(lands[0])}
            out.update({n: _blocks_to_cols(ld) for n, ld in zip(CONV_SHARDED, lands[1:])})
            return out
        if group == "gdn":
            w_in = _blocks_to_cols(lands[0].reshape((N_DEV,) + shards["gdn_w_in"].shape))
            return {"w_in": w_in, "w_out": _blocks_to_rows(lands[1])}
        return {"w1": lands[0], "w2": lands[1]}

    return now, need, token


def kernel(x, norm_mix_g, norm_ffn_g, final_norm_g, cv_w_pw1, cv_b_pw1, cv_w_dw, cv_b_dw, cv_ln_g, cv_ln_b, cv_w_pw2, cv_b_pw2, gdn_w_in, gdn_conv_w, gdn_a_log, gdn_dt_bias, gdn_norm_g, gdn_w_out, mlp_w1, mlp_w2, loss_target, m_norm_mix_g, m_norm_ffn_g, m_final_norm_g, m_cv_w_pw1, m_cv_b_pw1, m_cv_w_dw, m_cv_b_dw, m_cv_ln_g, m_cv_ln_b, m_cv_w_pw2, m_cv_b_pw2, m_gdn_w_in, m_gdn_conv_w, m_gdn_a_log, m_gdn_dt_bias, m_gdn_norm_g, m_gdn_w_out, m_mlp_w1, m_mlp_w2, v_norm_mix_g, v_norm_ffn_g, v_final_norm_g, v_cv_w_pw1, v_cv_b_pw1, v_cv_w_dw, v_cv_b_dw, v_cv_ln_g, v_cv_ln_b, v_cv_w_pw2, v_cv_b_pw2, v_gdn_w_in, v_gdn_conv_w, v_gdn_a_log, v_gdn_dt_bias, v_gdn_norm_g, v_gdn_w_out, v_mlp_w1, v_mlp_w2):
    w_in = dict(zip(WEIGHTS, (norm_mix_g, norm_ffn_g, final_norm_g, cv_w_pw1, cv_b_pw1, cv_w_dw, cv_b_dw, cv_ln_g, cv_ln_b, cv_w_pw2, cv_b_pw2, gdn_w_in, gdn_conv_w, gdn_a_log, gdn_dt_bias, gdn_norm_g, gdn_w_out, mlp_w1, mlp_w2)))
    m_in = dict(zip(WEIGHTS, (m_norm_mix_g, m_norm_ffn_g, m_final_norm_g, m_cv_w_pw1, m_cv_b_pw1, m_cv_w_dw, m_cv_b_dw, m_cv_ln_g, m_cv_ln_b, m_cv_w_pw2, m_cv_b_pw2, m_gdn_w_in, m_gdn_conv_w, m_gdn_a_log, m_gdn_dt_bias, m_gdn_norm_g, m_gdn_w_out, m_mlp_w1, m_mlp_w2)))
    v_in = dict(zip(WEIGHTS, (v_norm_mix_g, v_norm_ffn_g, v_final_norm_g, v_cv_w_pw1, v_cv_b_pw1, v_cv_w_dw, v_cv_b_dw, v_cv_ln_g, v_cv_ln_b, v_cv_w_pw2, v_cv_b_pw2, v_gdn_w_in, v_gdn_conv_w, v_gdn_a_log, v_gdn_dt_bias, v_gdn_norm_g, v_gdn_w_out, v_mlp_w1, v_mlp_w2)))
    me = 4 * lax.axis_index("x") + 2 * lax.axis_index("y") + lax.axis_index("c")

    shards = {n: _squeeze_layer(n, w_in[n]) for n in WEIGHTS}
    first, need, token = _gather_weights(shards)
    params = {n: shards[n] for n in REPLICATED}
    params.update(first)
    params["norm_mix_g"] = params["norm_mix_g"] + token

    def row_blocks(a):
        return a.reshape(N_DEV, a.shape[0] // N_DEV, a.shape[1])

    def flat_blocks(a):
        k, n8 = a.shape
        return _cols_to_blocks(a).reshape(N_DEV, k * (n8 // N_DEV) // LANES, LANES)

    def as_blocks(n, a):
        if n == "gdn_w_in":
            return flat_blocks(a).astype(bf16)
        return a if a.ndim == 3 else row_blocks(a)

    sent = []

    small = REPLICATED + CONV_SHARDED
    small_early = tuple(n for n in small if n != "norm_mix_g") + ("loss",)
    small_info = {}

    def emit(group, grads_out):
        names, blocks, scatter = list(grads_out), [], []
        for n in names:
            if n == "small":
                packed, small_info["sizes"] = _pack([grads_out[n][k] for k in small_early])
                small_info["shapes"] = [grads_out[n][k].shape for k in small_early]
                blocks.append(packed)
                scatter.append(False)
            else:
                blocks.append(as_blocks(n, grads_out[n]))
                scatter.append(True)
        sent.append((names, _exchange_start(blocks, scatter, f"grads_{group}_start"), scatter))
        return sent[-1][1][4][0, 0]

    _, grad_x, gr = _local_step(x, loss_target, params, need, emit)
    nm_shape = gr["norm_mix_g"].shape
    sent.append((["norm_mix_g"], _exchange_start([gr["norm_mix_g"].reshape(-1, LANES)], [False], "grads_norm_mix_start"), [False]))

    recv = {}

    def finish(entry, after):
        names, st, scatter = entry
        srcs, lands = _exchange_wait(st, after, scatter, f"grads_{names[0]}_wait")
        for n, ld, blk, sc in zip(names, lands, srcs, scatter):
            own = lax.dynamic_index_in_dim(blk, me, 0, keepdims=False) if sc else blk
            recv[n] = _own_block(ld, own, me)
        return lands[0]

    def as3d(n, a):
        if n == "gdn_w_in":
            return a.reshape(a.shape[0], -1, LANES)
        return a

    big = [("cv_w_pw2", 0, "cv_w_pw2"), ("gdn_w_in", 0, "gdn_w_in"), ("gdn_w_out", 0, "gdn_w_out"),
           ("mlp_w1", 0, "mlp_w1_0"), ("mlp_w1", 1, "mlp_w1_1"), ("mlp_w2", 0, "mlp_w2_0"), ("mlp_w2", 1, "mlp_w2_1"),
           ("cv_w_pw1", 0, "cv_w_pw1")]
    res = {n: {} for n in MATMUL_SHARDED}
    after = grad_x
    for entry in sent[:-2]:
        after = finish(entry, after)
    for n, layer, key in big[:-1]:
        res[n][layer] = _sum8_adamw(recv[key], as3d(n, w_in[n]), as3d(n, m_in[n]), as3d(n, v_in[n]), layer, f"adamw_{key}")
        after = res[n][layer][0]
    after = finish(sent[-2], after)
    finish(sent[-1], after)
    n, layer, key = big[-1]
    res[n][layer] = _sum8_adamw(recv[key], as3d(n, w_in[n]), as3d(n, m_in[n]), as3d(n, v_in[n]), layer, f"adamw_{key}")

    grads = dict(zip(small_early, _unpack(_sum_devices(recv["small"], "grads_small_sum"), small_info["sizes"], small_info["shapes"])))
    grads["norm_mix_g"] = _sum_devices(recv["norm_mix_g"], "grads_norm_mix_sum").reshape(nm_shape)
    loss = grads["loss"].reshape(())
    for n in CONV_SHARDED:
        cn = shards[n].shape[-1]
        grads[n] = lax.dynamic_slice_in_dim(grads[n], me * cn, cn, axis=1)

    out_groups = {n: [] for n in WEIGHTS}
    for n in MATMUL_SHARDED:
        layers = sorted(res[n])
        for k in range(4):
            pieces = [res[n][layer][k] for layer in layers]
            out_groups[n].append(jnp.stack(pieces).reshape(w_in[n].shape))

    sm_w = [shards[n] for n in small]
    sm_g = [grads[n].reshape(shards[n].shape) for n in small]
    sm_m = [_squeeze_layer(n, m_in[n]) for n in small]
    sm_v = [_squeeze_layer(n, v_in[n]) for n in small]
    wp, psz = _pack(sm_w)
    gp, _ = _pack(sm_g)
    mp, _ = _pack(sm_m)
    vp, _ = _pack(sm_v)
    dp, nmp, nvp = _adamw(wp, gp, mp, vp)
    shp = [a.shape for a in sm_w]
    for n, g, dl, nm, nv in zip(small, sm_g, _unpack(dp, psz, shp), _unpack(nmp, psz, shp), _unpack(nvp, psz, shp)):
        out_groups[n] = [a.reshape(w_in[n].shape) for a in (g, dl, nm, nv)]

    outs = [loss, grad_x]
    for k in range(4):
        outs += [out_groups[n][k] for n in WEIGHTS]
    return tuple(outs)
```

```python
import jax
import jax.numpy as jnp
from jax import lax
from jax.experimental import pallas as pl
from jax.experimental.pallas import tpu as pltpu

f32, bf16 = jnp.float32, jnp.bfloat16

NORM_EPS = 1e-6
L2_EPS = 1e-6
CHUNK = 64
LANES = 128
SUBLANES = 8
N_DEV = 8
VMEM_LIMIT = 56 * 1024 * 1024
ROW_TILE = 512
CONV_PAD = 32
NEG = -1e30

ADAM_LR, ADAM_B1, ADAM_B2, ADAM_EPS, ADAM_WD, ADAM_STEP = 0.001, 0.9, 0.999, 1e-08, 0.01, 10

NT = (((1,), (1,)), ((), ()))
TN = (((0,), (0,)), ((), ()))
HI = lax.Precision.HIGHEST


def _pc(body, *, name, grid, in_specs, out_specs, out_shape, scratch=(), sem=None):
    return pl.pallas_call(
        body, name=name, grid=grid, in_specs=in_specs, out_specs=out_specs, out_shape=out_shape,
        scratch_shapes=list(scratch),
        compiler_params=pltpu.CompilerParams(dimension_semantics=sem, vmem_limit_bytes=VMEM_LIMIT))


def _rows(tm, n):
    return pl.BlockSpec((tm, n), lambda i: (i, 0))


def _const(shape):
    return pl.BlockSpec(shape, lambda *_: (0,) * len(shape))


def _resident(shape):
    return pl.BlockSpec(shape, lambda *_: (0,) * len(shape), pipeline_mode=pl.Buffered(1))


def _tile(t, pref):
    return pref if t % pref == 0 else t


def _dot(a, b):
    return jnp.dot(a.astype(bf16), b.astype(bf16), preferred_element_type=f32)


def _dot_nt(a, b):
    return lax.dot_general(a.astype(bf16), b.astype(bf16), NT, preferred_element_type=f32)


def _dot_tn(a, b):
    return lax.dot_general(a.astype(bf16), b.astype(bf16), TN, preferred_element_type=f32)


def _sigmoid(x):
    return 1.0 / (1.0 + jnp.exp(-x))


def _silu_grad(x):
    s = _sigmoid(x)
    return s * (1.0 + x * (1.0 - s))


def _rms(x, g):
    rstd = lax.rsqrt(jnp.mean(x * x, axis=-1, keepdims=True) + NORM_EPS)
    xh = x * rstd
    return xh * g, xh, rstd


def _rms_bwd(dn, xh, rstd, g):
    dxh = dn * g
    return rstd * (dxh - xh * jnp.mean(dxh * xh, axis=-1, keepdims=True))


def _acc_init(step, *refs):
    @pl.when(step == 0)
    def _():
        for r in refs:
            r[...] = jnp.zeros(r.shape, r.dtype)


def _acc_rows(ref, val):
    ref[0:1, :] += jnp.sum(val, axis=0, keepdims=True)


def _pw1_glu(x, g, w, b):
    t, d = x.shape
    tm = _tile(t, ROW_TILE)

    nb_w = w.shape[0]

    def body(x_ref, g_ref, w_hbm, b_ref, n_ref, u_ref, gl_ref, w_ref, sems):
        _fetch_blocks(pl.program_id(0), w_hbm, w_ref, sems, True)
        n, _, _ = _rms(x_ref[...], g_ref[...])
        nb = n.astype(bf16)
        n_ref[...] = nb
        u = jnp.dot(nb, w_ref[...], preferred_element_type=f32) + b_ref[...]
        u_ref[...] = u.astype(bf16)
        gl_ref[...] = u[:, :d] * _sigmoid(u[:, d:])

    return _pc(body, name="pw1_glu", grid=(t // tm,),
               in_specs=[_rows(tm, d), _const((1, d)), ANY, _const((1, 2 * d))],
               out_specs=[_rows(tm, d), _rows(tm, 2 * d), _rows(tm, d)],
               out_shape=[jax.ShapeDtypeStruct((t, d), bf16), jax.ShapeDtypeStruct((t, 2 * d), bf16),
                          jax.ShapeDtypeStruct((t, d), f32)],
               scratch=[pltpu.VMEM((d, 2 * d), bf16), pltpu.SemaphoreType.DMA((nb_w,))],
               sem=("arbitrary",))(x, g, w, b)


def _ln_silu_mm_res(dc, ln_g, ln_b, w, b, res):
    t, d = dc.shape
    tm = _tile(t, ROW_TILE)

    def body(x_ref, g_ref, bb_ref, w_ref, b_ref, r_ref, s_ref, o_ref):
        x = x_ref[...]
        xc = x - jnp.mean(x, axis=-1, keepdims=True)
        rstd = lax.rsqrt(jnp.mean(xc * xc, axis=-1, keepdims=True) + NORM_EPS)
        ln = xc * rstd * g_ref[...] + bb_ref[...]
        sb = (ln * _sigmoid(ln)).astype(bf16)
        s_ref[...] = sb
        o_ref[...] = r_ref[...] + jnp.dot(sb, w_ref[...], preferred_element_type=f32) + b_ref[...]

    return _pc(body, name="ln_silu_pw2", grid=(t // tm,),
               in_specs=[_rows(tm, d), _const((1, d)), _const((1, d)), _resident((d, d)), _const((1, d)), _rows(tm, d)],
               out_specs=[_rows(tm, d), _rows(tm, d)],
               out_shape=[jax.ShapeDtypeStruct((t, d), bf16), jax.ShapeDtypeStruct((t, d), f32)],
               sem=("parallel",))(dc, ln_g, ln_b, w, b, res)


def _fetch_blocks(step, w_hbm, dst, sems, by_cols, layer=None):
    nb_w = w_hbm.shape[0]
    step_rows, step_cols = w_hbm.shape[-2], w_hbm.shape[-1]

    @pl.when(step == 0)
    def _():
        copies = []
        for j in range(nb_w):
            src = w_hbm.at[j] if layer is None else w_hbm.at[j, layer]
            if by_cols:
                part = dst.at[:, pl.ds(j * step_cols, step_cols)]
            else:
                part = dst.at[pl.ds(j * step_rows, step_rows), :]
            copies.append(pltpu.make_async_copy(src, part, sems.at[j]))
        for cp in copies:
            cp.start()
        for cp in copies:
            cp.wait()


def _mlp_fwd(h, g, w1g, w2g, name, loss=None):
    t, d = h.shape
    nb_w, _, bn = w1g.shape
    ff = nb_w * bn
    tm = _tile(t, 256)
    n_in = 4 if loss is None else 6
    n_out = 4 if loss is None else 6

    def body(*refs):
        h_ref, g_ref, w1_hbm, w2_hbm = refs[:4]
        n_ref, f_ref, r_ref = refs[n_in:n_in + 3]
        w1_ref, w2_ref, sem1, sem2 = refs[n_in + n_out:]
        _fetch_blocks(pl.program_id(0), w1_hbm, w1_ref, sem1, True)
        _fetch_blocks(pl.program_id(0), w2_hbm, w2_ref, sem2, False)
        hv = h_ref[...]
        n, _, _ = _rms(hv, g_ref[...])
        nb = n.astype(bf16)
        n_ref[...] = nb
        f = jnp.dot(nb, w1_ref[...], preferred_element_type=f32)
        f_ref[...] = f.astype(bf16)
        rb = jnp.square(jnp.maximum(f, 0.0)).astype(bf16)
        r_ref[...] = rb
        out = hv + jnp.dot(rb, w2_ref[...], preferred_element_type=f32)
        if loss is None:
            refs[n_in + 3][...] = out
        else:
            gf_ref, t_ref = refs[4:6]
            loss_ref, dh_ref, dg_ref = refs[n_in + 3:n_in + 6]
            _acc_init(pl.program_id(0), loss_ref, dg_ref)
            gv = gf_ref[...]
            y, xh, rstd = _rms(out, gv)
            e = y - t_ref[...]
            loss_ref[...] += 0.5 * jnp.sum(jnp.mean(e * e, axis=-1, keepdims=True))
            dy = e * (1.0 / d)
            _acc_rows(dg_ref, dy * xh)
            dh_ref[...] = _rms_bwd(dy, xh, rstd, gv)

    in_specs = [_rows(tm, d), _const((1, d)), ANY, ANY]
    out_specs = [_rows(tm, d), _rows(tm, ff), _rows(tm, ff)]
    out_shape = [jax.ShapeDtypeStruct((t, d), bf16), jax.ShapeDtypeStruct((t, ff), bf16), jax.ShapeDtypeStruct((t, ff), bf16)]
    args = [h, g, w1g, w2g]
    if loss is None:
        out_specs.append(_rows(tm, d))
        out_shape.append(jax.ShapeDtypeStruct((t, d), f32))
    else:
        in_specs += [_const((1, d)), _rows(tm, d)]
        args += list(loss)
        out_specs += [_const((SUBLANES, LANES)), _rows(tm, d), _const((SUBLANES, d))]
        out_shape += [jax.ShapeDtypeStruct((SUBLANES, LANES), f32), jax.ShapeDtypeStruct((t, d), f32),
                      jax.ShapeDtypeStruct((SUBLANES, d), f32)]
    return _pc(body, name=name, grid=(t // tm,), in_specs=in_specs, out_specs=out_specs, out_shape=out_shape,
               scratch=[pltpu.VMEM((d, ff), bf16), pltpu.VMEM((ff, d), bf16),
                        pltpu.SemaphoreType.DMA((nb_w,)), pltpu.SemaphoreType.DMA((nb_w,))],
               sem=("arbitrary",))(*args)


def _softplus(x):
    return jnp.maximum(x, 0.0) + jnp.log(1.0 + jnp.exp(-jnp.abs(x)))


def _gdn_in(h, g, w_main, w_ab, a_log_pad, dt_pad, n_heads):
    t, d = h.shape
    tm = _tile(t, ROW_TILE)
    gate_rows = 2 * n_heads

    def body(h_ref, g_ref, wm_ref, wab_ref, al_ref, dt_ref, n_ref, qkv_ref, z_ref, ab_ref, gb_ref):
        n, _, _ = _rms(h_ref[...], g_ref[...])
        nb = n.astype(bf16)
        n_ref[...] = nb
        p = jnp.dot(nb, wm_ref[...], preferred_element_type=f32)
        qkv_ref[...] = p[:, :3 * d]
        z_ref[...] = p[:, 3 * d:]
        ab = jnp.dot(nb, wab_ref[...], preferred_element_type=f32)
        ab_ref[...] = ab
        lane = lax.broadcasted_iota(jnp.int32, ab.shape, 1)
        decay = -jnp.exp(al_ref[...]) * _softplus(ab + dt_ref[...])
        gb_ref[...] = jnp.where(lane < n_heads, decay, jnp.where(lane < gate_rows, _sigmoid(ab), 0.0))

    return _pc(body, name="gdn_in", grid=(t // tm,),
               in_specs=[_rows(tm, d), _const((1, d)), _resident((d, 4 * d)), _resident((d, LANES)),
                         _const((1, LANES)), _const((1, LANES))],
               out_specs=[_rows(tm, d), _rows(tm, 3 * d), _rows(tm, d), _rows(tm, LANES), _rows(tm, LANES)],
               out_shape=[jax.ShapeDtypeStruct((t, d), bf16), jax.ShapeDtypeStruct((t, 3 * d), f32),
                          jax.ShapeDtypeStruct((t, d), f32), jax.ShapeDtypeStruct((t, LANES), f32),
                          jax.ShapeDtypeStruct((t, LANES), f32)],
               sem=("parallel",))(h, g, w_main, w_ab, a_log_pad, dt_pad)


def _gated_norm_mm_res(o, z, ng, w, res, n_heads):
    t, d = o.shape
    tm = _tile(t, ROW_TILE)

    def body(o_ref, z_ref, ng_ref, w_ref, r_ref, on_ref, out_ref):
        for hd in range(n_heads):
            sl = slice(hd * LANES, (hd + 1) * LANES)
            rn, _, _ = _rms(o_ref[:, sl], ng_ref[...])
            zz = z_ref[:, sl]
            on_ref[:, sl] = (rn * (zz * _sigmoid(zz))).astype(bf16)
        out_ref[...] = r_ref[...] + jnp.dot(on_ref[...], w_ref[...], preferred_element_type=f32)

    return _pc(body, name="gated_norm_wout", grid=(t // tm,),
               in_specs=[_rows(tm, d), _rows(tm, d), _const((1, LANES)), _resident((d, d)), _rows(tm, d)],
               out_specs=[_rows(tm, d), _rows(tm, d)],
               out_shape=[jax.ShapeDtypeStruct((t, d), bf16), jax.ShapeDtypeStruct((t, d), f32)],
               sem=("parallel",))(o, z, ng, w, res)


def _mlp_bwd(dho, h, g, fb, w1g, w2g, name):
    t, d = h.shape
    nb_w, _, bn = w1g.shape
    ff = nb_w * bn
    tm = _tile(t, 256)

    def body(do_ref, h_ref, g_ref, f_ref, w1_hbm, w2_hbm, df_ref, dh_ref, dg_ref, cs_ref, w1_ref, w2_ref, sem1, sem2):
        _fetch_blocks(pl.program_id(0), w1_hbm, w1_ref, sem1, True)
        _fetch_blocks(pl.program_id(0), w2_hbm, w2_ref, sem2, False)
        _acc_init(pl.program_id(0), dg_ref, cs_ref)
        do = do_ref[...]
        dr = lax.dot_general(do.astype(bf16), w2_ref[...], NT, preferred_element_type=f32)
        dfb = (dr * (2.0 * jnp.maximum(f_ref[...].astype(f32), 0.0))).astype(bf16)
        df_ref[...] = dfb
        dn = lax.dot_general(dfb, w1_ref[...], NT, preferred_element_type=f32)
        gv = g_ref[...]
        _, xh, rstd = _rms(h_ref[...], gv)
        _acc_rows(dg_ref, dn * xh)
        dh = do + _rms_bwd(dn, xh, rstd, gv)
        dh_ref[...] = dh
        _acc_rows(cs_ref, dh)

    return _pc(body, name=name, grid=(t // tm,),
               in_specs=[_rows(tm, d), _rows(tm, d), _const((1, d)), _rows(tm, ff), ANY, ANY],
               out_specs=[_rows(tm, ff), _rows(tm, d), _const((SUBLANES, d)), _const((SUBLANES, d))],
               out_shape=[jax.ShapeDtypeStruct((t, ff), bf16), jax.ShapeDtypeStruct((t, d), f32),
                          jax.ShapeDtypeStruct((SUBLANES, d), f32), jax.ShapeDtypeStruct((SUBLANES, d), f32)],
               scratch=[pltpu.VMEM((d, ff), bf16), pltpu.VMEM((ff, d), bf16),
                        pltpu.SemaphoreType.DMA((nb_w,)), pltpu.SemaphoreType.DMA((nb_w,))],
               sem=("arbitrary",))(dho, h, g, fb, w1g, w2g)


class _Tail:
    def __init__(self, fn, ins, outs):
        self.fn, self.ins, self.outs = fn, ins, outs


def _mm_nt(pairs, name, tail=None):
    t = pairs[0][0].shape[0]
    tm = _tile(t, ROW_TILE)
    npair = len(pairs)
    in_specs, args, scratch, blocked = [], [], [], []
    k = None
    for dy, w in pairs:
        nn = dy.shape[1]
        if isinstance(w, tuple) and isinstance(w[0], str):
            w = w[1]
            k = w.shape[1]
            wspec = ANY
            blocked.append(True)
            scratch += [pltpu.VMEM((k, nn), bf16), pltpu.SemaphoreType.DMA((w.shape[0],))]
        elif isinstance(w, tuple):
            w, idx = w
            k = w.shape[0]
            wspec = pl.BlockSpec((k, nn), lambda *_, idx=idx: (0, idx), pipeline_mode=pl.Buffered(1))
            blocked.append(False)
        else:
            k = w.shape[0]
            wspec = _resident(w.shape)
            blocked.append(False)
        in_specs += [_rows(tm, nn), wspec]
        args += [dy, w]
    n_tin = len(tail.ins) if tail else 0
    n_out = len(tail.outs) if tail else 1
    if tail:
        for arr, kind in tail.ins:
            in_specs.append(_rows(tm, arr.shape[1]) if kind == "rows" else _const(arr.shape))
            args.append(arr)
        out_specs = [_rows(tm, c) if kind == "rows" else _const((SUBLANES, c)) for c, kind in tail.outs]
        out_shape = [jax.ShapeDtypeStruct((t, c) if kind == "rows" else (SUBLANES, c), f32) for c, kind in tail.outs]
    else:
        out_specs = _rows(tm, k)
        out_shape = jax.ShapeDtypeStruct((t, k), f32)

    def body(*refs):
        step = pl.program_id(0)
        tin = refs[2 * npair:2 * npair + n_tin]
        outs = refs[2 * npair + n_tin:2 * npair + n_tin + n_out]
        scr = list(refs[2 * npair + n_tin + n_out:])
        acc = None
        for p in range(npair):
            w_ref = refs[2 * p + 1]
            if blocked[p]:
                w_vmem, sems = scr.pop(0), scr.pop(0)
                _fetch_blocks(step, w_ref, w_vmem, sems, True)
                w_ref = w_vmem
            part = lax.dot_general(refs[2 * p][...].astype(bf16), w_ref[...], NT, preferred_element_type=f32)
            acc = part if acc is None else acc + part
        if tail is None:
            outs[0][...] = acc
        else:
            _acc_init(step, *[o for o, (_, kind) in zip(outs, tail.outs) if kind == "acc"])
            tail.fn(acc, tin, outs)

    sequential = tail is not None or any(blocked)
    return _pc(body, name=name, grid=(t // tm,), in_specs=in_specs, out_specs=out_specs, out_shape=out_shape,
               scratch=scratch, sem=("arbitrary",) if sequential else ("parallel",))(*args)


def _rms_bwd_tail(h, g, dres):
    def fn(dn, ins, outs):
        h_ref, g_ref, dr_ref = ins
        dh_ref, dg_ref = outs
        gv = g_ref[...]
        _, xh, rstd = _rms(h_ref[...], gv)
        _acc_rows(dg_ref, dn * xh)
        dh_ref[...] = dr_ref[...] + _rms_bwd(dn, xh, rstd, gv)

    d = h.shape[1]
    return _Tail(fn, [(h, "rows"), (g, "const"), (dres, "rows")], [(d, "rows"), (d, "acc")])


def _ln_silu_bwd_tail(dc, ln_g, ln_b):
    def fn(ds, ins, outs):
        x_ref, g_ref, b_ref = ins
        dx_ref, dg_ref, db_ref, cs_ref = outs
        x = x_ref[...]
        gv = g_ref[...]
        xc = x - jnp.mean(x, axis=-1, keepdims=True)
        rstd = lax.rsqrt(jnp.mean(xc * xc, axis=-1, keepdims=True) + NORM_EPS)
        xh = xc * rstd
        dln = ds * _silu_grad(xh * gv + b_ref[...])
        _acc_rows(dg_ref, dln * xh)
        _acc_rows(db_ref, dln)
        dxh = dln * gv
        dx = rstd * (dxh - jnp.mean(dxh, axis=-1, keepdims=True) - xh * jnp.mean(dxh * xh, axis=-1, keepdims=True))
        dx_ref[...] = dx
        _acc_rows(cs_ref, dx)

    d = dc.shape[1]
    return _Tail(fn, [(dc, "rows"), (ln_g, "const"), (ln_b, "const")], [(d, "rows"), (d, "acc"), (d, "acc"), (d, "acc")])


def _gated_norm_bwd_tail(o, z, ng, n_heads):
    def fn(don_all, ins, outs):
        o_ref, z_ref, ng_ref = ins
        do_ref, dz_ref, dng_ref = outs
        gv = ng_ref[...]
        for hd in range(n_heads):
            sl = slice(hd * LANES, (hd + 1) * LANES)
            rn, xh, rstd = _rms(o_ref[:, sl], gv)
            zz = z_ref[:, sl]
            don = don_all[:, sl]
            dz_ref[:, sl] = don * rn * _silu_grad(zz)
            drn = don * (zz * _sigmoid(zz))
            _acc_rows(dng_ref, drn * xh)
            do_ref[:, sl] = _rms_bwd(drn, xh, rstd, gv)

    d = o.shape[1]
    return _Tail(fn, [(o, "rows"), (z, "rows"), (ng, "const")], [(d, "rows"), (d, "rows"), (LANES, "acc")])


def _mm_tn_blocked(x, dy, name, out_dtype=f32):
    t, k = x.shape
    bn = dy.shape[1] // N_DEV
    tm = _tile(t, 2048 if k <= 1024 else 1024)
    jb = N_DEV
    while jb > 1 and k * jb * bn * 4 > 8 * 1024 * 1024:
        jb //= 2
    nt = t // tm

    def body(x_ref, dy_ref, o_ref, *acc):
        acc_ref = acc[0] if acc else o_ref
        _acc_init(pl.program_id(1), acc_ref)
        xt = x_ref[...].astype(bf16).T
        for jj in range(jb):
            acc_ref[jj] += jnp.dot(xt, dy_ref[:, jj * bn:(jj + 1) * bn].astype(bf16), preferred_element_type=f32)
        if acc:
            @pl.when(pl.program_id(1) == nt - 1)
            def _():
                o_ref[...] = acc_ref[...].astype(out_dtype)

    return _pc(body, name=name, grid=(N_DEV // jb, nt),
               in_specs=[pl.BlockSpec((tm, k), lambda j, i: (i, 0)), pl.BlockSpec((tm, jb * bn), lambda j, i: (i, j))],
               out_specs=pl.BlockSpec((jb, k, bn), lambda j, i: (j, 0, 0)),
               out_shape=jax.ShapeDtypeStruct((N_DEV, k, bn), out_dtype),
               scratch=[] if out_dtype == f32 else [pltpu.VMEM((jb, k, bn), f32)],
               sem=("parallel", "arbitrary"))(x, dy)


def _mm_tn(x, dy, name, out_dtype=f32):
    t, k = x.shape
    n = dy.shape[1]
    tm = _tile(t, 2048 if k <= 1024 else 1024)
    cap = max(LANES, (2 * 1024 * 1024) // k)
    tn = n
    if n > cap:
        tn = max(c for c in range(LANES, cap + 1, LANES) if n % c == 0)
    nt = t // tm

    def body(x_ref, dy_ref, o_ref, *acc):
        acc_ref = acc[0] if acc else o_ref
        _acc_init(pl.program_id(1), acc_ref)
        acc_ref[...] += lax.dot_general(x_ref[...].astype(bf16), dy_ref[...].astype(bf16), TN, preferred_element_type=f32)
        if acc:
            @pl.when(pl.program_id(1) == nt - 1)
            def _():
                o_ref[...] = acc_ref[...].astype(out_dtype)

    return _pc(body, name=name, grid=(n // tn, nt),
               in_specs=[pl.BlockSpec((tm, k), lambda j, i: (i, 0)), pl.BlockSpec((tm, tn), lambda j, i: (i, j))],
               out_specs=pl.BlockSpec((k, tn), lambda j, i: (0, j)),
               out_shape=jax.ShapeDtypeStruct((k, n), out_dtype),
               scratch=[] if out_dtype == f32 else [pltpu.VMEM((k, tn), f32)],
               sem=("parallel", "arbitrary"))(x, dy)


def _gates_bwd(dgb_t, ab, a_log_pad, dt_pad, n_heads):
    t = ab.shape[0]
    tm = _tile(t, ROW_TILE)
    gate_rows = 2 * n_heads

    def body(dgb_ref, ab_ref, al_ref, dt_ref, dab_ref, dal_ref, ddt_ref):
        _acc_init(pl.program_id(0), dal_ref, ddt_ref)
        ab = ab_ref[...]
        dgb = jnp.concatenate([dgb_ref[...], jnp.zeros((LANES - gate_rows, tm), f32)], axis=0).T
        lane = lax.broadcasted_iota(jnp.int32, ab.shape, 1)
        is_a = lane < n_heads
        is_b = jnp.logical_and(lane >= n_heads, lane < 2 * n_heads)
        xa = ab + dt_ref[...]
        neg_a = -jnp.exp(al_ref[...])
        dg_da = neg_a * _sigmoid(xa)
        beta = _sigmoid(ab)
        da = jnp.where(is_a, dgb * dg_da, 0.0)
        dab_ref[...] = da + jnp.where(is_b, dgb * beta * (1.0 - beta), 0.0)
        _acc_rows(dal_ref, jnp.where(is_a, dgb * neg_a * _softplus(xa), 0.0))
        _acc_rows(ddt_ref, da)

    return _pc(body, name="gates_bwd", grid=(t // tm,),
               in_specs=[pl.BlockSpec((gate_rows, tm), lambda i: (0, i)), _rows(tm, LANES), _const((1, LANES)), _const((1, LANES))],
               out_specs=[_rows(tm, LANES), _const((SUBLANES, LANES)), _const((SUBLANES, LANES))],
               out_shape=[jax.ShapeDtypeStruct((t, LANES), f32), jax.ShapeDtypeStruct((SUBLANES, LANES), f32),
                          jax.ShapeDtypeStruct((SUBLANES, LANES), f32)],
               sem=("arbitrary",))(dgb_t, ab, a_log_pad, dt_pad)


def _glu_bwd(dgl, ub):
    t, d = dgl.shape
    tm = _tile(t, ROW_TILE)

    def body(dgl_ref, u_ref, du_ref, cs_ref):
        _acc_init(pl.program_id(0), cs_ref)
        dgl = dgl_ref[...]
        a = u_ref[:, :d].astype(f32)
        sb = _sigmoid(u_ref[:, d:].astype(f32))
        da = dgl * sb
        db = dgl * a * sb * (1.0 - sb)
        du_ref[:, :d] = da.astype(bf16)
        du_ref[:, d:] = db.astype(bf16)
        cs_ref[0:1, :d] += jnp.sum(da, axis=0, keepdims=True)
        cs_ref[0:1, d:] += jnp.sum(db, axis=0, keepdims=True)

    return _pc(body, name="glu_bwd", grid=(t // tm,),
               in_specs=[_rows(tm, d), _rows(tm, 2 * d)],
               out_specs=[_rows(tm, 2 * d), _const((SUBLANES, 2 * d))],
               out_shape=[jax.ShapeDtypeStruct((t, 2 * d), bf16), jax.ShapeDtypeStruct((SUBLANES, 2 * d), f32)],
               sem=("arbitrary",))(dgl, ub)


def _conv_rows(s):
    return 256 if s % 256 == 0 else s


def _conv_tap_sum(pad_ref, w_ref, base, rows, width):
    acc = jnp.zeros((rows, LANES), f32)
    for j in range(width):
        acc = acc + w_ref[j:j + 1, :] * pad_ref[pl.ds(base + CONV_PAD - (width - 1) + j, rows), :]
    return acc


def _qkv_kinds(j, n_heads, run):
    pl.when(j < n_heads)(lambda: run("q", float(LANES) ** -0.5))
    pl.when(jnp.logical_and(j >= n_heads, j < 2 * n_heads))(lambda: run("k", 1.0))
    pl.when(j >= 2 * n_heads)(lambda: run("v", 1.0))


def _l2_silu_post(c, kind, scale):
    a = c * _sigmoid(c)
    if kind == "v":
        return a, None, a
    r = lax.rsqrt(jnp.sum(a * a, axis=-1, keepdims=True) + L2_EPS)
    return a, r, a * (r * scale)


def _dwconv_fwd(x, w, b, name, qk_heads=None):
    bl, s, cn = x.shape
    width = w.shape[0]
    rows = _conv_rows(s)

    def body(x_ref, w_ref, b_ref, o_ref, pad_ref):
        pad_ref[0:CONV_PAD, :] = jnp.zeros((CONV_PAD, LANES), f32)
        pad_ref[CONV_PAD:, :] = x_ref[0, 0:rows, :]

        def run(kind, scale):
            def finish(acc, base):
                if kind is None:
                    acc = acc + b_ref[...]
                else:
                    _, _, acc = _l2_silu_post(acc, kind, scale)
                o_ref[0, pl.ds(base, rows), :] = acc

            finish(_conv_tap_sum(pad_ref, w_ref, 0, rows, width), 0)

            def step(i, carry):
                base = pl.multiple_of(i * rows, rows)
                acc = jnp.zeros((rows, LANES), f32)
                for j in range(width):
                    acc = acc + w_ref[j:j + 1, :] * x_ref[0, pl.ds(base - (width - 1) + j, rows), :]
                finish(acc, base)
                return carry

            lax.fori_loop(1, s // rows, step, 0)

        if qk_heads is None:
            run(None, 1.0)
        else:
            _qkv_kinds(pl.program_id(1), qk_heads, run)

    return _pc(body, name=name, grid=(bl, cn // LANES),
               in_specs=[pl.BlockSpec((1, s, LANES), lambda bi, j: (bi, 0, j)),
                         pl.BlockSpec((width, LANES), lambda bi, j: (0, j)),
                         pl.BlockSpec((1, LANES), lambda bi, j: (0, j))],
               out_specs=pl.BlockSpec((1, s, LANES), lambda bi, j: (bi, 0, j)),
               out_shape=jax.ShapeDtypeStruct((bl, s, cn), f32),
               scratch=[pltpu.VMEM((rows + CONV_PAD, LANES), f32)],
               sem=("parallel", "parallel"))(x, w, b)


def _dwconv_bwd(x, dys, w, name, qk_heads=None):
    bl, s, cn = x.shape
    width = w.shape[0]
    wp = -(-width // SUBLANES) * SUBLANES
    rows = _conv_rows(s)
    nblk = s // rows
    ndy = len(dys)

    def body(*refs):
        x_ref, w_ref = refs[0], refs[1]
        dy_refs = refs[2:2 + ndy]
        dx_ref, dw_ref, db_ref, xpad, dypad, acc = refs[2 + ndy:]
        j = pl.program_id(0)
        bi = pl.program_id(1)
        _acc_init(bi, acc, db_ref)
        xpad[0:CONV_PAD, :] = jnp.zeros((CONV_PAD, LANES), f32)
        xpad[CONV_PAD:, :] = x_ref[0]
        dypad[s:, :] = jnp.zeros((CONV_PAD, LANES), f32)
        if qk_heads is None:
            dypad[0:s, :] = dy_refs[0][0]
        else:
            def run(kind, scale):
                dy_ref = dy_refs["qkv".index(kind)]

                def pre(i, carry):
                    base = pl.multiple_of(i * rows, rows)
                    c = _conv_tap_sum(xpad, w_ref, base, rows, width)
                    a, r, _ = _l2_silu_post(c, kind, scale)
                    da = dy_ref[0, pl.ds(base, rows), :]
                    if kind != "v":
                        dy = da * scale
                        da = r * (dy - a * (r * r) * jnp.sum(a * dy, axis=-1, keepdims=True))
                    dypad[pl.ds(base, rows), :] = da * _silu_grad(c)
                    return carry

                lax.fori_loop(0, nblk, pre, 0)

            _qkv_kinds(j, qk_heads, run)

        def step(i, carry):
            base = pl.multiple_of(i * rows, rows)
            dxa = jnp.zeros((rows, LANES), f32)
            for jj in range(width):
                dxa = dxa + w_ref[jj:jj + 1, :] * dypad[pl.ds(base + (width - 1) - jj, rows), :]
            dx_ref[0, pl.ds(base, rows), :] = dxa
            dyc = dypad[pl.ds(base, rows), :]
            db_ref[...] += dyc.reshape(rows // SUBLANES, SUBLANES, LANES).sum(axis=0)
            for jj in range(width):
                prod = dyc * xpad[pl.ds(base + CONV_PAD - (width - 1) + jj, rows), :]
                acc[jj * SUBLANES:(jj + 1) * SUBLANES, :] += prod.reshape(rows // SUBLANES, SUBLANES, LANES).sum(axis=0)
            return carry

        lax.fori_loop(0, nblk, step, 0)

        @pl.when(bi == bl - 1)
        def _():
            dw_ref[...] = jnp.zeros((wp, LANES), f32)
            for jj in range(width):
                dw_ref[jj:jj + 1, :] = jnp.sum(acc[jj * SUBLANES:(jj + 1) * SUBLANES, :], axis=0, keepdims=True)

    if qk_heads is None:
        dy_specs = [pl.BlockSpec((1, s, LANES), lambda j, bi: (bi, 0, j))]
    else:
        hh = qk_heads
        def dy_spec(part):
            def index(j, bi):
                mine = jnp.logical_and(j >= part * hh, j < (part + 1) * hh)
                return (jnp.where(mine, bi * hh + j - part * hh, 0), 0, 0)
            return pl.BlockSpec((1, s, LANES), index)

        dy_specs = [dy_spec(0), dy_spec(1), dy_spec(2)]
    return _pc(body, name=name, grid=(cn // LANES, bl),
               in_specs=[pl.BlockSpec((1, s, LANES), lambda j, bi: (bi, 0, j)),
                         pl.BlockSpec((width, LANES), lambda j, bi: (0, j))] + dy_specs,
               out_specs=[pl.BlockSpec((1, s, LANES), lambda j, bi: (bi, 0, j)),
                          pl.BlockSpec((wp, LANES), lambda j, bi: (0, j)),
                          pl.BlockSpec((SUBLANES, LANES), lambda j, bi: (0, j))],
               out_shape=[jax.ShapeDtypeStruct((bl, s, cn), f32), jax.ShapeDtypeStruct((wp, cn), f32),
                          jax.ShapeDtypeStruct((SUBLANES, cn), f32)],
               scratch=[pltpu.VMEM((s + CONV_PAD, LANES), f32), pltpu.VMEM((s + CONV_PAD, LANES), f32),
                        pltpu.VMEM((width * SUBLANES, LANES), f32)],
               sem=("parallel", "arbitrary"))(x, w, *dys)


def _group_rows(s):
    for rows in (256, 128):
        if s % rows == 0:
            return rows
    return CHUNK


def _group_masks(rows):
    r = lax.broadcasted_iota(jnp.int32, (rows, rows), 0)
    c = lax.broadcasted_iota(jnp.int32, (rows, rows), 1)
    return r, c, r >= c


def _decay(gc_col, gc_row, causal):
    return jnp.exp(jnp.where(causal, gc_col - gc_row, NEG))


def _inv_unit_lower_many(mats, r, c):
    n = r.shape[0]
    eye = (r == c).astype(f32)
    same16 = (r >> 4) == (c >> 4)
    ads = [jnp.where(same16, a, 0.0) for a in mats]
    aos = [a - ad for a, ad in zip(mats, ads)]
    xs = ads
    tds = [-x for x in xs]
    for _ in range(3):
        xs = [_dot(x, x) for x in xs]
        tds = [td + x + _dot(td, x) for td, x in zip(tds, xs)]
    bs = [ao + _dot(td, ao) for td, ao in zip(tds, aos)]
    ps = [-b for b in bs]
    span = 2
    while span < n // 16:
        bs = [_dot(b, b) for b in bs]
        ps = [p + b + _dot(p, b) for p, b in zip(ps, bs)]
        span *= 2
    return [p + td + _dot(p, td) for p, td in zip(ps, tds)]


def _lane_cumsum(x, y, reverse, name):
    rr, n = x.shape

    def body(x_ref, y_ref, o_ref):
        i = lax.broadcasted_iota(jnp.int32, (n, n), 0)
        j = lax.broadcasted_iota(jnp.int32, (n, n), 1)
        tri = ((i >= j) if reverse else (i <= j)).astype(f32)
        o_ref[...] = jnp.dot(x_ref[...] + y_ref[...], tri, precision=HI, preferred_element_type=f32)

    spec = pl.BlockSpec((rr, n), lambda: (0, 0))
    return pl.pallas_call(body, name=name, in_specs=[spec, spec], out_specs=spec,
                          out_shape=jax.ShapeDtypeStruct((rr, n), f32))(x, y)


def _gdn_specs(n_heads, nblk, rows, hp=1, rev=False):
    def blk(n):
        return nblk - 1 - n if rev else n

    def qkv(off):
        return pl.BlockSpec((rows, hp * LANES), lambda g, n: (
            lax.div(g * hp, n_heads) * nblk + blk(n), lax.div(off * n_heads + lax.rem(g * hp, n_heads), hp)))

    def per_head(last, block_rows=rows):
        return pl.BlockSpec((hp, block_rows, last), lambda g, n: (g, blk(n), 0))

    def row_vec():
        return pl.BlockSpec((hp, 1, 1, rows), lambda g, n: (g, blk(n), 0, 0))

    return qkv, per_head, row_vec


def _gate_cumsum(gbeta, bl, s, n_heads):
    rows = _group_rows(s)
    nblk = s // rows

    def body(g_ref, cols_ref, gct_ref):
        i = lax.broadcasted_iota(jnp.int32, (rows, rows), 0)
        j = lax.broadcasted_iota(jnp.int32, (rows, rows), 1)
        blk = g_ref[...]
        summed = jnp.dot((i >= j).astype(f32), blk, precision=HI, preferred_element_type=f32)
        for h in range(n_heads):
            cols_ref[h, :, 0:1] = summed[:, h:h + 1]
            cols_ref[h, :, 1:2] = blk[:, n_heads + h:n_heads + h + 1]
        gct_ref[...] = summed.T[:n_heads, :]

    return _pc(body, name="gdn_gate_cumsum", grid=(bl, nblk),
               in_specs=[pl.BlockSpec((rows, LANES), lambda b, n: (b * nblk + n, 0))],
               out_specs=[pl.BlockSpec((n_heads, rows, 2), lambda b, n: (b, n, 0)),
                          pl.BlockSpec((None, n_heads, rows), lambda b, n: (b, 0, n))],
               out_shape=[jax.ShapeDtypeStruct((bl * n_heads, s, 2), f32), jax.ShapeDtypeStruct((bl, n_heads, s), f32)],
               sem=("parallel", "parallel"))(gbeta)


def _gate_cols_spec(nblk, rows, rev=False, n_heads=None):
    return pl.BlockSpec((n_heads, rows, 2), lambda g, n: (g, nblk - 1 - n if rev else n, 0))


def _gdn_prep(qkv, cols, grow, bl, s, n_heads):
    rows = _group_rows(s)
    nblk = s // rows
    bh_n = bl * n_heads
    hp = n_heads
    qkv_spec, ph, rv = _gdn_specs(n_heads, nblk, rows, hp)

    def body(k_ref, v_ref, cols_ref, grow_ref, u_ref, w_ref, t_ref):
        r, c, causal = _group_masks(rows)
        mats, rhs = [], []
        for h in range(hp):
            hs = slice(h * LANES, (h + 1) * LANES)
            k = k_ref[:, hs]
            gc = cols_ref[h, :, 0:1]
            beta = cols_ref[h, :, 1:2]
            kb = k * beta
            mats.append(jnp.where(r > c, _dot_nt(kb, k) * _decay(gc, grow_ref[h, 0], causal), 0.0))
            rhs.append((v_ref[:, hs] * beta, kb * jnp.exp(gc)))
        for h, tm in enumerate(_inv_unit_lower_many(mats, r, c)):
            tb = tm.astype(bf16)
            u_ref[h] = rhs[h][0] + jnp.dot(tb, rhs[h][0].astype(bf16), preferred_element_type=f32)
            w_ref[h] = (rhs[h][1] + jnp.dot(tb, rhs[h][1].astype(bf16), preferred_element_type=f32)).astype(bf16)
            t_ref[h] = tb

    return _pc(body, name="gdn_prep", grid=(bh_n // hp, nblk),
               in_specs=[qkv_spec(1), qkv_spec(2), _gate_cols_spec(nblk, rows, n_heads=hp), rv()],
               out_specs=[ph(LANES), ph(LANES), ph(rows)],
               out_shape=[jax.ShapeDtypeStruct((bh_n, s, LANES), f32), jax.ShapeDtypeStruct((bh_n, s, LANES), bf16),
                          jax.ShapeDtypeStruct((bh_n, s, rows), bf16)],
               sem=("parallel", "parallel"))(qkv, qkv, cols, grow)


def _gdn_scan(qkv, u, w, cols, grow, bl, s, n_heads):
    rows = _group_rows(s)
    nblk = s // rows
    bh_n = bl * n_heads
    d = n_heads * LANES
    hp = n_heads
    qkv_spec, ph, rv = _gdn_specs(n_heads, nblk, rows, hp)

    def body(q_ref, k_ref, u_ref, w_ref, cols_ref, grow_ref, o_ref, vn_ref, ss_ref, s_scr):
        _acc_init(pl.program_id(1), s_scr)
        _, _, causal = _group_masks(rows)
        hh = range(hp)
        qs = [q_ref[:, h * LANES:(h + 1) * LANES] for h in hh]
        ks = [k_ref[:, h * LANES:(h + 1) * LANES] for h in hh]
        gcs = [cols_ref[h, :, 0:1] for h in hh]
        ps = [_dot_nt(qs[h], ks[h]) * _decay(gcs[h], grow_ref[h, 0], causal) for h in hh]
        sts = [s_scr[h] for h in hh]
        for h in hh:
            ss_ref[h] = sts[h].astype(bf16)
        vns = [u_ref[h] - _dot(w_ref[h], sts[h]) for h in hh]
        for h in hh:
            vn_ref[h] = vns[h].astype(bf16)
        o_state = [_dot(qs[h] * jnp.exp(gcs[h]), sts[h]) for h in hh]
        o_intra = [_dot(ps[h], vns[h]) for h in hh]
        for h in hh:
            o_ref[:, h * LANES:(h + 1) * LANES] = o_state[h] + o_intra[h]
        for h in hh:
            g_last = gcs[h][rows - 1:rows, :]
            s_scr[h] = jnp.exp(g_last) * sts[h] + _dot_tn(ks[h] * jnp.exp(g_last - gcs[h]), vns[h])

    return _pc(body, name="gdn_scan", grid=(bh_n // hp, nblk),
               in_specs=[qkv_spec(0), qkv_spec(1), ph(LANES), ph(LANES), _gate_cols_spec(nblk, rows, n_heads=hp), rv()],
               out_specs=[qkv_spec(0), ph(LANES), ph(LANES, block_rows=LANES)],
               out_shape=[jax.ShapeDtypeStruct((bl * s, d), f32), jax.ShapeDtypeStruct((bh_n, s, LANES), bf16),
                          jax.ShapeDtypeStruct((bh_n, nblk * LANES, LANES), bf16)],
               scratch=[pltpu.VMEM((hp, LANES, LANES), f32)],
               sem=("parallel", "arbitrary"))(qkv, qkv, u, w, cols, grow)


def _gdn_scan_bwd(do, qkv, w, vn, cols, grow, ss, bl, s, n_heads):
    rows = _group_rows(s)
    nblk = s // rows
    bh_n = bl * n_heads
    hp = n_heads
    qkv_spec, ph, rv = _gdn_specs(n_heads, nblk, rows, hp, rev=True)

    def body(do_ref, q_ref, k_ref, w_ref, vn_ref, cols_ref, grow_ref, ss_ref,
             du_ref, dw_ref, dq_ref, dk_ref, dcol_ref, drow_ref, ds_scr):
        _acc_init(pl.program_id(1), ds_scr)
        _, _, causal = _group_masks(rows)
        last_row = lax.broadcasted_iota(jnp.int32, (rows, 1), 0) == rows - 1
        hh = range(hp)
        dos = [do_ref[:, h * LANES:(h + 1) * LANES] for h in hh]
        qs = [q_ref[:, h * LANES:(h + 1) * LANES] for h in hh]
        ks = [k_ref[:, h * LANES:(h + 1) * LANES] for h in hh]
        vns = [vn_ref[h] for h in hh]
        gcs = [cols_ref[h, :, 0:1] for h in hh]
        sts = [ss_ref[h] for h in hh]
        dss = [ds_scr[h] for h in hh]
        dmats = [_decay(gcs[h], grow_ref[h, 0], causal) for h in hh]
        gams = [jnp.exp(gc) for gc in gcs]
        qgs = [qs[h] * gams[h] for h in hh]
        g_lasts = [gc[rows - 1:rows, :] for gc in gcs]
        kd_scales = [jnp.exp(g_lasts[h] - gcs[h]) for h in hh]
        kdecs = [ks[h] * kd_scales[h] for h in hh]
        qks = [_dot_nt(qs[h], ks[h]) for h in hh]
        dpds = [_dot_nt(dos[h], vns[h]) * dmats[h] for h in hh]
        dvns = [_dot_tn(qks[h] * dmats[h], dos[h]) + _dot(kdecs[h], dss[h]) for h in hh]
        for h in hh:
            du_ref[h] = dvns[h]
        dkdecs = [_dot_nt(vns[h], dss[h]) for h in hh]
        for h in hh:
            dw_ref[h] = -_dot_nt(dvns[h], sts[h])
        dqgs = [_dot_nt(dos[h], sts[h]) for h in hh]
        dq_intra = [_dot(dpds[h], ks[h]) for h in hh]
        dk_intra = [_dot_tn(dpds[h], qs[h]) for h in hh]
        for h in hh:
            dq_ref[h] = dqgs[h] * gams[h] + dq_intra[h]
            dk_ref[h] = dk_intra[h] + dkdecs[h] * kd_scales[h]
            ep = dpds[h] * qks[h]
            drow_ref[h, 0] = -jnp.sum(ep, axis=0, keepdims=True)
            kd_rows = jnp.sum(dkdecs[h] * kdecs[h], axis=-1, keepdims=True)
            extra = jnp.sum(kd_rows) + jnp.exp(g_lasts[h]) * jnp.sum(sts[h] * dss[h])
            dcol_ref[h] = (jnp.sum(dqgs[h] * qgs[h], axis=-1, keepdims=True) + jnp.sum(ep, axis=-1, keepdims=True)
                           - kd_rows + jnp.where(last_row, extra, 0.0))
        ds_new = [jnp.exp(g_lasts[h]) * dss[h] + _dot_tn(qgs[h], dos[h]) - _dot_tn(w_ref[h], dvns[h]) for h in hh]
        for h in hh:
            ds_scr[h] = ds_new[h]

    return _pc(body, name="gdn_scan_bwd", grid=(bh_n // hp, nblk),
               in_specs=[qkv_spec(0), qkv_spec(0), qkv_spec(1), ph(LANES), ph(LANES), _gate_cols_spec(nblk, rows, rev=True, n_heads=hp), rv(),
                         ph(LANES, block_rows=LANES)],
               out_specs=[ph(LANES), ph(LANES), ph(LANES), ph(LANES), ph(1), rv()],
               out_shape=[jax.ShapeDtypeStruct((bh_n, s, LANES), f32)] * 4
               + [jax.ShapeDtypeStruct((bh_n, s, 1), f32), jax.ShapeDtypeStruct((bh_n, nblk, 1, rows), f32)],
               scratch=[pltpu.VMEM((hp, LANES, LANES), f32)],
               sem=("parallel", "arbitrary"))(do, qkv, qkv, w, vn, cols, grow, ss)


def _gdn_prep_bwd(qkv, cols, grow, tmat, du, dw, dk_scan, dcol_scan, drow_scan, bl, s, n_heads):
    rows = _group_rows(s)
    nblk = s // rows
    bh_n = bl * n_heads
    hp = n_heads
    qkv_spec, ph, rv = _gdn_specs(n_heads, nblk, rows, hp)

    def body(k_ref, v_ref, cols_ref, grow_ref, t_ref, du_ref, dw_ref, dks_ref, dcs_ref, drs_ref,
             dk_ref, dv_ref, dcols_ref, drow_ref):
        r, c, causal = _group_masks(rows)
        hh = range(hp)
        ks = [k_ref[:, h * LANES:(h + 1) * LANES] for h in hh]
        vs = [v_ref[:, h * LANES:(h + 1) * LANES] for h in hh]
        gcs = [cols_ref[h, :, 0:1] for h in hh]
        betas = [cols_ref[h, :, 1:2] for h in hh]
        tms = [t_ref[h] for h in hh]
        dus = [du_ref[h] for h in hh]
        dws = [dw_ref[h] for h in hh]
        gams = [jnp.exp(gc) for gc in gcs]
        kbs = [k * b for k, b in zip(ks, betas)]
        kbgs = [kb * g for kb, g in zip(kbs, gams)]
        dts = [_dot_nt(dus[h], vs[h] * betas[h]) + _dot_nt(dws[h], kbgs[h]) for h in hh]
        dvbs = [dus[h] + _dot_tn(tms[h], dus[h]) for h in hh]
        dkbgs = [dws[h] + _dot_tn(tms[h], dws[h]) for h in hh]
        kks = [_dot_nt(kbs[h], ks[h]) for h in hh]
        inner = [dts[h] + _dot_nt(dts[h], tms[h]) for h in hh]
        dads = [jnp.where(r > c, -(inner[h] + _dot_tn(tms[h], inner[h])), 0.0) * _decay(gcs[h], grow_ref[h, 0], causal)
                for h in hh]
        dkbs = [dkbgs[h] * gams[h] + _dot(dads[h], ks[h]) for h in hh]
        dk2 = [_dot_tn(dads[h], kbs[h]) for h in hh]
        for h in hh:
            dk_ref[h] = dks_ref[h] + dk2[h] + dkbs[h] * betas[h]
            dv_ref[h] = dvbs[h] * betas[h]
            ea = dads[h] * kks[h]
            dcols_ref[h, :, 0:1] = (dcs_ref[h] + jnp.sum(dkbgs[h] * kbgs[h], axis=-1, keepdims=True)
                                    + jnp.sum(ea, axis=-1, keepdims=True))
            dcols_ref[h, :, 1:2] = (jnp.sum(dvbs[h] * vs[h], axis=-1, keepdims=True)
                                    + jnp.sum(dkbs[h] * ks[h], axis=-1, keepdims=True))
            drow_ref[h, 0] = drs_ref[h, 0] - jnp.sum(ea, axis=0, keepdims=True)

    return _pc(body, name="gdn_prep_bwd", grid=(bh_n // hp, nblk),
               in_specs=[qkv_spec(1), qkv_spec(2), _gate_cols_spec(nblk, rows, n_heads=hp), rv(), ph(rows), ph(LANES), ph(LANES), ph(LANES), ph(1), rv()],
               out_specs=[ph(LANES), ph(LANES), ph(2), rv()],
               out_shape=[jax.ShapeDtypeStruct((bh_n, s, LANES), f32), jax.ShapeDtypeStruct((bh_n, s, LANES), f32),
                          jax.ShapeDtypeStruct((bh_n, s, 2), f32), jax.ShapeDtypeStruct((bh_n, nblk, 1, rows), f32)],
               sem=("parallel", "parallel"))(qkv, qkv, cols, grow, tmat, du, dw, dk_scan, dcol_scan, drow_scan)


def _row(v):
    return v.reshape(1, -1).astype(f32)


def _pad_lanes(v):
    v = v.reshape(1, -1).astype(f32)
    return jnp.pad(v, ((0, 0), (0, LANES - v.shape[1])))


def _local_step(x, tgt, p, need, emit):
    bl, s, d = x.shape
    t = bl * s
    n_heads = p["gdn_a_log"].shape[-1]
    assert d == n_heads * LANES and s % CHUNK == 0
    x2 = x.reshape(t, d)
    tgt2 = tgt.reshape(t, d)
    gr = {}

    n0, ub, gl = _pw1_glu(x2, _row(p["norm_mix_g"][0]), p["cv_w_pw1"], _row(p["cv_b_pw1"]))
    cv = need("conv", n0)
    dc = _dwconv_fwd(gl.reshape(bl, s, d), cv["cv_w_dw"], _row(p["cv_b_dw"]), "dwconv_fwd").reshape(t, d)
    sb, h1 = _ln_silu_mm_res(dc, _row(p["cv_ln_g"]), _row(p["cv_ln_b"]), cv["cv_w_pw2"], _row(p["cv_b_pw2"]), x2)
    m0 = need("mlp0", h1)
    n1, f0, r0, h2 = _mlp_fwd(h1, _row(p["norm_ffn_g"][0]), m0["w1"], m0["w2"], "mlp_fwd0")

    gd = need("gdn", h2)
    w_in = gd["w_in"]
    w_ab = jnp.pad(w_in[:, 4 * d:], ((0, 0), (0, LANES - 2 * n_heads)))
    a_log_pad = _pad_lanes(p["gdn_a_log"])
    dt_pad = _pad_lanes(p["gdn_dt_bias"])
    n2, qkv_pre, z, ab, gbeta = _gdn_in(h2, _row(p["norm_mix_g"][1]), w_in, w_ab, a_log_pad, dt_pad, n_heads)
    zero_bias = jnp.zeros((1, 3 * d), f32)
    qkv = _dwconv_fwd(qkv_pre.reshape(bl, s, 3 * d), cv["gdn_conv_w"], zero_bias, "sconv_fwd", qk_heads=n_heads).reshape(t, 3 * d)
    bh_n, rows = bl * n_heads, _group_rows(s)
    cols, gc_t = _gate_cumsum(gbeta, bl, s, n_heads)
    grow = gc_t.reshape(bh_n, s // rows, 1, rows)
    u, w, tmat = _gdn_prep(qkv, cols, grow, bl, s, n_heads)
    o, vn, ss = _gdn_scan(qkv, u, w, cols, grow, bl, s, n_heads)
    onb, h3 = _gated_norm_mm_res(o, z, _row(p["gdn_norm_g"]), gd["w_out"], h2, n_heads)
    m1 = need("mlp1", h3)
    n3, f1, r1, loss_acc, dh4, dgf = _mlp_fwd(h3, _row(p["norm_ffn_g"][1]), m1["w1"], m1["w2"], "mlp_fwd1_loss",
                                              loss=(_row(p["final_norm_g"]), tgt2))
    loss = loss_acc[0, 0]
    gr["loss"] = loss_acc[0, 0:1]
    gr["final_norm_g"] = dgf[0]

    df1, dh3, dg_ffn1, _ = _mlp_bwd(dh4, h3, _row(p["norm_ffn_g"][1]), f1, m1["w1"], m1["w2"], "mlp_bwd1")
    dw2_1 = _mm_tn(r1, dh4, "dw_mlp2_1", bf16)
    dw1_1 = _mm_tn_blocked(n3, df1, "dw_mlp1_1", bf16)

    dw_out = _mm_tn(onb, dh3, "dw_gdn_out", bf16)
    tie = emit("late", {"mlp_w2_1": dw2_1, "mlp_w1_1": dw1_1, "gdn_w_out": dw_out})
    do, dz, dng = _mm_nt([(dh3, gd["w_out"])], "dx_gdn_out",
                         _gated_norm_bwd_tail(o, z, _row(p["gdn_norm_g"]) + tie, n_heads))
    gr["gdn_norm_g"] = dng[0]
    du, dw_, dq, dk_scan, dcol_scan, drow_scan = _gdn_scan_bwd(do, qkv, w, vn, cols, grow, ss, bl, s, n_heads)
    dk, dv, dcols, drow = _gdn_prep_bwd(qkv, cols, grow, tmat, du, dw_, dk_scan, dcol_scan, drow_scan, bl, s, n_heads)
    lanes_shape = (bh_n * (s // rows), rows)
    dg_lanes = _lane_cumsum(dcols[..., 0].reshape(lanes_shape), drow.reshape(lanes_shape), True, "gdn_gate_cumsum_bwd")
    dqkv_pre, dconv_w, _ = _dwconv_bwd(qkv_pre.reshape(bl, s, 3 * d), [dq, dk, dv], cv["gdn_conv_w"], "sconv_bwd", qk_heads=n_heads)
    gr["gdn_conv_w"] = dconv_w[:cv["gdn_conv_w"].shape[0]]
    dgb_t = jnp.stack([dg_lanes.reshape(bl, n_heads, s), dcols[..., 1].reshape(bl, n_heads, s)])
    dgb_t = dgb_t.transpose(0, 2, 1, 3).reshape(2 * n_heads, t)
    dab, dal, ddt = _gates_bwd(dgb_t, ab, a_log_pad, dt_pad, n_heads)
    gr["gdn_a_log"] = dal[0, :n_heads]
    gr["gdn_dt_bias"] = ddt[0, :n_heads]
    dqkv2 = dqkv_pre.reshape(t, 3 * d)
    dw_in = jnp.concatenate(
        [_mm_tn(n2, dqkv2, "dw_gdn_in_qkv"), _mm_tn(n2, dz, "dw_gdn_in_z"), _mm_tn(n2, dab, "dw_gdn_in_ab")[:, :2 * n_heads]], axis=1)
    tie = emit("gdn_in", {"gdn_w_in": dw_in})
    dh2, dg_mix1 = _mm_nt([(dqkv2, (w_in, 0)), (dz, (w_in, 3)), (dab, w_ab)], "dx_gdn_in",
                          _rms_bwd_tail(h2, _row(p["norm_mix_g"][1]) + tie, dh3))

    df0, dh1, dg_ffn0, cs_h1 = _mlp_bwd(dh2, h1, _row(p["norm_ffn_g"][0]), f0, m0["w1"], m0["w2"], "mlp_bwd0")
    dw2_0 = _mm_tn(r0, dh2, "dw_mlp2_0", bf16)
    dw1_0 = _mm_tn_blocked(n1, df0, "dw_mlp1_0", bf16)
    dw_pw2 = _mm_tn(sb, dh1, "dw_pw2", bf16)
    tie = emit("mlp0", {"mlp_w2_0": dw2_0, "mlp_w1_0": dw1_0, "cv_w_pw2": dw_pw2})
    gr["norm_ffn_g"] = jnp.stack([dg_ffn0[0], dg_ffn1[0]])

    gr["cv_b_pw2"] = cs_h1[0]
    ddc, dlng, dlnb, cs_dc = _mm_nt([(dh1, cv["cv_w_pw2"])], "dx_pw2",
                                    _ln_silu_bwd_tail(dc, _row(p["cv_ln_g"]) + tie, _row(p["cv_ln_b"])))
    gr["cv_ln_g"] = dlng[0]
    gr["cv_ln_b"] = dlnb[0]
    gr["cv_b_dw"] = cs_dc[0]
    dgl, dw_dw, _ = _dwconv_bwd(gl.reshape(bl, s, d), [ddc.reshape(bl, s, d)], cv["cv_w_dw"], "dwconv_bwd")
    gr["cv_w_dw"] = dw_dw[:cv["cv_w_dw"].shape[0]]
    dub, cs_u = _glu_bwd(dgl.reshape(t, d), ub)
    gr["cv_b_pw1"] = cs_u[0]
    dw_pw1 = _mm_tn_blocked(n0, dub, "dw_pw1", bf16)
    tie = emit("last", {"cv_w_pw1": dw_pw1, "small": gr})
    dx, dg_mix0 = _mm_nt([(dub, ("blocks", p["cv_w_pw1"]))], "dx_pw1",
                         _rms_bwd_tail(x2, _row(p["norm_mix_g"][0]) + tie, dh1))
    return loss, dx.reshape(bl, s, d), {"norm_mix_g": jnp.stack([dg_mix0[0], dg_mix1[0]])}


ANY = pl.BlockSpec(memory_space=pl.ANY)
MESH = pl.DeviceIdType.MESH


def _flip(v, bit):
    return 1 - v if bit else v


def _all_gather_many(shards):
    na = len(shards)

    def body(*refs):
        x_refs, o_refs = refs[:na], refs[na:2 * na]
        send_sems, recv_sems, local_sems = refs[2 * na:]
        x, y, c = lax.axis_index("x"), lax.axis_index("y"), lax.axis_index("c")
        me, sibling = (x, y, c), (x, y, 1 - c)
        chips = [(1 - x, y), (x, 1 - y), (1 - x, 1 - y)]

        def copy(a, k, block, to, src=None):
            px, py, pc = block
            dst = o_refs[a].at[4 * px + 2 * py + pc]
            return pltpu.make_async_remote_copy(
                src_ref=dst if src is None else src, dst_ref=dst,
                send_sem=send_sems.at[a, k], recv_sem=recv_sems.at[a, k], device_id=to, device_id_type=MESH)

        mine = [pltpu.make_async_copy(x_refs[a], o_refs[a].at[4 * x + 2 * y + c], local_sems.at[a]) for a in range(na)]
        first = []
        for a in range(na):
            first.append(copy(a, 0, me, sibling, src=x_refs[a]))
            first += [copy(a, 1 + j, me, (*chip, c), src=x_refs[a]) for j, chip in enumerate(chips)]
        for cp in mine + first:
            cp.start()
        passed = []
        for j, chip in enumerate(chips):
            for a in range(na):
                copy(a, 1 + j, (*chip, c), me).wait_recv()
                fwd = copy(a, 4 + j, (*chip, c), sibling)
                fwd.start()
                passed.append(fwd)
        for a in range(na):
            copy(a, 0, sibling, me).wait_recv()
        for j, chip in enumerate(chips):
            for a in range(na):
                copy(a, 4 + j, (*chip, 1 - c), me).wait_recv()
        for cp in first + passed:
            cp.wait_send()
        for cp in mine:
            cp.wait()

    return pl.pallas_call(
        body, name="weights_all_gather",
        out_shape=[jax.ShapeDtypeStruct((N_DEV,) + a.shape, a.dtype) for a in shards],
        in_specs=[ANY] * na, out_specs=[ANY] * na,
        scratch_shapes=[pltpu.SemaphoreType.DMA((na, 7)), pltpu.SemaphoreType.DMA((na, 7)), pltpu.SemaphoreType.DMA((na,))],
        compiler_params=pltpu.CompilerParams(has_side_effects=True),
    )(*shards)


HBM = pl.BlockSpec(memory_space=pltpu.HBM)
SEM = pl.BlockSpec(memory_space=pltpu.SEMAPHORE)
EFFECT = pltpu.SideEffectType.DATAFLOW_SIDE_EFFECTING
N_PEERS = N_DEV - 1


def _exchange_copies(src_refs, land_refs, send_sems, recv_sems, scatter):
    x, y, c = lax.axis_index("x"), lax.axis_index("y"), lax.axis_index("c")
    me = 4 * x + 2 * y + c
    copies = []
    for a, (src, land) in enumerate(zip(src_refs, land_refs)):
        for k in range(1, N_DEV):
            px, py, pc = _flip(x, k & 4), _flip(y, k & 2), _flip(c, k & 1)
            i = a * N_PEERS + k - 1
            copies.append(pltpu.make_async_remote_copy(
                src_ref=src.at[4 * px + 2 * py + pc] if scatter[a] else src, dst_ref=land.at[me],
                send_sem=send_sems.at[i], recv_sem=recv_sems.at[i], device_id=(px, py, pc), device_id_type=MESH))
    return copies


def _exchange_start(srcs, scatter, name):
    na = len(srcs)
    lands = [lax.empty(s.shape if sc else (N_DEV,) + s.shape, s.dtype) for s, sc in zip(srcs, scatter)]

    def body(*refs):
        copies = _exchange_copies(refs[:na], refs[na:2 * na], refs[2 * na], refs[2 * na + 1], scatter)
        for cp in copies:
            cp.start()
        token = refs[-1]
        token[...] = jnp.zeros_like(token)

    outs = pl.pallas_call(
        body, name=name,
        out_shape=(pltpu.SemaphoreType.DMA((na * N_PEERS,)), pltpu.SemaphoreType.DMA((na * N_PEERS,)))
        + tuple(pltpu.HBM(a.shape, a.dtype) for a in srcs + lands) + (jax.ShapeDtypeStruct((SUBLANES, LANES), f32),),
        in_specs=[HBM] * (2 * na),
        out_specs=(SEM, SEM) + (HBM,) * (2 * na) + (pl.BlockSpec(memory_space=pltpu.VMEM),),
        input_output_aliases={i: 2 + i for i in range(2 * na)},
        compiler_params=pltpu.CompilerParams(has_side_effects=EFFECT),
    )(*[pltpu.with_memory_space_constraint(a, pltpu.HBM) for a in srcs + lands])
    return outs[0], outs[1], list(outs[2:2 + na]), list(outs[2 + na:2 + 2 * na]), outs[-1]


def _exchange_wait(started, after, scatter, name):
    send_sems, recv_sems, srcs, lands, _ = started
    na = len(srcs)

    def body(*refs):
        for cp in _exchange_copies(refs[:na], refs[na:2 * na], refs[2 * na], refs[2 * na + 1], scatter):
            cp.wait_send()
            cp.wait_recv()

    outs = pl.pallas_call(
        body, name=name,
        out_shape=tuple(pltpu.HBM(a.shape, a.dtype) for a in srcs + lands),
        in_specs=[HBM] * (2 * na) + [SEM, SEM, ANY], out_specs=(HBM,) * (2 * na),
        input_output_aliases={i: i for i in range(2 * na)},
        compiler_params=pltpu.CompilerParams(has_side_effects=EFFECT),
    )(*srcs, *lands, send_sems, recv_sems, after)
    return list(outs[:na]), list(outs[na:])


def _own_block(land, block, me):
    return lax.dynamic_update_index_in_dim(land, block, me, 0)


def _sum8_adamw(r2, w, m, v, layer, name):
    _, rr, cc = r2.shape
    tr = _tile(rr, 256)
    bc1 = 1.0 - ADAM_B1 ** ADAM_STEP
    bc2 = 1.0 - ADAM_B2 ** ADAM_STEP

    def body(r_ref, w_ref, m_ref, v_ref, g_ref, d_ref, nm_ref, nv_ref):
        gv = r_ref[0].astype(f32)
        for q in range(1, N_DEV):
            gv = gv + r_ref[q].astype(f32)
        g_ref[...] = gv
        nm = ADAM_B1 * m_ref[...] + (1.0 - ADAM_B1) * gv
        nv = ADAM_B2 * v_ref[...] + (1.0 - ADAM_B2) * (gv * gv)
        nm_ref[...] = nm
        nv_ref[...] = nv
        d_ref[...] = -ADAM_LR * ((nm / bc1) / (jnp.sqrt(nv / bc2) + ADAM_EPS) + ADAM_WD * w_ref[...])

    lspec = pl.BlockSpec((None, tr, cc), lambda i: (layer, i, 0))
    return _pc(body, name=name, grid=(rr // tr,),
               in_specs=[pl.BlockSpec((N_DEV, tr, cc), lambda i: (0, i, 0)), lspec, lspec, lspec],
               out_specs=[_rows(tr, cc)] * 4, out_shape=[jax.ShapeDtypeStruct((rr, cc), f32)] * 4,
               sem=("parallel",))(r2, w, m, v)


def _sum_devices(recv, name):
    _, rr, _ = recv.shape
    tr = _tile(rr, 512)

    def body(r_ref, o_ref):
        acc = r_ref[0]
        for i in range(1, N_DEV):
            acc = acc + r_ref[i]
        o_ref[...] = acc

    return _pc(body, name=name, grid=(rr // tr,),
               in_specs=[pl.BlockSpec((N_DEV, tr, LANES), lambda i: (0, i, 0))],
               out_specs=_rows(tr, LANES), out_shape=jax.ShapeDtypeStruct((rr, LANES), f32), sem=("parallel",))(recv)


def _adamw(w, g, m, v):
    rr = w.shape[0]
    tr = _tile(rr, 512)
    bc1 = 1.0 - ADAM_B1 ** ADAM_STEP
    bc2 = 1.0 - ADAM_B2 ** ADAM_STEP

    def body(w_ref, g_ref, m_ref, v_ref, d_ref, nm_ref, nv_ref):
        gv = g_ref[...]
        nm = ADAM_B1 * m_ref[...] + (1.0 - ADAM_B1) * gv
        nv = ADAM_B2 * v_ref[...] + (1.0 - ADAM_B2) * (gv * gv)
        nm_ref[...] = nm
        nv_ref[...] = nv
        d_ref[...] = -ADAM_LR * ((nm / bc1) / (jnp.sqrt(nv / bc2) + ADAM_EPS) + ADAM_WD * w_ref[...])

    spec = _rows(tr, LANES)
    return _pc(body, name="adamw", grid=(rr // tr,), in_specs=[spec] * 4, out_specs=[spec] * 3,
               out_shape=[jax.ShapeDtypeStruct((rr, LANES), f32)] * 3, sem=("parallel",))(w, g, m, v)


PACK_ROWS = 512
PART_ROWS = SUBLANES


def _pack(arrs):
    parts, sizes = [], []
    for a in arrs:
        flat = a.reshape(-1)
        n = flat.shape[0]
        rows = -(-n // (LANES * PART_ROWS)) * PART_ROWS
        if rows * LANES != n:
            flat = jnp.pad(flat, (0, rows * LANES - n))
        parts.append(flat.reshape(rows, LANES))
        sizes.append((rows, n))
    total = sum(r for r, _ in sizes)
    padded = -(-total // PACK_ROWS) * PACK_ROWS
    if padded > total:
        parts.append(jnp.zeros((padded - total, LANES), parts[0].dtype))
    return jnp.concatenate(parts, axis=0), sizes


def _unpack(packed, sizes, shapes):
    out, off = [], 0
    for (rows, n), shp in zip(sizes, shapes):
        piece = lax.slice_in_dim(packed, off, off + rows, axis=0)
        if rows * LANES != n:
            piece = lax.slice_in_dim(piece.reshape(-1), 0, n, axis=0)
        out.append(piece.reshape(tuple(shp)))
        off += rows
    return out


def _cols_to_blocks(a):
    n = a.shape[-1] // N_DEV
    a = a.reshape(a.shape[:-1] + (N_DEV, n))
    return jnp.moveaxis(a, -2, 0)


def _blocks_to_cols(a):
    a = jnp.moveaxis(a, 0, -2)
    return a.reshape(a.shape[:-2] + (a.shape[-2] * a.shape[-1],))


def _rows_to_blocks(a):
    k = a.shape[-2] // N_DEV
    a = a.reshape(a.shape[:-2] + (N_DEV, k, a.shape[-1]))
    return jnp.moveaxis(a, -3, 0)


def _blocks_to_rows(a):
    a = jnp.moveaxis(a, 0, -3)
    return a.reshape(a.shape[:-3] + (a.shape[-3] * a.shape[-2], a.shape[-1]))


COL_SHARDED = ("cv_w_pw1", "gdn_w_in", "mlp_w1")
ROW_SHARDED = ("cv_w_pw2", "gdn_w_out", "mlp_w2")
CONV_SHARDED = ("cv_w_dw", "gdn_conv_w")
REPLICATED = ("norm_mix_g", "norm_ffn_g", "final_norm_g", "cv_b_pw1", "cv_b_dw", "cv_ln_g", "cv_ln_b", "cv_b_pw2",
              "gdn_a_log", "gdn_dt_bias", "gdn_norm_g")
WEIGHTS = ("norm_mix_g", "norm_ffn_g", "final_norm_g", "cv_w_pw1", "cv_b_pw1", "cv_w_dw", "cv_b_dw", "cv_ln_g",
           "cv_ln_b", "cv_w_pw2", "cv_b_pw2", "gdn_w_in", "gdn_conv_w", "gdn_a_log", "gdn_dt_bias", "gdn_norm_g",
           "gdn_w_out", "mlp_w1", "mlp_w2")
MATMUL_SHARDED = COL_SHARDED + ROW_SHARDED


def _squeeze_layer(name, a):
    if name in ("norm_mix_g", "norm_ffn_g", "final_norm_g", "mlp_w1", "mlp_w2"):
        return a
    return a[0]


def _gather_weights(shards):
    me = 4 * lax.axis_index("x") + 2 * lax.axis_index("y") + lax.axis_index("c")
    pw1 = _all_gather_many([shards["cv_w_pw1"].astype(bf16)])[0]
    now = {"cv_w_pw1": pw1}

    def cast(a, tie):
        return (a + tie).astype(bf16)

    token, _ = lax.optimization_barrier((jnp.zeros((), f32), pw1))
    later, started = {}, {}
    for group in ("conv", "mlp0", "gdn", "mlp1"):
        if group == "conv":
            srcs = [cast(shards["cv_w_pw2"], token)] + [shards[n] + token for n in CONV_SHARDED]
        elif group == "gdn":
            srcs = [cast(shards["gdn_w_in"], token).reshape(-1, LANES), cast(shards["gdn_w_out"], token)]
        else:
            layer = int(group[-1])
            srcs = [cast(shards["mlp_w1"][layer], token), cast(shards["mlp_w2"][layer], token)]
        later[group] = srcs
        started[group] = _exchange_start(srcs, [False] * len(srcs), f"weights_{group}_start")
        token = started[group][4][0, 0]

    def need(group, after):
        srcs, lands = _exchange_wait(started[group], after, [False] * len(later[group]), f"weights_{group}_wait")
        lands = [_own_block(ld, own, me) for ld, own in zip(lands, srcs)]
        if group == "conv":
            out = {"cv_w_pw2": _blocks_to_rows(lands[0])}
            out.update({n: _blocks_to_cols(ld) for n, ld in zip(CONV_SHARDED, lands[1:])})
            return out
        if group == "gdn":
            w_in = _blocks_to_cols(lands[0].reshape((N_DEV,) + shards["gdn_w_in"].shape))
            return {"w_in": w_in, "w_out": _blocks_to_rows(lands[1])}
        return {"w1": lands[0], "w2": lands[1]}

    return now, need, token


def kernel(x, norm_mix_g, norm_ffn_g, final_norm_g, cv_w_pw1, cv_b_pw1, cv_w_dw, cv_b_dw, cv_ln_g, cv_ln_b, cv_w_pw2, cv_b_pw2, gdn_w_in, gdn_conv_w, gdn_a_log, gdn_dt_bias, gdn_norm_g, gdn_w_out, mlp_w1, mlp_w2, loss_target, m_norm_mix_g, m_norm_ffn_g, m_final_norm_g, m_cv_w_pw1, m_cv_b_pw1, m_cv_w_dw, m_cv_b_dw, m_cv_ln_g, m_cv_ln_b, m_cv_w_pw2, m_cv_b_pw2, m_gdn_w_in, m_gdn_conv_w, m_gdn_a_log, m_gdn_dt_bias, m_gdn_norm_g, m_gdn_w_out, m_mlp_w1, m_mlp_w2, v_norm_mix_g, v_norm_ffn_g, v_final_norm_g, v_cv_w_pw1, v_cv_b_pw1, v_cv_w_dw, v_cv_b_dw, v_cv_ln_g, v_cv_ln_b, v_cv_w_pw2, v_cv_b_pw2, v_gdn_w_in, v_gdn_conv_w, v_gdn_a_log, v_gdn_dt_bias, v_gdn_norm_g, v_gdn_w_out, v_mlp_w1, v_mlp_w2):
    w_in = dict(zip(WEIGHTS, (norm_mix_g, norm_ffn_g, final_norm_g, cv_w_pw1, cv_b_pw1, cv_w_dw, cv_b_dw, cv_ln_g, cv_ln_b, cv_w_pw2, cv_b_pw2, gdn_w_in, gdn_conv_w, gdn_a_log, gdn_dt_bias, gdn_norm_g, gdn_w_out, mlp_w1, mlp_w2)))
    m_in = dict(zip(WEIGHTS, (m_norm_mix_g, m_norm_ffn_g, m_final_norm_g, m_cv_w_pw1, m_cv_b_pw1, m_cv_w_dw, m_cv_b_dw, m_cv_ln_g, m_cv_ln_b, m_cv_w_pw2, m_cv_b_pw2, m_gdn_w_in, m_gdn_conv_w, m_gdn_a_log, m_gdn_dt_bias, m_gdn_norm_g, m_gdn_w_out, m_mlp_w1, m_mlp_w2)))
    v_in = dict(zip(WEIGHTS, (v_norm_mix_g, v_norm_ffn_g, v_final_norm_g, v_cv_w_pw1, v_cv_b_pw1, v_cv_w_dw, v_cv_b_dw, v_cv_ln_g, v_cv_ln_b, v_cv_w_pw2, v_cv_b_pw2, v_gdn_w_in, v_gdn_conv_w, v_gdn_a_log, v_gdn_dt_bias, v_gdn_norm_g, v_gdn_w_out, v_mlp_w1, v_mlp_w2)))
    me = 4 * lax.axis_index("x") + 2 * lax.axis_index("y") + lax.axis_index("c")

    shards = {n: _squeeze_layer(n, w_in[n]) for n in WEIGHTS}
    first, need, token = _gather_weights(shards)
    params = {n: shards[n] for n in REPLICATED}
    params.update(first)
    params["norm_mix_g"] = params["norm_mix_g"] + token

    def row_blocks(a):
        return a.reshape(N_DEV, a.shape[0] // N_DEV, a.shape[1])

    def flat_blocks(a):
        k, n8 = a.shape
        return _cols_to_blocks(a).reshape(N_DEV, k * (n8 // N_DEV) // LANES, LANES)

    def as_blocks(n, a):
        if n == "gdn_w_in":
            return flat_blocks(a).astype(bf16)
        return a if a.ndim == 3 else row_blocks(a)

    sent = []

    small = REPLICATED + CONV_SHARDED
    small_early = tuple(n for n in small if n != "norm_mix_g") + ("loss",)
    small_info = {}

    def emit(group, grads_out):
        names, blocks, scatter = list(grads_out), [], []
        for n in names:
            if n == "small":
                packed, small_info["sizes"] = _pack([grads_out[n][k] for k in small_early])
                small_info["shapes"] = [grads_out[n][k].shape for k in small_early]
                blocks.append(packed)
                scatter.append(False)
            else:
                blocks.append(as_blocks(n, grads_out[n]))
                scatter.append(True)
        sent.append((names, _exchange_start(blocks, scatter, f"grads_{group}_start"), scatter))
        return sent[-1][1][4][0, 0]

    _, grad_x, gr = _local_step(x, loss_target, params, need, emit)
    nm_shape = gr["norm_mix_g"].shape
    sent.append((["norm_mix_g"], _exchange_start([gr["norm_mix_g"].reshape(-1, LANES)], [False], "grads_norm_mix_start"), [False]))

    recv = {}

    def finish(entry, after):
        names, st, scatter = entry
        srcs, lands = _exchange_wait(st, after, scatter, f"grads_{names[0]}_wait")
        for n, ld, blk, sc in zip(names, lands, srcs, scatter):
            own = lax.dynamic_index_in_dim(blk, me, 0, keepdims=False) if sc else blk
            recv[n] = _own_block(ld, own, me)
        return lands[0]

    def as3d(n, a):
        if n == "gdn_w_in":
            return a.reshape(a.shape[0], -1, LANES)
        return a

    big = [("cv_w_pw2", 0, "cv_w_pw2"), ("gdn_w_in", 0, "gdn_w_in"), ("gdn_w_out", 0, "gdn_w_out"),
           ("mlp_w1", 0, "mlp_w1_0"), ("mlp_w1", 1, "mlp_w1_1"), ("mlp_w2", 0, "mlp_w2_0"), ("mlp_w2", 1, "mlp_w2_1"),
           ("cv_w_pw1", 0, "cv_w_pw1")]
    res = {n: {} for n in MATMUL_SHARDED}
    after = grad_x
    for entry in sent[:-2]:
        after = finish(entry, after)
    for n, layer, key in big[:-1]:
        res[n][layer] = _sum8_adamw(recv[key], as3d(n, w_in[n]), as3d(n, m_in[n]), as3d(n, v_in[n]), layer, f"adamw_{key}")
        after = res[n][layer][0]
    after = finish(sent[-2], after)
    finish(sent[-1], after)
    n, layer, key = big[-1]
    res[n][layer] = _sum8_adamw(recv[key], as3d(n, w_in[n]), as3d(n, m_in[n]), as3d(n, v_in[n]), layer, f"adamw_{key}")

    grads = dict(zip(small_early, _unpack(_sum_devices(recv["small"], "grads_small_sum"), small_info["sizes"], small_info["shapes"])))
    grads["norm_mix_g"] = _sum_devices(recv["norm_mix_g"], "grads_norm_mix_sum").reshape(nm_shape)
    loss = grads["loss"].reshape(())
    for n in CONV_SHARDED:
        cn = shards[n].shape[-1]
        grads[n] = lax.dynamic_slice_in_dim(grads[n], me * cn, cn, axis=1)

    out_groups = {n: [] for n in WEIGHTS}
    for n in MATMUL_SHARDED:
        layers = sorted(res[n])
        for k in range(4):
            pieces = [res[n][layer][k] for layer in layers]
            out_groups[n].append(jnp.stack(pieces).reshape(w_in[n].shape))

    sm_w = [shards[n] for n in small]
    sm_g = [grads[n].reshape(shards[n].shape) for n in small]
    sm_m = [_squeeze_layer(n, m_in[n]) for n in small]
    sm_v = [_squeeze_layer(n, v_in[n]) for n in small]
    wp, psz = _pack(sm_w)
    gp, _ = _pack(sm_g)
    mp, _ = _pack(sm_m)
    vp, _ = _pack(sm_v)
    dp, nmp, nvp = _adamw(wp, gp, mp, vp)
    shp = [a.shape for a in sm_w]
    for n, g, dl, nm, nv in zip(small, sm_g, _unpack(dp, psz, shp), _unpack(nmp, psz, shp), _unpack(nvp, psz, shp)):
        out_groups[n] = [a.reshape(w_in[n].shape) for a in (g, dl, nm, nv)]

    outs = [loss, grad_x]
    for k in range(4):
        outs += [out_groups[n][k] for n in WEIGHTS]
    return tuple(outs)
```
